```python
import jax, jax.numpy as jnp
from jax import lax
import numpy as np

D_MODEL = 1024
BATCH = 2
SEQ = 8192
DEPTH = 1
DEC_BATCH = 4
DEC_SEQ = 4096
PAST_LEN = 128

N_META = 16
MLA_HEADS = 4
MLA_Q_RANK = 384
MLA_KV_RANK = 256
MLA_NOPE = 128
MLA_ROPE = 64
MLA_V = 128
ROPE_THETA = 10000.0
Q_BLOCK = 128
GLA_HEADS = 4
GLA_DK = 64
GLA_DV = 128
GLA_GATE_RANK = 16
GLA_TAU = 16.0
GLA_CHUNK = 64
GLA_PAD = (-N_META) % GLA_CHUNK
MIX_WIDTH = MLA_HEADS * MLA_V + GLA_HEADS * GLA_DV
N_GROUPS = 4
EXPERTS_PER_GROUP = 8
TOP_K = 2
D_EXPERT = 256
EPS = 1e-6

IN_SPLIT_SIZES = (MLA_Q_RANK, MLA_KV_RANK, MLA_ROPE,
                  GLA_HEADS * GLA_DK, GLA_HEADS * GLA_DK, GLA_HEADS * GLA_DV,
                  GLA_GATE_RANK, GLA_GATE_RANK, GLA_HEADS * GLA_DV)
D_IN = int(sum(IN_SPLIT_SIZES))
IN_SPLIT_POINTS = tuple(int(v) for v in np.cumsum(IN_SPLIT_SIZES)[:-1])

kernel_name = 'hymba_mla_gla_hier_moe_encoder'


def rmsnorm(x, g):
    xf = x.astype(jnp.float32)
    y = xf * lax.rsqrt(jnp.mean(xf * xf, axis=-1, keepdims=True) + EPS)
    return (y * g.astype(jnp.float32)).astype(x.dtype)


def rope(x, cos, sin):
    half = x.shape[-1] // 2
    xf = x.astype(jnp.float32)
    x1, x2 = xf[..., :half], xf[..., half:]
    return jnp.concatenate([x1 * cos - x2 * sin, x1 * sin + x2 * cos], axis=-1).astype(x.dtype)


def mla(c_q, c_kv, k_pe, q_a_norm, w_uq, kv_a_norm, w_ukv, cos, sin):
    bsz, seq_len, _ = c_q.shape
    q = (rmsnorm(c_q, q_a_norm) @ w_uq).reshape(bsz, seq_len, MLA_HEADS, MLA_NOPE + MLA_ROPE)
    q_nope = q[..., :MLA_NOPE]
    q_pe = rope(q[..., MLA_NOPE:], cos[:, None, :], sin[:, None, :])
    kv = (rmsnorm(c_kv, kv_a_norm) @ w_ukv).reshape(bsz, seq_len, MLA_HEADS, MLA_NOPE + MLA_V)
    k_nope, v = kv[..., :MLA_NOPE], kv[..., MLA_NOPE:]
    k_pe = rope(k_pe, cos, sin)
    n_blocks = -(-seq_len // Q_BLOCK)
    pad = n_blocks * Q_BLOCK - seq_len
    qn = jnp.pad(q_nope, ((0, 0), (0, pad), (0, 0), (0, 0))).reshape(bsz, n_blocks, Q_BLOCK, MLA_HEADS, MLA_NOPE).swapaxes(0, 1)
    qp = jnp.pad(q_pe, ((0, 0), (0, pad), (0, 0), (0, 0))).reshape(bsz, n_blocks, Q_BLOCK, MLA_HEADS, MLA_ROPE).swapaxes(0, 1)
    scale = (MLA_NOPE + MLA_ROPE) ** -0.5

    def attend(blk):
        qn_b, qp_b = blk
        s = jnp.einsum('bqhd,bkhd->bhqk', qn_b, k_nope) + jnp.einsum('bqhd,bkd->bhqk', qp_b, k_pe)
        p = jax.nn.softmax(s.astype(jnp.float32) * scale, axis=-1).astype(v.dtype)
        return jnp.einsum('bhqk,bkhd->bqhd', p, v)

    o = lax.map(attend, (qn, qp))
    o = o.swapaxes(0, 1).reshape(bsz, n_blocks * Q_BLOCK, MLA_HEADS * MLA_V)
    return o[:, :seq_len]


def chunked_gla(q, k, v, g):
    bsz, nh, t_len, dk = q.shape
    dv = v.shape[-1]
    n_chunks = t_len // GLA_CHUNK
    shp = (bsz, nh, n_chunks, GLA_CHUNK)
    q = q.reshape(shp + (dk,))
    k = k.reshape(shp + (dk,))
    v = v.reshape(shp + (dv,))
    g = g.reshape(shp + (dk,))
    b = jnp.cumsum(g, axis=3)
    b_last = b[:, :, :, -1:, :]
    b_mid = b[:, :, :, GLA_CHUNK // 2 - 1:GLA_CHUNK // 2, :]
    scores = jnp.einsum('bhncd,bhnsd->bhncs', q * jnp.exp(b - b_mid), k * jnp.exp(b_mid - b))
    causal = jnp.tril(jnp.ones((GLA_CHUNK, GLA_CHUNK), dtype=bool))
    scores = jnp.where(causal, scores, 0.0)
    o_intra = jnp.einsum('bhncs,bhnsv->bhncv', scores, v)
    upd = jnp.einsum('bhncd,bhncv->bhndv', k * jnp.exp(b_last - b), v)
    decay = jnp.exp(b_last[:, :, :, 0, :])

    def step(state, inp):
        dec, u = inp
        return dec[..., None] * state + u, state

    _, s_prev = lax.scan(step, jnp.zeros((bsz, nh, dk, dv), q.dtype),
                         (jnp.moveaxis(decay, 2, 0), jnp.moveaxis(upd, 2, 0)))
    s_prev = jnp.moveaxis(s_prev, 0, 2)
    o_inter = jnp.einsum('bhncd,bhndv->bhncv', q * jnp.exp(b), s_prev)
    return (o_intra + o_inter).reshape(bsz, nh, t_len, dv)


def gla(q, k, v, lr_f, lr_b, og, w_gf, b_gf, w_gb, b_gb, gla_norm):
    bsz, seq_len, _ = q.shape
    out_dtype = q.dtype

    def heads(t, d):
        return t.reshape(bsz, seq_len, GLA_HEADS, d).transpose(0, 2, 1, 3).astype(jnp.float32)

    gf = jax.nn.log_sigmoid((lr_f @ w_gf + b_gf).astype(jnp.float32)) / GLA_TAU
    gb = jax.nn.log_sigmoid((lr_b @ w_gb + b_gb).astype(jnp.float32)) / GLA_TAU
    pad = ((0, 0), (0, 0), (GLA_PAD, 0), (0, 0))
    qh = jnp.pad(heads(q, GLA_DK) * GLA_DK ** -0.5, pad)
    kh = jnp.pad(heads(k, GLA_DK), pad)
    vh = jnp.pad(heads(v, GLA_DV), pad)
    gfh = jnp.pad(heads(gf, GLA_DK), pad)
    gbh = jnp.pad(heads(gb, GLA_DK), pad)
    flip = lambda t: jnp.flip(t, axis=2)
    o_f = chunked_gla(qh, kh, vh, gfh)
    o_b = flip(chunked_gla(flip(qh), flip(kh), flip(vh), flip(gbh)))
    o = (o_f + o_b)[:, :, GLA_PAD:].transpose(0, 2, 1, 3)
    o = rmsnorm(o, gla_norm) * jax.nn.silu(og.reshape(bsz, seq_len, GLA_HEADS, GLA_DV).astype(jnp.float32))
    return o.reshape(bsz, seq_len, GLA_HEADS * GLA_DV).astype(out_dtype)


def hier_moe(x, w_rg, b_rg, w_re, b_re, w_g, w_u, w_d):
    bsz, seq_len, d = x.shape
    t = x.reshape(bsz * seq_len, d)
    p_group = jax.nn.softmax((t @ w_rg + b_rg).astype(jnp.float32), axis=-1)
    g_w, g_idx = lax.top_k(p_group, 1)
    e_logits = (t @ w_re + b_re).astype(jnp.float32).reshape(-1, N_GROUPS, EXPERTS_PER_GROUP)
    e_sel = jnp.take_along_axis(e_logits, g_idx[:, :, None], axis=1)[:, 0]
    e_w, e_idx = lax.top_k(jax.nn.softmax(e_sel, axis=-1), TOP_K)
    e_w = e_w / jnp.sum(e_w, axis=-1, keepdims=True)
    w_within = jnp.sum(jax.nn.one_hot(e_idx, EXPERTS_PER_GROUP, dtype=jnp.float32) * e_w[..., None], axis=1)
    gate = jax.nn.one_hot(g_idx[:, 0], N_GROUPS, dtype=jnp.float32)[:, :, None] * (g_w[:, :, None] * w_within[:, None, :])
    y = jnp.zeros((t.shape[0], d), jnp.float32)
    for gi in range(N_GROUPS):
        hg = jax.nn.silu(jnp.einsum('td,edf->tef', t, w_g[gi])) * jnp.einsum('td,edf->tef', t, w_u[gi])
        y = y + jnp.einsum('tef,efd->td', hg * gate[:, gi, :, None].astype(hg.dtype), w_d[gi])
    return y.reshape(bsz, seq_len, d).astype(x.dtype)


def encoder(x, meta_tokens, p, norm_final):
    bsz = x.shape[0]
    h = jnp.concatenate([jnp.broadcast_to(meta_tokens[None].astype(x.dtype), (bsz, N_META, x.shape[-1])), x], axis=1)
    seq_len = h.shape[1]
    pos = jnp.arange(seq_len, dtype=jnp.float32)
    inv_freq = 1.0 / (ROPE_THETA ** (jnp.arange(0, MLA_ROPE, 2, dtype=jnp.float32) / MLA_ROPE))
    ang = pos[:, None] * inv_freq[None, :]
    cos, sin = jnp.cos(ang), jnp.sin(ang)
    for l in range(DEPTH):
        hn = rmsnorm(h, p['norm_mix'][l])
        proj = hn @ p['w_in'][l]
        c_q, c_kv, k_pe, gq, gk, gv, lr_f, lr_b, og = jnp.split(proj, IN_SPLIT_POINTS, axis=-1)
        a = mla(c_q, c_kv, k_pe, p['q_a_norm'][l], p['w_uq'][l], p['kv_a_norm'][l], p['w_ukv'][l], cos, sin)
        g = gla(gq, gk, gv, lr_f, lr_b, og, p['w_gate_fwd'][l], p['b_gate_fwd'][l],
                p['w_gate_bwd'][l], p['b_gate_bwd'][l], p['gla_norm'][l])
        h = h + jnp.concatenate([a, g], axis=-1) @ p['w_out'][l]
        hn = rmsnorm(h, p['norm_ffn'][l])
        h = h + hier_moe(hn, p['w_router_group'][l], p['b_router_group'][l],
                         p['w_router_expert'][l], p['b_router_expert'][l],
                         p['w_expert_gate'][l], p['w_expert_up'][l], p['w_expert_down'][l])
    h = rmsnorm(h, norm_final)
    return h[:, N_META:]


def setup_inputs(seed: int = 0) -> dict:
    key = jax.random.key(seed)
    ks = jax.random.split(key, 32)
    f32 = jnp.float32

    def nrm(k, shape, scale):
        return jax.random.normal(k, shape, f32) * scale

    def gain(k, shape):
        return 1.0 + 0.01 * jax.random.normal(k, shape, f32)

    n_exp = (DEPTH, N_GROUPS, EXPERTS_PER_GROUP)
    return {
        'x_prompt': nrm(ks[0], (BATCH, SEQ, D_MODEL), 1.0),
        'x_sample': nrm(ks[1], (DEC_BATCH, DEC_SEQ, D_MODEL), 1.0),
        'meta_tokens': nrm(ks[2], (N_META, D_MODEL), 1.0),
        'norm_mix': gain(ks[3], (DEPTH, D_MODEL)),
        'w_in': nrm(ks[4], (DEPTH, D_MODEL, D_IN), D_MODEL ** -0.5),
        'q_a_norm': gain(ks[5], (DEPTH, MLA_Q_RANK)),
        'w_uq': nrm(ks[6], (DEPTH, MLA_Q_RANK, MLA_HEADS * (MLA_NOPE + MLA_ROPE)), MLA_Q_RANK ** -0.5),
        'kv_a_norm': gain(ks[7], (DEPTH, MLA_KV_RANK)),
        'w_ukv': nrm(ks[8], (DEPTH, MLA_KV_RANK, MLA_HEADS * (MLA_NOPE + MLA_V)), MLA_KV_RANK ** -0.5),
        'w_gate_fwd': nrm(ks[9], (DEPTH, GLA_GATE_RANK, GLA_HEADS * GLA_DK), GLA_GATE_RANK ** -0.5),
        'b_gate_fwd': nrm(ks[10], (DEPTH, GLA_HEADS * GLA_DK), 0.1),
        'w_gate_bwd': nrm(ks[11], (DEPTH, GLA_GATE_RANK, GLA_HEADS * GLA_DK), GLA_GATE_RANK ** -0.5),
        'b_gate_bwd': nrm(ks[12], (DEPTH, GLA_HEADS * GLA_DK), 0.1),
        'gla_norm': gain(ks[13], (DEPTH, GLA_DV)),
        'w_out': nrm(ks[14], (DEPTH, MIX_WIDTH, D_MODEL), MIX_WIDTH ** -0.5),
        'norm_ffn': gain(ks[15], (DEPTH, D_MODEL)),
        'w_router_group': nrm(ks[16], (DEPTH, D_MODEL, N_GROUPS), D_MODEL ** -0.5),
        'b_router_group': nrm(ks[17], (DEPTH, N_GROUPS), 0.01),
        'w_router_expert': nrm(ks[18], (DEPTH, D_MODEL, N_GROUPS * EXPERTS_PER_GROUP), D_MODEL ** -0.5),
        'b_router_expert': nrm(ks[19], (DEPTH, N_GROUPS * EXPERTS_PER_GROUP), 0.01),
        'w_expert_gate': nrm(ks[20], n_exp + (D_MODEL, D_EXPERT), D_MODEL ** -0.5),
        'w_expert_up': nrm(ks[21], n_exp + (D_MODEL, D_EXPERT), D_MODEL ** -0.5),
        'w_expert_down': nrm(ks[22], n_exp + (D_EXPERT, D_MODEL), D_EXPERT ** -0.5),
        'norm_final': gain(ks[23], (D_MODEL,)),
    }


def reference(x_prompt, x_sample, meta_tokens, norm_mix, w_in, q_a_norm, w_uq, kv_a_norm, w_ukv,
              w_gate_fwd, b_gate_fwd, w_gate_bwd, b_gate_bwd, gla_norm, w_out, norm_ffn,
              w_router_group, b_router_group, w_router_expert, b_router_expert,
              w_expert_gate, w_expert_up, w_expert_down, norm_final):
    params = {
        'norm_mix': norm_mix, 'w_in': w_in, 'q_a_norm': q_a_norm, 'w_uq': w_uq,
        'kv_a_norm': kv_a_norm, 'w_ukv': w_ukv, 'w_gate_fwd': w_gate_fwd, 'b_gate_fwd': b_gate_fwd,
        'w_gate_bwd': w_gate_bwd, 'b_gate_bwd': b_gate_bwd, 'gla_norm': gla_norm, 'w_out': w_out,
        'norm_ffn': norm_ffn, 'w_router_group': w_router_group, 'b_router_group': b_router_group,
        'w_router_expert': w_router_expert, 'b_router_expert': b_router_expert,
        'w_expert_gate': w_expert_gate, 'w_expert_up': w_expert_up, 'w_expert_down': w_expert_down,
    }
    y_prompt = encoder(x_prompt, meta_tokens, params, norm_final)
    y_sample = encoder(x_sample, meta_tokens, params, norm_final)
    return (y_prompt, y_sample)
```

```python
import functools

import numpy as np
import jax
import jax.numpy as jnp
from jax import lax
from jax.experimental import pallas as pl
from jax.experimental.pallas import tpu as pltpu

F32 = jnp.float32
BF16 = jnp.bfloat16

D_MODEL = 1024
N_META = 16
MLA_HEADS = 4
MLA_Q_RANK = 384
MLA_KV_RANK = 256
MLA_NOPE = 128
MLA_ROPE = 64
MLA_V = 128
ROPE_THETA = 10000.0
GLA_HEADS = 4
GLA_DK = 64
GLA_DV = 128
GLA_GATE_RANK = 16
GLA_TAU = 16.0
GLA_CHUNK = 64
N_GROUPS = 4
EXPERTS_PER_GROUP = 8
N_EXPERTS = N_GROUPS * EXPERTS_PER_GROUP
D_EXPERT = 256
EPS = 1e-6

LANES = 128
V7X_VMEM_BYTES = 64 * 1024 * 1024
VMEM_LIMIT = V7X_VMEM_BYTES * 7 // 8

ATTN_SCALE = (MLA_NOPE + MLA_ROPE) ** -0.5
QK_WIDTH = 2 * LANES

C_CQ = 0
C_CKV = C_CQ + MLA_Q_RANK
C_KPE = C_CKV + MLA_KV_RANK
C_GQ = C_KPE + LANES
C_GK = C_GQ + GLA_HEADS * GLA_DK
C_GV = C_GK + GLA_HEADS * GLA_DK
C_OG = C_GV + GLA_HEADS * GLA_DV
C_LR = C_OG + GLA_HEADS * GLA_DV
D_IN_PACKED = C_LR + LANES

ROUTER_GROUP_LANE0 = 0
ROUTER_EXPERT_LANE0 = N_GROUPS


def _cparams(*semantics):
    return pltpu.CompilerParams(dimension_semantics=semantics, vmem_limit_bytes=VMEM_LIMIT)


def _rms(x, g):
    return x * lax.rsqrt(jnp.mean(x * x, axis=-1, keepdims=True) + EPS) * g


def _dot(a, b):
    return jnp.dot(a, b, preferred_element_type=F32)


def _dot_nt(a, b):
    return lax.dot_general(a, b, (((1,), (1,)), ((), ())), preferred_element_type=F32)


def _dot_tn(a, b):
    return lax.dot_general(a, b, (((0,), (0,)), ((), ())), preferred_element_type=F32)


def _full_spec(shape):
    return pl.BlockSpec(shape, lambda *_: (0,) * len(shape))


def _rope_pairs(x, cos, sin_signed, first_half):
    swapped = jnp.where(first_half, pltpu.roll(x, LANES - MLA_ROPE // 2, 1), pltpu.roll(x, MLA_ROPE // 2, 1))
    return x * cos + swapped * sin_signed


def _inproj_body(x_ref, cos_ref, sin_ref, nmix_ref, win_ref, qan_ref, wuq_ref, kvan_ref, wukv_ref,
                 wgate_ref, bgate_ref,
                 q_ref, k_ref, v_ref, gq_ref, gk_ref, gv_ref, gf_ref, gb_ref, og_ref):
    hn = _rms(x_ref[...], nmix_ref[...]).astype(BF16)

    def proj(lo, hi):
        return _dot(hn, win_ref[:, lo:hi])

    cos = cos_ref[...]
    sin = sin_ref[...]
    lane = lax.broadcasted_iota(jnp.int32, cos.shape, 1)
    first_half = (lane & (MLA_ROPE - 1)) < MLA_ROPE // 2
    low_lanes = lane < MLA_ROPE

    cq = _rms(proj(C_CQ, C_CKV), qan_ref[...]).astype(BF16)
    qn = _dot(cq, wuq_ref[:, 0:MLA_HEADS * MLA_NOPE]) * ATTN_SCALE
    qr = _dot(cq, wuq_ref[:, MLA_HEADS * MLA_NOPE:])
    for j in range(MLA_HEADS // 2):
        rj = (_rope_pairs(qr[:, j * LANES:(j + 1) * LANES], cos, sin, first_half) * ATTN_SCALE).astype(BF16)
        for h in (2 * j, 2 * j + 1):
            q_ref[h, :, 0:LANES] = qn[:, h * LANES:(h + 1) * LANES].astype(BF16)
            q_ref[h, :, LANES:QK_WIDTH] = rj

    ckv = _rms(proj(C_CKV, C_KPE), kvan_ref[...]).astype(BF16)
    kv = _dot(ckv, wukv_ref[...])
    kr = _rope_pairs(proj(C_KPE, C_GQ), cos, sin, first_half)
    kr_even = jnp.where(low_lanes, kr, 0.0).astype(BF16)
    kr_odd = jnp.where(low_lanes, 0.0, kr).astype(BF16)
    for h in range(MLA_HEADS):
        base = h * (MLA_NOPE + MLA_V)
        k_ref[h, :, 0:LANES] = kv[:, base:base + MLA_NOPE].astype(BF16)
        k_ref[h, :, LANES:QK_WIDTH] = kr_even if h % 2 == 0 else kr_odd
        v_ref[h] = kv[:, base + MLA_NOPE:base + MLA_NOPE + MLA_V].astype(BF16)

    gq_ref[...] = proj(C_GQ, C_GK) * (GLA_DK ** -0.5)
    gk_ref[...] = proj(C_GK, C_GV)
    gv_ref[...] = proj(C_GV, C_OG)
    og_ref[...] = proj(C_OG, C_LR)
    pre = _dot(proj(C_LR, D_IN_PACKED).astype(BF16), wgate_ref[...]) + bgate_ref[...]
    logsig = jnp.minimum(pre, 0.0) - jnp.log1p(jnp.exp(-jnp.abs(pre)))
    gates = logsig * (1.0 / GLA_TAU)
    gf_ref[...] = gates[:, 0:GLA_HEADS * GLA_DK]
    gb_ref[...] = gates[:, GLA_HEADS * GLA_DK:]


def _inproj(x2d, cos, sin, w, tm):
    t = x2d.shape[0]
    blocks_per_seq = cos.shape[0] // tm
    hk = GLA_HEADS * GLA_DK
    hv = GLA_HEADS * GLA_DV
    row = lambda width: pl.BlockSpec((tm, width), lambda i: (i, 0))
    head_rows = lambda width: pl.BlockSpec((MLA_HEADS, tm, width), lambda i: (0, i, 0))
    tab = pl.BlockSpec((tm, LANES), lambda i: (i % blocks_per_seq, 0))
    out_shape = (
        jax.ShapeDtypeStruct((MLA_HEADS, t, QK_WIDTH), BF16),
        jax.ShapeDtypeStruct((MLA_HEADS, t, QK_WIDTH), BF16),
        jax.ShapeDtypeStruct((MLA_HEADS, t, MLA_V), BF16),
        jax.ShapeDtypeStruct((t, hk), F32),
        jax.ShapeDtypeStruct((t, hk), F32),
        jax.ShapeDtypeStruct((t, hv), F32),
        jax.ShapeDtypeStruct((t, hk), F32),
        jax.ShapeDtypeStruct((t, hk), F32),
        jax.ShapeDtypeStruct((t, hv), F32),
    )
    return pl.pallas_call(
        _inproj_body,
        grid=(t // tm,),
        in_specs=[row(D_MODEL), tab, tab,
                  _full_spec((1, D_MODEL)), _full_spec((D_MODEL, D_IN_PACKED)),
                  _full_spec((1, MLA_Q_RANK)), _full_spec(w['wuq'].shape),
                  _full_spec((1, MLA_KV_RANK)), _full_spec(w['wukv'].shape),
                  _full_spec(w['wgate'].shape), _full_spec(w['bgate'].shape)],
        out_specs=(head_rows(QK_WIDTH), head_rows(QK_WIDTH), head_rows(MLA_V),
                   row(hk), row(hk), row(hv), row(hk), row(hk), row(hv)),
        out_shape=out_shape,
        compiler_params=_cparams("parallel"),
        name="inproj",
    )(x2d, cos, sin, w['norm_mix'], w['win'], w['q_a_norm'], w['wuq'], w['kv_a_norm'], w['wukv'],
      w['wgate'], w['bgate'])


def _attn_body(q_ref, k_ref, v_ref, km_ref, vm_ref, o_ref, *, tk):
    q = q_ref[...]
    tq = q.shape[0]
    nk = k_ref.shape[0] // tk

    def update(carry, s, v):
        m, l, acc = carry
        m_new = jnp.maximum(m, jnp.max(s, axis=-1, keepdims=True))
        p = jnp.exp(s - m_new)
        alpha = jnp.exp(m - m_new)
        l_new = alpha * l + jnp.sum(p, axis=-1, keepdims=True)
        acc_new = alpha * acc + _dot(p.astype(BF16), v)
        return m_new, l_new, acc_new

    def body(j, carry):
        r = pl.multiple_of(j * tk, tk)
        s = _dot_nt(q, k_ref[pl.ds(r, tk), :])
        return update(carry, s, v_ref[pl.ds(r, tk), :])

    init = (jnp.full((tq, 1), -jnp.inf, F32), jnp.zeros((tq, 1), F32), jnp.zeros((tq, MLA_V), F32))
    carry = lax.fori_loop(0, nk, body, init)
    s = _dot_nt(q, km_ref[...])
    lane = lax.broadcasted_iota(jnp.int32, s.shape, 1)
    s = jnp.where(lane < N_META, s, -jnp.inf)
    _, l, acc = update(carry, s, vm_ref[...])
    o_ref[...] = (acc / l).astype(o_ref.dtype)


def _attention(q, k, v, km, vm, bsz, seq, tq, tk):
    nq = seq // tq
    return pl.pallas_call(
        functools.partial(_attn_body, tk=tk),
        grid=(bsz, MLA_HEADS, nq),
        in_specs=[pl.BlockSpec((None, tq, QK_WIDTH), lambda b, h, i: (h, b * nq + i, 0)),
                  pl.BlockSpec((None, seq, QK_WIDTH), lambda b, h, i: (h, b, 0)),
                  pl.BlockSpec((None, seq, MLA_V), lambda b, h, i: (h, b, 0)),
                  pl.BlockSpec((None, LANES, QK_WIDTH), lambda b, h, i: (h, 0, 0)),
                  pl.BlockSpec((None, LANES, MLA_V), lambda b, h, i: (h, 0, 0))],
        out_specs=pl.BlockSpec((tq, MLA_V), lambda b, h, i: (b * nq + i, h)),
        out_shape=jax.ShapeDtypeStruct((bsz * seq, MLA_HEADS * MLA_V), BF16),
        compiler_params=_cparams("parallel", "parallel", "arbitrary"),
        name="mla_attention",
    )(q, k, v, km, vm)


def _split3(x):
    hi = x.astype(BF16)
    r1 = x - hi.astype(F32)
    mid = r1.astype(BF16)
    lo = (r1 - mid.astype(F32)).astype(BF16)
    return hi, mid, lo


def _gla_chunk(q, k, g, v_even, v_odd, state, tri, keep, mid, last):
    lane = lax.broadcasted_iota(jnp.int32, (GLA_CHUNK, LANES), 1)
    even = lane < GLA_DK
    g_hi, g_mid, g_lo = _split3(g)
    b = _dot(tri, g_hi) + _dot(tri, g_mid) + _dot(tri, g_lo)
    b_last = b[last:last + 1, :]
    ke = (k * jnp.exp(b_last - b)).astype(BF16)
    upd = jnp.where(lax.broadcasted_iota(jnp.int32, (GLA_DV, LANES), 1) < GLA_DK,
                    _dot_tn(v_even, ke), _dot_tn(v_odd, ke))
    new_state = state * jnp.exp(b_last) + upd
    if q is None:
        return new_state, None, None
    b_mid = b[mid:mid + 1, :]
    qs = q * jnp.exp(b - b_mid)
    ks = (k * jnp.exp(b_mid - b)).astype(BF16)
    qs2 = jnp.concatenate([jnp.where(even, qs, 0.0), jnp.where(even, 0.0, qs)], axis=0).astype(BF16)
    scores = jnp.where(keep, _dot_nt(qs2, ks), 0.0).astype(BF16)
    qe = q * jnp.exp(b)
    qe2 = jnp.concatenate([jnp.where(even, qe, 0.0), jnp.where(even, 0.0, qe)], axis=0).astype(BF16)
    inter = _dot_nt(qe2, state.astype(BF16))
    o_even = _dot(scores[:GLA_CHUNK], v_even) + inter[:GLA_CHUNK]
    o_odd = _dot(scores[GLA_CHUNK:], v_odd) + inter[GLA_CHUNK:]
    return new_state, o_even, o_odd


def _gla_body(qf_ref, kf_ref, vf_ref, gf_ref, qb_ref, kb_ref, vb_ref, gb_ref, mk_ref, mv_ref, mg_ref,
              of_ref, ob_ref, state_ref):
    n_chunks = qf_ref.shape[0] // GLA_CHUNK
    n_pairs = GLA_HEADS // 2
    r = lax.broadcasted_iota(jnp.int32, (GLA_CHUNK, GLA_CHUNK), 0)
    c = lax.broadcasted_iota(jnp.int32, (GLA_CHUNK, GLA_CHUNK), 1)
    tri_f = jnp.where(c <= r, 1.0, 0.0).astype(BF16)
    tri_b = jnp.where(c >= r, 1.0, 0.0).astype(BF16)
    r2 = lax.broadcasted_iota(jnp.int32, (2 * GLA_CHUNK, GLA_CHUNK), 0) & (GLA_CHUNK - 1)
    c2 = lax.broadcasted_iota(jnp.int32, (2 * GLA_CHUNK, GLA_CHUNK), 1)
    keep_f = c2 <= r2
    keep_b = c2 >= r2
    mid_f, last_f = GLA_CHUNK // 2 - 1, GLA_CHUNK - 1
    mid_b, last_b = GLA_CHUNK // 2, 0

    def pair_cols(p):
        return slice(p * LANES, (p + 1) * LANES)

    def head_cols(h):
        return slice(h * GLA_DV, (h + 1) * GLA_DV)

    @pl.when(pl.program_id(1) == 0)
    def _():
        for p in range(n_pairs):
            zero = jnp.zeros((GLA_DV, LANES), F32)
            st, _, _ = _gla_chunk(None, mk_ref[:, pair_cols(p)], mg_ref[:, pair_cols(p)],
                                  mv_ref[:, head_cols(2 * p)].astype(BF16),
                                  mv_ref[:, head_cols(2 * p + 1)].astype(BF16),
                                  zero, tri_f, keep_f, mid_f, last_f)
            state_ref[p] = st
            state_ref[n_pairs + p] = zero

    def step(i, _):
        rows_f = pl.ds(pl.multiple_of(i * GLA_CHUNK, GLA_CHUNK), GLA_CHUNK)
        rows_b = pl.ds(pl.multiple_of((n_chunks - 1 - i) * GLA_CHUNK, GLA_CHUNK), GLA_CHUNK)
        for p in range(n_pairs):
            for (rows, q_ref, k_ref, v_ref, g_ref, o_ref, slot, tri, keep, mid, last) in (
                    (rows_f, qf_ref, kf_ref, vf_ref, gf_ref, of_ref, p, tri_f, keep_f, mid_f, last_f),
                    (rows_b, qb_ref, kb_ref, vb_ref, gb_ref, ob_ref, n_pairs + p, tri_b, keep_b, mid_b, last_b)):
                st, o_even, o_odd = _gla_chunk(
                    q_ref[rows, pair_cols(p)], k_ref[rows, pair_cols(p)], g_ref[rows, pair_cols(p)],
                    v_ref[rows, head_cols(2 * p)].astype(BF16), v_ref[rows, head_cols(2 * p + 1)].astype(BF16),
                    state_ref[slot], tri, keep, mid, last)
                state_ref[slot] = st
                o_ref[rows, head_cols(2 * p)] = o_even
                o_ref[rows, head_cols(2 * p + 1)] = o_odd
        return 0

    lax.fori_loop(0, n_chunks, step, 0)


def _gla(gq, gk, gv, gf, gb, mk, mv, mg, bsz, seq, tb):
    nb = seq // tb
    hk = GLA_HEADS * GLA_DK
    hv = GLA_HEADS * GLA_DV
    fwd = lambda width: pl.BlockSpec((tb, width), lambda b, j: (b * nb + j, 0))
    bwd = lambda width: pl.BlockSpec((tb, width), lambda b, j: (b * nb + nb - 1 - j, 0))
    t = bsz * seq
    return pl.pallas_call(
        _gla_body,
        grid=(bsz, nb),
        in_specs=[fwd(hk), fwd(hk), fwd(hv), fwd(hk), bwd(hk), bwd(hk), bwd(hv), bwd(hk),
                  _full_spec(mk.shape), _full_spec(mv.shape), _full_spec(mg.shape)],
        out_specs=(fwd(hv), bwd(hv)),
        out_shape=(jax.ShapeDtypeStruct((t, hv), F32), jax.ShapeDtypeStruct((t, hv), F32)),
        scratch_shapes=[pltpu.VMEM((2 * (GLA_HEADS // 2), GLA_DV, LANES), F32)],
        compiler_params=_cparams("parallel", "arbitrary"),
        name="gla_scan",
    )(gq, gk, gv, gf, gq, gk, gv, gb, mk, mv, mg)


def _mix_body(x_ref, a_ref, of_ref, ob_ref, og_ref, gnorm_ref, wout_ref, nffn_ref, wr_hi_ref, wr_lo_ref, br_ref,
              h1_ref, hn_ref, gate_ref):
    a_width = MLA_HEADS * MLA_V
    h1 = x_ref[...] + _dot(a_ref[...], wout_ref[0:a_width, :])
    for h in range(GLA_HEADS):
        cols = slice(h * GLA_DV, (h + 1) * GLA_DV)
        o = of_ref[:, cols] + ob_ref[:, cols]
        og = og_ref[:, cols]
        silu = og / (1.0 + jnp.exp(-og))
        gh = (_rms(o, gnorm_ref[...]) * silu).astype(BF16)
        h1 = h1 + _dot(gh, wout_ref[a_width + h * GLA_DV:a_width + (h + 1) * GLA_DV, :])
    h1_ref[...] = h1
    hn = _rms(h1, nffn_ref[...])
    hn_ref[...] = hn.astype(BF16)

    hn_hi = hn.astype(BF16)
    hn_lo = (hn - hn_hi.astype(F32)).astype(BF16)
    logits = (_dot(hn_hi, wr_hi_ref[...]) + _dot(hn_lo, wr_hi_ref[...]) + _dot(hn_hi, wr_lo_ref[...])
              + br_ref[...])
    lane = lax.broadcasted_iota(jnp.int32, logits.shape, 1).astype(F32)
    none = float(LANES)
    neg = -jnp.inf

    def lane_max(x):
        return jnp.max(x, axis=-1, keepdims=True)

    def lane_sum(x):
        return jnp.sum(x, axis=-1, keepdims=True)

    def first_lane(mask):
        return jnp.min(jnp.where(mask, lane, none), axis=-1, keepdims=True)

    is_group = lane < float(N_GROUPS)
    g_max = lane_max(jnp.where(is_group, logits, neg))
    g_exp = jnp.where(is_group, jnp.exp(logits - g_max), 0.0)
    g_prob = g_exp / lane_sum(g_exp)
    g_w = lane_max(g_prob)
    g_idx = first_lane(is_group & (g_prob == g_w))
    e_lo = float(ROUTER_EXPERT_LANE0) + float(EXPERTS_PER_GROUP) * g_idx
    sel = (lane >= e_lo) & (lane < e_lo + float(EXPERTS_PER_GROUP))
    e_max = lane_max(jnp.where(sel, logits, neg))
    e_exp = jnp.where(sel, jnp.exp(logits - e_max), 0.0)
    e_prob = e_exp / lane_sum(e_exp)
    p1 = lane_max(jnp.where(sel, e_prob, neg))
    i1 = first_lane(sel & (e_prob == p1))
    rest = sel & (lane != i1)
    p2 = lane_max(jnp.where(rest, e_prob, neg))
    i2 = first_lane(rest & (e_prob == p2))
    denom = p1 + p2
    w_within = jnp.where(lane == i1, p1 / denom, jnp.where(lane == i2, p2 / denom, 0.0))
    gate_ref[...] = g_w * w_within


def _mix(x2d, a, o_f, o_b, og, w, tm):
    t = x2d.shape[0]
    hv = GLA_HEADS * GLA_DV
    row = lambda width: pl.BlockSpec((tm, width), lambda i: (i, 0))
    return pl.pallas_call(
        _mix_body,
        grid=(t // tm,),
        in_specs=[row(D_MODEL), row(MLA_HEADS * MLA_V), row(hv), row(hv), row(hv),
                  _full_spec((1, GLA_DV)), _full_spec(w['wout'].shape), _full_spec((1, D_MODEL)),
                  _full_spec(w['wr_hi'].shape), _full_spec(w['wr_lo'].shape), _full_spec(w['br'].shape)],
        out_specs=(row(D_MODEL), row(D_MODEL), row(LANES)),
        out_shape=(jax.ShapeDtypeStruct((t, D_MODEL), F32), jax.ShapeDtypeStruct((t, D_MODEL), BF16),
                   jax.ShapeDtypeStruct((t, LANES), F32)),
        compiler_params=_cparams("parallel"),
        name="mix_router",
    )(x2d, a, o_f, o_b, og, w['gla_norm'], w['wout'], w['norm_ffn'], w['wr_hi'], w['wr_lo'], w['br'])


def _moe_body(hn_ref, gate_ref, h1_ref, wgu_ref, wd_ref, nfin_ref, out_ref, acc_ref):
    e = pl.program_id(1)

    @pl.when(e == 0)
    def _():
        acc_ref[...] = jnp.zeros_like(acc_ref)

    h = _dot(hn_ref[...], wgu_ref[...])
    gate = gate_ref[...]
    lane = lax.broadcasted_iota(jnp.int32, gate.shape, 1)
    g_col = jnp.sum(jnp.where(lane == e + ROUTER_EXPERT_LANE0, gate, 0.0), axis=-1, keepdims=True)
    hg = h[:, :D_EXPERT]
    hu = h[:, D_EXPERT:]
    act = (hg / (1.0 + jnp.exp(-hg))) * hu * g_col
    acc_ref[...] += _dot(act.astype(BF16), wd_ref[...])

    @pl.when(e == N_EXPERTS - 1)
    def _():
        out_ref[...] = _rms(h1_ref[...] + acc_ref[...], nfin_ref[...])


def _moe(hn, gate, h1, w, tm):
    t = hn.shape[0]
    row = lambda width: pl.BlockSpec((tm, width), lambda i, e: (i, 0))
    return pl.pallas_call(
        _moe_body,
        grid=(t // tm, N_EXPERTS),
        in_specs=[row(D_MODEL), row(LANES), row(D_MODEL),
                  pl.BlockSpec((None, D_MODEL, 2 * D_EXPERT), lambda i, e: (e, 0, 0)),
                  pl.BlockSpec((None, D_EXPERT, D_MODEL), lambda i, e: (e, 0, 0)),
                  _full_spec((1, D_MODEL))],
        out_specs=row(D_MODEL),
        out_shape=jax.ShapeDtypeStruct((t, D_MODEL), F32),
        scratch_shapes=[pltpu.VMEM((tm, D_MODEL), F32)],
        compiler_params=_cparams("parallel", "arbitrary"),
        name="moe_experts",
    )(hn, gate, h1, w['wgu'], w['wd'], w['norm_final'])


def _rope_tables(positions):
    inv_freq = 1.0 / (ROPE_THETA ** (jnp.arange(0, MLA_ROPE, 2, dtype=F32) / MLA_ROPE))
    ang = positions.astype(F32)[:, None] * inv_freq[None, :]
    cos, sin = jnp.cos(ang), jnp.sin(ang)
    reps = LANES // MLA_ROPE
    return jnp.tile(jnp.concatenate([cos, cos], axis=-1), (1, reps)), jnp.tile(jnp.concatenate([-sin, sin], axis=-1), (1, reps))


def _pack_weights(norm_mix, w_in, q_a_norm, w_uq, kv_a_norm, w_ukv, w_gate_fwd, b_gate_fwd, w_gate_bwd, b_gate_bwd,
                  gla_norm, w_out, norm_ffn, w_router_group, b_router_group, w_router_expert, b_router_expert,
                  w_expert_gate, w_expert_up, w_expert_down, norm_final):
    l = 0
    hk = GLA_HEADS * GLA_DK
    hv = GLA_HEADS * GLA_DV
    c_q, c_kv, k_pe, gq, gk, gv, lr_f, lr_b, og = jnp.split(
        w_in[l], np.cumsum([MLA_Q_RANK, MLA_KV_RANK, MLA_ROPE, hk, hk, hv, GLA_GATE_RANK, GLA_GATE_RANK])[:].tolist(),
        axis=-1)
    lr_pad = jnp.zeros((D_MODEL, LANES - 2 * GLA_GATE_RANK), F32)
    win = jnp.concatenate([c_q, c_kv, k_pe, k_pe, gq, gk, gv, og, lr_f, lr_b, lr_pad], axis=-1).astype(BF16)
    wuq = w_uq[l].reshape(MLA_Q_RANK, MLA_HEADS, MLA_NOPE + MLA_ROPE)
    wuq = jnp.concatenate([wuq[:, :, :MLA_NOPE].reshape(MLA_Q_RANK, -1), wuq[:, :, MLA_NOPE:].reshape(MLA_Q_RANK, -1)],
                          axis=-1).astype(BF16)
    wgate = jnp.zeros((LANES, 2 * hk), F32)
    wgate = wgate.at[0:GLA_GATE_RANK, 0:hk].set(w_gate_fwd[l])
    wgate = wgate.at[GLA_GATE_RANK:2 * GLA_GATE_RANK, hk:].set(w_gate_bwd[l])
    wr = jnp.zeros((D_MODEL, LANES), F32)
    wr = wr.at[:, ROUTER_GROUP_LANE0:ROUTER_GROUP_LANE0 + N_GROUPS].set(w_router_group[l])
    wr = wr.at[:, ROUTER_EXPERT_LANE0:ROUTER_EXPERT_LANE0 + N_EXPERTS].set(w_router_expert[l])
    wr_hi = wr.astype(BF16)
    br = jnp.zeros((1, LANES), F32)
    br = br.at[0, ROUTER_GROUP_LANE0:ROUTER_GROUP_LANE0 + N_GROUPS].set(b_router_group[l])
    br = br.at[0, ROUTER_EXPERT_LANE0:ROUTER_EXPERT_LANE0 + N_EXPERTS].set(b_router_expert[l])
    wgu = jnp.concatenate([w_expert_gate[l], w_expert_up[l]], axis=-1).reshape(N_EXPERTS, D_MODEL, 2 * D_EXPERT)
    return {
        'norm_mix': norm_mix[l][None], 'win': win, 'q_a_norm': q_a_norm[l][None], 'wuq': wuq,
        'kv_a_norm': kv_a_norm[l][None], 'wukv': w_ukv[l].astype(BF16),
        'wgate': wgate.astype(BF16), 'bgate': jnp.concatenate([b_gate_fwd[l], b_gate_bwd[l]])[None],
        'gla_norm': gla_norm[l][None], 'wout': w_out[l].astype(BF16), 'norm_ffn': norm_ffn[l][None],
        'wr_hi': wr_hi, 'wr_lo': (wr - wr_hi.astype(F32)).astype(BF16), 'br': br,
        'wgu': wgu.astype(BF16), 'wd': w_expert_down[l].reshape(N_EXPERTS, D_EXPERT, D_MODEL).astype(BF16),
        'norm_final': norm_final[None],
    }


def _meta_streams(meta_tokens, w):
    cos, sin = _rope_tables(jnp.arange(N_META))
    _, k, v, _, gk, gv, gf, _, _ = _inproj(meta_tokens, cos, sin, w, N_META)
    pad_keys = ((0, 0), (0, LANES - N_META), (0, 0))
    front = ((GLA_CHUNK - N_META, 0), (0, 0))
    return (jnp.pad(k, pad_keys), jnp.pad(v, pad_keys), jnp.pad(gk, front), jnp.pad(gv, front), jnp.pad(gf, front))


def _token_mixers(x, meta, w, tm, tq, tk, tb):
    bsz, seq, _ = x.shape
    km, vm, mk, mv, mg = meta
    x2d = x.reshape(bsz * seq, D_MODEL)
    cos, sin = _rope_tables(N_META + jnp.arange(seq))
    q, k, v, gq, gk, gv, gf, gb, og = _inproj(x2d, cos, sin, w, tm)
    a = _attention(q, k, v, km, vm, bsz, seq, tq, tk)
    o_f, o_b = _gla(gq, gk, gv, gf, gb, mk, mv, mg, bsz, seq, tb)
    return _mix(x2d, a, o_f, o_b, og, w, tm)


def kernel(x_prompt, x_sample, meta_tokens, norm_mix, w_in, q_a_norm, w_uq, kv_a_norm, w_ukv, w_gate_fwd, b_gate_fwd, w_gate_bwd, b_gate_bwd, gla_norm, w_out, norm_ffn, w_router_group, b_router_group, w_router_expert, b_router_expert, w_expert_gate, w_expert_up, w_expert_down, norm_final):
    w = _pack_weights(norm_mix, w_in, q_a_norm, w_uq, kv_a_norm, w_ukv, w_gate_fwd, b_gate_fwd, w_gate_bwd,
                      b_gate_bwd, gla_norm, w_out, norm_ffn, w_router_group, b_router_group, w_router_expert,
                      b_router_expert, w_expert_gate, w_expert_up, w_expert_down, norm_final)
    meta = _meta_streams(meta_tokens, w)
    outs = []
    for x in (x_prompt, x_sample):
        h1, hn, gate = _token_mixers(x, meta, w, tm=512, tq=512, tk=512, tb=512)
        outs.append(_moe(hn, gate, h1, w, tm=1024).reshape(x.shape))
    return tuple(outs)
```

```python
import functools

import numpy as np
import jax
import jax.numpy as jnp
from jax import lax
from jax.experimental import pallas as pl
from jax.experimental.pallas import tpu as pltpu

F32 = jnp.float32
BF16 = jnp.bfloat16

D_MODEL = 1024
N_META = 16
MLA_HEADS = 4
MLA_Q_RANK = 384
MLA_KV_RANK = 256
MLA_NOPE = 128
MLA_ROPE = 64
MLA_V = 128
ROPE_THETA = 10000.0
GLA_HEADS = 4
GLA_DK = 64
GLA_DV = 128
GLA_GATE_RANK = 16
GLA_TAU = 16.0
GLA_CHUNK = 64
N_GROUPS = 4
EXPERTS_PER_GROUP = 8
N_EXPERTS = N_GROUPS * EXPERTS_PER_GROUP
D_EXPERT = 256
EPS = 1e-6

LANES = 128
V7X_VMEM_BYTES = 64 * 1024 * 1024
VMEM_LIMIT = V7X_VMEM_BYTES * 7 // 8

ATTN_SCALE = (MLA_NOPE + MLA_ROPE) ** -0.5 * float(np.log2(np.e))
QK_WIDTH = 2 * LANES
V_WIDTH = 2 * LANES
ATTN_GROUP = 4

C_CQ = 0
C_CKV = C_CQ + MLA_Q_RANK
C_KPE = C_CKV + MLA_KV_RANK
C_GQ = C_KPE + LANES
C_GK = C_GQ + GLA_HEADS * GLA_DK
C_GV = C_GK + GLA_HEADS * GLA_DK
C_OG = C_GV + GLA_HEADS * GLA_DV
C_LR = C_OG + GLA_HEADS * GLA_DV
D_IN_PACKED = C_LR + LANES

ROUTER_GROUP_LANE0 = 0
ROUTER_EXPERT_LANE0 = N_GROUPS


def _cparams(*semantics):
    return pltpu.CompilerParams(dimension_semantics=semantics, vmem_limit_bytes=VMEM_LIMIT)


def _rms(x, g):
    return x * lax.rsqrt(jnp.mean(x * x, axis=-1, keepdims=True) + EPS) * g


def _dot(a, b):
    return jnp.dot(a, b, preferred_element_type=F32)


def _dot_nt(a, b):
    return lax.dot_general(a, b, (((1,), (1,)), ((), ())), preferred_element_type=F32)


def _dot_tn(a, b):
    return lax.dot_general(a, b, (((0,), (0,)), ((), ())), preferred_element_type=F32)


def _full_spec(shape):
    return pl.BlockSpec(shape, lambda *_: (0,) * len(shape))


def _rope_pairs(x, cos, sin_signed, first_half):
    swapped = jnp.where(first_half, pltpu.roll(x, LANES - MLA_ROPE // 2, 1), pltpu.roll(x, MLA_ROPE // 2, 1))
    return x * cos + swapped * sin_signed


def _inproj_body(x_ref, cos_ref, sin_ref, nmix_ref, win_ref, qan_ref, wuq_ref, kvan_ref, wukv_ref,
                 wgate_ref, bgate_ref,
                 q_ref, k_ref, v_ref, gq_ref, gk_ref, gv_ref, gf_ref, gb_ref, og_ref):
    hn = _rms(x_ref[...], nmix_ref[...]).astype(BF16)

    def proj(lo, hi):
        return _dot(hn, win_ref[:, lo:hi])

    cos = cos_ref[...]
    sin = sin_ref[...]
    lane = lax.broadcasted_iota(jnp.int32, cos.shape, 1)
    first_half = (lane & (MLA_ROPE - 1)) < MLA_ROPE // 2
    low_lanes = lane < MLA_ROPE

    cq = _rms(proj(C_CQ, C_CKV), qan_ref[...]).astype(BF16)
    qn = _dot(cq, wuq_ref[:, 0:MLA_HEADS * MLA_NOPE]) * ATTN_SCALE
    qr = _dot(cq, wuq_ref[:, MLA_HEADS * MLA_NOPE:])
    for j in range(MLA_HEADS // 2):
        rj = (_rope_pairs(qr[:, j * LANES:(j + 1) * LANES], cos, sin, first_half) * ATTN_SCALE).astype(BF16)
        for h in (2 * j, 2 * j + 1):
            q_ref[h, :, 0:LANES] = qn[:, h * LANES:(h + 1) * LANES].astype(BF16)
            q_ref[h, :, LANES:QK_WIDTH] = rj

    ckv = _rms(proj(C_CKV, C_KPE), kvan_ref[...]).astype(BF16)
    kv = _dot(ckv, wukv_ref[...])
    kr = _rope_pairs(proj(C_KPE, C_GQ), cos, sin, first_half)
    kr_even = jnp.where(low_lanes, kr, 0.0).astype(BF16)
    kr_odd = jnp.where(low_lanes, 0.0, kr).astype(BF16)
    for h in range(MLA_HEADS):
        base = h * (MLA_NOPE + MLA_V)
        k_ref[h, :, 0:LANES] = kv[:, base:base + MLA_NOPE].astype(BF16)
        k_ref[h, :, LANES:QK_WIDTH] = kr_even if h % 2 == 0 else kr_odd
        v_ref[h, :, 0:MLA_V] = kv[:, base + MLA_NOPE:base + MLA_NOPE + MLA_V].astype(BF16)
        v_ref[h, :, MLA_V:V_WIDTH] = jnp.ones((kv.shape[0], V_WIDTH - MLA_V), BF16)

    gq_ref[...] = proj(C_GQ, C_GK) * (GLA_DK ** -0.5)
    gk_ref[...] = proj(C_GK, C_GV)
    gv_ref[...] = proj(C_GV, C_OG)
    og_ref[...] = proj(C_OG, C_LR)
    pre = _dot(proj(C_LR, D_IN_PACKED).astype(BF16), wgate_ref[...]) + bgate_ref[...]
    logsig = jnp.minimum(pre, 0.0) - jnp.log1p(jnp.exp(-jnp.abs(pre)))
    gates = logsig * (1.0 / GLA_TAU)
    gf_ref[...] = gates[:, 0:GLA_HEADS * GLA_DK]
    gb_ref[...] = gates[:, GLA_HEADS * GLA_DK:]


def _inproj(x2d, cos, sin, w, tm):
    t = x2d.shape[0]
    blocks_per_seq = cos.shape[0] // tm
    hk = GLA_HEADS * GLA_DK
    hv = GLA_HEADS * GLA_DV
    row = lambda width: pl.BlockSpec((tm, width), lambda i: (i, 0))
    head_rows = lambda width: pl.BlockSpec((MLA_HEADS, tm, width), lambda i: (0, i, 0))
    tab = pl.BlockSpec((tm, LANES), lambda i: (i % blocks_per_seq, 0))
    out_shape = (
        jax.ShapeDtypeStruct((MLA_HEADS, t, QK_WIDTH), BF16),
        jax.ShapeDtypeStruct((MLA_HEADS, t, QK_WIDTH), BF16),
        jax.ShapeDtypeStruct((MLA_HEADS, t, V_WIDTH), BF16),
        jax.ShapeDtypeStruct((t, hk), F32),
        jax.ShapeDtypeStruct((t, hk), F32),
        jax.ShapeDtypeStruct((t, hv), F32),
        jax.ShapeDtypeStruct((t, hk), F32),
        jax.ShapeDtypeStruct((t, hk), F32),
        jax.ShapeDtypeStruct((t, hv), F32),
    )
    return pl.pallas_call(
        _inproj_body,
        grid=(t // tm,),
        in_specs=[row(D_MODEL), tab, tab,
                  _full_spec((1, D_MODEL)), _full_spec((D_MODEL, D_IN_PACKED)),
                  _full_spec((1, MLA_Q_RANK)), _full_spec(w['wuq'].shape),
                  _full_spec((1, MLA_KV_RANK)), _full_spec(w['wukv'].shape),
                  _full_spec(w['wgate'].shape), _full_spec(w['bgate'].shape)],
        out_specs=(head_rows(QK_WIDTH), head_rows(QK_WIDTH), head_rows(V_WIDTH),
                   row(hk), row(hk), row(hv), row(hk), row(hk), row(hv)),
        out_shape=out_shape,
        compiler_params=_cparams("parallel"),
        name="inproj",
    )(x2d, cos, sin, w['norm_mix'], w['win'], w['q_a_norm'], w['wuq'], w['kv_a_norm'], w['wukv'],
      w['wgate'], w['bgate'])


def _attn_body(q_ref, k_ref, v_ref, km_ref, vm_ref, o_ref, s_ref, acc_ref, *, tk):
    q = q_ref[...]
    n_groups = k_ref.shape[0] // (ATTN_GROUP * tk)

    def scores(j):
        return _dot_nt(q, k_ref[pl.ds(pl.multiple_of(j * tk, tk), tk), :])

    def values(j):
        return v_ref[pl.ds(pl.multiple_of(j * tk, tk), tk), :]

    def absorb(m, s, v):
        m_new = jnp.maximum(m, jnp.max(s, axis=-1, keepdims=True))
        p = jnp.exp2(s - m_new)
        acc_ref[...] = jnp.exp2(m - m_new) * acc_ref[...] + _dot(p.astype(BF16), v)
        return m_new

    s_ref[0] = scores(0)
    sm = _dot_nt(q, km_ref[...])
    sm = jnp.where(lax.broadcasted_iota(jnp.int32, sm.shape, 1) < N_META, sm, -jnp.inf)
    m = jnp.max(sm, axis=-1, keepdims=True)
    acc_ref[...] = _dot(jnp.exp2(sm - m).astype(BF16), vm_ref[...])

    def group(g, m, last):
        for i in range(ATTN_GROUP):
            j = ATTN_GROUP * g + i
            s = s_ref[i % 2]
            if not (last and i == ATTN_GROUP - 1):
                s_ref[(i + 1) % 2] = scores(j + 1)
            m = absorb(m, s, values(j))
        return m

    m = lax.fori_loop(0, n_groups - 1, lambda g, m: group(g, m, False), m)
    group(n_groups - 1, m, True)
    acc = acc_ref[...]
    o_ref[...] = (acc[:, :MLA_V] / acc[:, MLA_V:]).astype(o_ref.dtype)


def _attention(q, k, v, km, vm, bsz, seq, tq, tk):
    nq = seq // tq
    return pl.pallas_call(
        functools.partial(_attn_body, tk=tk),
        grid=(bsz, MLA_HEADS, nq),
        in_specs=[pl.BlockSpec((None, tq, QK_WIDTH), lambda b, h, i: (h, b * nq + i, 0)),
                  pl.BlockSpec((None, seq, QK_WIDTH), lambda b, h, i: (h, b, 0)),
                  pl.BlockSpec((None, seq, V_WIDTH), lambda b, h, i: (h, b, 0)),
                  pl.BlockSpec((None, LANES, QK_WIDTH), lambda b, h, i: (h, 0, 0)),
                  pl.BlockSpec((None, LANES, V_WIDTH), lambda b, h, i: (h, 0, 0))],
        out_specs=pl.BlockSpec((tq, MLA_V), lambda b, h, i: (b * nq + i, h)),
        out_shape=jax.ShapeDtypeStruct((bsz * seq, MLA_HEADS * MLA_V), BF16),
        scratch_shapes=[pltpu.VMEM((2, tq, tk), F32), pltpu.VMEM((tq, V_WIDTH), F32)],
        compiler_params=_cparams("parallel", "parallel", "arbitrary"),
        name="mla_attention",
    )(q, k, v, km, vm)


def _split3(x):
    hi = x.astype(BF16)
    r1 = x - hi.astype(F32)
    mid = r1.astype(BF16)
    lo = (r1 - mid.astype(F32)).astype(BF16)
    return hi, mid, lo


def _gla_chunk(q, k, g, v_even, v_odd, state, tri, keep, mid, last):
    lane = lax.broadcasted_iota(jnp.int32, (GLA_CHUNK, LANES), 1)
    even = lane < GLA_DK
    g_hi, g_mid, g_lo = _split3(g)
    b = _dot(tri, g_hi) + _dot(tri, g_mid) + _dot(tri, g_lo)
    b_last = b[last:last + 1, :]
    ke = (k * jnp.exp(b_last - b)).astype(BF16)
    upd = jnp.where(lax.broadcasted_iota(jnp.int32, (GLA_DV, LANES), 1) < GLA_DK,
                    _dot_tn(v_even, ke), _dot_tn(v_odd, ke))
    new_state = state * jnp.exp(b_last) + upd
    if q is None:
        return new_state, None, None
    b_mid = b[mid:mid + 1, :]
    qs = q * jnp.exp(b - b_mid)
    ks = (k * jnp.exp(b_mid - b)).astype(BF16)
    qs2 = jnp.concatenate([jnp.where(even, qs, 0.0), jnp.where(even, 0.0, qs)], axis=0).astype(BF16)
    scores = jnp.where(keep, _dot_nt(qs2, ks), 0.0).astype(BF16)
    qe = q * jnp.exp(b)
    qe2 = jnp.concatenate([jnp.where(even, qe, 0.0), jnp.where(even, 0.0, qe)], axis=0).astype(BF16)
    inter = _dot_nt(qe2, state.astype(BF16))
    o_even = _dot(scores[:GLA_CHUNK], v_even) + inter[:GLA_CHUNK]
    o_odd = _dot(scores[GLA_CHUNK:], v_odd) + inter[GLA_CHUNK:]
    return new_state, o_even, o_odd


def _gla_body(qf_ref, kf_ref, vf_ref, gf_ref, qb_ref, kb_ref, vb_ref, gb_ref, mk_ref, mv_ref, mg_ref,
              of_ref, ob_ref, state_ref):
    n_chunks = qf_ref.shape[0] // GLA_CHUNK
    n_pairs = GLA_HEADS // 2
    r = lax.broadcasted_iota(jnp.int32, (GLA_CHUNK, GLA_CHUNK), 0)
    c = lax.broadcasted_iota(jnp.int32, (GLA_CHUNK, GLA_CHUNK), 1)
    tri_f = jnp.where(c <= r, 1.0, 0.0).astype(BF16)
    tri_b = jnp.where(c >= r, 1.0, 0.0).astype(BF16)
    r2 = lax.broadcasted_iota(jnp.int32, (2 * GLA_CHUNK, GLA_CHUNK), 0) & (GLA_CHUNK - 1)
    c2 = lax.broadcasted_iota(jnp.int32, (2 * GLA_CHUNK, GLA_CHUNK), 1)
    keep_f = c2 <= r2
    keep_b = c2 >= r2
    mid_f, last_f = GLA_CHUNK // 2 - 1, GLA_CHUNK - 1
    mid_b, last_b = GLA_CHUNK // 2, 0

    def pair_cols(p):
        return slice(p * LANES, (p + 1) * LANES)

    def head_cols(h):
        return slice(h * GLA_DV, (h + 1) * GLA_DV)

    @pl.when(pl.program_id(1) == 0)
    def _():
        for p in range(n_pairs):
            zero = jnp.zeros((GLA_DV, LANES), F32)
            st, _, _ = _gla_chunk(None, mk_ref[:, pair_cols(p)], mg_ref[:, pair_cols(p)],
                                  mv_ref[:, head_cols(2 * p)].astype(BF16),
                                  mv_ref[:, head_cols(2 * p + 1)].astype(BF16),
                                  zero, tri_f, keep_f, mid_f, last_f)
            state_ref[p] = st
            state_ref[n_pairs + p] = zero

    def step(i, _):
        rows_f = pl.ds(pl.multiple_of(i * GLA_CHUNK, GLA_CHUNK), GLA_CHUNK)
        rows_b = pl.ds(pl.multiple_of((n_chunks - 1 - i) * GLA_CHUNK, GLA_CHUNK), GLA_CHUNK)
        for p in range(n_pairs):
            for (rows, q_ref, k_ref, v_ref, g_ref, o_ref, slot, tri, keep, mid, last) in (
                    (rows_f, qf_ref, kf_ref, vf_ref, gf_ref, of_ref, p, tri_f, keep_f, mid_f, last_f),
                    (rows_b, qb_ref, kb_ref, vb_ref, gb_ref, ob_ref, n_pairs + p, tri_b, keep_b, mid_b, last_b)):
                st, o_even, o_odd = _gla_chunk(
                    q_ref[rows, pair_cols(p)], k_ref[rows, pair_cols(p)], g_ref[rows, pair_cols(p)],
                    v_ref[rows, head_cols(2 * p)].astype(BF16), v_ref[rows, head_cols(2 * p + 1)].astype(BF16),
                    state_ref[slot], tri, keep, mid, last)
                state_ref[slot] = st
                o_ref[rows, head_cols(2 * p)] = o_even
                o_ref[rows, head_cols(2 * p + 1)] = o_odd
        return 0

    lax.fori_loop(0, n_chunks, step, 0)


def _gla(gq, gk, gv, gf, gb, mk, mv, mg, bsz, seq, tb):
    nb = seq // tb
    hk = GLA_HEADS * GLA_DK
    hv = GLA_HEADS * GLA_DV
    fwd = lambda width: pl.BlockSpec((tb, width), lambda b, j: (b * nb + j, 0))
    bwd = lambda width: pl.BlockSpec((tb, width), lambda b, j: (b * nb + nb - 1 - j, 0))
    t = bsz * seq
    return pl.pallas_call(
        _gla_body,
        grid=(bsz, nb),
        in_specs=[fwd(hk), fwd(hk), fwd(hv), fwd(hk), bwd(hk), bwd(hk), bwd(hv), bwd(hk),
                  _full_spec(mk.shape), _full_spec(mv.shape), _full_spec(mg.shape)],
        out_specs=(fwd(hv), bwd(hv)),
        out_shape=(jax.ShapeDtypeStruct((t, hv), F32), jax.ShapeDtypeStruct((t, hv), F32)),
        scratch_shapes=[pltpu.VMEM((2 * (GLA_HEADS // 2), GLA_DV, LANES), F32)],
        compiler_params=_cparams("parallel", "arbitrary"),
        name="gla_scan",
    )(gq, gk, gv, gf, gq, gk, gv, gb, mk, mv, mg)


def _mix_body(x_ref, a_ref, of_ref, ob_ref, og_ref, gnorm_ref, wout_ref, nffn_ref, wr_hi_ref, wr_lo_ref, br_ref,
              h1_ref, hn_ref, gate_ref):
    a_width = MLA_HEADS * MLA_V
    h1 = x_ref[...] + _dot(a_ref[...], wout_ref[0:a_width, :])
    for h in range(GLA_HEADS):
        cols = slice(h * GLA_DV, (h + 1) * GLA_DV)
        o = of_ref[:, cols] + ob_ref[:, cols]
        og = og_ref[:, cols]
        silu = og / (1.0 + jnp.exp(-og))
        gh = (_rms(o, gnorm_ref[...]) * silu).astype(BF16)
        h1 = h1 + _dot(gh, wout_ref[a_width + h * GLA_DV:a_width + (h + 1) * GLA_DV, :])
    h1_ref[...] = h1
    hn = _rms(h1, nffn_ref[...])
    hn_ref[...] = hn.astype(BF16)

    hn_hi = hn.astype(BF16)
    hn_lo = (hn - hn_hi.astype(F32)).astype(BF16)
    logits = (_dot(hn_hi, wr_hi_ref[...]) + _dot(hn_lo, wr_hi_ref[...]) + _dot(hn_hi, wr_lo_ref[...])
              + br_ref[...])
    lane = lax.broadcasted_iota(jnp.int32, logits.shape, 1).astype(F32)
    none = float(LANES)
    neg = -jnp.inf

    def lane_max(x):
        return jnp.max(x, axis=-1, keepdims=True)

    def lane_sum(x):
        return jnp.sum(x, axis=-1, keepdims=True)

    def first_lane(mask):
        return jnp.min(jnp.where(mask, lane, none), axis=-1, keepdims=True)

    is_group = lane < float(N_GROUPS)
    g_max = lane_max(jnp.where(is_group, logits, neg))
    g_exp = jnp.where(is_group, jnp.exp(logits - g_max), 0.0)
    g_prob = g_exp / lane_sum(g_exp)
    g_w = lane_max(g_prob)
    g_idx = first_lane(is_group & (g_prob == g_w))
    e_lo = float(ROUTER_EXPERT_LANE0) + float(EXPERTS_PER_GROUP) * g_idx
    sel = (lane >= e_lo) & (lane < e_lo + float(EXPERTS_PER_GROUP))
    e_max = lane_max(jnp.where(sel, logits, neg))
    e_exp = jnp.where(sel, jnp.exp(logits - e_max), 0.0)
    e_prob = e_exp / lane_sum(e_exp)
    p1 = lane_max(jnp.where(sel, e_prob, neg))
    i1 = first_lane(sel & (e_prob == p1))
    rest = sel & (lane != i1)
    p2 = lane_max(jnp.where(rest, e_prob, neg))
    i2 = first_lane(rest & (e_prob == p2))
    denom = p1 + p2
    w_within = jnp.where(lane == i1, p1 / denom, jnp.where(lane == i2, p2 / denom, 0.0))
    gate_ref[...] = g_w * w_within


def _mix(x2d, a, o_f, o_b, og, w, tm):
    t = x2d.shape[0]
    hv = GLA_HEADS * GLA_DV
    row = lambda width: pl.BlockSpec((tm, width), lambda i: (i, 0))
    return pl.pallas_call(
        _mix_body,
        grid=(t // tm,),
        in_specs=[row(D_MODEL), row(MLA_HEADS * MLA_V), row(hv), row(hv), row(hv),
                  _full_spec((1, GLA_DV)), _full_spec(w['wout'].shape), _full_spec((1, D_MODEL)),
                  _full_spec(w['wr_hi'].shape), _full_spec(w['wr_lo'].shape), _full_spec(w['br'].shape)],
        out_specs=(row(D_MODEL), row(D_MODEL), row(LANES)),
        out_shape=(jax.ShapeDtypeStruct((t, D_MODEL), F32), jax.ShapeDtypeStruct((t, D_MODEL), BF16),
                   jax.ShapeDtypeStruct((t, LANES), F32)),
        compiler_params=_cparams("parallel"),
        name="mix_router",
    )(x2d, a, o_f, o_b, og, w['gla_norm'], w['wout'], w['norm_ffn'], w['wr_hi'], w['wr_lo'], w['br'])


def _moe_body(hn_ref, gate_ref, h1_ref, wgu_ref, wd_ref, nfin_ref, out_ref, acc_ref):
    e = pl.program_id(1)

    @pl.when(e == 0)
    def _():
        acc_ref[...] = jnp.zeros_like(acc_ref)

    h = _dot(hn_ref[...], wgu_ref[...])
    gate = gate_ref[...]
    lane = lax.broadcasted_iota(jnp.int32, gate.shape, 1)
    g_col = jnp.sum(jnp.where(lane == e + ROUTER_EXPERT_LANE0, gate, 0.0), axis=-1, keepdims=True)
    hg = h[:, :D_EXPERT]
    hu = h[:, D_EXPERT:]
    act = (hg / (1.0 + jnp.exp(-hg))) * hu * g_col
    acc_ref[...] += _dot(act.astype(BF16), wd_ref[...])

    @pl.when(e == N_EXPERTS - 1)
    def _():
        out_ref[...] = _rms(h1_ref[...] + acc_ref[...], nfin_ref[...])


def _moe(hn, gate, h1, w, tm):
    t = hn.shape[0]
    row = lambda width: pl.BlockSpec((tm, width), lambda i, e: (i, 0))
    return pl.pallas_call(
        _moe_body,
        grid=(t // tm, N_EXPERTS),
        in_specs=[row(D_MODEL), row(LANES), row(D_MODEL),
                  pl.BlockSpec((None, D_MODEL, 2 * D_EXPERT), lambda i, e: (e, 0, 0)),
                  pl.BlockSpec((None, D_EXPERT, D_MODEL), lambda i, e: (e, 0, 0)),
                  _full_spec((1, D_MODEL))],
        out_specs=row(D_MODEL),
        out_shape=jax.ShapeDtypeStruct((t, D_MODEL), F32),
        scratch_shapes=[pltpu.VMEM((tm, D_MODEL), F32)],
        compiler_params=_cparams("parallel", "arbitrary"),
        name="moe_experts",
    )(hn, gate, h1, w['wgu'], w['wd'], w['norm_final'])


def _rope_tables(positions):
    inv_freq = 1.0 / (ROPE_THETA ** (jnp.arange(0, MLA_ROPE, 2, dtype=F32) / MLA_ROPE))
    ang = positions.astype(F32)[:, None] * inv_freq[None, :]
    cos, sin = jnp.cos(ang), jnp.sin(ang)
    reps = LANES // MLA_ROPE
    return jnp.tile(jnp.concatenate([cos, cos], axis=-1), (1, reps)), jnp.tile(jnp.concatenate([-sin, sin], axis=-1), (1, reps))


def _pack_weights(norm_mix, w_in, q_a_norm, w_uq, kv_a_norm, w_ukv, w_gate_fwd, b_gate_fwd, w_gate_bwd, b_gate_bwd,
                  gla_norm, w_out, norm_ffn, w_router_group, b_router_group, w_router_expert, b_router_expert,
                  w_expert_gate, w_expert_up, w_expert_down, norm_final):
    l = 0
    hk = GLA_HEADS * GLA_DK
    hv = GLA_HEADS * GLA_DV
    c_q, c_kv, k_pe, gq, gk, gv, lr_f, lr_b, og = jnp.split(
        w_in[l], np.cumsum([MLA_Q_RANK, MLA_KV_RANK, MLA_ROPE, hk, hk, hv, GLA_GATE_RANK, GLA_GATE_RANK])[:].tolist(),
        axis=-1)
    lr_pad = jnp.zeros((D_MODEL, LANES - 2 * GLA_GATE_RANK), F32)
    win = jnp.concatenate([c_q, c_kv, k_pe, k_pe, gq, gk, gv, og, lr_f, lr_b, lr_pad], axis=-1).astype(BF16)
    wuq = w_uq[l].reshape(MLA_Q_RANK, MLA_HEADS, MLA_NOPE + MLA_ROPE)
    wuq = jnp.concatenate([wuq[:, :, :MLA_NOPE].reshape(MLA_Q_RANK, -1), wuq[:, :, MLA_NOPE:].reshape(MLA_Q_RANK, -1)],
                          axis=-1).astype(BF16)
    wgate = jnp.zeros((LANES, 2 * hk), F32)
    wgate = wgate.at[0:GLA_GATE_RANK, 0:hk].set(w_gate_fwd[l])
    wgate = wgate.at[GLA_GATE_RANK:2 * GLA_GATE_RANK, hk:].set(w_gate_bwd[l])
    wr = jnp.zeros((D_MODEL, LANES), F32)
    wr = wr.at[:, ROUTER_GROUP_LANE0:ROUTER_GROUP_LANE0 + N_GROUPS].set(w_router_group[l])
    wr = wr.at[:, ROUTER_EXPERT_LANE0:ROUTER_EXPERT_LANE0 + N_EXPERTS].set(w_router_expert[l])
    wr_hi = wr.astype(BF16)
    br = jnp.zeros((1, LANES), F32)
    br = br.at[0, ROUTER_GROUP_LANE0:ROUTER_GROUP_LANE0 + N_GROUPS].set(b_router_group[l])
    br = br.at[0, ROUTER_EXPERT_LANE0:ROUTER_EXPERT_LANE0 + N_EXPERTS].set(b_router_expert[l])
    wgu = jnp.concatenate([w_expert_gate[l], w_expert_up[l]], axis=-1).reshape(N_EXPERTS, D_MODEL, 2 * D_EXPERT)
    return {
        'norm_mix': norm_mix[l][None], 'win': win, 'q_a_norm': q_a_norm[l][None], 'wuq': wuq,
        'kv_a_norm': kv_a_norm[l][None], 'wukv': w_ukv[l].astype(BF16),
        'wgate': wgate.astype(BF16), 'bgate': jnp.concatenate([b_gate_fwd[l], b_gate_bwd[l]])[None],
        'gla_norm': gla_norm[l][None], 'wout': w_out[l].astype(BF16), 'norm_ffn': norm_ffn[l][None],
        'wr_hi': wr_hi, 'wr_lo': (wr - wr_hi.astype(F32)).astype(BF16), 'br': br,
        'wgu': wgu.astype(BF16), 'wd': w_expert_down[l].reshape(N_EXPERTS, D_EXPERT, D_MODEL).astype(BF16),
        'norm_final': norm_final[None],
    }


def _meta_streams(meta_tokens, w):
    cos, sin = _rope_tables(jnp.arange(N_META))
    _, k, v, _, gk, gv, gf, _, _ = _inproj(meta_tokens, cos, sin, w, N_META)
    pad_keys = ((0, 0), (0, LANES - N_META), (0, 0))
    front = ((GLA_CHUNK - N_META, 0), (0, 0))
    return (jnp.pad(k, pad_keys), jnp.pad(v, pad_keys), jnp.pad(gk, front), jnp.pad(gv, front), jnp.pad(gf, front))


def _token_mixers(x, meta, w, tm, tq, tk, tb):
    bsz, seq, _ = x.shape
    km, vm, mk, mv, mg = meta
    x2d = x.reshape(bsz * seq, D_MODEL)
    cos, sin = _rope_tables(N_META + jnp.arange(seq))
    q, k, v, gq, gk, gv, gf, gb, og = _inproj(x2d, cos, sin, w, tm)
    a = _attention(q, k, v, km, vm, bsz, seq, tq, tk)
    o_f, o_b = _gla(gq, gk, gv, gf, gb, mk, mv, mg, bsz, seq, tb)
    return _mix(x2d, a, o_f, o_b, og, w, tm)


def kernel(x_prompt, x_sample, meta_tokens, norm_mix, w_in, q_a_norm, w_uq, kv_a_norm, w_ukv, w_gate_fwd, b_gate_fwd, w_gate_bwd, b_gate_bwd, gla_norm, w_out, norm_ffn, w_router_group, b_router_group, w_router_expert, b_router_expert, w_expert_gate, w_expert_up, w_expert_down, norm_final):
    w = _pack_weights(norm_mix, w_in, q_a_norm, w_uq, kv_a_norm, w_ukv, w_gate_fwd, b_gate_fwd, w_gate_bwd,
                      b_gate_bwd, gla_norm, w_out, norm_ffn, w_router_group, b_router_group, w_router_expert,
                      b_router_expert, w_expert_gate, w_expert_up, w_expert_down, norm_final)
    meta = _meta_streams(meta_tokens, w)
    outs = []
    for x in (x_prompt, x_sample):
        h1, hn, gate = _token_mixers(x, meta, w, tm=512, tq=512, tk=512, tb=512)
        outs.append(_moe(hn, gate, h1, w, tm=1024).reshape(x.shape))
    return tuple(outs)
```

```python
import functools

import numpy as np
import jax
import jax.numpy as jnp
from jax import lax
from jax.experimental import pallas as pl
from jax.experimental.pallas import tpu as pltpu

F32 = jnp.float32
BF16 = jnp.bfloat16

D_MODEL = 1024
N_META = 16
MLA_HEADS = 4
MLA_Q_RANK = 384
MLA_KV_RANK = 256
MLA_NOPE = 128
MLA_ROPE = 64
MLA_V = 128
ROPE_THETA = 10000.0
GLA_HEADS = 4
GLA_DK = 64
GLA_DV = 128
GLA_GATE_RANK = 16
GLA_TAU = 16.0
GLA_CHUNK = 64
N_GROUPS = 4
EXPERTS_PER_GROUP = 8
N_EXPERTS = N_GROUPS * EXPERTS_PER_GROUP
D_EXPERT = 256
EPS = 1e-6

LANES = 128
V7X_VMEM_BYTES = 64 * 1024 * 1024
VMEM_LIMIT = V7X_VMEM_BYTES * 7 // 8

ATTN_SCALE = (MLA_NOPE + MLA_ROPE) ** -0.5 * float(np.log2(np.e))
QK_WIDTH = 2 * LANES
V_WIDTH = 2 * LANES
ATTN_GROUP = 4

C_CQ = 0
C_CKV = C_CQ + MLA_Q_RANK
C_KPE = C_CKV + MLA_KV_RANK
C_GQ = C_KPE + LANES
C_GK = C_GQ + GLA_HEADS * GLA_DK
C_GV = C_GK + GLA_HEADS * GLA_DK
C_OG = C_GV + GLA_HEADS * GLA_DV
C_LR = C_OG + GLA_HEADS * GLA_DV
D_IN_PACKED = C_LR + LANES

ROUTER_GROUP_LANE0 = 0
ROUTER_EXPERT_LANE0 = N_GROUPS


def _cparams(*semantics):
    return pltpu.CompilerParams(dimension_semantics=semantics, vmem_limit_bytes=VMEM_LIMIT)


def _rms(x, g):
    return x * lax.rsqrt(jnp.mean(x * x, axis=-1, keepdims=True) + EPS) * g


def _dot(a, b):
    return jnp.dot(a, b, preferred_element_type=F32)


def _dot_nt(a, b):
    return lax.dot_general(a, b, (((1,), (1,)), ((), ())), preferred_element_type=F32)


def _dot_tn(a, b):
    return lax.dot_general(a, b, (((0,), (0,)), ((), ())), preferred_element_type=F32)


def _full_spec(shape):
    return pl.BlockSpec(shape, lambda *_: (0,) * len(shape))


def _rope_pairs(x, cos, sin_signed, first_half):
    swapped = jnp.where(first_half, pltpu.roll(x, LANES - MLA_ROPE // 2, 1), pltpu.roll(x, MLA_ROPE // 2, 1))
    return x * cos + swapped * sin_signed


def _inproj_body(x_ref, cos_ref, sin_ref, nmix_ref, win_ref, qan_ref, wuq_ref, kvan_ref, wukv_ref,
                 wgate_ref, bgate_ref,
                 q_ref, k_ref, v_ref, gq_ref, gk_ref, gv_ref, gf_ref, gb_ref, og_ref):
    hn = _rms(x_ref[...], nmix_ref[...]).astype(BF16)

    def proj(lo, hi):
        return _dot(hn, win_ref[:, lo:hi])

    cos = cos_ref[...]
    sin = sin_ref[...]
    lane = lax.broadcasted_iota(jnp.int32, cos.shape, 1)
    first_half = (lane & (MLA_ROPE - 1)) < MLA_ROPE // 2
    low_lanes = lane < MLA_ROPE

    cq = _rms(proj(C_CQ, C_CKV), qan_ref[...]).astype(BF16)
    qn = _dot(cq, wuq_ref[:, 0:MLA_HEADS * MLA_NOPE]) * ATTN_SCALE
    qr = _dot(cq, wuq_ref[:, MLA_HEADS * MLA_NOPE:])
    for j in range(MLA_HEADS // 2):
        rj = (_rope_pairs(qr[:, j * LANES:(j + 1) * LANES], cos, sin, first_half) * ATTN_SCALE).astype(BF16)
        for h in (2 * j, 2 * j + 1):
            q_ref[h, :, 0:LANES] = qn[:, h * LANES:(h + 1) * LANES].astype(BF16)
            q_ref[h, :, LANES:QK_WIDTH] = rj

    ckv = _rms(proj(C_CKV, C_KPE), kvan_ref[...]).astype(BF16)
    kv = _dot(ckv, wukv_ref[...])
    kr = _rope_pairs(proj(C_KPE, C_GQ), cos, sin, first_half)
    kr_even = jnp.where(low_lanes, kr, 0.0).astype(BF16)
    kr_odd = jnp.where(low_lanes, 0.0, kr).astype(BF16)
    for h in range(MLA_HEADS):
        base = h * (MLA_NOPE + MLA_V)
        k_ref[h, :, 0:LANES] = kv[:, base:base + MLA_NOPE].astype(BF16)
        k_ref[h, :, LANES:QK_WIDTH] = kr_even if h % 2 == 0 else kr_odd
        v_ref[h, :, 0:MLA_V] = kv[:, base + MLA_NOPE:base + MLA_NOPE + MLA_V].astype(BF16)
        v_ref[h, :, MLA_V:V_WIDTH] = jnp.ones((kv.shape[0], V_WIDTH - MLA_V), BF16)

    gq_ref[...] = proj(C_GQ, C_GK) * (GLA_DK ** -0.5)
    gk_ref[...] = proj(C_GK, C_GV)
    gv_ref[...] = proj(C_GV, C_OG)
    og_ref[...] = proj(C_OG, C_LR)
    pre = _dot(proj(C_LR, D_IN_PACKED).astype(BF16), wgate_ref[...]) + bgate_ref[...]
    logsig = jnp.minimum(pre, 0.0) - jnp.log1p(jnp.exp(-jnp.abs(pre)))
    gates = logsig * (1.0 / GLA_TAU)
    gf_ref[...] = gates[:, 0:GLA_HEADS * GLA_DK]
    gb_ref[...] = gates[:, GLA_HEADS * GLA_DK:]


def _inproj(x2d, cos, sin, w, tm):
    t = x2d.shape[0]
    blocks_per_seq = cos.shape[0] // tm
    hk = GLA_HEADS * GLA_DK
    hv = GLA_HEADS * GLA_DV
    row = lambda width: pl.BlockSpec((tm, width), lambda i: (i, 0))
    head_rows = lambda width: pl.BlockSpec((MLA_HEADS, tm, width), lambda i: (0, i, 0))
    tab = pl.BlockSpec((tm, LANES), lambda i: (i % blocks_per_seq, 0))
    out_shape = (
        jax.ShapeDtypeStruct((MLA_HEADS, t, QK_WIDTH), BF16),
        jax.ShapeDtypeStruct((MLA_HEADS, t, QK_WIDTH), BF16),
        jax.ShapeDtypeStruct((MLA_HEADS, t, V_WIDTH), BF16),
        jax.ShapeDtypeStruct((t, hk), F32),
        jax.ShapeDtypeStruct((t, hk), F32),
        jax.ShapeDtypeStruct((t, hv), F32),
        jax.ShapeDtypeStruct((t, hk), F32),
        jax.ShapeDtypeStruct((t, hk), F32),
        jax.ShapeDtypeStruct((t, hv), F32),
    )
    return pl.pallas_call(
        _inproj_body,
        grid=(t // tm,),
        in_specs=[row(D_MODEL), tab, tab,
                  _full_spec((1, D_MODEL)), _full_spec((D_MODEL, D_IN_PACKED)),
                  _full_spec((1, MLA_Q_RANK)), _full_spec(w['wuq'].shape),
                  _full_spec((1, MLA_KV_RANK)), _full_spec(w['wukv'].shape),
                  _full_spec(w['wgate'].shape), _full_spec(w['bgate'].shape)],
        out_specs=(head_rows(QK_WIDTH), head_rows(QK_WIDTH), head_rows(V_WIDTH),
                   row(hk), row(hk), row(hv), row(hk), row(hk), row(hv)),
        out_shape=out_shape,
        compiler_params=_cparams("parallel"),
        name="inproj",
    )(x2d, cos, sin, w['norm_mix'], w['win'], w['q_a_norm'], w['wuq'], w['kv_a_norm'], w['wukv'],
      w['wgate'], w['bgate'])


def _attn_body(q_ref, k_ref, v_ref, km_ref, vm_ref, o_ref, s_ref, acc_ref, *, tk):
    q = q_ref[...]
    n_groups = k_ref.shape[0] // (ATTN_GROUP * tk)

    def scores(j):
        return _dot_nt(q, k_ref[pl.ds(pl.multiple_of(j * tk, tk), tk), :])

    def values(j):
        return v_ref[pl.ds(pl.multiple_of(j * tk, tk), tk), :]

    def absorb(m, s, v):
        m_new = jnp.maximum(m, jnp.max(s, axis=-1, keepdims=True))
        p = jnp.exp2(s - m_new)
        acc_ref[...] = jnp.exp2(m - m_new) * acc_ref[...] + _dot(p.astype(BF16), v)
        return m_new

    s_ref[0] = scores(0)
    sm = _dot_nt(q, km_ref[...])
    sm = jnp.where(lax.broadcasted_iota(jnp.int32, sm.shape, 1) < N_META, sm, -jnp.inf)
    m = jnp.max(sm, axis=-1, keepdims=True)
    acc_ref[...] = _dot(jnp.exp2(sm - m).astype(BF16), vm_ref[...])

    def group(g, m, last):
        for i in range(ATTN_GROUP):
            j = ATTN_GROUP * g + i
            s = s_ref[i % 2]
            if not (last and i == ATTN_GROUP - 1):
                s_ref[(i + 1) % 2] = scores(j + 1)
            m = absorb(m, s, values(j))
        return m

    m = lax.fori_loop(0, n_groups - 1, lambda g, m: group(g, m, False), m)
    group(n_groups - 1, m, True)
    acc = acc_ref[...]
    o_ref[...] = (acc[:, :MLA_V] / acc[:, MLA_V:]).astype(o_ref.dtype)


def _attention(q, k, v, km, vm, bsz, seq, tq, tk):
    nq = seq // tq
    return pl.pallas_call(
        functools.partial(_attn_body, tk=tk),
        grid=(bsz, MLA_HEADS, nq),
        in_specs=[pl.BlockSpec((None, tq, QK_WIDTH), lambda b, h, i: (h, b * nq + i, 0)),
                  pl.BlockSpec((None, seq, QK_WIDTH), lambda b, h, i: (h, b, 0)),
                  pl.BlockSpec((None, seq, V_WIDTH), lambda b, h, i: (h, b, 0)),
                  pl.BlockSpec((None, LANES, QK_WIDTH), lambda b, h, i: (h, 0, 0)),
                  pl.BlockSpec((None, LANES, V_WIDTH), lambda b, h, i: (h, 0, 0))],
        out_specs=pl.BlockSpec((tq, MLA_V), lambda b, h, i: (b * nq + i, h)),
        out_shape=jax.ShapeDtypeStruct((bsz * seq, MLA_HEADS * MLA_V), BF16),
        scratch_shapes=[pltpu.VMEM((2, tq, tk), F32), pltpu.VMEM((tq, V_WIDTH), F32)],
        compiler_params=_cparams("parallel", "parallel", "arbitrary"),
        name="mla_attention",
    )(q, k, v, km, vm)


def _split3(x):
    hi = x.astype(BF16)
    r1 = x - hi.astype(F32)
    mid = r1.astype(BF16)
    lo = (r1 - mid.astype(F32)).astype(BF16)
    return hi, mid, lo


def _gla_chunk(q, k, g, v_even, v_odd, state, tri, keep, mid, last):
    lane = lax.broadcasted_iota(jnp.int32, (GLA_CHUNK, LANES), 1)
    even = lane < GLA_DK
    g_hi, g_mid, g_lo = _split3(g)
    b = _dot(tri, g_hi) + _dot(tri, g_mid) + _dot(tri, g_lo)
    b_last = b[last:last + 1, :]
    ke = (k * jnp.exp(b_last - b)).astype(BF16)
    upd = jnp.where(lax.broadcasted_iota(jnp.int32, (GLA_DV, LANES), 1) < GLA_DK,
                    _dot_tn(v_even, ke), _dot_tn(v_odd, ke))
    new_state = state * jnp.exp(b_last) + upd
    if q is None:
        return new_state, None, None
    b_mid = b[mid:mid + 1, :]
    qs = q * jnp.exp(b - b_mid)
    ks = (k * jnp.exp(b_mid - b)).astype(BF16)
    qs2 = jnp.concatenate([jnp.where(even, qs, 0.0), jnp.where(even, 0.0, qs)], axis=0).astype(BF16)
    scores = jnp.where(keep, _dot_nt(qs2, ks), 0.0).astype(BF16)
    qe = q * jnp.exp(b)
    qe2 = jnp.concatenate([jnp.where(even, qe, 0.0), jnp.where(even, 0.0, qe)], axis=0).astype(BF16)
    inter = _dot_nt(qe2, state.astype(BF16))
    o_even = _dot(scores[:GLA_CHUNK], v_even) + inter[:GLA_CHUNK]
    o_odd = _dot(scores[GLA_CHUNK:], v_odd) + inter[GLA_CHUNK:]
    return new_state, o_even, o_odd


def _gla_body(qf_ref, kf_ref, vf_ref, gf_ref, qb_ref, kb_ref, vb_ref, gb_ref, mk_ref, mv_ref, mg_ref,
              of_ref, ob_ref, state_ref):
    n_chunks = qf_ref.shape[0] // GLA_CHUNK
    n_pairs = GLA_HEADS // 2
    r = lax.broadcasted_iota(jnp.int32, (GLA_CHUNK, GLA_CHUNK), 0)
    c = lax.broadcasted_iota(jnp.int32, (GLA_CHUNK, GLA_CHUNK), 1)
    tri_f = jnp.where(c <= r, 1.0, 0.0).astype(BF16)
    tri_b = jnp.where(c >= r, 1.0, 0.0).astype(BF16)
    r2 = lax.broadcasted_iota(jnp.int32, (2 * GLA_CHUNK, GLA_CHUNK), 0) & (GLA_CHUNK - 1)
    c2 = lax.broadcasted_iota(jnp.int32, (2 * GLA_CHUNK, GLA_CHUNK), 1)
    keep_f = c2 <= r2
    keep_b = c2 >= r2
    mid_f, last_f = GLA_CHUNK // 2 - 1, GLA_CHUNK - 1
    mid_b, last_b = GLA_CHUNK // 2, 0

    def pair_cols(p):
        return slice(p * LANES, (p + 1) * LANES)

    def head_cols(h):
        return slice(h * GLA_DV, (h + 1) * GLA_DV)

    @pl.when(pl.program_id(1) == 0)
    def _():
        for p in range(n_pairs):
            zero = jnp.zeros((GLA_DV, LANES), F32)
            st, _, _ = _gla_chunk(None, mk_ref[:, pair_cols(p)], mg_ref[:, pair_cols(p)],
                                  mv_ref[:, head_cols(2 * p)].astype(BF16),
                                  mv_ref[:, head_cols(2 * p + 1)].astype(BF16),
                                  zero, tri_f, keep_f, mid_f, last_f)
            state_ref[p] = st
            state_ref[n_pairs + p] = zero

    def step(i, _):
        rows_f = pl.ds(pl.multiple_of(i * GLA_CHUNK, GLA_CHUNK), GLA_CHUNK)
        rows_b = pl.ds(pl.multiple_of((n_chunks - 1 - i) * GLA_CHUNK, GLA_CHUNK), GLA_CHUNK)
        for p in range(n_pairs):
            for (rows, q_ref, k_ref, v_ref, g_ref, o_ref, slot, tri, keep, mid, last) in (
                    (rows_f, qf_ref, kf_ref, vf_ref, gf_ref, of_ref, p, tri_f, keep_f, mid_f, last_f),
                    (rows_b, qb_ref, kb_ref, vb_ref, gb_ref, ob_ref, n_pairs + p, tri_b, keep_b, mid_b, last_b)):
                st, o_even, o_odd = _gla_chunk(
                    q_ref[rows, pair_cols(p)], k_ref[rows, pair_cols(p)], g_ref[rows, pair_cols(p)],
                    v_ref[rows, head_cols(2 * p)].astype(BF16), v_ref[rows, head_cols(2 * p + 1)].astype(BF16),
                    state_ref[slot], tri, keep, mid, last)
                state_ref[slot] = st
                o_ref[rows, head_cols(2 * p)] = o_even
                o_ref[rows, head_cols(2 * p + 1)] = o_odd
        return 0

    lax.fori_loop(0, n_chunks, step, 0)


def _gla(gq, gk, gv, gf, gb, mk, mv, mg, bsz, seq, tb):
    nb = seq // tb
    hk = GLA_HEADS * GLA_DK
    hv = GLA_HEADS * GLA_DV
    fwd = lambda width: pl.BlockSpec((tb, width), lambda b, j: (b * nb + j, 0))
    bwd = lambda width: pl.BlockSpec((tb, width), lambda b, j: (b * nb + nb - 1 - j, 0))
    t = bsz * seq
    return pl.pallas_call(
        _gla_body,
        grid=(bsz, nb),
        in_specs=[fwd(hk), fwd(hk), fwd(hv), fwd(hk), bwd(hk), bwd(hk), bwd(hv), bwd(hk),
                  _full_spec(mk.shape), _full_spec(mv.shape), _full_spec(mg.shape)],
        out_specs=(fwd(hv), bwd(hv)),
        out_shape=(jax.ShapeDtypeStruct((t, hv), F32), jax.ShapeDtypeStruct((t, hv), F32)),
        scratch_shapes=[pltpu.VMEM((2 * (GLA_HEADS // 2), GLA_DV, LANES), F32)],
        compiler_params=_cparams("parallel", "arbitrary"),
        name="gla_scan",
    )(gq, gk, gv, gf, gq, gk, gv, gb, mk, mv, mg)


def _mix_body(x_ref, a_ref, of_ref, ob_ref, og_ref, gnorm_ref, wout_ref, nffn_ref, wr_hi_ref, wr_lo_ref, br_ref,
              h1_ref, hn_ref, route_ref, count_ref, tri_ref):
    tm = x_ref.shape[0]

    @pl.when(pl.program_id(0) == 0)
    def _():
        r = lax.broadcasted_iota(jnp.int32, (tm, tm), 0)
        c = lax.broadcasted_iota(jnp.int32, (tm, tm), 1)
        tri_ref[...] = jnp.where(c < r, 1.0, 0.0).astype(BF16)
        count_ref[...] = jnp.zeros_like(count_ref)

    a_width = MLA_HEADS * MLA_V
    h1 = x_ref[...] + _dot(a_ref[...], wout_ref[0:a_width, :])
    for h in range(GLA_HEADS):
        cols = slice(h * GLA_DV, (h + 1) * GLA_DV)
        o = of_ref[:, cols] + ob_ref[:, cols]
        og = og_ref[:, cols]
        silu = og / (1.0 + jnp.exp(-og))
        gh = (_rms(o, gnorm_ref[...]) * silu).astype(BF16)
        h1 = h1 + _dot(gh, wout_ref[a_width + h * GLA_DV:a_width + (h + 1) * GLA_DV, :])
    h1_ref[...] = h1
    hn = _rms(h1, nffn_ref[...])
    hn_ref[...] = hn

    hn_hi = hn.astype(BF16)
    hn_lo = (hn - hn_hi.astype(F32)).astype(BF16)
    logits = (_dot(hn_hi, wr_hi_ref[...]) + _dot(hn_lo, wr_hi_ref[...]) + _dot(hn_hi, wr_lo_ref[...])
              + br_ref[...])
    lane = lax.broadcasted_iota(jnp.int32, logits.shape, 1).astype(F32)
    none = float(LANES)
    neg = -jnp.inf

    def lane_max(x):
        return jnp.max(x, axis=-1, keepdims=True)

    def lane_sum(x):
        return jnp.sum(x, axis=-1, keepdims=True)

    def first_lane(mask):
        return jnp.min(jnp.where(mask, lane, none), axis=-1, keepdims=True)

    is_group = lane < float(N_GROUPS)
    g_max = lane_max(jnp.where(is_group, logits, neg))
    g_exp = jnp.where(is_group, jnp.exp(logits - g_max), 0.0)
    g_prob = g_exp / lane_sum(g_exp)
    g_w = lane_max(g_prob)
    g_idx = first_lane(is_group & (g_prob == g_w))
    e_lo = float(ROUTER_EXPERT_LANE0) + float(EXPERTS_PER_GROUP) * g_idx
    sel = (lane >= e_lo) & (lane < e_lo + float(EXPERTS_PER_GROUP))
    e_max = lane_max(jnp.where(sel, logits, neg))
    e_exp = jnp.where(sel, jnp.exp(logits - e_max), 0.0)
    e_prob = e_exp / lane_sum(e_exp)
    p1 = lane_max(jnp.where(sel, e_prob, neg))
    i1 = first_lane(sel & (e_prob == p1))
    rest = sel & (lane != i1)
    p2 = lane_max(jnp.where(rest, e_prob, neg))
    i2 = first_lane(rest & (e_prob == p2))
    denom = p1 + p2
    chosen = jnp.where((lane == i1) | (lane == i2), 1.0, 0.0)
    rank = count_ref[...] + _dot(tri_ref[...], chosen.astype(BF16))
    count_ref[...] += jnp.sum(chosen, axis=0, keepdims=True)
    fields = (i1 - float(ROUTER_EXPERT_LANE0), i2 - float(ROUTER_EXPERT_LANE0),
              lane_sum(jnp.where(lane == i1, rank, 0.0)), lane_sum(jnp.where(lane == i2, rank, 0.0)),
              g_w * (p1 / denom), g_w * (p2 / denom))
    route = jnp.zeros_like(logits)
    for k, value in enumerate(fields):
        route = jnp.where(lane == float(k), value, route)
    route_ref[...] = route


ROUTE_E1, ROUTE_E2, ROUTE_RANK1, ROUTE_RANK2, ROUTE_W1, ROUTE_W2 = range(6)


def _mix(x2d, a, o_f, o_b, og, w, tm):
    t = x2d.shape[0]
    hv = GLA_HEADS * GLA_DV
    row = lambda width: pl.BlockSpec((tm, width), lambda i: (i, 0))
    return pl.pallas_call(
        _mix_body,
        grid=(t // tm,),
        in_specs=[row(D_MODEL), row(MLA_HEADS * MLA_V), row(hv), row(hv), row(hv),
                  _full_spec((1, GLA_DV)), _full_spec(w['wout'].shape), _full_spec((1, D_MODEL)),
                  _full_spec(w['wr_hi'].shape), _full_spec(w['wr_lo'].shape), _full_spec(w['br'].shape)],
        out_specs=(row(D_MODEL), row(D_MODEL), row(LANES), _full_spec((1, LANES))),
        out_shape=(jax.ShapeDtypeStruct((t, D_MODEL), F32), jax.ShapeDtypeStruct((t, D_MODEL), F32),
                   jax.ShapeDtypeStruct((t, LANES), F32), jax.ShapeDtypeStruct((1, LANES), F32)),
        scratch_shapes=[pltpu.VMEM((tm, tm), BF16)],
        compiler_params=_cparams("arbitrary"),
        name="mix_router",
    )(x2d, a, o_f, o_b, og, w['gla_norm'], w['wout'], w['norm_ffn'], w['wr_hi'], w['wr_lo'], w['br'])


EXPERT_TILE = 256
ROW_TILE = 256


def _row_copy(src_ref, src_row, dst_ref, dst_row, sem):
    return pltpu.make_async_copy(src_ref.at[pl.ds(src_row, 1)], dst_ref.at[pl.ds(dst_row, 1)], sem)


def _dispatch_body(fill_ref, n_tiles_ref, slots_ref, hn_ref, zeros_ref, xs_ref, sem, fill_sem):
    i = pl.program_id(0)
    base = i * ROW_TILE

    def wait_rows():
        for _ in range(2 * ROW_TILE):
            _row_copy(hn_ref, 0, xs_ref, 0, sem).wait()

    @pl.when(i == 0)
    def _():
        def fill(start):
            return pltpu.make_async_copy(zeros_ref, xs_ref.at[pl.ds(pl.multiple_of(start, EXPERT_TILE), EXPERT_TILE)],
                                         fill_sem)
        unused = (n_tiles_ref[0], xs_ref.shape[0] // EXPERT_TILE)
        for e in range(N_EXPERTS):
            fill(fill_ref[e]).start()
        lax.fori_loop(*unused, lambda j, c: (fill(j * EXPERT_TILE).start(), c)[1], 0)
        for e in range(N_EXPERTS):
            fill(fill_ref[e]).wait()
        lax.fori_loop(*unused, lambda j, c: (fill(j * EXPERT_TILE).wait(), c)[1], 0)

    @pl.when(i > 0)
    def _():
        wait_rows()

    def issue(r, carry):
        _row_copy(hn_ref, base + r, xs_ref, slots_ref[0, r], sem).start()
        _row_copy(hn_ref, base + r, xs_ref, slots_ref[0, ROW_TILE + r], sem).start()
        return carry

    lax.fori_loop(0, ROW_TILE, issue, 0, unroll=8)

    @pl.when(i == pl.num_programs(0) - 1)
    def _():
        wait_rows()


def _dispatch(fill_starts, n_tiles, slots, hn, n_rows):
    t = hn.shape[0]
    zeros = jnp.zeros((EXPERT_TILE, D_MODEL), F32)
    return pl.pallas_call(
        _dispatch_body,
        grid_spec=pltpu.PrefetchScalarGridSpec(
            num_scalar_prefetch=2,
            grid=(t // ROW_TILE,),
            in_specs=[pl.BlockSpec((None, 1, 2 * ROW_TILE), lambda i, fill, nt: (i, 0, 0), memory_space=pltpu.SMEM),
                      pl.BlockSpec(memory_space=pl.ANY), pl.BlockSpec(memory_space=pl.ANY)],
            out_specs=pl.BlockSpec(memory_space=pl.ANY),
            scratch_shapes=[pltpu.SemaphoreType.DMA(()), pltpu.SemaphoreType.DMA(())]),
        out_shape=jax.ShapeDtypeStruct((n_rows, D_MODEL), F32),
        compiler_params=_cparams("arbitrary"),
        name="moe_dispatch",
    )(fill_starts, n_tiles, slots, hn, zeros)


def _expert_body(tile_expert_ref, n_tiles_ref, xs_ref, wgu_ref, wd_ref, ys_ref):
    used = pl.program_id(0) < n_tiles_ref[0]

    @pl.when(used)
    def _():
        h = _dot(xs_ref[...].astype(BF16), wgu_ref[...])
        hg = h[:, :D_EXPERT]
        act = (hg / (1.0 + jnp.exp(-hg))) * h[:, D_EXPERT:]
        ys_ref[...] = _dot(act.astype(BF16), wd_ref[...])

    @pl.when(jnp.logical_not(used))
    def _():
        ys_ref[...] = jnp.zeros_like(ys_ref)


def _experts(tile_expert, n_tiles, xs, w):
    n_rows = xs.shape[0]
    grid = n_rows // EXPERT_TILE

    def tile(i, tile_expert, n_tiles):
        return jnp.minimum(i, n_tiles[0] - 1)

    return pl.pallas_call(
        _expert_body,
        grid_spec=pltpu.PrefetchScalarGridSpec(
            num_scalar_prefetch=2,
            grid=(grid,),
            in_specs=[pl.BlockSpec((EXPERT_TILE, D_MODEL), lambda i, te, nt: (tile(i, te, nt), 0)),
                      pl.BlockSpec((None, D_MODEL, 2 * D_EXPERT), lambda i, te, nt: (te[tile(i, te, nt)], 0, 0)),
                      pl.BlockSpec((None, D_EXPERT, D_MODEL), lambda i, te, nt: (te[tile(i, te, nt)], 0, 0))],
            out_specs=pl.BlockSpec((EXPERT_TILE, D_MODEL), lambda i, te, nt: (i, 0))),
        out_shape=jax.ShapeDtypeStruct((n_rows, D_MODEL), F32),
        compiler_params=_cparams("arbitrary"),
        name="moe_experts",
    )(tile_expert, n_tiles, xs, w['wgu'], w['wd'])


def _combine_body(slots_ref, next_slots_ref, h1_ref, route_ref, nfin_ref, ys_ref, out_ref, buf_ref, sem):
    i = pl.program_id(0)
    cur = i % 2

    def issue(slots, half):
        def body(r, carry):
            _row_copy(ys_ref, slots[0, r], buf_ref.at[half, 0], r, sem.at[half]).start()
            _row_copy(ys_ref, slots[0, ROW_TILE + r], buf_ref.at[half, 1], r, sem.at[half]).start()
            return carry
        lax.fori_loop(0, ROW_TILE, body, 0, unroll=8)

    @pl.when(i == 0)
    def _():
        issue(slots_ref, 0)

    @pl.when(i + 1 < pl.num_programs(0))
    def _():
        issue(next_slots_ref, 1 - cur)

    for _ in range(2 * ROW_TILE):
        _row_copy(ys_ref, 0, buf_ref.at[cur, 0], 0, sem.at[cur]).wait()
    route = route_ref[...]
    lane = lax.broadcasted_iota(jnp.int32, route.shape, 1)
    w1 = jnp.sum(jnp.where(lane == ROUTE_W1, route, 0.0), axis=-1, keepdims=True)
    w2 = jnp.sum(jnp.where(lane == ROUTE_W2, route, 0.0), axis=-1, keepdims=True)
    out_ref[...] = _rms(h1_ref[...] + w1 * buf_ref[cur, 0] + w2 * buf_ref[cur, 1], nfin_ref[...])


def _combine(slots, h1, route, ys, w):
    t = h1.shape[0]
    n_steps = t // ROW_TILE
    row = lambda width: pl.BlockSpec((ROW_TILE, width), lambda i: (i, 0))
    slot_block = lambda index: pl.BlockSpec((None, 1, 2 * ROW_TILE), index, memory_space=pltpu.SMEM)
    return pl.pallas_call(
        _combine_body,
        grid=(n_steps,),
        in_specs=[slot_block(lambda i: (i, 0, 0)), slot_block(lambda i: (jnp.minimum(i + 1, n_steps - 1), 0, 0)),
                  row(D_MODEL), row(LANES), _full_spec((1, D_MODEL)), pl.BlockSpec(memory_space=pl.ANY)],
        out_specs=row(D_MODEL),
        out_shape=jax.ShapeDtypeStruct((t, D_MODEL), F32),
        scratch_shapes=[pltpu.VMEM((2, 2, ROW_TILE, D_MODEL), F32), pltpu.SemaphoreType.DMA((2,))],
        compiler_params=_cparams("arbitrary"),
        name="moe_combine",
    )(slots, slots, h1, route, w['norm_final'], ys)


def _moe(hn, route, counts, h1, w):
    t = hn.shape[0]
    n_rows = 2 * t + N_EXPERTS * EXPERT_TILE
    n_grid_tiles = n_rows // EXPERT_TILE
    count = counts[0, ROUTER_EXPERT_LANE0:ROUTER_EXPERT_LANE0 + N_EXPERTS].astype(jnp.int32)
    padded = jnp.maximum((count + EXPERT_TILE - 1) // EXPERT_TILE, 1) * EXPERT_TILE
    ends = jnp.cumsum(padded)
    starts = ends - padded
    n_tiles = (ends[-1] // EXPERT_TILE).reshape(1)
    tile_expert = jnp.minimum(jnp.searchsorted(ends, jnp.arange(n_grid_tiles) * EXPERT_TILE, side='right'),
                              N_EXPERTS - 1).astype(jnp.int32)
    fill_starts = (ends - EXPERT_TILE).astype(jnp.int32)
    field = lambda k: route[:, k].astype(jnp.int32)
    slot1 = starts[field(ROUTE_E1)] + field(ROUTE_RANK1)
    slot2 = starts[field(ROUTE_E2)] + field(ROUTE_RANK2)
    slots = jnp.concatenate([slot1.reshape(-1, 1, ROW_TILE), slot2.reshape(-1, 1, ROW_TILE)], axis=-1)
    xs = _dispatch(fill_starts, n_tiles, slots, hn, n_rows)
    ys = _experts(tile_expert, n_tiles, xs, w)
    return _combine(slots, h1, route, ys, w)


def _rope_tables(positions):
    inv_freq = 1.0 / (ROPE_THETA ** (jnp.arange(0, MLA_ROPE, 2, dtype=F32) / MLA_ROPE))
    ang = positions.astype(F32)[:, None] * inv_freq[None, :]
    cos, sin = jnp.cos(ang), jnp.sin(ang)
    reps = LANES // MLA_ROPE
    return jnp.tile(jnp.concatenate([cos, cos], axis=-1), (1, reps)), jnp.tile(jnp.concatenate([-sin, sin], axis=-1), (1, reps))


def _pack_weights(norm_mix, w_in, q_a_norm, w_uq, kv_a_norm, w_ukv, w_gate_fwd, b_gate_fwd, w_gate_bwd, b_gate_bwd,
                  gla_norm, w_out, norm_ffn, w_router_group, b_router_group, w_router_expert, b_router_expert,
                  w_expert_gate, w_expert_up, w_expert_down, norm_final):
    l = 0
    hk = GLA_HEADS * GLA_DK
    hv = GLA_HEADS * GLA_DV
    c_q, c_kv, k_pe, gq, gk, gv, lr_f, lr_b, og = jnp.split(
        w_in[l], np.cumsum([MLA_Q_RANK, MLA_KV_RANK, MLA_ROPE, hk, hk, hv, GLA_GATE_RANK, GLA_GATE_RANK])[:].tolist(),
        axis=-1)
    lr_pad = jnp.zeros((D_MODEL, LANES - 2 * GLA_GATE_RANK), F32)
    win = jnp.concatenate([c_q, c_kv, k_pe, k_pe, gq, gk, gv, og, lr_f, lr_b, lr_pad], axis=-1).astype(BF16)
    wuq = w_uq[l].reshape(MLA_Q_RANK, MLA_HEADS, MLA_NOPE + MLA_ROPE)
    wuq = jnp.concatenate([wuq[:, :, :MLA_NOPE].reshape(MLA_Q_RANK, -1), wuq[:, :, MLA_NOPE:].reshape(MLA_Q_RANK, -1)],
                          axis=-1).astype(BF16)
    wgate = jnp.zeros((LANES, 2 * hk), F32)
    wgate = wgate.at[0:GLA_GATE_RANK, 0:hk].set(w_gate_fwd[l])
    wgate = wgate.at[GLA_GATE_RANK:2 * GLA_GATE_RANK, hk:].set(w_gate_bwd[l])
    wr = jnp.zeros((D_MODEL, LANES), F32)
    wr = wr.at[:, ROUTER_GROUP_LANE0:ROUTER_GROUP_LANE0 + N_GROUPS].set(w_router_group[l])
    wr = wr.at[:, ROUTER_EXPERT_LANE0:ROUTER_EXPERT_LANE0 + N_EXPERTS].set(w_router_expert[l])
    wr_hi = wr.astype(BF16)
    br = jnp.zeros((1, LANES), F32)
    br = br.at[0, ROUTER_GROUP_LANE0:ROUTER_GROUP_LANE0 + N_GROUPS].set(b_router_group[l])
    br = br.at[0, ROUTER_EXPERT_LANE0:ROUTER_EXPERT_LANE0 + N_EXPERTS].set(b_router_expert[l])
    wgu = jnp.concatenate([w_expert_gate[l], w_expert_up[l]], axis=-1).reshape(N_EXPERTS, D_MODEL, 2 * D_EXPERT)
    return {
        'norm_mix': norm_mix[l][None], 'win': win, 'q_a_norm': q_a_norm[l][None], 'wuq': wuq,
        'kv_a_norm': kv_a_norm[l][None], 'wukv': w_ukv[l].astype(BF16),
        'wgate': wgate.astype(BF16), 'bgate': jnp.concatenate([b_gate_fwd[l], b_gate_bwd[l]])[None],
        'gla_norm': gla_norm[l][None], 'wout': w_out[l].astype(BF16), 'norm_ffn': norm_ffn[l][None],
        'wr_hi': wr_hi, 'wr_lo': (wr - wr_hi.astype(F32)).astype(BF16), 'br': br,
        'wgu': wgu.astype(BF16), 'wd': w_expert_down[l].reshape(N_EXPERTS, D_EXPERT, D_MODEL).astype(BF16),
        'norm_final': norm_final[None],
    }


def _meta_streams(meta_tokens, w):
    cos, sin = _rope_tables(jnp.arange(N_META))
    _, k, v, _, gk, gv, gf, _, _ = _inproj(meta_tokens, cos, sin, w, N_META)
    pad_keys = ((0, 0), (0, LANES - N_META), (0, 0))
    front = ((GLA_CHUNK - N_META, 0), (0, 0))
    return (jnp.pad(k, pad_keys), jnp.pad(v, pad_keys), jnp.pad(gk, front), jnp.pad(gv, front), jnp.pad(gf, front))


def _token_mixers(x, meta, w, tm, tq, tk, tb):
    bsz, seq, _ = x.shape
    km, vm, mk, mv, mg = meta
    x2d = x.reshape(bsz * seq, D_MODEL)
    cos, sin = _rope_tables(N_META + jnp.arange(seq))
    q, k, v, gq, gk, gv, gf, gb, og = _inproj(x2d, cos, sin, w, tm)
    a = _attention(q, k, v, km, vm, bsz, seq, tq, tk)
    o_f, o_b = _gla(gq, gk, gv, gf, gb, mk, mv, mg, bsz, seq, tb)
    return _mix(x2d, a, o_f, o_b, og, w, tm)


def kernel(x_prompt, x_sample, meta_tokens, norm_mix, w_in, q_a_norm, w_uq, kv_a_norm, w_ukv, w_gate_fwd, b_gate_fwd, w_gate_bwd, b_gate_bwd, gla_norm, w_out, norm_ffn, w_router_group, b_router_group, w_router_expert, b_router_expert, w_expert_gate, w_expert_up, w_expert_down, norm_final):
    w = _pack_weights(norm_mix, w_in, q_a_norm, w_uq, kv_a_norm, w_ukv, w_gate_fwd, b_gate_fwd, w_gate_bwd,
                      b_gate_bwd, gla_norm, w_out, norm_ffn, w_router_group, b_router_group, w_router_expert,
                      b_router_expert, w_expert_gate, w_expert_up, w_expert_down, norm_final)
    meta = _meta_streams(meta_tokens, w)
    outs = []
    for x in (x_prompt, x_sample):
        h1, hn, route, counts = _token_mixers(x, meta, w, tm=512, tq=512, tk=512, tb=512)
        outs.append(_moe(hn, route, counts, h1, w).reshape(x.shape))
    return tuple(outs)
```

```python
import functools

import numpy as np
import jax
import jax.numpy as jnp
from jax import lax
from jax.experimental import pallas as pl
from jax.experimental.pallas import tpu as pltpu

F32 = jnp.float32
BF16 = jnp.bfloat16

D_MODEL = 1024
N_META = 16
MLA_HEADS = 4
MLA_Q_RANK = 384
MLA_KV_RANK = 256
MLA_NOPE = 128
MLA_ROPE = 64
MLA_V = 128
ROPE_THETA = 10000.0
GLA_HEADS = 4
GLA_DK = 64
GLA_DV = 128
GLA_GATE_RANK = 16
GLA_TAU = 16.0
GLA_CHUNK = 64
N_GROUPS = 4
EXPERTS_PER_GROUP = 8
N_EXPERTS = N_GROUPS * EXPERTS_PER_GROUP
D_EXPERT = 256
EPS = 1e-6

LANES = 128
V7X_VMEM_BYTES = 64 * 1024 * 1024
VMEM_LIMIT = V7X_VMEM_BYTES * 7 // 8

ATTN_SCALE = (MLA_NOPE + MLA_ROPE) ** -0.5 * float(np.log2(np.e))
QK_WIDTH = 2 * LANES
V_WIDTH = 2 * LANES
ATTN_GROUP = 4

C_CQ = 0
C_CKV = C_CQ + MLA_Q_RANK
C_KPE = C_CKV + MLA_KV_RANK
C_GQ = C_KPE + LANES
C_GK = C_GQ + GLA_HEADS * GLA_DK
C_GV = C_GK + GLA_HEADS * GLA_DK
C_OG = C_GV + GLA_HEADS * GLA_DV
C_LR = C_OG + GLA_HEADS * GLA_DV
D_IN_PACKED = C_LR + LANES

ROUTER_GROUP_LANE0 = 0
ROUTER_EXPERT_LANE0 = N_GROUPS


def _cparams(*semantics):
    return pltpu.CompilerParams(dimension_semantics=semantics, vmem_limit_bytes=VMEM_LIMIT)


def _rms(x, g):
    return x * lax.rsqrt(jnp.mean(x * x, axis=-1, keepdims=True) + EPS) * g


def _dot(a, b):
    return jnp.dot(a, b, preferred_element_type=F32)


def _dot_nt(a, b):
    return lax.dot_general(a, b, (((1,), (1,)), ((), ())), preferred_element_type=F32)


def _dot_tn(a, b):
    return lax.dot_general(a, b, (((0,), (0,)), ((), ())), preferred_element_type=F32)


def _full_spec(shape):
    return pl.BlockSpec(shape, lambda *_: (0,) * len(shape))


def _rope_pairs(x, cos, sin_signed, first_half):
    swapped = jnp.where(first_half, pltpu.roll(x, LANES - MLA_ROPE // 2, 1), pltpu.roll(x, MLA_ROPE // 2, 1))
    return x * cos + swapped * sin_signed


def _inproj_body(x_ref, cos_ref, sin_ref, nmix_ref, win_ref, qan_ref, wuq_ref, kvan_ref, wukv_ref,
                 wgate_ref, bgate_ref,
                 q_ref, k_ref, v_ref, gq_ref, gk_ref, gv_ref, gf_ref, gb_ref, og_ref):
    hn = _rms(x_ref[...], nmix_ref[...]).astype(BF16)

    def proj(lo, hi):
        return _dot(hn, win_ref[:, lo:hi])

    cos = cos_ref[...]
    sin = sin_ref[...]
    lane = lax.broadcasted_iota(jnp.int32, cos.shape, 1)
    first_half = (lane & (MLA_ROPE - 1)) < MLA_ROPE // 2
    low_lanes = lane < MLA_ROPE

    cq = _rms(proj(C_CQ, C_CKV), qan_ref[...]).astype(BF16)
    qn = _dot(cq, wuq_ref[:, 0:MLA_HEADS * MLA_NOPE]) * ATTN_SCALE
    qr = _dot(cq, wuq_ref[:, MLA_HEADS * MLA_NOPE:])
    for j in range(MLA_HEADS // 2):
        rj = (_rope_pairs(qr[:, j * LANES:(j + 1) * LANES], cos, sin, first_half) * ATTN_SCALE).astype(BF16)
        for h in (2 * j, 2 * j + 1):
            q_ref[h, :, 0:LANES] = qn[:, h * LANES:(h + 1) * LANES].astype(BF16)
            q_ref[h, :, LANES:QK_WIDTH] = rj

    ckv = _rms(proj(C_CKV, C_KPE), kvan_ref[...]).astype(BF16)
    kv = _dot(ckv, wukv_ref[...])
    kr = _rope_pairs(proj(C_KPE, C_GQ), cos, sin, first_half)
    kr_even = jnp.where(low_lanes, kr, 0.0).astype(BF16)
    kr_odd = jnp.where(low_lanes, 0.0, kr).astype(BF16)
    for h in range(MLA_HEADS):
        base = h * (MLA_NOPE + MLA_V)
        k_ref[h, :, 0:LANES] = kv[:, base:base + MLA_NOPE].astype(BF16)
        k_ref[h, :, LANES:QK_WIDTH] = kr_even if h % 2 == 0 else kr_odd
        v_ref[h, :, 0:MLA_V] = kv[:, base + MLA_NOPE:base + MLA_NOPE + MLA_V].astype(BF16)
        v_ref[h, :, MLA_V:V_WIDTH] = jnp.ones((kv.shape[0], V_WIDTH - MLA_V), BF16)

    gq_ref[...] = proj(C_GQ, C_GK) * (GLA_DK ** -0.5)
    gk_ref[...] = proj(C_GK, C_GV)
    gv_ref[...] = proj(C_GV, C_OG)
    og_ref[...] = proj(C_OG, C_LR)
    pre = _dot(proj(C_LR, D_IN_PACKED).astype(BF16), wgate_ref[...]) + bgate_ref[...]
    logsig = jnp.minimum(pre, 0.0) - jnp.log1p(jnp.exp(-jnp.abs(pre)))
    gates = logsig * (1.0 / GLA_TAU)
    gf_ref[...] = gates[:, 0:GLA_HEADS * GLA_DK]
    gb_ref[...] = gates[:, GLA_HEADS * GLA_DK:]


def _inproj(x2d, cos, sin, w, tm):
    t = x2d.shape[0]
    blocks_per_seq = cos.shape[0] // tm
    hk = GLA_HEADS * GLA_DK
    hv = GLA_HEADS * GLA_DV
    row = lambda width: pl.BlockSpec((tm, width), lambda i: (i, 0))
    head_rows = lambda width: pl.BlockSpec((MLA_HEADS, tm, width), lambda i: (0, i, 0))
    tab = pl.BlockSpec((tm, LANES), lambda i: (i % blocks_per_seq, 0))
    out_shape = (
        jax.ShapeDtypeStruct((MLA_HEADS, t, QK_WIDTH), BF16),
        jax.ShapeDtypeStruct((MLA_HEADS, t, QK_WIDTH), BF16),
        jax.ShapeDtypeStruct((MLA_HEADS, t, V_WIDTH), BF16),
        jax.ShapeDtypeStruct((t, hk), F32),
        jax.ShapeDtypeStruct((t, hk), F32),
        jax.ShapeDtypeStruct((t, hv), F32),
        jax.ShapeDtypeStruct((t, hk), F32),
        jax.ShapeDtypeStruct((t, hk), F32),
        jax.ShapeDtypeStruct((t, hv), F32),
    )
    return pl.pallas_call(
        _inproj_body,
        grid=(t // tm,),
        in_specs=[row(D_MODEL), tab, tab,
                  _full_spec((1, D_MODEL)), _full_spec((D_MODEL, D_IN_PACKED)),
                  _full_spec((1, MLA_Q_RANK)), _full_spec(w['wuq'].shape),
                  _full_spec((1, MLA_KV_RANK)), _full_spec(w['wukv'].shape),
                  _full_spec(w['wgate'].shape), _full_spec(w['bgate'].shape)],
        out_specs=(head_rows(QK_WIDTH), head_rows(QK_WIDTH), head_rows(V_WIDTH),
                   row(hk), row(hk), row(hv), row(hk), row(hk), row(hv)),
        out_shape=out_shape,
        compiler_params=_cparams("parallel"),
        name="inproj",
    )(x2d, cos, sin, w['norm_mix'], w['win'], w['q_a_norm'], w['wuq'], w['kv_a_norm'], w['wukv'],
      w['wgate'], w['bgate'])


def _attn_body(q_ref, k_ref, v_ref, km_ref, vm_ref, o_ref, s_ref, acc_ref, *, tk):
    q = q_ref[...]
    n_groups = k_ref.shape[0] // (ATTN_GROUP * tk)

    def scores(j):
        return _dot_nt(q, k_ref[pl.ds(pl.multiple_of(j * tk, tk), tk), :])

    def values(j):
        return v_ref[pl.ds(pl.multiple_of(j * tk, tk), tk), :]

    def absorb(m, s, v):
        m_new = jnp.maximum(m, jnp.max(s, axis=-1, keepdims=True))
        p = jnp.exp2(s - m_new)
        acc_ref[...] = jnp.exp2(m - m_new) * acc_ref[...] + _dot(p.astype(BF16), v)
        return m_new

    s_ref[0] = scores(0)
    sm = _dot_nt(q, km_ref[...])
    sm = jnp.where(lax.broadcasted_iota(jnp.int32, sm.shape, 1) < N_META, sm, -jnp.inf)
    m = jnp.max(sm, axis=-1, keepdims=True)
    acc_ref[...] = _dot(jnp.exp2(sm - m).astype(BF16), vm_ref[...])

    def group(g, m, last):
        for i in range(ATTN_GROUP):
            j = ATTN_GROUP * g + i
            s = s_ref[i % 2]
            if not (last and i == ATTN_GROUP - 1):
                s_ref[(i + 1) % 2] = scores(j + 1)
            m = absorb(m, s, values(j))
        return m

    m = lax.fori_loop(0, n_groups - 1, lambda g, m: group(g, m, False), m)
    group(n_groups - 1, m, True)
    acc = acc_ref[...]
    o_ref[...] = (acc[:, :MLA_V] / acc[:, MLA_V:]).astype(o_ref.dtype)


def _attention(q, k, v, km, vm, bsz, seq, tq, tk):
    nq = seq // tq
    return pl.pallas_call(
        functools.partial(_attn_body, tk=tk),
        grid=(bsz, MLA_HEADS, nq),
        in_specs=[pl.BlockSpec((None, tq, QK_WIDTH), lambda b, h, i: (h, b * nq + i, 0)),
                  pl.BlockSpec((None, seq, QK_WIDTH), lambda b, h, i: (h, b, 0)),
                  pl.BlockSpec((None, seq, V_WIDTH), lambda b, h, i: (h, b, 0)),
                  pl.BlockSpec((None, LANES, QK_WIDTH), lambda b, h, i: (h, 0, 0)),
                  pl.BlockSpec((None, LANES, V_WIDTH), lambda b, h, i: (h, 0, 0))],
        out_specs=pl.BlockSpec((tq, MLA_V), lambda b, h, i: (b * nq + i, h)),
        out_shape=jax.ShapeDtypeStruct((bsz * seq, MLA_HEADS * MLA_V), BF16),
        scratch_shapes=[pltpu.VMEM((2, tq, tk), F32), pltpu.VMEM((tq, V_WIDTH), F32)],
        compiler_params=_cparams("parallel", "parallel", "arbitrary"),
        name="mla_attention",
    )(q, k, v, km, vm)


def _split3(x):
    hi = x.astype(BF16)
    r1 = x - hi.astype(F32)
    mid = r1.astype(BF16)
    lo = (r1 - mid.astype(F32)).astype(BF16)
    return hi, mid, lo


def _gla_chunk(q, k, g, v_even, v_odd, state, tri, keep, mid, last):
    lane = lax.broadcasted_iota(jnp.int32, (GLA_CHUNK, LANES), 1)
    even = lane < GLA_DK
    g_hi, g_mid, g_lo = _split3(g)
    b = _dot(tri, g_hi) + _dot(tri, g_mid) + _dot(tri, g_lo)
    b_last = b[last:last + 1, :]
    ke = (k * jnp.exp(b_last - b)).astype(BF16)
    upd = jnp.where(lax.broadcasted_iota(jnp.int32, (GLA_DV, LANES), 1) < GLA_DK,
                    _dot_tn(v_even, ke), _dot_tn(v_odd, ke))
    new_state = state * jnp.exp(b_last) + upd
    if q is None:
        return new_state, None, None
    b_mid = b[mid:mid + 1, :]
    qs = q * jnp.exp(b - b_mid)
    ks = (k * jnp.exp(b_mid - b)).astype(BF16)
    qs2 = jnp.concatenate([jnp.where(even, qs, 0.0), jnp.where(even, 0.0, qs)], axis=0).astype(BF16)
    scores = jnp.where(keep, _dot_nt(qs2, ks), 0.0).astype(BF16)
    qe = q * jnp.exp(b)
    qe2 = jnp.concatenate([jnp.where(even, qe, 0.0), jnp.where(even, 0.0, qe)], axis=0).astype(BF16)
    inter = _dot_nt(qe2, state.astype(BF16))
    o_even = _dot(scores[:GLA_CHUNK], v_even) + inter[:GLA_CHUNK]
    o_odd = _dot(scores[GLA_CHUNK:], v_odd) + inter[GLA_CHUNK:]
    return new_state, o_even, o_odd


def _gla_body(qf_ref, kf_ref, vf_ref, gf_ref, qb_ref, kb_ref, vb_ref, gb_ref, mk_ref, mv_ref, mg_ref,
              of_ref, ob_ref, state_ref):
    n_chunks = qf_ref.shape[0] // GLA_CHUNK
    n_pairs = GLA_HEADS // 2
    r = lax.broadcasted_iota(jnp.int32, (GLA_CHUNK, GLA_CHUNK), 0)
    c = lax.broadcasted_iota(jnp.int32, (GLA_CHUNK, GLA_CHUNK), 1)
    tri_f = jnp.where(c <= r, 1.0, 0.0).astype(BF16)
    tri_b = jnp.where(c >= r, 1.0, 0.0).astype(BF16)
    r2 = lax.broadcasted_iota(jnp.int32, (2 * GLA_CHUNK, GLA_CHUNK), 0) & (GLA_CHUNK - 1)
    c2 = lax.broadcasted_iota(jnp.int32, (2 * GLA_CHUNK, GLA_CHUNK), 1)
    keep_f = c2 <= r2
    keep_b = c2 >= r2
    mid_f, last_f = GLA_CHUNK // 2 - 1, GLA_CHUNK - 1
    mid_b, last_b = GLA_CHUNK // 2, 0

    def pair_cols(p):
        return slice(p * LANES, (p + 1) * LANES)

    def head_cols(h):
        return slice(h * GLA_DV, (h + 1) * GLA_DV)

    @pl.when(pl.program_id(1) == 0)
    def _():
        for p in range(n_pairs):
            zero = jnp.zeros((GLA_DV, LANES), F32)
            st, _, _ = _gla_chunk(None, mk_ref[:, pair_cols(p)], mg_ref[:, pair_cols(p)],
                                  mv_ref[:, head_cols(2 * p)].astype(BF16),
                                  mv_ref[:, head_cols(2 * p + 1)].astype(BF16),
                                  zero, tri_f, keep_f, mid_f, last_f)
            state_ref[p] = st
            state_ref[n_pairs + p] = zero

    def step(i, _):
        rows_f = pl.ds(pl.multiple_of(i * GLA_CHUNK, GLA_CHUNK), GLA_CHUNK)
        rows_b = pl.ds(pl.multiple_of((n_chunks - 1 - i) * GLA_CHUNK, GLA_CHUNK), GLA_CHUNK)
        for p in range(n_pairs):
            for (rows, q_ref, k_ref, v_ref, g_ref, o_ref, slot, tri, keep, mid, last) in (
                    (rows_f, qf_ref, kf_ref, vf_ref, gf_ref, of_ref, p, tri_f, keep_f, mid_f, last_f),
                    (rows_b, qb_ref, kb_ref, vb_ref, gb_ref, ob_ref, n_pairs + p, tri_b, keep_b, mid_b, last_b)):
                st, o_even, o_odd = _gla_chunk(
                    q_ref[rows, pair_cols(p)], k_ref[rows, pair_cols(p)], g_ref[rows, pair_cols(p)],
                    v_ref[rows, head_cols(2 * p)].astype(BF16), v_ref[rows, head_cols(2 * p + 1)].astype(BF16),
                    state_ref[slot], tri, keep, mid, last)
                state_ref[slot] = st
                o_ref[rows, head_cols(2 * p)] = o_even
                o_ref[rows, head_cols(2 * p + 1)] = o_odd
        return 0

    lax.fori_loop(0, n_chunks, step, 0)


def _gla(gq, gk, gv, gf, gb, mk, mv, mg, bsz, seq, tb):
    nb = seq // tb
    hk = GLA_HEADS * GLA_DK
    hv = GLA_HEADS * GLA_DV
    fwd = lambda width: pl.BlockSpec((tb, width), lambda b, j: (b * nb + j, 0))
    bwd = lambda width: pl.BlockSpec((tb, width), lambda b, j: (b * nb + nb - 1 - j, 0))
    t = bsz * seq
    return pl.pallas_call(
        _gla_body,
        grid=(bsz, nb),
        in_specs=[fwd(hk), fwd(hk), fwd(hv), fwd(hk), bwd(hk), bwd(hk), bwd(hv), bwd(hk),
                  _full_spec(mk.shape), _full_spec(mv.shape), _full_spec(mg.shape)],
        out_specs=(fwd(hv), bwd(hv)),
        out_shape=(jax.ShapeDtypeStruct((t, hv), F32), jax.ShapeDtypeStruct((t, hv), F32)),
        scratch_shapes=[pltpu.VMEM((2 * (GLA_HEADS // 2), GLA_DV, LANES), F32)],
        compiler_params=_cparams("parallel", "arbitrary"),
        name="gla_scan",
    )(gq, gk, gv, gf, gq, gk, gv, gb, mk, mv, mg)


def _mix_body(x_ref, a_ref, of_ref, ob_ref, og_ref, gnorm_ref, wout_ref, nffn_ref, wr_hi_ref, wr_lo_ref, br_ref,
              h1_ref, hn_ref, route_ref, ids_ref, count_ref, tri_ref):
    tm = x_ref.shape[0]

    @pl.when(pl.program_id(0) == 0)
    def _():
        r = lax.broadcasted_iota(jnp.int32, (tm, tm), 0)
        c = lax.broadcasted_iota(jnp.int32, (tm, tm), 1)
        tri_ref[...] = jnp.where(c < r, 1.0, 0.0).astype(BF16)
        count_ref[...] = jnp.zeros_like(count_ref)

    a_width = MLA_HEADS * MLA_V
    h1 = x_ref[...] + _dot(a_ref[...], wout_ref[0:a_width, :])
    for h in range(GLA_HEADS):
        cols = slice(h * GLA_DV, (h + 1) * GLA_DV)
        o = of_ref[:, cols] + ob_ref[:, cols]
        og = og_ref[:, cols]
        silu = og / (1.0 + jnp.exp(-og))
        gh = (_rms(o, gnorm_ref[...]) * silu).astype(BF16)
        h1 = h1 + _dot(gh, wout_ref[a_width + h * GLA_DV:a_width + (h + 1) * GLA_DV, :])
    h1_ref[...] = h1
    hn = _rms(h1, nffn_ref[...])
    hn_ref[...] = hn

    hn_hi = hn.astype(BF16)
    hn_lo = (hn - hn_hi.astype(F32)).astype(BF16)
    logits = (_dot(hn_hi, wr_hi_ref[...]) + _dot(hn_lo, wr_hi_ref[...]) + _dot(hn_hi, wr_lo_ref[...])
              + br_ref[...])
    lane = lax.broadcasted_iota(jnp.int32, logits.shape, 1).astype(F32)
    none = float(LANES)
    neg = -jnp.inf

    def lane_max(x):
        return jnp.max(x, axis=-1, keepdims=True)

    def lane_sum(x):
        return jnp.sum(x, axis=-1, keepdims=True)

    def first_lane(mask):
        return jnp.min(jnp.where(mask, lane, none), axis=-1, keepdims=True)

    is_group = lane < float(N_GROUPS)
    g_max = lane_max(jnp.where(is_group, logits, neg))
    g_exp = jnp.where(is_group, jnp.exp(logits - g_max), 0.0)
    g_prob = g_exp / lane_sum(g_exp)
    g_w = lane_max(g_prob)
    g_idx = first_lane(is_group & (g_prob == g_w))
    e_lo = float(ROUTER_EXPERT_LANE0) + float(EXPERTS_PER_GROUP) * g_idx
    sel = (lane >= e_lo) & (lane < e_lo + float(EXPERTS_PER_GROUP))
    e_max = lane_max(jnp.where(sel, logits, neg))
    e_exp = jnp.where(sel, jnp.exp(logits - e_max), 0.0)
    e_prob = e_exp / lane_sum(e_exp)
    p1 = lane_max(jnp.where(sel, e_prob, neg))
    i1 = first_lane(sel & (e_prob == p1))
    rest = sel & (lane != i1)
    p2 = lane_max(jnp.where(rest, e_prob, neg))
    i2 = first_lane(rest & (e_prob == p2))
    denom = p1 + p2
    chosen = jnp.where((lane == i1) | (lane == i2), 1.0, 0.0)
    rank = count_ref[...] + _dot(tri_ref[...], chosen.astype(BF16))
    count_ref[...] += jnp.sum(chosen, axis=0, keepdims=True)
    fields = (i1 - float(ROUTER_EXPERT_LANE0), i2 - float(ROUTER_EXPERT_LANE0),
              lane_sum(jnp.where(lane == i1, rank, 0.0)), lane_sum(jnp.where(lane == i2, rank, 0.0)),
              g_w * (p1 / denom), g_w * (p2 / denom))
    route = jnp.zeros_like(logits)
    for k, value in enumerate(fields):
        route = jnp.where(lane == float(k), value, route)
    route_ref[...] = route
    ids_ref[...] = jnp.transpose(route)[0:ROUTE_ID_ROWS, :].astype(jnp.int32)


ROUTE_E1, ROUTE_E2, ROUTE_RANK1, ROUTE_RANK2, ROUTE_W1, ROUTE_W2 = range(6)
ROUTE_ID_ROWS = 8


def _mix(x2d, a, o_f, o_b, og, w, tm):
    t = x2d.shape[0]
    hv = GLA_HEADS * GLA_DV
    row = lambda width: pl.BlockSpec((tm, width), lambda i: (i, 0))
    return pl.pallas_call(
        _mix_body,
        grid=(t // tm,),
        in_specs=[row(D_MODEL), row(MLA_HEADS * MLA_V), row(hv), row(hv), row(hv),
                  _full_spec((1, GLA_DV)), _full_spec(w['wout'].shape), _full_spec((1, D_MODEL)),
                  _full_spec(w['wr_hi'].shape), _full_spec(w['wr_lo'].shape), _full_spec(w['br'].shape)],
        out_specs=(row(D_MODEL), row(D_MODEL), row(LANES),
                   pl.BlockSpec((ROUTE_ID_ROWS, tm), lambda i: (0, i)), _full_spec((1, LANES))),
        out_shape=(jax.ShapeDtypeStruct((t, D_MODEL), F32), jax.ShapeDtypeStruct((t, D_MODEL), F32),
                   jax.ShapeDtypeStruct((t, LANES), F32), jax.ShapeDtypeStruct((ROUTE_ID_ROWS, t), jnp.int32),
                   jax.ShapeDtypeStruct((1, LANES), F32)),
        scratch_shapes=[pltpu.VMEM((tm, tm), BF16)],
        compiler_params=_cparams("arbitrary"),
        name="mix_router",
    )(x2d, a, o_f, o_b, og, w['gla_norm'], w['wout'], w['norm_ffn'], w['wr_hi'], w['wr_lo'], w['br'])


EXPERT_TILE = 256
ROW_TILE = 256


def _row_copy(src_ref, src_row, dst_ref, dst_row, sem):
    return pltpu.make_async_copy(src_ref.at[pl.ds(src_row, 1)], dst_ref.at[pl.ds(dst_row, 1)], sem)


def _slot(starts_ref, ids_ref, k, r):
    return starts_ref[ids_ref[ROUTE_E1 + k, r]] + ids_ref[ROUTE_RANK1 + k, r]


def _dispatch_body(starts_ref, fill_ref, n_tiles_ref, ids_ref, hn_ref, zeros_ref, xs_ref, stage_ref, sem, fill_sem):
    i = pl.program_id(0)
    last = pl.num_programs(0) - 1
    cur = i % 2

    def wait_rows(half):
        for _ in range(2 * ROW_TILE):
            _row_copy(stage_ref.at[half], 0, xs_ref, 0, sem.at[half]).wait()

    @pl.when(i == 0)
    def _():
        def fill(start):
            return pltpu.make_async_copy(zeros_ref, xs_ref.at[pl.ds(pl.multiple_of(start, EXPERT_TILE), EXPERT_TILE)],
                                         fill_sem)
        unused = (n_tiles_ref[0], xs_ref.shape[0] // EXPERT_TILE)
        for e in range(N_EXPERTS):
            fill(fill_ref[e]).start()
        lax.fori_loop(*unused, lambda j, c: (fill(j * EXPERT_TILE).start(), c)[1], 0)
        for e in range(N_EXPERTS):
            fill(fill_ref[e]).wait()
        lax.fori_loop(*unused, lambda j, c: (fill(j * EXPERT_TILE).wait(), c)[1], 0)

    @pl.when(i >= 2)
    def _():
        wait_rows(cur)

    stage_ref[cur] = hn_ref[...]

    def issue(r, carry):
        for k in range(2):
            _row_copy(stage_ref.at[cur], r, xs_ref, _slot(starts_ref, ids_ref, k, r), sem.at[cur]).start()
        return carry

    lax.fori_loop(0, ROW_TILE, issue, 0, unroll=8)

    @pl.when(i == last)
    def _():
        wait_rows(1 - cur)
        wait_rows(cur)


def _dispatch(starts, fill_starts, n_tiles, ids, hn, n_rows):
    t = hn.shape[0]
    assert t // ROW_TILE >= 2
    zeros = jnp.zeros((EXPERT_TILE, D_MODEL), F32)
    return pl.pallas_call(
        _dispatch_body,
        grid_spec=pltpu.PrefetchScalarGridSpec(
            num_scalar_prefetch=3,
            grid=(t // ROW_TILE,),
            in_specs=[pl.BlockSpec((ROUTE_ID_ROWS, ROW_TILE), lambda i, *_: (0, i), memory_space=pltpu.SMEM),
                      pl.BlockSpec((ROW_TILE, D_MODEL), lambda i, *_: (i, 0)),
                      pl.BlockSpec(memory_space=pl.ANY)],
            out_specs=pl.BlockSpec(memory_space=pl.ANY),
            scratch_shapes=[pltpu.VMEM((2, ROW_TILE, D_MODEL), F32), pltpu.SemaphoreType.DMA((2,)),
                            pltpu.SemaphoreType.DMA(())]),
        out_shape=jax.ShapeDtypeStruct((n_rows, D_MODEL), F32),
        compiler_params=_cparams("arbitrary"),
        name="moe_dispatch",
    )(starts, fill_starts, n_tiles, ids, hn, zeros)


def _expert_body(tile_expert_ref, n_tiles_ref, xs_ref, wgu_ref, wd_ref, ys_ref):
    used = pl.program_id(0) < n_tiles_ref[0]

    @pl.when(used)
    def _():
        h = _dot(xs_ref[...].astype(BF16), wgu_ref[...])
        hg = h[:, :D_EXPERT]
        act = (hg / (1.0 + jnp.exp(-hg))) * h[:, D_EXPERT:]
        ys_ref[...] = _dot(act.astype(BF16), wd_ref[...])

    @pl.when(jnp.logical_not(used))
    def _():
        ys_ref[...] = jnp.zeros_like(ys_ref)


def _experts(tile_expert, n_tiles, xs, w):
    n_rows = xs.shape[0]
    grid = n_rows // EXPERT_TILE

    def tile(i, tile_expert, n_tiles):
        return jnp.minimum(i, n_tiles[0] - 1)

    return pl.pallas_call(
        _expert_body,
        grid_spec=pltpu.PrefetchScalarGridSpec(
            num_scalar_prefetch=2,
            grid=(grid,),
            in_specs=[pl.BlockSpec((EXPERT_TILE, D_MODEL), lambda i, te, nt: (tile(i, te, nt), 0)),
                      pl.BlockSpec((None, D_MODEL, 2 * D_EXPERT), lambda i, te, nt: (te[tile(i, te, nt)], 0, 0)),
                      pl.BlockSpec((None, D_EXPERT, D_MODEL), lambda i, te, nt: (te[tile(i, te, nt)], 0, 0))],
            out_specs=pl.BlockSpec((EXPERT_TILE, D_MODEL), lambda i, te, nt: (i, 0))),
        out_shape=jax.ShapeDtypeStruct((n_rows, D_MODEL), F32),
        compiler_params=_cparams("arbitrary"),
        name="moe_experts",
    )(tile_expert, n_tiles, xs, w['wgu'], w['wd'])


def _combine_body(starts_ref, ids_ref, next_ids_ref, h1_ref, route_ref, nfin_ref, ys_ref, out_ref, buf_ref, sem):
    i = pl.program_id(0)
    cur = i % 2

    def issue(ids, half):
        def body(r, carry):
            for k in range(2):
                _row_copy(ys_ref, _slot(starts_ref, ids, k, r), buf_ref.at[half, k], r, sem.at[half]).start()
            return carry
        lax.fori_loop(0, ROW_TILE, body, 0, unroll=8)

    @pl.when(i == 0)
    def _():
        issue(ids_ref, 0)

    @pl.when(i + 1 < pl.num_programs(0))
    def _():
        issue(next_ids_ref, 1 - cur)

    for _ in range(2 * ROW_TILE):
        _row_copy(ys_ref, 0, buf_ref.at[cur, 0], 0, sem.at[cur]).wait()
    route = route_ref[...]
    lane = lax.broadcasted_iota(jnp.int32, route.shape, 1)
    w1 = jnp.sum(jnp.where(lane == ROUTE_W1, route, 0.0), axis=-1, keepdims=True)
    w2 = jnp.sum(jnp.where(lane == ROUTE_W2, route, 0.0), axis=-1, keepdims=True)
    out_ref[...] = _rms(h1_ref[...] + w1 * buf_ref[cur, 0] + w2 * buf_ref[cur, 1], nfin_ref[...])


def _combine(starts, ids, h1, route, ys, w):
    t = h1.shape[0]
    n_steps = t // ROW_TILE
    row = lambda width: pl.BlockSpec((ROW_TILE, width), lambda i, *_: (i, 0))
    ids_block = lambda index: pl.BlockSpec((ROUTE_ID_ROWS, ROW_TILE), index, memory_space=pltpu.SMEM)
    return pl.pallas_call(
        _combine_body,
        grid_spec=pltpu.PrefetchScalarGridSpec(
            num_scalar_prefetch=1,
            grid=(n_steps,),
            in_specs=[ids_block(lambda i, *_: (0, i)), ids_block(lambda i, *_: (0, jnp.minimum(i + 1, n_steps - 1))),
                      row(D_MODEL), row(LANES), pl.BlockSpec((1, D_MODEL), lambda i, *_: (0, 0)),
                      pl.BlockSpec(memory_space=pl.ANY)],
            out_specs=row(D_MODEL),
            scratch_shapes=[pltpu.VMEM((2, 2, ROW_TILE, D_MODEL), F32), pltpu.SemaphoreType.DMA((2,))]),
        out_shape=jax.ShapeDtypeStruct((t, D_MODEL), F32),
        compiler_params=_cparams("arbitrary"),
        name="moe_combine",
    )(starts, ids, ids, h1, route, w['norm_final'], ys)


def _moe(hn, route, ids, counts, h1, w):
    t = hn.shape[0]
    n_rows = 2 * t + N_EXPERTS * EXPERT_TILE
    n_grid_tiles = n_rows // EXPERT_TILE
    count = counts[0, ROUTER_EXPERT_LANE0:ROUTER_EXPERT_LANE0 + N_EXPERTS].astype(jnp.int32)
    padded = jnp.maximum((count + EXPERT_TILE - 1) // EXPERT_TILE, 1) * EXPERT_TILE
    ends = jnp.sum(jnp.where(jnp.arange(N_EXPERTS)[:, None] <= jnp.arange(N_EXPERTS)[None, :], padded[:, None], 0),
                   axis=0)
    starts = ends - padded
    n_tiles = (ends[-1:] // EXPERT_TILE)
    tile_rows = jnp.arange(n_grid_tiles, dtype=jnp.int32) * EXPERT_TILE
    tile_expert = jnp.minimum(jnp.sum((ends[None, :] <= tile_rows[:, None]).astype(jnp.int32), axis=1), N_EXPERTS - 1)
    xs = _dispatch(starts, ends - EXPERT_TILE, n_tiles, ids, hn, n_rows)
    ys = _experts(tile_expert, n_tiles, xs, w)
    return _combine(starts, ids, h1, route, ys, w)


def _rope_tables(positions):
    inv_freq = 1.0 / (ROPE_THETA ** (jnp.arange(0, MLA_ROPE, 2, dtype=F32) / MLA_ROPE))
    ang = positions.astype(F32)[:, None] * inv_freq[None, :]
    cos, sin = jnp.cos(ang), jnp.sin(ang)
    reps = LANES // MLA_ROPE
    return jnp.tile(jnp.concatenate([cos, cos], axis=-1), (1, reps)), jnp.tile(jnp.concatenate([-sin, sin], axis=-1), (1, reps))


def _pack_weights(norm_mix, w_in, q_a_norm, w_uq, kv_a_norm, w_ukv, w_gate_fwd, b_gate_fwd, w_gate_bwd, b_gate_bwd,
                  gla_norm, w_out, norm_ffn, w_router_group, b_router_group, w_router_expert, b_router_expert,
                  w_expert_gate, w_expert_up, w_expert_down, norm_final):
    l = 0
    hk = GLA_HEADS * GLA_DK
    hv = GLA_HEADS * GLA_DV
    c_q, c_kv, k_pe, gq, gk, gv, lr_f, lr_b, og = jnp.split(
        w_in[l], np.cumsum([MLA_Q_RANK, MLA_KV_RANK, MLA_ROPE, hk, hk, hv, GLA_GATE_RANK, GLA_GATE_RANK])[:].tolist(),
        axis=-1)
    lr_pad = jnp.zeros((D_MODEL, LANES - 2 * GLA_GATE_RANK), F32)
    win = jnp.concatenate([c_q, c_kv, k_pe, k_pe, gq, gk, gv, og, lr_f, lr_b, lr_pad], axis=-1).astype(BF16)
    wuq = w_uq[l].reshape(MLA_Q_RANK, MLA_HEADS, MLA_NOPE + MLA_ROPE)
    wuq = jnp.concatenate([wuq[:, :, :MLA_NOPE].reshape(MLA_Q_RANK, -1), wuq[:, :, MLA_NOPE:].reshape(MLA_Q_RANK, -1)],
                          axis=-1).astype(BF16)
    wgate = jnp.zeros((LANES, 2 * hk), F32)
    wgate = wgate.at[0:GLA_GATE_RANK, 0:hk].set(w_gate_fwd[l])
    wgate = wgate.at[GLA_GATE_RANK:2 * GLA_GATE_RANK, hk:].set(w_gate_bwd[l])
    wr = jnp.zeros((D_MODEL, LANES), F32)
    wr = wr.at[:, ROUTER_GROUP_LANE0:ROUTER_GROUP_LANE0 + N_GROUPS].set(w_router_group[l])
    wr = wr.at[:, ROUTER_EXPERT_LANE0:ROUTER_EXPERT_LANE0 + N_EXPERTS].set(w_router_expert[l])
    wr_hi = wr.astype(BF16)
    br = jnp.zeros((1, LANES), F32)
    br = br.at[0, ROUTER_GROUP_LANE0:ROUTER_GROUP_LANE0 + N_GROUPS].set(b_router_group[l])
    br = br.at[0, ROUTER_EXPERT_LANE0:ROUTER_EXPERT_LANE0 + N_EXPERTS].set(b_router_expert[l])
    wgu = jnp.concatenate([w_expert_gate[l], w_expert_up[l]], axis=-1).reshape(N_EXPERTS, D_MODEL, 2 * D_EXPERT)
    return {
        'norm_mix': norm_mix[l][None], 'win': win, 'q_a_norm': q_a_norm[l][None], 'wuq': wuq,
        'kv_a_norm': kv_a_norm[l][None], 'wukv': w_ukv[l].astype(BF16),
        'wgate': wgate.astype(BF16), 'bgate': jnp.concatenate([b_gate_fwd[l], b_gate_bwd[l]])[None],
        'gla_norm': gla_norm[l][None], 'wout': w_out[l].astype(BF16), 'norm_ffn': norm_ffn[l][None],
        'wr_hi': wr_hi, 'wr_lo': (wr - wr_hi.astype(F32)).astype(BF16), 'br': br,
        'wgu': wgu.astype(BF16), 'wd': w_expert_down[l].reshape(N_EXPERTS, D_EXPERT, D_MODEL).astype(BF16),
        'norm_final': norm_final[None],
    }


def _meta_streams(meta_tokens, w):
    cos, sin = _rope_tables(jnp.arange(N_META))
    _, k, v, _, gk, gv, gf, _, _ = _inproj(meta_tokens, cos, sin, w, N_META)
    pad_keys = ((0, 0), (0, LANES - N_META), (0, 0))
    front = ((GLA_CHUNK - N_META, 0), (0, 0))
    return (jnp.pad(k, pad_keys), jnp.pad(v, pad_keys), jnp.pad(gk, front), jnp.pad(gv, front), jnp.pad(gf, front))


def _token_mixers(x, meta, w, tm, tq, tk, tb):
    bsz, seq, _ = x.shape
    km, vm, mk, mv, mg = meta
    x2d = x.reshape(bsz * seq, D_MODEL)
    cos, sin = _rope_tables(N_META + jnp.arange(seq))
    q, k, v, gq, gk, gv, gf, gb, og = _inproj(x2d, cos, sin, w, tm)
    a = _attention(q, k, v, km, vm, bsz, seq, tq, tk)
    o_f, o_b = _gla(gq, gk, gv, gf, gb, mk, mv, mg, bsz, seq, tb)
    return _mix(x2d, a, o_f, o_b, og, w, tm)


def kernel(x_prompt, x_sample, meta_tokens, norm_mix, w_in, q_a_norm, w_uq, kv_a_norm, w_ukv, w_gate_fwd, b_gate_fwd, w_gate_bwd, b_gate_bwd, gla_norm, w_out, norm_ffn, w_router_group, b_router_group, w_router_expert, b_router_expert, w_expert_gate, w_expert_up, w_expert_down, norm_final):
    w = _pack_weights(norm_mix, w_in, q_a_norm, w_uq, kv_a_norm, w_ukv, w_gate_fwd, b_gate_fwd, w_gate_bwd,
                      b_gate_bwd, gla_norm, w_out, norm_ffn, w_router_group, b_router_group, w_router_expert,
                      b_router_expert, w_expert_gate, w_expert_up, w_expert_down, norm_final)
    meta = _meta_streams(meta_tokens, w)
    outs = []
    for x in (x_prompt, x_sample):
        h1, hn, route, ids, counts = _token_mixers(x, meta, w, tm=512, tq=512, tk=512, tb=512)
        outs.append(_moe(hn, route, ids, counts, h1, w).reshape(x.shape))
    return tuple(outs)
```

```python
import functools

import numpy as np
import jax
import jax.numpy as jnp
from jax import lax
from jax.experimental import pallas as pl
from jax.experimental.pallas import tpu as pltpu

F32 = jnp.float32
BF16 = jnp.bfloat16

D_MODEL = 1024
N_META = 16
MLA_HEADS = 4
MLA_Q_RANK = 384
MLA_KV_RANK = 256
MLA_NOPE = 128
MLA_ROPE = 64
MLA_V = 128
ROPE_THETA = 10000.0
GLA_HEADS = 4
GLA_DK = 64
GLA_DV = 128
GLA_GATE_RANK = 16
GLA_TAU = 16.0
GLA_CHUNK = 64
N_GROUPS = 4
EXPERTS_PER_GROUP = 8
N_EXPERTS = N_GROUPS * EXPERTS_PER_GROUP
D_EXPERT = 256
EPS = 1e-6

LANES = 128
V7X_VMEM_BYTES = 64 * 1024 * 1024
VMEM_LIMIT = V7X_VMEM_BYTES * 7 // 8

ATTN_SCALE = (MLA_NOPE + MLA_ROPE) ** -0.5 * float(np.log2(np.e))
QK_WIDTH = 2 * LANES
V_WIDTH = 2 * LANES
ATTN_GROUP = 4

C_CQ = 0
C_CKV = C_CQ + MLA_Q_RANK
C_KPE = C_CKV + MLA_KV_RANK
C_GQ = C_KPE + LANES
C_GK = C_GQ + GLA_HEADS * GLA_DK
C_GV = C_GK + GLA_HEADS * GLA_DK
C_OG = C_GV + GLA_HEADS * GLA_DV
C_LR = C_OG + GLA_HEADS * GLA_DV
D_IN_PACKED = C_LR + LANES

ROUTER_GROUP_LANE0 = 0
ROUTER_EXPERT_LANE0 = N_GROUPS


def _cparams(*semantics):
    return pltpu.CompilerParams(dimension_semantics=semantics, vmem_limit_bytes=VMEM_LIMIT)


def _rms(x, g):
    return x * lax.rsqrt(jnp.mean(x * x, axis=-1, keepdims=True) + EPS) * g


def _dot(a, b):
    return jnp.dot(a, b, preferred_element_type=F32)


def _dot_nt(a, b):
    return lax.dot_general(a, b, (((1,), (1,)), ((), ())), preferred_element_type=F32)


def _dot_tn(a, b):
    return lax.dot_general(a, b, (((0,), (0,)), ((), ())), preferred_element_type=F32)


def _full_spec(shape):
    return pl.BlockSpec(shape, lambda *_: (0,) * len(shape))


def _rope_pairs(x, cos, sin_signed, first_half):
    swapped = jnp.where(first_half, pltpu.roll(x, LANES - MLA_ROPE // 2, 1), pltpu.roll(x, MLA_ROPE // 2, 1))
    return x * cos + swapped * sin_signed


def _inproj_body(x_ref, cos_ref, sin_ref, nmix_ref, win_ref, qan_ref, wuq_ref, kvan_ref, wukv_ref,
                 wgate_ref, bgate_ref,
                 q_ref, k_ref, v_ref, gq_ref, gk_ref, gv_ref, gf_ref, gb_ref, og_ref):
    hn = _rms(x_ref[...], nmix_ref[...]).astype(BF16)

    def proj(lo, hi):
        return _dot(hn, win_ref[:, lo:hi])

    cos = cos_ref[...]
    sin = sin_ref[...]
    lane = lax.broadcasted_iota(jnp.int32, cos.shape, 1)
    first_half = (lane & (MLA_ROPE - 1)) < MLA_ROPE // 2
    low_lanes = lane < MLA_ROPE

    cq = _rms(proj(C_CQ, C_CKV), qan_ref[...]).astype(BF16)
    qn = _dot(cq, wuq_ref[:, 0:MLA_HEADS * MLA_NOPE]) * ATTN_SCALE
    qr = _dot(cq, wuq_ref[:, MLA_HEADS * MLA_NOPE:])
    for j in range(MLA_HEADS // 2):
        rj = (_rope_pairs(qr[:, j * LANES:(j + 1) * LANES], cos, sin, first_half) * ATTN_SCALE).astype(BF16)
        for h in (2 * j, 2 * j + 1):
            q_ref[h, :, 0:LANES] = qn[:, h * LANES:(h + 1) * LANES].astype(BF16)
            q_ref[h, :, LANES:QK_WIDTH] = rj

    ckv = _rms(proj(C_CKV, C_KPE), kvan_ref[...]).astype(BF16)
    kv = _dot(ckv, wukv_ref[...])
    kr = _rope_pairs(proj(C_KPE, C_GQ), cos, sin, first_half)
    kr_even = jnp.where(low_lanes, kr, 0.0).astype(BF16)
    kr_odd = jnp.where(low_lanes, 0.0, kr).astype(BF16)
    for h in range(MLA_HEADS):
        base = h * (MLA_NOPE + MLA_V)
        k_ref[h, :, 0:LANES] = kv[:, base:base + MLA_NOPE].astype(BF16)
        k_ref[h, :, LANES:QK_WIDTH] = kr_even if h % 2 == 0 else kr_odd
        v_ref[h, :, 0:MLA_V] = kv[:, base + MLA_NOPE:base + MLA_NOPE + MLA_V].astype(BF16)
        v_ref[h, :, MLA_V:V_WIDTH] = jnp.ones((kv.shape[0], V_WIDTH - MLA_V), BF16)

    gq_ref[...] = proj(C_GQ, C_GK) * (GLA_DK ** -0.5)
    gk_ref[...] = proj(C_GK, C_GV)
    gv_ref[...] = proj(C_GV, C_OG)
    og_ref[...] = proj(C_OG, C_LR)
    pre = _dot(proj(C_LR, D_IN_PACKED).astype(BF16), wgate_ref[...]) + bgate_ref[...]
    logsig = jnp.minimum(pre, 0.0) - jnp.log1p(jnp.exp(-jnp.abs(pre)))
    gates = logsig * (1.0 / GLA_TAU)
    gf_ref[...] = gates[:, 0:GLA_HEADS * GLA_DK]
    gb_ref[...] = gates[:, GLA_HEADS * GLA_DK:]


def _inproj(x2d, cos, sin, w, tm):
    t = x2d.shape[0]
    blocks_per_seq = cos.shape[0] // tm
    hk = GLA_HEADS * GLA_DK
    hv = GLA_HEADS * GLA_DV
    row = lambda width: pl.BlockSpec((tm, width), lambda i: (i, 0))
    head_rows = lambda width: pl.BlockSpec((MLA_HEADS, tm, width), lambda i: (0, i, 0))
    tab = pl.BlockSpec((tm, LANES), lambda i: (i % blocks_per_seq, 0))
    out_shape = (
        jax.ShapeDtypeStruct((MLA_HEADS, t, QK_WIDTH), BF16),
        jax.ShapeDtypeStruct((MLA_HEADS, t, QK_WIDTH), BF16),
        jax.ShapeDtypeStruct((MLA_HEADS, t, V_WIDTH), BF16),
        jax.ShapeDtypeStruct((t, hk), F32),
        jax.ShapeDtypeStruct((t, hk), F32),
        jax.ShapeDtypeStruct((t, hv), F32),
        jax.ShapeDtypeStruct((t, hk), F32),
        jax.ShapeDtypeStruct((t, hk), F32),
        jax.ShapeDtypeStruct((t, hv), F32),
    )
    return pl.pallas_call(
        _inproj_body,
        grid=(t // tm,),
        in_specs=[row(D_MODEL), tab, tab,
                  _full_spec((1, D_MODEL)), _full_spec((D_MODEL, D_IN_PACKED)),
                  _full_spec((1, MLA_Q_RANK)), _full_spec(w['wuq'].shape),
                  _full_spec((1, MLA_KV_RANK)), _full_spec(w['wukv'].shape),
                  _full_spec(w['wgate'].shape), _full_spec(w['bgate'].shape)],
        out_specs=(head_rows(QK_WIDTH), head_rows(QK_WIDTH), head_rows(V_WIDTH),
                   row(hk), row(hk), row(hv), row(hk), row(hk), row(hv)),
        out_shape=out_shape,
        compiler_params=_cparams("parallel"),
        name="inproj",
    )(x2d, cos, sin, w['norm_mix'], w['win'], w['q_a_norm'], w['wuq'], w['kv_a_norm'], w['wukv'],
      w['wgate'], w['bgate'])


def _attn_body(q_ref, k_ref, v_ref, km_ref, vm_ref, o_ref, s_ref, acc_ref, *, tk):
    q = q_ref[...]
    n_groups = k_ref.shape[0] // (ATTN_GROUP * tk)

    def scores(j):
        return _dot_nt(q, k_ref[pl.ds(pl.multiple_of(j * tk, tk), tk), :])

    def values(j):
        return v_ref[pl.ds(pl.multiple_of(j * tk, tk), tk), :]

    def absorb(m, s, v):
        m_new = jnp.maximum(m, jnp.max(s, axis=-1, keepdims=True))
        p = jnp.exp2(s - m_new)
        acc_ref[...] = jnp.exp2(m - m_new) * acc_ref[...] + _dot(p.astype(BF16), v)
        return m_new

    s_ref[0] = scores(0)
    sm = _dot_nt(q, km_ref[...])
    sm = jnp.where(lax.broadcasted_iota(jnp.int32, sm.shape, 1) < N_META, sm, -jnp.inf)
    m = jnp.max(sm, axis=-1, keepdims=True)
    acc_ref[...] = _dot(jnp.exp2(sm - m).astype(BF16), vm_ref[...])

    def group(g, m, last):
        for i in range(ATTN_GROUP):
            j = ATTN_GROUP * g + i
            s = s_ref[i % 2]
            if not (last and i == ATTN_GROUP - 1):
                s_ref[(i + 1) % 2] = scores(j + 1)
            m = absorb(m, s, values(j))
        return m

    m = lax.fori_loop(0, n_groups - 1, lambda g, m: group(g, m, False), m)
    group(n_groups - 1, m, True)
    acc = acc_ref[...]
    o_ref[...] = (acc[:, :MLA_V] / acc[:, MLA_V:]).astype(o_ref.dtype)


def _attention(q, k, v, km, vm, bsz, seq, tq, tk):
    nq = seq // tq
    return pl.pallas_call(
        functools.partial(_attn_body, tk=tk),
        grid=(bsz, MLA_HEADS, nq),
        in_specs=[pl.BlockSpec((None, tq, QK_WIDTH), lambda b, h, i: (h, b * nq + i, 0)),
                  pl.BlockSpec((None, seq, QK_WIDTH), lambda b, h, i: (h, b, 0)),
                  pl.BlockSpec((None, seq, V_WIDTH), lambda b, h, i: (h, b, 0)),
                  pl.BlockSpec((None, LANES, QK_WIDTH), lambda b, h, i: (h, 0, 0)),
                  pl.BlockSpec((None, LANES, V_WIDTH), lambda b, h, i: (h, 0, 0))],
        out_specs=pl.BlockSpec((tq, MLA_V), lambda b, h, i: (b * nq + i, h)),
        out_shape=jax.ShapeDtypeStruct((bsz * seq, MLA_HEADS * MLA_V), BF16),
        scratch_shapes=[pltpu.VMEM((2, tq, tk), F32), pltpu.VMEM((tq, V_WIDTH), F32)],
        compiler_params=_cparams("parallel", "parallel", "arbitrary"),
        name="mla_attention",
    )(q, k, v, km, vm)


def _split3(x):
    hi = x.astype(BF16)
    r1 = x - hi.astype(F32)
    mid = r1.astype(BF16)
    lo = (r1 - mid.astype(F32)).astype(BF16)
    return hi, mid, lo


def _gla_chunk(q, k, g, v_even, v_odd, state, tri, keep, mid, last):
    lane = lax.broadcasted_iota(jnp.int32, (GLA_CHUNK, LANES), 1)
    even = lane < GLA_DK
    g_hi, g_mid, g_lo = _split3(g)
    b = _dot(tri, g_hi) + _dot(tri, g_mid) + _dot(tri, g_lo)
    b_last = b[last:last + 1, :]
    ke = (k * jnp.exp(b_last - b)).astype(BF16)
    upd = jnp.where(lax.broadcasted_iota(jnp.int32, (GLA_DV, LANES), 1) < GLA_DK,
                    _dot_tn(v_even, ke), _dot_tn(v_odd, ke))
    new_state = state * jnp.exp(b_last) + upd
    if q is None:
        return new_state, None, None
    b_mid = b[mid:mid + 1, :]
    qs = q * jnp.exp(b - b_mid)
    ks = (k * jnp.exp(b_mid - b)).astype(BF16)
    qs2 = jnp.concatenate([jnp.where(even, qs, 0.0), jnp.where(even, 0.0, qs)], axis=0).astype(BF16)
    scores = jnp.where(keep, _dot_nt(qs2, ks), 0.0).astype(BF16)
    qe = q * jnp.exp(b)
    qe2 = jnp.concatenate([jnp.where(even, qe, 0.0), jnp.where(even, 0.0, qe)], axis=0).astype(BF16)
    inter = _dot_nt(qe2, state.astype(BF16))
    o_even = _dot(scores[:GLA_CHUNK], v_even) + inter[:GLA_CHUNK]
    o_odd = _dot(scores[GLA_CHUNK:], v_odd) + inter[GLA_CHUNK:]
    return new_state, o_even, o_odd


def _gla_body(qf_ref, kf_ref, vf_ref, gf_ref, qb_ref, kb_ref, vb_ref, gb_ref, mk_ref, mv_ref, mg_ref,
              of_ref, ob_ref, state_ref):
    n_chunks = qf_ref.shape[0] // GLA_CHUNK
    n_pairs = GLA_HEADS // 2
    r = lax.broadcasted_iota(jnp.int32, (GLA_CHUNK, GLA_CHUNK), 0)
    c = lax.broadcasted_iota(jnp.int32, (GLA_CHUNK, GLA_CHUNK), 1)
    tri_f = jnp.where(c <= r, 1.0, 0.0).astype(BF16)
    tri_b = jnp.where(c >= r, 1.0, 0.0).astype(BF16)
    r2 = lax.broadcasted_iota(jnp.int32, (2 * GLA_CHUNK, GLA_CHUNK), 0) & (GLA_CHUNK - 1)
    c2 = lax.broadcasted_iota(jnp.int32, (2 * GLA_CHUNK, GLA_CHUNK), 1)
    keep_f = c2 <= r2
    keep_b = c2 >= r2
    mid_f, last_f = GLA_CHUNK // 2 - 1, GLA_CHUNK - 1
    mid_b, last_b = GLA_CHUNK // 2, 0

    def pair_cols(p):
        return slice(p * LANES, (p + 1) * LANES)

    def head_cols(h):
        return slice(h * GLA_DV, (h + 1) * GLA_DV)

    @pl.when(pl.program_id(1) == 0)
    def _():
        for p in range(n_pairs):
            zero = jnp.zeros((GLA_DV, LANES), F32)
            st, _, _ = _gla_chunk(None, mk_ref[:, pair_cols(p)], mg_ref[:, pair_cols(p)],
                                  mv_ref[:, head_cols(2 * p)].astype(BF16),
                                  mv_ref[:, head_cols(2 * p + 1)].astype(BF16),
                                  zero, tri_f, keep_f, mid_f, last_f)
            state_ref[p] = st
            state_ref[n_pairs + p] = zero

    def step(i, _):
        rows_f = pl.ds(pl.multiple_of(i * GLA_CHUNK, GLA_CHUNK), GLA_CHUNK)
        rows_b = pl.ds(pl.multiple_of((n_chunks - 1 - i) * GLA_CHUNK, GLA_CHUNK), GLA_CHUNK)
        for p in range(n_pairs):
            for (rows, q_ref, k_ref, v_ref, g_ref, o_ref, slot, tri, keep, mid, last) in (
                    (rows_f, qf_ref, kf_ref, vf_ref, gf_ref, of_ref, p, tri_f, keep_f, mid_f, last_f),
                    (rows_b, qb_ref, kb_ref, vb_ref, gb_ref, ob_ref, n_pairs + p, tri_b, keep_b, mid_b, last_b)):
                st, o_even, o_odd = _gla_chunk(
                    q_ref[rows, pair_cols(p)], k_ref[rows, pair_cols(p)], g_ref[rows, pair_cols(p)],
                    v_ref[rows, head_cols(2 * p)].astype(BF16), v_ref[rows, head_cols(2 * p + 1)].astype(BF16),
                    state_ref[slot], tri, keep, mid, last)
                state_ref[slot] = st
                o_ref[rows, head_cols(2 * p)] = o_even
                o_ref[rows, head_cols(2 * p + 1)] = o_odd
        return 0

    lax.fori_loop(0, n_chunks, step, 0)


def _gla(gq, gk, gv, gf, gb, mk, mv, mg, bsz, seq, tb):
    nb = seq // tb
    hk = GLA_HEADS * GLA_DK
    hv = GLA_HEADS * GLA_DV
    fwd = lambda width: pl.BlockSpec((tb, width), lambda b, j: (b * nb + j, 0))
    bwd = lambda width: pl.BlockSpec((tb, width), lambda b, j: (b * nb + nb - 1 - j, 0))
    t = bsz * seq
    return pl.pallas_call(
        _gla_body,
        grid=(bsz, nb),
        in_specs=[fwd(hk), fwd(hk), fwd(hv), fwd(hk), bwd(hk), bwd(hk), bwd(hv), bwd(hk),
                  _full_spec(mk.shape), _full_spec(mv.shape), _full_spec(mg.shape)],
        out_specs=(fwd(hv), bwd(hv)),
        out_shape=(jax.ShapeDtypeStruct((t, hv), F32), jax.ShapeDtypeStruct((t, hv), F32)),
        scratch_shapes=[pltpu.VMEM((2 * (GLA_HEADS // 2), GLA_DV, LANES), F32)],
        compiler_params=_cparams("parallel", "arbitrary"),
        name="gla_scan",
    )(gq, gk, gv, gf, gq, gk, gv, gb, mk, mv, mg)


def _mix_body(x_ref, a_ref, of_ref, ob_ref, og_ref, gnorm_ref, wout_ref, nffn_ref, wr_hi_ref, wr_lo_ref, br_ref,
              h1_ref, hn_ref, route_ref, ids_ref, count_ref, tri_ref):
    tm = x_ref.shape[0]

    @pl.when(pl.program_id(0) == 0)
    def _():
        r = lax.broadcasted_iota(jnp.int32, (tm, tm), 0)
        c = lax.broadcasted_iota(jnp.int32, (tm, tm), 1)
        tri_ref[...] = jnp.where(c < r, 1.0, 0.0).astype(BF16)
        count_ref[...] = jnp.zeros_like(count_ref)

    a_width = MLA_HEADS * MLA_V
    h1 = x_ref[...] + _dot(a_ref[...], wout_ref[0:a_width, :])
    for h in range(GLA_HEADS):
        cols = slice(h * GLA_DV, (h + 1) * GLA_DV)
        o = of_ref[:, cols] + ob_ref[:, cols]
        og = og_ref[:, cols]
        silu = og / (1.0 + jnp.exp(-og))
        gh = (_rms(o, gnorm_ref[...]) * silu).astype(BF16)
        h1 = h1 + _dot(gh, wout_ref[a_width + h * GLA_DV:a_width + (h + 1) * GLA_DV, :])
    h1_ref[...] = h1
    hn = _rms(h1, nffn_ref[...])
    hn_ref[...] = hn

    hn_hi = hn.astype(BF16)
    hn_lo = (hn - hn_hi.astype(F32)).astype(BF16)
    logits = (_dot(hn_hi, wr_hi_ref[...]) + _dot(hn_lo, wr_hi_ref[...]) + _dot(hn_hi, wr_lo_ref[...])
              + br_ref[...])
    lane = lax.broadcasted_iota(jnp.int32, logits.shape, 1).astype(F32)
    none = float(LANES)
    neg = -jnp.inf

    def lane_max(x):
        return jnp.max(x, axis=-1, keepdims=True)

    def lane_sum(x):
        return jnp.sum(x, axis=-1, keepdims=True)

    def first_lane(mask):
        return jnp.min(jnp.where(mask, lane, none), axis=-1, keepdims=True)

    is_group = lane < float(N_GROUPS)
    g_max = lane_max(jnp.where(is_group, logits, neg))
    g_exp = jnp.where(is_group, jnp.exp(logits - g_max), 0.0)
    g_prob = g_exp / lane_sum(g_exp)
    g_w = lane_max(g_prob)
    g_idx = first_lane(is_group & (g_prob == g_w))
    e_lo = float(ROUTER_EXPERT_LANE0) + float(EXPERTS_PER_GROUP) * g_idx
    sel = (lane >= e_lo) & (lane < e_lo + float(EXPERTS_PER_GROUP))
    e_max = lane_max(jnp.where(sel, logits, neg))
    e_exp = jnp.where(sel, jnp.exp(logits - e_max), 0.0)
    e_prob = e_exp / lane_sum(e_exp)
    p1 = lane_max(jnp.where(sel, e_prob, neg))
    i1 = first_lane(sel & (e_prob == p1))
    rest = sel & (lane != i1)
    p2 = lane_max(jnp.where(rest, e_prob, neg))
    i2 = first_lane(rest & (e_prob == p2))
    denom = p1 + p2
    chosen = jnp.where((lane == i1) | (lane == i2), 1.0, 0.0)
    rank = count_ref[...] + _dot(tri_ref[...], chosen.astype(BF16))
    count_ref[...] += jnp.sum(chosen, axis=0, keepdims=True)
    fields = (i1 - float(ROUTER_EXPERT_LANE0), i2 - float(ROUTER_EXPERT_LANE0),
              lane_sum(jnp.where(lane == i1, rank, 0.0)), lane_sum(jnp.where(lane == i2, rank, 0.0)),
              g_w * (p1 / denom), g_w * (p2 / denom))
    route = jnp.zeros_like(logits)
    for k, value in enumerate(fields):
        route = jnp.where(lane == float(k), value, route)
    route_ref[...] = route
    ids_ref[...] = jnp.transpose(route)[0:ROUTE_ID_ROWS, :].astype(jnp.int32)


ROUTE_E1, ROUTE_E2, ROUTE_RANK1, ROUTE_RANK2, ROUTE_W1, ROUTE_W2 = range(6)
ROUTE_ID_ROWS = 8


def _mix(x2d, a, o_f, o_b, og, w, tm):
    t = x2d.shape[0]
    hv = GLA_HEADS * GLA_DV
    row = lambda width: pl.BlockSpec((tm, width), lambda i: (i, 0))
    return pl.pallas_call(
        _mix_body,
        grid=(t // tm,),
        in_specs=[row(D_MODEL), row(MLA_HEADS * MLA_V), row(hv), row(hv), row(hv),
                  _full_spec((1, GLA_DV)), _full_spec(w['wout'].shape), _full_spec((1, D_MODEL)),
                  _full_spec(w['wr_hi'].shape), _full_spec(w['wr_lo'].shape), _full_spec(w['br'].shape)],
        out_specs=(row(D_MODEL), row(D_MODEL), row(LANES),
                   pl.BlockSpec((ROUTE_ID_ROWS, tm), lambda i: (0, i)), _full_spec((1, LANES))),
        out_shape=(jax.ShapeDtypeStruct((t, D_MODEL), F32), jax.ShapeDtypeStruct((t, D_MODEL), F32),
                   jax.ShapeDtypeStruct((t, LANES), F32), jax.ShapeDtypeStruct((ROUTE_ID_ROWS, t), jnp.int32),
                   jax.ShapeDtypeStruct((1, LANES), F32)),
        scratch_shapes=[pltpu.VMEM((tm, tm), BF16)],
        compiler_params=_cparams("arbitrary"),
        name="mix_router",
    )(x2d, a, o_f, o_b, og, w['gla_norm'], w['wout'], w['norm_ffn'], w['wr_hi'], w['wr_lo'], w['br'])


EXPERT_TILE = 256
ROW_TILE = 256


SUBLANES = 8
ROW_CHUNKS = D_MODEL // LANES
assert ROW_CHUNKS == SUBLANES


def _row_copy(src_ref, src_row, dst_ref, dst_row, sem):
    return pltpu.make_async_copy(src_ref.at[pl.ds(src_row, 1)], dst_ref.at[pl.ds(dst_row, 1)], sem)


def _tile_copy(src_ref, src_row, dst_ref, dst_row, sem):
    return pltpu.make_async_copy(src_ref.at[src_row], dst_ref.at[dst_row], sem)


def _rows_to_tiles(x):
    chunks = jnp.stack([x[:, s * LANES:(s + 1) * LANES] for s in range(ROW_CHUNKS)], axis=0)
    return pltpu.einshape("smd->msd", chunks)


def _tiles_to_rows(x):
    chunks = pltpu.einshape("msd->smd", x)
    return jnp.concatenate([chunks[s] for s in range(ROW_CHUNKS)], axis=-1)


def _slot(starts_ref, ids_ref, k, r):
    return starts_ref[ids_ref[ROUTE_E1 + k, r]] + ids_ref[ROUTE_RANK1 + k, r]


def _dispatch_body(starts_ref, fill_ref, n_tiles_ref, ids_ref, hn_ref, zeros_ref, xs_ref, stage_ref, sem, fill_sem):
    i = pl.program_id(0)
    last = pl.num_programs(0) - 1
    cur = i % 2

    def wait_rows(half):
        for _ in range(2 * ROW_TILE):
            _tile_copy(stage_ref.at[half], 0, xs_ref, 0, sem.at[half]).wait()

    @pl.when(i == 0)
    def _():
        def fill(start):
            return pltpu.make_async_copy(zeros_ref, xs_ref.at[pl.ds(pl.multiple_of(start, EXPERT_TILE), EXPERT_TILE)],
                                         fill_sem)
        unused = (n_tiles_ref[0], xs_ref.shape[0] // EXPERT_TILE)
        for e in range(N_EXPERTS):
            fill(fill_ref[e]).start()
        lax.fori_loop(*unused, lambda j, c: (fill(j * EXPERT_TILE).start(), c)[1], 0)
        for e in range(N_EXPERTS):
            fill(fill_ref[e]).wait()
        lax.fori_loop(*unused, lambda j, c: (fill(j * EXPERT_TILE).wait(), c)[1], 0)

    @pl.when(i >= 2)
    def _():
        wait_rows(cur)

    stage_ref[cur] = _rows_to_tiles(hn_ref[...])

    def issue(r, carry):
        for k in range(2):
            _tile_copy(stage_ref.at[cur], r, xs_ref, _slot(starts_ref, ids_ref, k, r), sem.at[cur]).start()
        return carry

    lax.fori_loop(0, ROW_TILE, issue, 0, unroll=8)

    @pl.when(i == last)
    def _():
        wait_rows(1 - cur)
        wait_rows(cur)


def _dispatch(starts, fill_starts, n_tiles, ids, hn, n_rows):
    t = hn.shape[0]
    assert t // ROW_TILE >= 2
    zeros = jnp.zeros((EXPERT_TILE, ROW_CHUNKS, LANES), F32)
    return pl.pallas_call(
        _dispatch_body,
        grid_spec=pltpu.PrefetchScalarGridSpec(
            num_scalar_prefetch=3,
            grid=(t // ROW_TILE,),
            in_specs=[pl.BlockSpec((ROUTE_ID_ROWS, ROW_TILE), lambda i, *_: (0, i), memory_space=pltpu.SMEM),
                      pl.BlockSpec((ROW_TILE, D_MODEL), lambda i, *_: (i, 0)),
                      pl.BlockSpec(memory_space=pl.ANY)],
            out_specs=pl.BlockSpec(memory_space=pl.ANY),
            scratch_shapes=[pltpu.VMEM((2, ROW_TILE, ROW_CHUNKS, LANES), F32), pltpu.SemaphoreType.DMA((2,)),
                            pltpu.SemaphoreType.DMA(())]),
        out_shape=jax.ShapeDtypeStruct((n_rows, ROW_CHUNKS, LANES), F32),
        compiler_params=_cparams("arbitrary"),
        name="moe_dispatch",
    )(starts, fill_starts, n_tiles, ids, hn, zeros)


def _expert_body(tile_expert_ref, n_tiles_ref, xs_ref, wgu_ref, wd_ref, ys_ref):
    used = pl.program_id(0) < n_tiles_ref[0]

    @pl.when(used)
    def _():
        h = _dot(_tiles_to_rows(xs_ref[...]).astype(BF16), wgu_ref[...])
        hg = h[:, :D_EXPERT]
        act = (hg / (1.0 + jnp.exp(-hg))) * h[:, D_EXPERT:]
        ys_ref[...] = _dot(act.astype(BF16), wd_ref[...])

    @pl.when(jnp.logical_not(used))
    def _():
        ys_ref[...] = jnp.zeros_like(ys_ref)


def _experts(tile_expert, n_tiles, xs, w):
    n_rows = xs.shape[0]
    grid = n_rows // EXPERT_TILE

    def tile(i, tile_expert, n_tiles):
        return jnp.minimum(i, n_tiles[0] - 1)

    return pl.pallas_call(
        _expert_body,
        grid_spec=pltpu.PrefetchScalarGridSpec(
            num_scalar_prefetch=2,
            grid=(grid,),
            in_specs=[pl.BlockSpec((EXPERT_TILE, ROW_CHUNKS, LANES), lambda i, te, nt: (tile(i, te, nt), 0, 0)),
                      pl.BlockSpec((None, D_MODEL, 2 * D_EXPERT), lambda i, te, nt: (te[tile(i, te, nt)], 0, 0)),
                      pl.BlockSpec((None, D_EXPERT, D_MODEL), lambda i, te, nt: (te[tile(i, te, nt)], 0, 0))],
            out_specs=pl.BlockSpec((EXPERT_TILE, D_MODEL), lambda i, te, nt: (i, 0))),
        out_shape=jax.ShapeDtypeStruct((n_rows, D_MODEL), F32),
        compiler_params=_cparams("arbitrary"),
        name="moe_experts",
    )(tile_expert, n_tiles, xs, w['wgu'], w['wd'])


def _combine_body(starts_ref, ids_ref, next_ids_ref, h1_ref, route_ref, nfin_ref, ys_ref, out_ref, buf_ref, sem):
    i = pl.program_id(0)
    cur = i % 2

    def issue(ids, half):
        def body(r, carry):
            for k in range(2):
                _row_copy(ys_ref, _slot(starts_ref, ids, k, r), buf_ref.at[half, k], r, sem.at[half]).start()
            return carry
        lax.fori_loop(0, ROW_TILE, body, 0, unroll=8)

    @pl.when(i == 0)
    def _():
        issue(ids_ref, 0)

    @pl.when(i + 1 < pl.num_programs(0))
    def _():
        issue(next_ids_ref, 1 - cur)

    for _ in range(2 * ROW_TILE):
        _row_copy(ys_ref, 0, buf_ref.at[cur, 0], 0, sem.at[cur]).wait()
    route = route_ref[...]
    lane = lax.broadcasted_iota(jnp.int32, route.shape, 1)
    w1 = jnp.sum(jnp.where(lane == ROUTE_W1, route, 0.0), axis=-1, keepdims=True)
    w2 = jnp.sum(jnp.where(lane == ROUTE_W2, route, 0.0), axis=-1, keepdims=True)
    out_ref[...] = _rms(h1_ref[...] + w1 * buf_ref[cur, 0] + w2 * buf_ref[cur, 1], nfin_ref[...])


def _combine(starts, ids, h1, route, ys, w):
    t = h1.shape[0]
    n_steps = t // ROW_TILE
    row = lambda width: pl.BlockSpec((ROW_TILE, width), lambda i, *_: (i, 0))
    ids_block = lambda index: pl.BlockSpec((ROUTE_ID_ROWS, ROW_TILE), index, memory_space=pltpu.SMEM)
    return pl.pallas_call(
        _combine_body,
        grid_spec=pltpu.PrefetchScalarGridSpec(
            num_scalar_prefetch=1,
            grid=(n_steps,),
            in_specs=[ids_block(lambda i, *_: (0, i)), ids_block(lambda i, *_: (0, jnp.minimum(i + 1, n_steps - 1))),
                      row(D_MODEL), row(LANES), pl.BlockSpec((1, D_MODEL), lambda i, *_: (0, 0)),
                      pl.BlockSpec(memory_space=pl.ANY)],
            out_specs=row(D_MODEL),
            scratch_shapes=[pltpu.VMEM((2, 2, ROW_TILE, D_MODEL), F32), pltpu.SemaphoreType.DMA((2,))]),
        out_shape=jax.ShapeDtypeStruct((t, D_MODEL), F32),
        compiler_params=_cparams("arbitrary"),
        name="moe_combine",
    )(starts, ids, ids, h1, route, w['norm_final'], ys)


def _moe(hn, route, ids, counts, h1, w):
    t = hn.shape[0]
    n_rows = 2 * t + N_EXPERTS * EXPERT_TILE
    n_grid_tiles = n_rows // EXPERT_TILE
    count = counts[0, ROUTER_EXPERT_LANE0:ROUTER_EXPERT_LANE0 + N_EXPERTS].astype(jnp.int32)
    padded = jnp.maximum((count + EXPERT_TILE - 1) // EXPERT_TILE, 1) * EXPERT_TILE
    ends = jnp.sum(jnp.where(jnp.arange(N_EXPERTS)[:, None] <= jnp.arange(N_EXPERTS)[None, :], padded[:, None], 0),
                   axis=0)
    starts = ends - padded
    n_tiles = (ends[-1:] // EXPERT_TILE)
    tile_rows = jnp.arange(n_grid_tiles, dtype=jnp.int32) * EXPERT_TILE
    tile_expert = jnp.minimum(jnp.sum((ends[None, :] <= tile_rows[:, None]).astype(jnp.int32), axis=1), N_EXPERTS - 1)
    xs = _dispatch(starts, ends - EXPERT_TILE, n_tiles, ids, hn, n_rows)
    ys = _experts(tile_expert, n_tiles, xs, w)
    return _combine(starts, ids, h1, route, ys, w)


def _rope_tables(positions):
    inv_freq = 1.0 / (ROPE_THETA ** (jnp.arange(0, MLA_ROPE, 2, dtype=F32) / MLA_ROPE))
    ang = positions.astype(F32)[:, None] * inv_freq[None, :]
    cos, sin = jnp.cos(ang), jnp.sin(ang)
    reps = LANES // MLA_ROPE
    return jnp.tile(jnp.concatenate([cos, cos], axis=-1), (1, reps)), jnp.tile(jnp.concatenate([-sin, sin], axis=-1), (1, reps))


def _pack_weights(norm_mix, w_in, q_a_norm, w_uq, kv_a_norm, w_ukv, w_gate_fwd, b_gate_fwd, w_gate_bwd, b_gate_bwd,
                  gla_norm, w_out, norm_ffn, w_router_group, b_router_group, w_router_expert, b_router_expert,
                  w_expert_gate, w_expert_up, w_expert_down, norm_final):
    l = 0
    hk = GLA_HEADS * GLA_DK
    hv = GLA_HEADS * GLA_DV
    c_q, c_kv, k_pe, gq, gk, gv, lr_f, lr_b, og = jnp.split(
        w_in[l], np.cumsum([MLA_Q_RANK, MLA_KV_RANK, MLA_ROPE, hk, hk, hv, GLA_GATE_RANK, GLA_GATE_RANK])[:].tolist(),
        axis=-1)
    lr_pad = jnp.zeros((D_MODEL, LANES - 2 * GLA_GATE_RANK), F32)
    win = jnp.concatenate([c_q, c_kv, k_pe, k_pe, gq, gk, gv, og, lr_f, lr_b, lr_pad], axis=-1).astype(BF16)
    wuq = w_uq[l].reshape(MLA_Q_RANK, MLA_HEADS, MLA_NOPE + MLA_ROPE)
    wuq = jnp.concatenate([wuq[:, :, :MLA_NOPE].reshape(MLA_Q_RANK, -1), wuq[:, :, MLA_NOPE:].reshape(MLA_Q_RANK, -1)],
                          axis=-1).astype(BF16)
    wgate = jnp.zeros((LANES, 2 * hk), F32)
    wgate = wgate.at[0:GLA_GATE_RANK, 0:hk].set(w_gate_fwd[l])
    wgate = wgate.at[GLA_GATE_RANK:2 * GLA_GATE_RANK, hk:].set(w_gate_bwd[l])
    wr = jnp.zeros((D_MODEL, LANES), F32)
    wr = wr.at[:, ROUTER_GROUP_LANE0:ROUTER_GROUP_LANE0 + N_GROUPS].set(w_router_group[l])
    wr = wr.at[:, ROUTER_EXPERT_LANE0:ROUTER_EXPERT_LANE0 + N_EXPERTS].set(w_router_expert[l])
    wr_hi = wr.astype(BF16)
    br = jnp.zeros((1, LANES), F32)
    br = br.at[0, ROUTER_GROUP_LANE0:ROUTER_GROUP_LANE0 + N_GROUPS].set(b_router_group[l])
    br = br.at[0, ROUTER_EXPERT_LANE0:ROUTER_EXPERT_LANE0 + N_EXPERTS].set(b_router_expert[l])
    wgu = jnp.concatenate([w_expert_gate[l], w_expert_up[l]], axis=-1).reshape(N_EXPERTS, D_MODEL, 2 * D_EXPERT)
    return {
        'norm_mix': norm_mix[l][None], 'win': win, 'q_a_norm': q_a_norm[l][None], 'wuq': wuq,
        'kv_a_norm': kv_a_norm[l][None], 'wukv': w_ukv[l].astype(BF16),
        'wgate': wgate.astype(BF16), 'bgate': jnp.concatenate([b_gate_fwd[l], b_gate_bwd[l]])[None],
        'gla_norm': gla_norm[l][None], 'wout': w_out[l].astype(BF16), 'norm_ffn': norm_ffn[l][None],
        'wr_hi': wr_hi, 'wr_lo': (wr - wr_hi.astype(F32)).astype(BF16), 'br': br,
        'wgu': wgu.astype(BF16), 'wd': w_expert_down[l].reshape(N_EXPERTS, D_EXPERT, D_MODEL).astype(BF16),
        'norm_final': norm_final[None],
    }


def _meta_streams(meta_tokens, w):
    cos, sin = _rope_tables(jnp.arange(N_META))
    _, k, v, _, gk, gv, gf, _, _ = _inproj(meta_tokens, cos, sin, w, N_META)
    pad_keys = ((0, 0), (0, LANES - N_META), (0, 0))
    front = ((GLA_CHUNK - N_META, 0), (0, 0))
    return (jnp.pad(k, pad_keys), jnp.pad(v, pad_keys), jnp.pad(gk, front), jnp.pad(gv, front), jnp.pad(gf, front))


def _token_mixers(x, meta, w, tm, tq, tk, tb):
    bsz, seq, _ = x.shape
    km, vm, mk, mv, mg = meta
    x2d = x.reshape(bsz * seq, D_MODEL)
    cos, sin = _rope_tables(N_META + jnp.arange(seq))
    q, k, v, gq, gk, gv, gf, gb, og = _inproj(x2d, cos, sin, w, tm)
    a = _attention(q, k, v, km, vm, bsz, seq, tq, tk)
    o_f, o_b = _gla(gq, gk, gv, gf, gb, mk, mv, mg, bsz, seq, tb)
    return _mix(x2d, a, o_f, o_b, og, w, tm)


def kernel(x_prompt, x_sample, meta_tokens, norm_mix, w_in, q_a_norm, w_uq, kv_a_norm, w_ukv, w_gate_fwd, b_gate_fwd, w_gate_bwd, b_gate_bwd, gla_norm, w_out, norm_ffn, w_router_group, b_router_group, w_router_expert, b_router_expert, w_expert_gate, w_expert_up, w_expert_down, norm_final):
    w = _pack_weights(norm_mix, w_in, q_a_norm, w_uq, kv_a_norm, w_ukv, w_gate_fwd, b_gate_fwd, w_gate_bwd,
                      b_gate_bwd, gla_norm, w_out, norm_ffn, w_router_group, b_router_group, w_router_expert,
                      b_router_expert, w_expert_gate, w_expert_up, w_expert_down, norm_final)
    meta = _meta_streams(meta_tokens, w)
    outs = []
    for x in (x_prompt, x_sample):
        h1, hn, route, ids, counts = _token_mixers(x, meta, w, tm=512, tq=512, tk=512, tb=512)
        outs.append(_moe(hn, route, ids, counts, h1, w).reshape(x.shape))
    return tuple(outs)
```

```python
import functools

import numpy as np
import jax
import jax.numpy as jnp
from jax import lax
from jax.experimental import pallas as pl
from jax.experimental.pallas import tpu as pltpu

F32 = jnp.float32
BF16 = jnp.bfloat16

D_MODEL = 1024
N_META = 16
MLA_HEADS = 4
MLA_Q_RANK = 384
MLA_KV_RANK = 256
MLA_NOPE = 128
MLA_ROPE = 64
MLA_V = 128
ROPE_THETA = 10000.0
GLA_HEADS = 4
GLA_DK = 64
GLA_DV = 128
GLA_GATE_RANK = 16
GLA_TAU = 16.0
GLA_CHUNK = 64
N_GROUPS = 4
EXPERTS_PER_GROUP = 8
N_EXPERTS = N_GROUPS * EXPERTS_PER_GROUP
D_EXPERT = 256
EPS = 1e-6

LANES = 128
V7X_VMEM_BYTES = 64 * 1024 * 1024
VMEM_LIMIT = V7X_VMEM_BYTES * 7 // 8

ATTN_SCALE = (MLA_NOPE + MLA_ROPE) ** -0.5 * float(np.log2(np.e))
QK_WIDTH = 2 * LANES
V_WIDTH = 2 * LANES
ATTN_GROUP = 4

C_CQ = 0
C_CKV = C_CQ + MLA_Q_RANK
C_KPE = C_CKV + MLA_KV_RANK
C_GQ = C_KPE + LANES
C_GK = C_GQ + GLA_HEADS * GLA_DK
C_GV = C_GK + GLA_HEADS * GLA_DK
C_OG = C_GV + GLA_HEADS * GLA_DV
C_LR = C_OG + GLA_HEADS * GLA_DV
D_IN_PACKED = C_LR + LANES

ROUTER_GROUP_LANE0 = 0
ROUTER_EXPERT_LANE0 = N_GROUPS


def _cparams(*semantics):
    return pltpu.CompilerParams(dimension_semantics=semantics, vmem_limit_bytes=VMEM_LIMIT)


def _rms(x, g):
    return x * lax.rsqrt(jnp.mean(x * x, axis=-1, keepdims=True) + EPS) * g


def _dot(a, b):
    return jnp.dot(a, b, preferred_element_type=F32)


def _dot_nt(a, b):
    return lax.dot_general(a, b, (((1,), (1,)), ((), ())), preferred_element_type=F32)


def _dot_tn(a, b):
    return lax.dot_general(a, b, (((0,), (0,)), ((), ())), preferred_element_type=F32)


def _full_spec(shape):
    return pl.BlockSpec(shape, lambda *_: (0,) * len(shape))


def _rope_pairs(x, cos, sin_signed, first_half):
    swapped = jnp.where(first_half, pltpu.roll(x, LANES - MLA_ROPE // 2, 1), pltpu.roll(x, MLA_ROPE // 2, 1))
    return x * cos + swapped * sin_signed


def _inproj_body(x_ref, cos_ref, sin_ref, nmix_ref, win_ref, qan_ref, wuq_ref, kvan_ref, wukv_ref,
                 wgate_ref, bgate_ref,
                 q_ref, k_ref, v_ref, gq_ref, gk_ref, gv_ref, gf_ref, gb_ref, og_ref):
    hn = _rms(x_ref[...], nmix_ref[...]).astype(BF16)

    def proj(lo, hi):
        return _dot(hn, win_ref[:, lo:hi])

    cos = cos_ref[...]
    sin = sin_ref[...]
    lane = lax.broadcasted_iota(jnp.int32, cos.shape, 1)
    first_half = (lane & (MLA_ROPE - 1)) < MLA_ROPE // 2
    low_lanes = lane < MLA_ROPE

    cq = _rms(proj(C_CQ, C_CKV), qan_ref[...]).astype(BF16)
    qn = _dot(cq, wuq_ref[:, 0:MLA_HEADS * MLA_NOPE]) * ATTN_SCALE
    qr = _dot(cq, wuq_ref[:, MLA_HEADS * MLA_NOPE:])
    for j in range(MLA_HEADS // 2):
        rj = (_rope_pairs(qr[:, j * LANES:(j + 1) * LANES], cos, sin, first_half) * ATTN_SCALE).astype(BF16)
        for h in (2 * j, 2 * j + 1):
            q_ref[h, :, 0:LANES] = qn[:, h * LANES:(h + 1) * LANES].astype(BF16)
            q_ref[h, :, LANES:QK_WIDTH] = rj

    ckv = _rms(proj(C_CKV, C_KPE), kvan_ref[...]).astype(BF16)
    kv = _dot(ckv, wukv_ref[...])
    kr = _rope_pairs(proj(C_KPE, C_GQ), cos, sin, first_half)
    kr_even = jnp.where(low_lanes, kr, 0.0).astype(BF16)
    kr_odd = jnp.where(low_lanes, 0.0, kr).astype(BF16)
    for h in range(MLA_HEADS):
        base = h * (MLA_NOPE + MLA_V)
        k_ref[h, :, 0:LANES] = kv[:, base:base + MLA_NOPE].astype(BF16)
        k_ref[h, :, LANES:QK_WIDTH] = kr_even if h % 2 == 0 else kr_odd
        v_ref[h, :, 0:MLA_V] = kv[:, base + MLA_NOPE:base + MLA_NOPE + MLA_V].astype(BF16)
        v_ref[h, :, MLA_V:V_WIDTH] = jnp.ones((kv.shape[0], V_WIDTH - MLA_V), BF16)

    gq_ref[...] = proj(C_GQ, C_GK) * (GLA_DK ** -0.5)
    gk_ref[...] = proj(C_GK, C_GV)
    gv_ref[...] = proj(C_GV, C_OG)
    og_ref[...] = proj(C_OG, C_LR)
    pre = _dot(proj(C_LR, D_IN_PACKED).astype(BF16), wgate_ref[...]) + bgate_ref[...]
    logsig = jnp.minimum(pre, 0.0) - jnp.log1p(jnp.exp(-jnp.abs(pre)))
    gates = logsig * (1.0 / GLA_TAU)
    gf_ref[...] = gates[:, 0:GLA_HEADS * GLA_DK]
    gb_ref[...] = gates[:, GLA_HEADS * GLA_DK:]


def _inproj(x2d, cos, sin, w, tm):
    t = x2d.shape[0]
    blocks_per_seq = cos.shape[0] // tm
    hk = GLA_HEADS * GLA_DK
    hv = GLA_HEADS * GLA_DV
    row = lambda width: pl.BlockSpec((tm, width), lambda i: (i, 0))
    head_rows = lambda width: pl.BlockSpec((MLA_HEADS, tm, width), lambda i: (0, i, 0))
    tab = pl.BlockSpec((tm, LANES), lambda i: (i % blocks_per_seq, 0))
    out_shape = (
        jax.ShapeDtypeStruct((MLA_HEADS, t, QK_WIDTH), BF16),
        jax.ShapeDtypeStruct((MLA_HEADS, t, QK_WIDTH), BF16),
        jax.ShapeDtypeStruct((MLA_HEADS, t, V_WIDTH), BF16),
        jax.ShapeDtypeStruct((t, hk), F32),
        jax.ShapeDtypeStruct((t, hk), F32),
        jax.ShapeDtypeStruct((t, hv), F32),
        jax.ShapeDtypeStruct((t, hk), F32),
        jax.ShapeDtypeStruct((t, hk), F32),
        jax.ShapeDtypeStruct((t, hv), F32),
    )
    return pl.pallas_call(
        _inproj_body,
        grid=(t // tm,),
        in_specs=[row(D_MODEL), tab, tab,
                  _full_spec((1, D_MODEL)), _full_spec((D_MODEL, D_IN_PACKED)),
                  _full_spec((1, MLA_Q_RANK)), _full_spec(w['wuq'].shape),
                  _full_spec((1, MLA_KV_RANK)), _full_spec(w['wukv'].shape),
                  _full_spec(w['wgate'].shape), _full_spec(w['bgate'].shape)],
        out_specs=(head_rows(QK_WIDTH), head_rows(QK_WIDTH), head_rows(V_WIDTH),
                   row(hk), row(hk), row(hv), row(hk), row(hk), row(hv)),
        out_shape=out_shape,
        compiler_params=_cparams("parallel"),
        name="inproj",
    )(x2d, cos, sin, w['norm_mix'], w['win'], w['q_a_norm'], w['wuq'], w['kv_a_norm'], w['wukv'],
      w['wgate'], w['bgate'])


def _attn_body(q_ref, k_ref, v_ref, km_ref, vm_ref, o_ref, s_ref, acc_ref, *, tk):
    q = q_ref[...]
    n_groups = k_ref.shape[0] // (ATTN_GROUP * tk)

    def scores(j):
        return _dot_nt(q, k_ref[pl.ds(pl.multiple_of(j * tk, tk), tk), :])

    def values(j):
        return v_ref[pl.ds(pl.multiple_of(j * tk, tk), tk), :]

    def absorb(m, s, v):
        m_new = jnp.maximum(m, jnp.max(s, axis=-1, keepdims=True))
        p = jnp.exp2(s - m_new)
        acc_ref[...] = jnp.exp2(m - m_new) * acc_ref[...] + _dot(p.astype(BF16), v)
        return m_new

    s_ref[0] = scores(0)
    sm = _dot_nt(q, km_ref[...])
    sm = jnp.where(lax.broadcasted_iota(jnp.int32, sm.shape, 1) < N_META, sm, -jnp.inf)
    m = jnp.max(sm, axis=-1, keepdims=True)
    acc_ref[...] = _dot(jnp.exp2(sm - m).astype(BF16), vm_ref[...])

    def group(g, m, last):
        for i in range(ATTN_GROUP):
            j = ATTN_GROUP * g + i
            s = s_ref[i % 2]
            if not (last and i == ATTN_GROUP - 1):
                s_ref[(i + 1) % 2] = scores(j + 1)
            m = absorb(m, s, values(j))
        return m

    m = lax.fori_loop(0, n_groups - 1, lambda g, m: group(g, m, False), m)
    group(n_groups - 1, m, True)
    acc = acc_ref[...]
    o_ref[...] = (acc[:, :MLA_V] / acc[:, MLA_V:]).astype(o_ref.dtype)


def _attention(q, k, v, km, vm, bsz, seq, tq, tk):
    nq = seq // tq
    return pl.pallas_call(
        functools.partial(_attn_body, tk=tk),
        grid=(bsz, MLA_HEADS, nq),
        in_specs=[pl.BlockSpec((None, tq, QK_WIDTH), lambda b, h, i: (h, b * nq + i, 0)),
                  pl.BlockSpec((None, seq, QK_WIDTH), lambda b, h, i: (h, b, 0)),
                  pl.BlockSpec((None, seq, V_WIDTH), lambda b, h, i: (h, b, 0)),
                  pl.BlockSpec((None, LANES, QK_WIDTH), lambda b, h, i: (h, 0, 0)),
                  pl.BlockSpec((None, LANES, V_WIDTH), lambda b, h, i: (h, 0, 0))],
        out_specs=pl.BlockSpec((tq, MLA_V), lambda b, h, i: (b * nq + i, h)),
        out_shape=jax.ShapeDtypeStruct((bsz * seq, MLA_HEADS * MLA_V), BF16),
        scratch_shapes=[pltpu.VMEM((2, tq, tk), F32), pltpu.VMEM((tq, V_WIDTH), F32)],
        compiler_params=_cparams("parallel", "parallel", "arbitrary"),
        name="mla_attention",
    )(q, k, v, km, vm)


def _split3(x):
    hi = x.astype(BF16)
    r1 = x - hi.astype(F32)
    mid = r1.astype(BF16)
    lo = (r1 - mid.astype(F32)).astype(BF16)
    return hi, mid, lo


def _gla_chunk(q, k, g, v_even, v_odd, state, tri, keep, mid, last):
    lane = lax.broadcasted_iota(jnp.int32, (GLA_CHUNK, LANES), 1)
    even = lane < GLA_DK
    g_hi, g_mid, g_lo = _split3(g)
    b = _dot(tri, g_hi) + _dot(tri, g_mid) + _dot(tri, g_lo)
    b_last = b[last:last + 1, :]
    ke = (k * jnp.exp(b_last - b)).astype(BF16)
    upd = jnp.where(lax.broadcasted_iota(jnp.int32, (GLA_DV, LANES), 1) < GLA_DK,
                    _dot_tn(v_even, ke), _dot_tn(v_odd, ke))
    new_state = state * jnp.exp(b_last) + upd
    if q is None:
        return new_state, None, None
    b_mid = b[mid:mid + 1, :]
    qs = q * jnp.exp(b - b_mid)
    ks = (k * jnp.exp(b_mid - b)).astype(BF16)
    qs2 = jnp.concatenate([jnp.where(even, qs, 0.0), jnp.where(even, 0.0, qs)], axis=0).astype(BF16)
    scores = jnp.where(keep, _dot_nt(qs2, ks), 0.0).astype(BF16)
    qe = q * jnp.exp(b)
    qe2 = jnp.concatenate([jnp.where(even, qe, 0.0), jnp.where(even, 0.0, qe)], axis=0).astype(BF16)
    inter = _dot_nt(qe2, state.astype(BF16))
    o_even = _dot(scores[:GLA_CHUNK], v_even) + inter[:GLA_CHUNK]
    o_odd = _dot(scores[GLA_CHUNK:], v_odd) + inter[GLA_CHUNK:]
    return new_state, o_even, o_odd


def _gla_body(qf_ref, kf_ref, vf_ref, gf_ref, qb_ref, kb_ref, vb_ref, gb_ref, mk_ref, mv_ref, mg_ref,
              of_ref, ob_ref, state_ref):
    n_chunks = qf_ref.shape[0] // GLA_CHUNK
    n_pairs = GLA_HEADS // 2
    r = lax.broadcasted_iota(jnp.int32, (GLA_CHUNK, GLA_CHUNK), 0)
    c = lax.broadcasted_iota(jnp.int32, (GLA_CHUNK, GLA_CHUNK), 1)
    tri_f = jnp.where(c <= r, 1.0, 0.0).astype(BF16)
    tri_b = jnp.where(c >= r, 1.0, 0.0).astype(BF16)
    r2 = lax.broadcasted_iota(jnp.int32, (2 * GLA_CHUNK, GLA_CHUNK), 0) & (GLA_CHUNK - 1)
    c2 = lax.broadcasted_iota(jnp.int32, (2 * GLA_CHUNK, GLA_CHUNK), 1)
    keep_f = c2 <= r2
    keep_b = c2 >= r2
    mid_f, last_f = GLA_CHUNK // 2 - 1, GLA_CHUNK - 1
    mid_b, last_b = GLA_CHUNK // 2, 0

    def pair_cols(p):
        return slice(p * LANES, (p + 1) * LANES)

    def head_cols(h):
        return slice(h * GLA_DV, (h + 1) * GLA_DV)

    @pl.when(pl.program_id(1) == 0)
    def _():
        for p in range(n_pairs):
            zero = jnp.zeros((GLA_DV, LANES), F32)
            st, _, _ = _gla_chunk(None, mk_ref[:, pair_cols(p)], mg_ref[:, pair_cols(p)],
                                  mv_ref[:, head_cols(2 * p)].astype(BF16),
                                  mv_ref[:, head_cols(2 * p + 1)].astype(BF16),
                                  zero, tri_f, keep_f, mid_f, last_f)
            state_ref[p] = st
            state_ref[n_pairs + p] = zero

    def step(i, _):
        rows_f = pl.ds(pl.multiple_of(i * GLA_CHUNK, GLA_CHUNK), GLA_CHUNK)
        rows_b = pl.ds(pl.multiple_of((n_chunks - 1 - i) * GLA_CHUNK, GLA_CHUNK), GLA_CHUNK)
        for p in range(n_pairs):
            for (rows, q_ref, k_ref, v_ref, g_ref, o_ref, slot, tri, keep, mid, last) in (
                    (rows_f, qf_ref, kf_ref, vf_ref, gf_ref, of_ref, p, tri_f, keep_f, mid_f, last_f),
                    (rows_b, qb_ref, kb_ref, vb_ref, gb_ref, ob_ref, n_pairs + p, tri_b, keep_b, mid_b, last_b)):
                st, o_even, o_odd = _gla_chunk(
                    q_ref[rows, pair_cols(p)], k_ref[rows, pair_cols(p)], g_ref[rows, pair_cols(p)],
                    v_ref[rows, head_cols(2 * p)].astype(BF16), v_ref[rows, head_cols(2 * p + 1)].astype(BF16),
                    state_ref[slot], tri, keep, mid, last)
                state_ref[slot] = st
                o_ref[rows, head_cols(2 * p)] = o_even
                o_ref[rows, head_cols(2 * p + 1)] = o_odd
        return 0

    lax.fori_loop(0, n_chunks, step, 0)


def _gla(gq, gk, gv, gf, gb, mk, mv, mg, bsz, seq, tb):
    nb = seq // tb
    hk = GLA_HEADS * GLA_DK
    hv = GLA_HEADS * GLA_DV
    fwd = lambda width: pl.BlockSpec((tb, width), lambda b, j: (b * nb + j, 0))
    bwd = lambda width: pl.BlockSpec((tb, width), lambda b, j: (b * nb + nb - 1 - j, 0))
    t = bsz * seq
    return pl.pallas_call(
        _gla_body,
        grid=(bsz, nb),
        in_specs=[fwd(hk), fwd(hk), fwd(hv), fwd(hk), bwd(hk), bwd(hk), bwd(hv), bwd(hk),
                  _full_spec(mk.shape), _full_spec(mv.shape), _full_spec(mg.shape)],
        out_specs=(fwd(hv), bwd(hv)),
        out_shape=(jax.ShapeDtypeStruct((t, hv), F32), jax.ShapeDtypeStruct((t, hv), F32)),
        scratch_shapes=[pltpu.VMEM((2 * (GLA_HEADS // 2), GLA_DV, LANES), F32)],
        compiler_params=_cparams("parallel", "arbitrary"),
        name="gla_scan",
    )(gq, gk, gv, gf, gq, gk, gv, gb, mk, mv, mg)


def _mix_body(x_ref, a_ref, of_ref, ob_ref, og_ref, gnorm_ref, wout_ref, nffn_ref, wr_hi_ref, wr_lo_ref, br_ref,
              h1_ref, hn_ref, route_ref, ids_ref, count_ref, tri_ref):
    tm = x_ref.shape[0]

    @pl.when(pl.program_id(0) == 0)
    def _():
        r = lax.broadcasted_iota(jnp.int32, (tm, tm), 0)
        c = lax.broadcasted_iota(jnp.int32, (tm, tm), 1)
        tri_ref[...] = jnp.where(c < r, 1.0, 0.0).astype(BF16)
        count_ref[...] = jnp.zeros_like(count_ref)

    a_width = MLA_HEADS * MLA_V
    h1 = x_ref[...] + _dot(a_ref[...], wout_ref[0:a_width, :])
    for h in range(GLA_HEADS):
        cols = slice(h * GLA_DV, (h + 1) * GLA_DV)
        o = of_ref[:, cols] + ob_ref[:, cols]
        og = og_ref[:, cols]
        silu = og / (1.0 + jnp.exp(-og))
        gh = (_rms(o, gnorm_ref[...]) * silu).astype(BF16)
        h1 = h1 + _dot(gh, wout_ref[a_width + h * GLA_DV:a_width + (h + 1) * GLA_DV, :])
    h1_ref[...] = h1
    hn = _rms(h1, nffn_ref[...])
    hn_ref[...] = hn

    hn_hi = hn.astype(BF16)
    hn_lo = (hn - hn_hi.astype(F32)).astype(BF16)
    logits = (_dot(hn_hi, wr_hi_ref[...]) + _dot(hn_lo, wr_hi_ref[...]) + _dot(hn_hi, wr_lo_ref[...])
              + br_ref[...])
    lane = lax.broadcasted_iota(jnp.int32, logits.shape, 1).astype(F32)
    none = float(LANES)
    neg = -jnp.inf

    def lane_max(x):
        return jnp.max(x, axis=-1, keepdims=True)

    def lane_sum(x):
        return jnp.sum(x, axis=-1, keepdims=True)

    def first_lane(mask):
        return jnp.min(jnp.where(mask, lane, none), axis=-1, keepdims=True)

    is_group = lane < float(N_GROUPS)
    g_max = lane_max(jnp.where(is_group, logits, neg))
    g_exp = jnp.where(is_group, jnp.exp(logits - g_max), 0.0)
    g_prob = g_exp / lane_sum(g_exp)
    g_w = lane_max(g_prob)
    g_idx = first_lane(is_group & (g_prob == g_w))
    e_lo = float(ROUTER_EXPERT_LANE0) + float(EXPERTS_PER_GROUP) * g_idx
    sel = (lane >= e_lo) & (lane < e_lo + float(EXPERTS_PER_GROUP))
    e_max = lane_max(jnp.where(sel, logits, neg))
    e_exp = jnp.where(sel, jnp.exp(logits - e_max), 0.0)
    e_prob = e_exp / lane_sum(e_exp)
    p1 = lane_max(jnp.where(sel, e_prob, neg))
    i1 = first_lane(sel & (e_prob == p1))
    rest = sel & (lane != i1)
    p2 = lane_max(jnp.where(rest, e_prob, neg))
    i2 = first_lane(rest & (e_prob == p2))
    denom = p1 + p2
    chosen = jnp.where((lane == i1) | (lane == i2), 1.0, 0.0)
    rank = count_ref[...] + _dot(tri_ref[...], chosen.astype(BF16))
    count_ref[...] += jnp.sum(chosen, axis=0, keepdims=True)
    fields = (i1 - float(ROUTER_EXPERT_LANE0), i2 - float(ROUTER_EXPERT_LANE0),
              lane_sum(jnp.where(lane == i1, rank, 0.0)), lane_sum(jnp.where(lane == i2, rank, 0.0)),
              g_w * (p1 / denom), g_w * (p2 / denom))
    route = jnp.zeros_like(logits)
    for k, value in enumerate(fields):
        route = jnp.where(lane == float(k), value, route)
    route_ref[...] = route
    ids_ref[...] = jnp.transpose(route)[0:ROUTE_ID_ROWS, :].astype(jnp.int32)


ROUTE_E1, ROUTE_E2, ROUTE_RANK1, ROUTE_RANK2, ROUTE_W1, ROUTE_W2 = range(6)
ROUTE_ID_ROWS = 8


def _mix(x2d, a, o_f, o_b, og, w, tm):
    t = x2d.shape[0]
    hv = GLA_HEADS * GLA_DV
    row = lambda width: pl.BlockSpec((tm, width), lambda i: (i, 0))
    return pl.pallas_call(
        _mix_body,
        grid=(t // tm,),
        in_specs=[row(D_MODEL), row(MLA_HEADS * MLA_V), row(hv), row(hv), row(hv),
                  _full_spec((1, GLA_DV)), _full_spec(w['wout'].shape), _full_spec((1, D_MODEL)),
                  _full_spec(w['wr_hi'].shape), _full_spec(w['wr_lo'].shape), _full_spec(w['br'].shape)],
        out_specs=(row(D_MODEL), row(D_MODEL), row(LANES),
                   pl.BlockSpec((ROUTE_ID_ROWS, tm), lambda i: (0, i)), _full_spec((1, LANES))),
        out_shape=(jax.ShapeDtypeStruct((t, D_MODEL), F32), jax.ShapeDtypeStruct((t, D_MODEL), F32),
                   jax.ShapeDtypeStruct((t, LANES), F32), jax.ShapeDtypeStruct((ROUTE_ID_ROWS, t), jnp.int32),
                   jax.ShapeDtypeStruct((1, LANES), F32)),
        scratch_shapes=[pltpu.VMEM((tm, tm), BF16)],
        compiler_params=_cparams("arbitrary"),
        name="mix_router",
    )(x2d, a, o_f, o_b, og, w['gla_norm'], w['wout'], w['norm_ffn'], w['wr_hi'], w['wr_lo'], w['br'])


EXPERT_TILE = 256
ROW_TILE = 256


def _row_copy(src_ref, src_row, dst_ref, dst_row, sem):
    return pltpu.make_async_copy(src_ref.at[pl.ds(src_row, 1)], dst_ref.at[pl.ds(dst_row, 1)], sem)


def _gather_rows(src_ref, row_of, dst_ref, sem):
    def body(j, carry):
        for p in range(2):
            r = 2 * j + p
            _row_copy(src_ref, row_of(r), dst_ref, r, sem).start(priority=p)
        return carry
    lax.fori_loop(0, ROW_TILE // 2, body, 0, unroll=4)


def _wait_rows(src_ref, dst_ref, sem, n):
    for _ in range(n):
        _row_copy(src_ref, 0, dst_ref, 0, sem).wait()


def _queue_body(starts_ref, ids_ref, slots_ref, token_ref):
    i = pl.program_id(0)

    @pl.when(i == 0)
    def _():
        def clear(j, carry):
            token_ref[j] = 0
            return carry
        lax.fori_loop(0, token_ref.shape[0], clear, 0, unroll=8)

    def place(r, carry):
        for k in range(2):
            slot = starts_ref[ids_ref[ROUTE_E1 + k, r]] + ids_ref[ROUTE_RANK1 + k, r]
            slots_ref[k, r] = slot
            token_ref[slot] = i * ROW_TILE + r
        return carry

    lax.fori_loop(0, ROW_TILE, place, 0, unroll=8)


def _queue_tokens(starts, ids, n_rows):
    t = ids.shape[1]
    ids_block = pl.BlockSpec((ROUTE_ID_ROWS, ROW_TILE), lambda i, *_: (0, i), memory_space=pltpu.SMEM)
    return pl.pallas_call(
        _queue_body,
        grid_spec=pltpu.PrefetchScalarGridSpec(
            num_scalar_prefetch=1,
            grid=(t // ROW_TILE,),
            in_specs=[ids_block],
            out_specs=(pl.BlockSpec((2, ROW_TILE), lambda i, *_: (0, i), memory_space=pltpu.SMEM),
                       pl.BlockSpec(memory_space=pltpu.SMEM))),
        out_shape=(jax.ShapeDtypeStruct((2, t), jnp.int32), jax.ShapeDtypeStruct((n_rows,), jnp.int32)),
        compiler_params=_cparams("arbitrary"),
        name="moe_queue",
    )(starts, ids)


def _expert_body(tile_expert_ref, n_tiles_ref, tok_ref, next_tok_ref, hn_ref, wgu_ref, wd_ref, ys_ref, x_ref, sem):
    i = pl.program_id(0)
    n_tiles = n_tiles_ref[0]
    cur = i % 2

    @pl.when(i == 0)
    def _():
        _gather_rows(hn_ref, lambda r: tok_ref[0, r], x_ref.at[0], sem.at[0])

    @pl.when(i + 1 < n_tiles)
    def _():
        _gather_rows(hn_ref, lambda r: next_tok_ref[0, r], x_ref.at[1 - cur], sem.at[1 - cur])

    @pl.when(i < n_tiles)
    def _():
        _wait_rows(hn_ref, x_ref.at[cur], sem.at[cur], EXPERT_TILE)
        h = _dot(x_ref[cur].astype(BF16), wgu_ref[...])
        hg = h[:, :D_EXPERT]
        act = (hg / (1.0 + jnp.exp(-hg))) * h[:, D_EXPERT:]
        ys_ref[...] = _dot(act.astype(BF16), wd_ref[...])

    @pl.when(i >= n_tiles)
    def _():
        ys_ref[...] = jnp.zeros_like(ys_ref)


def _experts(tile_expert, n_tiles, tokens, hn, w):
    n_grid = tokens.shape[0]

    def tile(i, tile_expert, n_tiles):
        return jnp.minimum(i, n_tiles[0] - 1)

    tok_block = lambda index: pl.BlockSpec((None, 1, EXPERT_TILE), index, memory_space=pltpu.SMEM)
    return pl.pallas_call(
        _expert_body,
        grid_spec=pltpu.PrefetchScalarGridSpec(
            num_scalar_prefetch=2,
            grid=(n_grid,),
            in_specs=[tok_block(lambda i, te, nt: (i, 0, 0)),
                      tok_block(lambda i, te, nt: (jnp.minimum(i + 1, n_grid - 1), 0, 0)),
                      pl.BlockSpec(memory_space=pl.ANY),
                      pl.BlockSpec((None, D_MODEL, 2 * D_EXPERT), lambda i, te, nt: (te[tile(i, te, nt)], 0, 0)),
                      pl.BlockSpec((None, D_EXPERT, D_MODEL), lambda i, te, nt: (te[tile(i, te, nt)], 0, 0))],
            out_specs=pl.BlockSpec((EXPERT_TILE, D_MODEL), lambda i, te, nt: (i, 0)),
            scratch_shapes=[pltpu.VMEM((2, EXPERT_TILE, D_MODEL), F32), pltpu.SemaphoreType.DMA((2,))]),
        out_shape=jax.ShapeDtypeStruct((n_grid * EXPERT_TILE, D_MODEL), F32),
        compiler_params=_cparams("arbitrary"),
        name="moe_experts",
    )(tile_expert, n_tiles, tokens, tokens, hn, w['wgu'], w['wd'])


def _combine_body(slots_ref, next_slots_ref, h1_ref, route_ref, nfin_ref, ys_ref, out_ref, buf_ref, sem):
    i = pl.program_id(0)
    cur = i % 2

    def issue(slots, half):
        for k in range(2):
            _gather_rows(ys_ref, lambda r: slots[k, r], buf_ref.at[half, k], sem.at[half])

    @pl.when(i == 0)
    def _():
        issue(slots_ref, 0)

    @pl.when(i + 1 < pl.num_programs(0))
    def _():
        issue(next_slots_ref, 1 - cur)

    _wait_rows(ys_ref, buf_ref.at[cur, 0], sem.at[cur], 2 * ROW_TILE)
    route = route_ref[...]
    lane = lax.broadcasted_iota(jnp.int32, route.shape, 1)
    w1 = jnp.sum(jnp.where(lane == ROUTE_W1, route, 0.0), axis=-1, keepdims=True)
    w2 = jnp.sum(jnp.where(lane == ROUTE_W2, route, 0.0), axis=-1, keepdims=True)
    out_ref[...] = _rms(h1_ref[...] + w1 * buf_ref[cur, 0] + w2 * buf_ref[cur, 1], nfin_ref[...])


def _combine(slots, h1, route, ys, w):
    t = h1.shape[0]
    n_steps = t // ROW_TILE
    row = lambda width: pl.BlockSpec((ROW_TILE, width), lambda i: (i, 0))
    slots_block = lambda index: pl.BlockSpec((2, ROW_TILE), index, memory_space=pltpu.SMEM)
    return pl.pallas_call(
        _combine_body,
        grid=(n_steps,),
        in_specs=[slots_block(lambda i: (0, i)), slots_block(lambda i: (0, jnp.minimum(i + 1, n_steps - 1))),
                  row(D_MODEL), row(LANES), _full_spec((1, D_MODEL)), pl.BlockSpec(memory_space=pl.ANY)],
        out_specs=row(D_MODEL),
        out_shape=jax.ShapeDtypeStruct((t, D_MODEL), F32),
        scratch_shapes=[pltpu.VMEM((2, 2, ROW_TILE, D_MODEL), F32), pltpu.SemaphoreType.DMA((2,))],
        compiler_params=_cparams("arbitrary"),
        name="moe_combine",
    )(slots, slots, h1, route, w['norm_final'], ys)


def _moe(hn, route, ids, counts, h1, w):
    t = hn.shape[0]
    n_rows = 2 * t + N_EXPERTS * EXPERT_TILE
    n_grid_tiles = n_rows // EXPERT_TILE
    count = counts[0, ROUTER_EXPERT_LANE0:ROUTER_EXPERT_LANE0 + N_EXPERTS].astype(jnp.int32)
    padded = jnp.maximum((count + EXPERT_TILE - 1) // EXPERT_TILE, 1) * EXPERT_TILE
    ends = jnp.sum(jnp.where(jnp.arange(N_EXPERTS)[:, None] <= jnp.arange(N_EXPERTS)[None, :], padded[:, None], 0),
                   axis=0)
    starts = ends - padded
    n_tiles = (ends[-1:] // EXPERT_TILE)
    tile_rows = jnp.arange(n_grid_tiles, dtype=jnp.int32) * EXPERT_TILE
    tile_expert = jnp.minimum(jnp.sum((ends[None, :] <= tile_rows[:, None]).astype(jnp.int32), axis=1), N_EXPERTS - 1)
    slots, tokens = _queue_tokens(starts, ids, n_rows)
    ys = _experts(tile_expert, n_tiles, tokens.reshape(n_grid_tiles, 1, EXPERT_TILE), hn, w)
    return _combine(slots, h1, route, ys, w)


def _rope_tables(positions):
    inv_freq = 1.0 / (ROPE_THETA ** (jnp.arange(0, MLA_ROPE, 2, dtype=F32) / MLA_ROPE))
    ang = positions.astype(F32)[:, None] * inv_freq[None, :]
    cos, sin = jnp.cos(ang), jnp.sin(ang)
    reps = LANES // MLA_ROPE
    return jnp.tile(jnp.concatenate([cos, cos], axis=-1), (1, reps)), jnp.tile(jnp.concatenate([-sin, sin], axis=-1), (1, reps))


def _pack_weights(norm_mix, w_in, q_a_norm, w_uq, kv_a_norm, w_ukv, w_gate_fwd, b_gate_fwd, w_gate_bwd, b_gate_bwd,
                  gla_norm, w_out, norm_ffn, w_router_group, b_router_group, w_router_expert, b_router_expert,
                  w_expert_gate, w_expert_up, w_expert_down, norm_final):
    l = 0
    hk = GLA_HEADS * GLA_DK
    hv = GLA_HEADS * GLA_DV
    c_q, c_kv, k_pe, gq, gk, gv, lr_f, lr_b, og = jnp.split(
        w_in[l], np.cumsum([MLA_Q_RANK, MLA_KV_RANK, MLA_ROPE, hk, hk, hv, GLA_GATE_RANK, GLA_GATE_RANK])[:].tolist(),
        axis=-1)
    lr_pad = jnp.zeros((D_MODEL, LANES - 2 * GLA_GATE_RANK), F32)
    win = jnp.concatenate([c_q, c_kv, k_pe, k_pe, gq, gk, gv, og, lr_f, lr_b, lr_pad], axis=-1).astype(BF16)
    wuq = w_uq[l].reshape(MLA_Q_RANK, MLA_HEADS, MLA_NOPE + MLA_ROPE)
    wuq = jnp.concatenate([wuq[:, :, :MLA_NOPE].reshape(MLA_Q_RANK, -1), wuq[:, :, MLA_NOPE:].reshape(MLA_Q_RANK, -1)],
                          axis=-1).astype(BF16)
    wgate = jnp.zeros((LANES, 2 * hk), F32)
    wgate = wgate.at[0:GLA_GATE_RANK, 0:hk].set(w_gate_fwd[l])
    wgate = wgate.at[GLA_GATE_RANK:2 * GLA_GATE_RANK, hk:].set(w_gate_bwd[l])
    wr = jnp.zeros((D_MODEL, LANES), F32)
    wr = wr.at[:, ROUTER_GROUP_LANE0:ROUTER_GROUP_LANE0 + N_GROUPS].set(w_router_group[l])
    wr = wr.at[:, ROUTER_EXPERT_LANE0:ROUTER_EXPERT_LANE0 + N_EXPERTS].set(w_router_expert[l])
    wr_hi = wr.astype(BF16)
    br = jnp.zeros((1, LANES), F32)
    br = br.at[0, ROUTER_GROUP_LANE0:ROUTER_GROUP_LANE0 + N_GROUPS].set(b_router_group[l])
    br = br.at[0, ROUTER_EXPERT_LANE0:ROUTER_EXPERT_LANE0 + N_EXPERTS].set(b_router_expert[l])
    wgu = jnp.concatenate([w_expert_gate[l], w_expert_up[l]], axis=-1).reshape(N_EXPERTS, D_MODEL, 2 * D_EXPERT)
    return {
        'norm_mix': norm_mix[l][None], 'win': win, 'q_a_norm': q_a_norm[l][None], 'wuq': wuq,
        'kv_a_norm': kv_a_norm[l][None], 'wukv': w_ukv[l].astype(BF16),
        'wgate': wgate.astype(BF16), 'bgate': jnp.concatenate([b_gate_fwd[l], b_gate_bwd[l]])[None],
        'gla_norm': gla_norm[l][None], 'wout': w_out[l].astype(BF16), 'norm_ffn': norm_ffn[l][None],
        'wr_hi': wr_hi, 'wr_lo': (wr - wr_hi.astype(F32)).astype(BF16), 'br': br,
        'wgu': wgu.astype(BF16), 'wd': w_expert_down[l].reshape(N_EXPERTS, D_EXPERT, D_MODEL).astype(BF16),
        'norm_final': norm_final[None],
    }


def _meta_streams(meta_tokens, w):
    cos, sin = _rope_tables(jnp.arange(N_META))
    _, k, v, _, gk, gv, gf, _, _ = _inproj(meta_tokens, cos, sin, w, N_META)
    pad_keys = ((0, 0), (0, LANES - N_META), (0, 0))
    front = ((GLA_CHUNK - N_META, 0), (0, 0))
    return (jnp.pad(k, pad_keys), jnp.pad(v, pad_keys), jnp.pad(gk, front), jnp.pad(gv, front), jnp.pad(gf, front))


def _token_mixers(x, meta, w, tm, tq, tk, tb):
    bsz, seq, _ = x.shape
    km, vm, mk, mv, mg = meta
    x2d = x.reshape(bsz * seq, D_MODEL)
    cos, sin = _rope_tables(N_META + jnp.arange(seq))
    q, k, v, gq, gk, gv, gf, gb, og = _inproj(x2d, cos, sin, w, tm)
    a = _attention(q, k, v, km, vm, bsz, seq, tq, tk)
    o_f, o_b = _gla(gq, gk, gv, gf, gb, mk, mv, mg, bsz, seq, tb)
    return _mix(x2d, a, o_f, o_b, og, w, tm)


def kernel(x_prompt, x_sample, meta_tokens, norm_mix, w_in, q_a_norm, w_uq, kv_a_norm, w_ukv, w_gate_fwd, b_gate_fwd, w_gate_bwd, b_gate_bwd, gla_norm, w_out, norm_ffn, w_router_group, b_router_group, w_router_expert, b_router_expert, w_expert_gate, w_expert_up, w_expert_down, norm_final):
    w = _pack_weights(norm_mix, w_in, q_a_norm, w_uq, kv_a_norm, w_ukv, w_gate_fwd, b_gate_fwd, w_gate_bwd,
                      b_gate_bwd, gla_norm, w_out, norm_ffn, w_router_group, b_router_group, w_router_expert,
                      b_router_expert, w_expert_gate, w_expert_up, w_expert_down, norm_final)
    meta = _meta_streams(meta_tokens, w)
    outs = []
    for x in (x_prompt, x_sample):
        h1, hn, route, ids, counts = _token_mixers(x, meta, w, tm=512, tq=512, tk=512, tb=512)
        outs.append(_moe(hn, route, ids, counts, h1, w).reshape(x.shape))
    return tuple(outs)
```

```python
import functools

import numpy as np
import jax
import jax.numpy as jnp
from jax import lax
from jax.experimental import pallas as pl
from jax.experimental.pallas import tpu as pltpu
from jax.experimental.pallas import tpu_sc as plsc

F32 = jnp.float32
BF16 = jnp.bfloat16

D_MODEL = 1024
N_META = 16
MLA_HEADS = 4
MLA_Q_RANK = 384
MLA_KV_RANK = 256
MLA_NOPE = 128
MLA_ROPE = 64
MLA_V = 128
ROPE_THETA = 10000.0
GLA_HEADS = 4
GLA_DK = 64
GLA_DV = 128
GLA_GATE_RANK = 16
GLA_TAU = 16.0
GLA_CHUNK = 64
N_GROUPS = 4
EXPERTS_PER_GROUP = 8
N_EXPERTS = N_GROUPS * EXPERTS_PER_GROUP
D_EXPERT = 256
EPS = 1e-6

LANES = 128
V7X_VMEM_BYTES = 64 * 1024 * 1024
VMEM_LIMIT = V7X_VMEM_BYTES * 7 // 8

ATTN_SCALE = (MLA_NOPE + MLA_ROPE) ** -0.5 * float(np.log2(np.e))
QK_WIDTH = 2 * LANES
V_WIDTH = 2 * LANES
ATTN_GROUP = 4

C_CQ = 0
C_CKV = C_CQ + MLA_Q_RANK
C_KPE = C_CKV + MLA_KV_RANK
C_GQ = C_KPE + LANES
C_GK = C_GQ + GLA_HEADS * GLA_DK
C_GV = C_GK + GLA_HEADS * GLA_DK
C_OG = C_GV + GLA_HEADS * GLA_DV
C_LR = C_OG + GLA_HEADS * GLA_DV
D_IN_PACKED = C_LR + LANES

ROUTER_GROUP_LANE0 = 0
ROUTER_EXPERT_LANE0 = N_GROUPS


def _cparams(*semantics):
    return pltpu.CompilerParams(dimension_semantics=semantics, vmem_limit_bytes=VMEM_LIMIT)


def _rms(x, g):
    return x * lax.rsqrt(jnp.mean(x * x, axis=-1, keepdims=True) + EPS) * g


def _dot(a, b):
    return jnp.dot(a, b, preferred_element_type=F32)


def _dot_nt(a, b):
    return lax.dot_general(a, b, (((1,), (1,)), ((), ())), preferred_element_type=F32)


def _dot_tn(a, b):
    return lax.dot_general(a, b, (((0,), (0,)), ((), ())), preferred_element_type=F32)


def _full_spec(shape):
    return pl.BlockSpec(shape, lambda *_: (0,) * len(shape))


SUBLANES = 8
ROW_CHUNKS = D_MODEL // LANES
assert ROW_CHUNKS == SUBLANES


def _rows_to_tiles(x):
    chunks = jnp.stack([x[:, s * LANES:(s + 1) * LANES] for s in range(ROW_CHUNKS)], axis=0)
    return pltpu.einshape("smd->msd", chunks)


def _tiles_to_rows(x):
    chunks = pltpu.einshape("msd->smd", x)
    return jnp.concatenate([chunks[s] for s in range(ROW_CHUNKS)], axis=-1)


def _rope_pairs(x, cos, sin_signed, first_half):
    swapped = jnp.where(first_half, pltpu.roll(x, LANES - MLA_ROPE // 2, 1), pltpu.roll(x, MLA_ROPE // 2, 1))
    return x * cos + swapped * sin_signed


def _inproj_body(x_ref, cos_ref, sin_ref, nmix_ref, win_ref, qan_ref, wuq_ref, kvan_ref, wukv_ref,
                 wgate_ref, bgate_ref,
                 q_ref, k_ref, v_ref, gq_ref, gk_ref, gv_ref, gf_ref, gb_ref, og_ref):
    hn = _rms(x_ref[...], nmix_ref[...]).astype(BF16)

    def proj(lo, hi):
        return _dot(hn, win_ref[:, lo:hi])

    cos = cos_ref[...]
    sin = sin_ref[...]
    lane = lax.broadcasted_iota(jnp.int32, cos.shape, 1)
    first_half = (lane & (MLA_ROPE - 1)) < MLA_ROPE // 2
    low_lanes = lane < MLA_ROPE

    cq = _rms(proj(C_CQ, C_CKV), qan_ref[...]).astype(BF16)
    qn = _dot(cq, wuq_ref[:, 0:MLA_HEADS * MLA_NOPE]) * ATTN_SCALE
    qr = _dot(cq, wuq_ref[:, MLA_HEADS * MLA_NOPE:])
    for j in range(MLA_HEADS // 2):
        rj = (_rope_pairs(qr[:, j * LANES:(j + 1) * LANES], cos, sin, first_half) * ATTN_SCALE).astype(BF16)
        for h in (2 * j, 2 * j + 1):
            q_ref[h, :, 0:LANES] = qn[:, h * LANES:(h + 1) * LANES].astype(BF16)
            q_ref[h, :, LANES:QK_WIDTH] = rj

    ckv = _rms(proj(C_CKV, C_KPE), kvan_ref[...]).astype(BF16)
    kv = _dot(ckv, wukv_ref[...])
    kr = _rope_pairs(proj(C_KPE, C_GQ), cos, sin, first_half)
    kr_even = jnp.where(low_lanes, kr, 0.0).astype(BF16)
    kr_odd = jnp.where(low_lanes, 0.0, kr).astype(BF16)
    for h in range(MLA_HEADS):
        base = h * (MLA_NOPE + MLA_V)
        k_ref[h, :, 0:LANES] = kv[:, base:base + MLA_NOPE].astype(BF16)
        k_ref[h, :, LANES:QK_WIDTH] = kr_even if h % 2 == 0 else kr_odd
        v_ref[h, :, 0:MLA_V] = kv[:, base + MLA_NOPE:base + MLA_NOPE + MLA_V].astype(BF16)
        v_ref[h, :, MLA_V:V_WIDTH] = jnp.ones((kv.shape[0], V_WIDTH - MLA_V), BF16)

    gq_ref[...] = proj(C_GQ, C_GK) * (GLA_DK ** -0.5)
    gk_ref[...] = proj(C_GK, C_GV)
    gv_ref[...] = proj(C_GV, C_OG)
    og_ref[...] = proj(C_OG, C_LR)
    pre = _dot(proj(C_LR, D_IN_PACKED).astype(BF16), wgate_ref[...]) + bgate_ref[...]
    logsig = jnp.minimum(pre, 0.0) - jnp.log1p(jnp.exp(-jnp.abs(pre)))
    gates = logsig * (1.0 / GLA_TAU)
    gf_ref[...] = gates[:, 0:GLA_HEADS * GLA_DK]
    gb_ref[...] = gates[:, GLA_HEADS * GLA_DK:]


def _inproj(x2d, cos, sin, w, tm):
    t = x2d.shape[0]
    blocks_per_seq = cos.shape[0] // tm
    hk = GLA_HEADS * GLA_DK
    hv = GLA_HEADS * GLA_DV
    row = lambda width: pl.BlockSpec((tm, width), lambda i: (i, 0))
    head_rows = lambda width: pl.BlockSpec((MLA_HEADS, tm, width), lambda i: (0, i, 0))
    tab = pl.BlockSpec((tm, LANES), lambda i: (i % blocks_per_seq, 0))
    out_shape = (
        jax.ShapeDtypeStruct((MLA_HEADS, t, QK_WIDTH), BF16),
        jax.ShapeDtypeStruct((MLA_HEADS, t, QK_WIDTH), BF16),
        jax.ShapeDtypeStruct((MLA_HEADS, t, V_WIDTH), BF16),
        jax.ShapeDtypeStruct((t, hk), F32),
        jax.ShapeDtypeStruct((t, hk), F32),
        jax.ShapeDtypeStruct((t, hv), F32),
        jax.ShapeDtypeStruct((t, hk), F32),
        jax.ShapeDtypeStruct((t, hk), F32),
        jax.ShapeDtypeStruct((t, hv), F32),
    )
    return pl.pallas_call(
        _inproj_body,
        grid=(t // tm,),
        in_specs=[row(D_MODEL), tab, tab,
                  _full_spec((1, D_MODEL)), _full_spec((D_MODEL, D_IN_PACKED)),
                  _full_spec((1, MLA_Q_RANK)), _full_spec(w['wuq'].shape),
                  _full_spec((1, MLA_KV_RANK)), _full_spec(w['wukv'].shape),
                  _full_spec(w['wgate'].shape), _full_spec(w['bgate'].shape)],
        out_specs=(head_rows(QK_WIDTH), head_rows(QK_WIDTH), head_rows(V_WIDTH),
                   row(hk), row(hk), row(hv), row(hk), row(hk), row(hv)),
        out_shape=out_shape,
        compiler_params=_cparams("parallel"),
        name="inproj",
    )(x2d, cos, sin, w['norm_mix'], w['win'], w['q_a_norm'], w['wuq'], w['kv_a_norm'], w['wukv'],
      w['wgate'], w['bgate'])


def _attn_body(q_ref, k_ref, v_ref, km_ref, vm_ref, o_ref, s_ref, acc_ref, *, tk):
    q = q_ref[...]
    n_groups = k_ref.shape[0] // (ATTN_GROUP * tk)

    def scores(j):
        return _dot_nt(q, k_ref[pl.ds(pl.multiple_of(j * tk, tk), tk), :])

    def values(j):
        return v_ref[pl.ds(pl.multiple_of(j * tk, tk), tk), :]

    def absorb(m, s, v):
        m_new = jnp.maximum(m, jnp.max(s, axis=-1, keepdims=True))
        p = jnp.exp2(s - m_new)
        acc_ref[...] = jnp.exp2(m - m_new) * acc_ref[...] + _dot(p.astype(BF16), v)
        return m_new

    s_ref[0] = scores(0)
    sm = _dot_nt(q, km_ref[...])
    sm = jnp.where(lax.broadcasted_iota(jnp.int32, sm.shape, 1) < N_META, sm, -jnp.inf)
    m = jnp.max(sm, axis=-1, keepdims=True)
    acc_ref[...] = _dot(jnp.exp2(sm - m).astype(BF16), vm_ref[...])

    def group(g, m, last):
        for i in range(ATTN_GROUP):
            j = ATTN_GROUP * g + i
            s = s_ref[i % 2]
            if not (last and i == ATTN_GROUP - 1):
                s_ref[(i + 1) % 2] = scores(j + 1)
            m = absorb(m, s, values(j))
        return m

    m = lax.fori_loop(0, n_groups - 1, lambda g, m: group(g, m, False), m)
    group(n_groups - 1, m, True)
    acc = acc_ref[...]
    o_ref[...] = (acc[:, :MLA_V] / acc[:, MLA_V:]).astype(o_ref.dtype)


def _attention(q, k, v, km, vm, bsz, seq, tq, tk):
    nq = seq // tq
    return pl.pallas_call(
        functools.partial(_attn_body, tk=tk),
        grid=(bsz, MLA_HEADS, nq),
        in_specs=[pl.BlockSpec((None, tq, QK_WIDTH), lambda b, h, i: (h, b * nq + i, 0)),
                  pl.BlockSpec((None, seq, QK_WIDTH), lambda b, h, i: (h, b, 0)),
                  pl.BlockSpec((None, seq, V_WIDTH), lambda b, h, i: (h, b, 0)),
                  pl.BlockSpec((None, LANES, QK_WIDTH), lambda b, h, i: (h, 0, 0)),
                  pl.BlockSpec((None, LANES, V_WIDTH), lambda b, h, i: (h, 0, 0))],
        out_specs=pl.BlockSpec((tq, MLA_V), lambda b, h, i: (b * nq + i, h)),
        out_shape=jax.ShapeDtypeStruct((bsz * seq, MLA_HEADS * MLA_V), BF16),
        scratch_shapes=[pltpu.VMEM((2, tq, tk), F32), pltpu.VMEM((tq, V_WIDTH), F32)],
        compiler_params=_cparams("parallel", "parallel", "arbitrary"),
        name="mla_attention",
    )(q, k, v, km, vm)


def _split3(x):
    hi = x.astype(BF16)
    r1 = x - hi.astype(F32)
    mid = r1.astype(BF16)
    lo = (r1 - mid.astype(F32)).astype(BF16)
    return hi, mid, lo


def _gla_chunk(q, k, g, v_even, v_odd, state, tri, keep, mid, last):
    lane = lax.broadcasted_iota(jnp.int32, (GLA_CHUNK, LANES), 1)
    even = lane < GLA_DK
    g_hi, g_mid, g_lo = _split3(g)
    b = _dot(tri, g_hi) + _dot(tri, g_mid) + _dot(tri, g_lo)
    b_last = b[last:last + 1, :]
    ke = (k * jnp.exp(b_last - b)).astype(BF16)
    upd = jnp.where(lax.broadcasted_iota(jnp.int32, (GLA_DV, LANES), 1) < GLA_DK,
                    _dot_tn(v_even, ke), _dot_tn(v_odd, ke))
    new_state = state * jnp.exp(b_last) + upd
    if q is None:
        return new_state, None, None
    b_mid = b[mid:mid + 1, :]
    qs = q * jnp.exp(b - b_mid)
    ks = (k * jnp.exp(b_mid - b)).astype(BF16)
    qs2 = jnp.concatenate([jnp.where(even, qs, 0.0), jnp.where(even, 0.0, qs)], axis=0).astype(BF16)
    scores = jnp.where(keep, _dot_nt(qs2, ks), 0.0).astype(BF16)
    qe = q * jnp.exp(b)
    qe2 = jnp.concatenate([jnp.where(even, qe, 0.0), jnp.where(even, 0.0, qe)], axis=0).astype(BF16)
    inter = _dot_nt(qe2, state.astype(BF16))
    o_even = _dot(scores[:GLA_CHUNK], v_even) + inter[:GLA_CHUNK]
    o_odd = _dot(scores[GLA_CHUNK:], v_odd) + inter[GLA_CHUNK:]
    return new_state, o_even, o_odd


def _gla_body(qf_ref, kf_ref, vf_ref, gf_ref, qb_ref, kb_ref, vb_ref, gb_ref, mk_ref, mv_ref, mg_ref,
              of_ref, ob_ref, state_ref):
    n_chunks = qf_ref.shape[0] // GLA_CHUNK
    n_pairs = GLA_HEADS // 2
    r = lax.broadcasted_iota(jnp.int32, (GLA_CHUNK, GLA_CHUNK), 0)
    c = lax.broadcasted_iota(jnp.int32, (GLA_CHUNK, GLA_CHUNK), 1)
    tri_f = jnp.where(c <= r, 1.0, 0.0).astype(BF16)
    tri_b = jnp.where(c >= r, 1.0, 0.0).astype(BF16)
    r2 = lax.broadcasted_iota(jnp.int32, (2 * GLA_CHUNK, GLA_CHUNK), 0) & (GLA_CHUNK - 1)
    c2 = lax.broadcasted_iota(jnp.int32, (2 * GLA_CHUNK, GLA_CHUNK), 1)
    keep_f = c2 <= r2
    keep_b = c2 >= r2
    mid_f, last_f = GLA_CHUNK // 2 - 1, GLA_CHUNK - 1
    mid_b, last_b = GLA_CHUNK // 2, 0

    def pair_cols(p):
        return slice(p * LANES, (p + 1) * LANES)

    def head_cols(h):
        return slice(h * GLA_DV, (h + 1) * GLA_DV)

    @pl.when(pl.program_id(1) == 0)
    def _():
        for p in range(n_pairs):
            zero = jnp.zeros((GLA_DV, LANES), F32)
            st, _, _ = _gla_chunk(None, mk_ref[:, pair_cols(p)], mg_ref[:, pair_cols(p)],
                                  mv_ref[:, head_cols(2 * p)].astype(BF16),
                                  mv_ref[:, head_cols(2 * p + 1)].astype(BF16),
                                  zero, tri_f, keep_f, mid_f, last_f)
            state_ref[p] = st
            state_ref[n_pairs + p] = zero

    def step(i, _):
        rows_f = pl.ds(pl.multiple_of(i * GLA_CHUNK, GLA_CHUNK), GLA_CHUNK)
        rows_b = pl.ds(pl.multiple_of((n_chunks - 1 - i) * GLA_CHUNK, GLA_CHUNK), GLA_CHUNK)
        for p in range(n_pairs):
            for (rows, q_ref, k_ref, v_ref, g_ref, o_ref, slot, tri, keep, mid, last) in (
                    (rows_f, qf_ref, kf_ref, vf_ref, gf_ref, of_ref, p, tri_f, keep_f, mid_f, last_f),
                    (rows_b, qb_ref, kb_ref, vb_ref, gb_ref, ob_ref, n_pairs + p, tri_b, keep_b, mid_b, last_b)):
                st, o_even, o_odd = _gla_chunk(
                    q_ref[rows, pair_cols(p)], k_ref[rows, pair_cols(p)], g_ref[rows, pair_cols(p)],
                    v_ref[rows, head_cols(2 * p)].astype(BF16), v_ref[rows, head_cols(2 * p + 1)].astype(BF16),
                    state_ref[slot], tri, keep, mid, last)
                state_ref[slot] = st
                o_ref[rows, head_cols(2 * p)] = o_even
                o_ref[rows, head_cols(2 * p + 1)] = o_odd
        return 0

    lax.fori_loop(0, n_chunks, step, 0)


def _gla(gq, gk, gv, gf, gb, mk, mv, mg, bsz, seq, tb):
    nb = seq // tb
    hk = GLA_HEADS * GLA_DK
    hv = GLA_HEADS * GLA_DV
    fwd = lambda width: pl.BlockSpec((tb, width), lambda b, j: (b * nb + j, 0))
    bwd = lambda width: pl.BlockSpec((tb, width), lambda b, j: (b * nb + nb - 1 - j, 0))
    t = bsz * seq
    return pl.pallas_call(
        _gla_body,
        grid=(bsz, nb),
        in_specs=[fwd(hk), fwd(hk), fwd(hv), fwd(hk), bwd(hk), bwd(hk), bwd(hv), bwd(hk),
                  _full_spec(mk.shape), _full_spec(mv.shape), _full_spec(mg.shape)],
        out_specs=(fwd(hv), bwd(hv)),
        out_shape=(jax.ShapeDtypeStruct((t, hv), F32), jax.ShapeDtypeStruct((t, hv), F32)),
        scratch_shapes=[pltpu.VMEM((2 * (GLA_HEADS // 2), GLA_DV, LANES), F32)],
        compiler_params=_cparams("parallel", "arbitrary"),
        name="gla_scan",
    )(gq, gk, gv, gf, gq, gk, gv, gb, mk, mv, mg)


def _mix_body(x_ref, a_ref, of_ref, ob_ref, og_ref, gnorm_ref, wout_ref, nffn_ref, wr_hi_ref, wr_lo_ref, br_ref,
              h1_ref, hn_ref, route_ref, ids_ref, count_ref, tri_ref):
    tm = x_ref.shape[0]

    @pl.when(pl.program_id(0) == 0)
    def _():
        r = lax.broadcasted_iota(jnp.int32, (tm, tm), 0)
        c = lax.broadcasted_iota(jnp.int32, (tm, tm), 1)
        tri_ref[...] = jnp.where(c < r, 1.0, 0.0).astype(BF16)
        count_ref[...] = jnp.zeros_like(count_ref)

    a_width = MLA_HEADS * MLA_V
    h1 = x_ref[...] + _dot(a_ref[...], wout_ref[0:a_width, :])
    for h in range(GLA_HEADS):
        cols = slice(h * GLA_DV, (h + 1) * GLA_DV)
        o = of_ref[:, cols] + ob_ref[:, cols]
        og = og_ref[:, cols]
        silu = og / (1.0 + jnp.exp(-og))
        gh = (_rms(o, gnorm_ref[...]) * silu).astype(BF16)
        h1 = h1 + _dot(gh, wout_ref[a_width + h * GLA_DV:a_width + (h + 1) * GLA_DV, :])
    h1_ref[...] = h1
    hn = _rms(h1, nffn_ref[...])
    hn_ref[...] = _rows_to_tiles(hn)

    hn_hi = hn.astype(BF16)
    hn_lo = (hn - hn_hi.astype(F32)).astype(BF16)
    logits = (_dot(hn_hi, wr_hi_ref[...]) + _dot(hn_lo, wr_hi_ref[...]) + _dot(hn_hi, wr_lo_ref[...])
              + br_ref[...])
    lane = lax.broadcasted_iota(jnp.int32, logits.shape, 1).astype(F32)
    none = float(LANES)
    neg = -jnp.inf

    def lane_max(x):
        return jnp.max(x, axis=-1, keepdims=True)

    def lane_sum(x):
        return jnp.sum(x, axis=-1, keepdims=True)

    def first_lane(mask):
        return jnp.min(jnp.where(mask, lane, none), axis=-1, keepdims=True)

    is_group = lane < float(N_GROUPS)
    g_max = lane_max(jnp.where(is_group, logits, neg))
    g_exp = jnp.where(is_group, jnp.exp(logits - g_max), 0.0)
    g_prob = g_exp / lane_sum(g_exp)
    g_w = lane_max(g_prob)
    g_idx = first_lane(is_group & (g_prob == g_w))
    e_lo = float(ROUTER_EXPERT_LANE0) + float(EXPERTS_PER_GROUP) * g_idx
    sel = (lane >= e_lo) & (lane < e_lo + float(EXPERTS_PER_GROUP))
    e_max = lane_max(jnp.where(sel, logits, neg))
    e_exp = jnp.where(sel, jnp.exp(logits - e_max), 0.0)
    e_prob = e_exp / lane_sum(e_exp)
    p1 = lane_max(jnp.where(sel, e_prob, neg))
    i1 = first_lane(sel & (e_prob == p1))
    rest = sel & (lane != i1)
    p2 = lane_max(jnp.where(rest, e_prob, neg))
    i2 = first_lane(rest & (e_prob == p2))
    denom = p1 + p2
    chosen = jnp.where((lane == i1) | (lane == i2), 1.0, 0.0)
    rank = count_ref[...] + _dot(tri_ref[...], chosen.astype(BF16))
    count_ref[...] += jnp.sum(chosen, axis=0, keepdims=True)
    fields = (i1 - float(ROUTER_EXPERT_LANE0), i2 - float(ROUTER_EXPERT_LANE0),
              lane_sum(jnp.where(lane == i1, rank, 0.0)), lane_sum(jnp.where(lane == i2, rank, 0.0)),
              g_w * (p1 / denom), g_w * (p2 / denom))
    route = jnp.zeros_like(logits)
    for k, value in enumerate(fields):
        route = jnp.where(lane == float(k), value, route)
    route_ref[...] = route
    ids_ref[...] = jnp.transpose(route)[0:ROUTE_ID_ROWS, :].astype(jnp.int32)


ROUTE_E1, ROUTE_E2, ROUTE_RANK1, ROUTE_RANK2, ROUTE_W1, ROUTE_W2 = range(6)
ROUTE_ID_ROWS = 8


def _mix(x2d, a, o_f, o_b, og, w, tm):
    t = x2d.shape[0]
    hv = GLA_HEADS * GLA_DV
    row = lambda width: pl.BlockSpec((tm, width), lambda i: (i, 0))
    return pl.pallas_call(
        _mix_body,
        grid=(t // tm,),
        in_specs=[row(D_MODEL), row(MLA_HEADS * MLA_V), row(hv), row(hv), row(hv),
                  _full_spec((1, GLA_DV)), _full_spec(w['wout'].shape), _full_spec((1, D_MODEL)),
                  _full_spec(w['wr_hi'].shape), _full_spec(w['wr_lo'].shape), _full_spec(w['br'].shape)],
        out_specs=(row(D_MODEL), pl.BlockSpec((tm, ROW_CHUNKS, LANES), lambda i: (i, 0, 0)), row(LANES),
                   pl.BlockSpec((ROUTE_ID_ROWS, tm), lambda i: (0, i)), _full_spec((1, LANES))),
        out_shape=(jax.ShapeDtypeStruct((t, D_MODEL), F32), jax.ShapeDtypeStruct((t, ROW_CHUNKS, LANES), F32),
                   jax.ShapeDtypeStruct((t, LANES), F32), jax.ShapeDtypeStruct((ROUTE_ID_ROWS, t), jnp.int32),
                   jax.ShapeDtypeStruct((1, LANES), F32)),
        scratch_shapes=[pltpu.VMEM((tm, tm), BF16)],
        compiler_params=_cparams("arbitrary"),
        name="mix_router",
    )(x2d, a, o_f, o_b, og, w['gla_norm'], w['wout'], w['norm_ffn'], w['wr_hi'], w['wr_lo'], w['br'])


EXPERT_TILE = 256
ROW_TILE = 256


SC_CORES = 2
SC_SUBCORES = 16
SC_GATHER_ROWS = 32


def _sc_gather(table, idx):
    n = idx.shape[0]
    workers = SC_CORES * SC_SUBCORES
    per_worker = n // workers
    assert n % (workers * SC_GATHER_ROWS) == 0
    mesh = plsc.VectorSubcoreMesh(core_axis_name="c", subcore_axis_name="s")

    @functools.partial(
        pl.kernel, mesh=mesh,
        out_type=jax.ShapeDtypeStruct((n,) + table.shape[1:], table.dtype),
        scratch_types=[pltpu.VMEM((SC_GATHER_ROWS,), jnp.int32),
                       pltpu.VMEM((SC_GATHER_ROWS,) + table.shape[1:], table.dtype),
                       pltpu.SemaphoreType.DMA])
    def gather(table_ref, idx_ref, out_ref, idx_buf, rows_buf, sem):
        base = (lax.axis_index("s") * SC_CORES + lax.axis_index("c")) * per_worker

        @pl.loop(0, per_worker // SC_GATHER_ROWS)
        def _(j):
            rows = pl.ds(base + j * SC_GATHER_ROWS, SC_GATHER_ROWS)
            pltpu.sync_copy(idx_ref.at[rows], idx_buf)
            pltpu.async_copy(table_ref.at[idx_buf], rows_buf, sem).wait()
            pltpu.sync_copy(rows_buf, out_ref.at[rows])

    return gather(table, idx)


def _queue_body(starts_ref, ids_ref, slots_ref, token_ref):
    i = pl.program_id(0)

    @pl.when(i == 0)
    def _():
        def clear(j, carry):
            token_ref[j] = 0
            return carry
        lax.fori_loop(0, token_ref.shape[0], clear, 0, unroll=8)

    def place(r, carry):
        for k in range(2):
            slot = starts_ref[ids_ref[ROUTE_E1 + k, r]] + ids_ref[ROUTE_RANK1 + k, r]
            slots_ref[k, r] = slot
            token_ref[slot] = i * ROW_TILE + r
        return carry

    lax.fori_loop(0, ROW_TILE, place, 0, unroll=8)


def _queue_tokens(starts, ids, n_rows):
    t = ids.shape[1]
    ids_block = pl.BlockSpec((ROUTE_ID_ROWS, ROW_TILE), lambda i, *_: (0, i), memory_space=pltpu.SMEM)
    return pl.pallas_call(
        _queue_body,
        grid_spec=pltpu.PrefetchScalarGridSpec(
            num_scalar_prefetch=1,
            grid=(t // ROW_TILE,),
            in_specs=[ids_block],
            out_specs=(pl.BlockSpec((2, ROW_TILE), lambda i, *_: (0, i), memory_space=pltpu.SMEM),
                       pl.BlockSpec(memory_space=pltpu.SMEM))),
        out_shape=(jax.ShapeDtypeStruct((2, t), jnp.int32), jax.ShapeDtypeStruct((n_rows,), jnp.int32)),
        compiler_params=_cparams("arbitrary"),
        name="moe_queue",
    )(starts, ids)


def _expert_body(tile_expert_ref, n_tiles_ref, xs_ref, wgu_ref, wd_ref, ys_ref):
    used = pl.program_id(0) < n_tiles_ref[0]

    @pl.when(used)
    def _():
        h = _dot(_tiles_to_rows(xs_ref[...]).astype(BF16), wgu_ref[...])
        hg = h[:, :D_EXPERT]
        act = (hg / (1.0 + jnp.exp(-hg))) * h[:, D_EXPERT:]
        ys_ref[...] = _rows_to_tiles(_dot(act.astype(BF16), wd_ref[...]))

    @pl.when(jnp.logical_not(used))
    def _():
        ys_ref[...] = jnp.zeros_like(ys_ref)


def _experts(tile_expert, n_tiles, xs, w):
    n_grid = xs.shape[0] // EXPERT_TILE

    def tile(i, tile_expert, n_tiles):
        return jnp.minimum(i, n_tiles[0] - 1)

    tiles = lambda index: pl.BlockSpec((EXPERT_TILE, ROW_CHUNKS, LANES), index)
    return pl.pallas_call(
        _expert_body,
        grid_spec=pltpu.PrefetchScalarGridSpec(
            num_scalar_prefetch=2,
            grid=(n_grid,),
            in_specs=[tiles(lambda i, te, nt: (tile(i, te, nt), 0, 0)),
                      pl.BlockSpec((None, D_MODEL, 2 * D_EXPERT), lambda i, te, nt: (te[tile(i, te, nt)], 0, 0)),
                      pl.BlockSpec((None, D_EXPERT, D_MODEL), lambda i, te, nt: (te[tile(i, te, nt)], 0, 0))],
            out_specs=tiles(lambda i, te, nt: (i, 0, 0))),
        out_shape=jax.ShapeDtypeStruct(xs.shape, F32),
        compiler_params=_cparams("arbitrary"),
        name="moe_experts",
    )(tile_expert, n_tiles, xs, w['wgu'], w['wd'])


def _combine_body(h1_ref, route_ref, nfin_ref, y1_ref, y2_ref, out_ref):
    route = route_ref[...]
    lane = lax.broadcasted_iota(jnp.int32, route.shape, 1)
    w1 = jnp.sum(jnp.where(lane == ROUTE_W1, route, 0.0), axis=-1, keepdims=True)
    w2 = jnp.sum(jnp.where(lane == ROUTE_W2, route, 0.0), axis=-1, keepdims=True)
    y = w1 * _tiles_to_rows(y1_ref[...]) + w2 * _tiles_to_rows(y2_ref[...])
    out_ref[...] = _rms(h1_ref[...] + y, nfin_ref[...])


def _combine(h1, route, y12, w):
    t = h1.shape[0]
    row = lambda width: pl.BlockSpec((ROW_TILE, width), lambda i: (i, 0))
    tiles = lambda k: pl.BlockSpec((None, ROW_TILE, ROW_CHUNKS, LANES), lambda i: (k, i, 0, 0))
    return pl.pallas_call(
        _combine_body,
        grid=(t // ROW_TILE,),
        in_specs=[row(D_MODEL), row(LANES), _full_spec((1, D_MODEL)), tiles(0), tiles(1)],
        out_specs=row(D_MODEL),
        out_shape=jax.ShapeDtypeStruct((t, D_MODEL), F32),
        compiler_params=_cparams("parallel"),
        name="moe_combine",
    )(h1, route, w['norm_final'], y12, y12)


def _moe(hn, route, ids, counts, h1, w):
    t = hn.shape[0]
    n_rows = 2 * t + N_EXPERTS * EXPERT_TILE
    n_grid_tiles = n_rows // EXPERT_TILE
    count = counts[0, ROUTER_EXPERT_LANE0:ROUTER_EXPERT_LANE0 + N_EXPERTS].astype(jnp.int32)
    padded = jnp.maximum((count + EXPERT_TILE - 1) // EXPERT_TILE, 1) * EXPERT_TILE
    ends = jnp.sum(jnp.where(jnp.arange(N_EXPERTS)[:, None] <= jnp.arange(N_EXPERTS)[None, :], padded[:, None], 0),
                   axis=0)
    starts = ends - padded
    n_tiles = (ends[-1:] // EXPERT_TILE)
    tile_rows = jnp.arange(n_grid_tiles, dtype=jnp.int32) * EXPERT_TILE
    tile_expert = jnp.minimum(jnp.sum((ends[None, :] <= tile_rows[:, None]).astype(jnp.int32), axis=1), N_EXPERTS - 1)
    slots, tokens = _queue_tokens(starts, ids, n_rows)
    ys = _experts(tile_expert, n_tiles, _sc_gather(hn, tokens), w)
    y12 = _sc_gather(ys, slots.reshape(2 * t)).reshape(2, t, ROW_CHUNKS, LANES)
    return _combine(h1, route, y12, w)


def _rope_tables(positions):
    inv_freq = 1.0 / (ROPE_THETA ** (jnp.arange(0, MLA_ROPE, 2, dtype=F32) / MLA_ROPE))
    ang = positions.astype(F32)[:, None] * inv_freq[None, :]
    cos, sin = jnp.cos(ang), jnp.sin(ang)
    reps = LANES // MLA_ROPE
    return jnp.tile(jnp.concatenate([cos, cos], axis=-1), (1, reps)), jnp.tile(jnp.concatenate([-sin, sin], axis=-1), (1, reps))


def _pack_weights(norm_mix, w_in, q_a_norm, w_uq, kv_a_norm, w_ukv, w_gate_fwd, b_gate_fwd, w_gate_bwd, b_gate_bwd,
                  gla_norm, w_out, norm_ffn, w_router_group, b_router_group, w_router_expert, b_router_expert,
                  w_expert_gate, w_expert_up, w_expert_down, norm_final):
    l = 0
    hk = GLA_HEADS * GLA_DK
    hv = GLA_HEADS * GLA_DV
    c_q, c_kv, k_pe, gq, gk, gv, lr_f, lr_b, og = jnp.split(
        w_in[l], np.cumsum([MLA_Q_RANK, MLA_KV_RANK, MLA_ROPE, hk, hk, hv, GLA_GATE_RANK, GLA_GATE_RANK])[:].tolist(),
        axis=-1)
    lr_pad = jnp.zeros((D_MODEL, LANES - 2 * GLA_GATE_RANK), F32)
    win = jnp.concatenate([c_q, c_kv, k_pe, k_pe, gq, gk, gv, og, lr_f, lr_b, lr_pad], axis=-1).astype(BF16)
    wuq = w_uq[l].reshape(MLA_Q_RANK, MLA_HEADS, MLA_NOPE + MLA_ROPE)
    wuq = jnp.concatenate([wuq[:, :, :MLA_NOPE].reshape(MLA_Q_RANK, -1), wuq[:, :, MLA_NOPE:].reshape(MLA_Q_RANK, -1)],
                          axis=-1).astype(BF16)
    wgate = jnp.zeros((LANES, 2 * hk), F32)
    wgate = wgate.at[0:GLA_GATE_RANK, 0:hk].set(w_gate_fwd[l])
    wgate = wgate.at[GLA_GATE_RANK:2 * GLA_GATE_RANK, hk:].set(w_gate_bwd[l])
    wr = jnp.zeros((D_MODEL, LANES), F32)
    wr = wr.at[:, ROUTER_GROUP_LANE0:ROUTER_GROUP_LANE0 + N_GROUPS].set(w_router_group[l])
    wr = wr.at[:, ROUTER_EXPERT_LANE0:ROUTER_EXPERT_LANE0 + N_EXPERTS].set(w_router_expert[l])
    wr_hi = wr.astype(BF16)
    br = jnp.zeros((1, LANES), F32)
    br = br.at[0, ROUTER_GROUP_LANE0:ROUTER_GROUP_LANE0 + N_GROUPS].set(b_router_group[l])
    br = br.at[0, ROUTER_EXPERT_LANE0:ROUTER_EXPERT_LANE0 + N_EXPERTS].set(b_router_expert[l])
    wgu = jnp.concatenate([w_expert_gate[l], w_expert_up[l]], axis=-1).reshape(N_EXPERTS, D_MODEL, 2 * D_EXPERT)
    return {
        'norm_mix': norm_mix[l][None], 'win': win, 'q_a_norm': q_a_norm[l][None], 'wuq': wuq,
        'kv_a_norm': kv_a_norm[l][None], 'wukv': w_ukv[l].astype(BF16),
        'wgate': wgate.astype(BF16), 'bgate': jnp.concatenate([b_gate_fwd[l], b_gate_bwd[l]])[None],
        'gla_norm': gla_norm[l][None], 'wout': w_out[l].astype(BF16), 'norm_ffn': norm_ffn[l][None],
        'wr_hi': wr_hi, 'wr_lo': (wr - wr_hi.astype(F32)).astype(BF16), 'br': br,
        'wgu': wgu.astype(BF16), 'wd': w_expert_down[l].reshape(N_EXPERTS, D_EXPERT, D_MODEL).astype(BF16),
        'norm_final': norm_final[None],
    }


def _meta_streams(meta_tokens, w):
    cos, sin = _rope_tables(jnp.arange(N_META))
    _, k, v, _, gk, gv, gf, _, _ = _inproj(meta_tokens, cos, sin, w, N_META)
    pad_keys = ((0, 0), (0, LANES - N_META), (0, 0))
    front = ((GLA_CHUNK - N_META, 0), (0, 0))
    return (jnp.pad(k, pad_keys), jnp.pad(v, pad_keys), jnp.pad(gk, front), jnp.pad(gv, front), jnp.pad(gf, front))


def _token_mixers(x, meta, w, tm, tq, tk, tb):
    bsz, seq, _ = x.shape
    km, vm, mk, mv, mg = meta
    x2d = x.reshape(bsz * seq, D_MODEL)
    cos, sin = _rope_tables(N_META + jnp.arange(seq))
    q, k, v, gq, gk, gv, gf, gb, og = _inproj(x2d, cos, sin, w, tm)
    a = _attention(q, k, v, km, vm, bsz, seq, tq, tk)
    o_f, o_b = _gla(gq, gk, gv, gf, gb, mk, mv, mg, bsz, seq, tb)
    return _mix(x2d, a, o_f, o_b, og, w, tm)


def kernel(x_prompt, x_sample, meta_tokens, norm_mix, w_in, q_a_norm, w_uq, kv_a_norm, w_ukv, w_gate_fwd, b_gate_fwd, w_gate_bwd, b_gate_bwd, gla_norm, w_out, norm_ffn, w_router_group, b_router_group, w_router_expert, b_router_expert, w_expert_gate, w_expert_up, w_expert_down, norm_final):
    w = _pack_weights(norm_mix, w_in, q_a_norm, w_uq, kv_a_norm, w_ukv, w_gate_fwd, b_gate_fwd, w_gate_bwd,
                      b_gate_bwd, gla_norm, w_out, norm_ffn, w_router_group, b_router_group, w_router_expert,
                      b_router_expert, w_expert_gate, w_expert_up, w_expert_down, norm_final)
    meta = _meta_streams(meta_tokens, w)
    outs = []
    for x in (x_prompt, x_sample):
        h1, hn, route, ids, counts = _token_mixers(x, meta, w, tm=512, tq=512, tk=512, tb=512)
        outs.append(_moe(hn, route, ids, counts, h1, w).reshape(x.shape))
    return tuple(outs)
```

```python
import functools

import numpy as np
import jax
import jax.numpy as jnp
from jax import lax
from jax.experimental import pallas as pl
from jax.experimental.pallas import tpu as pltpu
from jax.experimental.pallas import tpu_sc as plsc

F32 = jnp.float32
BF16 = jnp.bfloat16

D_MODEL = 1024
N_META = 16
MLA_HEADS = 4
MLA_Q_RANK = 384
MLA_KV_RANK = 256
MLA_NOPE = 128
MLA_ROPE = 64
MLA_V = 128
ROPE_THETA = 10000.0
GLA_HEADS = 4
GLA_DK = 64
GLA_DV = 128
GLA_GATE_RANK = 16
GLA_TAU = 16.0
GLA_CHUNK = 64
N_GROUPS = 4
EXPERTS_PER_GROUP = 8
N_EXPERTS = N_GROUPS * EXPERTS_PER_GROUP
D_EXPERT = 256
EPS = 1e-6

LANES = 128
V7X_VMEM_BYTES = 64 * 1024 * 1024
VMEM_LIMIT = V7X_VMEM_BYTES * 7 // 8

ATTN_SCALE = (MLA_NOPE + MLA_ROPE) ** -0.5 * float(np.log2(np.e))
QK_WIDTH = 2 * LANES
V_WIDTH = 2 * LANES
ATTN_GROUP = 4

C_CQ = 0
C_CKV = C_CQ + MLA_Q_RANK
C_KPE = C_CKV + MLA_KV_RANK
C_GQ = C_KPE + LANES
C_GK = C_GQ + GLA_HEADS * GLA_DK
C_GV = C_GK + GLA_HEADS * GLA_DK
C_OG = C_GV + GLA_HEADS * GLA_DV
C_LR = C_OG + GLA_HEADS * GLA_DV
D_IN_PACKED = C_LR + LANES

ROUTER_GROUP_LANE0 = 0
ROUTER_EXPERT_LANE0 = N_GROUPS


def _cparams(*semantics):
    return pltpu.CompilerParams(dimension_semantics=semantics, vmem_limit_bytes=VMEM_LIMIT)


def _rms(x, g):
    return x * lax.rsqrt(jnp.mean(x * x, axis=-1, keepdims=True) + EPS) * g


def _dot(a, b):
    return jnp.dot(a, b, preferred_element_type=F32)


def _dot_nt(a, b):
    return lax.dot_general(a, b, (((1,), (1,)), ((), ())), preferred_element_type=F32)


def _dot_tn(a, b):
    return lax.dot_general(a, b, (((0,), (0,)), ((), ())), preferred_element_type=F32)


def _full_spec(shape):
    return pl.BlockSpec(shape, lambda *_: (0,) * len(shape))


SUBLANES = 8
ROW_CHUNKS = D_MODEL // LANES
assert ROW_CHUNKS == SUBLANES


def _rows_to_tiles(x):
    chunks = jnp.stack([x[:, s * LANES:(s + 1) * LANES] for s in range(ROW_CHUNKS)], axis=0)
    return pltpu.einshape("smd->msd", chunks)


def _tiles_to_rows(x):
    chunks = pltpu.einshape("msd->smd", x)
    return jnp.concatenate([chunks[s] for s in range(ROW_CHUNKS)], axis=-1)


def _rope_pairs(x, cos, sin_signed, first_half):
    swapped = jnp.where(first_half, pltpu.roll(x, LANES - MLA_ROPE // 2, 1), pltpu.roll(x, MLA_ROPE // 2, 1))
    return x * cos + swapped * sin_signed


def _inproj_body(x_ref, cos_ref, sin_ref, nmix_ref, win_ref, qan_ref, wuq_ref, kvan_ref, wukv_ref,
                 wgate_ref, bgate_ref,
                 q_ref, k_ref, v_ref, gq_ref, gk_ref, gv_ref, gf_ref, gb_ref, og_ref):
    hn = _rms(x_ref[...], nmix_ref[...]).astype(BF16)

    def proj(lo, hi):
        return _dot(hn, win_ref[:, lo:hi])

    cos = cos_ref[...]
    sin = sin_ref[...]
    lane = lax.broadcasted_iota(jnp.int32, cos.shape, 1)
    first_half = (lane & (MLA_ROPE - 1)) < MLA_ROPE // 2
    low_lanes = lane < MLA_ROPE

    cq = _rms(proj(C_CQ, C_CKV), qan_ref[...]).astype(BF16)
    qn = _dot(cq, wuq_ref[:, 0:MLA_HEADS * MLA_NOPE]) * ATTN_SCALE
    qr = _dot(cq, wuq_ref[:, MLA_HEADS * MLA_NOPE:])
    for j in range(MLA_HEADS // 2):
        rj = (_rope_pairs(qr[:, j * LANES:(j + 1) * LANES], cos, sin, first_half) * ATTN_SCALE).astype(BF16)
        for h in (2 * j, 2 * j + 1):
            q_ref[h, :, 0:LANES] = qn[:, h * LANES:(h + 1) * LANES].astype(BF16)
            q_ref[h, :, LANES:QK_WIDTH] = rj

    ckv = _rms(proj(C_CKV, C_KPE), kvan_ref[...]).astype(BF16)
    kv = _dot(ckv, wukv_ref[...])
    kr = _rope_pairs(proj(C_KPE, C_GQ), cos, sin, first_half)
    kr_even = jnp.where(low_lanes, kr, 0.0).astype(BF16)
    kr_odd = jnp.where(low_lanes, 0.0, kr).astype(BF16)
    for h in range(MLA_HEADS):
        base = h * (MLA_NOPE + MLA_V)
        k_ref[h, :, 0:LANES] = kv[:, base:base + MLA_NOPE].astype(BF16)
        k_ref[h, :, LANES:QK_WIDTH] = kr_even if h % 2 == 0 else kr_odd
        v_ref[h, :, 0:MLA_V] = kv[:, base + MLA_NOPE:base + MLA_NOPE + MLA_V].astype(BF16)
        v_ref[h, :, MLA_V:V_WIDTH] = jnp.ones((kv.shape[0], V_WIDTH - MLA_V), BF16)

    gq_ref[...] = proj(C_GQ, C_GK) * (GLA_DK ** -0.5)
    gk_ref[...] = proj(C_GK, C_GV)
    gv_ref[...] = proj(C_GV, C_OG)
    og_ref[...] = proj(C_OG, C_LR)
    pre = _dot(proj(C_LR, D_IN_PACKED).astype(BF16), wgate_ref[...]) + bgate_ref[...]
    logsig = jnp.minimum(pre, 0.0) - jnp.log1p(jnp.exp(-jnp.abs(pre)))
    gates = logsig * (1.0 / GLA_TAU)
    gf_ref[...] = gates[:, 0:GLA_HEADS * GLA_DK]
    gb_ref[...] = gates[:, GLA_HEADS * GLA_DK:]


def _inproj(x2d, cos, sin, w, tm):
    t = x2d.shape[0]
    blocks_per_seq = cos.shape[0] // tm
    hk = GLA_HEADS * GLA_DK
    hv = GLA_HEADS * GLA_DV
    row = lambda width: pl.BlockSpec((tm, width), lambda i: (i, 0))
    head_rows = lambda width: pl.BlockSpec((MLA_HEADS, tm, width), lambda i: (0, i, 0))
    tab = pl.BlockSpec((tm, LANES), lambda i: (i % blocks_per_seq, 0))
    out_shape = (
        jax.ShapeDtypeStruct((MLA_HEADS, t, QK_WIDTH), BF16),
        jax.ShapeDtypeStruct((MLA_HEADS, t, QK_WIDTH), BF16),
        jax.ShapeDtypeStruct((MLA_HEADS, t, V_WIDTH), BF16),
        jax.ShapeDtypeStruct((t, hk), F32),
        jax.ShapeDtypeStruct((t, hk), F32),
        jax.ShapeDtypeStruct((t, hv), F32),
        jax.ShapeDtypeStruct((t, hk), F32),
        jax.ShapeDtypeStruct((t, hk), F32),
        jax.ShapeDtypeStruct((t, hv), F32),
    )
    return pl.pallas_call(
        _inproj_body,
        grid=(t // tm,),
        in_specs=[row(D_MODEL), tab, tab,
                  _full_spec((1, D_MODEL)), _full_spec((D_MODEL, D_IN_PACKED)),
                  _full_spec((1, MLA_Q_RANK)), _full_spec(w['wuq'].shape),
                  _full_spec((1, MLA_KV_RANK)), _full_spec(w['wukv'].shape),
                  _full_spec(w['wgate'].shape), _full_spec(w['bgate'].shape)],
        out_specs=(head_rows(QK_WIDTH), head_rows(QK_WIDTH), head_rows(V_WIDTH),
                   row(hk), row(hk), row(hv), row(hk), row(hk), row(hv)),
        out_shape=out_shape,
        compiler_params=_cparams("parallel"),
        name="inproj",
    )(x2d, cos, sin, w['norm_mix'], w['win'], w['q_a_norm'], w['wuq'], w['kv_a_norm'], w['wukv'],
      w['wgate'], w['bgate'])


def _attn_body(q_ref, k_ref, v_ref, km_ref, vm_ref, o_ref, s_ref, acc_ref, *, tk):
    q = q_ref[...]
    n_groups = k_ref.shape[0] // (ATTN_GROUP * tk)

    def scores(j):
        return _dot_nt(q, k_ref[pl.ds(pl.multiple_of(j * tk, tk), tk), :])

    def values(j):
        return v_ref[pl.ds(pl.multiple_of(j * tk, tk), tk), :]

    def absorb(m, s, v):
        m_new = jnp.maximum(m, jnp.max(s, axis=-1, keepdims=True))
        p = jnp.exp2(s - m_new)
        acc_ref[...] = jnp.exp2(m - m_new) * acc_ref[...] + _dot(p.astype(BF16), v)
        return m_new

    s_ref[0] = scores(0)
    sm = _dot_nt(q, km_ref[...])
    sm = jnp.where(lax.broadcasted_iota(jnp.int32, sm.shape, 1) < N_META, sm, -jnp.inf)
    m = jnp.max(sm, axis=-1, keepdims=True)
    acc_ref[...] = _dot(jnp.exp2(sm - m).astype(BF16), vm_ref[...])

    def group(g, m, last):
        for i in range(ATTN_GROUP):
            j = ATTN_GROUP * g + i
            s = s_ref[i % 2]
            if not (last and i == ATTN_GROUP - 1):
                s_ref[(i + 1) % 2] = scores(j + 1)
            m = absorb(m, s, values(j))
        return m

    m = lax.fori_loop(0, n_groups - 1, lambda g, m: group(g, m, False), m)
    group(n_groups - 1, m, True)
    acc = acc_ref[...]
    o_ref[...] = (acc[:, :MLA_V] / acc[:, MLA_V:]).astype(o_ref.dtype)


def _attention(q, k, v, km, vm, bsz, seq, tq, tk):
    nq = seq // tq
    return pl.pallas_call(
        functools.partial(_attn_body, tk=tk),
        grid=(bsz, MLA_HEADS, nq),
        in_specs=[pl.BlockSpec((None, tq, QK_WIDTH), lambda b, h, i: (h, b * nq + i, 0)),
                  pl.BlockSpec((None, seq, QK_WIDTH), lambda b, h, i: (h, b, 0)),
                  pl.BlockSpec((None, seq, V_WIDTH), lambda b, h, i: (h, b, 0)),
                  pl.BlockSpec((None, LANES, QK_WIDTH), lambda b, h, i: (h, 0, 0)),
                  pl.BlockSpec((None, LANES, V_WIDTH), lambda b, h, i: (h, 0, 0))],
        out_specs=pl.BlockSpec((tq, MLA_V), lambda b, h, i: (b * nq + i, h)),
        out_shape=jax.ShapeDtypeStruct((bsz * seq, MLA_HEADS * MLA_V), BF16),
        scratch_shapes=[pltpu.VMEM((2, tq, tk), F32), pltpu.VMEM((tq, V_WIDTH), F32)],
        compiler_params=_cparams("parallel", "parallel", "arbitrary"),
        name="mla_attention",
    )(q, k, v, km, vm)


def _split3(x):
    hi = x.astype(BF16)
    r1 = x - hi.astype(F32)
    mid = r1.astype(BF16)
    lo = (r1 - mid.astype(F32)).astype(BF16)
    return hi, mid, lo


def _gla_chunk(q, k, g, v_even, v_odd, state, tri, keep, mid, last):
    lane = lax.broadcasted_iota(jnp.int32, (GLA_CHUNK, LANES), 1)
    even = lane < GLA_DK
    g_hi, g_mid, g_lo = _split3(g)
    b = _dot(tri, g_hi) + _dot(tri, g_mid) + _dot(tri, g_lo)
    b_last = b[last:last + 1, :]
    ke = (k * jnp.exp(b_last - b)).astype(BF16)
    upd = jnp.where(lax.broadcasted_iota(jnp.int32, (GLA_DV, LANES), 1) < GLA_DK,
                    _dot_tn(v_even, ke), _dot_tn(v_odd, ke))
    new_state = state * jnp.exp(b_last) + upd
    if q is None:
        return new_state, None, None
    b_mid = b[mid:mid + 1, :]
    qs = q * jnp.exp(b - b_mid)
    ks = (k * jnp.exp(b_mid - b)).astype(BF16)
    qs2 = jnp.concatenate([jnp.where(even, qs, 0.0), jnp.where(even, 0.0, qs)], axis=0).astype(BF16)
    scores = jnp.where(keep, _dot_nt(qs2, ks), 0.0).astype(BF16)
    qe = q * jnp.exp(b)
    qe2 = jnp.concatenate([jnp.where(even, qe, 0.0), jnp.where(even, 0.0, qe)], axis=0).astype(BF16)
    inter = _dot_nt(qe2, state.astype(BF16))
    o_even = _dot(scores[:GLA_CHUNK], v_even) + inter[:GLA_CHUNK]
    o_odd = _dot(scores[GLA_CHUNK:], v_odd) + inter[GLA_CHUNK:]
    return new_state, o_even, o_odd


def _gla_body(qf_ref, kf_ref, vf_ref, gf_ref, qb_ref, kb_ref, vb_ref, gb_ref, mk_ref, mv_ref, mg_ref,
              of_ref, ob_ref, state_ref):
    n_chunks = qf_ref.shape[0] // GLA_CHUNK
    n_pairs = GLA_HEADS // 2
    r = lax.broadcasted_iota(jnp.int32, (GLA_CHUNK, GLA_CHUNK), 0)
    c = lax.broadcasted_iota(jnp.int32, (GLA_CHUNK, GLA_CHUNK), 1)
    tri_f = jnp.where(c <= r, 1.0, 0.0).astype(BF16)
    tri_b = jnp.where(c >= r, 1.0, 0.0).astype(BF16)
    r2 = lax.broadcasted_iota(jnp.int32, (2 * GLA_CHUNK, GLA_CHUNK), 0) & (GLA_CHUNK - 1)
    c2 = lax.broadcasted_iota(jnp.int32, (2 * GLA_CHUNK, GLA_CHUNK), 1)
    keep_f = c2 <= r2
    keep_b = c2 >= r2
    mid_f, last_f = GLA_CHUNK // 2 - 1, GLA_CHUNK - 1
    mid_b, last_b = GLA_CHUNK // 2, 0

    def pair_cols(p):
        return slice(p * LANES, (p + 1) * LANES)

    def head_cols(h):
        return slice(h * GLA_DV, (h + 1) * GLA_DV)

    @pl.when(pl.program_id(1) == 0)
    def _():
        for p in range(n_pairs):
            zero = jnp.zeros((GLA_DV, LANES), F32)
            st, _, _ = _gla_chunk(None, mk_ref[:, pair_cols(p)], mg_ref[:, pair_cols(p)],
                                  mv_ref[:, head_cols(2 * p)].astype(BF16),
                                  mv_ref[:, head_cols(2 * p + 1)].astype(BF16),
                                  zero, tri_f, keep_f, mid_f, last_f)
            state_ref[p] = st
            state_ref[n_pairs + p] = zero

    def step(i, _):
        rows_f = pl.ds(pl.multiple_of(i * GLA_CHUNK, GLA_CHUNK), GLA_CHUNK)
        rows_b = pl.ds(pl.multiple_of((n_chunks - 1 - i) * GLA_CHUNK, GLA_CHUNK), GLA_CHUNK)
        for p in range(n_pairs):
            for (rows, q_ref, k_ref, v_ref, g_ref, o_ref, slot, tri, keep, mid, last) in (
                    (rows_f, qf_ref, kf_ref, vf_ref, gf_ref, of_ref, p, tri_f, keep_f, mid_f, last_f),
                    (rows_b, qb_ref, kb_ref, vb_ref, gb_ref, ob_ref, n_pairs + p, tri_b, keep_b, mid_b, last_b)):
                st, o_even, o_odd = _gla_chunk(
                    q_ref[rows, pair_cols(p)], k_ref[rows, pair_cols(p)], g_ref[rows, pair_cols(p)],
                    v_ref[rows, head_cols(2 * p)].astype(BF16), v_ref[rows, head_cols(2 * p + 1)].astype(BF16),
                    state_ref[slot], tri, keep, mid, last)
                state_ref[slot] = st
                o_ref[rows, head_cols(2 * p)] = o_even
                o_ref[rows, head_cols(2 * p + 1)] = o_odd
        return 0

    lax.fori_loop(0, n_chunks, step, 0)


def _gla(gq, gk, gv, gf, gb, mk, mv, mg, bsz, seq, tb):
    nb = seq // tb
    hk = GLA_HEADS * GLA_DK
    hv = GLA_HEADS * GLA_DV
    fwd = lambda width: pl.BlockSpec((tb, width), lambda b, j: (b * nb + j, 0))
    bwd = lambda width: pl.BlockSpec((tb, width), lambda b, j: (b * nb + nb - 1 - j, 0))
    t = bsz * seq
    return pl.pallas_call(
        _gla_body,
        grid=(bsz, nb),
        in_specs=[fwd(hk), fwd(hk), fwd(hv), fwd(hk), bwd(hk), bwd(hk), bwd(hv), bwd(hk),
                  _full_spec(mk.shape), _full_spec(mv.shape), _full_spec(mg.shape)],
        out_specs=(fwd(hv), bwd(hv)),
        out_shape=(jax.ShapeDtypeStruct((t, hv), F32), jax.ShapeDtypeStruct((t, hv), F32)),
        scratch_shapes=[pltpu.VMEM((2 * (GLA_HEADS // 2), GLA_DV, LANES), F32)],
        compiler_params=_cparams("parallel", "arbitrary"),
        name="gla_scan",
    )(gq, gk, gv, gf, gq, gk, gv, gb, mk, mv, mg)


def _mix_body(x_ref, a_ref, of_ref, ob_ref, og_ref, gnorm_ref, wout_ref, nffn_ref, wr_hi_ref, wr_lo_ref, br_ref,
              h1_ref, hn_ref, route_ref, ids_ref, count_ref, tri_ref):
    tm = x_ref.shape[0]

    @pl.when(pl.program_id(0) == 0)
    def _():
        r = lax.broadcasted_iota(jnp.int32, (tm, tm), 0)
        c = lax.broadcasted_iota(jnp.int32, (tm, tm), 1)
        tri_ref[...] = jnp.where(c < r, 1.0, 0.0).astype(BF16)
        count_ref[...] = jnp.zeros_like(count_ref)

    a_width = MLA_HEADS * MLA_V
    h1 = x_ref[...] + _dot(a_ref[...], wout_ref[0:a_width, :])
    for h in range(GLA_HEADS):
        cols = slice(h * GLA_DV, (h + 1) * GLA_DV)
        o = of_ref[:, cols] + ob_ref[:, cols]
        og = og_ref[:, cols]
        silu = og / (1.0 + jnp.exp(-og))
        gh = (_rms(o, gnorm_ref[...]) * silu).astype(BF16)
        h1 = h1 + _dot(gh, wout_ref[a_width + h * GLA_DV:a_width + (h + 1) * GLA_DV, :])
    h1_ref[...] = h1
    hn = _rms(h1, nffn_ref[...])
    hn_ref[...] = _rows_to_tiles(hn)

    hn_hi = hn.astype(BF16)
    hn_lo = (hn - hn_hi.astype(F32)).astype(BF16)
    logits = (_dot(hn_hi, wr_hi_ref[...]) + _dot(hn_lo, wr_hi_ref[...]) + _dot(hn_hi, wr_lo_ref[...])
              + br_ref[...])
    lane = lax.broadcasted_iota(jnp.int32, logits.shape, 1).astype(F32)
    none = float(LANES)
    neg = -jnp.inf

    def lane_max(x):
        return jnp.max(x, axis=-1, keepdims=True)

    def lane_sum(x):
        return jnp.sum(x, axis=-1, keepdims=True)

    def first_lane(mask):
        return jnp.min(jnp.where(mask, lane, none), axis=-1, keepdims=True)

    is_group = lane < float(N_GROUPS)
    g_max = lane_max(jnp.where(is_group, logits, neg))
    g_exp = jnp.where(is_group, jnp.exp(logits - g_max), 0.0)
    g_prob = g_exp / lane_sum(g_exp)
    g_w = lane_max(g_prob)
    g_idx = first_lane(is_group & (g_prob == g_w))
    e_lo = float(ROUTER_EXPERT_LANE0) + float(EXPERTS_PER_GROUP) * g_idx
    sel = (lane >= e_lo) & (lane < e_lo + float(EXPERTS_PER_GROUP))
    e_max = lane_max(jnp.where(sel, logits, neg))
    e_exp = jnp.where(sel, jnp.exp(logits - e_max), 0.0)
    e_prob = e_exp / lane_sum(e_exp)
    p1 = lane_max(jnp.where(sel, e_prob, neg))
    i1 = first_lane(sel & (e_prob == p1))
    rest = sel & (lane != i1)
    p2 = lane_max(jnp.where(rest, e_prob, neg))
    i2 = first_lane(rest & (e_prob == p2))
    denom = p1 + p2
    chosen = jnp.where((lane == i1) | (lane == i2), 1.0, 0.0)
    rank = count_ref[...] + _dot(tri_ref[...], chosen.astype(BF16))
    count_ref[...] += jnp.sum(chosen, axis=0, keepdims=True)
    fields = (i1 - float(ROUTER_EXPERT_LANE0), i2 - float(ROUTER_EXPERT_LANE0),
              lane_sum(jnp.where(lane == i1, rank, 0.0)), lane_sum(jnp.where(lane == i2, rank, 0.0)),
              g_w * (p1 / denom), g_w * (p2 / denom))
    route = jnp.zeros_like(logits)
    for k, value in enumerate(fields):
        route = jnp.where(lane == float(k), value, route)
    route_ref[...] = route
    ids_ref[...] = jnp.transpose(route)[0:ROUTE_ID_ROWS, :].astype(jnp.int32)


ROUTE_E1, ROUTE_E2, ROUTE_RANK1, ROUTE_RANK2, ROUTE_W1, ROUTE_W2 = range(6)
ROUTE_ID_ROWS = 8


def _mix(x2d, a, o_f, o_b, og, w, tm):
    t = x2d.shape[0]
    hv = GLA_HEADS * GLA_DV
    row = lambda width: pl.BlockSpec((tm, width), lambda i: (i, 0))
    return pl.pallas_call(
        _mix_body,
        grid=(t // tm,),
        in_specs=[row(D_MODEL), row(MLA_HEADS * MLA_V), row(hv), row(hv), row(hv),
                  _full_spec((1, GLA_DV)), _full_spec(w['wout'].shape), _full_spec((1, D_MODEL)),
                  _full_spec(w['wr_hi'].shape), _full_spec(w['wr_lo'].shape), _full_spec(w['br'].shape)],
        out_specs=(row(D_MODEL), pl.BlockSpec((tm, ROW_CHUNKS, LANES), lambda i: (i, 0, 0)), row(LANES),
                   pl.BlockSpec((ROUTE_ID_ROWS, tm), lambda i: (0, i)), _full_spec((1, LANES))),
        out_shape=(jax.ShapeDtypeStruct((t, D_MODEL), F32), jax.ShapeDtypeStruct((t, ROW_CHUNKS, LANES), F32),
                   jax.ShapeDtypeStruct((t, LANES), F32), jax.ShapeDtypeStruct((ROUTE_ID_ROWS, t), jnp.int32),
                   jax.ShapeDtypeStruct((1, LANES), F32)),
        scratch_shapes=[pltpu.VMEM((tm, tm), BF16)],
        compiler_params=_cparams("arbitrary"),
        name="mix_router",
    )(x2d, a, o_f, o_b, og, w['gla_norm'], w['wout'], w['norm_ffn'], w['wr_hi'], w['wr_lo'], w['br'])


EXPERT_TILE = 256
ROW_TILE = 256


SC_CORES = 2
SC_SUBCORES = 16
SC_GATHER_ROWS = 32


def _sc_gather(table, idx):
    n = idx.shape[0]
    workers = SC_CORES * SC_SUBCORES
    per_worker = n // workers
    assert n % (workers * SC_GATHER_ROWS) == 0
    mesh = plsc.VectorSubcoreMesh(core_axis_name="c", subcore_axis_name="s")

    @functools.partial(
        pl.kernel, mesh=mesh,
        out_type=jax.ShapeDtypeStruct((n,) + table.shape[1:], table.dtype),
        scratch_types=[pltpu.VMEM((SC_GATHER_ROWS,), jnp.int32),
                       pltpu.VMEM((SC_GATHER_ROWS,) + table.shape[1:], table.dtype),
                       pltpu.SemaphoreType.DMA])
    def gather(table_ref, idx_ref, out_ref, idx_buf, rows_buf, sem):
        base = (lax.axis_index("s") * SC_CORES + lax.axis_index("c")) * per_worker

        @pl.loop(0, per_worker // SC_GATHER_ROWS)
        def _(j):
            rows = pl.ds(base + j * SC_GATHER_ROWS, SC_GATHER_ROWS)
            pltpu.sync_copy(idx_ref.at[rows], idx_buf)
            pltpu.async_copy(table_ref.at[idx_buf], rows_buf, sem).wait()
            pltpu.sync_copy(rows_buf, out_ref.at[rows])

    return gather(table, idx)


def _queue_body(starts_ref, ids_ref, slots_ref, token_ref):
    i = pl.program_id(0)

    @pl.when(i == 0)
    def _():
        n_tokens = pl.num_programs(0) * ROW_TILE

        def clear(j, carry):
            token_ref[j] = lax.rem(j, n_tokens)
            return carry
        lax.fori_loop(0, token_ref.shape[0], clear, 0, unroll=8)

    def place(r, carry):
        for k in range(2):
            slot = starts_ref[ids_ref[ROUTE_E1 + k, r]] + ids_ref[ROUTE_RANK1 + k, r]
            slots_ref[k, r] = slot
            token_ref[slot] = i * ROW_TILE + r
        return carry

    lax.fori_loop(0, ROW_TILE, place, 0, unroll=8)


def _queue_tokens(starts, ids, n_rows):
    t = ids.shape[1]
    ids_block = pl.BlockSpec((ROUTE_ID_ROWS, ROW_TILE), lambda i, *_: (0, i), memory_space=pltpu.SMEM)
    return pl.pallas_call(
        _queue_body,
        grid_spec=pltpu.PrefetchScalarGridSpec(
            num_scalar_prefetch=1,
            grid=(t // ROW_TILE,),
            in_specs=[ids_block],
            out_specs=(pl.BlockSpec((2, ROW_TILE), lambda i, *_: (0, i), memory_space=pltpu.SMEM),
                       pl.BlockSpec(memory_space=pltpu.SMEM))),
        out_shape=(jax.ShapeDtypeStruct((2, t), jnp.int32), jax.ShapeDtypeStruct((n_rows,), jnp.int32)),
        compiler_params=_cparams("arbitrary"),
        name="moe_queue",
    )(starts, ids)


def _expert_body(tile_expert_ref, n_tiles_ref, xs_ref, wgu_ref, wd_ref, ys_ref):
    used = pl.program_id(0) < n_tiles_ref[0]

    @pl.when(used)
    def _():
        h = _dot(_tiles_to_rows(xs_ref[...]).astype(BF16), wgu_ref[...])
        hg = h[:, :D_EXPERT]
        act = (hg / (1.0 + jnp.exp(-hg))) * h[:, D_EXPERT:]
        ys_ref[...] = _rows_to_tiles(_dot(act.astype(BF16), wd_ref[...]))

    @pl.when(jnp.logical_not(used))
    def _():
        ys_ref[...] = jnp.zeros_like(ys_ref)


def _experts(tile_expert, n_tiles, xs, w):
    n_grid = xs.shape[0] // EXPERT_TILE

    def tile(i, tile_expert, n_tiles):
        return jnp.minimum(i, n_tiles[0] - 1)

    tiles = lambda index: pl.BlockSpec((EXPERT_TILE, ROW_CHUNKS, LANES), index)
    return pl.pallas_call(
        _expert_body,
        grid_spec=pltpu.PrefetchScalarGridSpec(
            num_scalar_prefetch=2,
            grid=(n_grid,),
            in_specs=[tiles(lambda i, te, nt: (tile(i, te, nt), 0, 0)),
                      pl.BlockSpec((None, D_MODEL, 2 * D_EXPERT), lambda i, te, nt: (te[tile(i, te, nt)], 0, 0)),
                      pl.BlockSpec((None, D_EXPERT, D_MODEL), lambda i, te, nt: (te[tile(i, te, nt)], 0, 0))],
            out_specs=tiles(lambda i, te, nt: (i, 0, 0))),
        out_shape=jax.ShapeDtypeStruct(xs.shape, F32),
        compiler_params=_cparams("arbitrary"),
        name="moe_experts",
    )(tile_expert, n_tiles, xs, w['wgu'], w['wd'])


def _combine_body(h1_ref, route_ref, nfin_ref, y1_ref, y2_ref, out_ref):
    route = route_ref[...]
    lane = lax.broadcasted_iota(jnp.int32, route.shape, 1)
    w1 = jnp.sum(jnp.where(lane == ROUTE_W1, route, 0.0), axis=-1, keepdims=True)
    w2 = jnp.sum(jnp.where(lane == ROUTE_W2, route, 0.0), axis=-1, keepdims=True)
    y = w1 * _tiles_to_rows(y1_ref[...]) + w2 * _tiles_to_rows(y2_ref[...])
    out_ref[...] = _rms(h1_ref[...] + y, nfin_ref[...])


def _combine(h1, route, y12, w):
    t = h1.shape[0]
    row = lambda width: pl.BlockSpec((ROW_TILE, width), lambda i: (i, 0))
    tiles = lambda k: pl.BlockSpec((None, ROW_TILE, ROW_CHUNKS, LANES), lambda i: (k, i, 0, 0))
    return pl.pallas_call(
        _combine_body,
        grid=(t // ROW_TILE,),
        in_specs=[row(D_MODEL), row(LANES), _full_spec((1, D_MODEL)), tiles(0), tiles(1)],
        out_specs=row(D_MODEL),
        out_shape=jax.ShapeDtypeStruct((t, D_MODEL), F32),
        compiler_params=_cparams("parallel"),
        name="moe_combine",
    )(h1, route, w['norm_final'], y12, y12)


def _moe(hn, route, ids, counts, h1, w):
    t = hn.shape[0]
    n_rows = 2 * t + N_EXPERTS * EXPERT_TILE
    n_grid_tiles = n_rows // EXPERT_TILE
    count = counts[0, ROUTER_EXPERT_LANE0:ROUTER_EXPERT_LANE0 + N_EXPERTS].astype(jnp.int32)
    padded = jnp.maximum((count + EXPERT_TILE - 1) // EXPERT_TILE, 1) * EXPERT_TILE
    ends = jnp.sum(jnp.where(jnp.arange(N_EXPERTS)[:, None] <= jnp.arange(N_EXPERTS)[None, :], padded[:, None], 0),
                   axis=0)
    starts = ends - padded
    n_tiles = (ends[-1:] // EXPERT_TILE)
    tile_rows = jnp.arange(n_grid_tiles, dtype=jnp.int32) * EXPERT_TILE
    tile_expert = jnp.minimum(jnp.sum((ends[None, :] <= tile_rows[:, None]).astype(jnp.int32), axis=1), N_EXPERTS - 1)
    slots, tokens = _queue_tokens(starts, ids, n_rows)
    ys = _experts(tile_expert, n_tiles, _sc_gather(hn, tokens), w)
    y12 = _sc_gather(ys, slots.reshape(2 * t)).reshape(2, t, ROW_CHUNKS, LANES)
    return _combine(h1, route, y12, w)


def _rope_tables(positions):
    inv_freq = 1.0 / (ROPE_THETA ** (jnp.arange(0, MLA_ROPE, 2, dtype=F32) / MLA_ROPE))
    ang = positions.astype(F32)[:, None] * inv_freq[None, :]
    cos, sin = jnp.cos(ang), jnp.sin(ang)
    reps = LANES // MLA_ROPE
    return jnp.tile(jnp.concatenate([cos, cos], axis=-1), (1, reps)), jnp.tile(jnp.concatenate([-sin, sin], axis=-1), (1, reps))


def _pack_weights(norm_mix, w_in, q_a_norm, w_uq, kv_a_norm, w_ukv, w_gate_fwd, b_gate_fwd, w_gate_bwd, b_gate_bwd,
                  gla_norm, w_out, norm_ffn, w_router_group, b_router_group, w_router_expert, b_router_expert,
                  w_expert_gate, w_expert_up, w_expert_down, norm_final):
    l = 0
    hk = GLA_HEADS * GLA_DK
    hv = GLA_HEADS * GLA_DV
    c_q, c_kv, k_pe, gq, gk, gv, lr_f, lr_b, og = jnp.split(
        w_in[l], np.cumsum([MLA_Q_RANK, MLA_KV_RANK, MLA_ROPE, hk, hk, hv, GLA_GATE_RANK, GLA_GATE_RANK])[:].tolist(),
        axis=-1)
    lr_pad = jnp.zeros((D_MODEL, LANES - 2 * GLA_GATE_RANK), F32)
    win = jnp.concatenate([c_q, c_kv, k_pe, k_pe, gq, gk, gv, og, lr_f, lr_b, lr_pad], axis=-1).astype(BF16)
    wuq = w_uq[l].reshape(MLA_Q_RANK, MLA_HEADS, MLA_NOPE + MLA_ROPE)
    wuq = jnp.concatenate([wuq[:, :, :MLA_NOPE].reshape(MLA_Q_RANK, -1), wuq[:, :, MLA_NOPE:].reshape(MLA_Q_RANK, -1)],
                          axis=-1).astype(BF16)
    wgate = jnp.zeros((LANES, 2 * hk), F32)
    wgate = wgate.at[0:GLA_GATE_RANK, 0:hk].set(w_gate_fwd[l])
    wgate = wgate.at[GLA_GATE_RANK:2 * GLA_GATE_RANK, hk:].set(w_gate_bwd[l])
    wr = jnp.zeros((D_MODEL, LANES), F32)
    wr = wr.at[:, ROUTER_GROUP_LANE0:ROUTER_GROUP_LANE0 + N_GROUPS].set(w_router_group[l])
    wr = wr.at[:, ROUTER_EXPERT_LANE0:ROUTER_EXPERT_LANE0 + N_EXPERTS].set(w_router_expert[l])
    wr_hi = wr.astype(BF16)
    br = jnp.zeros((1, LANES), F32)
    br = br.at[0, ROUTER_GROUP_LANE0:ROUTER_GROUP_LANE0 + N_GROUPS].set(b_router_group[l])
    br = br.at[0, ROUTER_EXPERT_LANE0:ROUTER_EXPERT_LANE0 + N_EXPERTS].set(b_router_expert[l])
    wgu = jnp.concatenate([w_expert_gate[l], w_expert_up[l]], axis=-1).reshape(N_EXPERTS, D_MODEL, 2 * D_EXPERT)
    return {
        'norm_mix': norm_mix[l][None], 'win': win, 'q_a_norm': q_a_norm[l][None], 'wuq': wuq,
        'kv_a_norm': kv_a_norm[l][None], 'wukv': w_ukv[l].astype(BF16),
        'wgate': wgate.astype(BF16), 'bgate': jnp.concatenate([b_gate_fwd[l], b_gate_bwd[l]])[None],
        'gla_norm': gla_norm[l][None], 'wout': w_out[l].astype(BF16), 'norm_ffn': norm_ffn[l][None],
        'wr_hi': wr_hi, 'wr_lo': (wr - wr_hi.astype(F32)).astype(BF16), 'br': br,
        'wgu': wgu.astype(BF16), 'wd': w_expert_down[l].reshape(N_EXPERTS, D_EXPERT, D_MODEL).astype(BF16),
        'norm_final': norm_final[None],
    }


def _meta_streams(meta_tokens, w):
    cos, sin = _rope_tables(jnp.arange(N_META))
    _, k, v, _, gk, gv, gf, _, _ = _inproj(meta_tokens, cos, sin, w, N_META)
    pad_keys = ((0, 0), (0, LANES - N_META), (0, 0))
    front = ((GLA_CHUNK - N_META, 0), (0, 0))
    return (jnp.pad(k, pad_keys), jnp.pad(v, pad_keys), jnp.pad(gk, front), jnp.pad(gv, front), jnp.pad(gf, front))


def _token_mixers(x, meta, w, tm, tq, tk, tb):
    bsz, seq, _ = x.shape
    km, vm, mk, mv, mg = meta
    x2d = x.reshape(bsz * seq, D_MODEL)
    cos, sin = _rope_tables(N_META + jnp.arange(seq))
    q, k, v, gq, gk, gv, gf, gb, og = _inproj(x2d, cos, sin, w, tm)
    a = _attention(q, k, v, km, vm, bsz, seq, tq, tk)
    o_f, o_b = _gla(gq, gk, gv, gf, gb, mk, mv, mg, bsz, seq, tb)
    return _mix(x2d, a, o_f, o_b, og, w, tm)


def kernel(x_prompt, x_sample, meta_tokens, norm_mix, w_in, q_a_norm, w_uq, kv_a_norm, w_ukv, w_gate_fwd, b_gate_fwd, w_gate_bwd, b_gate_bwd, gla_norm, w_out, norm_ffn, w_router_group, b_router_group, w_router_expert, b_router_expert, w_expert_gate, w_expert_up, w_expert_down, norm_final):
    w = _pack_weights(norm_mix, w_in, q_a_norm, w_uq, kv_a_norm, w_ukv, w_gate_fwd, b_gate_fwd, w_gate_bwd,
                      b_gate_bwd, gla_norm, w_out, norm_ffn, w_router_group, b_router_group, w_router_expert,
                      b_router_expert, w_expert_gate, w_expert_up, w_expert_down, norm_final)
    meta = _meta_streams(meta_tokens, w)
    outs = []
    for x in (x_prompt, x_sample):
        h1, hn, route, ids, counts = _token_mixers(x, meta, w, tm=512, tq=512, tk=512, tb=512)
        outs.append(_moe(hn, route, ids, counts, h1, w).reshape(x.shape))
    return tuple(outs)
```

```python
import functools

import numpy as np
import jax
import jax.numpy as jnp
from jax import lax
from jax.experimental import pallas as pl
from jax.experimental.pallas import tpu as pltpu
from jax.experimental.pallas import tpu_sc as plsc

F32 = jnp.float32
BF16 = jnp.bfloat16

D_MODEL = 1024
N_META = 16
MLA_HEADS = 4
MLA_Q_RANK = 384
MLA_KV_RANK = 256
MLA_NOPE = 128
MLA_ROPE = 64
MLA_V = 128
ROPE_THETA = 10000.0
GLA_HEADS = 4
GLA_DK = 64
GLA_DV = 128
GLA_GATE_RANK = 16
GLA_TAU = 16.0
GLA_CHUNK = 64
N_GROUPS = 4
EXPERTS_PER_GROUP = 8
N_EXPERTS = N_GROUPS * EXPERTS_PER_GROUP
D_EXPERT = 256
EPS = 1e-6

LANES = 128
V7X_VMEM_BYTES = 64 * 1024 * 1024
VMEM_LIMIT = V7X_VMEM_BYTES * 7 // 8

ATTN_SCALE = (MLA_NOPE + MLA_ROPE) ** -0.5 * float(np.log2(np.e))
QK_WIDTH = 2 * LANES
V_WIDTH = 2 * LANES
ATTN_GROUP = 4

C_CQ = 0
C_CKV = C_CQ + MLA_Q_RANK
C_KPE = C_CKV + MLA_KV_RANK
C_GQ = C_KPE + LANES
C_GK = C_GQ + GLA_HEADS * GLA_DK
C_GV = C_GK + GLA_HEADS * GLA_DK
C_OG = C_GV + GLA_HEADS * GLA_DV
C_LR = C_OG + GLA_HEADS * GLA_DV
D_IN_PACKED = C_LR + LANES

ROUTER_GROUP_LANE0 = 0
ROUTER_EXPERT_LANE0 = N_GROUPS


def _cparams(*semantics):
    return pltpu.CompilerParams(dimension_semantics=semantics, vmem_limit_bytes=VMEM_LIMIT)


def _rms(x, g):
    return x * lax.rsqrt(jnp.mean(x * x, axis=-1, keepdims=True) + EPS) * g


def _dot(a, b):
    return jnp.dot(a, b, preferred_element_type=F32)


def _dot_nt(a, b):
    return lax.dot_general(a, b, (((1,), (1,)), ((), ())), preferred_element_type=F32)


def _dot_tn(a, b):
    return lax.dot_general(a, b, (((0,), (0,)), ((), ())), preferred_element_type=F32)


def _full_spec(shape):
    return pl.BlockSpec(shape, lambda *_: (0,) * len(shape))


SUBLANES = 8
ROW_CHUNKS = D_MODEL // LANES
assert ROW_CHUNKS == SUBLANES


def _rows_to_tiles(x):
    chunks = jnp.stack([x[:, s * LANES:(s + 1) * LANES] for s in range(ROW_CHUNKS)], axis=0)
    return pltpu.einshape("smd->msd", chunks)


def _tiles_to_rows(x):
    chunks = pltpu.einshape("msd->smd", x)
    return jnp.concatenate([chunks[s] for s in range(ROW_CHUNKS)], axis=-1)


def _rope_pairs(x, cos, sin_signed, first_half):
    swapped = jnp.where(first_half, pltpu.roll(x, LANES - MLA_ROPE // 2, 1), pltpu.roll(x, MLA_ROPE // 2, 1))
    return x * cos + swapped * sin_signed


def _inproj_body(x_ref, cos_ref, sin_ref, nmix_ref, win_ref, qan_ref, wuq_ref, kvan_ref, wukv_ref,
                 wgate_ref, bgate_ref,
                 q_ref, k_ref, v_ref, gq_ref, gk_ref, gv_ref, gf_ref, gb_ref, og_ref):
    hn = _rms(x_ref[...], nmix_ref[...]).astype(BF16)

    def proj(lo, hi):
        return _dot(hn, win_ref[:, lo:hi])

    cos = cos_ref[...]
    sin = sin_ref[...]
    lane = lax.broadcasted_iota(jnp.int32, cos.shape, 1)
    first_half = (lane & (MLA_ROPE - 1)) < MLA_ROPE // 2
    low_lanes = lane < MLA_ROPE

    cq = _rms(proj(C_CQ, C_CKV), qan_ref[...]).astype(BF16)
    qn = _dot(cq, wuq_ref[:, 0:MLA_HEADS * MLA_NOPE]) * ATTN_SCALE
    qr = _dot(cq, wuq_ref[:, MLA_HEADS * MLA_NOPE:])
    for j in range(MLA_HEADS // 2):
        rj = (_rope_pairs(qr[:, j * LANES:(j + 1) * LANES], cos, sin, first_half) * ATTN_SCALE).astype(BF16)
        for h in (2 * j, 2 * j + 1):
            q_ref[h, :, 0:LANES] = qn[:, h * LANES:(h + 1) * LANES].astype(BF16)
            q_ref[h, :, LANES:QK_WIDTH] = rj

    ckv = _rms(proj(C_CKV, C_KPE), kvan_ref[...]).astype(BF16)
    kv = _dot(ckv, wukv_ref[...])
    kr = _rope_pairs(proj(C_KPE, C_GQ), cos, sin, first_half)
    kr_even = jnp.where(low_lanes, kr, 0.0).astype(BF16)
    kr_odd = jnp.where(low_lanes, 0.0, kr).astype(BF16)
    for h in range(MLA_HEADS):
        base = h * (MLA_NOPE + MLA_V)
        k_ref[h, :, 0:LANES] = kv[:, base:base + MLA_NOPE].astype(BF16)
        k_ref[h, :, LANES:QK_WIDTH] = kr_even if h % 2 == 0 else kr_odd
        v_ref[h, :, 0:MLA_V] = kv[:, base + MLA_NOPE:base + MLA_NOPE + MLA_V].astype(BF16)
        v_ref[h, :, MLA_V:V_WIDTH] = jnp.ones((kv.shape[0], V_WIDTH - MLA_V), BF16)

    gq_ref[...] = proj(C_GQ, C_GK) * (GLA_DK ** -0.5)
    gk_ref[...] = proj(C_GK, C_GV)
    gv_ref[...] = proj(C_GV, C_OG)
    og_ref[...] = proj(C_OG, C_LR)
    pre = _dot(proj(C_LR, D_IN_PACKED).astype(BF16), wgate_ref[...]) + bgate_ref[...]
    logsig = jnp.minimum(pre, 0.0) - jnp.log1p(jnp.exp(-jnp.abs(pre)))
    gates = logsig * (1.0 / GLA_TAU)
    gf_ref[...] = gates[:, 0:GLA_HEADS * GLA_DK]
    gb_ref[...] = gates[:, GLA_HEADS * GLA_DK:]


def _inproj(x2d, cos, sin, w, tm):
    t = x2d.shape[0]
    blocks_per_seq = cos.shape[0] // tm
    hk = GLA_HEADS * GLA_DK
    hv = GLA_HEADS * GLA_DV
    row = lambda width: pl.BlockSpec((tm, width), lambda i: (i, 0))
    head_rows = lambda width: pl.BlockSpec((MLA_HEADS, tm, width), lambda i: (0, i, 0))
    tab = pl.BlockSpec((tm, LANES), lambda i: (i % blocks_per_seq, 0))
    out_shape = (
        jax.ShapeDtypeStruct((MLA_HEADS, t, QK_WIDTH), BF16),
        jax.ShapeDtypeStruct((MLA_HEADS, t, QK_WIDTH), BF16),
        jax.ShapeDtypeStruct((MLA_HEADS, t, V_WIDTH), BF16),
        jax.ShapeDtypeStruct((t, hk), F32),
        jax.ShapeDtypeStruct((t, hk), F32),
        jax.ShapeDtypeStruct((t, hv), F32),
        jax.ShapeDtypeStruct((t, hk), F32),
        jax.ShapeDtypeStruct((t, hk), F32),
        jax.ShapeDtypeStruct((t, hv), F32),
    )
    return pl.pallas_call(
        _inproj_body,
        grid=(t // tm,),
        in_specs=[row(D_MODEL), tab, tab,
                  _full_spec((1, D_MODEL)), _full_spec((D_MODEL, D_IN_PACKED)),
                  _full_spec((1, MLA_Q_RANK)), _full_spec(w['wuq'].shape),
                  _full_spec((1, MLA_KV_RANK)), _full_spec(w['wukv'].shape),
                  _full_spec(w['wgate'].shape), _full_spec(w['bgate'].shape)],
        out_specs=(head_rows(QK_WIDTH), head_rows(QK_WIDTH), head_rows(V_WIDTH),
                   row(hk), row(hk), row(hv), row(hk), row(hk), row(hv)),
        out_shape=out_shape,
        compiler_params=_cparams("parallel"),
        name="inproj",
    )(x2d, cos, sin, w['norm_mix'], w['win'], w['q_a_norm'], w['wuq'], w['kv_a_norm'], w['wukv'],
      w['wgate'], w['bgate'])


def _attn_body(q_ref, k_ref, v_ref, km_ref, vm_ref, o_ref, s_ref, acc_ref, *, tk):
    q = q_ref[...]
    n_groups = k_ref.shape[0] // (ATTN_GROUP * tk)

    def scores(j):
        return _dot_nt(q, k_ref[pl.ds(pl.multiple_of(j * tk, tk), tk), :])

    def values(j):
        return v_ref[pl.ds(pl.multiple_of(j * tk, tk), tk), :]

    def absorb(m, s, v):
        m_new = jnp.maximum(m, jnp.max(s, axis=-1, keepdims=True))
        p = jnp.exp2(s - m_new)
        acc_ref[...] = jnp.exp2(m - m_new) * acc_ref[...] + _dot(p.astype(BF16), v)
        return m_new

    s_ref[0] = scores(0)
    sm = _dot_nt(q, km_ref[...])
    sm = jnp.where(lax.broadcasted_iota(jnp.int32, sm.shape, 1) < N_META, sm, -jnp.inf)
    m = jnp.max(sm, axis=-1, keepdims=True)
    acc_ref[...] = _dot(jnp.exp2(sm - m).astype(BF16), vm_ref[...])

    def group(g, m, last):
        for i in range(ATTN_GROUP):
            j = ATTN_GROUP * g + i
            s = s_ref[i % 2]
            if not (last and i == ATTN_GROUP - 1):
                s_ref[(i + 1) % 2] = scores(j + 1)
            m = absorb(m, s, values(j))
        return m

    m = lax.fori_loop(0, n_groups - 1, lambda g, m: group(g, m, False), m)
    group(n_groups - 1, m, True)
    acc = acc_ref[...]
    o_ref[...] = (acc[:, :MLA_V] / acc[:, MLA_V:]).astype(o_ref.dtype)


def _attention(q, k, v, km, vm, bsz, seq, tq, tk):
    nq = seq // tq
    return pl.pallas_call(
        functools.partial(_attn_body, tk=tk),
        grid=(bsz, MLA_HEADS, nq),
        in_specs=[pl.BlockSpec((None, tq, QK_WIDTH), lambda b, h, i: (h, b * nq + i, 0)),
                  pl.BlockSpec((None, seq, QK_WIDTH), lambda b, h, i: (h, b, 0)),
                  pl.BlockSpec((None, seq, V_WIDTH), lambda b, h, i: (h, b, 0)),
                  pl.BlockSpec((None, LANES, QK_WIDTH), lambda b, h, i: (h, 0, 0)),
                  pl.BlockSpec((None, LANES, V_WIDTH), lambda b, h, i: (h, 0, 0))],
        out_specs=pl.BlockSpec((tq, MLA_V), lambda b, h, i: (b * nq + i, h)),
        out_shape=jax.ShapeDtypeStruct((bsz * seq, MLA_HEADS * MLA_V), BF16),
        scratch_shapes=[pltpu.VMEM((2, tq, tk), F32), pltpu.VMEM((tq, V_WIDTH), F32)],
        compiler_params=_cparams("parallel", "parallel", "arbitrary"),
        name="mla_attention",
    )(q, k, v, km, vm)


def _split3(x):
    hi = x.astype(BF16)
    r1 = x - hi.astype(F32)
    mid = r1.astype(BF16)
    lo = (r1 - mid.astype(F32)).astype(BF16)
    return hi, mid, lo


def _gla_chunk(q, k, g, v_even, v_odd, state, tri, keep, mid, last):
    lane = lax.broadcasted_iota(jnp.int32, (GLA_CHUNK, LANES), 1)
    even = lane < GLA_DK
    g_hi, g_mid, g_lo = _split3(g)
    b = _dot(tri, g_hi) + _dot(tri, g_mid) + _dot(tri, g_lo)
    b_last = b[last:last + 1, :]
    ke = (k * jnp.exp(b_last - b)).astype(BF16)
    upd = jnp.where(lax.broadcasted_iota(jnp.int32, (GLA_DV, LANES), 1) < GLA_DK,
                    _dot_tn(v_even, ke), _dot_tn(v_odd, ke))
    new_state = state * jnp.exp(b_last) + upd
    if q is None:
        return new_state, None, None
    b_mid = b[mid:mid + 1, :]
    qs = q * jnp.exp(b - b_mid)
    ks = (k * jnp.exp(b_mid - b)).astype(BF16)
    qs2 = jnp.concatenate([jnp.where(even, qs, 0.0), jnp.where(even, 0.0, qs)], axis=0).astype(BF16)
    scores = jnp.where(keep, _dot_nt(qs2, ks), 0.0).astype(BF16)
    qe = q * jnp.exp(b)
    qe2 = jnp.concatenate([jnp.where(even, qe, 0.0), jnp.where(even, 0.0, qe)], axis=0).astype(BF16)
    inter = _dot_nt(qe2, state.astype(BF16))
    o_even = _dot(scores[:GLA_CHUNK], v_even) + inter[:GLA_CHUNK]
    o_odd = _dot(scores[GLA_CHUNK:], v_odd) + inter[GLA_CHUNK:]
    return new_state, o_even, o_odd


def _gla_body(qf_ref, kf_ref, vf_ref, gf_ref, qb_ref, kb_ref, vb_ref, gb_ref, mk_ref, mv_ref, mg_ref,
              of_ref, ob_ref, state_ref):
    n_chunks = qf_ref.shape[0] // GLA_CHUNK
    n_pairs = GLA_HEADS // 2
    r = lax.broadcasted_iota(jnp.int32, (GLA_CHUNK, GLA_CHUNK), 0)
    c = lax.broadcasted_iota(jnp.int32, (GLA_CHUNK, GLA_CHUNK), 1)
    tri_f = jnp.where(c <= r, 1.0, 0.0).astype(BF16)
    tri_b = jnp.where(c >= r, 1.0, 0.0).astype(BF16)
    r2 = lax.broadcasted_iota(jnp.int32, (2 * GLA_CHUNK, GLA_CHUNK), 0) & (GLA_CHUNK - 1)
    c2 = lax.broadcasted_iota(jnp.int32, (2 * GLA_CHUNK, GLA_CHUNK), 1)
    keep_f = c2 <= r2
    keep_b = c2 >= r2
    mid_f, last_f = GLA_CHUNK // 2 - 1, GLA_CHUNK - 1
    mid_b, last_b = GLA_CHUNK // 2, 0

    def pair_cols(p):
        return slice(p * LANES, (p + 1) * LANES)

    def head_cols(h):
        return slice(h * GLA_DV, (h + 1) * GLA_DV)

    @pl.when(pl.program_id(1) == 0)
    def _():
        for p in range(n_pairs):
            zero = jnp.zeros((GLA_DV, LANES), F32)
            st, _, _ = _gla_chunk(None, mk_ref[:, pair_cols(p)], mg_ref[:, pair_cols(p)],
                                  mv_ref[:, head_cols(2 * p)].astype(BF16),
                                  mv_ref[:, head_cols(2 * p + 1)].astype(BF16),
                                  zero, tri_f, keep_f, mid_f, last_f)
            state_ref[p] = st
            state_ref[n_pairs + p] = zero

    def step(i, _):
        rows_f = pl.ds(pl.multiple_of(i * GLA_CHUNK, GLA_CHUNK), GLA_CHUNK)
        rows_b = pl.ds(pl.multiple_of((n_chunks - 1 - i) * GLA_CHUNK, GLA_CHUNK), GLA_CHUNK)
        for p in range(n_pairs):
            for (rows, q_ref, k_ref, v_ref, g_ref, o_ref, slot, tri, keep, mid, last) in (
                    (rows_f, qf_ref, kf_ref, vf_ref, gf_ref, of_ref, p, tri_f, keep_f, mid_f, last_f),
                    (rows_b, qb_ref, kb_ref, vb_ref, gb_ref, ob_ref, n_pairs + p, tri_b, keep_b, mid_b, last_b)):
                st, o_even, o_odd = _gla_chunk(
                    q_ref[rows, pair_cols(p)], k_ref[rows, pair_cols(p)], g_ref[rows, pair_cols(p)],
                    v_ref[rows, head_cols(2 * p)].astype(BF16), v_ref[rows, head_cols(2 * p + 1)].astype(BF16),
                    state_ref[slot], tri, keep, mid, last)
                state_ref[slot] = st
                o_ref[rows, head_cols(2 * p)] = o_even
                o_ref[rows, head_cols(2 * p + 1)] = o_odd
        return 0

    lax.fori_loop(0, n_chunks, step, 0)


def _gla(gq, gk, gv, gf, gb, mk, mv, mg, bsz, seq, tb):
    nb = seq // tb
    hk = GLA_HEADS * GLA_DK
    hv = GLA_HEADS * GLA_DV
    fwd = lambda width: pl.BlockSpec((tb, width), lambda b, j: (b * nb + j, 0))
    bwd = lambda width: pl.BlockSpec((tb, width), lambda b, j: (b * nb + nb - 1 - j, 0))
    t = bsz * seq
    return pl.pallas_call(
        _gla_body,
        grid=(bsz, nb),
        in_specs=[fwd(hk), fwd(hk), fwd(hv), fwd(hk), bwd(hk), bwd(hk), bwd(hv), bwd(hk),
                  _full_spec(mk.shape), _full_spec(mv.shape), _full_spec(mg.shape)],
        out_specs=(fwd(hv), bwd(hv)),
        out_shape=(jax.ShapeDtypeStruct((t, hv), F32), jax.ShapeDtypeStruct((t, hv), F32)),
        scratch_shapes=[pltpu.VMEM((2 * (GLA_HEADS // 2), GLA_DV, LANES), F32)],
        compiler_params=_cparams("parallel", "arbitrary"),
        name="gla_scan",
    )(gq, gk, gv, gf, gq, gk, gv, gb, mk, mv, mg)


def _mix_body(x_ref, a_ref, of_ref, ob_ref, og_ref, gnorm_ref, wout_ref, nffn_ref, wr_hi_ref, wr_lo_ref, br_ref,
              h1_ref, hn_ref, route_ref, ids_ref, count_ref, tri_ref):
    tm = x_ref.shape[0]

    @pl.when(pl.program_id(0) == 0)
    def _():
        r = lax.broadcasted_iota(jnp.int32, (tm, tm), 0)
        c = lax.broadcasted_iota(jnp.int32, (tm, tm), 1)
        tri_ref[...] = jnp.where(c < r, 1.0, 0.0).astype(BF16)
        count_ref[...] = jnp.zeros_like(count_ref)

    a_width = MLA_HEADS * MLA_V
    h1 = x_ref[...] + _dot(a_ref[...], wout_ref[0:a_width, :])
    for h in range(GLA_HEADS):
        cols = slice(h * GLA_DV, (h + 1) * GLA_DV)
        o = of_ref[:, cols] + ob_ref[:, cols]
        og = og_ref[:, cols]
        silu = og / (1.0 + jnp.exp(-og))
        gh = (_rms(o, gnorm_ref[...]) * silu).astype(BF16)
        h1 = h1 + _dot(gh, wout_ref[a_width + h * GLA_DV:a_width + (h + 1) * GLA_DV, :])
    h1_ref[...] = h1
    hn = _rms(h1, nffn_ref[...])
    hn_ref[...] = _rows_to_tiles(hn)

    hn_hi = hn.astype(BF16)
    hn_lo = (hn - hn_hi.astype(F32)).astype(BF16)
    logits = (_dot(hn_hi, wr_hi_ref[...]) + _dot(hn_lo, wr_hi_ref[...]) + _dot(hn_hi, wr_lo_ref[...])
              + br_ref[...])
    lane = lax.broadcasted_iota(jnp.int32, logits.shape, 1).astype(F32)
    none = float(LANES)
    neg = -jnp.inf

    def lane_max(x):
        return jnp.max(x, axis=-1, keepdims=True)

    def lane_sum(x):
        return jnp.sum(x, axis=-1, keepdims=True)

    def first_lane(mask):
        return jnp.min(jnp.where(mask, lane, none), axis=-1, keepdims=True)

    is_group = lane < float(N_GROUPS)
    g_max = lane_max(jnp.where(is_group, logits, neg))
    g_exp = jnp.where(is_group, jnp.exp(logits - g_max), 0.0)
    g_prob = g_exp / lane_sum(g_exp)
    g_w = lane_max(g_prob)
    g_idx = first_lane(is_group & (g_prob == g_w))
    e_lo = float(ROUTER_EXPERT_LANE0) + float(EXPERTS_PER_GROUP) * g_idx
    sel = (lane >= e_lo) & (lane < e_lo + float(EXPERTS_PER_GROUP))
    e_max = lane_max(jnp.where(sel, logits, neg))
    e_exp = jnp.where(sel, jnp.exp(logits - e_max), 0.0)
    e_prob = e_exp / lane_sum(e_exp)
    p1 = lane_max(jnp.where(sel, e_prob, neg))
    i1 = first_lane(sel & (e_prob == p1))
    rest = sel & (lane != i1)
    p2 = lane_max(jnp.where(rest, e_prob, neg))
    i2 = first_lane(rest & (e_prob == p2))
    denom = p1 + p2
    chosen = jnp.where((lane == i1) | (lane == i2), 1.0, 0.0)
    rank = count_ref[...] + _dot(tri_ref[...], chosen.astype(BF16))
    count_ref[...] += jnp.sum(chosen, axis=0, keepdims=True)
    fields = (i1 - float(ROUTER_EXPERT_LANE0), i2 - float(ROUTER_EXPERT_LANE0),
              lane_sum(jnp.where(lane == i1, rank, 0.0)), lane_sum(jnp.where(lane == i2, rank, 0.0)),
              g_w * (p1 / denom), g_w * (p2 / denom))
    route = jnp.zeros_like(logits)
    for k, value in enumerate(fields):
        route = jnp.where(lane == float(k), value, route)
    route_ref[...] = route
    ids_ref[...] = jnp.transpose(route)[0:ROUTE_ID_ROWS, :].astype(jnp.int32)


ROUTE_E1, ROUTE_E2, ROUTE_RANK1, ROUTE_RANK2, ROUTE_W1, ROUTE_W2 = range(6)
ROUTE_ID_ROWS = 8


def _mix(x2d, a, o_f, o_b, og, w, tm):
    t = x2d.shape[0]
    hv = GLA_HEADS * GLA_DV
    row = lambda width: pl.BlockSpec((tm, width), lambda i: (i, 0))
    return pl.pallas_call(
        _mix_body,
        grid=(t // tm,),
        in_specs=[row(D_MODEL), row(MLA_HEADS * MLA_V), row(hv), row(hv), row(hv),
                  _full_spec((1, GLA_DV)), _full_spec(w['wout'].shape), _full_spec((1, D_MODEL)),
                  _full_spec(w['wr_hi'].shape), _full_spec(w['wr_lo'].shape), _full_spec(w['br'].shape)],
        out_specs=(row(D_MODEL), pl.BlockSpec((tm, ROW_CHUNKS, LANES), lambda i: (i, 0, 0)), row(LANES),
                   pl.BlockSpec((ROUTE_ID_ROWS, tm), lambda i: (0, i)), _full_spec((1, LANES))),
        out_shape=(jax.ShapeDtypeStruct((t, D_MODEL), F32), jax.ShapeDtypeStruct((t, ROW_CHUNKS, LANES), F32),
                   jax.ShapeDtypeStruct((t, LANES), F32), jax.ShapeDtypeStruct((ROUTE_ID_ROWS, t), jnp.int32),
                   jax.ShapeDtypeStruct((1, LANES), F32)),
        scratch_shapes=[pltpu.VMEM((tm, tm), BF16)],
        compiler_params=_cparams("arbitrary"),
        name="mix_router",
    )(x2d, a, o_f, o_b, og, w['gla_norm'], w['wout'], w['norm_ffn'], w['wr_hi'], w['wr_lo'], w['br'])


EXPERT_TILE = 256
ROW_TILE = 256


SC_CORES = 2
SC_SUBCORES = 16
SC_GATHER_ROWS = 32


def _sc_gather(table, idx):
    n = idx.shape[0]
    workers = SC_CORES * SC_SUBCORES
    per_worker = n // workers
    assert n % (workers * SC_GATHER_ROWS) == 0
    mesh = plsc.VectorSubcoreMesh(core_axis_name="c", subcore_axis_name="s")

    @functools.partial(
        pl.kernel, mesh=mesh,
        out_type=jax.ShapeDtypeStruct((n,) + table.shape[1:], table.dtype),
        scratch_types=[pltpu.VMEM((SC_GATHER_ROWS,), jnp.int32),
                       pltpu.VMEM((SC_GATHER_ROWS,) + table.shape[1:], table.dtype),
                       pltpu.SemaphoreType.DMA])
    def gather(table_ref, idx_ref, out_ref, idx_buf, rows_buf, sem):
        base = (lax.axis_index("s") * SC_CORES + lax.axis_index("c")) * per_worker

        @pl.loop(0, per_worker // SC_GATHER_ROWS)
        def _(j):
            rows = pl.ds(base + j * SC_GATHER_ROWS, SC_GATHER_ROWS)
            pltpu.sync_copy(idx_ref.at[rows], idx_buf)
            pltpu.async_copy(table_ref.at[idx_buf], rows_buf, sem).wait()
            pltpu.sync_copy(rows_buf, out_ref.at[rows])

    return gather(table, idx)


SLOT_TILE = 2048


def _slots_body(starts_ref, ids_ref, slots_ref):
    ids = ids_ref[...]
    experts = ids[ROUTE_E1:ROUTE_E1 + 2, :]
    start = jnp.zeros_like(experts)
    for e in range(N_EXPERTS):
        start = jnp.where(experts == e, starts_ref[e], start)
    slots_ref[...] = start + ids[ROUTE_RANK1:ROUTE_RANK1 + 2, :]


def _slots(starts, ids):
    t = ids.shape[1]
    return pl.pallas_call(
        _slots_body,
        grid_spec=pltpu.PrefetchScalarGridSpec(
            num_scalar_prefetch=1,
            grid=(t // SLOT_TILE,),
            in_specs=[pl.BlockSpec((ROUTE_ID_ROWS, SLOT_TILE), lambda i, *_: (0, i))],
            out_specs=pl.BlockSpec((2, SLOT_TILE), lambda i, *_: (0, i))),
        out_shape=jax.ShapeDtypeStruct((2, t), jnp.int32),
        compiler_params=_cparams("parallel"),
        name="moe_slots",
    )(starts, ids)


def _queue_body(slots_ref, token_ref, *, n_tokens):
    i = pl.program_id(0)

    @pl.when(i == 0)
    def _():
        for base in range(0, token_ref.shape[0], n_tokens):
            def clear(j, carry):
                token_ref[base + j] = j
                return carry
            lax.fori_loop(0, min(n_tokens, token_ref.shape[0] - base), clear, 0, unroll=8)

    def place(r, carry):
        for k in range(2):
            token_ref[slots_ref[k, r]] = i * SLOT_TILE + r
        return carry

    lax.fori_loop(0, SLOT_TILE, place, 0, unroll=8)


def _queue_tokens(slots, n_rows):
    t = slots.shape[1]
    return pl.pallas_call(
        functools.partial(_queue_body, n_tokens=t),
        grid=(t // SLOT_TILE,),
        in_specs=[pl.BlockSpec((2, SLOT_TILE), lambda i: (0, i), memory_space=pltpu.SMEM)],
        out_specs=pl.BlockSpec(memory_space=pltpu.SMEM),
        out_shape=jax.ShapeDtypeStruct((n_rows,), jnp.int32),
        compiler_params=_cparams("arbitrary"),
        name="moe_queue",
    )(slots)


def _expert_body(tile_expert_ref, n_tiles_ref, xs_ref, wgu_ref, wd_ref, ys_ref):
    used = pl.program_id(0) < n_tiles_ref[0]

    @pl.when(used)
    def _():
        h = _dot(_tiles_to_rows(xs_ref[...]).astype(BF16), wgu_ref[...])
        hg = h[:, :D_EXPERT]
        act = (hg / (1.0 + jnp.exp(-hg))) * h[:, D_EXPERT:]
        ys_ref[...] = _rows_to_tiles(_dot(act.astype(BF16), wd_ref[...]))

    @pl.when(jnp.logical_not(used))
    def _():
        ys_ref[...] = jnp.zeros_like(ys_ref)


def _experts(tile_expert, n_tiles, xs, w):
    n_grid = xs.shape[0] // EXPERT_TILE

    def tile(i, tile_expert, n_tiles):
        return jnp.minimum(i, n_tiles[0] - 1)

    tiles = lambda index: pl.BlockSpec((EXPERT_TILE, ROW_CHUNKS, LANES), index)
    return pl.pallas_call(
        _expert_body,
        grid_spec=pltpu.PrefetchScalarGridSpec(
            num_scalar_prefetch=2,
            grid=(n_grid,),
            in_specs=[tiles(lambda i, te, nt: (tile(i, te, nt), 0, 0)),
                      pl.BlockSpec((None, D_MODEL, 2 * D_EXPERT), lambda i, te, nt: (te[tile(i, te, nt)], 0, 0)),
                      pl.BlockSpec((None, D_EXPERT, D_MODEL), lambda i, te, nt: (te[tile(i, te, nt)], 0, 0))],
            out_specs=tiles(lambda i, te, nt: (i, 0, 0))),
        out_shape=jax.ShapeDtypeStruct(xs.shape, F32),
        compiler_params=_cparams("arbitrary"),
        name="moe_experts",
    )(tile_expert, n_tiles, xs, w['wgu'], w['wd'])


def _combine_body(h1_ref, route_ref, nfin_ref, y1_ref, y2_ref, out_ref):
    route = route_ref[...]
    lane = lax.broadcasted_iota(jnp.int32, route.shape, 1)
    w1 = jnp.sum(jnp.where(lane == ROUTE_W1, route, 0.0), axis=-1, keepdims=True)
    w2 = jnp.sum(jnp.where(lane == ROUTE_W2, route, 0.0), axis=-1, keepdims=True)
    y = w1 * _tiles_to_rows(y1_ref[...]) + w2 * _tiles_to_rows(y2_ref[...])
    out_ref[...] = _rms(h1_ref[...] + y, nfin_ref[...])


def _combine(h1, route, y12, w):
    t = h1.shape[0]
    row = lambda width: pl.BlockSpec((ROW_TILE, width), lambda i: (i, 0))
    tiles = lambda k: pl.BlockSpec((None, ROW_TILE, ROW_CHUNKS, LANES), lambda i: (k, i, 0, 0))
    return pl.pallas_call(
        _combine_body,
        grid=(t // ROW_TILE,),
        in_specs=[row(D_MODEL), row(LANES), _full_spec((1, D_MODEL)), tiles(0), tiles(1)],
        out_specs=row(D_MODEL),
        out_shape=jax.ShapeDtypeStruct((t, D_MODEL), F32),
        compiler_params=_cparams("parallel"),
        name="moe_combine",
    )(h1, route, w['norm_final'], y12, y12)


def _moe(hn, route, ids, counts, h1, w):
    t = hn.shape[0]
    n_rows = 2 * t + N_EXPERTS * EXPERT_TILE
    n_grid_tiles = n_rows // EXPERT_TILE
    count = counts[0, ROUTER_EXPERT_LANE0:ROUTER_EXPERT_LANE0 + N_EXPERTS].astype(jnp.int32)
    padded = jnp.maximum((count + EXPERT_TILE - 1) // EXPERT_TILE, 1) * EXPERT_TILE
    ends = jnp.sum(jnp.where(jnp.arange(N_EXPERTS)[:, None] <= jnp.arange(N_EXPERTS)[None, :], padded[:, None], 0),
                   axis=0)
    starts = ends - padded
    n_tiles = (ends[-1:] // EXPERT_TILE)
    tile_rows = jnp.arange(n_grid_tiles, dtype=jnp.int32) * EXPERT_TILE
    tile_expert = jnp.minimum(jnp.sum((ends[None, :] <= tile_rows[:, None]).astype(jnp.int32), axis=1), N_EXPERTS - 1)
    slots = _slots(starts, ids)
    tokens = _queue_tokens(slots, n_rows)
    ys = _experts(tile_expert, n_tiles, _sc_gather(hn, tokens), w)
    y12 = _sc_gather(ys, slots.reshape(2 * t)).reshape(2, t, ROW_CHUNKS, LANES)
    return _combine(h1, route, y12, w)


def _rope_tables(positions):
    inv_freq = 1.0 / (ROPE_THETA ** (jnp.arange(0, MLA_ROPE, 2, dtype=F32) / MLA_ROPE))
    ang = positions.astype(F32)[:, None] * inv_freq[None, :]
    cos, sin = jnp.cos(ang), jnp.sin(ang)
    reps = LANES // MLA_ROPE
    return jnp.tile(jnp.concatenate([cos, cos], axis=-1), (1, reps)), jnp.tile(jnp.concatenate([-sin, sin], axis=-1), (1, reps))


def _pack_weights(norm_mix, w_in, q_a_norm, w_uq, kv_a_norm, w_ukv, w_gate_fwd, b_gate_fwd, w_gate_bwd, b_gate_bwd,
                  gla_norm, w_out, norm_ffn, w_router_group, b_router_group, w_router_expert, b_router_expert,
                  w_expert_gate, w_expert_up, w_expert_down, norm_final):
    l = 0
    hk = GLA_HEADS * GLA_DK
    hv = GLA_HEADS * GLA_DV
    c_q, c_kv, k_pe, gq, gk, gv, lr_f, lr_b, og = jnp.split(
        w_in[l], np.cumsum([MLA_Q_RANK, MLA_KV_RANK, MLA_ROPE, hk, hk, hv, GLA_GATE_RANK, GLA_GATE_RANK])[:].tolist(),
        axis=-1)
    lr_pad = jnp.zeros((D_MODEL, LANES - 2 * GLA_GATE_RANK), F32)
    win = jnp.concatenate([c_q, c_kv, k_pe, k_pe, gq, gk, gv, og, lr_f, lr_b, lr_pad], axis=-1).astype(BF16)
    wuq = w_uq[l].reshape(MLA_Q_RANK, MLA_HEADS, MLA_NOPE + MLA_ROPE)
    wuq = jnp.concatenate([wuq[:, :, :MLA_NOPE].reshape(MLA_Q_RANK, -1), wuq[:, :, MLA_NOPE:].reshape(MLA_Q_RANK, -1)],
                          axis=-1).astype(BF16)
    wgate = jnp.zeros((LANES, 2 * hk), F32)
    wgate = wgate.at[0:GLA_GATE_RANK, 0:hk].set(w_gate_fwd[l])
    wgate = wgate.at[GLA_GATE_RANK:2 * GLA_GATE_RANK, hk:].set(w_gate_bwd[l])
    wr = jnp.zeros((D_MODEL, LANES), F32)
    wr = wr.at[:, ROUTER_GROUP_LANE0:ROUTER_GROUP_LANE0 + N_GROUPS].set(w_router_group[l])
    wr = wr.at[:, ROUTER_EXPERT_LANE0:ROUTER_EXPERT_LANE0 + N_EXPERTS].set(w_router_expert[l])
    wr_hi = wr.astype(BF16)
    br = jnp.zeros((1, LANES), F32)
    br = br.at[0, ROUTER_GROUP_LANE0:ROUTER_GROUP_LANE0 + N_GROUPS].set(b_router_group[l])
    br = br.at[0, ROUTER_EXPERT_LANE0:ROUTER_EXPERT_LANE0 + N_EXPERTS].set(b_router_expert[l])
    wgu = jnp.concatenate([w_expert_gate[l], w_expert_up[l]], axis=-1).reshape(N_EXPERTS, D_MODEL, 2 * D_EXPERT)
    return {
        'norm_mix': norm_mix[l][None], 'win': win, 'q_a_norm': q_a_norm[l][None], 'wuq': wuq,
        'kv_a_norm': kv_a_norm[l][None], 'wukv': w_ukv[l].astype(BF16),
        'wgate': wgate.astype(BF16), 'bgate': jnp.concatenate([b_gate_fwd[l], b_gate_bwd[l]])[None],
        'gla_norm': gla_norm[l][None], 'wout': w_out[l].astype(BF16), 'norm_ffn': norm_ffn[l][None],
        'wr_hi': wr_hi, 'wr_lo': (wr - wr_hi.astype(F32)).astype(BF16), 'br': br,
        'wgu': wgu.astype(BF16), 'wd': w_expert_down[l].reshape(N_EXPERTS, D_EXPERT, D_MODEL).astype(BF16),
        'norm_final': norm_final[None],
    }


def _meta_streams(meta_tokens, w):
    cos, sin = _rope_tables(jnp.arange(N_META))
    _, k, v, _, gk, gv, gf, _, _ = _inproj(meta_tokens, cos, sin, w, N_META)
    pad_keys = ((0, 0), (0, LANES - N_META), (0, 0))
    front = ((GLA_CHUNK - N_META, 0), (0, 0))
    return (jnp.pad(k, pad_keys), jnp.pad(v, pad_keys), jnp.pad(gk, front), jnp.pad(gv, front), jnp.pad(gf, front))


def _token_mixers(x, meta, w, tm, tq, tk, tb):
    bsz, seq, _ = x.shape
    km, vm, mk, mv, mg = meta
    x2d = x.reshape(bsz * seq, D_MODEL)
    cos, sin = _rope_tables(N_META + jnp.arange(seq))
    q, k, v, gq, gk, gv, gf, gb, og = _inproj(x2d, cos, sin, w, tm)
    a = _attention(q, k, v, km, vm, bsz, seq, tq, tk)
    o_f, o_b = _gla(gq, gk, gv, gf, gb, mk, mv, mg, bsz, seq, tb)
    return _mix(x2d, a, o_f, o_b, og, w, tm)


def kernel(x_prompt, x_sample, meta_tokens, norm_mix, w_in, q_a_norm, w_uq, kv_a_norm, w_ukv, w_gate_fwd, b_gate_fwd, w_gate_bwd, b_gate_bwd, gla_norm, w_out, norm_ffn, w_router_group, b_router_group, w_router_expert, b_router_expert, w_expert_gate, w_expert_up, w_expert_down, norm_final):
    w = _pack_weights(norm_mix, w_in, q_a_norm, w_uq, kv_a_norm, w_ukv, w_gate_fwd, b_gate_fwd, w_gate_bwd,
                      b_gate_bwd, gla_norm, w_out, norm_ffn, w_router_group, b_router_group, w_router_expert,
                      b_router_expert, w_expert_gate, w_expert_up, w_expert_down, norm_final)
    meta = _meta_streams(meta_tokens, w)
    outs = []
    for x in (x_prompt, x_sample):
        h1, hn, route, ids, counts = _token_mixers(x, meta, w, tm=512, tq=512, tk=512, tb=512)
        outs.append(_moe(hn, route, ids, counts, h1, w).reshape(x.shape))
    return tuple(outs)
```

```python
import functools

import numpy as np
import jax
import jax.numpy as jnp
from jax import lax
from jax.experimental import pallas as pl
from jax.experimental.pallas import tpu as pltpu
from jax.experimental.pallas import tpu_sc as plsc

F32 = jnp.float32
BF16 = jnp.bfloat16

D_MODEL = 1024
N_META = 16
MLA_HEADS = 4
MLA_Q_RANK = 384
MLA_KV_RANK = 256
MLA_NOPE = 128
MLA_ROPE = 64
MLA_V = 128
ROPE_THETA = 10000.0
GLA_HEADS = 4
GLA_DK = 64
GLA_DV = 128
GLA_GATE_RANK = 16
GLA_TAU = 16.0
GLA_CHUNK = 64
N_GROUPS = 4
EXPERTS_PER_GROUP = 8
N_EXPERTS = N_GROUPS * EXPERTS_PER_GROUP
D_EXPERT = 256
EPS = 1e-6

LANES = 128
V7X_VMEM_BYTES = 64 * 1024 * 1024
VMEM_LIMIT = V7X_VMEM_BYTES * 7 // 8

ATTN_SCALE = (MLA_NOPE + MLA_ROPE) ** -0.5 * float(np.log2(np.e))
QK_WIDTH = 2 * LANES
V_WIDTH = 2 * LANES
ATTN_GROUP = 8

C_CQ = 0
C_CKV = C_CQ + MLA_Q_RANK
C_KPE = C_CKV + MLA_KV_RANK
C_GQ = C_KPE + LANES
C_GK = C_GQ + GLA_HEADS * GLA_DK
C_GV = C_GK + GLA_HEADS * GLA_DK
C_OG = C_GV + GLA_HEADS * GLA_DV
C_LR = C_OG + GLA_HEADS * GLA_DV
D_IN_PACKED = C_LR + LANES

ROUTER_GROUP_LANE0 = 0
ROUTER_EXPERT_LANE0 = N_GROUPS


def _cparams(*semantics):
    return pltpu.CompilerParams(dimension_semantics=semantics, vmem_limit_bytes=VMEM_LIMIT)


def _rms(x, g):
    return x * lax.rsqrt(jnp.mean(x * x, axis=-1, keepdims=True) + EPS) * g


def _dot(a, b):
    return jnp.dot(a, b, preferred_element_type=F32)


def _dot_nt(a, b):
    return lax.dot_general(a, b, (((1,), (1,)), ((), ())), preferred_element_type=F32)


def _dot_tn(a, b):
    return lax.dot_general(a, b, (((0,), (0,)), ((), ())), preferred_element_type=F32)


def _full_spec(shape):
    return pl.BlockSpec(shape, lambda *_: (0,) * len(shape))


SUBLANES = 8
ROW_CHUNKS = D_MODEL // LANES
assert ROW_CHUNKS == SUBLANES


def _rows_to_tiles(x):
    chunks = jnp.stack([x[:, s * LANES:(s + 1) * LANES] for s in range(ROW_CHUNKS)], axis=0)
    return pltpu.einshape("smd->msd", chunks)


def _tiles_to_rows(x):
    chunks = pltpu.einshape("msd->smd", x)
    return jnp.concatenate([chunks[s] for s in range(ROW_CHUNKS)], axis=-1)


def _rope_pairs(x, cos, sin_signed, first_half):
    swapped = jnp.where(first_half, pltpu.roll(x, LANES - MLA_ROPE // 2, 1), pltpu.roll(x, MLA_ROPE // 2, 1))
    return x * cos + swapped * sin_signed


def _inproj_body(x_ref, cos_ref, sin_ref, nmix_ref, win_ref, qan_ref, wuq_ref, kvan_ref, wukv_ref,
                 wgate_ref, bgate_ref,
                 q_ref, k_ref, v_ref, gq_ref, gk_ref, gv_ref, gf_ref, gb_ref, og_ref):
    hn = _rms(x_ref[...], nmix_ref[...]).astype(BF16)

    def proj(lo, hi):
        return _dot(hn, win_ref[:, lo:hi])

    cos = cos_ref[...]
    sin = sin_ref[...]
    lane = lax.broadcasted_iota(jnp.int32, cos.shape, 1)
    first_half = (lane & (MLA_ROPE - 1)) < MLA_ROPE // 2
    low_lanes = lane < MLA_ROPE

    cq = _rms(proj(C_CQ, C_CKV), qan_ref[...]).astype(BF16)
    qn = _dot(cq, wuq_ref[:, 0:MLA_HEADS * MLA_NOPE]) * ATTN_SCALE
    qr = _dot(cq, wuq_ref[:, MLA_HEADS * MLA_NOPE:])
    for j in range(MLA_HEADS // 2):
        rj = (_rope_pairs(qr[:, j * LANES:(j + 1) * LANES], cos, sin, first_half) * ATTN_SCALE).astype(BF16)
        for h in (2 * j, 2 * j + 1):
            q_ref[h, :, 0:LANES] = qn[:, h * LANES:(h + 1) * LANES].astype(BF16)
            q_ref[h, :, LANES:QK_WIDTH] = rj

    ckv = _rms(proj(C_CKV, C_KPE), kvan_ref[...]).astype(BF16)
    kv = _dot(ckv, wukv_ref[...])
    kr = _rope_pairs(proj(C_KPE, C_GQ), cos, sin, first_half)
    kr_even = jnp.where(low_lanes, kr, 0.0).astype(BF16)
    kr_odd = jnp.where(low_lanes, 0.0, kr).astype(BF16)
    for h in range(MLA_HEADS):
        base = h * (MLA_NOPE + MLA_V)
        k_ref[h, :, 0:LANES] = kv[:, base:base + MLA_NOPE].astype(BF16)
        k_ref[h, :, LANES:QK_WIDTH] = kr_even if h % 2 == 0 else kr_odd
        v_ref[h, :, 0:MLA_V] = kv[:, base + MLA_NOPE:base + MLA_NOPE + MLA_V].astype(BF16)
        v_ref[h, :, MLA_V:V_WIDTH] = jnp.ones((kv.shape[0], V_WIDTH - MLA_V), BF16)

    gq_ref[...] = proj(C_GQ, C_GK) * (GLA_DK ** -0.5)
    gk_ref[...] = proj(C_GK, C_GV)
    gv_ref[...] = proj(C_GV, C_OG)
    og_ref[...] = proj(C_OG, C_LR)
    pre = _dot(proj(C_LR, D_IN_PACKED).astype(BF16), wgate_ref[...]) + bgate_ref[...]
    logsig = jnp.minimum(pre, 0.0) - jnp.log1p(jnp.exp(-jnp.abs(pre)))
    gates = logsig * (1.0 / GLA_TAU)
    gf_ref[...] = gates[:, 0:GLA_HEADS * GLA_DK]
    gb_ref[...] = gates[:, GLA_HEADS * GLA_DK:]


def _inproj(x2d, cos, sin, w, tm):
    t = x2d.shape[0]
    blocks_per_seq = cos.shape[0] // tm
    hk = GLA_HEADS * GLA_DK
    hv = GLA_HEADS * GLA_DV
    row = lambda width: pl.BlockSpec((tm, width), lambda i: (i, 0))
    head_rows = lambda width: pl.BlockSpec((MLA_HEADS, tm, width), lambda i: (0, i, 0))
    tab = pl.BlockSpec((tm, LANES), lambda i: (i % blocks_per_seq, 0))
    out_shape = (
        jax.ShapeDtypeStruct((MLA_HEADS, t, QK_WIDTH), BF16),
        jax.ShapeDtypeStruct((MLA_HEADS, t, QK_WIDTH), BF16),
        jax.ShapeDtypeStruct((MLA_HEADS, t, V_WIDTH), BF16),
        jax.ShapeDtypeStruct((t, hk), F32),
        jax.ShapeDtypeStruct((t, hk), F32),
        jax.ShapeDtypeStruct((t, hv), F32),
        jax.ShapeDtypeStruct((t, hk), F32),
        jax.ShapeDtypeStruct((t, hk), F32),
        jax.ShapeDtypeStruct((t, hv), F32),
    )
    return pl.pallas_call(
        _inproj_body,
        grid=(t // tm,),
        in_specs=[row(D_MODEL), tab, tab,
                  _full_spec((1, D_MODEL)), _full_spec((D_MODEL, D_IN_PACKED)),
                  _full_spec((1, MLA_Q_RANK)), _full_spec(w['wuq'].shape),
                  _full_spec((1, MLA_KV_RANK)), _full_spec(w['wukv'].shape),
                  _full_spec(w['wgate'].shape), _full_spec(w['bgate'].shape)],
        out_specs=(head_rows(QK_WIDTH), head_rows(QK_WIDTH), head_rows(V_WIDTH),
                   row(hk), row(hk), row(hv), row(hk), row(hk), row(hv)),
        out_shape=out_shape,
        compiler_params=_cparams("parallel"),
        name="inproj",
    )(x2d, cos, sin, w['norm_mix'], w['win'], w['q_a_norm'], w['wuq'], w['kv_a_norm'], w['wukv'],
      w['wgate'], w['bgate'])


def _attn_body(q_ref, k_ref, v_ref, km_ref, vm_ref, o_ref, s_ref, acc_ref, *, tk):
    q = q_ref[...]
    n_groups = k_ref.shape[0] // (ATTN_GROUP * tk)

    def scores(j):
        return _dot_nt(q, k_ref[pl.ds(pl.multiple_of(j * tk, tk), tk), :])

    def values(j):
        return v_ref[pl.ds(pl.multiple_of(j * tk, tk), tk), :]

    def absorb(m, s, v):
        m_new = jnp.maximum(m, jnp.max(s, axis=-1, keepdims=True))
        p = jnp.exp2(s - m_new)
        acc_ref[...] = jnp.exp2(m - m_new) * acc_ref[...] + _dot(p.astype(BF16), v)
        return m_new

    s_ref[0] = scores(0)
    sm = _dot_nt(q, km_ref[...])
    sm = jnp.where(lax.broadcasted_iota(jnp.int32, sm.shape, 1) < N_META, sm, -jnp.inf)
    m = jnp.max(sm, axis=-1, keepdims=True)
    acc_ref[...] = _dot(jnp.exp2(sm - m).astype(BF16), vm_ref[...])

    def group(g, m, last):
        for i in range(ATTN_GROUP):
            j = ATTN_GROUP * g + i
            s = s_ref[i % 2]
            if not (last and i == ATTN_GROUP - 1):
                s_ref[(i + 1) % 2] = scores(j + 1)
            m = absorb(m, s, values(j))
        return m

    m = lax.fori_loop(0, n_groups - 1, lambda g, m: group(g, m, False), m)
    group(n_groups - 1, m, True)
    acc = acc_ref[...]
    o_ref[...] = (acc[:, :MLA_V] / acc[:, MLA_V:]).astype(o_ref.dtype)


def _attention(q, k, v, km, vm, bsz, seq, tq, tk):
    nq = seq // tq
    return pl.pallas_call(
        functools.partial(_attn_body, tk=tk),
        grid=(bsz, MLA_HEADS, nq),
        in_specs=[pl.BlockSpec((None, tq, QK_WIDTH), lambda b, h, i: (h, b * nq + i, 0)),
                  pl.BlockSpec((None, seq, QK_WIDTH), lambda b, h, i: (h, b, 0)),
                  pl.BlockSpec((None, seq, V_WIDTH), lambda b, h, i: (h, b, 0)),
                  pl.BlockSpec((None, LANES, QK_WIDTH), lambda b, h, i: (h, 0, 0)),
                  pl.BlockSpec((None, LANES, V_WIDTH), lambda b, h, i: (h, 0, 0))],
        out_specs=pl.BlockSpec((tq, MLA_V), lambda b, h, i: (b * nq + i, h)),
        out_shape=jax.ShapeDtypeStruct((bsz * seq, MLA_HEADS * MLA_V), BF16),
        scratch_shapes=[pltpu.VMEM((2, tq, tk), F32), pltpu.VMEM((tq, V_WIDTH), F32)],
        compiler_params=_cparams("parallel", "parallel", "arbitrary"),
        name="mla_attention",
    )(q, k, v, km, vm)


def _split3(x):
    hi = x.astype(BF16)
    r1 = x - hi.astype(F32)
    mid = r1.astype(BF16)
    lo = (r1 - mid.astype(F32)).astype(BF16)
    return hi, mid, lo


def _gla_log_decay(g, tri):
    g_hi, g_mid, g_lo = _split3(g)
    return _dot(tri, g_hi) + _dot(tri, g_mid) + _dot(tri, g_lo)


def _head_rows(x):
    even = lax.broadcasted_iota(jnp.int32, x.shape, 1) < GLA_DK
    return jnp.concatenate([jnp.where(even, x, 0.0), jnp.where(even, 0.0, x)], axis=0).astype(BF16)


def _gla_operands(q, k, b, mid, last):
    b_last = b[last:last + 1, :]
    ke = (k * jnp.exp(b_last - b)).astype(BF16)
    if q is None:
        return jnp.exp(b_last), ke, None, None, None
    b_mid = b[mid:mid + 1, :]
    ks = (k * jnp.exp(b_mid - b)).astype(BF16)
    return jnp.exp(b_last), ke, ks, _head_rows(q * jnp.exp(b - b_mid)), _head_rows(q * jnp.exp(b))


def _gla_state_update(v_even, v_odd, ke):
    return jnp.where(lax.broadcasted_iota(jnp.int32, (GLA_DV, LANES), 1) < GLA_DK,
                     _dot_tn(v_even, ke), _dot_tn(v_odd, ke))


def _gla_intra(v_even, v_odd, ks, qs2, keep):
    scores = jnp.where(keep, _dot_nt(qs2, ks), 0.0).astype(BF16)
    return _dot(scores[:GLA_CHUNK], v_even), _dot(scores[GLA_CHUNK:], v_odd)


def _gla_body(qf_ref, kf_ref, vf_ref, gf_ref, qb_ref, kb_ref, vb_ref, gb_ref, mk_ref, mv_ref, mg_ref,
              of_ref, ob_ref, state_ref):
    n_chunks = qf_ref.shape[0] // GLA_CHUNK
    n_pairs = GLA_HEADS // 2
    r = lax.broadcasted_iota(jnp.int32, (GLA_CHUNK, GLA_CHUNK), 0)
    c = lax.broadcasted_iota(jnp.int32, (GLA_CHUNK, GLA_CHUNK), 1)
    tri_f = jnp.where(c <= r, 1.0, 0.0).astype(BF16)
    tri_b = jnp.where(c >= r, 1.0, 0.0).astype(BF16)
    r2 = lax.broadcasted_iota(jnp.int32, (2 * GLA_CHUNK, GLA_CHUNK), 0) & (GLA_CHUNK - 1)
    c2 = lax.broadcasted_iota(jnp.int32, (2 * GLA_CHUNK, GLA_CHUNK), 1)
    keep_f = c2 <= r2
    keep_b = c2 >= r2
    mid_f, last_f = GLA_CHUNK // 2 - 1, GLA_CHUNK - 1
    mid_b, last_b = GLA_CHUNK // 2, 0

    def pair_cols(p):
        return slice(p * LANES, (p + 1) * LANES)

    def head_cols(h):
        return slice(h * GLA_DV, (h + 1) * GLA_DV)

    @pl.when(pl.program_id(1) == 0)
    def _():
        for p in range(n_pairs):
            b = _gla_log_decay(mg_ref[:, pair_cols(p)], tri_f)
            _, ke, _, _, _ = _gla_operands(None, mk_ref[:, pair_cols(p)], b, mid_f, last_f)
            state_ref[p] = _gla_state_update(mv_ref[:, head_cols(2 * p)].astype(BF16),
                                             mv_ref[:, head_cols(2 * p + 1)].astype(BF16), ke)
            state_ref[n_pairs + p] = jnp.zeros((GLA_DV, LANES), F32)

    scans = []
    for p in range(n_pairs):
        scans.append((p, list(range(n_chunks)), qf_ref, kf_ref, vf_ref, gf_ref, of_ref, p,
                      tri_f, keep_f, mid_f, last_f))
        scans.append((p, list(reversed(range(n_chunks))), qb_ref, kb_ref, vb_ref, gb_ref, ob_ref, n_pairs + p,
                      tri_b, keep_b, mid_b, last_b))

    def rows(c):
        return slice(c * GLA_CHUNK, (c + 1) * GLA_CHUNK)

    def values(v_ref, p, c):
        return (v_ref[rows(c), head_cols(2 * p)].astype(BF16), v_ref[rows(c), head_cols(2 * p + 1)].astype(BF16))

    log_decay = [[_gla_log_decay(g_ref[rows(c), pair_cols(p)], tri) for c in order]
                 for (p, order, _, _, _, g_ref, _, _, tri, _, _, _) in scans]
    operands = [[_gla_operands(q_ref[rows(c), pair_cols(p)], k_ref[rows(c), pair_cols(p)], b, mid, last)
                 for c, b in zip(order, bs)]
                for (p, order, q_ref, k_ref, _, _, _, _, _, _, mid, last), bs in zip(scans, log_decay)]
    updates = [[_gla_state_update(*values(v_ref, p, c), ops[1]) for c, ops in zip(order, opss)]
               for (p, order, _, _, v_ref, _, _, _, _, _, _, _), opss in zip(scans, operands)]
    intra = [[_gla_intra(*values(v_ref, p, c), ops[2], ops[3], keep) for c, ops in zip(order, opss)]
             for (p, order, _, _, v_ref, _, _, _, _, keep, _, _), opss in zip(scans, operands)]
    states = []
    for (_, order, _, _, _, _, _, slot, _, _, _, _), opss, upds in zip(scans, operands, updates):
        st = state_ref[slot]
        entering = []
        for ops, upd in zip(opss, upds):
            entering.append(st.astype(BF16))
            st = st * ops[0] + upd
        state_ref[slot] = st
        states.append(entering)
    for (p, order, _, _, _, _, o_ref, _, _, _, _, _), opss, sts, locs in zip(scans, operands, states, intra):
        for c, ops, st, (o_even, o_odd) in zip(order, opss, sts, locs):
            inter = _dot_nt(ops[4], st)
            o_ref[rows(c), head_cols(2 * p)] = o_even + inter[:GLA_CHUNK]
            o_ref[rows(c), head_cols(2 * p + 1)] = o_odd + inter[GLA_CHUNK:]


def _gla(gq, gk, gv, gf, gb, mk, mv, mg, bsz, seq, tb):
    nb = seq // tb
    hk = GLA_HEADS * GLA_DK
    hv = GLA_HEADS * GLA_DV
    fwd = lambda width: pl.BlockSpec((tb, width), lambda b, j: (b * nb + j, 0))
    bwd = lambda width: pl.BlockSpec((tb, width), lambda b, j: (b * nb + nb - 1 - j, 0))
    t = bsz * seq
    return pl.pallas_call(
        _gla_body,
        grid=(bsz, nb),
        in_specs=[fwd(hk), fwd(hk), fwd(hv), fwd(hk), bwd(hk), bwd(hk), bwd(hv), bwd(hk),
                  _full_spec(mk.shape), _full_spec(mv.shape), _full_spec(mg.shape)],
        out_specs=(fwd(hv), bwd(hv)),
        out_shape=(jax.ShapeDtypeStruct((t, hv), F32), jax.ShapeDtypeStruct((t, hv), F32)),
        scratch_shapes=[pltpu.VMEM((2 * (GLA_HEADS // 2), GLA_DV, LANES), F32)],
        compiler_params=_cparams("parallel", "arbitrary"),
        name="gla_scan",
    )(gq, gk, gv, gf, gq, gk, gv, gb, mk, mv, mg)


def _mix_body(x_ref, a_ref, of_ref, ob_ref, og_ref, gnorm_ref, wout_ref, nffn_ref, wr_hi_ref, wr_lo_ref, br_ref,
              h1_ref, hn_ref, route_ref, ids_ref, count_ref, tri_ref):
    tm = x_ref.shape[0]

    @pl.when(pl.program_id(0) == 0)
    def _():
        r = lax.broadcasted_iota(jnp.int32, (tm, tm), 0)
        c = lax.broadcasted_iota(jnp.int32, (tm, tm), 1)
        tri_ref[...] = jnp.where(c < r, 1.0, 0.0).astype(BF16)
        count_ref[...] = jnp.zeros_like(count_ref)

    a_width = MLA_HEADS * MLA_V
    h1 = x_ref[...] + _dot(a_ref[...], wout_ref[0:a_width, :])
    for h in range(GLA_HEADS):
        cols = slice(h * GLA_DV, (h + 1) * GLA_DV)
        o = of_ref[:, cols] + ob_ref[:, cols]
        og = og_ref[:, cols]
        silu = og / (1.0 + jnp.exp(-og))
        gh = (_rms(o, gnorm_ref[...]) * silu).astype(BF16)
        h1 = h1 + _dot(gh, wout_ref[a_width + h * GLA_DV:a_width + (h + 1) * GLA_DV, :])
    h1_ref[...] = h1
    hn = _rms(h1, nffn_ref[...])
    hn_ref[...] = _rows_to_tiles(hn)

    hn_hi = hn.astype(BF16)
    hn_lo = (hn - hn_hi.astype(F32)).astype(BF16)
    logits = (_dot(hn_hi, wr_hi_ref[...]) + _dot(hn_lo, wr_hi_ref[...]) + _dot(hn_hi, wr_lo_ref[...])
              + br_ref[...])
    lane = lax.broadcasted_iota(jnp.int32, logits.shape, 1).astype(F32)
    none = float(LANES)
    neg = -jnp.inf

    def lane_max(x):
        return jnp.max(x, axis=-1, keepdims=True)

    def lane_sum(x):
        return jnp.sum(x, axis=-1, keepdims=True)

    def first_lane(mask):
        return jnp.min(jnp.where(mask, lane, none), axis=-1, keepdims=True)

    is_group = lane < float(N_GROUPS)
    g_max = lane_max(jnp.where(is_group, logits, neg))
    g_exp = jnp.where(is_group, jnp.exp(logits - g_max), 0.0)
    g_prob = g_exp / lane_sum(g_exp)
    g_w = lane_max(g_prob)
    g_idx = first_lane(is_group & (g_prob == g_w))
    e_lo = float(ROUTER_EXPERT_LANE0) + float(EXPERTS_PER_GROUP) * g_idx
    sel = (lane >= e_lo) & (lane < e_lo + float(EXPERTS_PER_GROUP))
    e_max = lane_max(jnp.where(sel, logits, neg))
    e_exp = jnp.where(sel, jnp.exp(logits - e_max), 0.0)
    e_prob = e_exp / lane_sum(e_exp)
    p1 = lane_max(jnp.where(sel, e_prob, neg))
    i1 = first_lane(sel & (e_prob == p1))
    rest = sel & (lane != i1)
    p2 = lane_max(jnp.where(rest, e_prob, neg))
    i2 = first_lane(rest & (e_prob == p2))
    denom = p1 + p2
    chosen = jnp.where((lane == i1) | (lane == i2), 1.0, 0.0)
    rank = count_ref[...] + _dot(tri_ref[...], chosen.astype(BF16))
    count_ref[...] += jnp.sum(chosen, axis=0, keepdims=True)
    fields = (i1 - float(ROUTER_EXPERT_LANE0), i2 - float(ROUTER_EXPERT_LANE0),
              lane_sum(jnp.where(lane == i1, rank, 0.0)), lane_sum(jnp.where(lane == i2, rank, 0.0)),
              g_w * (p1 / denom), g_w * (p2 / denom))
    route = jnp.zeros_like(logits)
    for k, value in enumerate(fields):
        route = jnp.where(lane == float(k), value, route)
    route_ref[...] = route
    ids_ref[...] = jnp.transpose(route)[0:ROUTE_ID_ROWS, :].astype(jnp.int32)


ROUTE_E1, ROUTE_E2, ROUTE_RANK1, ROUTE_RANK2, ROUTE_W1, ROUTE_W2 = range(6)
ROUTE_ID_ROWS = 8


def _mix(x2d, a, o_f, o_b, og, w, tm):
    t = x2d.shape[0]
    hv = GLA_HEADS * GLA_DV
    row = lambda width: pl.BlockSpec((tm, width), lambda i: (i, 0))
    return pl.pallas_call(
        _mix_body,
        grid=(t // tm,),
        in_specs=[row(D_MODEL), row(MLA_HEADS * MLA_V), row(hv), row(hv), row(hv),
                  _full_spec((1, GLA_DV)), _full_spec(w['wout'].shape), _full_spec((1, D_MODEL)),
                  _full_spec(w['wr_hi'].shape), _full_spec(w['wr_lo'].shape), _full_spec(w['br'].shape)],
        out_specs=(row(D_MODEL), pl.BlockSpec((tm, ROW_CHUNKS, LANES), lambda i: (i, 0, 0)), row(LANES),
                   pl.BlockSpec((ROUTE_ID_ROWS, tm), lambda i: (0, i)), _full_spec((1, LANES))),
        out_shape=(jax.ShapeDtypeStruct((t, D_MODEL), F32), jax.ShapeDtypeStruct((t, ROW_CHUNKS, LANES), F32),
                   jax.ShapeDtypeStruct((t, LANES), F32), jax.ShapeDtypeStruct((ROUTE_ID_ROWS, t), jnp.int32),
                   jax.ShapeDtypeStruct((1, LANES), F32)),
        scratch_shapes=[pltpu.VMEM((tm, tm), BF16)],
        compiler_params=_cparams("arbitrary"),
        name="mix_router",
    )(x2d, a, o_f, o_b, og, w['gla_norm'], w['wout'], w['norm_ffn'], w['wr_hi'], w['wr_lo'], w['br'])


EXPERT_TILE = 256
ROW_TILE = 256


SC_CORES = 2
SC_SUBCORES = 16
SC_GATHER_ROWS = 32


def _sc_gather(table, idx):
    n = idx.shape[0]
    workers = SC_CORES * SC_SUBCORES
    per_worker = n // workers
    assert n % (workers * SC_GATHER_ROWS) == 0
    mesh = plsc.VectorSubcoreMesh(core_axis_name="c", subcore_axis_name="s")

    @functools.partial(
        pl.kernel, mesh=mesh,
        out_type=jax.ShapeDtypeStruct((n,) + table.shape[1:], table.dtype),
        scratch_types=[pltpu.VMEM((SC_GATHER_ROWS,), jnp.int32),
                       pltpu.VMEM((SC_GATHER_ROWS,) + table.shape[1:], table.dtype),
                       pltpu.SemaphoreType.DMA])
    def gather(table_ref, idx_ref, out_ref, idx_buf, rows_buf, sem):
        base = (lax.axis_index("s") * SC_CORES + lax.axis_index("c")) * per_worker

        @pl.loop(0, per_worker // SC_GATHER_ROWS)
        def _(j):
            rows = pl.ds(base + j * SC_GATHER_ROWS, SC_GATHER_ROWS)
            pltpu.sync_copy(idx_ref.at[rows], idx_buf)
            pltpu.async_copy(table_ref.at[idx_buf], rows_buf, sem).wait()
            pltpu.sync_copy(rows_buf, out_ref.at[rows])

    return gather(table, idx)


SLOT_TILE = 2048


def _slots_body(starts_ref, ids_ref, slots_ref):
    ids = ids_ref[...]
    experts = ids[ROUTE_E1:ROUTE_E1 + 2, :]
    start = jnp.zeros_like(experts)
    for e in range(N_EXPERTS):
        start = jnp.where(experts == e, starts_ref[e], start)
    slots_ref[...] = start + ids[ROUTE_RANK1:ROUTE_RANK1 + 2, :]


def _slots(starts, ids):
    t = ids.shape[1]
    return pl.pallas_call(
        _slots_body,
        grid_spec=pltpu.PrefetchScalarGridSpec(
            num_scalar_prefetch=1,
            grid=(t // SLOT_TILE,),
            in_specs=[pl.BlockSpec((ROUTE_ID_ROWS, SLOT_TILE), lambda i, *_: (0, i))],
            out_specs=pl.BlockSpec((2, SLOT_TILE), lambda i, *_: (0, i))),
        out_shape=jax.ShapeDtypeStruct((2, t), jnp.int32),
        compiler_params=_cparams("parallel"),
        name="moe_slots",
    )(starts, ids)


def _queue_body(slots_ref, token_ref, *, n_tokens):
    i = pl.program_id(0)

    @pl.when(i == 0)
    def _():
        for base in range(0, token_ref.shape[0], n_tokens):
            def clear(j, carry):
                token_ref[base + j] = j
                return carry
            lax.fori_loop(0, min(n_tokens, token_ref.shape[0] - base), clear, 0, unroll=8)

    def place(r, carry):
        for k in range(2):
            token_ref[slots_ref[k, r]] = i * SLOT_TILE + r
        return carry

    lax.fori_loop(0, SLOT_TILE, place, 0, unroll=8)


def _queue_tokens(slots, n_rows):
    t = slots.shape[1]
    return pl.pallas_call(
        functools.partial(_queue_body, n_tokens=t),
        grid=(t // SLOT_TILE,),
        in_specs=[pl.BlockSpec((2, SLOT_TILE), lambda i: (0, i), memory_space=pltpu.SMEM)],
        out_specs=pl.BlockSpec(memory_space=pltpu.SMEM),
        out_shape=jax.ShapeDtypeStruct((n_rows,), jnp.int32),
        compiler_params=_cparams("arbitrary"),
        name="moe_queue",
    )(slots)


def _expert_body(tile_expert_ref, n_tiles_ref, xs_ref, wgu_ref, wd_ref, ys_ref):
    used = pl.program_id(0) < n_tiles_ref[0]

    @pl.when(used)
    def _():
        h = _dot(_tiles_to_rows(xs_ref[...]).astype(BF16), wgu_ref[...])
        hg = h[:, :D_EXPERT]
        act = (hg / (1.0 + jnp.exp(-hg))) * h[:, D_EXPERT:]
        ys_ref[...] = _rows_to_tiles(_dot(act.astype(BF16), wd_ref[...]))

    @pl.when(jnp.logical_not(used))
    def _():
        ys_ref[...] = jnp.zeros_like(ys_ref)


def _experts(tile_expert, n_tiles, xs, w):
    n_grid = xs.shape[0] // EXPERT_TILE

    def tile(i, tile_expert, n_tiles):
        return jnp.minimum(i, n_tiles[0] - 1)

    tiles = lambda index: pl.BlockSpec((EXPERT_TILE, ROW_CHUNKS, LANES), index)
    return pl.pallas_call(
        _expert_body,
        grid_spec=pltpu.PrefetchScalarGridSpec(
            num_scalar_prefetch=2,
            grid=(n_grid,),
            in_specs=[tiles(lambda i, te, nt: (tile(i, te, nt), 0, 0)),
                      pl.BlockSpec((None, D_MODEL, 2 * D_EXPERT), lambda i, te, nt: (te[tile(i, te, nt)], 0, 0)),
                      pl.BlockSpec((None, D_EXPERT, D_MODEL), lambda i, te, nt: (te[tile(i, te, nt)], 0, 0))],
            out_specs=tiles(lambda i, te, nt: (i, 0, 0))),
        out_shape=jax.ShapeDtypeStruct(xs.shape, F32),
        compiler_params=_cparams("arbitrary"),
        name="moe_experts",
    )(tile_expert, n_tiles, xs, w['wgu'], w['wd'])


def _combine_body(h1_ref, route_ref, nfin_ref, y1_ref, y2_ref, out_ref):
    route = route_ref[...]
    lane = lax.broadcasted_iota(jnp.int32, route.shape, 1)
    w1 = jnp.sum(jnp.where(lane == ROUTE_W1, route, 0.0), axis=-1, keepdims=True)
    w2 = jnp.sum(jnp.where(lane == ROUTE_W2, route, 0.0), axis=-1, keepdims=True)
    y = w1 * _tiles_to_rows(y1_ref[...]) + w2 * _tiles_to_rows(y2_ref[...])
    out_ref[...] = _rms(h1_ref[...] + y, nfin_ref[...])


def _combine(h1, route, y12, w):
    t = h1.shape[0]
    row = lambda width: pl.BlockSpec((ROW_TILE, width), lambda i: (i, 0))
    tiles = lambda k: pl.BlockSpec((None, ROW_TILE, ROW_CHUNKS, LANES), lambda i: (k, i, 0, 0))
    return pl.pallas_call(
        _combine_body,
        grid=(t // ROW_TILE,),
        in_specs=[row(D_MODEL), row(LANES), _full_spec((1, D_MODEL)), tiles(0), tiles(1)],
        out_specs=row(D_MODEL),
        out_shape=jax.ShapeDtypeStruct((t, D_MODEL), F32),
        compiler_params=_cparams("parallel"),
        name="moe_combine",
    )(h1, route, w['norm_final'], y12, y12)


def _moe(hn, route, ids, counts, h1, w):
    t = hn.shape[0]
    n_rows = 2 * t + N_EXPERTS * EXPERT_TILE
    n_grid_tiles = n_rows // EXPERT_TILE
    count = counts[0, ROUTER_EXPERT_LANE0:ROUTER_EXPERT_LANE0 + N_EXPERTS].astype(jnp.int32)
    padded = jnp.maximum((count + EXPERT_TILE - 1) // EXPERT_TILE, 1) * EXPERT_TILE
    ends = jnp.sum(jnp.where(jnp.arange(N_EXPERTS)[:, None] <= jnp.arange(N_EXPERTS)[None, :], padded[:, None], 0),
                   axis=0)
    starts = ends - padded
    n_tiles = (ends[-1:] // EXPERT_TILE)
    tile_rows = jnp.arange(n_grid_tiles, dtype=jnp.int32) * EXPERT_TILE
    tile_expert = jnp.minimum(jnp.sum((ends[None, :] <= tile_rows[:, None]).astype(jnp.int32), axis=1), N_EXPERTS - 1)
    slots = _slots(starts, ids)
    tokens = _queue_tokens(slots, n_rows)
    ys = _experts(tile_expert, n_tiles, _sc_gather(hn, tokens), w)
    y12 = _sc_gather(ys, slots.reshape(2 * t)).reshape(2, t, ROW_CHUNKS, LANES)
    return _combine(h1, route, y12, w)


def _rope_tables(positions):
    inv_freq = 1.0 / (ROPE_THETA ** (jnp.arange(0, MLA_ROPE, 2, dtype=F32) / MLA_ROPE))
    ang = positions.astype(F32)[:, None] * inv_freq[None, :]
    cos, sin = jnp.cos(ang), jnp.sin(ang)
    reps = LANES // MLA_ROPE
    return jnp.tile(jnp.concatenate([cos, cos], axis=-1), (1, reps)), jnp.tile(jnp.concatenate([-sin, sin], axis=-1), (1, reps))


def _pack_weights(norm_mix, w_in, q_a_norm, w_uq, kv_a_norm, w_ukv, w_gate_fwd, b_gate_fwd, w_gate_bwd, b_gate_bwd,
                  gla_norm, w_out, norm_ffn, w_router_group, b_router_group, w_router_expert, b_router_expert,
                  w_expert_gate, w_expert_up, w_expert_down, norm_final):
    l = 0
    hk = GLA_HEADS * GLA_DK
    hv = GLA_HEADS * GLA_DV
    c_q, c_kv, k_pe, gq, gk, gv, lr_f, lr_b, og = jnp.split(
        w_in[l], np.cumsum([MLA_Q_RANK, MLA_KV_RANK, MLA_ROPE, hk, hk, hv, GLA_GATE_RANK, GLA_GATE_RANK])[:].tolist(),
        axis=-1)
    lr_pad = jnp.zeros((D_MODEL, LANES - 2 * GLA_GATE_RANK), F32)
    win = jnp.concatenate([c_q, c_kv, k_pe, k_pe, gq, gk, gv, og, lr_f, lr_b, lr_pad], axis=-1).astype(BF16)
    wuq = w_uq[l].reshape(MLA_Q_RANK, MLA_HEADS, MLA_NOPE + MLA_ROPE)
    wuq = jnp.concatenate([wuq[:, :, :MLA_NOPE].reshape(MLA_Q_RANK, -1), wuq[:, :, MLA_NOPE:].reshape(MLA_Q_RANK, -1)],
                          axis=-1).astype(BF16)
    wgate = jnp.zeros((LANES, 2 * hk), F32)
    wgate = wgate.at[0:GLA_GATE_RANK, 0:hk].set(w_gate_fwd[l])
    wgate = wgate.at[GLA_GATE_RANK:2 * GLA_GATE_RANK, hk:].set(w_gate_bwd[l])
    wr = jnp.zeros((D_MODEL, LANES), F32)
    wr = wr.at[:, ROUTER_GROUP_LANE0:ROUTER_GROUP_LANE0 + N_GROUPS].set(w_router_group[l])
    wr = wr.at[:, ROUTER_EXPERT_LANE0:ROUTER_EXPERT_LANE0 + N_EXPERTS].set(w_router_expert[l])
    wr_hi = wr.astype(BF16)
    br = jnp.zeros((1, LANES), F32)
    br = br.at[0, ROUTER_GROUP_LANE0:ROUTER_GROUP_LANE0 + N_GROUPS].set(b_router_group[l])
    br = br.at[0, ROUTER_EXPERT_LANE0:ROUTER_EXPERT_LANE0 + N_EXPERTS].set(b_router_expert[l])
    wgu = jnp.concatenate([w_expert_gate[l], w_expert_up[l]], axis=-1).reshape(N_EXPERTS, D_MODEL, 2 * D_EXPERT)
    return {
        'norm_mix': norm_mix[l][None], 'win': win, 'q_a_norm': q_a_norm[l][None], 'wuq': wuq,
        'kv_a_norm': kv_a_norm[l][None], 'wukv': w_ukv[l].astype(BF16),
        'wgate': wgate.astype(BF16), 'bgate': jnp.concatenate([b_gate_fwd[l], b_gate_bwd[l]])[None],
        'gla_norm': gla_norm[l][None], 'wout': w_out[l].astype(BF16), 'norm_ffn': norm_ffn[l][None],
        'wr_hi': wr_hi, 'wr_lo': (wr - wr_hi.astype(F32)).astype(BF16), 'br': br,
        'wgu': wgu.astype(BF16), 'wd': w_expert_down[l].reshape(N_EXPERTS, D_EXPERT, D_MODEL).astype(BF16),
        'norm_final': norm_final[None],
    }


def _meta_streams(meta_tokens, w):
    cos, sin = _rope_tables(jnp.arange(N_META))
    _, k, v, _, gk, gv, gf, _, _ = _inproj(meta_tokens, cos, sin, w, N_META)
    pad_keys = ((0, 0), (0, LANES - N_META), (0, 0))
    front = ((GLA_CHUNK - N_META, 0), (0, 0))
    return (jnp.pad(k, pad_keys), jnp.pad(v, pad_keys), jnp.pad(gk, front), jnp.pad(gv, front), jnp.pad(gf, front))


def _token_mixers(x, meta, w, tm, tq, tk, tb):
    bsz, seq, _ = x.shape
    km, vm, mk, mv, mg = meta
    x2d = x.reshape(bsz * seq, D_MODEL)
    cos, sin = _rope_tables(N_META + jnp.arange(seq))
    q, k, v, gq, gk, gv, gf, gb, og = _inproj(x2d, cos, sin, w, tm)
    a = _attention(q, k, v, km, vm, bsz, seq, tq, tk)
    o_f, o_b = _gla(gq, gk, gv, gf, gb, mk, mv, mg, bsz, seq, tb)
    return _mix(x2d, a, o_f, o_b, og, w, tm)


def kernel(x_prompt, x_sample, meta_tokens, norm_mix, w_in, q_a_norm, w_uq, kv_a_norm, w_ukv, w_gate_fwd, b_gate_fwd, w_gate_bwd, b_gate_bwd, gla_norm, w_out, norm_ffn, w_router_group, b_router_group, w_router_expert, b_router_expert, w_expert_gate, w_expert_up, w_expert_down, norm_final):
    w = _pack_weights(norm_mix, w_in, q_a_norm, w_uq, kv_a_norm, w_ukv, w_gate_fwd, b_gate_fwd, w_gate_bwd,
                      b_gate_bwd, gla_norm, w_out, norm_ffn, w_router_group, b_router_group, w_router_expert,
                      b_router_expert, w_expert_gate, w_expert_up, w_expert_down, norm_final)
    meta = _meta_streams(meta_tokens, w)
    outs = []
    for x in (x_prompt, x_sample):
        h1, hn, route, ids, counts = _token_mixers(x, meta, w, tm=512, tq=512, tk=512, tb=512)
        outs.append(_moe(hn, route, ids, counts, h1, w).reshape(x.shape))
    return tuple(outs)
```

```python
import functools

import numpy as np
import jax
import jax.numpy as jnp
from jax import lax
from jax.experimental import pallas as pl
from jax.experimental.pallas import tpu as pltpu
from jax.experimental.pallas import tpu_sc as plsc

F32 = jnp.float32
BF16 = jnp.bfloat16

D_MODEL = 1024
N_META = 16
MLA_HEADS = 4
MLA_Q_RANK = 384
MLA_KV_RANK = 256
MLA_NOPE = 128
MLA_ROPE = 64
MLA_V = 128
ROPE_THETA = 10000.0
GLA_HEADS = 4
GLA_DK = 64
GLA_DV = 128
GLA_GATE_RANK = 16
GLA_TAU = 16.0
GLA_CHUNK = 64
N_GROUPS = 4
EXPERTS_PER_GROUP = 8
N_EXPERTS = N_GROUPS * EXPERTS_PER_GROUP
D_EXPERT = 256
EPS = 1e-6

LANES = 128
V7X_VMEM_BYTES = 64 * 1024 * 1024
VMEM_LIMIT = V7X_VMEM_BYTES * 7 // 8

ATTN_SCALE = (MLA_NOPE + MLA_ROPE) ** -0.5 * float(np.log2(np.e))
QK_WIDTH = 2 * LANES
V_WIDTH = 2 * LANES
ATTN_GROUP = 8

C_CQ = 0
C_CKV = C_CQ + MLA_Q_RANK
C_KPE = C_CKV + MLA_KV_RANK
C_GQ = C_KPE + LANES
C_GK = C_GQ + GLA_HEADS * GLA_DK
C_GV = C_GK + GLA_HEADS * GLA_DK
C_OG = C_GV + GLA_HEADS * GLA_DV
C_LR = C_OG + GLA_HEADS * GLA_DV
D_IN_PACKED = C_LR + LANES

ROUTER_GROUP_LANE0 = 0
ROUTER_EXPERT_LANE0 = N_GROUPS


def _cparams(*semantics):
    return pltpu.CompilerParams(dimension_semantics=semantics, vmem_limit_bytes=VMEM_LIMIT)


def _rms(x, g):
    return x * lax.rsqrt(jnp.mean(x * x, axis=-1, keepdims=True) + EPS) * g


def _dot(a, b):
    return jnp.dot(a, b, preferred_element_type=F32)


def _dot_nt(a, b):
    return lax.dot_general(a, b, (((1,), (1,)), ((), ())), preferred_element_type=F32)


def _dot_tn(a, b):
    return lax.dot_general(a, b, (((0,), (0,)), ((), ())), preferred_element_type=F32)


def _full_spec(shape):
    return pl.BlockSpec(shape, lambda *_: (0,) * len(shape))


SUBLANES = 8
ROW_CHUNKS = D_MODEL // LANES
assert ROW_CHUNKS == SUBLANES


def _rows_to_tiles(x):
    chunks = jnp.stack([x[:, s * LANES:(s + 1) * LANES] for s in range(ROW_CHUNKS)], axis=0)
    return pltpu.einshape("smd->msd", chunks)


def _tiles_to_rows(x):
    chunks = pltpu.einshape("msd->smd", x)
    return jnp.concatenate([chunks[s] for s in range(ROW_CHUNKS)], axis=-1)


def _rope_pairs(x, cos, sin_signed, first_half):
    swapped = jnp.where(first_half, pltpu.roll(x, LANES - MLA_ROPE // 2, 1), pltpu.roll(x, MLA_ROPE // 2, 1))
    return x * cos + swapped * sin_signed


def _inproj_body(x_ref, cos_ref, sin_ref, nmix_ref, win_ref, qan_ref, wuq_ref, kvan_ref, wukv_ref,
                 wgate_ref, bgate_ref,
                 q_ref, k_ref, v_ref, gq_ref, gk_ref, gv_ref, gf_ref, gb_ref, og_ref):
    hn = _rms(x_ref[...], nmix_ref[...]).astype(BF16)

    def proj(lo, hi):
        return _dot(hn, win_ref[:, lo:hi])

    cos = cos_ref[...]
    sin = sin_ref[...]
    lane = lax.broadcasted_iota(jnp.int32, cos.shape, 1)
    first_half = (lane & (MLA_ROPE - 1)) < MLA_ROPE // 2
    low_lanes = lane < MLA_ROPE

    cq = _rms(proj(C_CQ, C_CKV), qan_ref[...]).astype(BF16)
    qn = _dot(cq, wuq_ref[:, 0:MLA_HEADS * MLA_NOPE]) * ATTN_SCALE
    qr = _dot(cq, wuq_ref[:, MLA_HEADS * MLA_NOPE:])
    for j in range(MLA_HEADS // 2):
        rj = (_rope_pairs(qr[:, j * LANES:(j + 1) * LANES], cos, sin, first_half) * ATTN_SCALE).astype(BF16)
        for h in (2 * j, 2 * j + 1):
            q_ref[h, :, 0:LANES] = qn[:, h * LANES:(h + 1) * LANES].astype(BF16)
            q_ref[h, :, LANES:QK_WIDTH] = rj

    ckv = _rms(proj(C_CKV, C_KPE), kvan_ref[...]).astype(BF16)
    kv = _dot(ckv, wukv_ref[...])
    kr = _rope_pairs(proj(C_KPE, C_GQ), cos, sin, first_half)
    kr_even = jnp.where(low_lanes, kr, 0.0).astype(BF16)
    kr_odd = jnp.where(low_lanes, 0.0, kr).astype(BF16)
    for h in range(MLA_HEADS):
        base = h * (MLA_NOPE + MLA_V)
        k_ref[h, :, 0:LANES] = kv[:, base:base + MLA_NOPE].astype(BF16)
        k_ref[h, :, LANES:QK_WIDTH] = kr_even if h % 2 == 0 else kr_odd
        v_ref[h, :, 0:MLA_V] = kv[:, base + MLA_NOPE:base + MLA_NOPE + MLA_V].astype(BF16)
        v_ref[h, :, MLA_V:V_WIDTH] = jnp.ones((kv.shape[0], V_WIDTH - MLA_V), BF16)

    gq_ref[...] = proj(C_GQ, C_GK) * (GLA_DK ** -0.5)
    gk_ref[...] = proj(C_GK, C_GV)
    gv_ref[...] = proj(C_GV, C_OG)
    og_ref[...] = proj(C_OG, C_LR)
    pre = _dot(proj(C_LR, D_IN_PACKED).astype(BF16), wgate_ref[...]) + bgate_ref[...]
    logsig = jnp.minimum(pre, 0.0) - jnp.log1p(jnp.exp(-jnp.abs(pre)))
    gates = logsig * (1.0 / GLA_TAU)
    gf_ref[...] = gates[:, 0:GLA_HEADS * GLA_DK]
    gb_ref[...] = gates[:, GLA_HEADS * GLA_DK:]


def _inproj(x2d, cos, sin, w, tm):
    t = x2d.shape[0]
    blocks_per_seq = cos.shape[0] // tm
    hk = GLA_HEADS * GLA_DK
    hv = GLA_HEADS * GLA_DV
    row = lambda width: pl.BlockSpec((tm, width), lambda i: (i, 0))
    head_rows = lambda width: pl.BlockSpec((MLA_HEADS, tm, width), lambda i: (0, i, 0))
    tab = pl.BlockSpec((tm, LANES), lambda i: (i % blocks_per_seq, 0))
    out_shape = (
        jax.ShapeDtypeStruct((MLA_HEADS, t, QK_WIDTH), BF16),
        jax.ShapeDtypeStruct((MLA_HEADS, t, QK_WIDTH), BF16),
        jax.ShapeDtypeStruct((MLA_HEADS, t, V_WIDTH), BF16),
        jax.ShapeDtypeStruct((t, hk), F32),
        jax.ShapeDtypeStruct((t, hk), F32),
        jax.ShapeDtypeStruct((t, hv), F32),
        jax.ShapeDtypeStruct((t, hk), F32),
        jax.ShapeDtypeStruct((t, hk), F32),
        jax.ShapeDtypeStruct((t, hv), F32),
    )
    return pl.pallas_call(
        _inproj_body,
        grid=(t // tm,),
        in_specs=[row(D_MODEL), tab, tab,
                  _full_spec((1, D_MODEL)), _full_spec((D_MODEL, D_IN_PACKED)),
                  _full_spec((1, MLA_Q_RANK)), _full_spec(w['wuq'].shape),
                  _full_spec((1, MLA_KV_RANK)), _full_spec(w['wukv'].shape),
                  _full_spec(w['wgate'].shape), _full_spec(w['bgate'].shape)],
        out_specs=(head_rows(QK_WIDTH), head_rows(QK_WIDTH), head_rows(V_WIDTH),
                   row(hk), row(hk), row(hv), row(hk), row(hk), row(hv)),
        out_shape=out_shape,
        compiler_params=_cparams("parallel"),
        name="inproj",
    )(x2d, cos, sin, w['norm_mix'], w['win'], w['q_a_norm'], w['wuq'], w['kv_a_norm'], w['wukv'],
      w['wgate'], w['bgate'])


def _attn_body(q_ref, k_ref, v_ref, km_ref, vm_ref, o_ref, s_ref, acc_ref, *, tk):
    q = q_ref[...]
    n_groups = k_ref.shape[0] // (ATTN_GROUP * tk)

    def scores(j):
        return _dot_nt(q, k_ref[pl.ds(pl.multiple_of(j * tk, tk), tk), :])

    def values(j):
        return v_ref[pl.ds(pl.multiple_of(j * tk, tk), tk), :]

    def absorb(m, s, v):
        m_new = jnp.maximum(m, jnp.max(s, axis=-1, keepdims=True))
        p = jnp.exp2(s - m_new)
        acc_ref[...] = jnp.exp2(m - m_new) * acc_ref[...] + _dot(p.astype(BF16), v)
        return m_new

    s_ref[0] = scores(0)
    sm = _dot_nt(q, km_ref[...])
    sm = jnp.where(lax.broadcasted_iota(jnp.int32, sm.shape, 1) < N_META, sm, -jnp.inf)
    m = jnp.max(sm, axis=-1, keepdims=True)
    acc_ref[...] = _dot(jnp.exp2(sm - m).astype(BF16), vm_ref[...])

    def group(g, m, last):
        for i in range(ATTN_GROUP):
            j = ATTN_GROUP * g + i
            s = s_ref[i % 2]
            if not (last and i == ATTN_GROUP - 1):
                s_ref[(i + 1) % 2] = scores(j + 1)
            m = absorb(m, s, values(j))
        return m

    m = lax.fori_loop(0, n_groups - 1, lambda g, m: group(g, m, False), m)
    group(n_groups - 1, m, True)
    acc = acc_ref[...]
    o_ref[...] = (acc[:, :MLA_V] / acc[:, MLA_V:]).astype(o_ref.dtype)


def _attention(q, k, v, km, vm, bsz, seq, tq, tk):
    nq = seq // tq
    return pl.pallas_call(
        functools.partial(_attn_body, tk=tk),
        grid=(bsz, MLA_HEADS, nq),
        in_specs=[pl.BlockSpec((None, tq, QK_WIDTH), lambda b, h, i: (h, b * nq + i, 0)),
                  pl.BlockSpec((None, seq, QK_WIDTH), lambda b, h, i: (h, b, 0)),
                  pl.BlockSpec((None, seq, V_WIDTH), lambda b, h, i: (h, b, 0)),
                  pl.BlockSpec((None, LANES, QK_WIDTH), lambda b, h, i: (h, 0, 0)),
                  pl.BlockSpec((None, LANES, V_WIDTH), lambda b, h, i: (h, 0, 0))],
        out_specs=pl.BlockSpec((tq, MLA_V), lambda b, h, i: (b * nq + i, h)),
        out_shape=jax.ShapeDtypeStruct((bsz * seq, MLA_HEADS * MLA_V), BF16),
        scratch_shapes=[pltpu.VMEM((2, tq, tk), F32), pltpu.VMEM((tq, V_WIDTH), F32)],
        compiler_params=_cparams("parallel", "parallel", "arbitrary"),
        name="mla_attention",
    )(q, k, v, km, vm)


def _split3(x):
    hi = x.astype(BF16)
    r1 = x - hi.astype(F32)
    mid = r1.astype(BF16)
    lo = (r1 - mid.astype(F32)).astype(BF16)
    return hi, mid, lo


def _gla_log_decay(g, tri):
    g_hi, g_mid, g_lo = _split3(g)
    return _dot(tri, g_hi) + _dot(tri, g_mid) + _dot(tri, g_lo)


def _head_rows(x):
    even = lax.broadcasted_iota(jnp.int32, x.shape, 1) < GLA_DK
    return jnp.concatenate([jnp.where(even, x, 0.0), jnp.where(even, 0.0, x)], axis=0).astype(BF16)


def _gla_operands(q, k, b, mid, last):
    b_last = b[last:last + 1, :]
    ke = (k * jnp.exp(b_last - b)).astype(BF16)
    if q is None:
        return jnp.exp(b_last), ke, None, None, None
    b_mid = b[mid:mid + 1, :]
    ks = (k * jnp.exp(b_mid - b)).astype(BF16)
    return jnp.exp(b_last), ke, ks, _head_rows(q * jnp.exp(b - b_mid)), _head_rows(q * jnp.exp(b))


def _gla_state_update(v_even, v_odd, ke):
    return jnp.where(lax.broadcasted_iota(jnp.int32, (GLA_DV, LANES), 1) < GLA_DK,
                     _dot_tn(v_even, ke), _dot_tn(v_odd, ke))


def _gla_intra(v_even, v_odd, ks, qs2, keep):
    scores = jnp.where(keep, _dot_nt(qs2, ks), 0.0).astype(BF16)
    return _dot(scores[:GLA_CHUNK], v_even), _dot(scores[GLA_CHUNK:], v_odd)


def _gla_body(qf_ref, kf_ref, vf_ref, gf_ref, qb_ref, kb_ref, vb_ref, gb_ref, mk_ref, mv_ref, mg_ref,
              of_ref, ob_ref, state_ref):
    n_chunks = qf_ref.shape[0] // GLA_CHUNK
    n_pairs = GLA_HEADS // 2
    r = lax.broadcasted_iota(jnp.int32, (GLA_CHUNK, GLA_CHUNK), 0)
    c = lax.broadcasted_iota(jnp.int32, (GLA_CHUNK, GLA_CHUNK), 1)
    tri_f = jnp.where(c <= r, 1.0, 0.0).astype(BF16)
    tri_b = jnp.where(c >= r, 1.0, 0.0).astype(BF16)
    r2 = lax.broadcasted_iota(jnp.int32, (2 * GLA_CHUNK, GLA_CHUNK), 0) & (GLA_CHUNK - 1)
    c2 = lax.broadcasted_iota(jnp.int32, (2 * GLA_CHUNK, GLA_CHUNK), 1)
    keep_f = c2 <= r2
    keep_b = c2 >= r2
    mid_f, last_f = GLA_CHUNK // 2 - 1, GLA_CHUNK - 1
    mid_b, last_b = GLA_CHUNK // 2, 0

    def pair_cols(p):
        return slice(p * LANES, (p + 1) * LANES)

    def head_cols(h):
        return slice(h * GLA_DV, (h + 1) * GLA_DV)

    @pl.when(pl.program_id(1) == 0)
    def _():
        for p in range(n_pairs):
            b = _gla_log_decay(mg_ref[:, pair_cols(p)], tri_f)
            _, ke, _, _, _ = _gla_operands(None, mk_ref[:, pair_cols(p)], b, mid_f, last_f)
            state_ref[p] = _gla_state_update(mv_ref[:, head_cols(2 * p)].astype(BF16),
                                             mv_ref[:, head_cols(2 * p + 1)].astype(BF16), ke)
            state_ref[n_pairs + p] = jnp.zeros((GLA_DV, LANES), F32)

    scans = []
    for p in range(n_pairs):
        scans.append((p, list(range(n_chunks)), qf_ref, kf_ref, vf_ref, gf_ref, of_ref, p,
                      tri_f, keep_f, mid_f, last_f))
        scans.append((p, list(reversed(range(n_chunks))), qb_ref, kb_ref, vb_ref, gb_ref, ob_ref, n_pairs + p,
                      tri_b, keep_b, mid_b, last_b))

    def rows(c):
        return slice(c * GLA_CHUNK, (c + 1) * GLA_CHUNK)

    def values(v_ref, p, c):
        return (v_ref[rows(c), head_cols(2 * p)].astype(BF16), v_ref[rows(c), head_cols(2 * p + 1)].astype(BF16))

    log_decay = [[_gla_log_decay(g_ref[rows(c), pair_cols(p)], tri) for c in order]
                 for (p, order, _, _, _, g_ref, _, _, tri, _, _, _) in scans]
    operands = [[_gla_operands(q_ref[rows(c), pair_cols(p)], k_ref[rows(c), pair_cols(p)], b, mid, last)
                 for c, b in zip(order, bs)]
                for (p, order, q_ref, k_ref, _, _, _, _, _, _, mid, last), bs in zip(scans, log_decay)]
    updates = [[_gla_state_update(*values(v_ref, p, c), ops[1]) for c, ops in zip(order, opss)]
               for (p, order, _, _, v_ref, _, _, _, _, _, _, _), opss in zip(scans, operands)]
    intra = [[_gla_intra(*values(v_ref, p, c), ops[2], ops[3], keep) for c, ops in zip(order, opss)]
             for (p, order, _, _, v_ref, _, _, _, _, keep, _, _), opss in zip(scans, operands)]
    states = []
    for (_, order, _, _, _, _, _, slot, _, _, _, _), opss, upds in zip(scans, operands, updates):
        st = state_ref[slot]
        entering = []
        for ops, upd in zip(opss, upds):
            entering.append(st.astype(BF16))
            st = st * ops[0] + upd
        state_ref[slot] = st
        states.append(entering)
    for (p, order, _, _, _, _, o_ref, _, _, _, _, _), opss, sts, locs in zip(scans, operands, states, intra):
        for c, ops, st, (o_even, o_odd) in zip(order, opss, sts, locs):
            inter = _dot_nt(ops[4], st)
            o_ref[rows(c), head_cols(2 * p)] = o_even + inter[:GLA_CHUNK]
            o_ref[rows(c), head_cols(2 * p + 1)] = o_odd + inter[GLA_CHUNK:]


def _gla(gq, gk, gv, gf, gb, mk, mv, mg, bsz, seq, tb):
    nb = seq // tb
    hk = GLA_HEADS * GLA_DK
    hv = GLA_HEADS * GLA_DV
    fwd = lambda width: pl.BlockSpec((tb, width), lambda b, j: (b * nb + j, 0))
    bwd = lambda width: pl.BlockSpec((tb, width), lambda b, j: (b * nb + nb - 1 - j, 0))
    t = bsz * seq
    return pl.pallas_call(
        _gla_body,
        grid=(bsz, nb),
        in_specs=[fwd(hk), fwd(hk), fwd(hv), fwd(hk), bwd(hk), bwd(hk), bwd(hv), bwd(hk),
                  _full_spec(mk.shape), _full_spec(mv.shape), _full_spec(mg.shape)],
        out_specs=(fwd(hv), bwd(hv)),
        out_shape=(jax.ShapeDtypeStruct((t, hv), F32), jax.ShapeDtypeStruct((t, hv), F32)),
        scratch_shapes=[pltpu.VMEM((2 * (GLA_HEADS // 2), GLA_DV, LANES), F32)],
        compiler_params=_cparams("parallel", "arbitrary"),
        name="gla_scan",
    )(gq, gk, gv, gf, gq, gk, gv, gb, mk, mv, mg)


def _mix_body(x_ref, a_ref, of_ref, ob_ref, og_ref, gnorm_ref, wout_ref, nffn_ref, wr_hi_ref, wr_lo_ref, br_ref,
              h1_ref, hn_ref, route_ref, ids_ref, count_ref, tri_ref):
    tm = x_ref.shape[0]

    @pl.when(pl.program_id(0) == 0)
    def _():
        r = lax.broadcasted_iota(jnp.int32, (tm, tm), 0)
        c = lax.broadcasted_iota(jnp.int32, (tm, tm), 1)
        tri_ref[...] = jnp.where(c < r, 1.0, 0.0).astype(BF16)
        count_ref[...] = jnp.zeros_like(count_ref)

    a_width = MLA_HEADS * MLA_V
    h1 = x_ref[...] + _dot(a_ref[...], wout_ref[0:a_width, :])
    for h in range(GLA_HEADS):
        cols = slice(h * GLA_DV, (h + 1) * GLA_DV)
        o = of_ref[:, cols] + ob_ref[:, cols]
        og = og_ref[:, cols]
        silu = og / (1.0 + jnp.exp(-og))
        gh = (_rms(o, gnorm_ref[...]) * silu).astype(BF16)
        h1 = h1 + _dot(gh, wout_ref[a_width + h * GLA_DV:a_width + (h + 1) * GLA_DV, :])
    h1_ref[...] = h1
    hn = _rms(h1, nffn_ref[...])
    hn_ref[...] = _rows_to_tiles(hn)

    hn_hi = hn.astype(BF16)
    hn_lo = (hn - hn_hi.astype(F32)).astype(BF16)
    logits = (_dot(hn_hi, wr_hi_ref[...]) + _dot(hn_lo, wr_hi_ref[...]) + _dot(hn_hi, wr_lo_ref[...])
              + br_ref[...])
    lane = lax.broadcasted_iota(jnp.int32, logits.shape, 1).astype(F32)
    none = float(LANES)
    neg = -jnp.inf

    def lane_max(x):
        return jnp.max(x, axis=-1, keepdims=True)

    def lane_sum(x):
        return jnp.sum(x, axis=-1, keepdims=True)

    def first_lane(mask):
        return jnp.min(jnp.where(mask, lane, none), axis=-1, keepdims=True)

    is_group = lane < float(N_GROUPS)
    g_max = lane_max(jnp.where(is_group, logits, neg))
    g_exp = jnp.where(is_group, jnp.exp(logits - g_max), 0.0)
    g_prob = g_exp / lane_sum(g_exp)
    g_w = lane_max(g_prob)
    g_idx = first_lane(is_group & (g_prob == g_w))
    e_lo = float(ROUTER_EXPERT_LANE0) + float(EXPERTS_PER_GROUP) * g_idx
    sel = (lane >= e_lo) & (lane < e_lo + float(EXPERTS_PER_GROUP))
    e_max = lane_max(jnp.where(sel, logits, neg))
    e_exp = jnp.where(sel, jnp.exp(logits - e_max), 0.0)
    e_prob = e_exp / lane_sum(e_exp)
    p1 = lane_max(jnp.where(sel, e_prob, neg))
    i1 = first_lane(sel & (e_prob == p1))
    rest = sel & (lane != i1)
    p2 = lane_max(jnp.where(rest, e_prob, neg))
    i2 = first_lane(rest & (e_prob == p2))
    denom = p1 + p2
    chosen = jnp.where((lane == i1) | (lane == i2), 1.0, 0.0)
    rank = count_ref[...] + _dot(tri_ref[...], chosen.astype(BF16))
    count_ref[...] += jnp.sum(chosen, axis=0, keepdims=True)
    fields = (i1 - float(ROUTER_EXPERT_LANE0), i2 - float(ROUTER_EXPERT_LANE0),
              lane_sum(jnp.where(lane == i1, rank, 0.0)), lane_sum(jnp.where(lane == i2, rank, 0.0)),
              g_w * (p1 / denom), g_w * (p2 / denom))
    route = jnp.zeros_like(logits)
    for k, value in enumerate(fields):
        route = jnp.where(lane == float(k), value, route)
    route_ref[...] = route
    ids_ref[...] = jnp.transpose(route)[0:ROUTE_ID_ROWS, :].astype(jnp.int32)


ROUTE_E1, ROUTE_E2, ROUTE_RANK1, ROUTE_RANK2, ROUTE_W1, ROUTE_W2 = range(6)
ROUTE_ID_ROWS = 8


def _mix(x2d, a, o_f, o_b, og, w, tm):
    t = x2d.shape[0]
    hv = GLA_HEADS * GLA_DV
    row = lambda width: pl.BlockSpec((tm, width), lambda i: (i, 0))
    return pl.pallas_call(
        _mix_body,
        grid=(t // tm,),
        in_specs=[row(D_MODEL), row(MLA_HEADS * MLA_V), row(hv), row(hv), row(hv),
                  _full_spec((1, GLA_DV)), _full_spec(w['wout'].shape), _full_spec((1, D_MODEL)),
                  _full_spec(w['wr_hi'].shape), _full_spec(w['wr_lo'].shape), _full_spec(w['br'].shape)],
        out_specs=(row(D_MODEL), pl.BlockSpec((tm, ROW_CHUNKS, LANES), lambda i: (i, 0, 0)), row(LANES),
                   pl.BlockSpec((ROUTE_ID_ROWS, tm), lambda i: (0, i)), _full_spec((1, LANES))),
        out_shape=(jax.ShapeDtypeStruct((t, D_MODEL), F32), jax.ShapeDtypeStruct((t, ROW_CHUNKS, LANES), F32),
                   jax.ShapeDtypeStruct((t, LANES), F32), jax.ShapeDtypeStruct((ROUTE_ID_ROWS, t), jnp.int32),
                   jax.ShapeDtypeStruct((1, LANES), F32)),
        scratch_shapes=[pltpu.VMEM((tm, tm), BF16)],
        compiler_params=_cparams("arbitrary"),
        name="mix_router",
    )(x2d, a, o_f, o_b, og, w['gla_norm'], w['wout'], w['norm_ffn'], w['wr_hi'], w['wr_lo'], w['br'])


EXPERT_TILE = 256
ROW_TILE = 256


SC_CORES = 2
SC_SUBCORES = 16
SC_GATHER_ROWS = 32


def _sc_gather(table, idx):
    n = idx.shape[0]
    workers = SC_CORES * SC_SUBCORES
    per_worker = n // workers
    assert n % (workers * SC_GATHER_ROWS) == 0
    mesh = plsc.VectorSubcoreMesh(core_axis_name="c", subcore_axis_name="s")

    @functools.partial(
        pl.kernel, mesh=mesh,
        out_type=jax.ShapeDtypeStruct((n,) + table.shape[1:], table.dtype),
        scratch_types=[pltpu.VMEM((SC_GATHER_ROWS,), jnp.int32),
                       pltpu.VMEM((SC_GATHER_ROWS,) + table.shape[1:], table.dtype),
                       pltpu.SemaphoreType.DMA])
    def gather(table_ref, idx_ref, out_ref, idx_buf, rows_buf, sem):
        base = (lax.axis_index("s") * SC_CORES + lax.axis_index("c")) * per_worker

        @pl.loop(0, per_worker // SC_GATHER_ROWS)
        def _(j):
            rows = pl.ds(base + j * SC_GATHER_ROWS, SC_GATHER_ROWS)
            pltpu.sync_copy(idx_ref.at[rows], idx_buf)
            pltpu.async_copy(table_ref.at[idx_buf], rows_buf, sem).wait()
            pltpu.sync_copy(rows_buf, out_ref.at[rows])

    return gather(table, idx)


def _sc_scatter(rows, idx, n_out):
    copies, n = idx.shape
    idx = idx.reshape(copies * n)
    workers = SC_CORES * SC_SUBCORES
    per_worker = n // workers
    assert n % (workers * SC_GATHER_ROWS) == 0
    mesh = plsc.VectorSubcoreMesh(core_axis_name="c", subcore_axis_name="s")

    @functools.partial(
        pl.kernel, mesh=mesh,
        out_type=jax.ShapeDtypeStruct((n_out,) + rows.shape[1:], rows.dtype),
        scratch_types=[pltpu.VMEM((SC_GATHER_ROWS,), jnp.int32),
                       pltpu.VMEM((SC_GATHER_ROWS,) + rows.shape[1:], rows.dtype),
                       pltpu.SemaphoreType.DMA])
    def scatter(rows_ref, idx_ref, out_ref, idx_buf, rows_buf, sem):
        base = (lax.axis_index("s") * SC_CORES + lax.axis_index("c")) * per_worker

        @pl.loop(0, per_worker // SC_GATHER_ROWS)
        def _(j):
            first = base + j * SC_GATHER_ROWS
            pltpu.sync_copy(rows_ref.at[pl.ds(first, SC_GATHER_ROWS)], rows_buf)
            for k in range(copies):
                pltpu.sync_copy(idx_ref.at[pl.ds(k * n + first, SC_GATHER_ROWS)], idx_buf)
                pltpu.async_copy(rows_buf, out_ref.at[idx_buf], sem).wait()

    return scatter(rows, idx)


SLOT_TILE = 2048


def _slots_body(starts_ref, ids_ref, slots_ref):
    ids = ids_ref[...]
    experts = ids[ROUTE_E1:ROUTE_E1 + 2, :]
    start = jnp.zeros_like(experts)
    for e in range(N_EXPERTS):
        start = jnp.where(experts == e, starts_ref[e], start)
    slots_ref[...] = start + ids[ROUTE_RANK1:ROUTE_RANK1 + 2, :]


def _slots(starts, ids):
    t = ids.shape[1]
    return pl.pallas_call(
        _slots_body,
        grid_spec=pltpu.PrefetchScalarGridSpec(
            num_scalar_prefetch=1,
            grid=(t // SLOT_TILE,),
            in_specs=[pl.BlockSpec((ROUTE_ID_ROWS, SLOT_TILE), lambda i, *_: (0, i))],
            out_specs=pl.BlockSpec((2, SLOT_TILE), lambda i, *_: (0, i))),
        out_shape=jax.ShapeDtypeStruct((2, t), jnp.int32),
        compiler_params=_cparams("parallel"),
        name="moe_slots",
    )(starts, ids)


def _queue_body(slots_ref, token_ref, *, n_tokens):
    i = pl.program_id(0)

    @pl.when(i == 0)
    def _():
        for base in range(0, token_ref.shape[0], n_tokens):
            def clear(j, carry):
                token_ref[base + j] = j
                return carry
            lax.fori_loop(0, min(n_tokens, token_ref.shape[0] - base), clear, 0, unroll=8)

    def place(r, carry):
        for k in range(2):
            token_ref[slots_ref[k, r]] = i * SLOT_TILE + r
        return carry

    lax.fori_loop(0, SLOT_TILE, place, 0, unroll=8)


def _queue_tokens(slots, n_rows):
    t = slots.shape[1]
    return pl.pallas_call(
        functools.partial(_queue_body, n_tokens=t),
        grid=(t // SLOT_TILE,),
        in_specs=[pl.BlockSpec((2, SLOT_TILE), lambda i: (0, i), memory_space=pltpu.SMEM)],
        out_specs=pl.BlockSpec(memory_space=pltpu.SMEM),
        out_shape=jax.ShapeDtypeStruct((n_rows,), jnp.int32),
        compiler_params=_cparams("arbitrary"),
        name="moe_queue",
    )(slots)


def _expert_body(tile_expert_ref, n_tiles_ref, tile_valid_ref, xs_ref, wgu_ref, wd_ref, ys_ref):
    i = pl.program_id(0)
    used = i < n_tiles_ref[0]

    @pl.when(used)
    def _():
        x = _tiles_to_rows(xs_ref[...])
        x = jnp.where(lax.broadcasted_iota(jnp.int32, x.shape, 0) < tile_valid_ref[i], x, 0.0)
        h = _dot(x.astype(BF16), wgu_ref[...])
        hg = h[:, :D_EXPERT]
        act = (hg / (1.0 + jnp.exp(-hg))) * h[:, D_EXPERT:]
        ys_ref[...] = _rows_to_tiles(_dot(act.astype(BF16), wd_ref[...]))

    @pl.when(jnp.logical_not(used))
    def _():
        ys_ref[...] = jnp.zeros_like(ys_ref)


def _experts(tile_expert, n_tiles, tile_valid, xs, w):
    n_grid = xs.shape[0] // EXPERT_TILE

    def tile(i, n_tiles):
        return jnp.minimum(i, n_tiles[0] - 1)

    tiles = lambda index: pl.BlockSpec((EXPERT_TILE, ROW_CHUNKS, LANES), index)
    return pl.pallas_call(
        _expert_body,
        grid_spec=pltpu.PrefetchScalarGridSpec(
            num_scalar_prefetch=3,
            grid=(n_grid,),
            in_specs=[tiles(lambda i, te, nt, tv: (tile(i, nt), 0, 0)),
                      pl.BlockSpec((None, D_MODEL, 2 * D_EXPERT), lambda i, te, nt, tv: (te[tile(i, nt)], 0, 0)),
                      pl.BlockSpec((None, D_EXPERT, D_MODEL), lambda i, te, nt, tv: (te[tile(i, nt)], 0, 0))],
            out_specs=tiles(lambda i, te, nt, tv: (i, 0, 0))),
        out_shape=jax.ShapeDtypeStruct(xs.shape, F32),
        compiler_params=_cparams("arbitrary"),
        name="moe_experts",
    )(tile_expert, n_tiles, tile_valid, xs, w['wgu'], w['wd'])


def _combine_body(h1_ref, route_ref, nfin_ref, y1_ref, y2_ref, out_ref):
    route = route_ref[...]
    lane = lax.broadcasted_iota(jnp.int32, route.shape, 1)
    w1 = jnp.sum(jnp.where(lane == ROUTE_W1, route, 0.0), axis=-1, keepdims=True)
    w2 = jnp.sum(jnp.where(lane == ROUTE_W2, route, 0.0), axis=-1, keepdims=True)
    y = w1 * _tiles_to_rows(y1_ref[...]) + w2 * _tiles_to_rows(y2_ref[...])
    out_ref[...] = _rms(h1_ref[...] + y, nfin_ref[...])


def _combine(h1, route, y12, w):
    t = h1.shape[0]
    row = lambda width: pl.BlockSpec((ROW_TILE, width), lambda i: (i, 0))
    tiles = lambda k: pl.BlockSpec((None, ROW_TILE, ROW_CHUNKS, LANES), lambda i: (k, i, 0, 0))
    return pl.pallas_call(
        _combine_body,
        grid=(t // ROW_TILE,),
        in_specs=[row(D_MODEL), row(LANES), _full_spec((1, D_MODEL)), tiles(0), tiles(1)],
        out_specs=row(D_MODEL),
        out_shape=jax.ShapeDtypeStruct((t, D_MODEL), F32),
        compiler_params=_cparams("parallel"),
        name="moe_combine",
    )(h1, route, w['norm_final'], y12, y12)


def _moe(hn, route, ids, counts, h1, w):
    t = hn.shape[0]
    n_rows = 2 * t + N_EXPERTS * EXPERT_TILE
    n_grid_tiles = n_rows // EXPERT_TILE
    count = counts[0, ROUTER_EXPERT_LANE0:ROUTER_EXPERT_LANE0 + N_EXPERTS].astype(jnp.int32)
    padded = jnp.maximum((count + EXPERT_TILE - 1) // EXPERT_TILE, 1) * EXPERT_TILE
    ends = jnp.sum(jnp.where(jnp.arange(N_EXPERTS)[:, None] <= jnp.arange(N_EXPERTS)[None, :], padded[:, None], 0),
                   axis=0)
    starts = ends - padded
    n_tiles = (ends[-1:] // EXPERT_TILE)
    tile_rows = jnp.arange(n_grid_tiles, dtype=jnp.int32) * EXPERT_TILE
    tile_expert = jnp.minimum(jnp.sum((ends[None, :] <= tile_rows[:, None]).astype(jnp.int32), axis=1), N_EXPERTS - 1)
    tile_valid = jnp.clip(jnp.sum(jnp.where(jnp.arange(N_EXPERTS)[None, :] == tile_expert[:, None],
                                            (starts + count)[None, :], 0), axis=1) - tile_rows, 0, EXPERT_TILE)
    slots = _slots(starts, ids)
    ys = _experts(tile_expert, n_tiles, tile_valid, _sc_scatter(hn, slots, n_rows), w)
    y12 = _sc_gather(ys, slots.reshape(2 * t)).reshape(2, t, ROW_CHUNKS, LANES)
    return _combine(h1, route, y12, w)


def _rope_tables(positions):
    inv_freq = 1.0 / (ROPE_THETA ** (jnp.arange(0, MLA_ROPE, 2, dtype=F32) / MLA_ROPE))
    ang = positions.astype(F32)[:, None] * inv_freq[None, :]
    cos, sin = jnp.cos(ang), jnp.sin(ang)
    reps = LANES // MLA_ROPE
    return jnp.tile(jnp.concatenate([cos, cos], axis=-1), (1, reps)), jnp.tile(jnp.concatenate([-sin, sin], axis=-1), (1, reps))


def _pack_weights(norm_mix, w_in, q_a_norm, w_uq, kv_a_norm, w_ukv, w_gate_fwd, b_gate_fwd, w_gate_bwd, b_gate_bwd,
                  gla_norm, w_out, norm_ffn, w_router_group, b_router_group, w_router_expert, b_router_expert,
                  w_expert_gate, w_expert_up, w_expert_down, norm_final):
    l = 0
    hk = GLA_HEADS * GLA_DK
    hv = GLA_HEADS * GLA_DV
    c_q, c_kv, k_pe, gq, gk, gv, lr_f, lr_b, og = jnp.split(
        w_in[l], np.cumsum([MLA_Q_RANK, MLA_KV_RANK, MLA_ROPE, hk, hk, hv, GLA_GATE_RANK, GLA_GATE_RANK])[:].tolist(),
        axis=-1)
    lr_pad = jnp.zeros((D_MODEL, LANES - 2 * GLA_GATE_RANK), F32)
    win = jnp.concatenate([c_q, c_kv, k_pe, k_pe, gq, gk, gv, og, lr_f, lr_b, lr_pad], axis=-1).astype(BF16)
    wuq = w_uq[l].reshape(MLA_Q_RANK, MLA_HEADS, MLA_NOPE + MLA_ROPE)
    wuq = jnp.concatenate([wuq[:, :, :MLA_NOPE].reshape(MLA_Q_RANK, -1), wuq[:, :, MLA_NOPE:].reshape(MLA_Q_RANK, -1)],
                          axis=-1).astype(BF16)
    wgate = jnp.zeros((LANES, 2 * hk), F32)
    wgate = wgate.at[0:GLA_GATE_RANK, 0:hk].set(w_gate_fwd[l])
    wgate = wgate.at[GLA_GATE_RANK:2 * GLA_GATE_RANK, hk:].set(w_gate_bwd[l])
    wr = jnp.zeros((D_MODEL, LANES), F32)
    wr = wr.at[:, ROUTER_GROUP_LANE0:ROUTER_GROUP_LANE0 + N_GROUPS].set(w_router_group[l])
    wr = wr.at[:, ROUTER_EXPERT_LANE0:ROUTER_EXPERT_LANE0 + N_EXPERTS].set(w_router_expert[l])
    wr_hi = wr.astype(BF16)
    br = jnp.zeros((1, LANES), F32)
    br = br.at[0, ROUTER_GROUP_LANE0:ROUTER_GROUP_LANE0 + N_GROUPS].set(b_router_group[l])
    br = br.at[0, ROUTER_EXPERT_LANE0:ROUTER_EXPERT_LANE0 + N_EXPERTS].set(b_router_expert[l])
    wgu = jnp.concatenate([w_expert_gate[l], w_expert_up[l]], axis=-1).reshape(N_EXPERTS, D_MODEL, 2 * D_EXPERT)
    return {
        'norm_mix': norm_mix[l][None], 'win': win, 'q_a_norm': q_a_norm[l][None], 'wuq': wuq,
        'kv_a_norm': kv_a_norm[l][None], 'wukv': w_ukv[l].astype(BF16),
        'wgate': wgate.astype(BF16), 'bgate': jnp.concatenate([b_gate_fwd[l], b_gate_bwd[l]])[None],
        'gla_norm': gla_norm[l][None], 'wout': w_out[l].astype(BF16), 'norm_ffn': norm_ffn[l][None],
        'wr_hi': wr_hi, 'wr_lo': (wr - wr_hi.astype(F32)).astype(BF16), 'br': br,
        'wgu': wgu.astype(BF16), 'wd': w_expert_down[l].reshape(N_EXPERTS, D_EXPERT, D_MODEL).astype(BF16),
        'norm_final': norm_final[None],
    }


def _meta_streams(meta_tokens, w):
    cos, sin = _rope_tables(jnp.arange(N_META))
    _, k, v, _, gk, gv, gf, _, _ = _inproj(meta_tokens, cos, sin, w, N_META)
    pad_keys = ((0, 0), (0, LANES - N_META), (0, 0))
    front = ((GLA_CHUNK - N_META, 0), (0, 0))
    return (jnp.pad(k, pad_keys), jnp.pad(v, pad_keys), jnp.pad(gk, front), jnp.pad(gv, front), jnp.pad(gf, front))


def _token_mixers(x, meta, w, tm, tq, tk, tb):
    bsz, seq, _ = x.shape
    km, vm, mk, mv, mg = meta
    x2d = x.reshape(bsz * seq, D_MODEL)
    cos, sin = _rope_tables(N_META + jnp.arange(seq))
    q, k, v, gq, gk, gv, gf, gb, og = _inproj(x2d, cos, sin, w, tm)
    a = _attention(q, k, v, km, vm, bsz, seq, tq, tk)
    o_f, o_b = _gla(gq, gk, gv, gf, gb, mk, mv, mg, bsz, seq, tb)
    return _mix(x2d, a, o_f, o_b, og, w, tm)


def kernel(x_prompt, x_sample, meta_tokens, norm_mix, w_in, q_a_norm, w_uq, kv_a_norm, w_ukv, w_gate_fwd, b_gate_fwd, w_gate_bwd, b_gate_bwd, gla_norm, w_out, norm_ffn, w_router_group, b_router_group, w_router_expert, b_router_expert, w_expert_gate, w_expert_up, w_expert_down, norm_final):
    w = _pack_weights(norm_mix, w_in, q_a_norm, w_uq, kv_a_norm, w_ukv, w_gate_fwd, b_gate_fwd, w_gate_bwd,
                      b_gate_bwd, gla_norm, w_out, norm_ffn, w_router_group, b_router_group, w_router_expert,
                      b_router_expert, w_expert_gate, w_expert_up, w_expert_down, norm_final)
    meta = _meta_streams(meta_tokens, w)
    outs = []
    for x in (x_prompt, x_sample):
        h1, hn, route, ids, counts = _token_mixers(x, meta, w, tm=512, tq=512, tk=512, tb=512)
        outs.append(_moe(hn, route, ids, counts, h1, w).reshape(x.shape))
    return tuple(outs)
```

```python
import functools

import numpy as np
import jax
import jax.numpy as jnp
from jax import lax
from jax.experimental import pallas as pl
from jax.experimental.pallas import tpu as pltpu
from jax.experimental.pallas import tpu_sc as plsc

F32 = jnp.float32
BF16 = jnp.bfloat16

D_MODEL = 1024
N_META = 16
MLA_HEADS = 4
MLA_Q_RANK = 384
MLA_KV_RANK = 256
MLA_NOPE = 128
MLA_ROPE = 64
MLA_V = 128
ROPE_THETA = 10000.0
GLA_HEADS = 4
GLA_DK = 64
GLA_DV = 128
GLA_GATE_RANK = 16
GLA_TAU = 16.0
GLA_CHUNK = 64
N_GROUPS = 4
EXPERTS_PER_GROUP = 8
N_EXPERTS = N_GROUPS * EXPERTS_PER_GROUP
D_EXPERT = 256
EPS = 1e-6

LANES = 128
V7X_VMEM_BYTES = 64 * 1024 * 1024
VMEM_LIMIT = V7X_VMEM_BYTES * 7 // 8

ATTN_SCALE = (MLA_NOPE + MLA_ROPE) ** -0.5 * float(np.log2(np.e))
QK_WIDTH = 2 * LANES
V_ROWS = MLA_V + 16

C_CQ = 0
C_CKV = C_CQ + MLA_Q_RANK
C_KPE = C_CKV + MLA_KV_RANK
C_GQ = C_KPE + LANES
C_GK = C_GQ + GLA_HEADS * GLA_DK
C_GV = C_GK + GLA_HEADS * GLA_DK
C_OG = C_GV + GLA_HEADS * GLA_DV
C_LR = C_OG + GLA_HEADS * GLA_DV
D_IN_PACKED = C_LR + LANES

ROUTER_GROUP_LANE0 = 0
ROUTER_EXPERT_LANE0 = N_GROUPS


def _cparams(*semantics):
    return pltpu.CompilerParams(dimension_semantics=semantics, vmem_limit_bytes=VMEM_LIMIT)


def _rms(x, g):
    return x * lax.rsqrt(jnp.mean(x * x, axis=-1, keepdims=True) + EPS) * g


def _dot(a, b):
    return jnp.dot(a, b, preferred_element_type=F32)


def _dot_nt(a, b):
    return lax.dot_general(a, b, (((1,), (1,)), ((), ())), preferred_element_type=F32)


def _dot_tn(a, b):
    return lax.dot_general(a, b, (((0,), (0,)), ((), ())), preferred_element_type=F32)


def _full_spec(shape):
    return pl.BlockSpec(shape, lambda *_: (0,) * len(shape))


SUBLANES = 8
ROW_CHUNKS = D_MODEL // LANES
assert ROW_CHUNKS == SUBLANES


def _rows_to_tiles(x):
    chunks = jnp.stack([x[:, s * LANES:(s + 1) * LANES] for s in range(ROW_CHUNKS)], axis=0)
    return pltpu.einshape("smd->msd", chunks)


def _tiles_to_rows(x):
    chunks = pltpu.einshape("msd->smd", x)
    return jnp.concatenate([chunks[s] for s in range(ROW_CHUNKS)], axis=-1)


def _rope_pairs(x, cos, sin_signed, first_half):
    swapped = jnp.where(first_half, pltpu.roll(x, LANES - MLA_ROPE // 2, 1), pltpu.roll(x, MLA_ROPE // 2, 1))
    return x * cos + swapped * sin_signed


def _inproj_body(x_ref, cos_ref, sin_ref, nmix_ref, win_ref, qan_ref, wuq_ref, kvan_ref, wukv_ref,
                 wgate_ref, bgate_ref,
                 q_ref, k_ref, v_ref, gq_ref, gk_ref, gv_ref, gf_ref, gb_ref, og_ref):
    hn = _rms(x_ref[...], nmix_ref[...]).astype(BF16)

    def proj(lo, hi):
        return _dot(hn, win_ref[:, lo:hi])

    cos = cos_ref[...]
    sin = sin_ref[...]
    lane = lax.broadcasted_iota(jnp.int32, cos.shape, 1)
    first_half = (lane & (MLA_ROPE - 1)) < MLA_ROPE // 2
    low_lanes = lane < MLA_ROPE

    cq = _rms(proj(C_CQ, C_CKV), qan_ref[...]).astype(BF16)
    qn = _dot(cq, wuq_ref[:, 0:MLA_HEADS * MLA_NOPE]) * ATTN_SCALE
    qr = _dot(cq, wuq_ref[:, MLA_HEADS * MLA_NOPE:])
    for j in range(MLA_HEADS // 2):
        rj = (_rope_pairs(qr[:, j * LANES:(j + 1) * LANES], cos, sin, first_half) * ATTN_SCALE).T.astype(BF16)
        for h in (2 * j, 2 * j + 1):
            q_ref[h, 0:LANES, :] = qn[:, h * LANES:(h + 1) * LANES].T.astype(BF16)
            q_ref[h, LANES:QK_WIDTH, :] = rj

    ckv = _rms(proj(C_CKV, C_KPE), kvan_ref[...]).astype(BF16)
    kv = _dot(ckv, wukv_ref[...])
    kr = _rope_pairs(proj(C_KPE, C_GQ), cos, sin, first_half)
    kr_even = jnp.where(low_lanes, kr, 0.0).astype(BF16)
    kr_odd = jnp.where(low_lanes, 0.0, kr).astype(BF16)
    for h in range(MLA_HEADS):
        base = h * (MLA_NOPE + MLA_V)
        k_ref[h, :, 0:LANES] = kv[:, base:base + MLA_NOPE].astype(BF16)
        k_ref[h, :, LANES:QK_WIDTH] = kr_even if h % 2 == 0 else kr_odd
        v_ref[h, 0:MLA_V, :] = kv[:, base + MLA_NOPE:base + MLA_NOPE + MLA_V].T.astype(BF16)
        v_ref[h, MLA_V:V_ROWS, :] = jnp.ones((V_ROWS - MLA_V, kv.shape[0]), BF16)

    gq_ref[...] = proj(C_GQ, C_GK) * (GLA_DK ** -0.5)
    gk_ref[...] = proj(C_GK, C_GV)
    gv_ref[...] = proj(C_GV, C_OG)
    og_ref[...] = proj(C_OG, C_LR)
    pre = _dot(proj(C_LR, D_IN_PACKED).astype(BF16), wgate_ref[...]) + bgate_ref[...]
    logsig = jnp.minimum(pre, 0.0) - jnp.log1p(jnp.exp(-jnp.abs(pre)))
    gates = logsig * (1.0 / GLA_TAU)
    gf_ref[...] = gates[:, 0:GLA_HEADS * GLA_DK]
    gb_ref[...] = gates[:, GLA_HEADS * GLA_DK:]


def _inproj(x2d, cos, sin, w, tm):
    t = x2d.shape[0]
    blocks_per_seq = cos.shape[0] // tm
    hk = GLA_HEADS * GLA_DK
    hv = GLA_HEADS * GLA_DV
    row = lambda width: pl.BlockSpec((tm, width), lambda i: (i, 0))
    head_rows = lambda width: pl.BlockSpec((MLA_HEADS, tm, width), lambda i: (0, i, 0))
    head_cols = lambda height: pl.BlockSpec((MLA_HEADS, height, tm), lambda i: (0, 0, i))
    tab = pl.BlockSpec((tm, LANES), lambda i: (i % blocks_per_seq, 0))
    out_shape = (
        jax.ShapeDtypeStruct((MLA_HEADS, QK_WIDTH, t), BF16),
        jax.ShapeDtypeStruct((MLA_HEADS, t, QK_WIDTH), BF16),
        jax.ShapeDtypeStruct((MLA_HEADS, V_ROWS, t), BF16),
        jax.ShapeDtypeStruct((t, hk), F32),
        jax.ShapeDtypeStruct((t, hk), F32),
        jax.ShapeDtypeStruct((t, hv), F32),
        jax.ShapeDtypeStruct((t, hk), F32),
        jax.ShapeDtypeStruct((t, hk), F32),
        jax.ShapeDtypeStruct((t, hv), F32),
    )
    return pl.pallas_call(
        _inproj_body,
        grid=(t // tm,),
        in_specs=[row(D_MODEL), tab, tab,
                  _full_spec((1, D_MODEL)), _full_spec((D_MODEL, D_IN_PACKED)),
                  _full_spec((1, MLA_Q_RANK)), _full_spec(w['wuq'].shape),
                  _full_spec((1, MLA_KV_RANK)), _full_spec(w['wukv'].shape),
                  _full_spec(w['wgate'].shape), _full_spec(w['bgate'].shape)],
        out_specs=(head_cols(QK_WIDTH), head_rows(QK_WIDTH), head_cols(V_ROWS),
                   row(hk), row(hk), row(hv), row(hk), row(hk), row(hv)),
        out_shape=out_shape,
        compiler_params=_cparams("parallel"),
        name="inproj",
    )(x2d, cos, sin, w['norm_mix'], w['win'], w['q_a_norm'], w['wuq'], w['kv_a_norm'], w['wukv'],
      w['wgate'], w['bgate'])


def _attn_body(q_ref, k_ref, v_ref, km_ref, vm_ref, o_ref, s_ref, acc_ref, *, tk):
    q = q_ref[...]
    n_blocks = k_ref.shape[0] // tk

    def scores(j):
        return _dot(k_ref[j * tk:(j + 1) * tk, :], q)

    def absorb(m, s, v):
        m_new = jnp.maximum(m, jnp.max(s, axis=0, keepdims=True))
        p = jnp.exp2(s - m_new)
        acc_ref[...] = jnp.exp2(m - m_new) * acc_ref[...] + _dot(v, p.astype(BF16))
        return m_new

    s_ref[0] = scores(0)
    sm = _dot(km_ref[...], q)
    sm = jnp.where(lax.broadcasted_iota(jnp.int32, sm.shape, 0) < N_META, sm, -jnp.inf)
    m = jnp.max(sm, axis=0, keepdims=True)
    acc_ref[...] = _dot(vm_ref[...], jnp.exp2(sm - m).astype(BF16))
    for j in range(n_blocks):
        s = s_ref[j % 2]
        if j + 1 < n_blocks:
            s_ref[(j + 1) % 2] = scores(j + 1)
        m = absorb(m, s, v_ref[:, j * tk:(j + 1) * tk])
    acc = acc_ref[...]
    o_ref[...] = (acc[:MLA_V, :] / acc[MLA_V:MLA_V + 1, :]).T.astype(o_ref.dtype)


def _attention(q, k, v, km, vm, bsz, seq, tq, tk):
    nq = seq // tq
    return pl.pallas_call(
        functools.partial(_attn_body, tk=tk),
        grid=(bsz, MLA_HEADS, nq),
        in_specs=[pl.BlockSpec((None, QK_WIDTH, tq), lambda b, h, i: (h, 0, b * nq + i)),
                  pl.BlockSpec((None, seq, QK_WIDTH), lambda b, h, i: (h, b, 0)),
                  pl.BlockSpec((None, V_ROWS, seq), lambda b, h, i: (h, 0, b)),
                  pl.BlockSpec((None, LANES, QK_WIDTH), lambda b, h, i: (h, 0, 0)),
                  pl.BlockSpec((None, V_ROWS, LANES), lambda b, h, i: (h, 0, 0))],
        out_specs=pl.BlockSpec((tq, MLA_V), lambda b, h, i: (b * nq + i, h)),
        out_shape=jax.ShapeDtypeStruct((bsz * seq, MLA_HEADS * MLA_V), BF16),
        scratch_shapes=[pltpu.VMEM((2, tk, tq), F32), pltpu.VMEM((V_ROWS, tq), F32)],
        compiler_params=_cparams("parallel", "parallel", "arbitrary"),
        name="mla_attention",
    )(q, k, v, km, vm)


def _split3(x):
    hi = x.astype(BF16)
    r1 = x - hi.astype(F32)
    mid = r1.astype(BF16)
    lo = (r1 - mid.astype(F32)).astype(BF16)
    return hi, mid, lo


def _gla_log_decay(g, tri):
    g_hi, g_mid, g_lo = _split3(g)
    return _dot(tri, g_hi) + _dot(tri, g_mid) + _dot(tri, g_lo)


def _head_rows(x):
    even = lax.broadcasted_iota(jnp.int32, x.shape, 1) < GLA_DK
    return jnp.concatenate([jnp.where(even, x, 0.0), jnp.where(even, 0.0, x)], axis=0).astype(BF16)


def _gla_operands(q, k, b, mid, last):
    b_last = b[last:last + 1, :]
    ke = (k * jnp.exp(b_last - b)).astype(BF16)
    if q is None:
        return jnp.exp(b_last), ke, None, None, None
    b_mid = b[mid:mid + 1, :]
    ks = (k * jnp.exp(b_mid - b)).astype(BF16)
    return jnp.exp(b_last), ke, ks, _head_rows(q * jnp.exp(b - b_mid)), _head_rows(q * jnp.exp(b))


def _gla_state_update(v_even, v_odd, ke):
    return jnp.where(lax.broadcasted_iota(jnp.int32, (GLA_DV, LANES), 1) < GLA_DK,
                     _dot_tn(v_even, ke), _dot_tn(v_odd, ke))


def _gla_intra(v_even, v_odd, ks, qs2, keep):
    scores = jnp.where(keep, _dot_nt(qs2, ks), 0.0).astype(BF16)
    return _dot(scores[:GLA_CHUNK], v_even), _dot(scores[GLA_CHUNK:], v_odd)


def _gla_body(qf_ref, kf_ref, vf_ref, gf_ref, qb_ref, kb_ref, vb_ref, gb_ref, mk_ref, mv_ref, mg_ref,
              of_ref, ob_ref, state_ref):
    n_chunks = qf_ref.shape[0] // GLA_CHUNK
    n_pairs = GLA_HEADS // 2
    r = lax.broadcasted_iota(jnp.int32, (GLA_CHUNK, GLA_CHUNK), 0)
    c = lax.broadcasted_iota(jnp.int32, (GLA_CHUNK, GLA_CHUNK), 1)
    tri_f = jnp.where(c <= r, 1.0, 0.0).astype(BF16)
    tri_b = jnp.where(c >= r, 1.0, 0.0).astype(BF16)
    r2 = lax.broadcasted_iota(jnp.int32, (2 * GLA_CHUNK, GLA_CHUNK), 0) & (GLA_CHUNK - 1)
    c2 = lax.broadcasted_iota(jnp.int32, (2 * GLA_CHUNK, GLA_CHUNK), 1)
    keep_f = c2 <= r2
    keep_b = c2 >= r2
    mid_f, last_f = GLA_CHUNK // 2 - 1, GLA_CHUNK - 1
    mid_b, last_b = GLA_CHUNK // 2, 0

    def pair_cols(p):
        return slice(p * LANES, (p + 1) * LANES)

    def head_cols(h):
        return slice(h * GLA_DV, (h + 1) * GLA_DV)

    @pl.when(pl.program_id(1) == 0)
    def _():
        for p in range(n_pairs):
            b = _gla_log_decay(mg_ref[:, pair_cols(p)], tri_f)
            _, ke, _, _, _ = _gla_operands(None, mk_ref[:, pair_cols(p)], b, mid_f, last_f)
            state_ref[p] = _gla_state_update(mv_ref[:, head_cols(2 * p)].astype(BF16),
                                             mv_ref[:, head_cols(2 * p + 1)].astype(BF16), ke)
            state_ref[n_pairs + p] = jnp.zeros((GLA_DV, LANES), F32)

    scans = []
    for p in range(n_pairs):
        scans.append((p, list(range(n_chunks)), qf_ref, kf_ref, vf_ref, gf_ref, of_ref, p,
                      tri_f, keep_f, mid_f, last_f))
        scans.append((p, list(reversed(range(n_chunks))), qb_ref, kb_ref, vb_ref, gb_ref, ob_ref, n_pairs + p,
                      tri_b, keep_b, mid_b, last_b))

    def rows(c):
        return slice(c * GLA_CHUNK, (c + 1) * GLA_CHUNK)

    def values(v_ref, p, c):
        return (v_ref[rows(c), head_cols(2 * p)].astype(BF16), v_ref[rows(c), head_cols(2 * p + 1)].astype(BF16))

    log_decay = [[_gla_log_decay(g_ref[rows(c), pair_cols(p)], tri) for c in order]
                 for (p, order, _, _, _, g_ref, _, _, tri, _, _, _) in scans]
    operands = [[_gla_operands(q_ref[rows(c), pair_cols(p)], k_ref[rows(c), pair_cols(p)], b, mid, last)
                 for c, b in zip(order, bs)]
                for (p, order, q_ref, k_ref, _, _, _, _, _, _, mid, last), bs in zip(scans, log_decay)]
    updates = [[_gla_state_update(*values(v_ref, p, c), ops[1]) for c, ops in zip(order, opss)]
               for (p, order, _, _, v_ref, _, _, _, _, _, _, _), opss in zip(scans, operands)]
    intra = [[_gla_intra(*values(v_ref, p, c), ops[2], ops[3], keep) for c, ops in zip(order, opss)]
             for (p, order, _, _, v_ref, _, _, _, _, keep, _, _), opss in zip(scans, operands)]
    states = []
    for (_, order, _, _, _, _, _, slot, _, _, _, _), opss, upds in zip(scans, operands, updates):
        st = state_ref[slot]
        entering = []
        for ops, upd in zip(opss, upds):
            entering.append(st.astype(BF16))
            st = st * ops[0] + upd
        state_ref[slot] = st
        states.append(entering)
    for (p, order, _, _, _, _, o_ref, _, _, _, _, _), opss, sts, locs in zip(scans, operands, states, intra):
        for c, ops, st, (o_even, o_odd) in zip(order, opss, sts, locs):
            inter = _dot_nt(ops[4], st)
            o_ref[rows(c), head_cols(2 * p)] = o_even + inter[:GLA_CHUNK]
            o_ref[rows(c), head_cols(2 * p + 1)] = o_odd + inter[GLA_CHUNK:]


def _gla(gq, gk, gv, gf, gb, mk, mv, mg, bsz, seq, tb):
    nb = seq // tb
    hk = GLA_HEADS * GLA_DK
    hv = GLA_HEADS * GLA_DV
    fwd = lambda width: pl.BlockSpec((tb, width), lambda b, j: (b * nb + j, 0))
    bwd = lambda width: pl.BlockSpec((tb, width), lambda b, j: (b * nb + nb - 1 - j, 0))
    t = bsz * seq
    return pl.pallas_call(
        _gla_body,
        grid=(bsz, nb),
        in_specs=[fwd(hk), fwd(hk), fwd(hv), fwd(hk), bwd(hk), bwd(hk), bwd(hv), bwd(hk),
                  _full_spec(mk.shape), _full_spec(mv.shape), _full_spec(mg.shape)],
        out_specs=(fwd(hv), bwd(hv)),
        out_shape=(jax.ShapeDtypeStruct((t, hv), F32), jax.ShapeDtypeStruct((t, hv), F32)),
        scratch_shapes=[pltpu.VMEM((2 * (GLA_HEADS // 2), GLA_DV, LANES), F32)],
        compiler_params=_cparams("parallel", "arbitrary"),
        name="gla_scan",
    )(gq, gk, gv, gf, gq, gk, gv, gb, mk, mv, mg)


def _mix_body(x_ref, a_ref, of_ref, ob_ref, og_ref, gnorm_ref, wout_ref, nffn_ref, wr_hi_ref, wr_lo_ref, br_ref,
              h1_ref, hn_ref, route_ref, ids_ref, count_ref, tri_ref):
    tm = x_ref.shape[0]

    @pl.when(pl.program_id(0) == 0)
    def _():
        r = lax.broadcasted_iota(jnp.int32, (tm, tm), 0)
        c = lax.broadcasted_iota(jnp.int32, (tm, tm), 1)
        tri_ref[...] = jnp.where(c < r, 1.0, 0.0).astype(BF16)
        count_ref[...] = jnp.zeros_like(count_ref)

    a_width = MLA_HEADS * MLA_V
    h1 = x_ref[...] + _dot(a_ref[...], wout_ref[0:a_width, :])
    for h in range(GLA_HEADS):
        cols = slice(h * GLA_DV, (h + 1) * GLA_DV)
        o = of_ref[:, cols] + ob_ref[:, cols]
        og = og_ref[:, cols]
        silu = og / (1.0 + jnp.exp(-og))
        gh = (_rms(o, gnorm_ref[...]) * silu).astype(BF16)
        h1 = h1 + _dot(gh, wout_ref[a_width + h * GLA_DV:a_width + (h + 1) * GLA_DV, :])
    h1_ref[...] = h1
    hn = _rms(h1, nffn_ref[...])
    hn_ref[...] = _rows_to_tiles(hn)

    hn_hi = hn.astype(BF16)
    hn_lo = (hn - hn_hi.astype(F32)).astype(BF16)
    logits = (_dot(hn_hi, wr_hi_ref[...]) + _dot(hn_lo, wr_hi_ref[...]) + _dot(hn_hi, wr_lo_ref[...])
              + br_ref[...])
    lane = lax.broadcasted_iota(jnp.int32, logits.shape, 1).astype(F32)
    none = float(LANES)
    neg = -jnp.inf

    def lane_max(x):
        return jnp.max(x, axis=-1, keepdims=True)

    def lane_sum(x):
        return jnp.sum(x, axis=-1, keepdims=True)

    def first_lane(mask):
        return jnp.min(jnp.where(mask, lane, none), axis=-1, keepdims=True)

    is_group = lane < float(N_GROUPS)
    g_max = lane_max(jnp.where(is_group, logits, neg))
    g_exp = jnp.where(is_group, jnp.exp(logits - g_max), 0.0)
    g_prob = g_exp / lane_sum(g_exp)
    g_w = lane_max(g_prob)
    g_idx = first_lane(is_group & (g_prob == g_w))
    e_lo = float(ROUTER_EXPERT_LANE0) + float(EXPERTS_PER_GROUP) * g_idx
    sel = (lane >= e_lo) & (lane < e_lo + float(EXPERTS_PER_GROUP))
    e_max = lane_max(jnp.where(sel, logits, neg))
    e_exp = jnp.where(sel, jnp.exp(logits - e_max), 0.0)
    e_prob = e_exp / lane_sum(e_exp)
    p1 = lane_max(jnp.where(sel, e_prob, neg))
    i1 = first_lane(sel & (e_prob == p1))
    rest = sel & (lane != i1)
    p2 = lane_max(jnp.where(rest, e_prob, neg))
    i2 = first_lane(rest & (e_prob == p2))
    denom = p1 + p2
    chosen = jnp.where((lane == i1) | (lane == i2), 1.0, 0.0)
    rank = count_ref[...] + _dot(tri_ref[...], chosen.astype(BF16))
    count_ref[...] += jnp.sum(chosen, axis=0, keepdims=True)
    fields = (i1 - float(ROUTER_EXPERT_LANE0), i2 - float(ROUTER_EXPERT_LANE0),
              lane_sum(jnp.where(lane == i1, rank, 0.0)), lane_sum(jnp.where(lane == i2, rank, 0.0)),
              g_w * (p1 / denom), g_w * (p2 / denom))
    route = jnp.zeros_like(logits)
    for k, value in enumerate(fields):
        route = jnp.where(lane == float(k), value, route)
    route_ref[...] = route
    ids_ref[...] = jnp.transpose(route)[0:ROUTE_ID_ROWS, :].astype(jnp.int32)


ROUTE_E1, ROUTE_E2, ROUTE_RANK1, ROUTE_RANK2, ROUTE_W1, ROUTE_W2 = range(6)
ROUTE_ID_ROWS = 8


def _mix(x2d, a, o_f, o_b, og, w, tm):
    t = x2d.shape[0]
    hv = GLA_HEADS * GLA_DV
    row = lambda width: pl.BlockSpec((tm, width), lambda i: (i, 0))
    return pl.pallas_call(
        _mix_body,
        grid=(t // tm,),
        in_specs=[row(D_MODEL), row(MLA_HEADS * MLA_V), row(hv), row(hv), row(hv),
                  _full_spec((1, GLA_DV)), _full_spec(w['wout'].shape), _full_spec((1, D_MODEL)),
                  _full_spec(w['wr_hi'].shape), _full_spec(w['wr_lo'].shape), _full_spec(w['br'].shape)],
        out_specs=(row(D_MODEL), pl.BlockSpec((tm, ROW_CHUNKS, LANES), lambda i: (i, 0, 0)), row(LANES),
                   pl.BlockSpec((ROUTE_ID_ROWS, tm), lambda i: (0, i)), _full_spec((1, LANES))),
        out_shape=(jax.ShapeDtypeStruct((t, D_MODEL), F32), jax.ShapeDtypeStruct((t, ROW_CHUNKS, LANES), F32),
                   jax.ShapeDtypeStruct((t, LANES), F32), jax.ShapeDtypeStruct((ROUTE_ID_ROWS, t), jnp.int32),
                   jax.ShapeDtypeStruct((1, LANES), F32)),
        scratch_shapes=[pltpu.VMEM((tm, tm), BF16)],
        compiler_params=_cparams("arbitrary"),
        name="mix_router",
    )(x2d, a, o_f, o_b, og, w['gla_norm'], w['wout'], w['norm_ffn'], w['wr_hi'], w['wr_lo'], w['br'])


EXPERT_TILE = 256
ROW_TILE = 256


SC_CORES = 2
SC_SUBCORES = 16
SC_GATHER_ROWS = 32


def _sc_gather(table, idx):
    n = idx.shape[0]
    workers = SC_CORES * SC_SUBCORES
    per_worker = n // workers
    assert n % (workers * SC_GATHER_ROWS) == 0
    mesh = plsc.VectorSubcoreMesh(core_axis_name="c", subcore_axis_name="s")

    @functools.partial(
        pl.kernel, mesh=mesh,
        out_type=jax.ShapeDtypeStruct((n,) + table.shape[1:], table.dtype),
        scratch_types=[pltpu.VMEM((SC_GATHER_ROWS,), jnp.int32),
                       pltpu.VMEM((SC_GATHER_ROWS,) + table.shape[1:], table.dtype),
                       pltpu.SemaphoreType.DMA])
    def gather(table_ref, idx_ref, out_ref, idx_buf, rows_buf, sem):
        base = (lax.axis_index("s") * SC_CORES + lax.axis_index("c")) * per_worker

        @pl.loop(0, per_worker // SC_GATHER_ROWS)
        def _(j):
            rows = pl.ds(base + j * SC_GATHER_ROWS, SC_GATHER_ROWS)
            pltpu.sync_copy(idx_ref.at[rows], idx_buf)
            pltpu.async_copy(table_ref.at[idx_buf], rows_buf, sem).wait()
            pltpu.sync_copy(rows_buf, out_ref.at[rows])

    return gather(table, idx)


def _sc_scatter(rows, idx, n_out):
    copies, n = idx.shape
    idx = idx.reshape(copies * n)
    workers = SC_CORES * SC_SUBCORES
    per_worker = n // workers
    assert n % (workers * SC_GATHER_ROWS) == 0
    mesh = plsc.VectorSubcoreMesh(core_axis_name="c", subcore_axis_name="s")

    @functools.partial(
        pl.kernel, mesh=mesh,
        out_type=jax.ShapeDtypeStruct((n_out,) + rows.shape[1:], rows.dtype),
        scratch_types=[pltpu.VMEM((SC_GATHER_ROWS,), jnp.int32),
                       pltpu.VMEM((SC_GATHER_ROWS,) + rows.shape[1:], rows.dtype),
                       pltpu.SemaphoreType.DMA])
    def scatter(rows_ref, idx_ref, out_ref, idx_buf, rows_buf, sem):
        base = (lax.axis_index("s") * SC_CORES + lax.axis_index("c")) * per_worker

        @pl.loop(0, per_worker // SC_GATHER_ROWS)
        def _(j):
            first = base + j * SC_GATHER_ROWS
            pltpu.sync_copy(rows_ref.at[pl.ds(first, SC_GATHER_ROWS)], rows_buf)
            for k in range(copies):
                pltpu.sync_copy(idx_ref.at[pl.ds(k * n + first, SC_GATHER_ROWS)], idx_buf)
                pltpu.async_copy(rows_buf, out_ref.at[idx_buf], sem).wait()

    return scatter(rows, idx)


SLOT_TILE = 2048


def _slots_body(starts_ref, ids_ref, slots_ref):
    ids = ids_ref[...]
    experts = ids[ROUTE_E1:ROUTE_E1 + 2, :]
    start = jnp.zeros_like(experts)
    for e in range(N_EXPERTS):
        start = jnp.where(experts == e, starts_ref[e], start)
    slots_ref[...] = start + ids[ROUTE_RANK1:ROUTE_RANK1 + 2, :]


def _slots(starts, ids):
    t = ids.shape[1]
    return pl.pallas_call(
        _slots_body,
        grid_spec=pltpu.PrefetchScalarGridSpec(
            num_scalar_prefetch=1,
            grid=(t // SLOT_TILE,),
            in_specs=[pl.BlockSpec((ROUTE_ID_ROWS, SLOT_TILE), lambda i, *_: (0, i))],
            out_specs=pl.BlockSpec((2, SLOT_TILE), lambda i, *_: (0, i))),
        out_shape=jax.ShapeDtypeStruct((2, t), jnp.int32),
        compiler_params=_cparams("parallel"),
        name="moe_slots",
    )(starts, ids)


def _queue_body(slots_ref, token_ref, *, n_tokens):
    i = pl.program_id(0)

    @pl.when(i == 0)
    def _():
        for base in range(0, token_ref.shape[0], n_tokens):
            def clear(j, carry):
                token_ref[base + j] = j
                return carry
            lax.fori_loop(0, min(n_tokens, token_ref.shape[0] - base), clear, 0, unroll=8)

    def place(r, carry):
        for k in range(2):
            token_ref[slots_ref[k, r]] = i * SLOT_TILE + r
        return carry

    lax.fori_loop(0, SLOT_TILE, place, 0, unroll=8)


def _queue_tokens(slots, n_rows):
    t = slots.shape[1]
    return pl.pallas_call(
        functools.partial(_queue_body, n_tokens=t),
        grid=(t // SLOT_TILE,),
        in_specs=[pl.BlockSpec((2, SLOT_TILE), lambda i: (0, i), memory_space=pltpu.SMEM)],
        out_specs=pl.BlockSpec(memory_space=pltpu.SMEM),
        out_shape=jax.ShapeDtypeStruct((n_rows,), jnp.int32),
        compiler_params=_cparams("arbitrary"),
        name="moe_queue",
    )(slots)


def _expert_body(tile_expert_ref, n_tiles_ref, tile_valid_ref, xs_ref, wgu_ref, wd_ref, ys_ref):
    i = pl.program_id(0)
    used = i < n_tiles_ref[0]

    @pl.when(used)
    def _():
        x = _tiles_to_rows(xs_ref[...])
        x = jnp.where(lax.broadcasted_iota(jnp.int32, x.shape, 0) < tile_valid_ref[i], x, 0.0)
        h = _dot(x.astype(BF16), wgu_ref[...])
        hg = h[:, :D_EXPERT]
        act = (hg / (1.0 + jnp.exp(-hg))) * h[:, D_EXPERT:]
        ys_ref[...] = _rows_to_tiles(_dot(act.astype(BF16), wd_ref[...]))

    @pl.when(jnp.logical_not(used))
    def _():
        ys_ref[...] = jnp.zeros_like(ys_ref)


def _experts(tile_expert, n_tiles, tile_valid, xs, w):
    n_grid = xs.shape[0] // EXPERT_TILE

    def tile(i, n_tiles):
        return jnp.minimum(i, n_tiles[0] - 1)

    tiles = lambda index: pl.BlockSpec((EXPERT_TILE, ROW_CHUNKS, LANES), index)
    return pl.pallas_call(
        _expert_body,
        grid_spec=pltpu.PrefetchScalarGridSpec(
            num_scalar_prefetch=3,
            grid=(n_grid,),
            in_specs=[tiles(lambda i, te, nt, tv: (tile(i, nt), 0, 0)),
                      pl.BlockSpec((None, D_MODEL, 2 * D_EXPERT), lambda i, te, nt, tv: (te[tile(i, nt)], 0, 0)),
                      pl.BlockSpec((None, D_EXPERT, D_MODEL), lambda i, te, nt, tv: (te[tile(i, nt)], 0, 0))],
            out_specs=tiles(lambda i, te, nt, tv: (i, 0, 0))),
        out_shape=jax.ShapeDtypeStruct(xs.shape, F32),
        compiler_params=_cparams("arbitrary"),
        name="moe_experts",
    )(tile_expert, n_tiles, tile_valid, xs, w['wgu'], w['wd'])


def _combine_body(h1_ref, route_ref, nfin_ref, y1_ref, y2_ref, out_ref):
    route = route_ref[...]
    lane = lax.broadcasted_iota(jnp.int32, route.shape, 1)
    w1 = jnp.sum(jnp.where(lane == ROUTE_W1, route, 0.0), axis=-1, keepdims=True)
    w2 = jnp.sum(jnp.where(lane == ROUTE_W2, route, 0.0), axis=-1, keepdims=True)
    y = w1 * _tiles_to_rows(y1_ref[...]) + w2 * _tiles_to_rows(y2_ref[...])
    out_ref[...] = _rms(h1_ref[...] + y, nfin_ref[...])


def _combine(h1, route, y12, w):
    t = h1.shape[0]
    row = lambda width: pl.BlockSpec((ROW_TILE, width), lambda i: (i, 0))
    tiles = lambda k: pl.BlockSpec((None, ROW_TILE, ROW_CHUNKS, LANES), lambda i: (k, i, 0, 0))
    return pl.pallas_call(
        _combine_body,
        grid=(t // ROW_TILE,),
        in_specs=[row(D_MODEL), row(LANES), _full_spec((1, D_MODEL)), tiles(0), tiles(1)],
        out_specs=row(D_MODEL),
        out_shape=jax.ShapeDtypeStruct((t, D_MODEL), F32),
        compiler_params=_cparams("parallel"),
        name="moe_combine",
    )(h1, route, w['norm_final'], y12, y12)


def _moe(hn, route, ids, counts, h1, w):
    t = hn.shape[0]
    n_rows = 2 * t + N_EXPERTS * EXPERT_TILE
    n_grid_tiles = n_rows // EXPERT_TILE
    count = counts[0, ROUTER_EXPERT_LANE0:ROUTER_EXPERT_LANE0 + N_EXPERTS].astype(jnp.int32)
    padded = jnp.maximum((count + EXPERT_TILE - 1) // EXPERT_TILE, 1) * EXPERT_TILE
    ends = jnp.sum(jnp.where(jnp.arange(N_EXPERTS)[:, None] <= jnp.arange(N_EXPERTS)[None, :], padded[:, None], 0),
                   axis=0)
    starts = ends - padded
    n_tiles = (ends[-1:] // EXPERT_TILE)
    tile_rows = jnp.arange(n_grid_tiles, dtype=jnp.int32) * EXPERT_TILE
    tile_expert = jnp.minimum(jnp.sum((ends[None, :] <= tile_rows[:, None]).astype(jnp.int32), axis=1), N_EXPERTS - 1)
    tile_valid = jnp.clip(jnp.sum(jnp.where(jnp.arange(N_EXPERTS)[None, :] == tile_expert[:, None],
                                            (starts + count)[None, :], 0), axis=1) - tile_rows, 0, EXPERT_TILE)
    slots = _slots(starts, ids)
    ys = _experts(tile_expert, n_tiles, tile_valid, _sc_scatter(hn, slots, n_rows), w)
    y12 = _sc_gather(ys, slots.reshape(2 * t)).reshape(2, t, ROW_CHUNKS, LANES)
    return _combine(h1, route, y12, w)


def _rope_tables(positions):
    inv_freq = 1.0 / (ROPE_THETA ** (jnp.arange(0, MLA_ROPE, 2, dtype=F32) / MLA_ROPE))
    ang = positions.astype(F32)[:, None] * inv_freq[None, :]
    cos, sin = jnp.cos(ang), jnp.sin(ang)
    reps = LANES // MLA_ROPE
    return jnp.tile(jnp.concatenate([cos, cos], axis=-1), (1, reps)), jnp.tile(jnp.concatenate([-sin, sin], axis=-1), (1, reps))


def _pack_weights(norm_mix, w_in, q_a_norm, w_uq, kv_a_norm, w_ukv, w_gate_fwd, b_gate_fwd, w_gate_bwd, b_gate_bwd,
                  gla_norm, w_out, norm_ffn, w_router_group, b_router_group, w_router_expert, b_router_expert,
                  w_expert_gate, w_expert_up, w_expert_down, norm_final):
    l = 0
    hk = GLA_HEADS * GLA_DK
    hv = GLA_HEADS * GLA_DV
    c_q, c_kv, k_pe, gq, gk, gv, lr_f, lr_b, og = jnp.split(
        w_in[l], np.cumsum([MLA_Q_RANK, MLA_KV_RANK, MLA_ROPE, hk, hk, hv, GLA_GATE_RANK, GLA_GATE_RANK])[:].tolist(),
        axis=-1)
    lr_pad = jnp.zeros((D_MODEL, LANES - 2 * GLA_GATE_RANK), F32)
    win = jnp.concatenate([c_q, c_kv, k_pe, k_pe, gq, gk, gv, og, lr_f, lr_b, lr_pad], axis=-1).astype(BF16)
    wuq = w_uq[l].reshape(MLA_Q_RANK, MLA_HEADS, MLA_NOPE + MLA_ROPE)
    wuq = jnp.concatenate([wuq[:, :, :MLA_NOPE].reshape(MLA_Q_RANK, -1), wuq[:, :, MLA_NOPE:].reshape(MLA_Q_RANK, -1)],
                          axis=-1).astype(BF16)
    wgate = jnp.zeros((LANES, 2 * hk), F32)
    wgate = wgate.at[0:GLA_GATE_RANK, 0:hk].set(w_gate_fwd[l])
    wgate = wgate.at[GLA_GATE_RANK:2 * GLA_GATE_RANK, hk:].set(w_gate_bwd[l])
    wr = jnp.zeros((D_MODEL, LANES), F32)
    wr = wr.at[:, ROUTER_GROUP_LANE0:ROUTER_GROUP_LANE0 + N_GROUPS].set(w_router_group[l])
    wr = wr.at[:, ROUTER_EXPERT_LANE0:ROUTER_EXPERT_LANE0 + N_EXPERTS].set(w_router_expert[l])
    wr_hi = wr.astype(BF16)
    br = jnp.zeros((1, LANES), F32)
    br = br.at[0, ROUTER_GROUP_LANE0:ROUTER_GROUP_LANE0 + N_GROUPS].set(b_router_group[l])
    br = br.at[0, ROUTER_EXPERT_LANE0:ROUTER_EXPERT_LANE0 + N_EXPERTS].set(b_router_expert[l])
    wgu = jnp.concatenate([w_expert_gate[l], w_expert_up[l]], axis=-1).reshape(N_EXPERTS, D_MODEL, 2 * D_EXPERT)
    return {
        'norm_mix': norm_mix[l][None], 'win': win, 'q_a_norm': q_a_norm[l][None], 'wuq': wuq,
        'kv_a_norm': kv_a_norm[l][None], 'wukv': w_ukv[l].astype(BF16),
        'wgate': wgate.astype(BF16), 'bgate': jnp.concatenate([b_gate_fwd[l], b_gate_bwd[l]])[None],
        'gla_norm': gla_norm[l][None], 'wout': w_out[l].astype(BF16), 'norm_ffn': norm_ffn[l][None],
        'wr_hi': wr_hi, 'wr_lo': (wr - wr_hi.astype(F32)).astype(BF16), 'br': br,
        'wgu': wgu.astype(BF16), 'wd': w_expert_down[l].reshape(N_EXPERTS, D_EXPERT, D_MODEL).astype(BF16),
        'norm_final': norm_final[None],
    }


def _meta_streams(meta_tokens, w):
    cos, sin = _rope_tables(jnp.arange(N_META))
    _, k, v, _, gk, gv, gf, _, _ = _inproj(meta_tokens, cos, sin, w, N_META)
    pad_keys = ((0, 0), (0, LANES - N_META), (0, 0))
    front = ((GLA_CHUNK - N_META, 0), (0, 0))
    pad_values = ((0, 0), (0, 0), (0, LANES - N_META))
    return (jnp.pad(k, pad_keys), jnp.pad(v, pad_values), jnp.pad(gk, front), jnp.pad(gv, front), jnp.pad(gf, front))


def _token_mixers(x, meta, w, tm, tq, tk, tb):
    bsz, seq, _ = x.shape
    km, vm, mk, mv, mg = meta
    x2d = x.reshape(bsz * seq, D_MODEL)
    cos, sin = _rope_tables(N_META + jnp.arange(seq))
    q, k, v, gq, gk, gv, gf, gb, og = _inproj(x2d, cos, sin, w, tm)
    a = _attention(q, k, v, km, vm, bsz, seq, tq, tk)
    o_f, o_b = _gla(gq, gk, gv, gf, gb, mk, mv, mg, bsz, seq, tb)
    return _mix(x2d, a, o_f, o_b, og, w, tm)


def kernel(x_prompt, x_sample, meta_tokens, norm_mix, w_in, q_a_norm, w_uq, kv_a_norm, w_ukv, w_gate_fwd, b_gate_fwd, w_gate_bwd, b_gate_bwd, gla_norm, w_out, norm_ffn, w_router_group, b_router_group, w_router_expert, b_router_expert, w_expert_gate, w_expert_up, w_expert_down, norm_final):
    w = _pack_weights(norm_mix, w_in, q_a_norm, w_uq, kv_a_norm, w_ukv, w_gate_fwd, b_gate_fwd, w_gate_bwd,
                      b_gate_bwd, gla_norm, w_out, norm_ffn, w_router_group, b_router_group, w_router_expert,
                      b_router_expert, w_expert_gate, w_expert_up, w_expert_down, norm_final)
    meta = _meta_streams(meta_tokens, w)
    outs = []
    for x in (x_prompt, x_sample):
        h1, hn, route, ids, counts = _token_mixers(x, meta, w, tm=512, tq=512, tk=512, tb=512)
        outs.append(_moe(hn, route, ids, counts, h1, w).reshape(x.shape))
    return tuple(outs)
```

```python
import functools

import numpy as np
import jax
import jax.numpy as jnp
from jax import lax
from jax.experimental import pallas as pl
from jax.experimental.pallas import tpu as pltpu
from jax.experimental.pallas import tpu_sc as plsc

F32 = jnp.float32
BF16 = jnp.bfloat16

D_MODEL = 1024
N_META = 16
MLA_HEADS = 4
MLA_Q_RANK = 384
MLA_KV_RANK = 256
MLA_NOPE = 128
MLA_ROPE = 64
MLA_V = 128
ROPE_THETA = 10000.0
GLA_HEADS = 4
GLA_DK = 64
GLA_DV = 128
GLA_GATE_RANK = 16
GLA_TAU = 16.0
GLA_CHUNK = 64
N_GROUPS = 4
EXPERTS_PER_GROUP = 8
N_EXPERTS = N_GROUPS * EXPERTS_PER_GROUP
D_EXPERT = 256
EPS = 1e-6

LANES = 128
V7X_VMEM_BYTES = 64 * 1024 * 1024
VMEM_LIMIT = V7X_VMEM_BYTES * 7 // 8

ATTN_SCALE = (MLA_NOPE + MLA_ROPE) ** -0.5 * float(np.log2(np.e))
QK_WIDTH = 2 * LANES
V_WIDTH = 2 * LANES
ATTN_GROUP = 8

C_CQ = 0
C_CKV = C_CQ + MLA_Q_RANK
C_KPE = C_CKV + MLA_KV_RANK
C_GQ = C_KPE + LANES
C_GK = C_GQ + GLA_HEADS * GLA_DK
C_GV = C_GK + GLA_HEADS * GLA_DK
C_OG = C_GV + GLA_HEADS * GLA_DV
C_LR = C_OG + GLA_HEADS * GLA_DV
D_IN_PACKED = C_LR + LANES

ROUTER_GROUP_LANE0 = 0
ROUTER_EXPERT_LANE0 = N_GROUPS


def _cparams(*semantics):
    return pltpu.CompilerParams(dimension_semantics=semantics, vmem_limit_bytes=VMEM_LIMIT)


def _rms(x, g):
    return x * lax.rsqrt(jnp.mean(x * x, axis=-1, keepdims=True) + EPS) * g


def _dot(a, b):
    return jnp.dot(a, b, preferred_element_type=F32)


def _dot_nt(a, b):
    return lax.dot_general(a, b, (((1,), (1,)), ((), ())), preferred_element_type=F32)


def _dot_tn(a, b):
    return lax.dot_general(a, b, (((0,), (0,)), ((), ())), preferred_element_type=F32)


def _full_spec(shape):
    return pl.BlockSpec(shape, lambda *_: (0,) * len(shape))


SUBLANES = 8
ROW_CHUNKS = D_MODEL // LANES
assert ROW_CHUNKS == SUBLANES


def _rows_to_tiles(x):
    chunks = jnp.stack([x[:, s * LANES:(s + 1) * LANES] for s in range(ROW_CHUNKS)], axis=0)
    return pltpu.einshape("smd->msd", chunks)


def _tiles_to_rows(x):
    chunks = pltpu.einshape("msd->smd", x)
    return jnp.concatenate([chunks[s] for s in range(ROW_CHUNKS)], axis=-1)


def _rope_pairs(x, cos, sin_signed, first_half):
    swapped = jnp.where(first_half, pltpu.roll(x, LANES - MLA_ROPE // 2, 1), pltpu.roll(x, MLA_ROPE // 2, 1))
    return x * cos + swapped * sin_signed


def _inproj_body(x_ref, cos_ref, sin_ref, nmix_ref, win_ref, qan_ref, wuq_ref, kvan_ref, wukv_ref,
                 wgate_ref, bgate_ref,
                 q_ref, k_ref, v_ref, gq_ref, gk_ref, gv_ref, gf_ref, gb_ref, og_ref):
    hn = _rms(x_ref[...], nmix_ref[...]).astype(BF16)

    def proj(lo, hi):
        return _dot(hn, win_ref[:, lo:hi])

    cos = cos_ref[...]
    sin = sin_ref[...]
    lane = lax.broadcasted_iota(jnp.int32, cos.shape, 1)
    first_half = (lane & (MLA_ROPE - 1)) < MLA_ROPE // 2
    low_lanes = lane < MLA_ROPE

    cq = _rms(proj(C_CQ, C_CKV), qan_ref[...]).astype(BF16)
    qn = _dot(cq, wuq_ref[:, 0:MLA_HEADS * MLA_NOPE]) * ATTN_SCALE
    qr = _dot(cq, wuq_ref[:, MLA_HEADS * MLA_NOPE:])
    for j in range(MLA_HEADS // 2):
        rj = (_rope_pairs(qr[:, j * LANES:(j + 1) * LANES], cos, sin, first_half) * ATTN_SCALE).astype(BF16)
        for h in (2 * j, 2 * j + 1):
            q_ref[h, :, 0:LANES] = qn[:, h * LANES:(h + 1) * LANES].astype(BF16)
            q_ref[h, :, LANES:QK_WIDTH] = rj

    ckv = _rms(proj(C_CKV, C_KPE), kvan_ref[...]).astype(BF16)
    kv = _dot(ckv, wukv_ref[...])
    kr = _rope_pairs(proj(C_KPE, C_GQ), cos, sin, first_half)
    kr_even = jnp.where(low_lanes, kr, 0.0).astype(BF16)
    kr_odd = jnp.where(low_lanes, 0.0, kr).astype(BF16)
    for h in range(MLA_HEADS):
        base = h * (MLA_NOPE + MLA_V)
        k_ref[h, :, 0:LANES] = kv[:, base:base + MLA_NOPE].astype(BF16)
        k_ref[h, :, LANES:QK_WIDTH] = kr_even if h % 2 == 0 else kr_odd
        v_ref[h, :, 0:MLA_V] = kv[:, base + MLA_NOPE:base + MLA_NOPE + MLA_V].astype(BF16)
        v_ref[h, :, MLA_V:V_WIDTH] = jnp.ones((kv.shape[0], V_WIDTH - MLA_V), BF16)

    gq_ref[...] = proj(C_GQ, C_GK) * (GLA_DK ** -0.5)
    gk_ref[...] = proj(C_GK, C_GV)
    gv_ref[...] = proj(C_GV, C_OG)
    og_ref[...] = proj(C_OG, C_LR)
    pre = _dot(proj(C_LR, D_IN_PACKED).astype(BF16), wgate_ref[...]) + bgate_ref[...]
    logsig = jnp.minimum(pre, 0.0) - jnp.log1p(jnp.exp(-jnp.abs(pre)))
    gates = logsig * (1.0 / GLA_TAU)
    gf_ref[...] = gates[:, 0:GLA_HEADS * GLA_DK]
    gb_ref[...] = gates[:, GLA_HEADS * GLA_DK:]


def _inproj(x2d, cos, sin, w, tm):
    t = x2d.shape[0]
    blocks_per_seq = cos.shape[0] // tm
    hk = GLA_HEADS * GLA_DK
    hv = GLA_HEADS * GLA_DV
    row = lambda width: pl.BlockSpec((tm, width), lambda i: (i, 0))
    head_rows = lambda width: pl.BlockSpec((MLA_HEADS, tm, width), lambda i: (0, i, 0))
    tab = pl.BlockSpec((tm, LANES), lambda i: (i % blocks_per_seq, 0))
    out_shape = (
        jax.ShapeDtypeStruct((MLA_HEADS, t, QK_WIDTH), BF16),
        jax.ShapeDtypeStruct((MLA_HEADS, t, QK_WIDTH), BF16),
        jax.ShapeDtypeStruct((MLA_HEADS, t, V_WIDTH), BF16),
        jax.ShapeDtypeStruct((t, hk), F32),
        jax.ShapeDtypeStruct((t, hk), F32),
        jax.ShapeDtypeStruct((t, hv), F32),
        jax.ShapeDtypeStruct((t, hk), F32),
        jax.ShapeDtypeStruct((t, hk), F32),
        jax.ShapeDtypeStruct((t, hv), F32),
    )
    return pl.pallas_call(
        _inproj_body,
        grid=(t // tm,),
        in_specs=[row(D_MODEL), tab, tab,
                  _full_spec((1, D_MODEL)), _full_spec((D_MODEL, D_IN_PACKED)),
                  _full_spec((1, MLA_Q_RANK)), _full_spec(w['wuq'].shape),
                  _full_spec((1, MLA_KV_RANK)), _full_spec(w['wukv'].shape),
                  _full_spec(w['wgate'].shape), _full_spec(w['bgate'].shape)],
        out_specs=(head_rows(QK_WIDTH), head_rows(QK_WIDTH), head_rows(V_WIDTH),
                   row(hk), row(hk), row(hv), row(hk), row(hk), row(hv)),
        out_shape=out_shape,
        compiler_params=_cparams("parallel"),
        name="inproj",
    )(x2d, cos, sin, w['norm_mix'], w['win'], w['q_a_norm'], w['wuq'], w['kv_a_norm'], w['wukv'],
      w['wgate'], w['bgate'])


def _attn_body(q_ref, k_ref, v_ref, km_ref, vm_ref, o_ref, s_ref, acc_ref, *, tk):
    q = q_ref[...]
    n_groups = k_ref.shape[0] // (ATTN_GROUP * tk)

    def scores(j):
        return _dot_nt(q, k_ref[pl.ds(pl.multiple_of(j * tk, tk), tk), :])

    def values(j):
        return v_ref[pl.ds(pl.multiple_of(j * tk, tk), tk), :]

    def absorb(m, s, v):
        m_new = jnp.maximum(m, jnp.max(s, axis=-1, keepdims=True))
        p = jnp.exp2(s - m_new)
        acc_ref[...] = jnp.exp2(m - m_new) * acc_ref[...] + _dot(p.astype(BF16), v)
        return m_new

    s_ref[0] = scores(0)
    sm = _dot_nt(q, km_ref[...])
    sm = jnp.where(lax.broadcasted_iota(jnp.int32, sm.shape, 1) < N_META, sm, -jnp.inf)
    m = jnp.max(sm, axis=-1, keepdims=True)
    acc_ref[...] = _dot(jnp.exp2(sm - m).astype(BF16), vm_ref[...])

    def group(g, m, last):
        for i in range(ATTN_GROUP):
            j = ATTN_GROUP * g + i
            s = s_ref[i % 2]
            if not (last and i == ATTN_GROUP - 1):
                s_ref[(i + 1) % 2] = scores(j + 1)
            m = absorb(m, s, values(j))
        return m

    m = lax.fori_loop(0, n_groups - 1, lambda g, m: group(g, m, False), m)
    group(n_groups - 1, m, True)
    acc = acc_ref[...]
    o_ref[...] = (acc[:, :MLA_V] / acc[:, MLA_V:]).astype(o_ref.dtype)


def _attention(q, k, v, km, vm, bsz, seq, tq, tk):
    nq = seq // tq
    return pl.pallas_call(
        functools.partial(_attn_body, tk=tk),
        grid=(bsz, MLA_HEADS, nq),
        in_specs=[pl.BlockSpec((None, tq, QK_WIDTH), lambda b, h, i: (h, b * nq + i, 0)),
                  pl.BlockSpec((None, seq, QK_WIDTH), lambda b, h, i: (h, b, 0)),
                  pl.BlockSpec((None, seq, V_WIDTH), lambda b, h, i: (h, b, 0)),
                  pl.BlockSpec((None, LANES, QK_WIDTH), lambda b, h, i: (h, 0, 0)),
                  pl.BlockSpec((None, LANES, V_WIDTH), lambda b, h, i: (h, 0, 0))],
        out_specs=pl.BlockSpec((tq, MLA_V), lambda b, h, i: (b * nq + i, h)),
        out_shape=jax.ShapeDtypeStruct((bsz * seq, MLA_HEADS * MLA_V), BF16),
        scratch_shapes=[pltpu.VMEM((2, tq, tk), F32), pltpu.VMEM((tq, V_WIDTH), F32)],
        compiler_params=_cparams("parallel", "parallel", "arbitrary"),
        name="mla_attention",
    )(q, k, v, km, vm)


def _split3(x):
    hi = x.astype(BF16)
    r1 = x - hi.astype(F32)
    mid = r1.astype(BF16)
    lo = (r1 - mid.astype(F32)).astype(BF16)
    return hi, mid, lo


def _gla_log_decay(g, tri):
    g_hi, g_mid, g_lo = _split3(g)
    return _dot(tri, g_hi) + _dot(tri, g_mid) + _dot(tri, g_lo)


def _head_rows(x):
    even = lax.broadcasted_iota(jnp.int32, x.shape, 1) < GLA_DK
    return jnp.concatenate([jnp.where(even, x, 0.0), jnp.where(even, 0.0, x)], axis=0).astype(BF16)


def _gla_operands(q, k, b, mid, last):
    b_last = b[last:last + 1, :]
    ke = (k * jnp.exp(b_last - b)).astype(BF16)
    if q is None:
        return jnp.exp(b_last), ke, None, None, None
    b_mid = b[mid:mid + 1, :]
    ks = (k * jnp.exp(b_mid - b)).astype(BF16)
    return jnp.exp(b_last), ke, ks, _head_rows(q * jnp.exp(b - b_mid)), _head_rows(q * jnp.exp(b))


def _gla_state_update(v_even, v_odd, ke):
    return jnp.where(lax.broadcasted_iota(jnp.int32, (GLA_DV, LANES), 1) < GLA_DK,
                     _dot_tn(v_even, ke), _dot_tn(v_odd, ke))


def _gla_intra(v_even, v_odd, ks, qs2, keep):
    scores = jnp.where(keep, _dot_nt(qs2, ks), 0.0).astype(BF16)
    return _dot(scores[:GLA_CHUNK], v_even), _dot(scores[GLA_CHUNK:], v_odd)


def _gla_body(qf_ref, kf_ref, vf_ref, gf_ref, qb_ref, kb_ref, vb_ref, gb_ref, mk_ref, mv_ref, mg_ref,
              of_ref, ob_ref, state_ref):
    n_chunks = qf_ref.shape[0] // GLA_CHUNK
    n_pairs = GLA_HEADS // 2
    r = lax.broadcasted_iota(jnp.int32, (GLA_CHUNK, GLA_CHUNK), 0)
    c = lax.broadcasted_iota(jnp.int32, (GLA_CHUNK, GLA_CHUNK), 1)
    tri_f = jnp.where(c <= r, 1.0, 0.0).astype(BF16)
    tri_b = jnp.where(c >= r, 1.0, 0.0).astype(BF16)
    r2 = lax.broadcasted_iota(jnp.int32, (2 * GLA_CHUNK, GLA_CHUNK), 0) & (GLA_CHUNK - 1)
    c2 = lax.broadcasted_iota(jnp.int32, (2 * GLA_CHUNK, GLA_CHUNK), 1)
    keep_f = c2 <= r2
    keep_b = c2 >= r2
    mid_f, last_f = GLA_CHUNK // 2 - 1, GLA_CHUNK - 1
    mid_b, last_b = GLA_CHUNK // 2, 0

    def pair_cols(p):
        return slice(p * LANES, (p + 1) * LANES)

    def head_cols(h):
        return slice(h * GLA_DV, (h + 1) * GLA_DV)

    @pl.when(pl.program_id(1) == 0)
    def _():
        for p in range(n_pairs):
            b = _gla_log_decay(mg_ref[:, pair_cols(p)], tri_f)
            _, ke, _, _, _ = _gla_operands(None, mk_ref[:, pair_cols(p)], b, mid_f, last_f)
            state_ref[p] = _gla_state_update(mv_ref[:, head_cols(2 * p)].astype(BF16),
                                             mv_ref[:, head_cols(2 * p + 1)].astype(BF16), ke)
            state_ref[n_pairs + p] = jnp.zeros((GLA_DV, LANES), F32)

    scans = []
    for p in range(n_pairs):
        scans.append((p, list(range(n_chunks)), qf_ref, kf_ref, vf_ref, gf_ref, of_ref, p,
                      tri_f, keep_f, mid_f, last_f))
        scans.append((p, list(reversed(range(n_chunks))), qb_ref, kb_ref, vb_ref, gb_ref, ob_ref, n_pairs + p,
                      tri_b, keep_b, mid_b, last_b))

    def rows(c):
        return slice(c * GLA_CHUNK, (c + 1) * GLA_CHUNK)

    def values(v_ref, p, c):
        return (v_ref[rows(c), head_cols(2 * p)].astype(BF16), v_ref[rows(c), head_cols(2 * p + 1)].astype(BF16))

    log_decay = [[_gla_log_decay(g_ref[rows(c), pair_cols(p)], tri) for c in order]
                 for (p, order, _, _, _, g_ref, _, _, tri, _, _, _) in scans]
    operands = [[_gla_operands(q_ref[rows(c), pair_cols(p)], k_ref[rows(c), pair_cols(p)], b, mid, last)
                 for c, b in zip(order, bs)]
                for (p, order, q_ref, k_ref, _, _, _, _, _, _, mid, last), bs in zip(scans, log_decay)]
    updates = [[_gla_state_update(*values(v_ref, p, c), ops[1]) for c, ops in zip(order, opss)]
               for (p, order, _, _, v_ref, _, _, _, _, _, _, _), opss in zip(scans, operands)]
    intra = [[_gla_intra(*values(v_ref, p, c), ops[2], ops[3], keep) for c, ops in zip(order, opss)]
             for (p, order, _, _, v_ref, _, _, _, _, keep, _, _), opss in zip(scans, operands)]
    states = []
    for (_, order, _, _, _, _, _, slot, _, _, _, _), opss, upds in zip(scans, operands, updates):
        st = state_ref[slot]
        entering = []
        for ops, upd in zip(opss, upds):
            entering.append(st.astype(BF16))
            st = st * ops[0] + upd
        state_ref[slot] = st
        states.append(entering)
    for (p, order, _, _, _, _, o_ref, _, _, _, _, _), opss, sts, locs in zip(scans, operands, states, intra):
        for c, ops, st, (o_even, o_odd) in zip(order, opss, sts, locs):
            inter = _dot_nt(ops[4], st)
            o_ref[rows(c), head_cols(2 * p)] = o_even + inter[:GLA_CHUNK]
            o_ref[rows(c), head_cols(2 * p + 1)] = o_odd + inter[GLA_CHUNK:]


def _gla(gq, gk, gv, gf, gb, mk, mv, mg, bsz, seq, tb):
    nb = seq // tb
    hk = GLA_HEADS * GLA_DK
    hv = GLA_HEADS * GLA_DV
    fwd = lambda width: pl.BlockSpec((tb, width), lambda b, j: (b * nb + j, 0))
    bwd = lambda width: pl.BlockSpec((tb, width), lambda b, j: (b * nb + nb - 1 - j, 0))
    t = bsz * seq
    return pl.pallas_call(
        _gla_body,
        grid=(bsz, nb),
        in_specs=[fwd(hk), fwd(hk), fwd(hv), fwd(hk), bwd(hk), bwd(hk), bwd(hv), bwd(hk),
                  _full_spec(mk.shape), _full_spec(mv.shape), _full_spec(mg.shape)],
        out_specs=(fwd(hv), bwd(hv)),
        out_shape=(jax.ShapeDtypeStruct((t, hv), F32), jax.ShapeDtypeStruct((t, hv), F32)),
        scratch_shapes=[pltpu.VMEM((2 * (GLA_HEADS // 2), GLA_DV, LANES), F32)],
        compiler_params=_cparams("parallel", "arbitrary"),
        name="gla_scan",
    )(gq, gk, gv, gf, gq, gk, gv, gb, mk, mv, mg)


def _mix_body(x_ref, a_ref, of_ref, ob_ref, og_ref, gnorm_ref, wout_ref, nffn_ref, wr_hi_ref, wr_lo_ref, br_ref,
              h1_ref, hn_ref, route_ref, ids_ref, count_ref, tri_ref):
    tm = x_ref.shape[0]

    @pl.when(pl.program_id(0) == 0)
    def _():
        r = lax.broadcasted_iota(jnp.int32, (tm, tm), 0)
        c = lax.broadcasted_iota(jnp.int32, (tm, tm), 1)
        tri_ref[...] = jnp.where(c < r, 1.0, 0.0).astype(BF16)
        count_ref[...] = jnp.zeros_like(count_ref)

    a_width = MLA_HEADS * MLA_V
    h1 = x_ref[...] + _dot(a_ref[...], wout_ref[0:a_width, :])
    for h in range(GLA_HEADS):
        cols = slice(h * GLA_DV, (h + 1) * GLA_DV)
        o = of_ref[:, cols] + ob_ref[:, cols]
        og = og_ref[:, cols]
        silu = og / (1.0 + jnp.exp(-og))
        gh = (_rms(o, gnorm_ref[...]) * silu).astype(BF16)
        h1 = h1 + _dot(gh, wout_ref[a_width + h * GLA_DV:a_width + (h + 1) * GLA_DV, :])
    h1_ref[...] = h1
    hn = _rms(h1, nffn_ref[...])
    hn_ref[...] = _rows_to_tiles(hn)

    hn_hi = hn.astype(BF16)
    hn_lo = (hn - hn_hi.astype(F32)).astype(BF16)
    logits = (_dot(hn_hi, wr_hi_ref[...]) + _dot(hn_lo, wr_hi_ref[...]) + _dot(hn_hi, wr_lo_ref[...])
              + br_ref[...])
    lane = lax.broadcasted_iota(jnp.int32, logits.shape, 1).astype(F32)
    none = float(LANES)
    neg = -jnp.inf

    def lane_max(x):
        return jnp.max(x, axis=-1, keepdims=True)

    def lane_sum(x):
        return jnp.sum(x, axis=-1, keepdims=True)

    def first_lane(mask):
        return jnp.min(jnp.where(mask, lane, none), axis=-1, keepdims=True)

    is_group = lane < float(N_GROUPS)
    g_max = lane_max(jnp.where(is_group, logits, neg))
    g_exp = jnp.where(is_group, jnp.exp(logits - g_max), 0.0)
    g_prob = g_exp / lane_sum(g_exp)
    g_w = lane_max(g_prob)
    g_idx = first_lane(is_group & (g_prob == g_w))
    e_lo = float(ROUTER_EXPERT_LANE0) + float(EXPERTS_PER_GROUP) * g_idx
    sel = (lane >= e_lo) & (lane < e_lo + float(EXPERTS_PER_GROUP))
    e_max = lane_max(jnp.where(sel, logits, neg))
    e_exp = jnp.where(sel, jnp.exp(logits - e_max), 0.0)
    e_prob = e_exp / lane_sum(e_exp)
    p1 = lane_max(jnp.where(sel, e_prob, neg))
    i1 = first_lane(sel & (e_prob == p1))
    rest = sel & (lane != i1)
    p2 = lane_max(jnp.where(rest, e_prob, neg))
    i2 = first_lane(rest & (e_prob == p2))
    denom = p1 + p2
    chosen = jnp.where((lane == i1) | (lane == i2), 1.0, 0.0)
    rank = count_ref[...] + _dot(tri_ref[...], chosen.astype(BF16))
    count_ref[...] += jnp.sum(chosen, axis=0, keepdims=True)
    fields = (i1 - float(ROUTER_EXPERT_LANE0), i2 - float(ROUTER_EXPERT_LANE0),
              lane_sum(jnp.where(lane == i1, rank, 0.0)), lane_sum(jnp.where(lane == i2, rank, 0.0)),
              g_w * (p1 / denom), g_w * (p2 / denom))
    route = jnp.zeros_like(logits)
    for k, value in enumerate(fields):
        route = jnp.where(lane == float(k), value, route)
    route_ref[...] = route
    ids_ref[...] = jnp.transpose(route)[0:ROUTE_ID_ROWS, :].astype(jnp.int32)


ROUTE_E1, ROUTE_E2, ROUTE_RANK1, ROUTE_RANK2, ROUTE_W1, ROUTE_W2 = range(6)
ROUTE_ID_ROWS = 8


def _mix(x2d, a, o_f, o_b, og, w, tm):
    t = x2d.shape[0]
    hv = GLA_HEADS * GLA_DV
    row = lambda width: pl.BlockSpec((tm, width), lambda i: (i, 0))
    return pl.pallas_call(
        _mix_body,
        grid=(t // tm,),
        in_specs=[row(D_MODEL), row(MLA_HEADS * MLA_V), row(hv), row(hv), row(hv),
                  _full_spec((1, GLA_DV)), _full_spec(w['wout'].shape), _full_spec((1, D_MODEL)),
                  _full_spec(w['wr_hi'].shape), _full_spec(w['wr_lo'].shape), _full_spec(w['br'].shape)],
        out_specs=(row(D_MODEL), pl.BlockSpec((tm, ROW_CHUNKS, LANES), lambda i: (i, 0, 0)), row(LANES),
                   pl.BlockSpec((ROUTE_ID_ROWS, tm), lambda i: (0, i)), _full_spec((1, LANES))),
        out_shape=(jax.ShapeDtypeStruct((t, D_MODEL), F32), jax.ShapeDtypeStruct((t, ROW_CHUNKS, LANES), F32),
                   jax.ShapeDtypeStruct((t, LANES), F32), jax.ShapeDtypeStruct((ROUTE_ID_ROWS, t), jnp.int32),
                   jax.ShapeDtypeStruct((1, LANES), F32)),
        scratch_shapes=[pltpu.VMEM((tm, tm), BF16)],
        compiler_params=_cparams("arbitrary"),
        name="mix_router",
    )(x2d, a, o_f, o_b, og, w['gla_norm'], w['wout'], w['norm_ffn'], w['wr_hi'], w['wr_lo'], w['br'])


EXPERT_TILE = 256
ROW_TILE = 256


SC_CORES = 2
SC_SUBCORES = 16
SC_GATHER_ROWS = 32


def _sc_gather(table, idx):
    n = idx.shape[0]
    workers = SC_CORES * SC_SUBCORES
    per_worker = n // workers
    assert n % (workers * SC_GATHER_ROWS) == 0
    mesh = plsc.VectorSubcoreMesh(core_axis_name="c", subcore_axis_name="s")

    @functools.partial(
        pl.kernel, mesh=mesh,
        out_type=jax.ShapeDtypeStruct((n,) + table.shape[1:], table.dtype),
        scratch_types=[pltpu.VMEM((SC_GATHER_ROWS,), jnp.int32),
                       pltpu.VMEM((SC_GATHER_ROWS,) + table.shape[1:], table.dtype),
                       pltpu.SemaphoreType.DMA])
    def gather(table_ref, idx_ref, out_ref, idx_buf, rows_buf, sem):
        base = (lax.axis_index("s") * SC_CORES + lax.axis_index("c")) * per_worker

        @pl.loop(0, per_worker // SC_GATHER_ROWS)
        def _(j):
            rows = pl.ds(base + j * SC_GATHER_ROWS, SC_GATHER_ROWS)
            pltpu.sync_copy(idx_ref.at[rows], idx_buf)
            pltpu.async_copy(table_ref.at[idx_buf], rows_buf, sem).wait()
            pltpu.sync_copy(rows_buf, out_ref.at[rows])

    return gather(table, idx)


def _sc_scatter(rows, idx, n_out):
    copies, n = idx.shape
    idx = idx.reshape(copies * n)
    workers = SC_CORES * SC_SUBCORES
    per_worker = n // workers
    assert n % (workers * SC_GATHER_ROWS) == 0
    mesh = plsc.VectorSubcoreMesh(core_axis_name="c", subcore_axis_name="s")

    @functools.partial(
        pl.kernel, mesh=mesh,
        out_type=jax.ShapeDtypeStruct((n_out,) + rows.shape[1:], rows.dtype),
        scratch_types=[pltpu.VMEM((SC_GATHER_ROWS,), jnp.int32),
                       pltpu.VMEM((SC_GATHER_ROWS,) + rows.shape[1:], rows.dtype),
                       pltpu.SemaphoreType.DMA])
    def scatter(rows_ref, idx_ref, out_ref, idx_buf, rows_buf, sem):
        base = (lax.axis_index("s") * SC_CORES + lax.axis_index("c")) * per_worker

        @pl.loop(0, per_worker // SC_GATHER_ROWS)
        def _(j):
            first = base + j * SC_GATHER_ROWS
            pltpu.sync_copy(rows_ref.at[pl.ds(first, SC_GATHER_ROWS)], rows_buf)
            for k in range(copies):
                pltpu.sync_copy(idx_ref.at[pl.ds(k * n + first, SC_GATHER_ROWS)], idx_buf)
                pltpu.async_copy(rows_buf, out_ref.at[idx_buf], sem).wait()

    return scatter(rows, idx)


SLOT_TILE = 2048


def _slots_body(starts_ref, ids_ref, slots_ref):
    ids = ids_ref[...]
    experts = ids[ROUTE_E1:ROUTE_E1 + 2, :]
    start = jnp.zeros_like(experts)
    for e in range(N_EXPERTS):
        start = jnp.where(experts == e, starts_ref[e], start)
    slots_ref[...] = start + ids[ROUTE_RANK1:ROUTE_RANK1 + 2, :]


def _slots(starts, ids):
    t = ids.shape[1]
    return pl.pallas_call(
        _slots_body,
        grid_spec=pltpu.PrefetchScalarGridSpec(
            num_scalar_prefetch=1,
            grid=(t // SLOT_TILE,),
            in_specs=[pl.BlockSpec((ROUTE_ID_ROWS, SLOT_TILE), lambda i, *_: (0, i))],
            out_specs=pl.BlockSpec((2, SLOT_TILE), lambda i, *_: (0, i))),
        out_shape=jax.ShapeDtypeStruct((2, t), jnp.int32),
        compiler_params=_cparams("parallel"),
        name="moe_slots",
    )(starts, ids)


def _queue_body(slots_ref, token_ref, *, n_tokens):
    i = pl.program_id(0)

    @pl.when(i == 0)
    def _():
        for base in range(0, token_ref.shape[0], n_tokens):
            def clear(j, carry):
                token_ref[base + j] = j
                return carry
            lax.fori_loop(0, min(n_tokens, token_ref.shape[0] - base), clear, 0, unroll=8)

    def place(r, carry):
        for k in range(2):
            token_ref[slots_ref[k, r]] = i * SLOT_TILE + r
        return carry

    lax.fori_loop(0, SLOT_TILE, place, 0, unroll=8)


def _queue_tokens(slots, n_rows):
    t = slots.shape[1]
    return pl.pallas_call(
        functools.partial(_queue_body, n_tokens=t),
        grid=(t // SLOT_TILE,),
        in_specs=[pl.BlockSpec((2, SLOT_TILE), lambda i: (0, i), memory_space=pltpu.SMEM)],
        out_specs=pl.BlockSpec(memory_space=pltpu.SMEM),
        out_shape=jax.ShapeDtypeStruct((n_rows,), jnp.int32),
        compiler_params=_cparams("arbitrary"),
        name="moe_queue",
    )(slots)


def _expert_body(tile_expert_ref, n_tiles_ref, tile_valid_ref, xs_ref, wgu_ref, wd_ref, ys_ref):
    i = pl.program_id(0)
    used = i < n_tiles_ref[0]

    @pl.when(used)
    def _():
        x = _tiles_to_rows(xs_ref[...])
        x = jnp.where(lax.broadcasted_iota(jnp.int32, x.shape, 0) < tile_valid_ref[i], x, 0.0)
        h = _dot(x.astype(BF16), wgu_ref[...])
        hg = h[:, :D_EXPERT]
        act = (hg / (1.0 + jnp.exp(-hg))) * h[:, D_EXPERT:]
        ys_ref[...] = _rows_to_tiles(_dot(act.astype(BF16), wd_ref[...]))

    @pl.when(jnp.logical_not(used))
    def _():
        ys_ref[...] = jnp.zeros_like(ys_ref)


def _experts(tile_expert, n_tiles, tile_valid, xs, w):
    n_grid = xs.shape[0] // EXPERT_TILE

    def tile(i, n_tiles):
        return jnp.minimum(i, n_tiles[0] - 1)

    tiles = lambda index: pl.BlockSpec((EXPERT_TILE, ROW_CHUNKS, LANES), index)
    return pl.pallas_call(
        _expert_body,
        grid_spec=pltpu.PrefetchScalarGridSpec(
            num_scalar_prefetch=3,
            grid=(n_grid,),
            in_specs=[tiles(lambda i, te, nt, tv: (tile(i, nt), 0, 0)),
                      pl.BlockSpec((None, D_MODEL, 2 * D_EXPERT), lambda i, te, nt, tv: (te[tile(i, nt)], 0, 0)),
                      pl.BlockSpec((None, D_EXPERT, D_MODEL), lambda i, te, nt, tv: (te[tile(i, nt)], 0, 0))],
            out_specs=tiles(lambda i, te, nt, tv: (i, 0, 0))),
        out_shape=jax.ShapeDtypeStruct(xs.shape, F32),
        compiler_params=_cparams("arbitrary"),
        name="moe_experts",
    )(tile_expert, n_tiles, tile_valid, xs, w['wgu'], w['wd'])


def _combine_body(h1_ref, route_ref, nfin_ref, y1_ref, y2_ref, out_ref):
    route = route_ref[...]
    lane = lax.broadcasted_iota(jnp.int32, route.shape, 1)
    w1 = jnp.sum(jnp.where(lane == ROUTE_W1, route, 0.0), axis=-1, keepdims=True)
    w2 = jnp.sum(jnp.where(lane == ROUTE_W2, route, 0.0), axis=-1, keepdims=True)
    y = w1 * _tiles_to_rows(y1_ref[...]) + w2 * _tiles_to_rows(y2_ref[...])
    out_ref[...] = _rms(h1_ref[...] + y, nfin_ref[...])


def _combine(h1, route, y12, w):
    t = h1.shape[0]
    row = lambda width: pl.BlockSpec((ROW_TILE, width), lambda i: (i, 0))
    tiles = lambda k: pl.BlockSpec((None, ROW_TILE, ROW_CHUNKS, LANES), lambda i: (k, i, 0, 0))
    return pl.pallas_call(
        _combine_body,
        grid=(t // ROW_TILE,),
        in_specs=[row(D_MODEL), row(LANES), _full_spec((1, D_MODEL)), tiles(0), tiles(1)],
        out_specs=row(D_MODEL),
        out_shape=jax.ShapeDtypeStruct((t, D_MODEL), F32),
        compiler_params=_cparams("parallel"),
        name="moe_combine",
    )(h1, route, w['norm_final'], y12, y12)


def _moe(hn, route, ids, counts, h1, w):
    t = hn.shape[0]
    n_rows = 2 * t + N_EXPERTS * EXPERT_TILE
    n_grid_tiles = n_rows // EXPERT_TILE
    count = counts[0, ROUTER_EXPERT_LANE0:ROUTER_EXPERT_LANE0 + N_EXPERTS].astype(jnp.int32)
    padded = jnp.maximum((count + EXPERT_TILE - 1) // EXPERT_TILE, 1) * EXPERT_TILE
    ends = jnp.sum(jnp.where(jnp.arange(N_EXPERTS)[:, None] <= jnp.arange(N_EXPERTS)[None, :], padded[:, None], 0),
                   axis=0)
    starts = ends - padded
    n_tiles = (ends[-1:] // EXPERT_TILE)
    tile_rows = jnp.arange(n_grid_tiles, dtype=jnp.int32) * EXPERT_TILE
    tile_expert = jnp.minimum(jnp.sum((ends[None, :] <= tile_rows[:, None]).astype(jnp.int32), axis=1), N_EXPERTS - 1)
    tile_valid = jnp.clip(jnp.sum(jnp.where(jnp.arange(N_EXPERTS)[None, :] == tile_expert[:, None],
                                            (starts + count)[None, :], 0), axis=1) - tile_rows, 0, EXPERT_TILE)
    slots = _slots(starts, ids)
    ys = _experts(tile_expert, n_tiles, tile_valid, _sc_scatter(hn, slots, n_rows), w)
    y12 = _sc_gather(ys, slots.reshape(2 * t)).reshape(2, t, ROW_CHUNKS, LANES)
    return _combine(h1, route, y12, w)


def _rope_tables(positions):
    inv_freq = 1.0 / (ROPE_THETA ** (jnp.arange(0, MLA_ROPE, 2, dtype=F32) / MLA_ROPE))
    ang = positions.astype(F32)[:, None] * inv_freq[None, :]
    cos, sin = jnp.cos(ang), jnp.sin(ang)
    reps = LANES // MLA_ROPE
    return jnp.tile(jnp.concatenate([cos, cos], axis=-1), (1, reps)), jnp.tile(jnp.concatenate([-sin, sin], axis=-1), (1, reps))


def _pack_weights(norm_mix, w_in, q_a_norm, w_uq, kv_a_norm, w_ukv, w_gate_fwd, b_gate_fwd, w_gate_bwd, b_gate_bwd,
                  gla_norm, w_out, norm_ffn, w_router_group, b_router_group, w_router_expert, b_router_expert,
                  w_expert_gate, w_expert_up, w_expert_down, norm_final):
    l = 0
    hk = GLA_HEADS * GLA_DK
    hv = GLA_HEADS * GLA_DV
    c_q, c_kv, k_pe, gq, gk, gv, lr_f, lr_b, og = jnp.split(
        w_in[l], np.cumsum([MLA_Q_RANK, MLA_KV_RANK, MLA_ROPE, hk, hk, hv, GLA_GATE_RANK, GLA_GATE_RANK])[:].tolist(),
        axis=-1)
    lr_pad = jnp.zeros((D_MODEL, LANES - 2 * GLA_GATE_RANK), F32)
    win = jnp.concatenate([c_q, c_kv, k_pe, k_pe, gq, gk, gv, og, lr_f, lr_b, lr_pad], axis=-1).astype(BF16)
    wuq = w_uq[l].reshape(MLA_Q_RANK, MLA_HEADS, MLA_NOPE + MLA_ROPE)
    wuq = jnp.concatenate([wuq[:, :, :MLA_NOPE].reshape(MLA_Q_RANK, -1), wuq[:, :, MLA_NOPE:].reshape(MLA_Q_RANK, -1)],
                          axis=-1).astype(BF16)
    wgate = jnp.zeros((LANES, 2 * hk), F32)
    wgate = wgate.at[0:GLA_GATE_RANK, 0:hk].set(w_gate_fwd[l])
    wgate = wgate.at[GLA_GATE_RANK:2 * GLA_GATE_RANK, hk:].set(w_gate_bwd[l])
    wr = jnp.zeros((D_MODEL, LANES), F32)
    wr = wr.at[:, ROUTER_GROUP_LANE0:ROUTER_GROUP_LANE0 + N_GROUPS].set(w_router_group[l])
    wr = wr.at[:, ROUTER_EXPERT_LANE0:ROUTER_EXPERT_LANE0 + N_EXPERTS].set(w_router_expert[l])
    wr_hi = wr.astype(BF16)
    br = jnp.zeros((1, LANES), F32)
    br = br.at[0, ROUTER_GROUP_LANE0:ROUTER_GROUP_LANE0 + N_GROUPS].set(b_router_group[l])
    br = br.at[0, ROUTER_EXPERT_LANE0:ROUTER_EXPERT_LANE0 + N_EXPERTS].set(b_router_expert[l])
    wgu = jnp.concatenate([w_expert_gate[l], w_expert_up[l]], axis=-1).reshape(N_EXPERTS, D_MODEL, 2 * D_EXPERT)
    return {
        'norm_mix': norm_mix[l][None], 'win': win, 'q_a_norm': q_a_norm[l][None], 'wuq': wuq,
        'kv_a_norm': kv_a_norm[l][None], 'wukv': w_ukv[l].astype(BF16),
        'wgate': wgate.astype(BF16), 'bgate': jnp.concatenate([b_gate_fwd[l], b_gate_bwd[l]])[None],
        'gla_norm': gla_norm[l][None], 'wout': w_out[l].astype(BF16), 'norm_ffn': norm_ffn[l][None],
        'wr_hi': wr_hi, 'wr_lo': (wr - wr_hi.astype(F32)).astype(BF16), 'br': br,
        'wgu': wgu.astype(BF16), 'wd': w_expert_down[l].reshape(N_EXPERTS, D_EXPERT, D_MODEL).astype(BF16),
        'norm_final': norm_final[None],
    }


def _meta_streams(meta_tokens, w):
    cos, sin = _rope_tables(jnp.arange(N_META))
    _, k, v, _, gk, gv, gf, _, _ = _inproj(meta_tokens, cos, sin, w, N_META)
    pad_keys = ((0, 0), (0, LANES - N_META), (0, 0))
    front = ((GLA_CHUNK - N_META, 0), (0, 0))
    return (jnp.pad(k, pad_keys), jnp.pad(v, pad_keys), jnp.pad(gk, front), jnp.pad(gv, front), jnp.pad(gf, front))


def _token_mixers(x, meta, w, tm, tq, tk, tb):
    bsz, seq, _ = x.shape
    km, vm, mk, mv, mg = meta
    x2d = x.reshape(bsz * seq, D_MODEL)
    cos, sin = _rope_tables(N_META + jnp.arange(seq))
    q, k, v, gq, gk, gv, gf, gb, og = _inproj(x2d, cos, sin, w, tm)
    a = _attention(q, k, v, km, vm, bsz, seq, tq, tk)
    o_f, o_b = _gla(gq, gk, gv, gf, gb, mk, mv, mg, bsz, seq, tb)
    return _mix(x2d, a, o_f, o_b, og, w, tm)


def kernel(x_prompt, x_sample, meta_tokens, norm_mix, w_in, q_a_norm, w_uq, kv_a_norm, w_ukv, w_gate_fwd, b_gate_fwd, w_gate_bwd, b_gate_bwd, gla_norm, w_out, norm_ffn, w_router_group, b_router_group, w_router_expert, b_router_expert, w_expert_gate, w_expert_up, w_expert_down, norm_final):
    w = _pack_weights(norm_mix, w_in, q_a_norm, w_uq, kv_a_norm, w_ukv, w_gate_fwd, b_gate_fwd, w_gate_bwd,
                      b_gate_bwd, gla_norm, w_out, norm_ffn, w_router_group, b_router_group, w_router_expert,
                      b_router_expert, w_expert_gate, w_expert_up, w_expert_down, norm_final)
    meta = _meta_streams(meta_tokens, w)
    outs = []
    for x in (x_prompt, x_sample):
        h1, hn, route, ids, counts = _token_mixers(x, meta, w, tm=512, tq=1024, tk=512, tb=512)
        outs.append(_moe(hn, route, ids, counts, h1, w).reshape(x.shape))
    return tuple(outs)
```

```python
import functools

import numpy as np
import jax
import jax.numpy as jnp
from jax import lax
from jax.experimental import pallas as pl
from jax.experimental.pallas import tpu as pltpu
from jax.experimental.pallas import tpu_sc as plsc

F32 = jnp.float32
BF16 = jnp.bfloat16

D_MODEL = 1024
N_META = 16
MLA_HEADS = 4
MLA_Q_RANK = 384
MLA_KV_RANK = 256
MLA_NOPE = 128
MLA_ROPE = 64
MLA_V = 128
ROPE_THETA = 10000.0
GLA_HEADS = 4
GLA_DK = 64
GLA_DV = 128
GLA_GATE_RANK = 16
GLA_TAU = 16.0
GLA_CHUNK = 64
N_GROUPS = 4
EXPERTS_PER_GROUP = 8
N_EXPERTS = N_GROUPS * EXPERTS_PER_GROUP
D_EXPERT = 256
EPS = 1e-6

LANES = 128
V7X_VMEM_BYTES = 64 * 1024 * 1024
VMEM_LIMIT = V7X_VMEM_BYTES * 7 // 8

ATTN_SCALE = (MLA_NOPE + MLA_ROPE) ** -0.5 * float(np.log2(np.e))
QK_WIDTH = 2 * LANES
V_WIDTH = 2 * LANES
ATTN_GROUP = 8

C_CQ = 0
C_CKV = C_CQ + MLA_Q_RANK
C_KPE = C_CKV + MLA_KV_RANK
C_GQ = C_KPE + LANES
C_GK = C_GQ + GLA_HEADS * GLA_DK
C_GV = C_GK + GLA_HEADS * GLA_DK
C_OG = C_GV + GLA_HEADS * GLA_DV
C_LR = C_OG + GLA_HEADS * GLA_DV
D_IN_PACKED = C_LR + LANES

ROUTER_GROUP_LANE0 = 0
ROUTER_EXPERT_LANE0 = N_GROUPS


def _cparams(*semantics):
    return pltpu.CompilerParams(dimension_semantics=semantics, vmem_limit_bytes=VMEM_LIMIT)


def _rms(x, g):
    return x * lax.rsqrt(jnp.mean(x * x, axis=-1, keepdims=True) + EPS) * g


def _dot(a, b):
    return jnp.dot(a, b, preferred_element_type=F32)


def _dot_nt(a, b):
    return lax.dot_general(a, b, (((1,), (1,)), ((), ())), preferred_element_type=F32)


def _dot_tn(a, b):
    return lax.dot_general(a, b, (((0,), (0,)), ((), ())), preferred_element_type=F32)


def _full_spec(shape):
    return pl.BlockSpec(shape, lambda *_: (0,) * len(shape))


SUBLANES = 8
ROW_CHUNKS = D_MODEL // LANES
assert ROW_CHUNKS == SUBLANES


def _rows_to_tiles(x):
    chunks = jnp.stack([x[:, s * LANES:(s + 1) * LANES] for s in range(ROW_CHUNKS)], axis=0)
    return pltpu.einshape("smd->msd", chunks)


def _tiles_to_rows(x):
    chunks = pltpu.einshape("msd->smd", x)
    return jnp.concatenate([chunks[s] for s in range(ROW_CHUNKS)], axis=-1)


def _rope_pairs(x, cos, sin_signed, first_half):
    swapped = jnp.where(first_half, pltpu.roll(x, LANES - MLA_ROPE // 2, 1), pltpu.roll(x, MLA_ROPE // 2, 1))
    return x * cos + swapped * sin_signed


def _inproj_body(x_ref, cos_ref, sin_ref, nmix_ref, win_ref, qan_ref, wuq_ref, kvan_ref, wukv_ref,
                 wgate_ref, bgate_ref,
                 q_ref, k_ref, v_ref, gq_ref, gk_ref, gv_ref, gf_ref, gb_ref, og_ref):
    hn = _rms(x_ref[...], nmix_ref[...]).astype(BF16)

    def proj(lo, hi):
        return _dot(hn, win_ref[:, lo:hi])

    cos = cos_ref[...]
    sin = sin_ref[...]
    lane = lax.broadcasted_iota(jnp.int32, cos.shape, 1)
    first_half = (lane & (MLA_ROPE - 1)) < MLA_ROPE // 2
    low_lanes = lane < MLA_ROPE

    cq = _rms(proj(C_CQ, C_CKV), qan_ref[...]).astype(BF16)
    qn = _dot(cq, wuq_ref[:, 0:MLA_HEADS * MLA_NOPE]) * ATTN_SCALE
    qr = _dot(cq, wuq_ref[:, MLA_HEADS * MLA_NOPE:])
    for j in range(MLA_HEADS // 2):
        rj = (_rope_pairs(qr[:, j * LANES:(j + 1) * LANES], cos, sin, first_half) * ATTN_SCALE).astype(BF16)
        for h in (2 * j, 2 * j + 1):
            q_ref[h, :, 0:LANES] = qn[:, h * LANES:(h + 1) * LANES].astype(BF16)
            q_ref[h, :, LANES:QK_WIDTH] = rj

    ckv = _rms(proj(C_CKV, C_KPE), kvan_ref[...]).astype(BF16)
    kv = _dot(ckv, wukv_ref[...])
    kr = _rope_pairs(proj(C_KPE, C_GQ), cos, sin, first_half)
    kr_even = jnp.where(low_lanes, kr, 0.0).astype(BF16)
    kr_odd = jnp.where(low_lanes, 0.0, kr).astype(BF16)
    for h in range(MLA_HEADS):
        base = h * (MLA_NOPE + MLA_V)
        k_ref[h, :, 0:LANES] = kv[:, base:base + MLA_NOPE].astype(BF16)
        k_ref[h, :, LANES:QK_WIDTH] = kr_even if h % 2 == 0 else kr_odd
        v_ref[h, :, 0:MLA_V] = kv[:, base + MLA_NOPE:base + MLA_NOPE + MLA_V].astype(BF16)
        v_ref[h, :, MLA_V:V_WIDTH] = jnp.ones((kv.shape[0], V_WIDTH - MLA_V), BF16)

    gq_ref[...] = (proj(C_GQ, C_GK) * (GLA_DK ** -0.5)).astype(BF16)
    gk_ref[...] = proj(C_GK, C_GV).astype(BF16)
    gv_ref[...] = proj(C_GV, C_OG).astype(BF16)
    og_ref[...] = proj(C_OG, C_LR).astype(BF16)
    pre = _dot(proj(C_LR, D_IN_PACKED).astype(BF16), wgate_ref[...]) + bgate_ref[...]
    logsig = jnp.minimum(pre, 0.0) - jnp.log1p(jnp.exp(-jnp.abs(pre)))
    gates = logsig * (1.0 / GLA_TAU)
    gf_ref[...] = gates[:, 0:GLA_HEADS * GLA_DK]
    gb_ref[...] = gates[:, GLA_HEADS * GLA_DK:]


def _inproj(x2d, cos, sin, w, tm):
    t = x2d.shape[0]
    blocks_per_seq = cos.shape[0] // tm
    hk = GLA_HEADS * GLA_DK
    hv = GLA_HEADS * GLA_DV
    row = lambda width: pl.BlockSpec((tm, width), lambda i: (i, 0))
    head_rows = lambda width: pl.BlockSpec((MLA_HEADS, tm, width), lambda i: (0, i, 0))
    tab = pl.BlockSpec((tm, LANES), lambda i: (i % blocks_per_seq, 0))
    out_shape = (
        jax.ShapeDtypeStruct((MLA_HEADS, t, QK_WIDTH), BF16),
        jax.ShapeDtypeStruct((MLA_HEADS, t, QK_WIDTH), BF16),
        jax.ShapeDtypeStruct((MLA_HEADS, t, V_WIDTH), BF16),
        jax.ShapeDtypeStruct((t, hk), BF16),
        jax.ShapeDtypeStruct((t, hk), BF16),
        jax.ShapeDtypeStruct((t, hv), BF16),
        jax.ShapeDtypeStruct((t, hk), F32),
        jax.ShapeDtypeStruct((t, hk), F32),
        jax.ShapeDtypeStruct((t, hv), BF16),
    )
    return pl.pallas_call(
        _inproj_body,
        grid=(t // tm,),
        in_specs=[row(D_MODEL), tab, tab,
                  _full_spec((1, D_MODEL)), _full_spec((D_MODEL, D_IN_PACKED)),
                  _full_spec((1, MLA_Q_RANK)), _full_spec(w['wuq'].shape),
                  _full_spec((1, MLA_KV_RANK)), _full_spec(w['wukv'].shape),
                  _full_spec(w['wgate'].shape), _full_spec(w['bgate'].shape)],
        out_specs=(head_rows(QK_WIDTH), head_rows(QK_WIDTH), head_rows(V_WIDTH),
                   row(hk), row(hk), row(hv), row(hk), row(hk), row(hv)),
        out_shape=out_shape,
        compiler_params=_cparams("parallel"),
        name="inproj",
    )(x2d, cos, sin, w['norm_mix'], w['win'], w['q_a_norm'], w['wuq'], w['kv_a_norm'], w['wukv'],
      w['wgate'], w['bgate'])


def _attn_body(q_ref, k_ref, v_ref, km_ref, vm_ref, o_ref, s_ref, acc_ref, *, tk):
    q = q_ref[...]
    n_groups = k_ref.shape[0] // (ATTN_GROUP * tk)

    def scores(j):
        return _dot_nt(q, k_ref[pl.ds(pl.multiple_of(j * tk, tk), tk), :])

    def values(j):
        return v_ref[pl.ds(pl.multiple_of(j * tk, tk), tk), :]

    def absorb(m, s, v):
        m_new = jnp.maximum(m, jnp.max(s, axis=-1, keepdims=True))
        p = jnp.exp2(s - m_new)
        acc_ref[...] = jnp.exp2(m - m_new) * acc_ref[...] + _dot(p.astype(BF16), v)
        return m_new

    s_ref[0] = scores(0)
    sm = _dot_nt(q, km_ref[...])
    sm = jnp.where(lax.broadcasted_iota(jnp.int32, sm.shape, 1) < N_META, sm, -jnp.inf)
    m = jnp.max(sm, axis=-1, keepdims=True)
    acc_ref[...] = _dot(jnp.exp2(sm - m).astype(BF16), vm_ref[...])

    def group(g, m, last):
        for i in range(ATTN_GROUP):
            j = ATTN_GROUP * g + i
            s = s_ref[i % 2]
            if not (last and i == ATTN_GROUP - 1):
                s_ref[(i + 1) % 2] = scores(j + 1)
            m = absorb(m, s, values(j))
        return m

    m = lax.fori_loop(0, n_groups - 1, lambda g, m: group(g, m, False), m)
    group(n_groups - 1, m, True)
    acc = acc_ref[...]
    o_ref[...] = (acc[:, :MLA_V] / acc[:, MLA_V:]).astype(o_ref.dtype)


def _attention(q, k, v, km, vm, bsz, seq, tq, tk):
    nq = seq // tq
    return pl.pallas_call(
        functools.partial(_attn_body, tk=tk),
        grid=(bsz, MLA_HEADS, nq),
        in_specs=[pl.BlockSpec((None, tq, QK_WIDTH), lambda b, h, i: (h, b * nq + i, 0)),
                  pl.BlockSpec((None, seq, QK_WIDTH), lambda b, h, i: (h, b, 0)),
                  pl.BlockSpec((None, seq, V_WIDTH), lambda b, h, i: (h, b, 0)),
                  pl.BlockSpec((None, LANES, QK_WIDTH), lambda b, h, i: (h, 0, 0)),
                  pl.BlockSpec((None, LANES, V_WIDTH), lambda b, h, i: (h, 0, 0))],
        out_specs=pl.BlockSpec((tq, MLA_V), lambda b, h, i: (b * nq + i, h)),
        out_shape=jax.ShapeDtypeStruct((bsz * seq, MLA_HEADS * MLA_V), BF16),
        scratch_shapes=[pltpu.VMEM((2, tq, tk), F32), pltpu.VMEM((tq, V_WIDTH), F32)],
        compiler_params=_cparams("parallel", "parallel", "arbitrary"),
        name="mla_attention",
    )(q, k, v, km, vm)


def _split3(x):
    hi = x.astype(BF16)
    r1 = x - hi.astype(F32)
    mid = r1.astype(BF16)
    lo = (r1 - mid.astype(F32)).astype(BF16)
    return hi, mid, lo


def _gla_log_decay(g, tri):
    g_hi, g_mid, g_lo = _split3(g)
    return _dot(tri, g_hi) + _dot(tri, g_mid) + _dot(tri, g_lo)


def _head_rows(x):
    even = lax.broadcasted_iota(jnp.int32, x.shape, 1) < GLA_DK
    return jnp.concatenate([jnp.where(even, x, 0.0), jnp.where(even, 0.0, x)], axis=0).astype(BF16)


def _gla_operands(q, k, b, mid, last):
    b_last = b[last:last + 1, :]
    ke = (k * jnp.exp(b_last - b)).astype(BF16)
    if q is None:
        return jnp.exp(b_last), ke, None, None, None
    b_mid = b[mid:mid + 1, :]
    ks = (k * jnp.exp(b_mid - b)).astype(BF16)
    return jnp.exp(b_last), ke, ks, _head_rows(q * jnp.exp(b - b_mid)), _head_rows(q * jnp.exp(b))


def _gla_state_update(v_even, v_odd, ke):
    return jnp.where(lax.broadcasted_iota(jnp.int32, (GLA_DV, LANES), 1) < GLA_DK,
                     _dot_tn(v_even, ke), _dot_tn(v_odd, ke))


def _gla_intra(v_even, v_odd, ks, qs2, keep):
    scores = jnp.where(keep, _dot_nt(qs2, ks), 0.0).astype(BF16)
    return _dot(scores[:GLA_CHUNK], v_even), _dot(scores[GLA_CHUNK:], v_odd)


def _gla_body(qf_ref, kf_ref, vf_ref, gf_ref, qb_ref, kb_ref, vb_ref, gb_ref, mk_ref, mv_ref, mg_ref,
              of_ref, ob_ref, state_ref):
    n_chunks = qf_ref.shape[0] // GLA_CHUNK
    n_pairs = GLA_HEADS // 2
    r = lax.broadcasted_iota(jnp.int32, (GLA_CHUNK, GLA_CHUNK), 0)
    c = lax.broadcasted_iota(jnp.int32, (GLA_CHUNK, GLA_CHUNK), 1)
    tri_f = jnp.where(c <= r, 1.0, 0.0).astype(BF16)
    tri_b = jnp.where(c >= r, 1.0, 0.0).astype(BF16)
    r2 = lax.broadcasted_iota(jnp.int32, (2 * GLA_CHUNK, GLA_CHUNK), 0) & (GLA_CHUNK - 1)
    c2 = lax.broadcasted_iota(jnp.int32, (2 * GLA_CHUNK, GLA_CHUNK), 1)
    keep_f = c2 <= r2
    keep_b = c2 >= r2
    mid_f, last_f = GLA_CHUNK // 2 - 1, GLA_CHUNK - 1
    mid_b, last_b = GLA_CHUNK // 2, 0

    def pair_cols(p):
        return slice(p * LANES, (p + 1) * LANES)

    def head_cols(h):
        return slice(h * GLA_DV, (h + 1) * GLA_DV)

    @pl.when(pl.program_id(1) == 0)
    def _():
        for p in range(n_pairs):
            b = _gla_log_decay(mg_ref[:, pair_cols(p)], tri_f)
            _, ke, _, _, _ = _gla_operands(None, mk_ref[:, pair_cols(p)], b, mid_f, last_f)
            state_ref[p] = _gla_state_update(mv_ref[:, head_cols(2 * p)].astype(BF16),
                                             mv_ref[:, head_cols(2 * p + 1)].astype(BF16), ke)
            state_ref[n_pairs + p] = jnp.zeros((GLA_DV, LANES), F32)

    scans = []
    for p in range(n_pairs):
        scans.append((p, list(range(n_chunks)), qf_ref, kf_ref, vf_ref, gf_ref, of_ref, p,
                      tri_f, keep_f, mid_f, last_f))
        scans.append((p, list(reversed(range(n_chunks))), qb_ref, kb_ref, vb_ref, gb_ref, ob_ref, n_pairs + p,
                      tri_b, keep_b, mid_b, last_b))

    def rows(c):
        return slice(c * GLA_CHUNK, (c + 1) * GLA_CHUNK)

    def values(v_ref, p, c):
        return (v_ref[rows(c), head_cols(2 * p)].astype(BF16), v_ref[rows(c), head_cols(2 * p + 1)].astype(BF16))

    log_decay = [[_gla_log_decay(g_ref[rows(c), pair_cols(p)], tri) for c in order]
                 for (p, order, _, _, _, g_ref, _, _, tri, _, _, _) in scans]
    operands = [[_gla_operands(q_ref[rows(c), pair_cols(p)], k_ref[rows(c), pair_cols(p)], b, mid, last)
                 for c, b in zip(order, bs)]
                for (p, order, q_ref, k_ref, _, _, _, _, _, _, mid, last), bs in zip(scans, log_decay)]
    updates = [[_gla_state_update(*values(v_ref, p, c), ops[1]) for c, ops in zip(order, opss)]
               for (p, order, _, _, v_ref, _, _, _, _, _, _, _), opss in zip(scans, operands)]
    intra = [[_gla_intra(*values(v_ref, p, c), ops[2], ops[3], keep) for c, ops in zip(order, opss)]
             for (p, order, _, _, v_ref, _, _, _, _, keep, _, _), opss in zip(scans, operands)]
    states = []
    for (_, order, _, _, _, _, _, slot, _, _, _, _), opss, upds in zip(scans, operands, updates):
        st = state_ref[slot]
        entering = []
        for ops, upd in zip(opss, upds):
            entering.append(st.astype(BF16))
            st = st * ops[0] + upd
        state_ref[slot] = st
        states.append(entering)
    for (p, order, _, _, _, _, o_ref, _, _, _, _, _), opss, sts, locs in zip(scans, operands, states, intra):
        for c, ops, st, (o_even, o_odd) in zip(order, opss, sts, locs):
            inter = _dot_nt(ops[4], st)
            o_ref[rows(c), head_cols(2 * p)] = (o_even + inter[:GLA_CHUNK]).astype(o_ref.dtype)
            o_ref[rows(c), head_cols(2 * p + 1)] = (o_odd + inter[GLA_CHUNK:]).astype(o_ref.dtype)


def _gla(gq, gk, gv, gf, gb, mk, mv, mg, bsz, seq, tb):
    nb = seq // tb
    hk = GLA_HEADS * GLA_DK
    hv = GLA_HEADS * GLA_DV
    fwd = lambda width: pl.BlockSpec((tb, width), lambda b, j: (b * nb + j, 0))
    bwd = lambda width: pl.BlockSpec((tb, width), lambda b, j: (b * nb + nb - 1 - j, 0))
    t = bsz * seq
    return pl.pallas_call(
        _gla_body,
        grid=(bsz, nb),
        in_specs=[fwd(hk), fwd(hk), fwd(hv), fwd(hk), bwd(hk), bwd(hk), bwd(hv), bwd(hk),
                  _full_spec(mk.shape), _full_spec(mv.shape), _full_spec(mg.shape)],
        out_specs=(fwd(hv), bwd(hv)),
        out_shape=(jax.ShapeDtypeStruct((t, hv), BF16), jax.ShapeDtypeStruct((t, hv), BF16)),
        scratch_shapes=[pltpu.VMEM((2 * (GLA_HEADS // 2), GLA_DV, LANES), F32)],
        compiler_params=_cparams("parallel", "arbitrary"),
        name="gla_scan",
    )(gq, gk, gv, gf, gq, gk, gv, gb, mk, mv, mg)


def _mix_body(x_ref, a_ref, of_ref, ob_ref, og_ref, gnorm_ref, wout_ref, nffn_ref, wr_hi_ref, wr_lo_ref, br_ref,
              h1_ref, hn_ref, route_ref, ids_ref, count_ref, tri_ref):
    tm = x_ref.shape[0]

    @pl.when(pl.program_id(0) == 0)
    def _():
        r = lax.broadcasted_iota(jnp.int32, (tm, tm), 0)
        c = lax.broadcasted_iota(jnp.int32, (tm, tm), 1)
        tri_ref[...] = jnp.where(c < r, 1.0, 0.0).astype(BF16)
        count_ref[...] = jnp.zeros_like(count_ref)

    a_width = MLA_HEADS * MLA_V
    h1 = x_ref[...] + _dot(a_ref[...], wout_ref[0:a_width, :])
    for h in range(GLA_HEADS):
        cols = slice(h * GLA_DV, (h + 1) * GLA_DV)
        o = of_ref[:, cols].astype(F32) + ob_ref[:, cols].astype(F32)
        og = og_ref[:, cols].astype(F32)
        silu = og / (1.0 + jnp.exp(-og))
        gh = (_rms(o, gnorm_ref[...]) * silu).astype(BF16)
        h1 = h1 + _dot(gh, wout_ref[a_width + h * GLA_DV:a_width + (h + 1) * GLA_DV, :])
    h1_ref[...] = h1
    hn = _rms(h1, nffn_ref[...])
    hn_ref[...] = _rows_to_tiles(hn)

    hn_hi = hn.astype(BF16)
    hn_lo = (hn - hn_hi.astype(F32)).astype(BF16)
    logits = (_dot(hn_hi, wr_hi_ref[...]) + _dot(hn_lo, wr_hi_ref[...]) + _dot(hn_hi, wr_lo_ref[...])
              + br_ref[...])
    lane = lax.broadcasted_iota(jnp.int32, logits.shape, 1).astype(F32)
    none = float(LANES)
    neg = -jnp.inf

    def lane_max(x):
        return jnp.max(x, axis=-1, keepdims=True)

    def lane_sum(x):
        return jnp.sum(x, axis=-1, keepdims=True)

    def first_lane(mask):
        return jnp.min(jnp.where(mask, lane, none), axis=-1, keepdims=True)

    is_group = lane < float(N_GROUPS)
    g_max = lane_max(jnp.where(is_group, logits, neg))
    g_exp = jnp.where(is_group, jnp.exp(logits - g_max), 0.0)
    g_prob = g_exp / lane_sum(g_exp)
    g_w = lane_max(g_prob)
    g_idx = first_lane(is_group & (g_prob == g_w))
    e_lo = float(ROUTER_EXPERT_LANE0) + float(EXPERTS_PER_GROUP) * g_idx
    sel = (lane >= e_lo) & (lane < e_lo + float(EXPERTS_PER_GROUP))
    e_max = lane_max(jnp.where(sel, logits, neg))
    e_exp = jnp.where(sel, jnp.exp(logits - e_max), 0.0)
    e_prob = e_exp / lane_sum(e_exp)
    p1 = lane_max(jnp.where(sel, e_prob, neg))
    i1 = first_lane(sel & (e_prob == p1))
    rest = sel & (lane != i1)
    p2 = lane_max(jnp.where(rest, e_prob, neg))
    i2 = first_lane(rest & (e_prob == p2))
    denom = p1 + p2
    chosen = jnp.where((lane == i1) | (lane == i2), 1.0, 0.0)
    rank = count_ref[...] + _dot(tri_ref[...], chosen.astype(BF16))
    count_ref[...] += jnp.sum(chosen, axis=0, keepdims=True)
    fields = (i1 - float(ROUTER_EXPERT_LANE0), i2 - float(ROUTER_EXPERT_LANE0),
              lane_sum(jnp.where(lane == i1, rank, 0.0)), lane_sum(jnp.where(lane == i2, rank, 0.0)),
              g_w * (p1 / denom), g_w * (p2 / denom))
    route = jnp.zeros_like(logits)
    for k, value in enumerate(fields):
        route = jnp.where(lane == float(k), value, route)
    route_ref[...] = route
    ids_ref[...] = jnp.transpose(route)[0:ROUTE_ID_ROWS, :].astype(jnp.int32)


ROUTE_E1, ROUTE_E2, ROUTE_RANK1, ROUTE_RANK2, ROUTE_W1, ROUTE_W2 = range(6)
ROUTE_ID_ROWS = 8


def _mix(x2d, a, o_f, o_b, og, w, tm):
    t = x2d.shape[0]
    hv = GLA_HEADS * GLA_DV
    row = lambda width: pl.BlockSpec((tm, width), lambda i: (i, 0))
    return pl.pallas_call(
        _mix_body,
        grid=(t // tm,),
        in_specs=[row(D_MODEL), row(MLA_HEADS * MLA_V), row(hv), row(hv), row(hv),
                  _full_spec((1, GLA_DV)), _full_spec(w['wout'].shape), _full_spec((1, D_MODEL)),
                  _full_spec(w['wr_hi'].shape), _full_spec(w['wr_lo'].shape), _full_spec(w['br'].shape)],
        out_specs=(row(D_MODEL), pl.BlockSpec((tm, ROW_CHUNKS, LANES), lambda i: (i, 0, 0)), row(LANES),
                   pl.BlockSpec((ROUTE_ID_ROWS, tm), lambda i: (0, i)), _full_spec((1, LANES))),
        out_shape=(jax.ShapeDtypeStruct((t, D_MODEL), F32), jax.ShapeDtypeStruct((t, ROW_CHUNKS, LANES), F32),
                   jax.ShapeDtypeStruct((t, LANES), F32), jax.ShapeDtypeStruct((ROUTE_ID_ROWS, t), jnp.int32),
                   jax.ShapeDtypeStruct((1, LANES), F32)),
        scratch_shapes=[pltpu.VMEM((tm, tm), BF16)],
        compiler_params=_cparams("arbitrary"),
        name="mix_router",
    )(x2d, a, o_f, o_b, og, w['gla_norm'], w['wout'], w['norm_ffn'], w['wr_hi'], w['wr_lo'], w['br'])


EXPERT_TILE = 256
ROW_TILE = 256


SC_CORES = 2
SC_SUBCORES = 16
SC_GATHER_ROWS = 32


def _sc_gather(table, idx):
    n = idx.shape[0]
    workers = SC_CORES * SC_SUBCORES
    per_worker = n // workers
    assert n % (workers * SC_GATHER_ROWS) == 0
    mesh = plsc.VectorSubcoreMesh(core_axis_name="c", subcore_axis_name="s")

    @functools.partial(
        pl.kernel, mesh=mesh,
        out_type=jax.ShapeDtypeStruct((n,) + table.shape[1:], table.dtype),
        scratch_types=[pltpu.VMEM((SC_GATHER_ROWS,), jnp.int32),
                       pltpu.VMEM((SC_GATHER_ROWS,) + table.shape[1:], table.dtype),
                       pltpu.SemaphoreType.DMA])
    def gather(table_ref, idx_ref, out_ref, idx_buf, rows_buf, sem):
        base = (lax.axis_index("s") * SC_CORES + lax.axis_index("c")) * per_worker

        @pl.loop(0, per_worker // SC_GATHER_ROWS)
        def _(j):
            rows = pl.ds(base + j * SC_GATHER_ROWS, SC_GATHER_ROWS)
            pltpu.sync_copy(idx_ref.at[rows], idx_buf)
            pltpu.async_copy(table_ref.at[idx_buf], rows_buf, sem).wait()
            pltpu.sync_copy(rows_buf, out_ref.at[rows])

    return gather(table, idx)


def _sc_scatter(rows, idx, n_out):
    copies, n = idx.shape
    idx = idx.reshape(copies * n)
    workers = SC_CORES * SC_SUBCORES
    per_worker = n // workers
    assert n % (workers * SC_GATHER_ROWS) == 0
    mesh = plsc.VectorSubcoreMesh(core_axis_name="c", subcore_axis_name="s")

    @functools.partial(
        pl.kernel, mesh=mesh,
        out_type=jax.ShapeDtypeStruct((n_out,) + rows.shape[1:], rows.dtype),
        scratch_types=[pltpu.VMEM((SC_GATHER_ROWS,), jnp.int32),
                       pltpu.VMEM((SC_GATHER_ROWS,) + rows.shape[1:], rows.dtype),
                       pltpu.SemaphoreType.DMA])
    def scatter(rows_ref, idx_ref, out_ref, idx_buf, rows_buf, sem):
        base = (lax.axis_index("s") * SC_CORES + lax.axis_index("c")) * per_worker

        @pl.loop(0, per_worker // SC_GATHER_ROWS)
        def _(j):
            first = base + j * SC_GATHER_ROWS
            pltpu.sync_copy(rows_ref.at[pl.ds(first, SC_GATHER_ROWS)], rows_buf)
            for k in range(copies):
                pltpu.sync_copy(idx_ref.at[pl.ds(k * n + first, SC_GATHER_ROWS)], idx_buf)
                pltpu.async_copy(rows_buf, out_ref.at[idx_buf], sem).wait()

    return scatter(rows, idx)


SLOT_TILE = 2048


def _slots_body(starts_ref, ids_ref, slots_ref):
    ids = ids_ref[...]
    experts = ids[ROUTE_E1:ROUTE_E1 + 2, :]
    start = jnp.zeros_like(experts)
    for e in range(N_EXPERTS):
        start = jnp.where(experts == e, starts_ref[e], start)
    slots_ref[...] = start + ids[ROUTE_RANK1:ROUTE_RANK1 + 2, :]


def _slots(starts, ids):
    t = ids.shape[1]
    return pl.pallas_call(
        _slots_body,
        grid_spec=pltpu.PrefetchScalarGridSpec(
            num_scalar_prefetch=1,
            grid=(t // SLOT_TILE,),
            in_specs=[pl.BlockSpec((ROUTE_ID_ROWS, SLOT_TILE), lambda i, *_: (0, i))],
            out_specs=pl.BlockSpec((2, SLOT_TILE), lambda i, *_: (0, i))),
        out_shape=jax.ShapeDtypeStruct((2, t), jnp.int32),
        compiler_params=_cparams("parallel"),
        name="moe_slots",
    )(starts, ids)


def _queue_body(slots_ref, token_ref, *, n_tokens):
    i = pl.program_id(0)

    @pl.when(i == 0)
    def _():
        for base in range(0, token_ref.shape[0], n_tokens):
            def clear(j, carry):
                token_ref[base + j] = j
                return carry
            lax.fori_loop(0, min(n_tokens, token_ref.shape[0] - base), clear, 0, unroll=8)

    def place(r, carry):
        for k in range(2):
            token_ref[slots_ref[k, r]] = i * SLOT_TILE + r
        return carry

    lax.fori_loop(0, SLOT_TILE, place, 0, unroll=8)


def _queue_tokens(slots, n_rows):
    t = slots.shape[1]
    return pl.pallas_call(
        functools.partial(_queue_body, n_tokens=t),
        grid=(t // SLOT_TILE,),
        in_specs=[pl.BlockSpec((2, SLOT_TILE), lambda i: (0, i), memory_space=pltpu.SMEM)],
        out_specs=pl.BlockSpec(memory_space=pltpu.SMEM),
        out_shape=jax.ShapeDtypeStruct((n_rows,), jnp.int32),
        compiler_params=_cparams("arbitrary"),
        name="moe_queue",
    )(slots)


TILES_PER_STEP = 2


def _expert_body(tile_expert_ref, n_tiles_ref, tile_valid_ref, xs_ref, *refs):
    wgu_refs = refs[0:TILES_PER_STEP]
    wd_refs = refs[TILES_PER_STEP:2 * TILES_PER_STEP]
    ys_ref = refs[2 * TILES_PER_STEP]
    first = pl.program_id(0) * TILES_PER_STEP
    used = first < n_tiles_ref[0]

    @pl.when(used)
    def _():
        rows = [slice(j * EXPERT_TILE, (j + 1) * EXPERT_TILE) for j in range(TILES_PER_STEP)]
        xs = [_tiles_to_rows(xs_ref[r]) for r in rows]
        xs = [jnp.where(lax.broadcasted_iota(jnp.int32, x.shape, 0) < tile_valid_ref[first + j], x, 0.0)
              for j, x in enumerate(xs)]
        hs = [_dot(x.astype(BF16), wgu[...]) for x, wgu in zip(xs, wgu_refs)]
        acts = [(h[:, :D_EXPERT] / (1.0 + jnp.exp(-h[:, :D_EXPERT]))) * h[:, D_EXPERT:] for h in hs]
        ys = [_dot(act.astype(BF16), wd[...]) for act, wd in zip(acts, wd_refs)]
        for r, y in zip(rows, ys):
            ys_ref[r] = _rows_to_tiles(y)

    @pl.when(jnp.logical_not(used))
    def _():
        ys_ref[...] = jnp.zeros_like(ys_ref)


def _experts(tile_expert, n_tiles, tile_valid, xs, w):
    n_steps = xs.shape[0] // (EXPERT_TILE * TILES_PER_STEP)

    def step(i, nt):
        return jnp.minimum(i, (nt[0] - 1) // TILES_PER_STEP)

    def expert(i, j, te, nt):
        return te[jnp.minimum(step(i, nt) * TILES_PER_STEP + j, nt[0] - 1)]

    tiles = lambda index: pl.BlockSpec((EXPERT_TILE * TILES_PER_STEP, ROW_CHUNKS, LANES), index)
    wgu_spec = lambda j: pl.BlockSpec((None, D_MODEL, 2 * D_EXPERT), lambda i, te, nt, tv: (expert(i, j, te, nt), 0, 0))
    wd_spec = lambda j: pl.BlockSpec((None, D_EXPERT, D_MODEL), lambda i, te, nt, tv: (expert(i, j, te, nt), 0, 0))
    return pl.pallas_call(
        _expert_body,
        grid_spec=pltpu.PrefetchScalarGridSpec(
            num_scalar_prefetch=3,
            grid=(n_steps,),
            in_specs=([tiles(lambda i, te, nt, tv: (step(i, nt), 0, 0))]
                      + [wgu_spec(j) for j in range(TILES_PER_STEP)] + [wd_spec(j) for j in range(TILES_PER_STEP)]),
            out_specs=tiles(lambda i, te, nt, tv: (i, 0, 0))),
        out_shape=jax.ShapeDtypeStruct(xs.shape, F32),
        compiler_params=_cparams("arbitrary"),
        name="moe_experts",
    )(tile_expert, n_tiles, tile_valid, xs, *([w['wgu']] * TILES_PER_STEP), *([w['wd']] * TILES_PER_STEP))


def _combine_body(h1_ref, route_ref, nfin_ref, y1_ref, y2_ref, out_ref):
    route = route_ref[...]
    lane = lax.broadcasted_iota(jnp.int32, route.shape, 1)
    w1 = jnp.sum(jnp.where(lane == ROUTE_W1, route, 0.0), axis=-1, keepdims=True)
    w2 = jnp.sum(jnp.where(lane == ROUTE_W2, route, 0.0), axis=-1, keepdims=True)
    y = w1 * _tiles_to_rows(y1_ref[...]) + w2 * _tiles_to_rows(y2_ref[...])
    out_ref[...] = _rms(h1_ref[...] + y, nfin_ref[...])


def _combine(h1, route, y12, w):
    t = h1.shape[0]
    row = lambda width: pl.BlockSpec((ROW_TILE, width), lambda i: (i, 0))
    tiles = lambda k: pl.BlockSpec((None, ROW_TILE, ROW_CHUNKS, LANES), lambda i: (k, i, 0, 0))
    return pl.pallas_call(
        _combine_body,
        grid=(t // ROW_TILE,),
        in_specs=[row(D_MODEL), row(LANES), _full_spec((1, D_MODEL)), tiles(0), tiles(1)],
        out_specs=row(D_MODEL),
        out_shape=jax.ShapeDtypeStruct((t, D_MODEL), F32),
        compiler_params=_cparams("parallel"),
        name="moe_combine",
    )(h1, route, w['norm_final'], y12, y12)


def _moe(hn, route, ids, counts, h1, w):
    t = hn.shape[0]
    n_rows = 2 * t + N_EXPERTS * EXPERT_TILE
    n_grid_tiles = n_rows // EXPERT_TILE
    count = counts[0, ROUTER_EXPERT_LANE0:ROUTER_EXPERT_LANE0 + N_EXPERTS].astype(jnp.int32)
    padded = jnp.maximum((count + EXPERT_TILE - 1) // EXPERT_TILE, 1) * EXPERT_TILE
    ends = jnp.sum(jnp.where(jnp.arange(N_EXPERTS)[:, None] <= jnp.arange(N_EXPERTS)[None, :], padded[:, None], 0),
                   axis=0)
    starts = ends - padded
    n_tiles = (ends[-1:] // EXPERT_TILE)
    tile_rows = jnp.arange(n_grid_tiles, dtype=jnp.int32) * EXPERT_TILE
    tile_expert = jnp.minimum(jnp.sum((ends[None, :] <= tile_rows[:, None]).astype(jnp.int32), axis=1), N_EXPERTS - 1)
    tile_valid = jnp.clip(jnp.sum(jnp.where(jnp.arange(N_EXPERTS)[None, :] == tile_expert[:, None],
                                            (starts + count)[None, :], 0), axis=1) - tile_rows, 0, EXPERT_TILE)
    slots = _slots(starts, ids)
    ys = _experts(tile_expert, n_tiles, tile_valid, _sc_scatter(hn, slots, n_rows), w)
    y12 = _sc_gather(ys, slots.reshape(2 * t)).reshape(2, t, ROW_CHUNKS, LANES)
    return _combine(h1, route, y12, w)


def _rope_tables(positions):
    inv_freq = 1.0 / (ROPE_THETA ** (jnp.arange(0, MLA_ROPE, 2, dtype=F32) / MLA_ROPE))
    ang = positions.astype(F32)[:, None] * inv_freq[None, :]
    cos, sin = jnp.cos(ang), jnp.sin(ang)
    reps = LANES // MLA_ROPE
    return jnp.tile(jnp.concatenate([cos, cos], axis=-1), (1, reps)), jnp.tile(jnp.concatenate([-sin, sin], axis=-1), (1, reps))


def _pack_weights(norm_mix, w_in, q_a_norm, w_uq, kv_a_norm, w_ukv, w_gate_fwd, b_gate_fwd, w_gate_bwd, b_gate_bwd,
                  gla_norm, w_out, norm_ffn, w_router_group, b_router_group, w_router_expert, b_router_expert,
                  w_expert_gate, w_expert_up, w_expert_down, norm_final):
    l = 0
    hk = GLA_HEADS * GLA_DK
    hv = GLA_HEADS * GLA_DV
    c_q, c_kv, k_pe, gq, gk, gv, lr_f, lr_b, og = jnp.split(
        w_in[l], np.cumsum([MLA_Q_RANK, MLA_KV_RANK, MLA_ROPE, hk, hk, hv, GLA_GATE_RANK, GLA_GATE_RANK])[:].tolist(),
        axis=-1)
    lr_pad = jnp.zeros((D_MODEL, LANES - 2 * GLA_GATE_RANK), F32)
    win = jnp.concatenate([c_q, c_kv, k_pe, k_pe, gq, gk, gv, og, lr_f, lr_b, lr_pad], axis=-1).astype(BF16)
    wuq = w_uq[l].reshape(MLA_Q_RANK, MLA_HEADS, MLA_NOPE + MLA_ROPE)
    wuq = jnp.concatenate([wuq[:, :, :MLA_NOPE].reshape(MLA_Q_RANK, -1), wuq[:, :, MLA_NOPE:].reshape(MLA_Q_RANK, -1)],
                          axis=-1).astype(BF16)
    wgate = jnp.zeros((LANES, 2 * hk), F32)
    wgate = wgate.at[0:GLA_GATE_RANK, 0:hk].set(w_gate_fwd[l])
    wgate = wgate.at[GLA_GATE_RANK:2 * GLA_GATE_RANK, hk:].set(w_gate_bwd[l])
    wr = jnp.zeros((D_MODEL, LANES), F32)
    wr = wr.at[:, ROUTER_GROUP_LANE0:ROUTER_GROUP_LANE0 + N_GROUPS].set(w_router_group[l])
    wr = wr.at[:, ROUTER_EXPERT_LANE0:ROUTER_EXPERT_LANE0 + N_EXPERTS].set(w_router_expert[l])
    wr_hi = wr.astype(BF16)
    br = jnp.zeros((1, LANES), F32)
    br = br.at[0, ROUTER_GROUP_LANE0:ROUTER_GROUP_LANE0 + N_GROUPS].set(b_router_group[l])
    br = br.at[0, ROUTER_EXPERT_LANE0:ROUTER_EXPERT_LANE0 + N_EXPERTS].set(b_router_expert[l])
    wgu = jnp.concatenate([w_expert_gate[l], w_expert_up[l]], axis=-1).reshape(N_EXPERTS, D_MODEL, 2 * D_EXPERT)
    return {
        'norm_mix': norm_mix[l][None], 'win': win, 'q_a_norm': q_a_norm[l][None], 'wuq': wuq,
        'kv_a_norm': kv_a_norm[l][None], 'wukv': w_ukv[l].astype(BF16),
        'wgate': wgate.astype(BF16), 'bgate': jnp.concatenate([b_gate_fwd[l], b_gate_bwd[l]])[None],
        'gla_norm': gla_norm[l][None], 'wout': w_out[l].astype(BF16), 'norm_ffn': norm_ffn[l][None],
        'wr_hi': wr_hi, 'wr_lo': (wr - wr_hi.astype(F32)).astype(BF16), 'br': br,
        'wgu': wgu.astype(BF16), 'wd': w_expert_down[l].reshape(N_EXPERTS, D_EXPERT, D_MODEL).astype(BF16),
        'norm_final': norm_final[None],
    }


def _meta_streams(meta_tokens, w):
    cos, sin = _rope_tables(jnp.arange(N_META))
    _, k, v, _, gk, gv, gf, _, _ = _inproj(meta_tokens, cos, sin, w, N_META)
    pad_keys = ((0, 0), (0, LANES - N_META), (0, 0))
    front = ((GLA_CHUNK - N_META, 0), (0, 0))
    return (jnp.pad(k, pad_keys), jnp.pad(v, pad_keys), jnp.pad(gk, front), jnp.pad(gv, front), jnp.pad(gf, front))


def _token_mixers(x, meta, w, tm, tq, tk, tb):
    bsz, seq, _ = x.shape
    km, vm, mk, mv, mg = meta
    x2d = x.reshape(bsz * seq, D_MODEL)
    cos, sin = _rope_tables(N_META + jnp.arange(seq))
    q, k, v, gq, gk, gv, gf, gb, og = _inproj(x2d, cos, sin, w, tm)
    a = _attention(q, k, v, km, vm, bsz, seq, tq, tk)
    o_f, o_b = _gla(gq, gk, gv, gf, gb, mk, mv, mg, bsz, seq, tb)
    return _mix(x2d, a, o_f, o_b, og, w, tm)


def kernel(x_prompt, x_sample, meta_tokens, norm_mix, w_in, q_a_norm, w_uq, kv_a_norm, w_ukv, w_gate_fwd, b_gate_fwd, w_gate_bwd, b_gate_bwd, gla_norm, w_out, norm_ffn, w_router_group, b_router_group, w_router_expert, b_router_expert, w_expert_gate, w_expert_up, w_expert_down, norm_final):
    w = _pack_weights(norm_mix, w_in, q_a_norm, w_uq, kv_a_norm, w_ukv, w_gate_fwd, b_gate_fwd, w_gate_bwd,
                      b_gate_bwd, gla_norm, w_out, norm_ffn, w_router_group, b_router_group, w_router_expert,
                      b_router_expert, w_expert_gate, w_expert_up, w_expert_down, norm_final)
    meta = _meta_streams(meta_tokens, w)
    outs = []
    for x in (x_prompt, x_sample):
        h1, hn, route, ids, counts = _token_mixers(x, meta, w, tm=512, tq=1024, tk=512, tb=512)
        outs.append(_moe(hn, route, ids, counts, h1, w).reshape(x.shape))
    return tuple(outs)
```

```python
import functools

import numpy as np
import jax
import jax.numpy as jnp
from jax import lax
from jax.experimental import pallas as pl
from jax.experimental.pallas import tpu as pltpu
from jax.experimental.pallas import tpu_sc as plsc

F32 = jnp.float32
BF16 = jnp.bfloat16

D_MODEL = 1024
N_META = 16
MLA_HEADS = 4
MLA_Q_RANK = 384
MLA_KV_RANK = 256
MLA_NOPE = 128
MLA_ROPE = 64
MLA_V = 128
ROPE_THETA = 10000.0
GLA_HEADS = 4
GLA_DK = 64
GLA_DV = 128
GLA_GATE_RANK = 16
GLA_TAU = 16.0
GLA_CHUNK = 64
N_GROUPS = 4
EXPERTS_PER_GROUP = 8
N_EXPERTS = N_GROUPS * EXPERTS_PER_GROUP
D_EXPERT = 256
EPS = 1e-6

LANES = 128
V7X_VMEM_BYTES = 64 * 1024 * 1024
VMEM_LIMIT = V7X_VMEM_BYTES * 7 // 8

ATTN_SCALE = (MLA_NOPE + MLA_ROPE) ** -0.5 * float(np.log2(np.e))
QK_WIDTH = 2 * LANES
V_WIDTH = 2 * LANES
ATTN_GROUP = 8

C_CQ = 0
C_CKV = C_CQ + MLA_Q_RANK
C_KPE = C_CKV + MLA_KV_RANK
C_GQ = C_KPE + LANES
C_GK = C_GQ + GLA_HEADS * GLA_DK
C_GV = C_GK + GLA_HEADS * GLA_DK
C_OG = C_GV + GLA_HEADS * GLA_DV
C_LR = C_OG + GLA_HEADS * GLA_DV
D_IN_PACKED = C_LR + LANES

ROUTER_GROUP_LANE0 = 0
ROUTER_EXPERT_LANE0 = N_GROUPS


def _cparams(*semantics):
    return pltpu.CompilerParams(dimension_semantics=semantics, vmem_limit_bytes=VMEM_LIMIT)


def _rms(x, g):
    return x * lax.rsqrt(jnp.mean(x * x, axis=-1, keepdims=True) + EPS) * g


def _dot(a, b):
    return jnp.dot(a, b, preferred_element_type=F32)


def _dot_nt(a, b):
    return lax.dot_general(a, b, (((1,), (1,)), ((), ())), preferred_element_type=F32)


def _dot_tn(a, b):
    return lax.dot_general(a, b, (((0,), (0,)), ((), ())), preferred_element_type=F32)


def _full_spec(shape):
    return pl.BlockSpec(shape, lambda *_: (0,) * len(shape))


SUBLANES = 8
ROW_CHUNKS = D_MODEL // LANES
assert ROW_CHUNKS == SUBLANES


def _rows_to_tiles(x):
    chunks = jnp.stack([x[:, s * LANES:(s + 1) * LANES] for s in range(ROW_CHUNKS)], axis=0)
    return pltpu.einshape("smd->msd", chunks)


def _tiles_to_rows(x):
    chunks = pltpu.einshape("msd->smd", x)
    return jnp.concatenate([chunks[s] for s in range(ROW_CHUNKS)], axis=-1)


def _rope_pairs(x, cos, sin_signed, first_half):
    swapped = jnp.where(first_half, pltpu.roll(x, LANES - MLA_ROPE // 2, 1), pltpu.roll(x, MLA_ROPE // 2, 1))
    return x * cos + swapped * sin_signed


def _inproj_body(x_ref, cos_ref, sin_ref, nmix_ref, win_ref, qan_ref, wuq_ref, kvan_ref, wukv_ref,
                 wgate_ref, bgate_ref,
                 q_ref, k_ref, v_ref, gq_ref, gk_ref, gv_ref, gf_ref, gb_ref, og_ref):
    hn = _rms(x_ref[...], nmix_ref[...]).astype(BF16)

    def proj(lo, hi):
        return _dot(hn, win_ref[:, lo:hi])

    cos = cos_ref[...]
    sin = sin_ref[...]
    lane = lax.broadcasted_iota(jnp.int32, cos.shape, 1)
    first_half = (lane & (MLA_ROPE - 1)) < MLA_ROPE // 2
    low_lanes = lane < MLA_ROPE

    cq = _rms(proj(C_CQ, C_CKV), qan_ref[...]).astype(BF16)
    qn = _dot(cq, wuq_ref[:, 0:MLA_HEADS * MLA_NOPE]) * ATTN_SCALE
    qr = _dot(cq, wuq_ref[:, MLA_HEADS * MLA_NOPE:])
    for j in range(MLA_HEADS // 2):
        rj = (_rope_pairs(qr[:, j * LANES:(j + 1) * LANES], cos, sin, first_half) * ATTN_SCALE).astype(BF16)
        for h in (2 * j, 2 * j + 1):
            q_ref[h, :, 0:LANES] = qn[:, h * LANES:(h + 1) * LANES].astype(BF16)
            q_ref[h, :, LANES:QK_WIDTH] = rj

    ckv = _rms(proj(C_CKV, C_KPE), kvan_ref[...]).astype(BF16)
    kv = _dot(ckv, wukv_ref[...])
    kr = _rope_pairs(proj(C_KPE, C_GQ), cos, sin, first_half)
    kr_even = jnp.where(low_lanes, kr, 0.0).astype(BF16)
    kr_odd = jnp.where(low_lanes, 0.0, kr).astype(BF16)
    for h in range(MLA_HEADS):
        base = h * (MLA_NOPE + MLA_V)
        k_ref[h, :, 0:LANES] = kv[:, base:base + MLA_NOPE].astype(BF16)
        k_ref[h, :, LANES:QK_WIDTH] = kr_even if h % 2 == 0 else kr_odd
        v_ref[h, :, 0:MLA_V] = kv[:, base + MLA_NOPE:base + MLA_NOPE + MLA_V].astype(BF16)
        v_ref[h, :, MLA_V:V_WIDTH] = jnp.ones((kv.shape[0], V_WIDTH - MLA_V), BF16)

    gq_ref[...] = (proj(C_GQ, C_GK) * (GLA_DK ** -0.5)).astype(BF16)
    gk_ref[...] = proj(C_GK, C_GV).astype(BF16)
    gv_ref[...] = proj(C_GV, C_OG).astype(BF16)
    og_ref[...] = proj(C_OG, C_LR).astype(BF16)
    pre = _dot(proj(C_LR, D_IN_PACKED).astype(BF16), wgate_ref[...]) + bgate_ref[...]
    logsig = jnp.minimum(pre, 0.0) - jnp.log1p(jnp.exp(-jnp.abs(pre)))
    gates = logsig * (1.0 / GLA_TAU)
    gf_ref[...] = gates[:, 0:GLA_HEADS * GLA_DK]
    gb_ref[...] = gates[:, GLA_HEADS * GLA_DK:]


def _inproj(x2d, cos, sin, w, tm):
    t = x2d.shape[0]
    blocks_per_seq = cos.shape[0] // tm
    hk = GLA_HEADS * GLA_DK
    hv = GLA_HEADS * GLA_DV
    row = lambda width: pl.BlockSpec((tm, width), lambda i: (i, 0))
    head_rows = lambda width: pl.BlockSpec((MLA_HEADS, tm, width), lambda i: (0, i, 0))
    tab = pl.BlockSpec((tm, LANES), lambda i: (i % blocks_per_seq, 0))
    out_shape = (
        jax.ShapeDtypeStruct((MLA_HEADS, t, QK_WIDTH), BF16),
        jax.ShapeDtypeStruct((MLA_HEADS, t, QK_WIDTH), BF16),
        jax.ShapeDtypeStruct((MLA_HEADS, t, V_WIDTH), BF16),
        jax.ShapeDtypeStruct((t, hk), BF16),
        jax.ShapeDtypeStruct((t, hk), BF16),
        jax.ShapeDtypeStruct((t, hv), BF16),
        jax.ShapeDtypeStruct((t, hk), F32),
        jax.ShapeDtypeStruct((t, hk), F32),
        jax.ShapeDtypeStruct((t, hv), BF16),
    )
    return pl.pallas_call(
        _inproj_body,
        grid=(t // tm,),
        in_specs=[row(D_MODEL), tab, tab,
                  _full_spec((1, D_MODEL)), _full_spec((D_MODEL, D_IN_PACKED)),
                  _full_spec((1, MLA_Q_RANK)), _full_spec(w['wuq'].shape),
                  _full_spec((1, MLA_KV_RANK)), _full_spec(w['wukv'].shape),
                  _full_spec(w['wgate'].shape), _full_spec(w['bgate'].shape)],
        out_specs=(head_rows(QK_WIDTH), head_rows(QK_WIDTH), head_rows(V_WIDTH),
                   row(hk), row(hk), row(hv), row(hk), row(hk), row(hv)),
        out_shape=out_shape,
        compiler_params=_cparams("parallel"),
        name="inproj",
    )(x2d, cos, sin, w['norm_mix'], w['win'], w['q_a_norm'], w['wuq'], w['kv_a_norm'], w['wukv'],
      w['wgate'], w['bgate'])


def _attn_body(q_ref, k_ref, v_ref, km_ref, vm_ref, o_ref, s_ref, acc_ref, *, tk):
    q = q_ref[...]
    n_groups = k_ref.shape[0] // (ATTN_GROUP * tk)

    def scores(j):
        return _dot_nt(q, k_ref[pl.ds(pl.multiple_of(j * tk, tk), tk), :])

    def values(j):
        return v_ref[pl.ds(pl.multiple_of(j * tk, tk), tk), :]

    def absorb(m, s, v):
        m_new = jnp.maximum(m, jnp.max(s, axis=-1, keepdims=True))
        p = jnp.exp2(s - m_new)
        acc_ref[...] = jnp.exp2(m - m_new) * acc_ref[...] + _dot(p.astype(BF16), v)
        return m_new

    s_ref[0] = scores(0)
    sm = _dot_nt(q, km_ref[...])
    sm = jnp.where(lax.broadcasted_iota(jnp.int32, sm.shape, 1) < N_META, sm, -jnp.inf)
    m = jnp.max(sm, axis=-1, keepdims=True)
    acc_ref[...] = _dot(jnp.exp2(sm - m).astype(BF16), vm_ref[...])

    def group(g, m, last):
        for i in range(ATTN_GROUP):
            j = ATTN_GROUP * g + i
            s = s_ref[i % 2]
            if not (last and i == ATTN_GROUP - 1):
                s_ref[(i + 1) % 2] = scores(j + 1)
            m = absorb(m, s, values(j))
        return m

    m = lax.fori_loop(0, n_groups - 1, lambda g, m: group(g, m, False), m)
    group(n_groups - 1, m, True)
    acc = acc_ref[...]
    o_ref[...] = (acc[:, :MLA_V] / acc[:, MLA_V:]).astype(o_ref.dtype)


def _attention(q, k, v, km, vm, bsz, seq, tq, tk):
    nq = seq // tq
    return pl.pallas_call(
        functools.partial(_attn_body, tk=tk),
        grid=(bsz, MLA_HEADS, nq),
        in_specs=[pl.BlockSpec((None, tq, QK_WIDTH), lambda b, h, i: (h, b * nq + i, 0)),
                  pl.BlockSpec((None, seq, QK_WIDTH), lambda b, h, i: (h, b, 0)),
                  pl.BlockSpec((None, seq, V_WIDTH), lambda b, h, i: (h, b, 0)),
                  pl.BlockSpec((None, LANES, QK_WIDTH), lambda b, h, i: (h, 0, 0)),
                  pl.BlockSpec((None, LANES, V_WIDTH), lambda b, h, i: (h, 0, 0))],
        out_specs=pl.BlockSpec((tq, MLA_V), lambda b, h, i: (b * nq + i, h)),
        out_shape=jax.ShapeDtypeStruct((bsz * seq, MLA_HEADS * MLA_V), BF16),
        scratch_shapes=[pltpu.VMEM((2, tq, tk), F32), pltpu.VMEM((tq, V_WIDTH), F32)],
        compiler_params=_cparams("parallel", "parallel", "arbitrary"),
        name="mla_attention",
    )(q, k, v, km, vm)


def _split3(x):
    hi = x.astype(BF16)
    r1 = x - hi.astype(F32)
    mid = r1.astype(BF16)
    lo = (r1 - mid.astype(F32)).astype(BF16)
    return hi, mid, lo


def _gla_log_decay(g, tri):
    g_hi, g_mid, g_lo = _split3(g)
    return _dot(tri, g_hi) + _dot(tri, g_mid) + _dot(tri, g_lo)


def _head_rows(x):
    even = lax.broadcasted_iota(jnp.int32, x.shape, 1) < GLA_DK
    return jnp.concatenate([jnp.where(even, x, 0.0), jnp.where(even, 0.0, x)], axis=0).astype(BF16)


def _gla_operands(q, k, b, mid, last):
    b_last = b[last:last + 1, :]
    ke = (k * jnp.exp(b_last - b)).astype(BF16)
    if q is None:
        return jnp.exp(b_last), ke, None, None, None
    b_mid = b[mid:mid + 1, :]
    ks = (k * jnp.exp(b_mid - b)).astype(BF16)
    return jnp.exp(b_last), ke, ks, _head_rows(q * jnp.exp(b - b_mid)), _head_rows(q * jnp.exp(b))


def _gla_state_update(v_even, v_odd, ke):
    return jnp.where(lax.broadcasted_iota(jnp.int32, (GLA_DV, LANES), 1) < GLA_DK,
                     _dot_tn(v_even, ke), _dot_tn(v_odd, ke))


def _gla_intra(v_even, v_odd, ks, qs2, keep):
    scores = jnp.where(keep, _dot_nt(qs2, ks), 0.0).astype(BF16)
    return _dot(scores[:GLA_CHUNK], v_even), _dot(scores[GLA_CHUNK:], v_odd)


def _gla_body(qf_ref, kf_ref, vf_ref, gf_ref, qb_ref, kb_ref, vb_ref, gb_ref, mk_ref, mv_ref, mg_ref,
              of_ref, ob_ref, state_ref):
    n_chunks = qf_ref.shape[0] // GLA_CHUNK
    n_pairs = GLA_HEADS // 2
    r = lax.broadcasted_iota(jnp.int32, (GLA_CHUNK, GLA_CHUNK), 0)
    c = lax.broadcasted_iota(jnp.int32, (GLA_CHUNK, GLA_CHUNK), 1)
    tri_f = jnp.where(c <= r, 1.0, 0.0).astype(BF16)
    tri_b = jnp.where(c >= r, 1.0, 0.0).astype(BF16)
    r2 = lax.broadcasted_iota(jnp.int32, (2 * GLA_CHUNK, GLA_CHUNK), 0) & (GLA_CHUNK - 1)
    c2 = lax.broadcasted_iota(jnp.int32, (2 * GLA_CHUNK, GLA_CHUNK), 1)
    keep_f = c2 <= r2
    keep_b = c2 >= r2
    mid_f, last_f = GLA_CHUNK // 2 - 1, GLA_CHUNK - 1
    mid_b, last_b = GLA_CHUNK // 2, 0

    def pair_cols(p):
        return slice(p * LANES, (p + 1) * LANES)

    def head_cols(h):
        return slice(h * GLA_DV, (h + 1) * GLA_DV)

    @pl.when(pl.program_id(1) == 0)
    def _():
        for p in range(n_pairs):
            b = _gla_log_decay(mg_ref[:, pair_cols(p)], tri_f)
            _, ke, _, _, _ = _gla_operands(None, mk_ref[:, pair_cols(p)], b, mid_f, last_f)
            state_ref[p] = _gla_state_update(mv_ref[:, head_cols(2 * p)].astype(BF16),
                                             mv_ref[:, head_cols(2 * p + 1)].astype(BF16), ke)
            state_ref[n_pairs + p] = jnp.zeros((GLA_DV, LANES), F32)

    scans = []
    for p in range(n_pairs):
        scans.append((p, list(range(n_chunks)), qf_ref, kf_ref, vf_ref, gf_ref, of_ref, p,
                      tri_f, keep_f, mid_f, last_f))
        scans.append((p, list(reversed(range(n_chunks))), qb_ref, kb_ref, vb_ref, gb_ref, ob_ref, n_pairs + p,
                      tri_b, keep_b, mid_b, last_b))

    def rows(c):
        return slice(c * GLA_CHUNK, (c + 1) * GLA_CHUNK)

    def values(v_ref, p, c):
        return (v_ref[rows(c), head_cols(2 * p)].astype(BF16), v_ref[rows(c), head_cols(2 * p + 1)].astype(BF16))

    log_decay = [[_gla_log_decay(g_ref[rows(c), pair_cols(p)], tri) for c in order]
                 for (p, order, _, _, _, g_ref, _, _, tri, _, _, _) in scans]
    operands = [[_gla_operands(q_ref[rows(c), pair_cols(p)], k_ref[rows(c), pair_cols(p)], b, mid, last)
                 for c, b in zip(order, bs)]
                for (p, order, q_ref, k_ref, _, _, _, _, _, _, mid, last), bs in zip(scans, log_decay)]
    updates = [[_gla_state_update(*values(v_ref, p, c), ops[1]) for c, ops in zip(order, opss)]
               for (p, order, _, _, v_ref, _, _, _, _, _, _, _), opss in zip(scans, operands)]
    intra = [[_gla_intra(*values(v_ref, p, c), ops[2], ops[3], keep) for c, ops in zip(order, opss)]
             for (p, order, _, _, v_ref, _, _, _, _, keep, _, _), opss in zip(scans, operands)]
    states = []
    for (_, order, _, _, _, _, _, slot, _, _, _, _), opss, upds in zip(scans, operands, updates):
        st = state_ref[slot]
        entering = []
        for ops, upd in zip(opss, upds):
            entering.append(st.astype(BF16))
            st = st * ops[0] + upd
        state_ref[slot] = st
        states.append(entering)
    for (p, order, _, _, _, _, o_ref, _, _, _, _, _), opss, sts, locs in zip(scans, operands, states, intra):
        for c, ops, st, (o_even, o_odd) in zip(order, opss, sts, locs):
            inter = _dot_nt(ops[4], st)
            o_ref[rows(c), head_cols(2 * p)] = (o_even + inter[:GLA_CHUNK]).astype(o_ref.dtype)
            o_ref[rows(c), head_cols(2 * p + 1)] = (o_odd + inter[GLA_CHUNK:]).astype(o_ref.dtype)


def _gla(gq, gk, gv, gf, gb, mk, mv, mg, bsz, seq, tb):
    nb = seq // tb
    hk = GLA_HEADS * GLA_DK
    hv = GLA_HEADS * GLA_DV
    fwd = lambda width: pl.BlockSpec((tb, width), lambda b, j: (b * nb + j, 0))
    bwd = lambda width: pl.BlockSpec((tb, width), lambda b, j: (b * nb + nb - 1 - j, 0))
    t = bsz * seq
    return pl.pallas_call(
        _gla_body,
        grid=(bsz, nb),
        in_specs=[fwd(hk), fwd(hk), fwd(hv), fwd(hk), bwd(hk), bwd(hk), bwd(hv), bwd(hk),
                  _full_spec(mk.shape), _full_spec(mv.shape), _full_spec(mg.shape)],
        out_specs=(fwd(hv), bwd(hv)),
        out_shape=(jax.ShapeDtypeStruct((t, hv), BF16), jax.ShapeDtypeStruct((t, hv), BF16)),
        scratch_shapes=[pltpu.VMEM((2 * (GLA_HEADS // 2), GLA_DV, LANES), F32)],
        compiler_params=_cparams("parallel", "arbitrary"),
        name="gla_scan",
    )(gq, gk, gv, gf, gq, gk, gv, gb, mk, mv, mg)


def _mix_body(x_ref, a_ref, of_ref, ob_ref, og_ref, gnorm_ref, wout_ref, nffn_ref, wr_hi_ref, wr_lo_ref, br_ref,
              h1_ref, hn_ref, route_ref, ids_ref, count_ref, tri_ref):
    tm = x_ref.shape[0]

    @pl.when(pl.program_id(0) == 0)
    def _():
        r = lax.broadcasted_iota(jnp.int32, (tm, tm), 0)
        c = lax.broadcasted_iota(jnp.int32, (tm, tm), 1)
        tri_ref[...] = jnp.where(c < r, 1.0, 0.0).astype(BF16)
        count_ref[...] = jnp.zeros_like(count_ref)

    a_width = MLA_HEADS * MLA_V
    h1 = x_ref[...] + _dot(a_ref[...], wout_ref[0:a_width, :])
    for h in range(GLA_HEADS):
        cols = slice(h * GLA_DV, (h + 1) * GLA_DV)
        o = of_ref[:, cols].astype(F32) + ob_ref[:, cols].astype(F32)
        og = og_ref[:, cols].astype(F32)
        silu = og / (1.0 + jnp.exp(-og))
        gh = (_rms(o, gnorm_ref[...]) * silu).astype(BF16)
        h1 = h1 + _dot(gh, wout_ref[a_width + h * GLA_DV:a_width + (h + 1) * GLA_DV, :])
    h1_ref[...] = h1
    hn = _rms(h1, nffn_ref[...])
    hn_ref[...] = _rows_to_tiles(hn)

    hn_hi = hn.astype(BF16)
    hn_lo = (hn - hn_hi.astype(F32)).astype(BF16)
    logits = (_dot(hn_hi, wr_hi_ref[...]) + _dot(hn_lo, wr_hi_ref[...]) + _dot(hn_hi, wr_lo_ref[...])
              + br_ref[...])
    lane = lax.broadcasted_iota(jnp.int32, logits.shape, 1).astype(F32)
    none = float(LANES)
    neg = -jnp.inf

    def lane_max(x):
        return jnp.max(x, axis=-1, keepdims=True)

    def lane_sum(x):
        return jnp.sum(x, axis=-1, keepdims=True)

    def first_lane(mask):
        return jnp.min(jnp.where(mask, lane, none), axis=-1, keepdims=True)

    is_group = lane < float(N_GROUPS)
    g_max = lane_max(jnp.where(is_group, logits, neg))
    g_exp = jnp.where(is_group, jnp.exp(logits - g_max), 0.0)
    g_prob = g_exp / lane_sum(g_exp)
    g_w = lane_max(g_prob)
    g_idx = first_lane(is_group & (g_prob == g_w))
    e_lo = float(ROUTER_EXPERT_LANE0) + float(EXPERTS_PER_GROUP) * g_idx
    sel = (lane >= e_lo) & (lane < e_lo + float(EXPERTS_PER_GROUP))
    e_max = lane_max(jnp.where(sel, logits, neg))
    e_exp = jnp.where(sel, jnp.exp(logits - e_max), 0.0)
    e_prob = e_exp / lane_sum(e_exp)
    p1 = lane_max(jnp.where(sel, e_prob, neg))
    i1 = first_lane(sel & (e_prob == p1))
    rest = sel & (lane != i1)
    p2 = lane_max(jnp.where(rest, e_prob, neg))
    i2 = first_lane(rest & (e_prob == p2))
    denom = p1 + p2
    chosen = jnp.where((lane == i1) | (lane == i2), 1.0, 0.0)
    rank = count_ref[...] + _dot(tri_ref[...], chosen.astype(BF16))
    count_ref[...] += jnp.sum(chosen, axis=0, keepdims=True)
    fields = (i1 - float(ROUTER_EXPERT_LANE0), i2 - float(ROUTER_EXPERT_LANE0),
              lane_sum(jnp.where(lane == i1, rank, 0.0)), lane_sum(jnp.where(lane == i2, rank, 0.0)),
              g_w * (p1 / denom), g_w * (p2 / denom))
    route = jnp.zeros_like(logits)
    for k, value in enumerate(fields):
        route = jnp.where(lane == float(k), value, route)
    route_ref[...] = route
    ids_ref[...] = jnp.transpose(route)[0:ROUTE_ID_ROWS, :].astype(jnp.int32)


ROUTE_E1, ROUTE_E2, ROUTE_RANK1, ROUTE_RANK2, ROUTE_W1, ROUTE_W2 = range(6)
ROUTE_ID_ROWS = 8


def _mix(x2d, a, o_f, o_b, og, w, tm):
    t = x2d.shape[0]
    hv = GLA_HEADS * GLA_DV
    row = lambda width: pl.BlockSpec((tm, width), lambda i: (i, 0))
    return pl.pallas_call(
        _mix_body,
        grid=(t // tm,),
        in_specs=[row(D_MODEL), row(MLA_HEADS * MLA_V), row(hv), row(hv), row(hv),
                  _full_spec((1, GLA_DV)), _full_spec(w['wout'].shape), _full_spec((1, D_MODEL)),
                  _full_spec(w['wr_hi'].shape), _full_spec(w['wr_lo'].shape), _full_spec(w['br'].shape)],
        out_specs=(row(D_MODEL), pl.BlockSpec((tm, ROW_CHUNKS, LANES), lambda i: (i, 0, 0)), row(LANES),
                   pl.BlockSpec((ROUTE_ID_ROWS, tm), lambda i: (0, i)), _full_spec((1, LANES))),
        out_shape=(jax.ShapeDtypeStruct((t, D_MODEL), F32), jax.ShapeDtypeStruct((t, ROW_CHUNKS, LANES), F32),
                   jax.ShapeDtypeStruct((t, LANES), F32), jax.ShapeDtypeStruct((ROUTE_ID_ROWS, t), jnp.int32),
                   jax.ShapeDtypeStruct((1, LANES), F32)),
        scratch_shapes=[pltpu.VMEM((tm, tm), BF16)],
        compiler_params=_cparams("arbitrary"),
        name="mix_router",
    )(x2d, a, o_f, o_b, og, w['gla_norm'], w['wout'], w['norm_ffn'], w['wr_hi'], w['wr_lo'], w['br'])


EXPERT_TILE = 256
ROW_TILE = 256


SC_CORES = 2
SC_SUBCORES = 16
SC_GATHER_ROWS = 32


def _sc_gather(table, idx):
    n = idx.shape[0]
    workers = SC_CORES * SC_SUBCORES
    per_worker = n // workers
    assert n % (workers * SC_GATHER_ROWS) == 0
    mesh = plsc.VectorSubcoreMesh(core_axis_name="c", subcore_axis_name="s")

    @functools.partial(
        pl.kernel, mesh=mesh,
        out_type=jax.ShapeDtypeStruct((n,) + table.shape[1:], table.dtype),
        scratch_types=[pltpu.VMEM((SC_GATHER_ROWS,), jnp.int32),
                       pltpu.VMEM((SC_GATHER_ROWS,) + table.shape[1:], table.dtype),
                       pltpu.SemaphoreType.DMA])
    def gather(table_ref, idx_ref, out_ref, idx_buf, rows_buf, sem):
        base = (lax.axis_index("s") * SC_CORES + lax.axis_index("c")) * per_worker

        @pl.loop(0, per_worker // SC_GATHER_ROWS)
        def _(j):
            rows = pl.ds(base + j * SC_GATHER_ROWS, SC_GATHER_ROWS)
            pltpu.sync_copy(idx_ref.at[rows], idx_buf)
            pltpu.async_copy(table_ref.at[idx_buf], rows_buf, sem).wait()
            pltpu.sync_copy(rows_buf, out_ref.at[rows])

    return gather(table, idx)


def _sc_scatter(rows, idx, n_out):
    copies, n = idx.shape
    idx = idx.reshape(copies * n)
    workers = SC_CORES * SC_SUBCORES
    per_worker = n // workers
    assert n % (workers * SC_GATHER_ROWS) == 0
    mesh = plsc.VectorSubcoreMesh(core_axis_name="c", subcore_axis_name="s")

    @functools.partial(
        pl.kernel, mesh=mesh,
        out_type=jax.ShapeDtypeStruct((n_out,) + rows.shape[1:], rows.dtype),
        scratch_types=[pltpu.VMEM((SC_GATHER_ROWS,), jnp.int32),
                       pltpu.VMEM((SC_GATHER_ROWS,) + rows.shape[1:], rows.dtype),
                       pltpu.SemaphoreType.DMA])
    def scatter(rows_ref, idx_ref, out_ref, idx_buf, rows_buf, sem):
        base = (lax.axis_index("s") * SC_CORES + lax.axis_index("c")) * per_worker

        @pl.loop(0, per_worker // SC_GATHER_ROWS)
        def _(j):
            first = base + j * SC_GATHER_ROWS
            pltpu.sync_copy(rows_ref.at[pl.ds(first, SC_GATHER_ROWS)], rows_buf)
            for k in range(copies):
                pltpu.sync_copy(idx_ref.at[pl.ds(k * n + first, SC_GATHER_ROWS)], idx_buf)
                pltpu.async_copy(rows_buf, out_ref.at[idx_buf], sem).wait()

    return scatter(rows, idx)


SLOT_TILE = 2048


def _slots_body(starts_ref, ids_ref, slots_ref):
    ids = ids_ref[...]
    experts = ids[ROUTE_E1:ROUTE_E1 + 2, :]
    start = jnp.zeros_like(experts)
    for e in range(N_EXPERTS):
        start = jnp.where(experts == e, starts_ref[e], start)
    slots_ref[...] = start + ids[ROUTE_RANK1:ROUTE_RANK1 + 2, :]


def _slots(starts, ids):
    t = ids.shape[1]
    return pl.pallas_call(
        _slots_body,
        grid_spec=pltpu.PrefetchScalarGridSpec(
            num_scalar_prefetch=1,
            grid=(t // SLOT_TILE,),
            in_specs=[pl.BlockSpec((ROUTE_ID_ROWS, SLOT_TILE), lambda i, *_: (0, i))],
            out_specs=pl.BlockSpec((2, SLOT_TILE), lambda i, *_: (0, i))),
        out_shape=jax.ShapeDtypeStruct((2, t), jnp.int32),
        compiler_params=_cparams("parallel"),
        name="moe_slots",
    )(starts, ids)


def _queue_body(slots_ref, token_ref, *, n_tokens):
    i = pl.program_id(0)

    @pl.when(i == 0)
    def _():
        for base in range(0, token_ref.shape[0], n_tokens):
            def clear(j, carry):
                token_ref[base + j] = j
                return carry
            lax.fori_loop(0, min(n_tokens, token_ref.shape[0] - base), clear, 0, unroll=8)

    def place(r, carry):
        for k in range(2):
            token_ref[slots_ref[k, r]] = i * SLOT_TILE + r
        return carry

    lax.fori_loop(0, SLOT_TILE, place, 0, unroll=8)


def _queue_tokens(slots, n_rows):
    t = slots.shape[1]
    return pl.pallas_call(
        functools.partial(_queue_body, n_tokens=t),
        grid=(t // SLOT_TILE,),
        in_specs=[pl.BlockSpec((2, SLOT_TILE), lambda i: (0, i), memory_space=pltpu.SMEM)],
        out_specs=pl.BlockSpec(memory_space=pltpu.SMEM),
        out_shape=jax.ShapeDtypeStruct((n_rows,), jnp.int32),
        compiler_params=_cparams("arbitrary"),
        name="moe_queue",
    )(slots)


TILES_PER_STEP = 2


def _expert_body(tile_expert_ref, n_tiles_ref, tile_valid_ref, xs_ref, *refs):
    wg_refs = refs[0:TILES_PER_STEP]
    wu_refs = refs[TILES_PER_STEP:2 * TILES_PER_STEP]
    wdown_refs = refs[2 * TILES_PER_STEP:3 * TILES_PER_STEP]
    ys_ref, wgu_bf_ref, wd_bf_ref = refs[3 * TILES_PER_STEP:]
    i = pl.program_id(0)
    first = i * TILES_PER_STEP
    used = first < n_tiles_ref[0]

    def expert_of(tile):
        return tile_expert_ref[jnp.clip(tile, 0, n_tiles_ref[0] - 1)]

    for j in range(TILES_PER_STEP):
        changed = jnp.logical_or(i == 0, expert_of(first + j) != expert_of(first - TILES_PER_STEP + j))

        @pl.when(jnp.logical_and(used, changed))
        def _():
            wgu_bf_ref[j, :, 0:D_EXPERT] = wg_refs[j][...].astype(BF16)
            wgu_bf_ref[j, :, D_EXPERT:] = wu_refs[j][...].astype(BF16)
            wd_bf_ref[j] = wdown_refs[j][...].astype(BF16)

    wgu_refs = [wgu_bf_ref.at[j] for j in range(TILES_PER_STEP)]
    wd_refs = [wd_bf_ref.at[j] for j in range(TILES_PER_STEP)]

    @pl.when(used)
    def _():
        rows = [slice(j * EXPERT_TILE, (j + 1) * EXPERT_TILE) for j in range(TILES_PER_STEP)]
        xs = [_tiles_to_rows(xs_ref[r]) for r in rows]
        xs = [jnp.where(lax.broadcasted_iota(jnp.int32, x.shape, 0) < tile_valid_ref[first + j], x, 0.0)
              for j, x in enumerate(xs)]
        hs = [_dot(x.astype(BF16), wgu[...]) for x, wgu in zip(xs, wgu_refs)]
        acts = [(h[:, :D_EXPERT] / (1.0 + jnp.exp(-h[:, :D_EXPERT]))) * h[:, D_EXPERT:] for h in hs]
        ys = [_dot(act.astype(BF16), wd[...]) for act, wd in zip(acts, wd_refs)]
        for r, y in zip(rows, ys):
            ys_ref[r] = _rows_to_tiles(y)

    @pl.when(jnp.logical_not(used))
    def _():
        ys_ref[...] = jnp.zeros_like(ys_ref)


def _experts(tile_expert, n_tiles, tile_valid, xs, w):
    n_steps = xs.shape[0] // (EXPERT_TILE * TILES_PER_STEP)

    def step(i, nt):
        return jnp.minimum(i, (nt[0] - 1) // TILES_PER_STEP)

    def expert(i, j, te, nt):
        return te[jnp.minimum(step(i, nt) * TILES_PER_STEP + j, nt[0] - 1)]

    tiles = lambda index: pl.BlockSpec((EXPERT_TILE * TILES_PER_STEP, ROW_CHUNKS, LANES), index)
    up_spec = lambda j: pl.BlockSpec((None, D_MODEL, D_EXPERT), lambda i, te, nt, tv: (expert(i, j, te, nt), 0, 0))
    down_spec = lambda j: pl.BlockSpec((None, D_EXPERT, D_MODEL), lambda i, te, nt, tv: (expert(i, j, te, nt), 0, 0))
    slots = range(TILES_PER_STEP)
    return pl.pallas_call(
        _expert_body,
        grid_spec=pltpu.PrefetchScalarGridSpec(
            num_scalar_prefetch=3,
            grid=(n_steps,),
            in_specs=([tiles(lambda i, te, nt, tv: (step(i, nt), 0, 0))]
                      + [up_spec(j) for j in slots] + [up_spec(j) for j in slots] + [down_spec(j) for j in slots]),
            out_specs=tiles(lambda i, te, nt, tv: (i, 0, 0)),
            scratch_shapes=[pltpu.VMEM((TILES_PER_STEP, D_MODEL, 2 * D_EXPERT), BF16),
                            pltpu.VMEM((TILES_PER_STEP, D_EXPERT, D_MODEL), BF16)]),
        out_shape=jax.ShapeDtypeStruct(xs.shape, F32),
        compiler_params=_cparams("arbitrary"),
        name="moe_experts",
    )(tile_expert, n_tiles, tile_valid, xs, *([w['w_gate']] * TILES_PER_STEP), *([w['w_up']] * TILES_PER_STEP),
      *([w['w_down']] * TILES_PER_STEP))


def _combine_body(h1_ref, route_ref, nfin_ref, y1_ref, y2_ref, out_ref):
    route = route_ref[...]
    lane = lax.broadcasted_iota(jnp.int32, route.shape, 1)
    w1 = jnp.sum(jnp.where(lane == ROUTE_W1, route, 0.0), axis=-1, keepdims=True)
    w2 = jnp.sum(jnp.where(lane == ROUTE_W2, route, 0.0), axis=-1, keepdims=True)
    y = w1 * _tiles_to_rows(y1_ref[...]) + w2 * _tiles_to_rows(y2_ref[...])
    out_ref[...] = _rms(h1_ref[...] + y, nfin_ref[...])


def _combine(h1, route, y12, w):
    t = h1.shape[0]
    row = lambda width: pl.BlockSpec((ROW_TILE, width), lambda i: (i, 0))
    tiles = lambda k: pl.BlockSpec((None, ROW_TILE, ROW_CHUNKS, LANES), lambda i: (k, i, 0, 0))
    return pl.pallas_call(
        _combine_body,
        grid=(t // ROW_TILE,),
        in_specs=[row(D_MODEL), row(LANES), _full_spec((1, D_MODEL)), tiles(0), tiles(1)],
        out_specs=row(D_MODEL),
        out_shape=jax.ShapeDtypeStruct((t, D_MODEL), F32),
        compiler_params=_cparams("parallel"),
        name="moe_combine",
    )(h1, route, w['norm_final'], y12, y12)


def _moe(hn, route, ids, counts, h1, w):
    t = hn.shape[0]
    n_rows = 2 * t + N_EXPERTS * EXPERT_TILE
    n_grid_tiles = n_rows // EXPERT_TILE
    count = counts[0, ROUTER_EXPERT_LANE0:ROUTER_EXPERT_LANE0 + N_EXPERTS].astype(jnp.int32)
    padded = jnp.maximum((count + EXPERT_TILE - 1) // EXPERT_TILE, 1) * EXPERT_TILE
    ends = jnp.sum(jnp.where(jnp.arange(N_EXPERTS)[:, None] <= jnp.arange(N_EXPERTS)[None, :], padded[:, None], 0),
                   axis=0)
    starts = ends - padded
    n_tiles = (ends[-1:] // EXPERT_TILE)
    tile_rows = jnp.arange(n_grid_tiles, dtype=jnp.int32) * EXPERT_TILE
    tile_expert = jnp.minimum(jnp.sum((ends[None, :] <= tile_rows[:, None]).astype(jnp.int32), axis=1), N_EXPERTS - 1)
    tile_valid = jnp.clip(jnp.sum(jnp.where(jnp.arange(N_EXPERTS)[None, :] == tile_expert[:, None],
                                            (starts + count)[None, :], 0), axis=1) - tile_rows, 0, EXPERT_TILE)
    slots = _slots(starts, ids)
    ys = _experts(tile_expert, n_tiles, tile_valid, _sc_scatter(hn, slots, n_rows), w)
    y12 = _sc_gather(ys, slots.reshape(2 * t)).reshape(2, t, ROW_CHUNKS, LANES)
    return _combine(h1, route, y12, w)


def _rope_tables(positions):
    inv_freq = 1.0 / (ROPE_THETA ** (jnp.arange(0, MLA_ROPE, 2, dtype=F32) / MLA_ROPE))
    ang = positions.astype(F32)[:, None] * inv_freq[None, :]
    cos, sin = jnp.cos(ang), jnp.sin(ang)
    reps = LANES // MLA_ROPE
    return jnp.tile(jnp.concatenate([cos, cos], axis=-1), (1, reps)), jnp.tile(jnp.concatenate([-sin, sin], axis=-1), (1, reps))


def _pack_weights(norm_mix, w_in, q_a_norm, w_uq, kv_a_norm, w_ukv, w_gate_fwd, b_gate_fwd, w_gate_bwd, b_gate_bwd,
                  gla_norm, w_out, norm_ffn, w_router_group, b_router_group, w_router_expert, b_router_expert,
                  w_expert_gate, w_expert_up, w_expert_down, norm_final):
    l = 0
    hk = GLA_HEADS * GLA_DK
    hv = GLA_HEADS * GLA_DV
    c_q, c_kv, k_pe, gq, gk, gv, lr_f, lr_b, og = jnp.split(
        w_in[l], np.cumsum([MLA_Q_RANK, MLA_KV_RANK, MLA_ROPE, hk, hk, hv, GLA_GATE_RANK, GLA_GATE_RANK])[:].tolist(),
        axis=-1)
    lr_pad = jnp.zeros((D_MODEL, LANES - 2 * GLA_GATE_RANK), F32)
    win = jnp.concatenate([c_q, c_kv, k_pe, k_pe, gq, gk, gv, og, lr_f, lr_b, lr_pad], axis=-1).astype(BF16)
    wuq = w_uq[l].reshape(MLA_Q_RANK, MLA_HEADS, MLA_NOPE + MLA_ROPE)
    wuq = jnp.concatenate([wuq[:, :, :MLA_NOPE].reshape(MLA_Q_RANK, -1), wuq[:, :, MLA_NOPE:].reshape(MLA_Q_RANK, -1)],
                          axis=-1).astype(BF16)
    wgate = jnp.zeros((LANES, 2 * hk), F32)
    wgate = wgate.at[0:GLA_GATE_RANK, 0:hk].set(w_gate_fwd[l])
    wgate = wgate.at[GLA_GATE_RANK:2 * GLA_GATE_RANK, hk:].set(w_gate_bwd[l])
    wr = jnp.zeros((D_MODEL, LANES), F32)
    wr = wr.at[:, ROUTER_GROUP_LANE0:ROUTER_GROUP_LANE0 + N_GROUPS].set(w_router_group[l])
    wr = wr.at[:, ROUTER_EXPERT_LANE0:ROUTER_EXPERT_LANE0 + N_EXPERTS].set(w_router_expert[l])
    wr_hi = wr.astype(BF16)
    br = jnp.zeros((1, LANES), F32)
    br = br.at[0, ROUTER_GROUP_LANE0:ROUTER_GROUP_LANE0 + N_GROUPS].set(b_router_group[l])
    br = br.at[0, ROUTER_EXPERT_LANE0:ROUTER_EXPERT_LANE0 + N_EXPERTS].set(b_router_expert[l])
    return {
        'norm_mix': norm_mix[l][None], 'win': win, 'q_a_norm': q_a_norm[l][None], 'wuq': wuq,
        'kv_a_norm': kv_a_norm[l][None], 'wukv': w_ukv[l].astype(BF16),
        'wgate': wgate.astype(BF16), 'bgate': jnp.concatenate([b_gate_fwd[l], b_gate_bwd[l]])[None],
        'gla_norm': gla_norm[l][None], 'wout': w_out[l].astype(BF16), 'norm_ffn': norm_ffn[l][None],
        'wr_hi': wr_hi, 'wr_lo': (wr - wr_hi.astype(F32)).astype(BF16), 'br': br,
        'w_gate': w_expert_gate[l].reshape(N_EXPERTS, D_MODEL, D_EXPERT),
        'w_up': w_expert_up[l].reshape(N_EXPERTS, D_MODEL, D_EXPERT),
        'w_down': w_expert_down[l].reshape(N_EXPERTS, D_EXPERT, D_MODEL),
        'norm_final': norm_final[None],
    }


def _meta_streams(meta_tokens, w):
    cos, sin = _rope_tables(jnp.arange(N_META))
    _, k, v, _, gk, gv, gf, _, _ = _inproj(meta_tokens, cos, sin, w, N_META)
    pad_keys = ((0, 0), (0, LANES - N_META), (0, 0))
    front = ((GLA_CHUNK - N_META, 0), (0, 0))
    return (jnp.pad(k, pad_keys), jnp.pad(v, pad_keys), jnp.pad(gk, front), jnp.pad(gv, front), jnp.pad(gf, front))


def _token_mixers(x, meta, w, tm, tq, tk, tb):
    bsz, seq, _ = x.shape
    km, vm, mk, mv, mg = meta
    x2d = x.reshape(bsz * seq, D_MODEL)
    cos, sin = _rope_tables(N_META + jnp.arange(seq))
    q, k, v, gq, gk, gv, gf, gb, og = _inproj(x2d, cos, sin, w, tm)
    a = _attention(q, k, v, km, vm, bsz, seq, tq, tk)
    o_f, o_b = _gla(gq, gk, gv, gf, gb, mk, mv, mg, bsz, seq, tb)
    return _mix(x2d, a, o_f, o_b, og, w, tm)


def kernel(x_prompt, x_sample, meta_tokens, norm_mix, w_in, q_a_norm, w_uq, kv_a_norm, w_ukv, w_gate_fwd, b_gate_fwd, w_gate_bwd, b_gate_bwd, gla_norm, w_out, norm_ffn, w_router_group, b_router_group, w_router_expert, b_router_expert, w_expert_gate, w_expert_up, w_expert_down, norm_final):
    w = _pack_weights(norm_mix, w_in, q_a_norm, w_uq, kv_a_norm, w_ukv, w_gate_fwd, b_gate_fwd, w_gate_bwd,
                      b_gate_bwd, gla_norm, w_out, norm_ffn, w_router_group, b_router_group, w_router_expert,
                      b_router_expert, w_expert_gate, w_expert_up, w_expert_down, norm_final)
    meta = _meta_streams(meta_tokens, w)
    outs = []
    for x in (x_prompt, x_sample):
        h1, hn, route, ids, counts = _token_mixers(x, meta, w, tm=512, tq=1024, tk=512, tb=512)
        outs.append(_moe(hn, route, ids, counts, h1, w).reshape(x.shape))
    return tuple(outs)
```

```python
import functools

import numpy as np
import jax
import jax.numpy as jnp
from jax import lax
from jax.experimental import pallas as pl
from jax.experimental.pallas import tpu as pltpu
from jax.experimental.pallas import tpu_sc as plsc

F32 = jnp.float32
BF16 = jnp.bfloat16

D_MODEL = 1024
N_META = 16
MLA_HEADS = 4
MLA_Q_RANK = 384
MLA_KV_RANK = 256
MLA_NOPE = 128
MLA_ROPE = 64
MLA_V = 128
ROPE_THETA = 10000.0
GLA_HEADS = 4
GLA_DK = 64
GLA_DV = 128
GLA_GATE_RANK = 16
GLA_TAU = 16.0
GLA_CHUNK = 64
N_GROUPS = 4
EXPERTS_PER_GROUP = 8
N_EXPERTS = N_GROUPS * EXPERTS_PER_GROUP
D_EXPERT = 256
EPS = 1e-6

LANES = 128
V7X_VMEM_BYTES = 64 * 1024 * 1024
VMEM_LIMIT = V7X_VMEM_BYTES * 7 // 8

ATTN_SCALE = (MLA_NOPE + MLA_ROPE) ** -0.5 * float(np.log2(np.e))
QK_WIDTH = 2 * LANES
V_WIDTH = 2 * LANES
ATTN_GROUP = 8

C_CQ = 0
C_CKV = C_CQ + MLA_Q_RANK
C_KPE = C_CKV + MLA_KV_RANK
C_GQ = C_KPE + LANES
C_GK = C_GQ + GLA_HEADS * GLA_DK
C_GV = C_GK + GLA_HEADS * GLA_DK
C_OG = C_GV + GLA_HEADS * GLA_DV
C_LR = C_OG + GLA_HEADS * GLA_DV
D_IN_PACKED = C_LR + LANES

ROUTER_GROUP_LANE0 = 0
ROUTER_EXPERT_LANE0 = N_GROUPS


def _cparams(*semantics):
    return pltpu.CompilerParams(dimension_semantics=semantics, vmem_limit_bytes=VMEM_LIMIT)


def _rms(x, g):
    return x * lax.rsqrt(jnp.mean(x * x, axis=-1, keepdims=True) + EPS) * g


def _dot(a, b):
    return jnp.dot(a, b, preferred_element_type=F32)


def _dot_nt(a, b):
    return lax.dot_general(a, b, (((1,), (1,)), ((), ())), preferred_element_type=F32)


def _dot_tn(a, b):
    return lax.dot_general(a, b, (((0,), (0,)), ((), ())), preferred_element_type=F32)


def _full_spec(shape):
    return pl.BlockSpec(shape, lambda *_: (0,) * len(shape))


SUBLANES = 8
ROW_CHUNKS = D_MODEL // LANES
assert ROW_CHUNKS == SUBLANES


def _rows_to_tiles(x):
    chunks = jnp.stack([x[:, s * LANES:(s + 1) * LANES] for s in range(ROW_CHUNKS)], axis=0)
    return pltpu.einshape("smd->msd", chunks)


def _tiles_to_rows(x):
    chunks = pltpu.einshape("msd->smd", x)
    return jnp.concatenate([chunks[s] for s in range(ROW_CHUNKS)], axis=-1)


def _rope_pairs(x, cos, sin_signed, first_half):
    swapped = jnp.where(first_half, pltpu.roll(x, LANES - MLA_ROPE // 2, 1), pltpu.roll(x, MLA_ROPE // 2, 1))
    return x * cos + swapped * sin_signed


def _inproj_body(x_ref, cos_ref, sin_ref, nmix_ref, win_ref, qan_ref, wuq_ref, kvan_ref, wukv_ref,
                 wgate_ref, bgate_ref,
                 q_ref, k_ref, v_ref, gq_ref, gk_ref, gv_ref, gf_ref, gb_ref, og_ref):
    hn = _rms(x_ref[...], nmix_ref[...]).astype(BF16)

    def proj(lo, hi):
        return _dot(hn, win_ref[:, lo:hi])

    cos = cos_ref[...]
    sin = sin_ref[...]
    lane = lax.broadcasted_iota(jnp.int32, cos.shape, 1)
    first_half = (lane & (MLA_ROPE - 1)) < MLA_ROPE // 2
    low_lanes = lane < MLA_ROPE

    cq = _rms(proj(C_CQ, C_CKV), qan_ref[...]).astype(BF16)
    qn = _dot(cq, wuq_ref[:, 0:MLA_HEADS * MLA_NOPE]) * ATTN_SCALE
    qr = _dot(cq, wuq_ref[:, MLA_HEADS * MLA_NOPE:])
    for j in range(MLA_HEADS // 2):
        rj = (_rope_pairs(qr[:, j * LANES:(j + 1) * LANES], cos, sin, first_half) * ATTN_SCALE).astype(BF16)
        for h in (2 * j, 2 * j + 1):
            q_ref[h, :, 0:LANES] = qn[:, h * LANES:(h + 1) * LANES].astype(BF16)
            q_ref[h, :, LANES:QK_WIDTH] = rj

    ckv = _rms(proj(C_CKV, C_KPE), kvan_ref[...]).astype(BF16)
    kv = _dot(ckv, wukv_ref[...])
    kr = _rope_pairs(proj(C_KPE, C_GQ), cos, sin, first_half)
    kr_even = jnp.where(low_lanes, kr, 0.0).astype(BF16)
    kr_odd = jnp.where(low_lanes, 0.0, kr).astype(BF16)
    for h in range(MLA_HEADS):
        base = h * (MLA_NOPE + MLA_V)
        k_ref[h, :, 0:LANES] = kv[:, base:base + MLA_NOPE].astype(BF16)
        k_ref[h, :, LANES:QK_WIDTH] = kr_even if h % 2 == 0 else kr_odd
        v_ref[h, :, 0:MLA_V] = kv[:, base + MLA_NOPE:base + MLA_NOPE + MLA_V].astype(BF16)
        v_ref[h, :, MLA_V:V_WIDTH] = jnp.ones((kv.shape[0], V_WIDTH - MLA_V), BF16)

    gq_ref[...] = (proj(C_GQ, C_GK) * (GLA_DK ** -0.5)).astype(BF16)
    gk_ref[...] = proj(C_GK, C_GV).astype(BF16)
    gv_ref[...] = proj(C_GV, C_OG).astype(BF16)
    og_ref[...] = proj(C_OG, C_LR).astype(BF16)
    pre = _dot(proj(C_LR, D_IN_PACKED).astype(BF16), wgate_ref[...]) + bgate_ref[...]
    logsig = jnp.minimum(pre, 0.0) - jnp.log1p(jnp.exp(-jnp.abs(pre)))
    gates = logsig * (1.0 / GLA_TAU)
    gf_ref[...] = gates[:, 0:GLA_HEADS * GLA_DK]
    gb_ref[...] = gates[:, GLA_HEADS * GLA_DK:]


def _inproj(x2d, cos, sin, w, tm):
    t = x2d.shape[0]
    blocks_per_seq = cos.shape[0] // tm
    hk = GLA_HEADS * GLA_DK
    hv = GLA_HEADS * GLA_DV
    row = lambda width: pl.BlockSpec((tm, width), lambda i: (i, 0))
    head_rows = lambda width: pl.BlockSpec((MLA_HEADS, tm, width), lambda i: (0, i, 0))
    tab = pl.BlockSpec((tm, LANES), lambda i: (i % blocks_per_seq, 0))
    out_shape = (
        jax.ShapeDtypeStruct((MLA_HEADS, t, QK_WIDTH), BF16),
        jax.ShapeDtypeStruct((MLA_HEADS, t, QK_WIDTH), BF16),
        jax.ShapeDtypeStruct((MLA_HEADS, t, V_WIDTH), BF16),
        jax.ShapeDtypeStruct((t, hk), BF16),
        jax.ShapeDtypeStruct((t, hk), BF16),
        jax.ShapeDtypeStruct((t, hv), BF16),
        jax.ShapeDtypeStruct((t, hk), F32),
        jax.ShapeDtypeStruct((t, hk), F32),
        jax.ShapeDtypeStruct((t, hv), BF16),
    )
    return pl.pallas_call(
        _inproj_body,
        grid=(t // tm,),
        in_specs=[row(D_MODEL), tab, tab,
                  _full_spec((1, D_MODEL)), _full_spec((D_MODEL, D_IN_PACKED)),
                  _full_spec((1, MLA_Q_RANK)), _full_spec(w['wuq'].shape),
                  _full_spec((1, MLA_KV_RANK)), _full_spec(w['wukv'].shape),
                  _full_spec(w['wgate'].shape), _full_spec(w['bgate'].shape)],
        out_specs=(head_rows(QK_WIDTH), head_rows(QK_WIDTH), head_rows(V_WIDTH),
                   row(hk), row(hk), row(hv), row(hk), row(hk), row(hv)),
        out_shape=out_shape,
        compiler_params=_cparams("parallel"),
        name="inproj",
    )(x2d, cos, sin, w['norm_mix'], w['win'], w['q_a_norm'], w['wuq'], w['kv_a_norm'], w['wukv'],
      w['wgate'], w['bgate'])


def _attn_body(q_ref, k_ref, v_ref, km_ref, vm_ref, o_ref, s_ref, acc_ref, *, tk):
    q = q_ref[...]
    n_groups = k_ref.shape[0] // (ATTN_GROUP * tk)

    def scores(j):
        return _dot_nt(q, k_ref[pl.ds(pl.multiple_of(j * tk, tk), tk), :])

    def values(j):
        return v_ref[pl.ds(pl.multiple_of(j * tk, tk), tk), :]

    def absorb(m, s, v):
        m_new = jnp.maximum(m, jnp.max(s, axis=-1, keepdims=True))
        p = jnp.exp2(s - m_new)
        acc_ref[...] = jnp.exp2(m - m_new) * acc_ref[...] + _dot(p.astype(BF16), v)
        return m_new

    s_ref[0] = scores(0)
    sm = _dot_nt(q, km_ref[...])
    sm = jnp.where(lax.broadcasted_iota(jnp.int32, sm.shape, 1) < N_META, sm, -jnp.inf)
    m = jnp.max(sm, axis=-1, keepdims=True)
    acc_ref[...] = _dot(jnp.exp2(sm - m).astype(BF16), vm_ref[...])

    def group(g, m, last):
        for i in range(ATTN_GROUP):
            j = ATTN_GROUP * g + i
            s = s_ref[i % 2]
            if not (last and i == ATTN_GROUP - 1):
                s_ref[(i + 1) % 2] = scores(j + 1)
            m = absorb(m, s, values(j))
        return m

    m = lax.fori_loop(0, n_groups - 1, lambda g, m: group(g, m, False), m)
    group(n_groups - 1, m, True)
    acc = acc_ref[...]
    o_ref[...] = (acc[:, :MLA_V] / acc[:, MLA_V:]).astype(o_ref.dtype)


def _attention(q, k, v, km, vm, bsz, seq, tq, tk):
    nq = seq // tq
    return pl.pallas_call(
        functools.partial(_attn_body, tk=tk),
        grid=(bsz, MLA_HEADS, nq),
        in_specs=[pl.BlockSpec((None, tq, QK_WIDTH), lambda b, h, i: (h, b * nq + i, 0)),
                  pl.BlockSpec((None, seq, QK_WIDTH), lambda b, h, i: (h, b, 0)),
                  pl.BlockSpec((None, seq, V_WIDTH), lambda b, h, i: (h, b, 0)),
                  pl.BlockSpec((None, LANES, QK_WIDTH), lambda b, h, i: (h, 0, 0)),
                  pl.BlockSpec((None, LANES, V_WIDTH), lambda b, h, i: (h, 0, 0))],
        out_specs=pl.BlockSpec((tq, MLA_V), lambda b, h, i: (b * nq + i, h)),
        out_shape=jax.ShapeDtypeStruct((bsz * seq, MLA_HEADS * MLA_V), BF16),
        scratch_shapes=[pltpu.VMEM((2, tq, tk), F32), pltpu.VMEM((tq, V_WIDTH), F32)],
        compiler_params=_cparams("parallel", "parallel", "arbitrary"),
        name="mla_attention",
    )(q, k, v, km, vm)


def _split3(x):
    hi = x.astype(BF16)
    r1 = x - hi.astype(F32)
    mid = r1.astype(BF16)
    lo = (r1 - mid.astype(F32)).astype(BF16)
    return hi, mid, lo


def _gla_log_decay(g, tri):
    g_hi, g_mid, g_lo = _split3(g)
    return _dot(tri, g_hi) + _dot(tri, g_mid) + _dot(tri, g_lo)


def _head_rows(x):
    even = lax.broadcasted_iota(jnp.int32, x.shape, 1) < GLA_DK
    return jnp.concatenate([jnp.where(even, x, 0.0), jnp.where(even, 0.0, x)], axis=0).astype(BF16)


def _gla_operands(q, k, b, mid, last):
    b_last = b[last:last + 1, :]
    ke = (k * jnp.exp(b_last - b)).astype(BF16)
    if q is None:
        return jnp.exp(b_last), ke, None, None, None
    b_mid = b[mid:mid + 1, :]
    ks = (k * jnp.exp(b_mid - b)).astype(BF16)
    return jnp.exp(b_last), ke, ks, _head_rows(q * jnp.exp(b - b_mid)), _head_rows(q * jnp.exp(b))


def _gla_state_update(v_even, v_odd, ke):
    return jnp.where(lax.broadcasted_iota(jnp.int32, (GLA_DV, LANES), 1) < GLA_DK,
                     _dot_tn(v_even, ke), _dot_tn(v_odd, ke))


def _gla_intra(v_even, v_odd, ks, qs2, keep):
    scores = jnp.where(keep, _dot_nt(qs2, ks), 0.0).astype(BF16)
    return _dot(scores[:GLA_CHUNK], v_even), _dot(scores[GLA_CHUNK:], v_odd)


def _gla_body(qf_ref, kf_ref, vf_ref, gf_ref, qb_ref, kb_ref, vb_ref, gb_ref, mk_ref, mv_ref, mg_ref,
              of_ref, ob_ref, state_ref):
    n_chunks = qf_ref.shape[0] // GLA_CHUNK
    n_pairs = GLA_HEADS // 2
    r = lax.broadcasted_iota(jnp.int32, (GLA_CHUNK, GLA_CHUNK), 0)
    c = lax.broadcasted_iota(jnp.int32, (GLA_CHUNK, GLA_CHUNK), 1)
    tri_f = jnp.where(c <= r, 1.0, 0.0).astype(BF16)
    tri_b = jnp.where(c >= r, 1.0, 0.0).astype(BF16)
    r2 = lax.broadcasted_iota(jnp.int32, (2 * GLA_CHUNK, GLA_CHUNK), 0) & (GLA_CHUNK - 1)
    c2 = lax.broadcasted_iota(jnp.int32, (2 * GLA_CHUNK, GLA_CHUNK), 1)
    keep_f = c2 <= r2
    keep_b = c2 >= r2
    mid_f, last_f = GLA_CHUNK // 2 - 1, GLA_CHUNK - 1
    mid_b, last_b = GLA_CHUNK // 2, 0

    def pair_cols(p):
        return slice(p * LANES, (p + 1) * LANES)

    def head_cols(h):
        return slice(h * GLA_DV, (h + 1) * GLA_DV)

    @pl.when(pl.program_id(1) == 0)
    def _():
        for p in range(n_pairs):
            b = _gla_log_decay(mg_ref[:, pair_cols(p)], tri_f)
            _, ke, _, _, _ = _gla_operands(None, mk_ref[:, pair_cols(p)], b, mid_f, last_f)
            state_ref[p] = _gla_state_update(mv_ref[:, head_cols(2 * p)].astype(BF16),
                                             mv_ref[:, head_cols(2 * p + 1)].astype(BF16), ke)
            state_ref[n_pairs + p] = jnp.zeros((GLA_DV, LANES), F32)

    scans = []
    for p in range(n_pairs):
        scans.append((p, list(range(n_chunks)), qf_ref, kf_ref, vf_ref, gf_ref, of_ref, p,
                      tri_f, keep_f, mid_f, last_f))
        scans.append((p, list(reversed(range(n_chunks))), qb_ref, kb_ref, vb_ref, gb_ref, ob_ref, n_pairs + p,
                      tri_b, keep_b, mid_b, last_b))

    def rows(c):
        return slice(c * GLA_CHUNK, (c + 1) * GLA_CHUNK)

    def values(v_ref, p, c):
        return (v_ref[rows(c), head_cols(2 * p)].astype(BF16), v_ref[rows(c), head_cols(2 * p + 1)].astype(BF16))

    log_decay = [[_gla_log_decay(g_ref[rows(c), pair_cols(p)], tri) for c in order]
                 for (p, order, _, _, _, g_ref, _, _, tri, _, _, _) in scans]
    operands = [[_gla_operands(q_ref[rows(c), pair_cols(p)], k_ref[rows(c), pair_cols(p)], b, mid, last)
                 for c, b in zip(order, bs)]
                for (p, order, q_ref, k_ref, _, _, _, _, _, _, mid, last), bs in zip(scans, log_decay)]
    updates = [[_gla_state_update(*values(v_ref, p, c), ops[1]) for c, ops in zip(order, opss)]
               for (p, order, _, _, v_ref, _, _, _, _, _, _, _), opss in zip(scans, operands)]
    intra = [[_gla_intra(*values(v_ref, p, c), ops[2], ops[3], keep) for c, ops in zip(order, opss)]
             for (p, order, _, _, v_ref, _, _, _, _, keep, _, _), opss in zip(scans, operands)]
    states = []
    for (_, order, _, _, _, _, _, slot, _, _, _, _), opss, upds in zip(scans, operands, updates):
        st = state_ref[slot]
        entering = []
        for ops, upd in zip(opss, upds):
            entering.append(st.astype(BF16))
            st = st * ops[0] + upd
        state_ref[slot] = st
        states.append(entering)
    for (p, order, _, _, _, _, o_ref, _, _, _, _, _), opss, sts, locs in zip(scans, operands, states, intra):
        for c, ops, st, (o_even, o_odd) in zip(order, opss, sts, locs):
            inter = _dot_nt(ops[4], st)
            o_ref[rows(c), head_cols(2 * p)] = (o_even + inter[:GLA_CHUNK]).astype(o_ref.dtype)
            o_ref[rows(c), head_cols(2 * p + 1)] = (o_odd + inter[GLA_CHUNK:]).astype(o_ref.dtype)


def _gla(gq, gk, gv, gf, gb, mk, mv, mg, bsz, seq, tb):
    nb = seq // tb
    hk = GLA_HEADS * GLA_DK
    hv = GLA_HEADS * GLA_DV
    fwd = lambda width: pl.BlockSpec((tb, width), lambda b, j: (b * nb + j, 0))
    bwd = lambda width: pl.BlockSpec((tb, width), lambda b, j: (b * nb + nb - 1 - j, 0))
    t = bsz * seq
    return pl.pallas_call(
        _gla_body,
        grid=(bsz, nb),
        in_specs=[fwd(hk), fwd(hk), fwd(hv), fwd(hk), bwd(hk), bwd(hk), bwd(hv), bwd(hk),
                  _full_spec(mk.shape), _full_spec(mv.shape), _full_spec(mg.shape)],
        out_specs=(fwd(hv), bwd(hv)),
        out_shape=(jax.ShapeDtypeStruct((t, hv), BF16), jax.ShapeDtypeStruct((t, hv), BF16)),
        scratch_shapes=[pltpu.VMEM((2 * (GLA_HEADS // 2), GLA_DV, LANES), F32)],
        compiler_params=_cparams("parallel", "arbitrary"),
        name="gla_scan",
    )(gq, gk, gv, gf, gq, gk, gv, gb, mk, mv, mg)


def _mix_body(x_ref, a_ref, of_ref, ob_ref, og_ref, gnorm_ref, wout_ref, nffn_ref, wr_hi_ref, wr_lo_ref, br_ref,
              h1_ref, hn_ref, route_ref, ids_ref, count_ref, tri_ref):
    tm = x_ref.shape[0]

    @pl.when(pl.program_id(0) == 0)
    def _():
        r = lax.broadcasted_iota(jnp.int32, (tm, tm), 0)
        c = lax.broadcasted_iota(jnp.int32, (tm, tm), 1)
        tri_ref[...] = jnp.where(c < r, 1.0, 0.0).astype(BF16)
        count_ref[...] = jnp.zeros_like(count_ref)

    a_width = MLA_HEADS * MLA_V
    h1 = x_ref[...] + _dot(a_ref[...], wout_ref[0:a_width, :])
    for h in range(GLA_HEADS):
        cols = slice(h * GLA_DV, (h + 1) * GLA_DV)
        o = of_ref[:, cols].astype(F32) + ob_ref[:, cols].astype(F32)
        og = og_ref[:, cols].astype(F32)
        silu = og / (1.0 + jnp.exp(-og))
        gh = (_rms(o, gnorm_ref[...]) * silu).astype(BF16)
        h1 = h1 + _dot(gh, wout_ref[a_width + h * GLA_DV:a_width + (h + 1) * GLA_DV, :])
    h1_ref[...] = h1
    hn = _rms(h1, nffn_ref[...])
    hn_ref[...] = _rows_to_tiles(hn)

    hn_hi = hn.astype(BF16)
    hn_lo = (hn - hn_hi.astype(F32)).astype(BF16)
    logits = (_dot(hn_hi, wr_hi_ref[...]) + _dot(hn_lo, wr_hi_ref[...]) + _dot(hn_hi, wr_lo_ref[...])
              + br_ref[...])
    lane = lax.broadcasted_iota(jnp.int32, logits.shape, 1).astype(F32)
    none = float(LANES)
    neg = -jnp.inf

    def lane_max(x):
        return jnp.max(x, axis=-1, keepdims=True)

    def lane_sum(x):
        return jnp.sum(x, axis=-1, keepdims=True)

    def first_lane(mask):
        return jnp.min(jnp.where(mask, lane, none), axis=-1, keepdims=True)

    is_group = lane < float(N_GROUPS)
    g_max = lane_max(jnp.where(is_group, logits, neg))
    g_exp = jnp.where(is_group, jnp.exp(logits - g_max), 0.0)
    g_prob = g_exp / lane_sum(g_exp)
    g_w = lane_max(g_prob)
    g_idx = first_lane(is_group & (g_prob == g_w))
    e_lo = float(ROUTER_EXPERT_LANE0) + float(EXPERTS_PER_GROUP) * g_idx
    sel = (lane >= e_lo) & (lane < e_lo + float(EXPERTS_PER_GROUP))
    e_max = lane_max(jnp.where(sel, logits, neg))
    e_exp = jnp.where(sel, jnp.exp(logits - e_max), 0.0)
    e_prob = e_exp / lane_sum(e_exp)
    p1 = lane_max(jnp.where(sel, e_prob, neg))
    i1 = first_lane(sel & (e_prob == p1))
    rest = sel & (lane != i1)
    p2 = lane_max(jnp.where(rest, e_prob, neg))
    i2 = first_lane(rest & (e_prob == p2))
    denom = p1 + p2
    chosen = jnp.where((lane == i1) | (lane == i2), 1.0, 0.0)
    rank = count_ref[...] + _dot(tri_ref[...], chosen.astype(BF16))
    count_ref[...] += jnp.sum(chosen, axis=0, keepdims=True)
    fields = (i1 - float(ROUTER_EXPERT_LANE0), i2 - float(ROUTER_EXPERT_LANE0),
              lane_sum(jnp.where(lane == i1, rank, 0.0)), lane_sum(jnp.where(lane == i2, rank, 0.0)),
              g_w * (p1 / denom), g_w * (p2 / denom))
    route = jnp.zeros_like(logits)
    for k, value in enumerate(fields):
        route = jnp.where(lane == float(k), value, route)
    route_ref[...] = route
    ids_ref[...] = jnp.transpose(route)[0:ROUTE_ID_ROWS, :].astype(jnp.int32)


ROUTE_E1, ROUTE_E2, ROUTE_RANK1, ROUTE_RANK2, ROUTE_W1, ROUTE_W2 = range(6)
ROUTE_ID_ROWS = 8


def _mix(x2d, a, o_f, o_b, og, w, tm):
    t = x2d.shape[0]
    hv = GLA_HEADS * GLA_DV
    row = lambda width: pl.BlockSpec((tm, width), lambda i: (i, 0))
    return pl.pallas_call(
        _mix_body,
        grid=(t // tm,),
        in_specs=[row(D_MODEL), row(MLA_HEADS * MLA_V), row(hv), row(hv), row(hv),
                  _full_spec((1, GLA_DV)), _full_spec(w['wout'].shape), _full_spec((1, D_MODEL)),
                  _full_spec(w['wr_hi'].shape), _full_spec(w['wr_lo'].shape), _full_spec(w['br'].shape)],
        out_specs=(row(D_MODEL), pl.BlockSpec((tm, ROW_CHUNKS, LANES), lambda i: (i, 0, 0)), row(LANES),
                   pl.BlockSpec((ROUTE_ID_ROWS, tm), lambda i: (0, i)), _full_spec((1, LANES))),
        out_shape=(jax.ShapeDtypeStruct((t, D_MODEL), F32), jax.ShapeDtypeStruct((t, ROW_CHUNKS, LANES), F32),
                   jax.ShapeDtypeStruct((t, LANES), F32), jax.ShapeDtypeStruct((ROUTE_ID_ROWS, t), jnp.int32),
                   jax.ShapeDtypeStruct((1, LANES), F32)),
        scratch_shapes=[pltpu.VMEM((tm, tm), BF16)],
        compiler_params=_cparams("arbitrary"),
        name="mix_router",
    )(x2d, a, o_f, o_b, og, w['gla_norm'], w['wout'], w['norm_ffn'], w['wr_hi'], w['wr_lo'], w['br'])


EXPERT_TILE = 256
ROW_TILE = 256


SC_CORES = 2
SC_SUBCORES = 16
SC_GATHER_ROWS = 32


def _sc_gather(table, idx):
    n = idx.shape[0]
    workers = SC_CORES * SC_SUBCORES
    per_worker = n // workers
    assert n % (workers * SC_GATHER_ROWS) == 0
    mesh = plsc.VectorSubcoreMesh(core_axis_name="c", subcore_axis_name="s")

    @functools.partial(
        pl.kernel, mesh=mesh,
        out_type=jax.ShapeDtypeStruct((n,) + table.shape[1:], table.dtype),
        scratch_types=[pltpu.VMEM((SC_GATHER_ROWS,), jnp.int32),
                       pltpu.VMEM((SC_GATHER_ROWS,) + table.shape[1:], table.dtype),
                       pltpu.SemaphoreType.DMA])
    def gather(table_ref, idx_ref, out_ref, idx_buf, rows_buf, sem):
        base = (lax.axis_index("s") * SC_CORES + lax.axis_index("c")) * per_worker

        @pl.loop(0, per_worker // SC_GATHER_ROWS)
        def _(j):
            rows = pl.ds(base + j * SC_GATHER_ROWS, SC_GATHER_ROWS)
            pltpu.sync_copy(idx_ref.at[rows], idx_buf)
            pltpu.async_copy(table_ref.at[idx_buf], rows_buf, sem).wait()
            pltpu.sync_copy(rows_buf, out_ref.at[rows])

    return gather(table, idx)


def _sc_scatter(rows, idx, n_out):
    copies, n = idx.shape
    idx = idx.reshape(copies * n)
    workers = SC_CORES * SC_SUBCORES
    per_worker = n // workers
    assert n % (workers * SC_GATHER_ROWS) == 0
    mesh = plsc.VectorSubcoreMesh(core_axis_name="c", subcore_axis_name="s")

    @functools.partial(
        pl.kernel, mesh=mesh,
        out_type=jax.ShapeDtypeStruct((n_out,) + rows.shape[1:], rows.dtype),
        scratch_types=[pltpu.VMEM((SC_GATHER_ROWS,), jnp.int32),
                       pltpu.VMEM((SC_GATHER_ROWS,) + rows.shape[1:], rows.dtype),
                       pltpu.SemaphoreType.DMA])
    def scatter(rows_ref, idx_ref, out_ref, idx_buf, rows_buf, sem):
        base = (lax.axis_index("s") * SC_CORES + lax.axis_index("c")) * per_worker

        @pl.loop(0, per_worker // SC_GATHER_ROWS)
        def _(j):
            first = base + j * SC_GATHER_ROWS
            pltpu.sync_copy(rows_ref.at[pl.ds(first, SC_GATHER_ROWS)], rows_buf)
            for k in range(copies):
                pltpu.sync_copy(idx_ref.at[pl.ds(k * n + first, SC_GATHER_ROWS)], idx_buf)
                pltpu.async_copy(rows_buf, out_ref.at[idx_buf], sem).wait()

    return scatter(rows, idx)


SLOT_TILE = 2048


def _slots_body(starts_ref, ids_ref, slots_ref):
    ids = ids_ref[...]
    experts = ids[ROUTE_E1:ROUTE_E1 + 2, :]
    start = jnp.zeros_like(experts)
    for e in range(N_EXPERTS):
        start = jnp.where(experts == e, starts_ref[e], start)
    slots_ref[...] = start + ids[ROUTE_RANK1:ROUTE_RANK1 + 2, :]


def _slots(starts, ids):
    t = ids.shape[1]
    return pl.pallas_call(
        _slots_body,
        grid_spec=pltpu.PrefetchScalarGridSpec(
            num_scalar_prefetch=1,
            grid=(t // SLOT_TILE,),
            in_specs=[pl.BlockSpec((ROUTE_ID_ROWS, SLOT_TILE), lambda i, *_: (0, i))],
            out_specs=pl.BlockSpec((2, SLOT_TILE), lambda i, *_: (0, i))),
        out_shape=jax.ShapeDtypeStruct((2, t), jnp.int32),
        compiler_params=_cparams("parallel"),
        name="moe_slots",
    )(starts, ids)


def _queue_body(slots_ref, token_ref, *, n_tokens):
    i = pl.program_id(0)

    @pl.when(i == 0)
    def _():
        for base in range(0, token_ref.shape[0], n_tokens):
            def clear(j, carry):
                token_ref[base + j] = j
                return carry
            lax.fori_loop(0, min(n_tokens, token_ref.shape[0] - base), clear, 0, unroll=8)

    def place(r, carry):
        for k in range(2):
            token_ref[slots_ref[k, r]] = i * SLOT_TILE + r
        return carry

    lax.fori_loop(0, SLOT_TILE, place, 0, unroll=8)


def _queue_tokens(slots, n_rows):
    t = slots.shape[1]
    return pl.pallas_call(
        functools.partial(_queue_body, n_tokens=t),
        grid=(t // SLOT_TILE,),
        in_specs=[pl.BlockSpec((2, SLOT_TILE), lambda i: (0, i), memory_space=pltpu.SMEM)],
        out_specs=pl.BlockSpec(memory_space=pltpu.SMEM),
        out_shape=jax.ShapeDtypeStruct((n_rows,), jnp.int32),
        compiler_params=_cparams("arbitrary"),
        name="moe_queue",
    )(slots)


TILES_PER_STEP = 2


def _expert_body(tile_expert_ref, n_tiles_ref, tile_valid_ref, xs_ref, *refs):
    wg_refs = refs[0:TILES_PER_STEP]
    wu_refs = refs[TILES_PER_STEP:2 * TILES_PER_STEP]
    wd_refs = refs[2 * TILES_PER_STEP:3 * TILES_PER_STEP]
    ys_ref = refs[3 * TILES_PER_STEP]
    first = pl.program_id(0) * TILES_PER_STEP
    used = first < n_tiles_ref[0]

    @pl.when(used)
    def _():
        rows = [slice(j * EXPERT_TILE, (j + 1) * EXPERT_TILE) for j in range(TILES_PER_STEP)]
        xs = [_tiles_to_rows(xs_ref[r]) for r in rows]
        xs = [jnp.where(lax.broadcasted_iota(jnp.int32, x.shape, 0) < tile_valid_ref[first + j], x, 0.0)
              for j, x in enumerate(xs)]
        xs = [x.astype(BF16) for x in xs]
        gates = [_dot(x, wg[...]) for x, wg in zip(xs, wg_refs)]
        ups = [_dot(x, wu[...]) for x, wu in zip(xs, wu_refs)]
        acts = [(g / (1.0 + jnp.exp(-g))) * u for g, u in zip(gates, ups)]
        ys = [_dot(act.astype(BF16), wd[...]) for act, wd in zip(acts, wd_refs)]
        for r, y in zip(rows, ys):
            ys_ref[r] = _rows_to_tiles(y)

    @pl.when(jnp.logical_not(used))
    def _():
        ys_ref[...] = jnp.zeros_like(ys_ref)


def _experts(tile_expert, n_tiles, tile_valid, xs, w):
    n_steps = xs.shape[0] // (EXPERT_TILE * TILES_PER_STEP)

    def step(i, nt):
        return jnp.minimum(i, (nt[0] - 1) // TILES_PER_STEP)

    def expert(i, j, te, nt):
        return te[jnp.minimum(step(i, nt) * TILES_PER_STEP + j, nt[0] - 1)]

    tiles = lambda index: pl.BlockSpec((EXPERT_TILE * TILES_PER_STEP, ROW_CHUNKS, LANES), index)
    up_spec = lambda j: pl.BlockSpec((None, D_MODEL, D_EXPERT), lambda i, te, nt, tv: (expert(i, j, te, nt), 0, 0))
    down_spec = lambda j: pl.BlockSpec((None, D_EXPERT, D_MODEL), lambda i, te, nt, tv: (expert(i, j, te, nt), 0, 0))
    slots = range(TILES_PER_STEP)
    return pl.pallas_call(
        _expert_body,
        grid_spec=pltpu.PrefetchScalarGridSpec(
            num_scalar_prefetch=3,
            grid=(n_steps,),
            in_specs=([tiles(lambda i, te, nt, tv: (step(i, nt), 0, 0))]
                      + [up_spec(j) for j in slots] + [up_spec(j) for j in slots] + [down_spec(j) for j in slots]),
            out_specs=tiles(lambda i, te, nt, tv: (i, 0, 0))),
        out_shape=jax.ShapeDtypeStruct(xs.shape, F32),
        compiler_params=_cparams("arbitrary"),
        name="moe_experts",
    )(tile_expert, n_tiles, tile_valid, xs, *([w['w_gate']] * TILES_PER_STEP), *([w['w_up']] * TILES_PER_STEP),
      *([w['w_down']] * TILES_PER_STEP))


def _combine_body(h1_ref, route_ref, nfin_ref, y1_ref, y2_ref, out_ref):
    route = route_ref[...]
    lane = lax.broadcasted_iota(jnp.int32, route.shape, 1)
    w1 = jnp.sum(jnp.where(lane == ROUTE_W1, route, 0.0), axis=-1, keepdims=True)
    w2 = jnp.sum(jnp.where(lane == ROUTE_W2, route, 0.0), axis=-1, keepdims=True)
    y = w1 * _tiles_to_rows(y1_ref[...]) + w2 * _tiles_to_rows(y2_ref[...])
    out_ref[...] = _rms(h1_ref[...] + y, nfin_ref[...])


def _combine(h1, route, y12, w):
    t = h1.shape[0]
    row = lambda width: pl.BlockSpec((ROW_TILE, width), lambda i: (i, 0))
    tiles = lambda k: pl.BlockSpec((None, ROW_TILE, ROW_CHUNKS, LANES), lambda i: (k, i, 0, 0))
    return pl.pallas_call(
        _combine_body,
        grid=(t // ROW_TILE,),
        in_specs=[row(D_MODEL), row(LANES), _full_spec((1, D_MODEL)), tiles(0), tiles(1)],
        out_specs=row(D_MODEL),
        out_shape=jax.ShapeDtypeStruct((t, D_MODEL), F32),
        compiler_params=_cparams("parallel"),
        name="moe_combine",
    )(h1, route, w['norm_final'], y12, y12)


def _moe(hn, route, ids, counts, h1, w):
    t = hn.shape[0]
    n_rows = 2 * t + N_EXPERTS * EXPERT_TILE
    n_grid_tiles = n_rows // EXPERT_TILE
    count = counts[0, ROUTER_EXPERT_LANE0:ROUTER_EXPERT_LANE0 + N_EXPERTS].astype(jnp.int32)
    padded = jnp.maximum((count + EXPERT_TILE - 1) // EXPERT_TILE, 1) * EXPERT_TILE
    ends = jnp.sum(jnp.where(jnp.arange(N_EXPERTS)[:, None] <= jnp.arange(N_EXPERTS)[None, :], padded[:, None], 0),
                   axis=0)
    starts = ends - padded
    n_tiles = (ends[-1:] // EXPERT_TILE)
    tile_rows = jnp.arange(n_grid_tiles, dtype=jnp.int32) * EXPERT_TILE
    tile_expert = jnp.minimum(jnp.sum((ends[None, :] <= tile_rows[:, None]).astype(jnp.int32), axis=1), N_EXPERTS - 1)
    tile_valid = jnp.clip(jnp.sum(jnp.where(jnp.arange(N_EXPERTS)[None, :] == tile_expert[:, None],
                                            (starts + count)[None, :], 0), axis=1) - tile_rows, 0, EXPERT_TILE)
    slots = _slots(starts, ids)
    ys = _experts(tile_expert, n_tiles, tile_valid, _sc_scatter(hn, slots, n_rows), w)
    y12 = _sc_gather(ys, slots.reshape(2 * t)).reshape(2, t, ROW_CHUNKS, LANES)
    return _combine(h1, route, y12, w)


def _rope_tables(positions):
    inv_freq = 1.0 / (ROPE_THETA ** (jnp.arange(0, MLA_ROPE, 2, dtype=F32) / MLA_ROPE))
    ang = positions.astype(F32)[:, None] * inv_freq[None, :]
    cos, sin = jnp.cos(ang), jnp.sin(ang)
    reps = LANES // MLA_ROPE
    return jnp.tile(jnp.concatenate([cos, cos], axis=-1), (1, reps)), jnp.tile(jnp.concatenate([-sin, sin], axis=-1), (1, reps))


def _pack_weights(norm_mix, w_in, q_a_norm, w_uq, kv_a_norm, w_ukv, w_gate_fwd, b_gate_fwd, w_gate_bwd, b_gate_bwd,
                  gla_norm, w_out, norm_ffn, w_router_group, b_router_group, w_router_expert, b_router_expert,
                  w_expert_gate, w_expert_up, w_expert_down, norm_final):
    l = 0
    hk = GLA_HEADS * GLA_DK
    hv = GLA_HEADS * GLA_DV
    c_q, c_kv, k_pe, gq, gk, gv, lr_f, lr_b, og = jnp.split(
        w_in[l], np.cumsum([MLA_Q_RANK, MLA_KV_RANK, MLA_ROPE, hk, hk, hv, GLA_GATE_RANK, GLA_GATE_RANK])[:].tolist(),
        axis=-1)
    lr_pad = jnp.zeros((D_MODEL, LANES - 2 * GLA_GATE_RANK), F32)
    win = jnp.concatenate([c_q, c_kv, k_pe, k_pe, gq, gk, gv, og, lr_f, lr_b, lr_pad], axis=-1).astype(BF16)
    wuq = w_uq[l].reshape(MLA_Q_RANK, MLA_HEADS, MLA_NOPE + MLA_ROPE)
    wuq = jnp.concatenate([wuq[:, :, :MLA_NOPE].reshape(MLA_Q_RANK, -1), wuq[:, :, MLA_NOPE:].reshape(MLA_Q_RANK, -1)],
                          axis=-1).astype(BF16)
    wgate = jnp.zeros((LANES, 2 * hk), F32)
    wgate = wgate.at[0:GLA_GATE_RANK, 0:hk].set(w_gate_fwd[l])
    wgate = wgate.at[GLA_GATE_RANK:2 * GLA_GATE_RANK, hk:].set(w_gate_bwd[l])
    wr = jnp.zeros((D_MODEL, LANES), F32)
    wr = wr.at[:, ROUTER_GROUP_LANE0:ROUTER_GROUP_LANE0 + N_GROUPS].set(w_router_group[l])
    wr = wr.at[:, ROUTER_EXPERT_LANE0:ROUTER_EXPERT_LANE0 + N_EXPERTS].set(w_router_expert[l])
    wr_hi = wr.astype(BF16)
    br = jnp.zeros((1, LANES), F32)
    br = br.at[0, ROUTER_GROUP_LANE0:ROUTER_GROUP_LANE0 + N_GROUPS].set(b_router_group[l])
    br = br.at[0, ROUTER_EXPERT_LANE0:ROUTER_EXPERT_LANE0 + N_EXPERTS].set(b_router_expert[l])
    return {
        'norm_mix': norm_mix[l][None], 'win': win, 'q_a_norm': q_a_norm[l][None], 'wuq': wuq,
        'kv_a_norm': kv_a_norm[l][None], 'wukv': w_ukv[l].astype(BF16),
        'wgate': wgate.astype(BF16), 'bgate': jnp.concatenate([b_gate_fwd[l], b_gate_bwd[l]])[None],
        'gla_norm': gla_norm[l][None], 'wout': w_out[l].astype(BF16), 'norm_ffn': norm_ffn[l][None],
        'wr_hi': wr_hi, 'wr_lo': (wr - wr_hi.astype(F32)).astype(BF16), 'br': br,
        'w_gate': w_expert_gate[l].reshape(N_EXPERTS, D_MODEL, D_EXPERT).astype(BF16),
        'w_up': w_expert_up[l].reshape(N_EXPERTS, D_MODEL, D_EXPERT).astype(BF16),
        'w_down': w_expert_down[l].reshape(N_EXPERTS, D_EXPERT, D_MODEL).astype(BF16),
        'norm_final': norm_final[None],
    }


def _meta_streams(meta_tokens, w):
    cos, sin = _rope_tables(jnp.arange(N_META))
    _, k, v, _, gk, gv, gf, _, _ = _inproj(meta_tokens, cos, sin, w, N_META)
    pad_keys = ((0, 0), (0, LANES - N_META), (0, 0))
    front = ((GLA_CHUNK - N_META, 0), (0, 0))
    return (jnp.pad(k, pad_keys), jnp.pad(v, pad_keys), jnp.pad(gk, front), jnp.pad(gv, front), jnp.pad(gf, front))


def _token_mixers(x, meta, w, tm, tq, tk, tb):
    bsz, seq, _ = x.shape
    km, vm, mk, mv, mg = meta
    x2d = x.reshape(bsz * seq, D_MODEL)
    cos, sin = _rope_tables(N_META + jnp.arange(seq))
    q, k, v, gq, gk, gv, gf, gb, og = _inproj(x2d, cos, sin, w, tm)
    a = _attention(q, k, v, km, vm, bsz, seq, tq, tk)
    o_f, o_b = _gla(gq, gk, gv, gf, gb, mk, mv, mg, bsz, seq, tb)
    return _mix(x2d, a, o_f, o_b, og, w, tm)


def kernel(x_prompt, x_sample, meta_tokens, norm_mix, w_in, q_a_norm, w_uq, kv_a_norm, w_ukv, w_gate_fwd, b_gate_fwd, w_gate_bwd, b_gate_bwd, gla_norm, w_out, norm_ffn, w_router_group, b_router_group, w_router_expert, b_router_expert, w_expert_gate, w_expert_up, w_expert_down, norm_final):
    w = _pack_weights(norm_mix, w_in, q_a_norm, w_uq, kv_a_norm, w_ukv, w_gate_fwd, b_gate_fwd, w_gate_bwd,
                      b_gate_bwd, gla_norm, w_out, norm_ffn, w_router_group, b_router_group, w_router_expert,
                      b_router_expert, w_expert_gate, w_expert_up, w_expert_down, norm_final)
    meta = _meta_streams(meta_tokens, w)
    outs = []
    for x in (x_prompt, x_sample):
        h1, hn, route, ids, counts = _token_mixers(x, meta, w, tm=512, tq=1024, tk=512, tb=512)
        outs.append(_moe(hn, route, ids, counts, h1, w).reshape(x.shape))
    return tuple(outs)
```

```python
import functools

import numpy as np
import jax
import jax.numpy as jnp
from jax import lax
from jax.experimental import pallas as pl
from jax.experimental.pallas import tpu as pltpu
from jax.experimental.pallas import tpu_sc as plsc

F32 = jnp.float32
BF16 = jnp.bfloat16

D_MODEL = 1024
N_META = 16
MLA_HEADS = 4
MLA_Q_RANK = 384
MLA_KV_RANK = 256
MLA_NOPE = 128
MLA_ROPE = 64
MLA_V = 128
ROPE_THETA = 10000.0
GLA_HEADS = 4
GLA_DK = 64
GLA_DV = 128
GLA_GATE_RANK = 16
GLA_TAU = 16.0
GLA_CHUNK = 64
N_GROUPS = 4
EXPERTS_PER_GROUP = 8
N_EXPERTS = N_GROUPS * EXPERTS_PER_GROUP
D_EXPERT = 256
EPS = 1e-6

LANES = 128
V7X_VMEM_BYTES = 64 * 1024 * 1024
VMEM_LIMIT = V7X_VMEM_BYTES * 7 // 8

ATTN_SCALE = (MLA_NOPE + MLA_ROPE) ** -0.5 * float(np.log2(np.e))
QK_WIDTH = 2 * LANES
V_WIDTH = 2 * LANES
ATTN_GROUP = 8

C_CQ = 0
C_CKV = C_CQ + MLA_Q_RANK
C_KPE = C_CKV + MLA_KV_RANK
C_GQ = C_KPE + LANES
C_GK = C_GQ + GLA_HEADS * GLA_DK
C_GV = C_GK + GLA_HEADS * GLA_DK
C_OG = C_GV + GLA_HEADS * GLA_DV
C_LR = C_OG + GLA_HEADS * GLA_DV
D_IN_PACKED = C_LR + LANES

ROUTER_GROUP_LANE0 = 0
ROUTER_EXPERT_LANE0 = N_GROUPS


def _cparams(*semantics):
    return pltpu.CompilerParams(dimension_semantics=semantics, vmem_limit_bytes=VMEM_LIMIT)


def _rms(x, g):
    return x * lax.rsqrt(jnp.mean(x * x, axis=-1, keepdims=True) + EPS) * g


def _dot(a, b):
    return jnp.dot(a, b, preferred_element_type=F32)


def _dot_nt(a, b):
    return lax.dot_general(a, b, (((1,), (1,)), ((), ())), preferred_element_type=F32)


def _dot_tn(a, b):
    return lax.dot_general(a, b, (((0,), (0,)), ((), ())), preferred_element_type=F32)


def _full_spec(shape):
    return pl.BlockSpec(shape, lambda *_: (0,) * len(shape))


SUBLANES = 8
ROW_CHUNKS = D_MODEL // LANES
assert ROW_CHUNKS == SUBLANES


def _rows_to_tiles(x):
    chunks = jnp.stack([x[:, s * LANES:(s + 1) * LANES] for s in range(x.shape[1] // LANES)], axis=0)
    return pltpu.einshape("smd->msd", chunks)


def _tiles_to_rows(x):
    chunks = pltpu.einshape("msd->smd", x)
    return jnp.concatenate([chunks[s] for s in range(x.shape[1])], axis=-1)


PACKED_CHUNKS = ROW_CHUNKS // 2


def _pack_bf16_pairs(x):
    half = x.shape[1] // 2
    bits = lambda v: lax.bitcast_convert_type(v.astype(BF16).astype(F32), jnp.uint32)
    return (bits(x[:, half:]) & jnp.uint32(0xFFFF0000)) | (bits(x[:, :half]) >> 16)


def _unpack_bf16_pairs(w):
    lo = lax.bitcast_convert_type(w << 16, F32)
    hi = lax.bitcast_convert_type(w & jnp.uint32(0xFFFF0000), F32)
    return jnp.concatenate([lo, hi], axis=-1)


def _rope_pairs(x, cos, sin_signed, first_half):
    swapped = jnp.where(first_half, pltpu.roll(x, LANES - MLA_ROPE // 2, 1), pltpu.roll(x, MLA_ROPE // 2, 1))
    return x * cos + swapped * sin_signed


def _inproj_body(x_ref, cos_ref, sin_ref, nmix_ref, win_ref, qan_ref, wuq_ref, kvan_ref, wukv_ref,
                 wgate_ref, bgate_ref,
                 q_ref, k_ref, v_ref, gq_ref, gk_ref, gv_ref, gf_ref, gb_ref, og_ref):
    hn = _rms(x_ref[...], nmix_ref[...]).astype(BF16)

    def proj(lo, hi):
        return _dot(hn, win_ref[:, lo:hi])

    cos = cos_ref[...]
    sin = sin_ref[...]
    lane = lax.broadcasted_iota(jnp.int32, cos.shape, 1)
    first_half = (lane & (MLA_ROPE - 1)) < MLA_ROPE // 2
    low_lanes = lane < MLA_ROPE

    cq = _rms(proj(C_CQ, C_CKV), qan_ref[...]).astype(BF16)
    qn = _dot(cq, wuq_ref[:, 0:MLA_HEADS * MLA_NOPE]) * ATTN_SCALE
    qr = _dot(cq, wuq_ref[:, MLA_HEADS * MLA_NOPE:])
    for j in range(MLA_HEADS // 2):
        rj = (_rope_pairs(qr[:, j * LANES:(j + 1) * LANES], cos, sin, first_half) * ATTN_SCALE).astype(BF16)
        for h in (2 * j, 2 * j + 1):
            q_ref[h, :, 0:LANES] = qn[:, h * LANES:(h + 1) * LANES].astype(BF16)
            q_ref[h, :, LANES:QK_WIDTH] = rj

    ckv = _rms(proj(C_CKV, C_KPE), kvan_ref[...]).astype(BF16)
    kv = _dot(ckv, wukv_ref[...])
    kr = _rope_pairs(proj(C_KPE, C_GQ), cos, sin, first_half)
    kr_even = jnp.where(low_lanes, kr, 0.0).astype(BF16)
    kr_odd = jnp.where(low_lanes, 0.0, kr).astype(BF16)
    for h in range(MLA_HEADS):
        base = h * (MLA_NOPE + MLA_V)
        k_ref[h, :, 0:LANES] = kv[:, base:base + MLA_NOPE].astype(BF16)
        k_ref[h, :, LANES:QK_WIDTH] = kr_even if h % 2 == 0 else kr_odd
        v_ref[h, :, 0:MLA_V] = kv[:, base + MLA_NOPE:base + MLA_NOPE + MLA_V].astype(BF16)
        v_ref[h, :, MLA_V:V_WIDTH] = jnp.ones((kv.shape[0], V_WIDTH - MLA_V), BF16)

    gq_ref[...] = (proj(C_GQ, C_GK) * (GLA_DK ** -0.5)).astype(BF16)
    gk_ref[...] = proj(C_GK, C_GV).astype(BF16)
    gv_ref[...] = proj(C_GV, C_OG).astype(BF16)
    og_ref[...] = proj(C_OG, C_LR).astype(BF16)
    pre = _dot(proj(C_LR, D_IN_PACKED).astype(BF16), wgate_ref[...]) + bgate_ref[...]
    logsig = jnp.minimum(pre, 0.0) - jnp.log1p(jnp.exp(-jnp.abs(pre)))
    gates = logsig * (1.0 / GLA_TAU)
    gf_ref[...] = gates[:, 0:GLA_HEADS * GLA_DK]
    gb_ref[...] = gates[:, GLA_HEADS * GLA_DK:]


def _inproj(x2d, cos, sin, w, tm):
    t = x2d.shape[0]
    blocks_per_seq = cos.shape[0] // tm
    hk = GLA_HEADS * GLA_DK
    hv = GLA_HEADS * GLA_DV
    row = lambda width: pl.BlockSpec((tm, width), lambda i: (i, 0))
    head_rows = lambda width: pl.BlockSpec((MLA_HEADS, tm, width), lambda i: (0, i, 0))
    tab = pl.BlockSpec((tm, LANES), lambda i: (i % blocks_per_seq, 0))
    out_shape = (
        jax.ShapeDtypeStruct((MLA_HEADS, t, QK_WIDTH), BF16),
        jax.ShapeDtypeStruct((MLA_HEADS, t, QK_WIDTH), BF16),
        jax.ShapeDtypeStruct((MLA_HEADS, t, V_WIDTH), BF16),
        jax.ShapeDtypeStruct((t, hk), BF16),
        jax.ShapeDtypeStruct((t, hk), BF16),
        jax.ShapeDtypeStruct((t, hv), BF16),
        jax.ShapeDtypeStruct((t, hk), F32),
        jax.ShapeDtypeStruct((t, hk), F32),
        jax.ShapeDtypeStruct((t, hv), BF16),
    )
    return pl.pallas_call(
        _inproj_body,
        grid=(t // tm,),
        in_specs=[row(D_MODEL), tab, tab,
                  _full_spec((1, D_MODEL)), _full_spec((D_MODEL, D_IN_PACKED)),
                  _full_spec((1, MLA_Q_RANK)), _full_spec(w['wuq'].shape),
                  _full_spec((1, MLA_KV_RANK)), _full_spec(w['wukv'].shape),
                  _full_spec(w['wgate'].shape), _full_spec(w['bgate'].shape)],
        out_specs=(head_rows(QK_WIDTH), head_rows(QK_WIDTH), head_rows(V_WIDTH),
                   row(hk), row(hk), row(hv), row(hk), row(hk), row(hv)),
        out_shape=out_shape,
        compiler_params=_cparams("parallel"),
        name="inproj",
    )(x2d, cos, sin, w['norm_mix'], w['win'], w['q_a_norm'], w['wuq'], w['kv_a_norm'], w['wukv'],
      w['wgate'], w['bgate'])


def _attn_body(q_ref, k_ref, v_ref, km_ref, vm_ref, o_ref, s_ref, acc_ref, *, tk):
    q = q_ref[...]
    n_groups = k_ref.shape[0] // (ATTN_GROUP * tk)

    def scores(j):
        return _dot_nt(q, k_ref[pl.ds(pl.multiple_of(j * tk, tk), tk), :])

    def values(j):
        return v_ref[pl.ds(pl.multiple_of(j * tk, tk), tk), :]

    def absorb(m, s, v):
        m_new = jnp.maximum(m, jnp.max(s, axis=-1, keepdims=True))
        p = jnp.exp2(s - m_new)
        acc_ref[...] = jnp.exp2(m - m_new) * acc_ref[...] + _dot(p.astype(BF16), v)
        return m_new

    s_ref[0] = scores(0)
    sm = _dot_nt(q, km_ref[...])
    sm = jnp.where(lax.broadcasted_iota(jnp.int32, sm.shape, 1) < N_META, sm, -jnp.inf)
    m = jnp.max(sm, axis=-1, keepdims=True)
    acc_ref[...] = _dot(jnp.exp2(sm - m).astype(BF16), vm_ref[...])

    def group(g, m, last):
        for i in range(ATTN_GROUP):
            j = ATTN_GROUP * g + i
            s = s_ref[i % 2]
            if not (last and i == ATTN_GROUP - 1):
                s_ref[(i + 1) % 2] = scores(j + 1)
            m = absorb(m, s, values(j))
        return m

    m = lax.fori_loop(0, n_groups - 1, lambda g, m: group(g, m, False), m)
    group(n_groups - 1, m, True)
    acc = acc_ref[...]
    o_ref[...] = (acc[:, :MLA_V] / acc[:, MLA_V:]).astype(o_ref.dtype)


def _attention(q, k, v, km, vm, bsz, seq, tq, tk):
    nq = seq // tq
    return pl.pallas_call(
        functools.partial(_attn_body, tk=tk),
        grid=(bsz, MLA_HEADS, nq),
        in_specs=[pl.BlockSpec((None, tq, QK_WIDTH), lambda b, h, i: (h, b * nq + i, 0)),
                  pl.BlockSpec((None, seq, QK_WIDTH), lambda b, h, i: (h, b, 0)),
                  pl.BlockSpec((None, seq, V_WIDTH), lambda b, h, i: (h, b, 0)),
                  pl.BlockSpec((None, LANES, QK_WIDTH), lambda b, h, i: (h, 0, 0)),
                  pl.BlockSpec((None, LANES, V_WIDTH), lambda b, h, i: (h, 0, 0))],
        out_specs=pl.BlockSpec((tq, MLA_V), lambda b, h, i: (b * nq + i, h)),
        out_shape=jax.ShapeDtypeStruct((bsz * seq, MLA_HEADS * MLA_V), BF16),
        scratch_shapes=[pltpu.VMEM((2, tq, tk), F32), pltpu.VMEM((tq, V_WIDTH), F32)],
        compiler_params=_cparams("parallel", "parallel", "arbitrary"),
        name="mla_attention",
    )(q, k, v, km, vm)


def _split3(x):
    hi = x.astype(BF16)
    r1 = x - hi.astype(F32)
    mid = r1.astype(BF16)
    lo = (r1 - mid.astype(F32)).astype(BF16)
    return hi, mid, lo


def _gla_log_decay(g, tri):
    g_hi, g_mid, g_lo = _split3(g)
    return _dot(tri, g_hi) + _dot(tri, g_mid) + _dot(tri, g_lo)


def _head_rows(x):
    even = lax.broadcasted_iota(jnp.int32, x.shape, 1) < GLA_DK
    return jnp.concatenate([jnp.where(even, x, 0.0), jnp.where(even, 0.0, x)], axis=0).astype(BF16)


def _gla_operands(q, k, b, mid, last):
    b_last = b[last:last + 1, :]
    ke = (k * jnp.exp(b_last - b)).astype(BF16)
    if q is None:
        return jnp.exp(b_last), ke, None, None, None
    b_mid = b[mid:mid + 1, :]
    ks = (k * jnp.exp(b_mid - b)).astype(BF16)
    return jnp.exp(b_last), ke, ks, _head_rows(q * jnp.exp(b - b_mid)), _head_rows(q * jnp.exp(b))


def _gla_state_update(v_even, v_odd, ke):
    return jnp.where(lax.broadcasted_iota(jnp.int32, (GLA_DV, LANES), 1) < GLA_DK,
                     _dot_tn(v_even, ke), _dot_tn(v_odd, ke))


def _gla_intra(v_even, v_odd, ks, qs2, keep):
    scores = jnp.where(keep, _dot_nt(qs2, ks), 0.0).astype(BF16)
    return _dot(scores[:GLA_CHUNK], v_even), _dot(scores[GLA_CHUNK:], v_odd)


def _gla_body(qf_ref, kf_ref, vf_ref, gf_ref, qb_ref, kb_ref, vb_ref, gb_ref, mk_ref, mv_ref, mg_ref,
              of_ref, ob_ref, state_ref):
    n_chunks = qf_ref.shape[0] // GLA_CHUNK
    n_pairs = GLA_HEADS // 2
    r = lax.broadcasted_iota(jnp.int32, (GLA_CHUNK, GLA_CHUNK), 0)
    c = lax.broadcasted_iota(jnp.int32, (GLA_CHUNK, GLA_CHUNK), 1)
    tri_f = jnp.where(c <= r, 1.0, 0.0).astype(BF16)
    tri_b = jnp.where(c >= r, 1.0, 0.0).astype(BF16)
    r2 = lax.broadcasted_iota(jnp.int32, (2 * GLA_CHUNK, GLA_CHUNK), 0) & (GLA_CHUNK - 1)
    c2 = lax.broadcasted_iota(jnp.int32, (2 * GLA_CHUNK, GLA_CHUNK), 1)
    keep_f = c2 <= r2
    keep_b = c2 >= r2
    mid_f, last_f = GLA_CHUNK // 2 - 1, GLA_CHUNK - 1
    mid_b, last_b = GLA_CHUNK // 2, 0

    def pair_cols(p):
        return slice(p * LANES, (p + 1) * LANES)

    def head_cols(h):
        return slice(h * GLA_DV, (h + 1) * GLA_DV)

    @pl.when(pl.program_id(1) == 0)
    def _():
        for p in range(n_pairs):
            b = _gla_log_decay(mg_ref[:, pair_cols(p)], tri_f)
            _, ke, _, _, _ = _gla_operands(None, mk_ref[:, pair_cols(p)], b, mid_f, last_f)
            state_ref[p] = _gla_state_update(mv_ref[:, head_cols(2 * p)].astype(BF16),
                                             mv_ref[:, head_cols(2 * p + 1)].astype(BF16), ke)
            state_ref[n_pairs + p] = jnp.zeros((GLA_DV, LANES), F32)

    scans = []
    for p in range(n_pairs):
        scans.append((p, list(range(n_chunks)), qf_ref, kf_ref, vf_ref, gf_ref, of_ref, p,
                      tri_f, keep_f, mid_f, last_f))
        scans.append((p, list(reversed(range(n_chunks))), qb_ref, kb_ref, vb_ref, gb_ref, ob_ref, n_pairs + p,
                      tri_b, keep_b, mid_b, last_b))

    def rows(c):
        return slice(c * GLA_CHUNK, (c + 1) * GLA_CHUNK)

    def values(v_ref, p, c):
        return (v_ref[rows(c), head_cols(2 * p)].astype(BF16), v_ref[rows(c), head_cols(2 * p + 1)].astype(BF16))

    log_decay = [[_gla_log_decay(g_ref[rows(c), pair_cols(p)], tri) for c in order]
                 for (p, order, _, _, _, g_ref, _, _, tri, _, _, _) in scans]
    operands = [[_gla_operands(q_ref[rows(c), pair_cols(p)], k_ref[rows(c), pair_cols(p)], b, mid, last)
                 for c, b in zip(order, bs)]
                for (p, order, q_ref, k_ref, _, _, _, _, _, _, mid, last), bs in zip(scans, log_decay)]
    updates = [[_gla_state_update(*values(v_ref, p, c), ops[1]) for c, ops in zip(order, opss)]
               for (p, order, _, _, v_ref, _, _, _, _, _, _, _), opss in zip(scans, operands)]
    intra = [[_gla_intra(*values(v_ref, p, c), ops[2], ops[3], keep) for c, ops in zip(order, opss)]
             for (p, order, _, _, v_ref, _, _, _, _, keep, _, _), opss in zip(scans, operands)]
    states = []
    for (_, order, _, _, _, _, _, slot, _, _, _, _), opss, upds in zip(scans, operands, updates):
        st = state_ref[slot]
        entering = []
        for ops, upd in zip(opss, upds):
            entering.append(st.astype(BF16))
            st = st * ops[0] + upd
        state_ref[slot] = st
        states.append(entering)
    for (p, order, _, _, _, _, o_ref, _, _, _, _, _), opss, sts, locs in zip(scans, operands, states, intra):
        for c, ops, st, (o_even, o_odd) in zip(order, opss, sts, locs):
            inter = _dot_nt(ops[4], st)
            o_ref[rows(c), head_cols(2 * p)] = (o_even + inter[:GLA_CHUNK]).astype(o_ref.dtype)
            o_ref[rows(c), head_cols(2 * p + 1)] = (o_odd + inter[GLA_CHUNK:]).astype(o_ref.dtype)


def _gla(gq, gk, gv, gf, gb, mk, mv, mg, bsz, seq, tb):
    nb = seq // tb
    hk = GLA_HEADS * GLA_DK
    hv = GLA_HEADS * GLA_DV
    fwd = lambda width: pl.BlockSpec((tb, width), lambda b, j: (b * nb + j, 0))
    bwd = lambda width: pl.BlockSpec((tb, width), lambda b, j: (b * nb + nb - 1 - j, 0))
    t = bsz * seq
    return pl.pallas_call(
        _gla_body,
        grid=(bsz, nb),
        in_specs=[fwd(hk), fwd(hk), fwd(hv), fwd(hk), bwd(hk), bwd(hk), bwd(hv), bwd(hk),
                  _full_spec(mk.shape), _full_spec(mv.shape), _full_spec(mg.shape)],
        out_specs=(fwd(hv), bwd(hv)),
        out_shape=(jax.ShapeDtypeStruct((t, hv), BF16), jax.ShapeDtypeStruct((t, hv), BF16)),
        scratch_shapes=[pltpu.VMEM((2 * (GLA_HEADS // 2), GLA_DV, LANES), F32)],
        compiler_params=_cparams("parallel", "arbitrary"),
        name="gla_scan",
    )(gq, gk, gv, gf, gq, gk, gv, gb, mk, mv, mg)


def _mix_body(x_ref, a_ref, of_ref, ob_ref, og_ref, gnorm_ref, wout_ref, nffn_ref, wr_hi_ref, wr_lo_ref, br_ref,
              h1_ref, hn_ref, route_ref, ids_ref, count_ref, tri_ref):
    tm = x_ref.shape[0]

    @pl.when(pl.program_id(0) == 0)
    def _():
        r = lax.broadcasted_iota(jnp.int32, (tm, tm), 0)
        c = lax.broadcasted_iota(jnp.int32, (tm, tm), 1)
        tri_ref[...] = jnp.where(c < r, 1.0, 0.0).astype(BF16)
        count_ref[...] = jnp.zeros_like(count_ref)

    a_width = MLA_HEADS * MLA_V
    h1 = x_ref[...] + _dot(a_ref[...], wout_ref[0:a_width, :])
    for h in range(GLA_HEADS):
        cols = slice(h * GLA_DV, (h + 1) * GLA_DV)
        o = of_ref[:, cols].astype(F32) + ob_ref[:, cols].astype(F32)
        og = og_ref[:, cols].astype(F32)
        silu = og / (1.0 + jnp.exp(-og))
        gh = (_rms(o, gnorm_ref[...]) * silu).astype(BF16)
        h1 = h1 + _dot(gh, wout_ref[a_width + h * GLA_DV:a_width + (h + 1) * GLA_DV, :])
    h1_ref[...] = h1
    hn = _rms(h1, nffn_ref[...])
    hn_ref[...] = _rows_to_tiles(hn)

    hn_hi = hn.astype(BF16)
    hn_lo = (hn - hn_hi.astype(F32)).astype(BF16)
    logits = (_dot(hn_hi, wr_hi_ref[...]) + _dot(hn_lo, wr_hi_ref[...]) + _dot(hn_hi, wr_lo_ref[...])
              + br_ref[...])
    lane = lax.broadcasted_iota(jnp.int32, logits.shape, 1).astype(F32)
    none = float(LANES)
    neg = -jnp.inf

    def lane_max(x):
        return jnp.max(x, axis=-1, keepdims=True)

    def lane_sum(x):
        return jnp.sum(x, axis=-1, keepdims=True)

    def first_lane(mask):
        return jnp.min(jnp.where(mask, lane, none), axis=-1, keepdims=True)

    is_group = lane < float(N_GROUPS)
    g_max = lane_max(jnp.where(is_group, logits, neg))
    g_exp = jnp.where(is_group, jnp.exp(logits - g_max), 0.0)
    g_prob = g_exp / lane_sum(g_exp)
    g_w = lane_max(g_prob)
    g_idx = first_lane(is_group & (g_prob == g_w))
    e_lo = float(ROUTER_EXPERT_LANE0) + float(EXPERTS_PER_GROUP) * g_idx
    sel = (lane >= e_lo) & (lane < e_lo + float(EXPERTS_PER_GROUP))
    e_max = lane_max(jnp.where(sel, logits, neg))
    e_exp = jnp.where(sel, jnp.exp(logits - e_max), 0.0)
    e_prob = e_exp / lane_sum(e_exp)
    p1 = lane_max(jnp.where(sel, e_prob, neg))
    i1 = first_lane(sel & (e_prob == p1))
    rest = sel & (lane != i1)
    p2 = lane_max(jnp.where(rest, e_prob, neg))
    i2 = first_lane(rest & (e_prob == p2))
    denom = p1 + p2
    chosen = jnp.where((lane == i1) | (lane == i2), 1.0, 0.0)
    rank = count_ref[...] + _dot(tri_ref[...], chosen.astype(BF16))
    count_ref[...] += jnp.sum(chosen, axis=0, keepdims=True)
    fields = (i1 - float(ROUTER_EXPERT_LANE0), i2 - float(ROUTER_EXPERT_LANE0),
              lane_sum(jnp.where(lane == i1, rank, 0.0)), lane_sum(jnp.where(lane == i2, rank, 0.0)),
              g_w * (p1 / denom), g_w * (p2 / denom))
    route = jnp.zeros_like(logits)
    for k, value in enumerate(fields):
        route = jnp.where(lane == float(k), value, route)
    route_ref[...] = route
    ids_ref[...] = jnp.transpose(route)[0:ROUTE_ID_ROWS, :].astype(jnp.int32)


ROUTE_E1, ROUTE_E2, ROUTE_RANK1, ROUTE_RANK2, ROUTE_W1, ROUTE_W2 = range(6)
ROUTE_ID_ROWS = 8


def _mix(x2d, a, o_f, o_b, og, w, tm):
    t = x2d.shape[0]
    hv = GLA_HEADS * GLA_DV
    row = lambda width: pl.BlockSpec((tm, width), lambda i: (i, 0))
    return pl.pallas_call(
        _mix_body,
        grid=(t // tm,),
        in_specs=[row(D_MODEL), row(MLA_HEADS * MLA_V), row(hv), row(hv), row(hv),
                  _full_spec((1, GLA_DV)), _full_spec(w['wout'].shape), _full_spec((1, D_MODEL)),
                  _full_spec(w['wr_hi'].shape), _full_spec(w['wr_lo'].shape), _full_spec(w['br'].shape)],
        out_specs=(row(D_MODEL), pl.BlockSpec((tm, ROW_CHUNKS, LANES), lambda i: (i, 0, 0)), row(LANES),
                   pl.BlockSpec((ROUTE_ID_ROWS, tm), lambda i: (0, i)), _full_spec((1, LANES))),
        out_shape=(jax.ShapeDtypeStruct((t, D_MODEL), F32), jax.ShapeDtypeStruct((t, ROW_CHUNKS, LANES), F32),
                   jax.ShapeDtypeStruct((t, LANES), F32), jax.ShapeDtypeStruct((ROUTE_ID_ROWS, t), jnp.int32),
                   jax.ShapeDtypeStruct((1, LANES), F32)),
        scratch_shapes=[pltpu.VMEM((tm, tm), BF16)],
        compiler_params=_cparams("arbitrary"),
        name="mix_router",
    )(x2d, a, o_f, o_b, og, w['gla_norm'], w['wout'], w['norm_ffn'], w['wr_hi'], w['wr_lo'], w['br'])


EXPERT_TILE = 256
ROW_TILE = 256


SC_CORES = 2
SC_SUBCORES = 16
SC_GATHER_ROWS = 32


def _sc_gather(table, idx):
    n = idx.shape[0]
    workers = SC_CORES * SC_SUBCORES
    per_worker = n // workers
    assert n % (workers * SC_GATHER_ROWS) == 0
    mesh = plsc.VectorSubcoreMesh(core_axis_name="c", subcore_axis_name="s")

    @functools.partial(
        pl.kernel, mesh=mesh,
        out_type=jax.ShapeDtypeStruct((n,) + table.shape[1:], table.dtype),
        scratch_types=[pltpu.VMEM((SC_GATHER_ROWS,), jnp.int32),
                       pltpu.VMEM((SC_GATHER_ROWS,) + table.shape[1:], table.dtype),
                       pltpu.SemaphoreType.DMA])
    def gather(table_ref, idx_ref, out_ref, idx_buf, rows_buf, sem):
        base = (lax.axis_index("s") * SC_CORES + lax.axis_index("c")) * per_worker

        @pl.loop(0, per_worker // SC_GATHER_ROWS)
        def _(j):
            rows = pl.ds(base + j * SC_GATHER_ROWS, SC_GATHER_ROWS)
            pltpu.sync_copy(idx_ref.at[rows], idx_buf)
            pltpu.async_copy(table_ref.at[idx_buf], rows_buf, sem).wait()
            pltpu.sync_copy(rows_buf, out_ref.at[rows])

    return gather(table, idx)


def _sc_scatter(rows, idx, n_out):
    copies, n = idx.shape
    idx = idx.reshape(copies * n)
    workers = SC_CORES * SC_SUBCORES
    per_worker = n // workers
    assert n % (workers * SC_GATHER_ROWS) == 0
    mesh = plsc.VectorSubcoreMesh(core_axis_name="c", subcore_axis_name="s")

    @functools.partial(
        pl.kernel, mesh=mesh,
        out_type=jax.ShapeDtypeStruct((n_out,) + rows.shape[1:], rows.dtype),
        scratch_types=[pltpu.VMEM((SC_GATHER_ROWS,), jnp.int32),
                       pltpu.VMEM((SC_GATHER_ROWS,) + rows.shape[1:], rows.dtype),
                       pltpu.SemaphoreType.DMA])
    def scatter(rows_ref, idx_ref, out_ref, idx_buf, rows_buf, sem):
        base = (lax.axis_index("s") * SC_CORES + lax.axis_index("c")) * per_worker

        @pl.loop(0, per_worker // SC_GATHER_ROWS)
        def _(j):
            first = base + j * SC_GATHER_ROWS
            pltpu.sync_copy(rows_ref.at[pl.ds(first, SC_GATHER_ROWS)], rows_buf)
            for k in range(copies):
                pltpu.sync_copy(idx_ref.at[pl.ds(k * n + first, SC_GATHER_ROWS)], idx_buf)
                pltpu.async_copy(rows_buf, out_ref.at[idx_buf], sem).wait()

    return scatter(rows, idx)


SLOT_TILE = 2048


def _slots_body(starts_ref, ids_ref, slots_ref):
    ids = ids_ref[...]
    experts = ids[ROUTE_E1:ROUTE_E1 + 2, :]
    start = jnp.zeros_like(experts)
    for e in range(N_EXPERTS):
        start = jnp.where(experts == e, starts_ref[e], start)
    slots_ref[...] = start + ids[ROUTE_RANK1:ROUTE_RANK1 + 2, :]


def _slots(starts, ids):
    t = ids.shape[1]
    return pl.pallas_call(
        _slots_body,
        grid_spec=pltpu.PrefetchScalarGridSpec(
            num_scalar_prefetch=1,
            grid=(t // SLOT_TILE,),
            in_specs=[pl.BlockSpec((ROUTE_ID_ROWS, SLOT_TILE), lambda i, *_: (0, i))],
            out_specs=pl.BlockSpec((2, SLOT_TILE), lambda i, *_: (0, i))),
        out_shape=jax.ShapeDtypeStruct((2, t), jnp.int32),
        compiler_params=_cparams("parallel"),
        name="moe_slots",
    )(starts, ids)


def _queue_body(slots_ref, token_ref, *, n_tokens):
    i = pl.program_id(0)

    @pl.when(i == 0)
    def _():
        for base in range(0, token_ref.shape[0], n_tokens):
            def clear(j, carry):
                token_ref[base + j] = j
                return carry
            lax.fori_loop(0, min(n_tokens, token_ref.shape[0] - base), clear, 0, unroll=8)

    def place(r, carry):
        for k in range(2):
            token_ref[slots_ref[k, r]] = i * SLOT_TILE + r
        return carry

    lax.fori_loop(0, SLOT_TILE, place, 0, unroll=8)


def _queue_tokens(slots, n_rows):
    t = slots.shape[1]
    return pl.pallas_call(
        functools.partial(_queue_body, n_tokens=t),
        grid=(t // SLOT_TILE,),
        in_specs=[pl.BlockSpec((2, SLOT_TILE), lambda i: (0, i), memory_space=pltpu.SMEM)],
        out_specs=pl.BlockSpec(memory_space=pltpu.SMEM),
        out_shape=jax.ShapeDtypeStruct((n_rows,), jnp.int32),
        compiler_params=_cparams("arbitrary"),
        name="moe_queue",
    )(slots)


TILES_PER_STEP = 2


def _expert_body(tile_expert_ref, n_tiles_ref, tile_valid_ref, xs_ref, *refs):
    wg_refs = refs[0:TILES_PER_STEP]
    wu_refs = refs[TILES_PER_STEP:2 * TILES_PER_STEP]
    wd_refs = refs[2 * TILES_PER_STEP:3 * TILES_PER_STEP]
    ys_ref = refs[3 * TILES_PER_STEP]
    first = pl.program_id(0) * TILES_PER_STEP
    used = first < n_tiles_ref[0]

    @pl.when(used)
    def _():
        rows = [slice(j * EXPERT_TILE, (j + 1) * EXPERT_TILE) for j in range(TILES_PER_STEP)]
        xs = [_tiles_to_rows(xs_ref[r]) for r in rows]
        xs = [jnp.where(lax.broadcasted_iota(jnp.int32, x.shape, 0) < tile_valid_ref[first + j], x, 0.0)
              for j, x in enumerate(xs)]
        xs = [x.astype(BF16) for x in xs]
        gates = [_dot(x, wg[...]) for x, wg in zip(xs, wg_refs)]
        ups = [_dot(x, wu[...]) for x, wu in zip(xs, wu_refs)]
        acts = [(g / (1.0 + jnp.exp(-g))) * u for g, u in zip(gates, ups)]
        ys = [_dot(act.astype(BF16), wd[...]) for act, wd in zip(acts, wd_refs)]
        for r, y in zip(rows, ys):
            ys_ref[r] = _rows_to_tiles(_pack_bf16_pairs(y))

    @pl.when(jnp.logical_not(used))
    def _():
        ys_ref[...] = jnp.zeros_like(ys_ref)


def _experts(tile_expert, n_tiles, tile_valid, xs, w):
    n_steps = xs.shape[0] // (EXPERT_TILE * TILES_PER_STEP)

    def step(i, nt):
        return jnp.minimum(i, (nt[0] - 1) // TILES_PER_STEP)

    def expert(i, j, te, nt):
        return te[jnp.minimum(step(i, nt) * TILES_PER_STEP + j, nt[0] - 1)]

    tiles = lambda index: pl.BlockSpec((EXPERT_TILE * TILES_PER_STEP, ROW_CHUNKS, LANES), index)
    up_spec = lambda j: pl.BlockSpec((None, D_MODEL, D_EXPERT), lambda i, te, nt, tv: (expert(i, j, te, nt), 0, 0))
    down_spec = lambda j: pl.BlockSpec((None, D_EXPERT, D_MODEL), lambda i, te, nt, tv: (expert(i, j, te, nt), 0, 0))
    slots = range(TILES_PER_STEP)
    return pl.pallas_call(
        _expert_body,
        grid_spec=pltpu.PrefetchScalarGridSpec(
            num_scalar_prefetch=3,
            grid=(n_steps,),
            in_specs=([tiles(lambda i, te, nt, tv: (step(i, nt), 0, 0))]
                      + [up_spec(j) for j in slots] + [up_spec(j) for j in slots] + [down_spec(j) for j in slots]),
            out_specs=pl.BlockSpec((EXPERT_TILE * TILES_PER_STEP, PACKED_CHUNKS, LANES),
                                   lambda i, te, nt, tv: (i, 0, 0))),
        out_shape=jax.ShapeDtypeStruct((xs.shape[0], PACKED_CHUNKS, LANES), jnp.uint32),
        compiler_params=_cparams("arbitrary"),
        name="moe_experts",
    )(tile_expert, n_tiles, tile_valid, xs, *([w['w_gate']] * TILES_PER_STEP), *([w['w_up']] * TILES_PER_STEP),
      *([w['w_down']] * TILES_PER_STEP))


def _combine_body(h1_ref, route_ref, nfin_ref, y1_ref, y2_ref, out_ref):
    route = route_ref[...]
    lane = lax.broadcasted_iota(jnp.int32, route.shape, 1)
    w1 = jnp.sum(jnp.where(lane == ROUTE_W1, route, 0.0), axis=-1, keepdims=True)
    w2 = jnp.sum(jnp.where(lane == ROUTE_W2, route, 0.0), axis=-1, keepdims=True)
    y1 = _unpack_bf16_pairs(_tiles_to_rows(y1_ref[...]))
    y2 = _unpack_bf16_pairs(_tiles_to_rows(y2_ref[...]))
    y = w1 * y1 + w2 * y2
    out_ref[...] = _rms(h1_ref[...] + y, nfin_ref[...])


def _combine(h1, route, y12, w):
    t = h1.shape[0]
    row = lambda width: pl.BlockSpec((ROW_TILE, width), lambda i: (i, 0))
    tiles = lambda k: pl.BlockSpec((None, ROW_TILE, PACKED_CHUNKS, LANES), lambda i: (k, i, 0, 0))
    return pl.pallas_call(
        _combine_body,
        grid=(t // ROW_TILE,),
        in_specs=[row(D_MODEL), row(LANES), _full_spec((1, D_MODEL)), tiles(0), tiles(1)],
        out_specs=row(D_MODEL),
        out_shape=jax.ShapeDtypeStruct((t, D_MODEL), F32),
        compiler_params=_cparams("parallel"),
        name="moe_combine",
    )(h1, route, w['norm_final'], y12, y12)


def _moe(hn, route, ids, counts, h1, w):
    t = hn.shape[0]
    n_rows = 2 * t + N_EXPERTS * EXPERT_TILE
    n_grid_tiles = n_rows // EXPERT_TILE
    count = counts[0, ROUTER_EXPERT_LANE0:ROUTER_EXPERT_LANE0 + N_EXPERTS].astype(jnp.int32)
    padded = jnp.maximum((count + EXPERT_TILE - 1) // EXPERT_TILE, 1) * EXPERT_TILE
    ends = jnp.sum(jnp.where(jnp.arange(N_EXPERTS)[:, None] <= jnp.arange(N_EXPERTS)[None, :], padded[:, None], 0),
                   axis=0)
    starts = ends - padded
    n_tiles = (ends[-1:] // EXPERT_TILE)
    tile_rows = jnp.arange(n_grid_tiles, dtype=jnp.int32) * EXPERT_TILE
    tile_expert = jnp.minimum(jnp.sum((ends[None, :] <= tile_rows[:, None]).astype(jnp.int32), axis=1), N_EXPERTS - 1)
    tile_valid = jnp.clip(jnp.sum(jnp.where(jnp.arange(N_EXPERTS)[None, :] == tile_expert[:, None],
                                            (starts + count)[None, :], 0), axis=1) - tile_rows, 0, EXPERT_TILE)
    slots = _slots(starts, ids)
    ys = _experts(tile_expert, n_tiles, tile_valid, _sc_scatter(hn, slots, n_rows), w)
    y12 = _sc_gather(ys, slots.reshape(2 * t)).reshape(2, t, PACKED_CHUNKS, LANES)
    return _combine(h1, route, y12, w)


def _rope_tables(positions):
    inv_freq = 1.0 / (ROPE_THETA ** (jnp.arange(0, MLA_ROPE, 2, dtype=F32) / MLA_ROPE))
    ang = positions.astype(F32)[:, None] * inv_freq[None, :]
    cos, sin = jnp.cos(ang), jnp.sin(ang)
    reps = LANES // MLA_ROPE
    return jnp.tile(jnp.concatenate([cos, cos], axis=-1), (1, reps)), jnp.tile(jnp.concatenate([-sin, sin], axis=-1), (1, reps))


def _pack_weights(norm_mix, w_in, q_a_norm, w_uq, kv_a_norm, w_ukv, w_gate_fwd, b_gate_fwd, w_gate_bwd, b_gate_bwd,
                  gla_norm, w_out, norm_ffn, w_router_group, b_router_group, w_router_expert, b_router_expert,
                  w_expert_gate, w_expert_up, w_expert_down, norm_final):
    l = 0
    hk = GLA_HEADS * GLA_DK
    hv = GLA_HEADS * GLA_DV
    c_q, c_kv, k_pe, gq, gk, gv, lr_f, lr_b, og = jnp.split(
        w_in[l], np.cumsum([MLA_Q_RANK, MLA_KV_RANK, MLA_ROPE, hk, hk, hv, GLA_GATE_RANK, GLA_GATE_RANK])[:].tolist(),
        axis=-1)
    lr_pad = jnp.zeros((D_MODEL, LANES - 2 * GLA_GATE_RANK), F32)
    win = jnp.concatenate([c_q, c_kv, k_pe, k_pe, gq, gk, gv, og, lr_f, lr_b, lr_pad], axis=-1).astype(BF16)
    wuq = w_uq[l].reshape(MLA_Q_RANK, MLA_HEADS, MLA_NOPE + MLA_ROPE)
    wuq = jnp.concatenate([wuq[:, :, :MLA_NOPE].reshape(MLA_Q_RANK, -1), wuq[:, :, MLA_NOPE:].reshape(MLA_Q_RANK, -1)],
                          axis=-1).astype(BF16)
    wgate = jnp.zeros((LANES, 2 * hk), F32)
    wgate = wgate.at[0:GLA_GATE_RANK, 0:hk].set(w_gate_fwd[l])
    wgate = wgate.at[GLA_GATE_RANK:2 * GLA_GATE_RANK, hk:].set(w_gate_bwd[l])
    wr = jnp.zeros((D_MODEL, LANES), F32)
    wr = wr.at[:, ROUTER_GROUP_LANE0:ROUTER_GROUP_LANE0 + N_GROUPS].set(w_router_group[l])
    wr = wr.at[:, ROUTER_EXPERT_LANE0:ROUTER_EXPERT_LANE0 + N_EXPERTS].set(w_router_expert[l])
    wr_hi = wr.astype(BF16)
    br = jnp.zeros((1, LANES), F32)
    br = br.at[0, ROUTER_GROUP_LANE0:ROUTER_GROUP_LANE0 + N_GROUPS].set(b_router_group[l])
    br = br.at[0, ROUTER_EXPERT_LANE0:ROUTER_EXPERT_LANE0 + N_EXPERTS].set(b_router_expert[l])
    return {
        'norm_mix': norm_mix[l][None], 'win': win, 'q_a_norm': q_a_norm[l][None], 'wuq': wuq,
        'kv_a_norm': kv_a_norm[l][None], 'wukv': w_ukv[l].astype(BF16),
        'wgate': wgate.astype(BF16), 'bgate': jnp.concatenate([b_gate_fwd[l], b_gate_bwd[l]])[None],
        'gla_norm': gla_norm[l][None], 'wout': w_out[l].astype(BF16), 'norm_ffn': norm_ffn[l][None],
        'wr_hi': wr_hi, 'wr_lo': (wr - wr_hi.astype(F32)).astype(BF16), 'br': br,
        'w_gate': w_expert_gate[l].reshape(N_EXPERTS, D_MODEL, D_EXPERT).astype(BF16),
        'w_up': w_expert_up[l].reshape(N_EXPERTS, D_MODEL, D_EXPERT).astype(BF16),
        'w_down': w_expert_down[l].reshape(N_EXPERTS, D_EXPERT, D_MODEL).astype(BF16),
        'norm_final': norm_final[None],
    }


def _meta_streams(meta_tokens, w):
    cos, sin = _rope_tables(jnp.arange(N_META))
    _, k, v, _, gk, gv, gf, _, _ = _inproj(meta_tokens, cos, sin, w, N_META)
    pad_keys = ((0, 0), (0, LANES - N_META), (0, 0))
    front = ((GLA_CHUNK - N_META, 0), (0, 0))
    return (jnp.pad(k, pad_keys), jnp.pad(v, pad_keys), jnp.pad(gk, front), jnp.pad(gv, front), jnp.pad(gf, front))


def _token_mixers(x, meta, w, tm, tq, tk, tb):
    bsz, seq, _ = x.shape
    km, vm, mk, mv, mg = meta
    x2d = x.reshape(bsz * seq, D_MODEL)
    cos, sin = _rope_tables(N_META + jnp.arange(seq))
    q, k, v, gq, gk, gv, gf, gb, og = _inproj(x2d, cos, sin, w, tm)
    a = _attention(q, k, v, km, vm, bsz, seq, tq, tk)
    o_f, o_b = _gla(gq, gk, gv, gf, gb, mk, mv, mg, bsz, seq, tb)
    return _mix(x2d, a, o_f, o_b, og, w, tm)


def kernel(x_prompt, x_sample, meta_tokens, norm_mix, w_in, q_a_norm, w_uq, kv_a_norm, w_ukv, w_gate_fwd, b_gate_fwd, w_gate_bwd, b_gate_bwd, gla_norm, w_out, norm_ffn, w_router_group, b_router_group, w_router_expert, b_router_expert, w_expert_gate, w_expert_up, w_expert_down, norm_final):
    w = _pack_weights(norm_mix, w_in, q_a_norm, w_uq, kv_a_norm, w_ukv, w_gate_fwd, b_gate_fwd, w_gate_bwd,
                      b_gate_bwd, gla_norm, w_out, norm_ffn, w_router_group, b_router_group, w_router_expert,
                      b_router_expert, w_expert_gate, w_expert_up, w_expert_down, norm_final)
    meta = _meta_streams(meta_tokens, w)
    outs = []
    for x in (x_prompt, x_sample):
        h1, hn, route, ids, counts = _token_mixers(x, meta, w, tm=512, tq=1024, tk=512, tb=512)
        outs.append(_moe(hn, route, ids, counts, h1, w).reshape(x.shape))
    return tuple(outs)
```

```python
import functools

import numpy as np
import jax
import jax.numpy as jnp
from jax import lax
from jax.experimental import pallas as pl
from jax.experimental.pallas import tpu as pltpu
from jax.experimental.pallas import tpu_sc as plsc

F32 = jnp.float32
BF16 = jnp.bfloat16

D_MODEL = 1024
N_META = 16
MLA_HEADS = 4
MLA_Q_RANK = 384
MLA_KV_RANK = 256
MLA_NOPE = 128
MLA_ROPE = 64
MLA_V = 128
ROPE_THETA = 10000.0
GLA_HEADS = 4
GLA_DK = 64
GLA_DV = 128
GLA_GATE_RANK = 16
GLA_TAU = 16.0
GLA_CHUNK = 64
N_GROUPS = 4
EXPERTS_PER_GROUP = 8
N_EXPERTS = N_GROUPS * EXPERTS_PER_GROUP
D_EXPERT = 256
EPS = 1e-6

LANES = 128
V7X_VMEM_BYTES = 64 * 1024 * 1024
VMEM_LIMIT = V7X_VMEM_BYTES * 7 // 8

ATTN_SCALE = (MLA_NOPE + MLA_ROPE) ** -0.5 * float(np.log2(np.e))
QK_WIDTH = 2 * LANES
V_WIDTH = 2 * LANES
ATTN_GROUP = 8

C_CQ = 0
C_CKV = C_CQ + MLA_Q_RANK
C_KPE = C_CKV + MLA_KV_RANK
C_GQ = C_KPE + LANES
C_GK = C_GQ + GLA_HEADS * GLA_DK
C_GV = C_GK + GLA_HEADS * GLA_DK
C_OG = C_GV + GLA_HEADS * GLA_DV
C_LR = C_OG + GLA_HEADS * GLA_DV
D_IN_PACKED = C_LR + LANES

ROUTER_GROUP_LANE0 = 0
ROUTER_EXPERT_LANE0 = N_GROUPS


def _cparams(*semantics):
    return pltpu.CompilerParams(dimension_semantics=semantics, vmem_limit_bytes=VMEM_LIMIT)


def _rms(x, g):
    return x * lax.rsqrt(jnp.mean(x * x, axis=-1, keepdims=True) + EPS) * g


def _dot(a, b):
    return jnp.dot(a, b, preferred_element_type=F32)


def _dot_nt(a, b):
    return lax.dot_general(a, b, (((1,), (1,)), ((), ())), preferred_element_type=F32)


def _dot_tn(a, b):
    return lax.dot_general(a, b, (((0,), (0,)), ((), ())), preferred_element_type=F32)


def _full_spec(shape):
    return pl.BlockSpec(shape, lambda *_: (0,) * len(shape))


ROW_CHUNKS = D_MODEL // LANES


def _rows_to_tiles(x):
    chunks = jnp.stack([x[:, s * LANES:(s + 1) * LANES] for s in range(x.shape[1] // LANES)], axis=0)
    return pltpu.einshape("smd->msd", chunks)


def _tiles_to_rows(x):
    chunks = pltpu.einshape("msd->smd", x)
    return jnp.concatenate([chunks[s] for s in range(x.shape[1])], axis=-1)


PACKED_CHUNKS = ROW_CHUNKS // 2


def _pack_bf16_pairs(x):
    half = x.shape[1] // 2
    bits = lambda v: lax.bitcast_convert_type(v.astype(BF16).astype(F32), jnp.uint32)
    return (bits(x[:, half:]) & jnp.uint32(0xFFFF0000)) | (bits(x[:, :half]) >> 16)


def _unpack_bf16_pairs(w):
    lo = lax.bitcast_convert_type(w << 16, F32)
    hi = lax.bitcast_convert_type(w & jnp.uint32(0xFFFF0000), F32)
    return jnp.concatenate([lo, hi], axis=-1)


def _rope_pairs(x, cos, sin_signed, first_half):
    swapped = jnp.where(first_half, pltpu.roll(x, LANES - MLA_ROPE // 2, 1), pltpu.roll(x, MLA_ROPE // 2, 1))
    return x * cos + swapped * sin_signed


def _inproj_body(x_ref, cos_ref, sin_ref, nmix_ref, win_ref, qan_ref, wuq_ref, kvan_ref, wukv_ref,
                 wgate_ref, bgate_ref,
                 q_ref, k_ref, v_ref, gq_ref, gk_ref, gv_ref, gf_ref, gb_ref, og_ref):
    hn = _rms(x_ref[...], nmix_ref[...]).astype(BF16)

    def proj(lo, hi):
        return _dot(hn, win_ref[:, lo:hi])

    cos = cos_ref[...]
    sin = sin_ref[...]
    lane = lax.broadcasted_iota(jnp.int32, cos.shape, 1)
    first_half = (lane & (MLA_ROPE - 1)) < MLA_ROPE // 2
    low_lanes = lane < MLA_ROPE

    cq = _rms(proj(C_CQ, C_CKV), qan_ref[...]).astype(BF16)
    qn = _dot(cq, wuq_ref[:, 0:MLA_HEADS * MLA_NOPE]) * ATTN_SCALE
    qr = _dot(cq, wuq_ref[:, MLA_HEADS * MLA_NOPE:])
    for j in range(MLA_HEADS // 2):
        rj = (_rope_pairs(qr[:, j * LANES:(j + 1) * LANES], cos, sin, first_half) * ATTN_SCALE).astype(BF16)
        for h in (2 * j, 2 * j + 1):
            q_ref[h, :, 0:LANES] = qn[:, h * LANES:(h + 1) * LANES].astype(BF16)
            q_ref[h, :, LANES:QK_WIDTH] = rj

    ckv = _rms(proj(C_CKV, C_KPE), kvan_ref[...]).astype(BF16)
    kv = _dot(ckv, wukv_ref[...])
    kr = _rope_pairs(proj(C_KPE, C_GQ), cos, sin, first_half)
    kr_even = jnp.where(low_lanes, kr, 0.0).astype(BF16)
    kr_odd = jnp.where(low_lanes, 0.0, kr).astype(BF16)
    for h in range(MLA_HEADS):
        base = h * (MLA_NOPE + MLA_V)
        k_ref[h, :, 0:LANES] = kv[:, base:base + MLA_NOPE].astype(BF16)
        k_ref[h, :, LANES:QK_WIDTH] = kr_even if h % 2 == 0 else kr_odd
        v_ref[h, :, 0:MLA_V] = kv[:, base + MLA_NOPE:base + MLA_NOPE + MLA_V].astype(BF16)
        v_ref[h, :, MLA_V:V_WIDTH] = jnp.ones((kv.shape[0], V_WIDTH - MLA_V), BF16)

    gq_ref[...] = (proj(C_GQ, C_GK) * (GLA_DK ** -0.5)).astype(BF16)
    gk_ref[...] = proj(C_GK, C_GV).astype(BF16)
    gv_ref[...] = proj(C_GV, C_OG).astype(BF16)
    og_ref[...] = proj(C_OG, C_LR).astype(BF16)
    pre = _dot(proj(C_LR, D_IN_PACKED).astype(BF16), wgate_ref[...]) + bgate_ref[...]
    logsig = jnp.minimum(pre, 0.0) - jnp.log1p(jnp.exp(-jnp.abs(pre)))
    gates = logsig * (1.0 / GLA_TAU)
    gf_ref[...] = gates[:, 0:GLA_HEADS * GLA_DK]
    gb_ref[...] = gates[:, GLA_HEADS * GLA_DK:]


def _inproj(x2d, cos, sin, w, tm):
    t = x2d.shape[0]
    blocks_per_seq = cos.shape[0] // tm
    hk = GLA_HEADS * GLA_DK
    hv = GLA_HEADS * GLA_DV
    row = lambda width: pl.BlockSpec((tm, width), lambda i: (i, 0))
    head_rows = lambda width: pl.BlockSpec((MLA_HEADS, tm, width), lambda i: (0, i, 0))
    tab = pl.BlockSpec((tm, LANES), lambda i: (i % blocks_per_seq, 0))
    out_shape = (
        jax.ShapeDtypeStruct((MLA_HEADS, t, QK_WIDTH), BF16),
        jax.ShapeDtypeStruct((MLA_HEADS, t, QK_WIDTH), BF16),
        jax.ShapeDtypeStruct((MLA_HEADS, t, V_WIDTH), BF16),
        jax.ShapeDtypeStruct((t, hk), BF16),
        jax.ShapeDtypeStruct((t, hk), BF16),
        jax.ShapeDtypeStruct((t, hv), BF16),
        jax.ShapeDtypeStruct((t, hk), F32),
        jax.ShapeDtypeStruct((t, hk), F32),
        jax.ShapeDtypeStruct((t, hv), BF16),
    )
    return pl.pallas_call(
        _inproj_body,
        grid=(t // tm,),
        in_specs=[row(D_MODEL), tab, tab,
                  _full_spec((1, D_MODEL)), _full_spec((D_MODEL, D_IN_PACKED)),
                  _full_spec((1, MLA_Q_RANK)), _full_spec(w['wuq'].shape),
                  _full_spec((1, MLA_KV_RANK)), _full_spec(w['wukv'].shape),
                  _full_spec(w['wgate'].shape), _full_spec(w['bgate'].shape)],
        out_specs=(head_rows(QK_WIDTH), head_rows(QK_WIDTH), head_rows(V_WIDTH),
                   row(hk), row(hk), row(hv), row(hk), row(hk), row(hv)),
        out_shape=out_shape,
        compiler_params=_cparams("parallel"),
        name="inproj",
    )(x2d, cos, sin, w['norm_mix'], w['win'], w['q_a_norm'], w['wuq'], w['kv_a_norm'], w['wukv'],
      w['wgate'], w['bgate'])


def _attn_body(q_ref, k_ref, v_ref, km_ref, vm_ref, o_ref, s_ref, acc_ref, *, tk):
    q = q_ref[...]
    n_groups = k_ref.shape[0] // (ATTN_GROUP * tk)

    def scores(j):
        return _dot_nt(q, k_ref[pl.ds(pl.multiple_of(j * tk, tk), tk), :])

    def values(j):
        return v_ref[pl.ds(pl.multiple_of(j * tk, tk), tk), :]

    def absorb(m, s, v):
        m_new = jnp.maximum(m, jnp.max(s, axis=-1, keepdims=True))
        p = jnp.exp2(s - m_new)
        acc_ref[...] = jnp.exp2(m - m_new) * acc_ref[...] + _dot(p.astype(BF16), v)
        return m_new

    s_ref[0] = scores(0)
    sm = _dot_nt(q, km_ref[...])
    sm = jnp.where(lax.broadcasted_iota(jnp.int32, sm.shape, 1) < N_META, sm, -jnp.inf)
    m = jnp.max(sm, axis=-1, keepdims=True)
    acc_ref[...] = _dot(jnp.exp2(sm - m).astype(BF16), vm_ref[...])

    def group(g, m, last):
        for i in range(ATTN_GROUP):
            j = ATTN_GROUP * g + i
            s = s_ref[i % 2]
            if not (last and i == ATTN_GROUP - 1):
                s_ref[(i + 1) % 2] = scores(j + 1)
            m = absorb(m, s, values(j))
        return m

    m = lax.fori_loop(0, n_groups - 1, lambda g, m: group(g, m, False), m)
    group(n_groups - 1, m, True)
    acc = acc_ref[...]
    o_ref[...] = (acc[:, :MLA_V] / acc[:, MLA_V:]).astype(o_ref.dtype)


def _attention(q, k, v, km, vm, bsz, seq, tq, tk):
    nq = seq // tq
    return pl.pallas_call(
        functools.partial(_attn_body, tk=tk),
        grid=(bsz, MLA_HEADS, nq),
        in_specs=[pl.BlockSpec((None, tq, QK_WIDTH), lambda b, h, i: (h, b * nq + i, 0)),
                  pl.BlockSpec((None, seq, QK_WIDTH), lambda b, h, i: (h, b, 0)),
                  pl.BlockSpec((None, seq, V_WIDTH), lambda b, h, i: (h, b, 0)),
                  pl.BlockSpec((None, LANES, QK_WIDTH), lambda b, h, i: (h, 0, 0)),
                  pl.BlockSpec((None, LANES, V_WIDTH), lambda b, h, i: (h, 0, 0))],
        out_specs=pl.BlockSpec((tq, MLA_V), lambda b, h, i: (b * nq + i, h)),
        out_shape=jax.ShapeDtypeStruct((bsz * seq, MLA_HEADS * MLA_V), BF16),
        scratch_shapes=[pltpu.VMEM((2, tq, tk), F32), pltpu.VMEM((tq, V_WIDTH), F32)],
        compiler_params=_cparams("parallel", "parallel", "arbitrary"),
        name="mla_attention",
    )(q, k, v, km, vm)


def _split3(x):
    hi = x.astype(BF16)
    r1 = x - hi.astype(F32)
    mid = r1.astype(BF16)
    lo = (r1 - mid.astype(F32)).astype(BF16)
    return hi, mid, lo


def _gla_log_decay(g, tri):
    g_hi, g_mid, g_lo = _split3(g)
    return _dot(tri, g_hi) + _dot(tri, g_mid) + _dot(tri, g_lo)


def _head_rows(x):
    even = lax.broadcasted_iota(jnp.int32, x.shape, 1) < GLA_DK
    return jnp.concatenate([jnp.where(even, x, 0.0), jnp.where(even, 0.0, x)], axis=0).astype(BF16)


def _gla_operands(q, k, b, mid, last):
    b_last = b[last:last + 1, :]
    ke = (k * jnp.exp(b_last - b)).astype(BF16)
    if q is None:
        return jnp.exp(b_last), ke, None, None, None
    b_mid = b[mid:mid + 1, :]
    ks = (k * jnp.exp(b_mid - b)).astype(BF16)
    return jnp.exp(b_last), ke, ks, _head_rows(q * jnp.exp(b - b_mid)), _head_rows(q * jnp.exp(b))


def _gla_state_update(v_even, v_odd, ke):
    return jnp.where(lax.broadcasted_iota(jnp.int32, (GLA_DV, LANES), 1) < GLA_DK,
                     _dot_tn(v_even, ke), _dot_tn(v_odd, ke))


def _gla_intra(v_even, v_odd, ks, qs2, keep):
    scores = jnp.where(keep, _dot_nt(qs2, ks), 0.0).astype(BF16)
    return _dot(scores[:GLA_CHUNK], v_even), _dot(scores[GLA_CHUNK:], v_odd)


def _gla_body(qf_ref, kf_ref, vf_ref, gf_ref, qb_ref, kb_ref, vb_ref, gb_ref, mk_ref, mv_ref, mg_ref,
              of_ref, ob_ref, state_ref):
    n_chunks = qf_ref.shape[0] // GLA_CHUNK
    n_pairs = GLA_HEADS // 2
    r = lax.broadcasted_iota(jnp.int32, (GLA_CHUNK, GLA_CHUNK), 0)
    c = lax.broadcasted_iota(jnp.int32, (GLA_CHUNK, GLA_CHUNK), 1)
    tri_f = jnp.where(c <= r, 1.0, 0.0).astype(BF16)
    tri_b = jnp.where(c >= r, 1.0, 0.0).astype(BF16)
    r2 = lax.broadcasted_iota(jnp.int32, (2 * GLA_CHUNK, GLA_CHUNK), 0) & (GLA_CHUNK - 1)
    c2 = lax.broadcasted_iota(jnp.int32, (2 * GLA_CHUNK, GLA_CHUNK), 1)
    keep_f = c2 <= r2
    keep_b = c2 >= r2
    mid_f, last_f = GLA_CHUNK // 2 - 1, GLA_CHUNK - 1
    mid_b, last_b = GLA_CHUNK // 2, 0

    def pair_cols(p):
        return slice(p * LANES, (p + 1) * LANES)

    def head_cols(h):
        return slice(h * GLA_DV, (h + 1) * GLA_DV)

    @pl.when(pl.program_id(1) == 0)
    def _():
        for p in range(n_pairs):
            b = _gla_log_decay(mg_ref[:, pair_cols(p)], tri_f)
            _, ke, _, _, _ = _gla_operands(None, mk_ref[:, pair_cols(p)], b, mid_f, last_f)
            state_ref[p] = _gla_state_update(mv_ref[:, head_cols(2 * p)].astype(BF16),
                                             mv_ref[:, head_cols(2 * p + 1)].astype(BF16), ke)
            state_ref[n_pairs + p] = jnp.zeros((GLA_DV, LANES), F32)

    scans = []
    for p in range(n_pairs):
        scans.append((p, list(range(n_chunks)), qf_ref, kf_ref, vf_ref, gf_ref, of_ref, p,
                      tri_f, keep_f, mid_f, last_f))
        scans.append((p, list(reversed(range(n_chunks))), qb_ref, kb_ref, vb_ref, gb_ref, ob_ref, n_pairs + p,
                      tri_b, keep_b, mid_b, last_b))

    def rows(c):
        return slice(c * GLA_CHUNK, (c + 1) * GLA_CHUNK)

    def values(v_ref, p, c):
        return (v_ref[rows(c), head_cols(2 * p)].astype(BF16), v_ref[rows(c), head_cols(2 * p + 1)].astype(BF16))

    log_decay = [[_gla_log_decay(g_ref[rows(c), pair_cols(p)], tri) for c in order]
                 for (p, order, _, _, _, g_ref, _, _, tri, _, _, _) in scans]
    operands = [[_gla_operands(q_ref[rows(c), pair_cols(p)], k_ref[rows(c), pair_cols(p)], b, mid, last)
                 for c, b in zip(order, bs)]
                for (p, order, q_ref, k_ref, _, _, _, _, _, _, mid, last), bs in zip(scans, log_decay)]
    updates = [[_gla_state_update(*values(v_ref, p, c), ops[1]) for c, ops in zip(order, opss)]
               for (p, order, _, _, v_ref, _, _, _, _, _, _, _), opss in zip(scans, operands)]
    intra = [[_gla_intra(*values(v_ref, p, c), ops[2], ops[3], keep) for c, ops in zip(order, opss)]
             for (p, order, _, _, v_ref, _, _, _, _, keep, _, _), opss in zip(scans, operands)]
    states = []
    for (_, order, _, _, _, _, _, slot, _, _, _, _), opss, upds in zip(scans, operands, updates):
        st = state_ref[slot]
        entering = []
        for ops, upd in zip(opss, upds):
            entering.append(st.astype(BF16))
            st = st * ops[0] + upd
        state_ref[slot] = st
        states.append(entering)
    for (p, order, _, _, _, _, o_ref, _, _, _, _, _), opss, sts, locs in zip(scans, operands, states, intra):
        for c, ops, st, (o_even, o_odd) in zip(order, opss, sts, locs):
            inter = _dot_nt(ops[4], st)
            o_ref[rows(c), head_cols(2 * p)] = (o_even + inter[:GLA_CHUNK]).astype(o_ref.dtype)
            o_ref[rows(c), head_cols(2 * p + 1)] = (o_odd + inter[GLA_CHUNK:]).astype(o_ref.dtype)


def _gla(gq, gk, gv, gf, gb, mk, mv, mg, bsz, seq, tb):
    nb = seq // tb
    hk = GLA_HEADS * GLA_DK
    hv = GLA_HEADS * GLA_DV
    fwd = lambda width: pl.BlockSpec((tb, width), lambda b, j: (b * nb + j, 0))
    bwd = lambda width: pl.BlockSpec((tb, width), lambda b, j: (b * nb + nb - 1 - j, 0))
    t = bsz * seq
    return pl.pallas_call(
        _gla_body,
        grid=(bsz, nb),
        in_specs=[fwd(hk), fwd(hk), fwd(hv), fwd(hk), bwd(hk), bwd(hk), bwd(hv), bwd(hk),
                  _full_spec(mk.shape), _full_spec(mv.shape), _full_spec(mg.shape)],
        out_specs=(fwd(hv), bwd(hv)),
        out_shape=(jax.ShapeDtypeStruct((t, hv), BF16), jax.ShapeDtypeStruct((t, hv), BF16)),
        scratch_shapes=[pltpu.VMEM((2 * (GLA_HEADS // 2), GLA_DV, LANES), F32)],
        compiler_params=_cparams("parallel", "arbitrary"),
        name="gla_scan",
    )(gq, gk, gv, gf, gq, gk, gv, gb, mk, mv, mg)


def _mix_body(x_ref, a_ref, of_ref, ob_ref, og_ref, gnorm_ref, wout_ref, nffn_ref, wr_hi_ref, wr_lo_ref, br_ref,
              h1_ref, hn_ref, route_ref, ids_ref, count_ref, tri_ref):
    tm = x_ref.shape[0]

    @pl.when(pl.program_id(0) == 0)
    def _():
        r = lax.broadcasted_iota(jnp.int32, (tm, tm), 0)
        c = lax.broadcasted_iota(jnp.int32, (tm, tm), 1)
        tri_ref[...] = jnp.where(c < r, 1.0, 0.0).astype(BF16)
        count_ref[...] = jnp.zeros_like(count_ref)

    a_width = MLA_HEADS * MLA_V
    h1 = x_ref[...] + _dot(a_ref[...], wout_ref[0:a_width, :])
    for h in range(GLA_HEADS):
        cols = slice(h * GLA_DV, (h + 1) * GLA_DV)
        o = of_ref[:, cols].astype(F32) + ob_ref[:, cols].astype(F32)
        og = og_ref[:, cols].astype(F32)
        silu = og / (1.0 + jnp.exp(-og))
        gh = (_rms(o, gnorm_ref[...]) * silu).astype(BF16)
        h1 = h1 + _dot(gh, wout_ref[a_width + h * GLA_DV:a_width + (h + 1) * GLA_DV, :])
    h1_ref[...] = h1
    hn = _rms(h1, nffn_ref[...])
    hn_ref[...] = _rows_to_tiles(_pack_bf16_pairs(hn))

    hn_hi = hn.astype(BF16)
    hn_lo = (hn - hn_hi.astype(F32)).astype(BF16)
    logits = (_dot(hn_hi, wr_hi_ref[...]) + _dot(hn_lo, wr_hi_ref[...]) + _dot(hn_hi, wr_lo_ref[...])
              + br_ref[...])
    lane = lax.broadcasted_iota(jnp.int32, logits.shape, 1).astype(F32)
    none = float(LANES)
    neg = -jnp.inf

    def lane_max(x):
        return jnp.max(x, axis=-1, keepdims=True)

    def lane_sum(x):
        return jnp.sum(x, axis=-1, keepdims=True)

    def first_lane(mask):
        return jnp.min(jnp.where(mask, lane, none), axis=-1, keepdims=True)

    is_group = lane < float(N_GROUPS)
    g_max = lane_max(jnp.where(is_group, logits, neg))
    g_exp = jnp.where(is_group, jnp.exp(logits - g_max), 0.0)
    g_prob = g_exp / lane_sum(g_exp)
    g_w = lane_max(g_prob)
    g_idx = first_lane(is_group & (g_prob == g_w))
    e_lo = float(ROUTER_EXPERT_LANE0) + float(EXPERTS_PER_GROUP) * g_idx
    sel = (lane >= e_lo) & (lane < e_lo + float(EXPERTS_PER_GROUP))
    e_max = lane_max(jnp.where(sel, logits, neg))
    e_exp = jnp.where(sel, jnp.exp(logits - e_max), 0.0)
    e_prob = e_exp / lane_sum(e_exp)
    p1 = lane_max(jnp.where(sel, e_prob, neg))
    i1 = first_lane(sel & (e_prob == p1))
    rest = sel & (lane != i1)
    p2 = lane_max(jnp.where(rest, e_prob, neg))
    i2 = first_lane(rest & (e_prob == p2))
    denom = p1 + p2
    chosen = jnp.where((lane == i1) | (lane == i2), 1.0, 0.0)
    rank = count_ref[...] + _dot(tri_ref[...], chosen.astype(BF16))
    count_ref[...] += jnp.sum(chosen, axis=0, keepdims=True)
    fields = (i1 - float(ROUTER_EXPERT_LANE0), i2 - float(ROUTER_EXPERT_LANE0),
              lane_sum(jnp.where(lane == i1, rank, 0.0)), lane_sum(jnp.where(lane == i2, rank, 0.0)),
              g_w * (p1 / denom), g_w * (p2 / denom))
    route = jnp.zeros_like(logits)
    for k, value in enumerate(fields):
        route = jnp.where(lane == float(k), value, route)
    route_ref[...] = route
    ids_ref[...] = jnp.transpose(route)[0:ROUTE_ID_ROWS, :].astype(jnp.int32)


ROUTE_E1, ROUTE_E2, ROUTE_RANK1, ROUTE_RANK2, ROUTE_W1, ROUTE_W2 = range(6)
ROUTE_ID_ROWS = 8


def _mix(x2d, a, o_f, o_b, og, w, tm):
    t = x2d.shape[0]
    hv = GLA_HEADS * GLA_DV
    row = lambda width: pl.BlockSpec((tm, width), lambda i: (i, 0))
    return pl.pallas_call(
        _mix_body,
        grid=(t // tm,),
        in_specs=[row(D_MODEL), row(MLA_HEADS * MLA_V), row(hv), row(hv), row(hv),
                  _full_spec((1, GLA_DV)), _full_spec(w['wout'].shape), _full_spec((1, D_MODEL)),
                  _full_spec(w['wr_hi'].shape), _full_spec(w['wr_lo'].shape), _full_spec(w['br'].shape)],
        out_specs=(row(D_MODEL), pl.BlockSpec((tm, PACKED_CHUNKS, LANES), lambda i: (i, 0, 0)), row(LANES),
                   pl.BlockSpec((ROUTE_ID_ROWS, tm), lambda i: (0, i)), _full_spec((1, LANES))),
        out_shape=(jax.ShapeDtypeStruct((t, D_MODEL), F32), jax.ShapeDtypeStruct((t, PACKED_CHUNKS, LANES), jnp.uint32),
                   jax.ShapeDtypeStruct((t, LANES), F32), jax.ShapeDtypeStruct((ROUTE_ID_ROWS, t), jnp.int32),
                   jax.ShapeDtypeStruct((1, LANES), F32)),
        scratch_shapes=[pltpu.VMEM((tm, tm), BF16)],
        compiler_params=_cparams("arbitrary"),
        name="mix_router",
    )(x2d, a, o_f, o_b, og, w['gla_norm'], w['wout'], w['norm_ffn'], w['wr_hi'], w['wr_lo'], w['br'])


EXPERT_TILE = 256
ROW_TILE = 256


SC_CORES = 2
SC_SUBCORES = 16
SC_GATHER_ROWS = 32


def _sc_gather(table, idx):
    n = idx.shape[0]
    workers = SC_CORES * SC_SUBCORES
    per_worker = n // workers
    assert n % (workers * SC_GATHER_ROWS) == 0
    mesh = plsc.VectorSubcoreMesh(core_axis_name="c", subcore_axis_name="s")

    @functools.partial(
        pl.kernel, mesh=mesh,
        out_type=jax.ShapeDtypeStruct((n,) + table.shape[1:], table.dtype),
        scratch_types=[pltpu.VMEM((SC_GATHER_ROWS,), jnp.int32),
                       pltpu.VMEM((SC_GATHER_ROWS,) + table.shape[1:], table.dtype),
                       pltpu.SemaphoreType.DMA])
    def gather(table_ref, idx_ref, out_ref, idx_buf, rows_buf, sem):
        base = (lax.axis_index("s") * SC_CORES + lax.axis_index("c")) * per_worker

        @pl.loop(0, per_worker // SC_GATHER_ROWS)
        def _(j):
            rows = pl.ds(base + j * SC_GATHER_ROWS, SC_GATHER_ROWS)
            pltpu.sync_copy(idx_ref.at[rows], idx_buf)
            pltpu.async_copy(table_ref.at[idx_buf], rows_buf, sem).wait()
            pltpu.sync_copy(rows_buf, out_ref.at[rows])

    return gather(table, idx)


def _sc_scatter(rows, idx, n_out):
    copies, n = idx.shape
    idx = idx.reshape(copies * n)
    workers = SC_CORES * SC_SUBCORES
    per_worker = n // workers
    assert n % (workers * SC_GATHER_ROWS) == 0
    mesh = plsc.VectorSubcoreMesh(core_axis_name="c", subcore_axis_name="s")

    @functools.partial(
        pl.kernel, mesh=mesh,
        out_type=jax.ShapeDtypeStruct((n_out,) + rows.shape[1:], rows.dtype),
        scratch_types=[pltpu.VMEM((SC_GATHER_ROWS,), jnp.int32),
                       pltpu.VMEM((SC_GATHER_ROWS,) + rows.shape[1:], rows.dtype),
                       pltpu.SemaphoreType.DMA])
    def scatter(rows_ref, idx_ref, out_ref, idx_buf, rows_buf, sem):
        base = (lax.axis_index("s") * SC_CORES + lax.axis_index("c")) * per_worker

        @pl.loop(0, per_worker // SC_GATHER_ROWS)
        def _(j):
            first = base + j * SC_GATHER_ROWS
            pltpu.sync_copy(rows_ref.at[pl.ds(first, SC_GATHER_ROWS)], rows_buf)
            for k in range(copies):
                pltpu.sync_copy(idx_ref.at[pl.ds(k * n + first, SC_GATHER_ROWS)], idx_buf)
                pltpu.async_copy(rows_buf, out_ref.at[idx_buf], sem).wait()

    return scatter(rows, idx)


SLOT_TILE = 2048


def _slots_body(starts_ref, ids_ref, slots_ref):
    ids = ids_ref[...]
    experts = ids[ROUTE_E1:ROUTE_E1 + 2, :]
    start = jnp.zeros_like(experts)
    for e in range(N_EXPERTS):
        start = jnp.where(experts == e, starts_ref[e], start)
    slots_ref[...] = start + ids[ROUTE_RANK1:ROUTE_RANK1 + 2, :]


def _slots(starts, ids):
    t = ids.shape[1]
    return pl.pallas_call(
        _slots_body,
        grid_spec=pltpu.PrefetchScalarGridSpec(
            num_scalar_prefetch=1,
            grid=(t // SLOT_TILE,),
            in_specs=[pl.BlockSpec((ROUTE_ID_ROWS, SLOT_TILE), lambda i, *_: (0, i))],
            out_specs=pl.BlockSpec((2, SLOT_TILE), lambda i, *_: (0, i))),
        out_shape=jax.ShapeDtypeStruct((2, t), jnp.int32),
        compiler_params=_cparams("parallel"),
        name="moe_slots",
    )(starts, ids)


def _queue_body(slots_ref, token_ref, *, n_tokens):
    i = pl.program_id(0)

    @pl.when(i == 0)
    def _():
        for base in range(0, token_ref.shape[0], n_tokens):
            def clear(j, carry):
                token_ref[base + j] = j
                return carry
            lax.fori_loop(0, min(n_tokens, token_ref.shape[0] - base), clear, 0, unroll=8)

    def place(r, carry):
        for k in range(2):
            token_ref[slots_ref[k, r]] = i * SLOT_TILE + r
        return carry

    lax.fori_loop(0, SLOT_TILE, place, 0, unroll=8)


def _queue_tokens(slots, n_rows):
    t = slots.shape[1]
    return pl.pallas_call(
        functools.partial(_queue_body, n_tokens=t),
        grid=(t // SLOT_TILE,),
        in_specs=[pl.BlockSpec((2, SLOT_TILE), lambda i: (0, i), memory_space=pltpu.SMEM)],
        out_specs=pl.BlockSpec(memory_space=pltpu.SMEM),
        out_shape=jax.ShapeDtypeStruct((n_rows,), jnp.int32),
        compiler_params=_cparams("arbitrary"),
        name="moe_queue",
    )(slots)


TILES_PER_STEP = 2


def _expert_body(tile_expert_ref, n_tiles_ref, tile_valid_ref, xs_ref, *refs):
    wg_refs = refs[0:TILES_PER_STEP]
    wu_refs = refs[TILES_PER_STEP:2 * TILES_PER_STEP]
    wd_refs = refs[2 * TILES_PER_STEP:3 * TILES_PER_STEP]
    ys_ref = refs[3 * TILES_PER_STEP]
    first = pl.program_id(0) * TILES_PER_STEP
    used = first < n_tiles_ref[0]

    @pl.when(used)
    def _():
        rows = [slice(j * EXPERT_TILE, (j + 1) * EXPERT_TILE) for j in range(TILES_PER_STEP)]
        xs = [_unpack_bf16_pairs(_tiles_to_rows(xs_ref[r])) for r in rows]
        xs = [jnp.where(lax.broadcasted_iota(jnp.int32, x.shape, 0) < tile_valid_ref[first + j], x, 0.0)
              for j, x in enumerate(xs)]
        xs = [x.astype(BF16) for x in xs]
        gates = [_dot(x, wg[...]) for x, wg in zip(xs, wg_refs)]
        ups = [_dot(x, wu[...]) for x, wu in zip(xs, wu_refs)]
        acts = [(g / (1.0 + jnp.exp(-g))) * u for g, u in zip(gates, ups)]
        ys = [_dot(act.astype(BF16), wd[...]) for act, wd in zip(acts, wd_refs)]
        for r, y in zip(rows, ys):
            ys_ref[r] = _rows_to_tiles(_pack_bf16_pairs(y))

    @pl.when(jnp.logical_not(used))
    def _():
        ys_ref[...] = jnp.zeros_like(ys_ref)


def _experts(tile_expert, n_tiles, tile_valid, xs, w):
    n_steps = xs.shape[0] // (EXPERT_TILE * TILES_PER_STEP)

    def step(i, nt):
        return jnp.minimum(i, (nt[0] - 1) // TILES_PER_STEP)

    def expert(i, j, te, nt):
        return te[jnp.minimum(step(i, nt) * TILES_PER_STEP + j, nt[0] - 1)]

    tiles = lambda index: pl.BlockSpec((EXPERT_TILE * TILES_PER_STEP, PACKED_CHUNKS, LANES), index)
    up_spec = lambda j: pl.BlockSpec((None, D_MODEL, D_EXPERT), lambda i, te, nt, tv: (expert(i, j, te, nt), 0, 0))
    down_spec = lambda j: pl.BlockSpec((None, D_EXPERT, D_MODEL), lambda i, te, nt, tv: (expert(i, j, te, nt), 0, 0))
    slots = range(TILES_PER_STEP)
    return pl.pallas_call(
        _expert_body,
        grid_spec=pltpu.PrefetchScalarGridSpec(
            num_scalar_prefetch=3,
            grid=(n_steps,),
            in_specs=([tiles(lambda i, te, nt, tv: (step(i, nt), 0, 0))]
                      + [up_spec(j) for j in slots] + [up_spec(j) for j in slots] + [down_spec(j) for j in slots]),
            out_specs=tiles(lambda i, te, nt, tv: (i, 0, 0))),
        out_shape=jax.ShapeDtypeStruct(xs.shape, xs.dtype),
        compiler_params=_cparams("arbitrary"),
        name="moe_experts",
    )(tile_expert, n_tiles, tile_valid, xs, *([w['w_gate']] * TILES_PER_STEP), *([w['w_up']] * TILES_PER_STEP),
      *([w['w_down']] * TILES_PER_STEP))


def _combine_body(h1_ref, route_ref, nfin_ref, y1_ref, y2_ref, out_ref):
    route = route_ref[...]
    lane = lax.broadcasted_iota(jnp.int32, route.shape, 1)
    w1 = jnp.sum(jnp.where(lane == ROUTE_W1, route, 0.0), axis=-1, keepdims=True)
    w2 = jnp.sum(jnp.where(lane == ROUTE_W2, route, 0.0), axis=-1, keepdims=True)
    y1 = _unpack_bf16_pairs(_tiles_to_rows(y1_ref[...]))
    y2 = _unpack_bf16_pairs(_tiles_to_rows(y2_ref[...]))
    y = w1 * y1 + w2 * y2
    out_ref[...] = _rms(h1_ref[...] + y, nfin_ref[...])


def _combine(h1, route, y12, w):
    t = h1.shape[0]
    row = lambda width: pl.BlockSpec((ROW_TILE, width), lambda i: (i, 0))
    tiles = lambda k: pl.BlockSpec((None, ROW_TILE, PACKED_CHUNKS, LANES), lambda i: (k, i, 0, 0))
    return pl.pallas_call(
        _combine_body,
        grid=(t // ROW_TILE,),
        in_specs=[row(D_MODEL), row(LANES), _full_spec((1, D_MODEL)), tiles(0), tiles(1)],
        out_specs=row(D_MODEL),
        out_shape=jax.ShapeDtypeStruct((t, D_MODEL), F32),
        compiler_params=_cparams("parallel"),
        name="moe_combine",
    )(h1, route, w['norm_final'], y12, y12)


def _moe(hn, route, ids, counts, h1, w):
    t = hn.shape[0]
    n_rows = 2 * t + N_EXPERTS * EXPERT_TILE
    n_grid_tiles = n_rows // EXPERT_TILE
    count = counts[0, ROUTER_EXPERT_LANE0:ROUTER_EXPERT_LANE0 + N_EXPERTS].astype(jnp.int32)
    padded = jnp.maximum((count + EXPERT_TILE - 1) // EXPERT_TILE, 1) * EXPERT_TILE
    ends = jnp.sum(jnp.where(jnp.arange(N_EXPERTS)[:, None] <= jnp.arange(N_EXPERTS)[None, :], padded[:, None], 0),
                   axis=0)
    starts = ends - padded
    n_tiles = (ends[-1:] // EXPERT_TILE)
    tile_rows = jnp.arange(n_grid_tiles, dtype=jnp.int32) * EXPERT_TILE
    tile_expert = jnp.minimum(jnp.sum((ends[None, :] <= tile_rows[:, None]).astype(jnp.int32), axis=1), N_EXPERTS - 1)
    tile_valid = jnp.clip(jnp.sum(jnp.where(jnp.arange(N_EXPERTS)[None, :] == tile_expert[:, None],
                                            (starts + count)[None, :], 0), axis=1) - tile_rows, 0, EXPERT_TILE)
    slots = _slots(starts, ids)
    ys = _experts(tile_expert, n_tiles, tile_valid, _sc_scatter(hn, slots, n_rows), w)
    y12 = _sc_gather(ys, slots.reshape(2 * t)).reshape(2, t, PACKED_CHUNKS, LANES)
    return _combine(h1, route, y12, w)


def _rope_tables(positions):
    inv_freq = 1.0 / (ROPE_THETA ** (jnp.arange(0, MLA_ROPE, 2, dtype=F32) / MLA_ROPE))
    ang = positions.astype(F32)[:, None] * inv_freq[None, :]
    cos, sin = jnp.cos(ang), jnp.sin(ang)
    reps = LANES // MLA_ROPE
    return jnp.tile(jnp.concatenate([cos, cos], axis=-1), (1, reps)), jnp.tile(jnp.concatenate([-sin, sin], axis=-1), (1, reps))


def _pack_weights(norm_mix, w_in, q_a_norm, w_uq, kv_a_norm, w_ukv, w_gate_fwd, b_gate_fwd, w_gate_bwd, b_gate_bwd,
                  gla_norm, w_out, norm_ffn, w_router_group, b_router_group, w_router_expert, b_router_expert,
                  w_expert_gate, w_expert_up, w_expert_down, norm_final):
    l = 0
    hk = GLA_HEADS * GLA_DK
    hv = GLA_HEADS * GLA_DV
    c_q, c_kv, k_pe, gq, gk, gv, lr_f, lr_b, og = jnp.split(
        w_in[l], np.cumsum([MLA_Q_RANK, MLA_KV_RANK, MLA_ROPE, hk, hk, hv, GLA_GATE_RANK, GLA_GATE_RANK])[:].tolist(),
        axis=-1)
    lr_pad = jnp.zeros((D_MODEL, LANES - 2 * GLA_GATE_RANK), F32)
    win = jnp.concatenate([c_q, c_kv, k_pe, k_pe, gq, gk, gv, og, lr_f, lr_b, lr_pad], axis=-1).astype(BF16)
    wuq = w_uq[l].reshape(MLA_Q_RANK, MLA_HEADS, MLA_NOPE + MLA_ROPE)
    wuq = jnp.concatenate([wuq[:, :, :MLA_NOPE].reshape(MLA_Q_RANK, -1), wuq[:, :, MLA_NOPE:].reshape(MLA_Q_RANK, -1)],
                          axis=-1).astype(BF16)
    wgate = jnp.zeros((LANES, 2 * hk), F32)
    wgate = wgate.at[0:GLA_GATE_RANK, 0:hk].set(w_gate_fwd[l])
    wgate = wgate.at[GLA_GATE_RANK:2 * GLA_GATE_RANK, hk:].set(w_gate_bwd[l])
    wr = jnp.zeros((D_MODEL, LANES), F32)
    wr = wr.at[:, ROUTER_GROUP_LANE0:ROUTER_GROUP_LANE0 + N_GROUPS].set(w_router_group[l])
    wr = wr.at[:, ROUTER_EXPERT_LANE0:ROUTER_EXPERT_LANE0 + N_EXPERTS].set(w_router_expert[l])
    wr_hi = wr.astype(BF16)
    br = jnp.zeros((1, LANES), F32)
    br = br.at[0, ROUTER_GROUP_LANE0:ROUTER_GROUP_LANE0 + N_GROUPS].set(b_router_group[l])
    br = br.at[0, ROUTER_EXPERT_LANE0:ROUTER_EXPERT_LANE0 + N_EXPERTS].set(b_router_expert[l])
    return {
        'norm_mix': norm_mix[l][None], 'win': win, 'q_a_norm': q_a_norm[l][None], 'wuq': wuq,
        'kv_a_norm': kv_a_norm[l][None], 'wukv': w_ukv[l].astype(BF16),
        'wgate': wgate.astype(BF16), 'bgate': jnp.concatenate([b_gate_fwd[l], b_gate_bwd[l]])[None],
        'gla_norm': gla_norm[l][None], 'wout': w_out[l].astype(BF16), 'norm_ffn': norm_ffn[l][None],
        'wr_hi': wr_hi, 'wr_lo': (wr - wr_hi.astype(F32)).astype(BF16), 'br': br,
        'w_gate': w_expert_gate[l].reshape(N_EXPERTS, D_MODEL, D_EXPERT).astype(BF16),
        'w_up': w_expert_up[l].reshape(N_EXPERTS, D_MODEL, D_EXPERT).astype(BF16),
        'w_down': w_expert_down[l].reshape(N_EXPERTS, D_EXPERT, D_MODEL).astype(BF16),
        'norm_final': norm_final[None],
    }


def _meta_streams(meta_tokens, w):
    cos, sin = _rope_tables(jnp.arange(N_META))
    _, k, v, _, gk, gv, gf, _, _ = _inproj(meta_tokens, cos, sin, w, N_META)
    pad_keys = ((0, 0), (0, LANES - N_META), (0, 0))
    front = ((GLA_CHUNK - N_META, 0), (0, 0))
    return (jnp.pad(k, pad_keys), jnp.pad(v, pad_keys), jnp.pad(gk, front), jnp.pad(gv, front), jnp.pad(gf, front))


def _token_mixers(x, meta, w, tm, tq, tk, tb):
    bsz, seq, _ = x.shape
    km, vm, mk, mv, mg = meta
    x2d = x.reshape(bsz * seq, D_MODEL)
    cos, sin = _rope_tables(N_META + jnp.arange(seq))
    q, k, v, gq, gk, gv, gf, gb, og = _inproj(x2d, cos, sin, w, tm)
    a = _attention(q, k, v, km, vm, bsz, seq, tq, tk)
    o_f, o_b = _gla(gq, gk, gv, gf, gb, mk, mv, mg, bsz, seq, tb)
    return _mix(x2d, a, o_f, o_b, og, w, tm)


def kernel(x_prompt, x_sample, meta_tokens, norm_mix, w_in, q_a_norm, w_uq, kv_a_norm, w_ukv, w_gate_fwd, b_gate_fwd, w_gate_bwd, b_gate_bwd, gla_norm, w_out, norm_ffn, w_router_group, b_router_group, w_router_expert, b_router_expert, w_expert_gate, w_expert_up, w_expert_down, norm_final):
    w = _pack_weights(norm_mix, w_in, q_a_norm, w_uq, kv_a_norm, w_ukv, w_gate_fwd, b_gate_fwd, w_gate_bwd,
                      b_gate_bwd, gla_norm, w_out, norm_ffn, w_router_group, b_router_group, w_router_expert,
                      b_router_expert, w_expert_gate, w_expert_up, w_expert_down, norm_final)
    meta = _meta_streams(meta_tokens, w)
    outs = []
    for x in (x_prompt, x_sample):
        h1, hn, route, ids, counts = _token_mixers(x, meta, w, tm=512, tq=1024, tk=512, tb=512)
        outs.append(_moe(hn, route, ids, counts, h1, w).reshape(x.shape))
    return tuple(outs)
```

```python
import functools

import numpy as np
import jax
import jax.numpy as jnp
from jax import lax
from jax.experimental import pallas as pl
from jax.experimental.pallas import tpu as pltpu
from jax.experimental.pallas import tpu_sc as plsc

F32 = jnp.float32
BF16 = jnp.bfloat16

D_MODEL = 1024
N_META = 16
MLA_HEADS = 4
MLA_Q_RANK = 384
MLA_KV_RANK = 256
MLA_NOPE = 128
MLA_ROPE = 64
MLA_V = 128
ROPE_THETA = 10000.0
GLA_HEADS = 4
GLA_DK = 64
GLA_DV = 128
GLA_GATE_RANK = 16
GLA_TAU = 16.0
GLA_CHUNK = 64
N_GROUPS = 4
EXPERTS_PER_GROUP = 8
N_EXPERTS = N_GROUPS * EXPERTS_PER_GROUP
D_EXPERT = 256
EPS = 1e-6

LANES = 128
V7X_VMEM_BYTES = 64 * 1024 * 1024
VMEM_LIMIT = V7X_VMEM_BYTES * 7 // 8

ATTN_SCALE = (MLA_NOPE + MLA_ROPE) ** -0.5 * float(np.log2(np.e))
QK_WIDTH = 2 * LANES
V_WIDTH = 2 * LANES
ATTN_GROUP = 8

C_CQ = 0
C_CKV = C_CQ + MLA_Q_RANK
C_KPE = C_CKV + MLA_KV_RANK
C_GQ = C_KPE + LANES
C_GK = C_GQ + GLA_HEADS * GLA_DK
C_GV = C_GK + GLA_HEADS * GLA_DK
C_OG = C_GV + GLA_HEADS * GLA_DV
C_LR = C_OG + GLA_HEADS * GLA_DV
D_IN_PACKED = C_LR + LANES

ROUTER_GROUP_LANE0 = 0
ROUTER_EXPERT_LANE0 = N_GROUPS


def _cparams(*semantics):
    return pltpu.CompilerParams(dimension_semantics=semantics, vmem_limit_bytes=VMEM_LIMIT)


def _rms(x, g):
    return x * lax.rsqrt(jnp.mean(x * x, axis=-1, keepdims=True) + EPS) * g


def _dot(a, b):
    return jnp.dot(a, b, preferred_element_type=F32)


def _dot_nt(a, b):
    return lax.dot_general(a, b, (((1,), (1,)), ((), ())), preferred_element_type=F32)


def _dot_tn(a, b):
    return lax.dot_general(a, b, (((0,), (0,)), ((), ())), preferred_element_type=F32)


def _full_spec(shape):
    return pl.BlockSpec(shape, lambda *_: (0,) * len(shape))


ROW_CHUNKS = D_MODEL // LANES


def _rows_to_tiles(x):
    chunks = jnp.stack([x[:, s * LANES:(s + 1) * LANES] for s in range(x.shape[1] // LANES)], axis=0)
    return pltpu.einshape("smd->msd", chunks)


def _tiles_to_rows(x):
    chunks = pltpu.einshape("msd->smd", x)
    return jnp.concatenate([chunks[s] for s in range(x.shape[1])], axis=-1)


PACKED_CHUNKS = ROW_CHUNKS // 2


def _pack_bf16_pairs(x):
    half = x.shape[1] // 2
    bits = lambda v: lax.bitcast_convert_type(v.astype(BF16).astype(F32), jnp.uint32)
    return (bits(x[:, half:]) & jnp.uint32(0xFFFF0000)) | (bits(x[:, :half]) >> 16)


def _unpack_bf16_pairs(w):
    lo = lax.bitcast_convert_type(w << 16, F32)
    hi = lax.bitcast_convert_type(w & jnp.uint32(0xFFFF0000), F32)
    return jnp.concatenate([lo, hi], axis=-1)


def _rope_pairs(x, cos, sin_signed, first_half):
    swapped = jnp.where(first_half, pltpu.roll(x, LANES - MLA_ROPE // 2, 1), pltpu.roll(x, MLA_ROPE // 2, 1))
    return x * cos + swapped * sin_signed


def _inproj_body(x_ref, cos_ref, sin_ref, nmix_ref, win_ref, qan_ref, wuq_ref, kvan_ref, wukv_ref,
                 wgate_ref, bgate_ref,
                 q_ref, k_ref, v_ref, gq_ref, gk_ref, gv_ref, gf_ref, gb_ref, og_ref):
    hn = _rms(x_ref[...], nmix_ref[...]).astype(BF16)

    def proj(lo, hi):
        return _dot(hn, win_ref[:, lo:hi])

    cos = cos_ref[...]
    sin = sin_ref[...]
    lane = lax.broadcasted_iota(jnp.int32, cos.shape, 1)
    first_half = (lane & (MLA_ROPE - 1)) < MLA_ROPE // 2
    low_lanes = lane < MLA_ROPE

    cq = _rms(proj(C_CQ, C_CKV), qan_ref[...]).astype(BF16)
    qn = _dot(cq, wuq_ref[:, 0:MLA_HEADS * MLA_NOPE]) * ATTN_SCALE
    qr = _dot(cq, wuq_ref[:, MLA_HEADS * MLA_NOPE:])
    for j in range(MLA_HEADS // 2):
        rj = (_rope_pairs(qr[:, j * LANES:(j + 1) * LANES], cos, sin, first_half) * ATTN_SCALE).astype(BF16)
        for h in (2 * j, 2 * j + 1):
            q_ref[h, :, 0:LANES] = qn[:, h * LANES:(h + 1) * LANES].astype(BF16)
            q_ref[h, :, LANES:QK_WIDTH] = rj

    ckv = _rms(proj(C_CKV, C_KPE), kvan_ref[...]).astype(BF16)
    kv = _dot(ckv, wukv_ref[...])
    kr = _rope_pairs(proj(C_KPE, C_GQ), cos, sin, first_half)
    kr_even = jnp.where(low_lanes, kr, 0.0).astype(BF16)
    kr_odd = jnp.where(low_lanes, 0.0, kr).astype(BF16)
    for h in range(MLA_HEADS):
        base = h * (MLA_NOPE + MLA_V)
        k_ref[h, :, 0:LANES] = kv[:, base:base + MLA_NOPE].astype(BF16)
        k_ref[h, :, LANES:QK_WIDTH] = kr_even if h % 2 == 0 else kr_odd
        v_ref[h, :, 0:MLA_V] = kv[:, base + MLA_NOPE:base + MLA_NOPE + MLA_V].astype(BF16)
        v_ref[h, :, MLA_V:V_WIDTH] = jnp.ones((kv.shape[0], V_WIDTH - MLA_V), BF16)

    gq_ref[...] = (proj(C_GQ, C_GK) * (GLA_DK ** -0.5)).astype(BF16)
    gk_ref[...] = proj(C_GK, C_GV).astype(BF16)
    gv_ref[...] = proj(C_GV, C_OG).astype(BF16)
    og_ref[...] = proj(C_OG, C_LR).astype(BF16)
    pre = _dot(proj(C_LR, D_IN_PACKED).astype(BF16), wgate_ref[...]) + bgate_ref[...]
    logsig = jnp.minimum(pre, 0.0) - jnp.log1p(jnp.exp(-jnp.abs(pre)))
    gates = logsig * (1.0 / GLA_TAU)
    gf_ref[...] = gates[:, 0:GLA_HEADS * GLA_DK]
    gb_ref[...] = gates[:, GLA_HEADS * GLA_DK:]


def _inproj(x2d, cos, sin, w, tm):
    t = x2d.shape[0]
    blocks_per_seq = cos.shape[0] // tm
    hk = GLA_HEADS * GLA_DK
    hv = GLA_HEADS * GLA_DV
    row = lambda width: pl.BlockSpec((tm, width), lambda i: (i, 0))
    head_rows = lambda width: pl.BlockSpec((MLA_HEADS, tm, width), lambda i: (0, i, 0))
    tab = pl.BlockSpec((tm, LANES), lambda i: (i % blocks_per_seq, 0))
    out_shape = (
        jax.ShapeDtypeStruct((MLA_HEADS, t, QK_WIDTH), BF16),
        jax.ShapeDtypeStruct((MLA_HEADS, t, QK_WIDTH), BF16),
        jax.ShapeDtypeStruct((MLA_HEADS, t, V_WIDTH), BF16),
        jax.ShapeDtypeStruct((t, hk), BF16),
        jax.ShapeDtypeStruct((t, hk), BF16),
        jax.ShapeDtypeStruct((t, hv), BF16),
        jax.ShapeDtypeStruct((t, hk), F32),
        jax.ShapeDtypeStruct((t, hk), F32),
        jax.ShapeDtypeStruct((t, hv), BF16),
    )
    return pl.pallas_call(
        _inproj_body,
        grid=(t // tm,),
        in_specs=[row(D_MODEL), tab, tab,
                  _full_spec((1, D_MODEL)), _full_spec((D_MODEL, D_IN_PACKED)),
                  _full_spec((1, MLA_Q_RANK)), _full_spec(w['wuq'].shape),
                  _full_spec((1, MLA_KV_RANK)), _full_spec(w['wukv'].shape),
                  _full_spec(w['wgate'].shape), _full_spec(w['bgate'].shape)],
        out_specs=(head_rows(QK_WIDTH), head_rows(QK_WIDTH), head_rows(V_WIDTH),
                   row(hk), row(hk), row(hv), row(hk), row(hk), row(hv)),
        out_shape=out_shape,
        compiler_params=_cparams("parallel"),
        name="inproj",
    )(x2d, cos, sin, w['norm_mix'], w['win'], w['q_a_norm'], w['wuq'], w['kv_a_norm'], w['wukv'],
      w['wgate'], w['bgate'])


def _attn_body(q_ref, k_ref, v_ref, km_ref, vm_ref, o_ref, s_ref, acc_ref, *, tk):
    q = q_ref[...]
    n_groups = k_ref.shape[0] // (ATTN_GROUP * tk)

    def scores(j):
        return _dot_nt(q, k_ref[pl.ds(pl.multiple_of(j * tk, tk), tk), :])

    def values(j):
        return v_ref[pl.ds(pl.multiple_of(j * tk, tk), tk), :]

    def absorb(m, s, v):
        m_new = jnp.maximum(m, jnp.max(s, axis=-1, keepdims=True))
        p = jnp.exp2(s - m_new)
        acc_ref[...] = jnp.exp2(m - m_new) * acc_ref[...] + _dot(p.astype(BF16), v)
        return m_new

    s_ref[0] = scores(0)
    sm = _dot_nt(q, km_ref[...])
    sm = jnp.where(lax.broadcasted_iota(jnp.int32, sm.shape, 1) < N_META, sm, -jnp.inf)
    m = jnp.max(sm, axis=-1, keepdims=True)
    acc_ref[...] = _dot(jnp.exp2(sm - m).astype(BF16), vm_ref[...])

    def group(g, m, last):
        for i in range(ATTN_GROUP):
            j = ATTN_GROUP * g + i
            s = s_ref[i % 2]
            if not (last and i == ATTN_GROUP - 1):
                s_ref[(i + 1) % 2] = scores(j + 1)
            m = absorb(m, s, values(j))
        return m

    m = lax.fori_loop(0, n_groups - 1, lambda g, m: group(g, m, False), m)
    group(n_groups - 1, m, True)
    acc = acc_ref[...]
    o_ref[...] = (acc[:, :MLA_V] / acc[:, MLA_V:]).astype(o_ref.dtype)


def _attention(q, k, v, km, vm, bsz, seq, tq, tk):
    nq = seq // tq
    return pl.pallas_call(
        functools.partial(_attn_body, tk=tk),
        grid=(bsz, MLA_HEADS, nq),
        in_specs=[pl.BlockSpec((None, tq, QK_WIDTH), lambda b, h, i: (h, b * nq + i, 0)),
                  pl.BlockSpec((None, seq, QK_WIDTH), lambda b, h, i: (h, b, 0)),
                  pl.BlockSpec((None, seq, V_WIDTH), lambda b, h, i: (h, b, 0)),
                  pl.BlockSpec((None, LANES, QK_WIDTH), lambda b, h, i: (h, 0, 0)),
                  pl.BlockSpec((None, LANES, V_WIDTH), lambda b, h, i: (h, 0, 0))],
        out_specs=pl.BlockSpec((tq, MLA_V), lambda b, h, i: (b * nq + i, h)),
        out_shape=jax.ShapeDtypeStruct((bsz * seq, MLA_HEADS * MLA_V), BF16),
        scratch_shapes=[pltpu.VMEM((2, tq, tk), F32), pltpu.VMEM((tq, V_WIDTH), F32)],
        compiler_params=_cparams("parallel", "parallel", "arbitrary"),
        name="mla_attention",
    )(q, k, v, km, vm)


def _split3(x):
    hi = x.astype(BF16)
    r1 = x - hi.astype(F32)
    mid = r1.astype(BF16)
    lo = (r1 - mid.astype(F32)).astype(BF16)
    return hi, mid, lo


def _gla_log_decay(g, tri):
    g_hi, g_mid, g_lo = _split3(g)
    return _dot(tri, g_hi) + _dot(tri, g_mid) + _dot(tri, g_lo)


def _head_rows(x):
    even = lax.broadcasted_iota(jnp.int32, x.shape, 1) < GLA_DK
    return jnp.concatenate([jnp.where(even, x, 0.0), jnp.where(even, 0.0, x)], axis=0).astype(BF16)


def _gla_operands(q, k, b, mid, last):
    b_last = b[last:last + 1, :]
    ke = (k * jnp.exp(b_last - b)).astype(BF16)
    if q is None:
        return jnp.exp(b_last), ke, None, None, None
    b_mid = b[mid:mid + 1, :]
    ks = (k * jnp.exp(b_mid - b)).astype(BF16)
    return jnp.exp(b_last), ke, ks, _head_rows(q * jnp.exp(b - b_mid)), _head_rows(q * jnp.exp(b))


def _gla_state_update(v_even, v_odd, ke):
    return jnp.where(lax.broadcasted_iota(jnp.int32, (GLA_DV, LANES), 1) < GLA_DK,
                     _dot_tn(v_even, ke), _dot_tn(v_odd, ke))


def _gla_intra(v_even, v_odd, ks, qs2, keep):
    scores = jnp.where(keep, _dot_nt(qs2, ks), 0.0).astype(BF16)
    return _dot(scores[:GLA_CHUNK], v_even), _dot(scores[GLA_CHUNK:], v_odd)


def _gla_body(qf_ref, kf_ref, vf_ref, gf_ref, qb_ref, kb_ref, vb_ref, gb_ref, mk_ref, mv_ref, mg_ref,
              of_ref, ob_ref, state_ref):
    n_chunks = qf_ref.shape[0] // GLA_CHUNK
    n_pairs = GLA_HEADS // 2
    r = lax.broadcasted_iota(jnp.int32, (GLA_CHUNK, GLA_CHUNK), 0)
    c = lax.broadcasted_iota(jnp.int32, (GLA_CHUNK, GLA_CHUNK), 1)
    tri_f = jnp.where(c <= r, 1.0, 0.0).astype(BF16)
    tri_b = jnp.where(c >= r, 1.0, 0.0).astype(BF16)
    r2 = lax.broadcasted_iota(jnp.int32, (2 * GLA_CHUNK, GLA_CHUNK), 0) & (GLA_CHUNK - 1)
    c2 = lax.broadcasted_iota(jnp.int32, (2 * GLA_CHUNK, GLA_CHUNK), 1)
    keep_f = c2 <= r2
    keep_b = c2 >= r2
    mid_f, last_f = GLA_CHUNK // 2 - 1, GLA_CHUNK - 1
    mid_b, last_b = GLA_CHUNK // 2, 0

    def pair_cols(p):
        return slice(p * LANES, (p + 1) * LANES)

    def head_cols(h):
        return slice(h * GLA_DV, (h + 1) * GLA_DV)

    @pl.when(pl.program_id(1) == 0)
    def _():
        for p in range(n_pairs):
            b = _gla_log_decay(mg_ref[:, pair_cols(p)], tri_f)
            _, ke, _, _, _ = _gla_operands(None, mk_ref[:, pair_cols(p)], b, mid_f, last_f)
            state_ref[p] = _gla_state_update(mv_ref[:, head_cols(2 * p)].astype(BF16),
                                             mv_ref[:, head_cols(2 * p + 1)].astype(BF16), ke)
            state_ref[n_pairs + p] = jnp.zeros((GLA_DV, LANES), F32)

    scans = []
    for p in range(n_pairs):
        scans.append((p, list(range(n_chunks)), qf_ref, kf_ref, vf_ref, gf_ref, of_ref, p,
                      tri_f, keep_f, mid_f, last_f))
        scans.append((p, list(reversed(range(n_chunks))), qb_ref, kb_ref, vb_ref, gb_ref, ob_ref, n_pairs + p,
                      tri_b, keep_b, mid_b, last_b))

    def rows(c):
        return slice(c * GLA_CHUNK, (c + 1) * GLA_CHUNK)

    def values(v_ref, p, c):
        return (v_ref[rows(c), head_cols(2 * p)].astype(BF16), v_ref[rows(c), head_cols(2 * p + 1)].astype(BF16))

    log_decay = [[_gla_log_decay(g_ref[rows(c), pair_cols(p)], tri) for c in order]
                 for (p, order, _, _, _, g_ref, _, _, tri, _, _, _) in scans]
    operands = [[_gla_operands(q_ref[rows(c), pair_cols(p)], k_ref[rows(c), pair_cols(p)], b, mid, last)
                 for c, b in zip(order, bs)]
                for (p, order, q_ref, k_ref, _, _, _, _, _, _, mid, last), bs in zip(scans, log_decay)]
    updates = [[_gla_state_update(*values(v_ref, p, c), ops[1]) for c, ops in zip(order, opss)]
               for (p, order, _, _, v_ref, _, _, _, _, _, _, _), opss in zip(scans, operands)]
    intra = [[_gla_intra(*values(v_ref, p, c), ops[2], ops[3], keep) for c, ops in zip(order, opss)]
             for (p, order, _, _, v_ref, _, _, _, _, keep, _, _), opss in zip(scans, operands)]
    states = []
    for (_, order, _, _, _, _, _, slot, _, _, _, _), opss, upds in zip(scans, operands, updates):
        st = state_ref[slot]
        entering = []
        for ops, upd in zip(opss, upds):
            entering.append(st.astype(BF16))
            st = st * ops[0] + upd
        state_ref[slot] = st
        states.append(entering)
    for (p, order, _, _, _, _, o_ref, _, _, _, _, _), opss, sts, locs in zip(scans, operands, states, intra):
        for c, ops, st, (o_even, o_odd) in zip(order, opss, sts, locs):
            inter = _dot_nt(ops[4], st)
            o_ref[rows(c), head_cols(2 * p)] = (o_even + inter[:GLA_CHUNK]).astype(o_ref.dtype)
            o_ref[rows(c), head_cols(2 * p + 1)] = (o_odd + inter[GLA_CHUNK:]).astype(o_ref.dtype)


def _gla(gq, gk, gv, gf, gb, mk, mv, mg, bsz, seq, tb):
    nb = seq // tb
    hk = GLA_HEADS * GLA_DK
    hv = GLA_HEADS * GLA_DV
    fwd = lambda width: pl.BlockSpec((tb, width), lambda b, j: (b * nb + j, 0))
    bwd = lambda width: pl.BlockSpec((tb, width), lambda b, j: (b * nb + nb - 1 - j, 0))
    t = bsz * seq
    return pl.pallas_call(
        _gla_body,
        grid=(bsz, nb),
        in_specs=[fwd(hk), fwd(hk), fwd(hv), fwd(hk), bwd(hk), bwd(hk), bwd(hv), bwd(hk),
                  _full_spec(mk.shape), _full_spec(mv.shape), _full_spec(mg.shape)],
        out_specs=(fwd(hv), bwd(hv)),
        out_shape=(jax.ShapeDtypeStruct((t, hv), BF16), jax.ShapeDtypeStruct((t, hv), BF16)),
        scratch_shapes=[pltpu.VMEM((2 * (GLA_HEADS // 2), GLA_DV, LANES), F32)],
        compiler_params=_cparams("parallel", "arbitrary"),
        name="gla_scan",
    )(gq, gk, gv, gf, gq, gk, gv, gb, mk, mv, mg)


def _mix_body(x_ref, a_ref, of_ref, ob_ref, og_ref, gnorm_ref, wout_ref, nffn_ref, wr_ref, br_ref,
              h1_ref, hn_ref, route_ref, ids_ref, count_ref, tri_ref):
    tm = x_ref.shape[0]

    @pl.when(pl.program_id(0) == 0)
    def _():
        r = lax.broadcasted_iota(jnp.int32, (tm, tm), 0)
        c = lax.broadcasted_iota(jnp.int32, (tm, tm), 1)
        tri_ref[...] = jnp.where(c < r, 1.0, 0.0).astype(BF16)
        count_ref[...] = jnp.zeros_like(count_ref)

    a_width = MLA_HEADS * MLA_V
    gated = []
    for h in range(GLA_HEADS):
        cols = slice(h * GLA_DV, (h + 1) * GLA_DV)
        o = of_ref[:, cols].astype(F32) + ob_ref[:, cols].astype(F32)
        og = og_ref[:, cols].astype(F32)
        silu = og / (1.0 + jnp.exp(-og))
        gated.append((_rms(o, gnorm_ref[...]) * silu).astype(BF16))
    h1 = (x_ref[...] + _dot(a_ref[...], wout_ref[0:a_width, :])
          + _dot(jnp.concatenate(gated, axis=-1), wout_ref[a_width:, :]))
    h1_ref[...] = h1
    hn = _rms(h1, nffn_ref[...])
    hn_ref[...] = _rows_to_tiles(_pack_bf16_pairs(hn))

    hn_hi = hn.astype(BF16)
    hn_lo = (hn - hn_hi.astype(F32)).astype(BF16)
    hi_terms = _dot(hn_hi, wr_ref[...])
    logits = hi_terms[:, :LANES] + hi_terms[:, LANES:] + _dot(hn_lo, wr_ref[:, :LANES]) + br_ref[...]
    lane = lax.broadcasted_iota(jnp.int32, logits.shape, 1).astype(F32)
    none = float(LANES)
    neg = -jnp.inf

    def lane_max(x):
        return jnp.max(x, axis=-1, keepdims=True)

    def lane_sum(x):
        return jnp.sum(x, axis=-1, keepdims=True)

    def first_lane(mask):
        return jnp.min(jnp.where(mask, lane, none), axis=-1, keepdims=True)

    is_group = lane < float(N_GROUPS)
    g_max = lane_max(jnp.where(is_group, logits, neg))
    g_exp = jnp.where(is_group, jnp.exp(logits - g_max), 0.0)
    g_prob = g_exp / lane_sum(g_exp)
    g_w = lane_max(g_prob)
    g_idx = first_lane(is_group & (g_prob == g_w))
    e_lo = float(ROUTER_EXPERT_LANE0) + float(EXPERTS_PER_GROUP) * g_idx
    sel = (lane >= e_lo) & (lane < e_lo + float(EXPERTS_PER_GROUP))
    e_max = lane_max(jnp.where(sel, logits, neg))
    e_exp = jnp.where(sel, jnp.exp(logits - e_max), 0.0)
    e_prob = e_exp / lane_sum(e_exp)
    p1 = lane_max(jnp.where(sel, e_prob, neg))
    i1 = first_lane(sel & (e_prob == p1))
    rest = sel & (lane != i1)
    p2 = lane_max(jnp.where(rest, e_prob, neg))
    i2 = first_lane(rest & (e_prob == p2))
    denom = p1 + p2
    chosen = jnp.where((lane == i1) | (lane == i2), 1.0, 0.0)
    rank = count_ref[...] + _dot(tri_ref[...], chosen.astype(BF16))
    count_ref[...] += jnp.sum(chosen, axis=0, keepdims=True)
    fields = (i1 - float(ROUTER_EXPERT_LANE0), i2 - float(ROUTER_EXPERT_LANE0),
              lane_sum(jnp.where(lane == i1, rank, 0.0)), lane_sum(jnp.where(lane == i2, rank, 0.0)),
              g_w * (p1 / denom), g_w * (p2 / denom))
    route = jnp.zeros_like(logits)
    for k, value in enumerate(fields):
        route = jnp.where(lane == float(k), value, route)
    route_ref[...] = route
    ids_ref[...] = jnp.transpose(route)[0:ROUTE_ID_ROWS, :].astype(jnp.int32)


ROUTE_E1, ROUTE_E2, ROUTE_RANK1, ROUTE_RANK2, ROUTE_W1, ROUTE_W2 = range(6)
ROUTE_ID_ROWS = 8


def _mix(x2d, a, o_f, o_b, og, w, tm):
    t = x2d.shape[0]
    hv = GLA_HEADS * GLA_DV
    row = lambda width: pl.BlockSpec((tm, width), lambda i: (i, 0))
    return pl.pallas_call(
        _mix_body,
        grid=(t // tm,),
        in_specs=[row(D_MODEL), row(MLA_HEADS * MLA_V), row(hv), row(hv), row(hv),
                  _full_spec((1, GLA_DV)), _full_spec(w['wout'].shape), _full_spec((1, D_MODEL)),
                  _full_spec(w['wr'].shape), _full_spec(w['br'].shape)],
        out_specs=(row(D_MODEL), pl.BlockSpec((tm, PACKED_CHUNKS, LANES), lambda i: (i, 0, 0)), row(LANES),
                   pl.BlockSpec((ROUTE_ID_ROWS, tm), lambda i: (0, i)), _full_spec((1, LANES))),
        out_shape=(jax.ShapeDtypeStruct((t, D_MODEL), F32), jax.ShapeDtypeStruct((t, PACKED_CHUNKS, LANES), jnp.uint32),
                   jax.ShapeDtypeStruct((t, LANES), F32), jax.ShapeDtypeStruct((ROUTE_ID_ROWS, t), jnp.int32),
                   jax.ShapeDtypeStruct((1, LANES), F32)),
        scratch_shapes=[pltpu.VMEM((tm, tm), BF16)],
        compiler_params=_cparams("arbitrary"),
        name="mix_router",
    )(x2d, a, o_f, o_b, og, w['gla_norm'], w['wout'], w['norm_ffn'], w['wr'], w['br'])


EXPERT_TILE = 256
ROW_TILE = 256


SC_CORES = 2
SC_SUBCORES = 16
SC_GATHER_ROWS = 32


def _sc_gather(table, idx):
    n = idx.shape[0]
    workers = SC_CORES * SC_SUBCORES
    per_worker = n // workers
    assert n % (workers * SC_GATHER_ROWS) == 0
    mesh = plsc.VectorSubcoreMesh(core_axis_name="c", subcore_axis_name="s")

    @functools.partial(
        pl.kernel, mesh=mesh,
        out_type=jax.ShapeDtypeStruct((n,) + table.shape[1:], table.dtype),
        scratch_types=[pltpu.VMEM((SC_GATHER_ROWS,), jnp.int32),
                       pltpu.VMEM((SC_GATHER_ROWS,) + table.shape[1:], table.dtype),
                       pltpu.SemaphoreType.DMA])
    def gather(table_ref, idx_ref, out_ref, idx_buf, rows_buf, sem):
        base = (lax.axis_index("s") * SC_CORES + lax.axis_index("c")) * per_worker

        @pl.loop(0, per_worker // SC_GATHER_ROWS)
        def _(j):
            rows = pl.ds(base + j * SC_GATHER_ROWS, SC_GATHER_ROWS)
            pltpu.sync_copy(idx_ref.at[rows], idx_buf)
            pltpu.async_copy(table_ref.at[idx_buf], rows_buf, sem).wait()
            pltpu.sync_copy(rows_buf, out_ref.at[rows])

    return gather(table, idx)


def _sc_scatter(rows, idx, n_out):
    copies, n = idx.shape
    idx = idx.reshape(copies * n)
    workers = SC_CORES * SC_SUBCORES
    per_worker = n // workers
    assert n % (workers * SC_GATHER_ROWS) == 0
    mesh = plsc.VectorSubcoreMesh(core_axis_name="c", subcore_axis_name="s")

    @functools.partial(
        pl.kernel, mesh=mesh,
        out_type=jax.ShapeDtypeStruct((n_out,) + rows.shape[1:], rows.dtype),
        scratch_types=[pltpu.VMEM((SC_GATHER_ROWS,), jnp.int32),
                       pltpu.VMEM((SC_GATHER_ROWS,) + rows.shape[1:], rows.dtype),
                       pltpu.SemaphoreType.DMA])
    def scatter(rows_ref, idx_ref, out_ref, idx_buf, rows_buf, sem):
        base = (lax.axis_index("s") * SC_CORES + lax.axis_index("c")) * per_worker

        @pl.loop(0, per_worker // SC_GATHER_ROWS)
        def _(j):
            first = base + j * SC_GATHER_ROWS
            pltpu.sync_copy(rows_ref.at[pl.ds(first, SC_GATHER_ROWS)], rows_buf)
            for k in range(copies):
                pltpu.sync_copy(idx_ref.at[pl.ds(k * n + first, SC_GATHER_ROWS)], idx_buf)
                pltpu.async_copy(rows_buf, out_ref.at[idx_buf], sem).wait()

    return scatter(rows, idx)


SLOT_TILE = 2048


def _slots_body(starts_ref, ids_ref, slots_ref):
    ids = ids_ref[...]
    experts = ids[ROUTE_E1:ROUTE_E1 + 2, :]
    start = jnp.zeros_like(experts)
    for e in range(N_EXPERTS):
        start = jnp.where(experts == e, starts_ref[e], start)
    slots_ref[...] = start + ids[ROUTE_RANK1:ROUTE_RANK1 + 2, :]


def _slots(starts, ids):
    t = ids.shape[1]
    return pl.pallas_call(
        _slots_body,
        grid_spec=pltpu.PrefetchScalarGridSpec(
            num_scalar_prefetch=1,
            grid=(t // SLOT_TILE,),
            in_specs=[pl.BlockSpec((ROUTE_ID_ROWS, SLOT_TILE), lambda i, *_: (0, i))],
            out_specs=pl.BlockSpec((2, SLOT_TILE), lambda i, *_: (0, i))),
        out_shape=jax.ShapeDtypeStruct((2, t), jnp.int32),
        compiler_params=_cparams("parallel"),
        name="moe_slots",
    )(starts, ids)


def _queue_body(slots_ref, token_ref, *, n_tokens):
    i = pl.program_id(0)

    @pl.when(i == 0)
    def _():
        for base in range(0, token_ref.shape[0], n_tokens):
            def clear(j, carry):
                token_ref[base + j] = j
                return carry
            lax.fori_loop(0, min(n_tokens, token_ref.shape[0] - base), clear, 0, unroll=8)

    def place(r, carry):
        for k in range(2):
            token_ref[slots_ref[k, r]] = i * SLOT_TILE + r
        return carry

    lax.fori_loop(0, SLOT_TILE, place, 0, unroll=8)


def _queue_tokens(slots, n_rows):
    t = slots.shape[1]
    return pl.pallas_call(
        functools.partial(_queue_body, n_tokens=t),
        grid=(t // SLOT_TILE,),
        in_specs=[pl.BlockSpec((2, SLOT_TILE), lambda i: (0, i), memory_space=pltpu.SMEM)],
        out_specs=pl.BlockSpec(memory_space=pltpu.SMEM),
        out_shape=jax.ShapeDtypeStruct((n_rows,), jnp.int32),
        compiler_params=_cparams("arbitrary"),
        name="moe_queue",
    )(slots)


TILES_PER_STEP = 2


def _expert_body(tile_expert_ref, n_tiles_ref, tile_valid_ref, xs_ref, *refs):
    wg_refs = refs[0:TILES_PER_STEP]
    wu_refs = refs[TILES_PER_STEP:2 * TILES_PER_STEP]
    wd_refs = refs[2 * TILES_PER_STEP:3 * TILES_PER_STEP]
    ys_ref = refs[3 * TILES_PER_STEP]
    first = pl.program_id(0) * TILES_PER_STEP
    used = first < n_tiles_ref[0]

    @pl.when(used)
    def _():
        rows = [slice(j * EXPERT_TILE, (j + 1) * EXPERT_TILE) for j in range(TILES_PER_STEP)]
        xs = [_unpack_bf16_pairs(_tiles_to_rows(xs_ref[r])) for r in rows]
        xs = [jnp.where(lax.broadcasted_iota(jnp.int32, x.shape, 0) < tile_valid_ref[first + j], x, 0.0)
              for j, x in enumerate(xs)]
        xs = [x.astype(BF16) for x in xs]
        gates = [_dot(x, wg[...]) for x, wg in zip(xs, wg_refs)]
        ups = [_dot(x, wu[...]) for x, wu in zip(xs, wu_refs)]
        acts = [(g / (1.0 + jnp.exp(-g))) * u for g, u in zip(gates, ups)]
        ys = [_dot(act.astype(BF16), wd[...]) for act, wd in zip(acts, wd_refs)]
        for r, y in zip(rows, ys):
            ys_ref[r] = _rows_to_tiles(_pack_bf16_pairs(y))

    @pl.when(jnp.logical_not(used))
    def _():
        ys_ref[...] = jnp.zeros_like(ys_ref)


def _experts(tile_expert, n_tiles, tile_valid, xs, w):
    n_steps = xs.shape[0] // (EXPERT_TILE * TILES_PER_STEP)

    def step(i, nt):
        return jnp.minimum(i, (nt[0] - 1) // TILES_PER_STEP)

    def expert(i, j, te, nt):
        return te[jnp.minimum(step(i, nt) * TILES_PER_STEP + j, nt[0] - 1)]

    tiles = lambda index: pl.BlockSpec((EXPERT_TILE * TILES_PER_STEP, PACKED_CHUNKS, LANES), index)
    up_spec = lambda j: pl.BlockSpec((None, D_MODEL, D_EXPERT), lambda i, te, nt, tv: (expert(i, j, te, nt), 0, 0))
    down_spec = lambda j: pl.BlockSpec((None, D_EXPERT, D_MODEL), lambda i, te, nt, tv: (expert(i, j, te, nt), 0, 0))
    slots = range(TILES_PER_STEP)
    return pl.pallas_call(
        _expert_body,
        grid_spec=pltpu.PrefetchScalarGridSpec(
            num_scalar_prefetch=3,
            grid=(n_steps,),
            in_specs=([tiles(lambda i, te, nt, tv: (step(i, nt), 0, 0))]
                      + [up_spec(j) for j in slots] + [up_spec(j) for j in slots] + [down_spec(j) for j in slots]),
            out_specs=tiles(lambda i, te, nt, tv: (i, 0, 0))),
        out_shape=jax.ShapeDtypeStruct(xs.shape, xs.dtype),
        compiler_params=_cparams("arbitrary"),
        name="moe_experts",
    )(tile_expert, n_tiles, tile_valid, xs, *([w['w_gate']] * TILES_PER_STEP), *([w['w_up']] * TILES_PER_STEP),
      *([w['w_down']] * TILES_PER_STEP))


def _combine_body(h1_ref, route_ref, nfin_ref, y1_ref, y2_ref, out_ref):
    route = route_ref[...]
    lane = lax.broadcasted_iota(jnp.int32, route.shape, 1)
    w1 = jnp.sum(jnp.where(lane == ROUTE_W1, route, 0.0), axis=-1, keepdims=True)
    w2 = jnp.sum(jnp.where(lane == ROUTE_W2, route, 0.0), axis=-1, keepdims=True)
    y1 = _unpack_bf16_pairs(_tiles_to_rows(y1_ref[...]))
    y2 = _unpack_bf16_pairs(_tiles_to_rows(y2_ref[...]))
    y = w1 * y1 + w2 * y2
    out_ref[...] = _rms(h1_ref[...] + y, nfin_ref[...])


def _combine(h1, route, y12, w):
    t = h1.shape[0]
    row = lambda width: pl.BlockSpec((ROW_TILE, width), lambda i: (i, 0))
    tiles = lambda k: pl.BlockSpec((None, ROW_TILE, PACKED_CHUNKS, LANES), lambda i: (k, i, 0, 0))
    return pl.pallas_call(
        _combine_body,
        grid=(t // ROW_TILE,),
        in_specs=[row(D_MODEL), row(LANES), _full_spec((1, D_MODEL)), tiles(0), tiles(1)],
        out_specs=row(D_MODEL),
        out_shape=jax.ShapeDtypeStruct((t, D_MODEL), F32),
        compiler_params=_cparams("parallel"),
        name="moe_combine",
    )(h1, route, w['norm_final'], y12, y12)


def _moe(hn, route, ids, counts, h1, w):
    t = hn.shape[0]
    n_rows = 2 * t + N_EXPERTS * EXPERT_TILE
    n_grid_tiles = n_rows // EXPERT_TILE
    count = counts[0, ROUTER_EXPERT_LANE0:ROUTER_EXPERT_LANE0 + N_EXPERTS].astype(jnp.int32)
    padded = jnp.maximum((count + EXPERT_TILE - 1) // EXPERT_TILE, 1) * EXPERT_TILE
    ends = jnp.sum(jnp.where(jnp.arange(N_EXPERTS)[:, None] <= jnp.arange(N_EXPERTS)[None, :], padded[:, None], 0),
                   axis=0)
    starts = ends - padded
    n_tiles = (ends[-1:] // EXPERT_TILE)
    tile_rows = jnp.arange(n_grid_tiles, dtype=jnp.int32) * EXPERT_TILE
    tile_expert = jnp.minimum(jnp.sum((ends[None, :] <= tile_rows[:, None]).astype(jnp.int32), axis=1), N_EXPERTS - 1)
    tile_valid = jnp.clip(jnp.sum(jnp.where(jnp.arange(N_EXPERTS)[None, :] == tile_expert[:, None],
                                            (starts + count)[None, :], 0), axis=1) - tile_rows, 0, EXPERT_TILE)
    slots = _slots(starts, ids)
    ys = _experts(tile_expert, n_tiles, tile_valid, _sc_scatter(hn, slots, n_rows), w)
    y12 = _sc_gather(ys, slots.reshape(2 * t)).reshape(2, t, PACKED_CHUNKS, LANES)
    return _combine(h1, route, y12, w)


def _rope_tables(positions):
    inv_freq = 1.0 / (ROPE_THETA ** (jnp.arange(0, MLA_ROPE, 2, dtype=F32) / MLA_ROPE))
    ang = positions.astype(F32)[:, None] * inv_freq[None, :]
    cos, sin = jnp.cos(ang), jnp.sin(ang)
    reps = LANES // MLA_ROPE
    return jnp.tile(jnp.concatenate([cos, cos], axis=-1), (1, reps)), jnp.tile(jnp.concatenate([-sin, sin], axis=-1), (1, reps))


def _pack_weights(norm_mix, w_in, q_a_norm, w_uq, kv_a_norm, w_ukv, w_gate_fwd, b_gate_fwd, w_gate_bwd, b_gate_bwd,
                  gla_norm, w_out, norm_ffn, w_router_group, b_router_group, w_router_expert, b_router_expert,
                  w_expert_gate, w_expert_up, w_expert_down, norm_final):
    l = 0
    hk = GLA_HEADS * GLA_DK
    hv = GLA_HEADS * GLA_DV
    c_q, c_kv, k_pe, gq, gk, gv, lr_f, lr_b, og = jnp.split(
        w_in[l], np.cumsum([MLA_Q_RANK, MLA_KV_RANK, MLA_ROPE, hk, hk, hv, GLA_GATE_RANK, GLA_GATE_RANK])[:].tolist(),
        axis=-1)
    lr_pad = jnp.zeros((D_MODEL, LANES - 2 * GLA_GATE_RANK), F32)
    win = jnp.concatenate([c_q, c_kv, k_pe, k_pe, gq, gk, gv, og, lr_f, lr_b, lr_pad], axis=-1).astype(BF16)
    wuq = w_uq[l].reshape(MLA_Q_RANK, MLA_HEADS, MLA_NOPE + MLA_ROPE)
    wuq = jnp.concatenate([wuq[:, :, :MLA_NOPE].reshape(MLA_Q_RANK, -1), wuq[:, :, MLA_NOPE:].reshape(MLA_Q_RANK, -1)],
                          axis=-1).astype(BF16)
    wgate = jnp.zeros((LANES, 2 * hk), F32)
    wgate = wgate.at[0:GLA_GATE_RANK, 0:hk].set(w_gate_fwd[l])
    wgate = wgate.at[GLA_GATE_RANK:2 * GLA_GATE_RANK, hk:].set(w_gate_bwd[l])
    wr = jnp.zeros((D_MODEL, LANES), F32)
    wr = wr.at[:, ROUTER_GROUP_LANE0:ROUTER_GROUP_LANE0 + N_GROUPS].set(w_router_group[l])
    wr = wr.at[:, ROUTER_EXPERT_LANE0:ROUTER_EXPERT_LANE0 + N_EXPERTS].set(w_router_expert[l])
    wr_hi = wr.astype(BF16)
    br = jnp.zeros((1, LANES), F32)
    br = br.at[0, ROUTER_GROUP_LANE0:ROUTER_GROUP_LANE0 + N_GROUPS].set(b_router_group[l])
    br = br.at[0, ROUTER_EXPERT_LANE0:ROUTER_EXPERT_LANE0 + N_EXPERTS].set(b_router_expert[l])
    return {
        'norm_mix': norm_mix[l][None], 'win': win, 'q_a_norm': q_a_norm[l][None], 'wuq': wuq,
        'kv_a_norm': kv_a_norm[l][None], 'wukv': w_ukv[l].astype(BF16),
        'wgate': wgate.astype(BF16), 'bgate': jnp.concatenate([b_gate_fwd[l], b_gate_bwd[l]])[None],
        'gla_norm': gla_norm[l][None], 'wout': w_out[l].astype(BF16), 'norm_ffn': norm_ffn[l][None],
        'wr': jnp.concatenate([wr_hi, (wr - wr_hi.astype(F32)).astype(BF16)], axis=-1), 'br': br,
        'w_gate': w_expert_gate[l].reshape(N_EXPERTS, D_MODEL, D_EXPERT).astype(BF16),
        'w_up': w_expert_up[l].reshape(N_EXPERTS, D_MODEL, D_EXPERT).astype(BF16),
        'w_down': w_expert_down[l].reshape(N_EXPERTS, D_EXPERT, D_MODEL).astype(BF16),
        'norm_final': norm_final[None],
    }


def _meta_streams(meta_tokens, w):
    cos, sin = _rope_tables(jnp.arange(N_META))
    _, k, v, _, gk, gv, gf, _, _ = _inproj(meta_tokens, cos, sin, w, N_META)
    pad_keys = ((0, 0), (0, LANES - N_META), (0, 0))
    front = ((GLA_CHUNK - N_META, 0), (0, 0))
    return (jnp.pad(k, pad_keys), jnp.pad(v, pad_keys), jnp.pad(gk, front), jnp.pad(gv, front), jnp.pad(gf, front))


def _token_mixers(x, meta, w, tm, tq, tk, tb):
    bsz, seq, _ = x.shape
    km, vm, mk, mv, mg = meta
    x2d = x.reshape(bsz * seq, D_MODEL)
    cos, sin = _rope_tables(N_META + jnp.arange(seq))
    q, k, v, gq, gk, gv, gf, gb, og = _inproj(x2d, cos, sin, w, tm)
    a = _attention(q, k, v, km, vm, bsz, seq, tq, tk)
    o_f, o_b = _gla(gq, gk, gv, gf, gb, mk, mv, mg, bsz, seq, tb)
    return _mix(x2d, a, o_f, o_b, og, w, tm)


def kernel(x_prompt, x_sample, meta_tokens, norm_mix, w_in, q_a_norm, w_uq, kv_a_norm, w_ukv, w_gate_fwd, b_gate_fwd, w_gate_bwd, b_gate_bwd, gla_norm, w_out, norm_ffn, w_router_group, b_router_group, w_router_expert, b_router_expert, w_expert_gate, w_expert_up, w_expert_down, norm_final):
    w = _pack_weights(norm_mix, w_in, q_a_norm, w_uq, kv_a_norm, w_ukv, w_gate_fwd, b_gate_fwd, w_gate_bwd,
                      b_gate_bwd, gla_norm, w_out, norm_ffn, w_router_group, b_router_group, w_router_expert,
                      b_router_expert, w_expert_gate, w_expert_up, w_expert_down, norm_final)
    meta = _meta_streams(meta_tokens, w)
    outs = []
    for x in (x_prompt, x_sample):
        h1, hn, route, ids, counts = _token_mixers(x, meta, w, tm=512, tq=1024, tk=512, tb=512)
        outs.append(_moe(hn, route, ids, counts, h1, w).reshape(x.shape))
    return tuple(outs)
```

```python
import functools

import numpy as np
import jax
import jax.numpy as jnp
from jax import lax
from jax.experimental import pallas as pl
from jax.experimental.pallas import tpu as pltpu
from jax.experimental.pallas import tpu_sc as plsc

F32 = jnp.float32
BF16 = jnp.bfloat16

D_MODEL = 1024
N_META = 16
MLA_HEADS = 4
MLA_Q_RANK = 384
MLA_KV_RANK = 256
MLA_NOPE = 128
MLA_ROPE = 64
MLA_V = 128
ROPE_THETA = 10000.0
GLA_HEADS = 4
GLA_DK = 64
GLA_DV = 128
GLA_GATE_RANK = 16
GLA_TAU = 16.0
GLA_CHUNK = 64
N_GROUPS = 4
EXPERTS_PER_GROUP = 8
N_EXPERTS = N_GROUPS * EXPERTS_PER_GROUP
D_EXPERT = 256
EPS = 1e-6

LANES = 128
V7X_VMEM_BYTES = 64 * 1024 * 1024
VMEM_LIMIT = V7X_VMEM_BYTES * 7 // 8

ATTN_SCALE = (MLA_NOPE + MLA_ROPE) ** -0.5 * float(np.log2(np.e))
QK_WIDTH = 2 * LANES
V_WIDTH = 2 * LANES
ATTN_GROUP = 8

C_CQ = 0
C_CKV = C_CQ + MLA_Q_RANK
C_KPE = C_CKV + MLA_KV_RANK
C_GQ = C_KPE + LANES
C_GK = C_GQ + GLA_HEADS * GLA_DK
C_GV = C_GK + GLA_HEADS * GLA_DK
C_OG = C_GV + GLA_HEADS * GLA_DV
C_LR = C_OG + GLA_HEADS * GLA_DV
D_IN_PACKED = C_LR + LANES

ROUTER_GROUP_LANE0 = 0
ROUTER_EXPERT_LANE0 = N_GROUPS


def _cparams(*semantics):
    return pltpu.CompilerParams(dimension_semantics=semantics, vmem_limit_bytes=VMEM_LIMIT)


def _rms(x, g):
    return x * lax.rsqrt(jnp.mean(x * x, axis=-1, keepdims=True) + EPS) * g


def _dot(a, b):
    return jnp.dot(a, b, preferred_element_type=F32)


def _dot_nt(a, b):
    return lax.dot_general(a, b, (((1,), (1,)), ((), ())), preferred_element_type=F32)


def _dot_tn(a, b):
    return lax.dot_general(a, b, (((0,), (0,)), ((), ())), preferred_element_type=F32)


def _full_spec(shape):
    return pl.BlockSpec(shape, lambda *_: (0,) * len(shape))


ROW_CHUNKS = D_MODEL // LANES


def _rows_to_tiles(x):
    chunks = jnp.stack([x[:, s * LANES:(s + 1) * LANES] for s in range(x.shape[1] // LANES)], axis=0)
    return pltpu.einshape("smd->msd", chunks)


def _tiles_to_rows(x):
    chunks = pltpu.einshape("msd->smd", x)
    return jnp.concatenate([chunks[s] for s in range(x.shape[1])], axis=-1)


PACKED_CHUNKS = ROW_CHUNKS // 2


def _pack_bf16_pairs(x):
    half = x.shape[1] // 2
    bits = lambda v: lax.bitcast_convert_type(v.astype(BF16).astype(F32), jnp.uint32)
    return (bits(x[:, half:]) & jnp.uint32(0xFFFF0000)) | (bits(x[:, :half]) >> 16)


def _unpack_bf16_pairs(w):
    lo = lax.bitcast_convert_type(w << 16, F32)
    hi = lax.bitcast_convert_type(w & jnp.uint32(0xFFFF0000), F32)
    return jnp.concatenate([lo, hi], axis=-1)


def _rope_pairs(x, cos, sin_signed, first_half):
    swapped = jnp.where(first_half, pltpu.roll(x, LANES - MLA_ROPE // 2, 1), pltpu.roll(x, MLA_ROPE // 2, 1))
    return x * cos + swapped * sin_signed


def _inproj_body(x_ref, cos_ref, sin_ref, nmix_ref, win_ref, qan_ref, wuq_ref, kvan_ref, wukv_ref,
                 wgate_ref, bgate_ref,
                 q_ref, k_ref, v_ref, gq_ref, gk_ref, gv_ref, gf_ref, gb_ref, og_ref):
    hn = _rms(x_ref[...], nmix_ref[...]).astype(BF16)

    def proj(lo, hi):
        return _dot(hn, win_ref[:, lo:hi])

    cos = cos_ref[...]
    sin = sin_ref[...]
    lane = lax.broadcasted_iota(jnp.int32, cos.shape, 1)
    first_half = (lane & (MLA_ROPE - 1)) < MLA_ROPE // 2
    low_lanes = lane < MLA_ROPE

    cq = _rms(proj(C_CQ, C_CKV), qan_ref[...]).astype(BF16)
    qn = _dot(cq, wuq_ref[:, 0:MLA_HEADS * MLA_NOPE]) * ATTN_SCALE
    qr = _dot(cq, wuq_ref[:, MLA_HEADS * MLA_NOPE:])
    for j in range(MLA_HEADS // 2):
        rj = (_rope_pairs(qr[:, j * LANES:(j + 1) * LANES], cos, sin, first_half) * ATTN_SCALE).astype(BF16)
        for h in (2 * j, 2 * j + 1):
            q_ref[h, :, 0:LANES] = qn[:, h * LANES:(h + 1) * LANES].astype(BF16)
            q_ref[h, :, LANES:QK_WIDTH] = rj

    ckv = _rms(proj(C_CKV, C_KPE), kvan_ref[...]).astype(BF16)
    kv = _dot(ckv, wukv_ref[...])
    kr = _rope_pairs(proj(C_KPE, C_GQ), cos, sin, first_half)
    kr_even = jnp.where(low_lanes, kr, 0.0).astype(BF16)
    kr_odd = jnp.where(low_lanes, 0.0, kr).astype(BF16)
    for h in range(MLA_HEADS):
        base = h * (MLA_NOPE + MLA_V)
        k_ref[h, :, 0:LANES] = kv[:, base:base + MLA_NOPE].astype(BF16)
        k_ref[h, :, LANES:QK_WIDTH] = kr_even if h % 2 == 0 else kr_odd
        v_ref[h, :, 0:MLA_V] = kv[:, base + MLA_NOPE:base + MLA_NOPE + MLA_V].astype(BF16)
        v_ref[h, :, MLA_V:V_WIDTH] = jnp.ones((kv.shape[0], V_WIDTH - MLA_V), BF16)

    gq_ref[...] = (proj(C_GQ, C_GK) * (GLA_DK ** -0.5)).astype(BF16)
    gk_ref[...] = proj(C_GK, C_GV).astype(BF16)
    gv_ref[...] = proj(C_GV, C_OG).astype(BF16)
    og_ref[...] = proj(C_OG, C_LR).astype(BF16)
    pre = _dot(proj(C_LR, D_IN_PACKED).astype(BF16), wgate_ref[...]) + bgate_ref[...]
    logsig = jnp.minimum(pre, 0.0) - jnp.log1p(jnp.exp(-jnp.abs(pre)))
    gates = logsig * (1.0 / GLA_TAU)
    gf_ref[...] = gates[:, 0:GLA_HEADS * GLA_DK]
    gb_ref[...] = gates[:, GLA_HEADS * GLA_DK:]


def _inproj(x2d, cos, sin, w, tm):
    t = x2d.shape[0]
    blocks_per_seq = cos.shape[0] // tm
    hk = GLA_HEADS * GLA_DK
    hv = GLA_HEADS * GLA_DV
    row = lambda width: pl.BlockSpec((tm, width), lambda i: (i, 0))
    head_rows = lambda width: pl.BlockSpec((MLA_HEADS, tm, width), lambda i: (0, i, 0))
    tab = pl.BlockSpec((tm, LANES), lambda i: (i % blocks_per_seq, 0))
    out_shape = (
        jax.ShapeDtypeStruct((MLA_HEADS, t, QK_WIDTH), BF16),
        jax.ShapeDtypeStruct((MLA_HEADS, t, QK_WIDTH), BF16),
        jax.ShapeDtypeStruct((MLA_HEADS, t, V_WIDTH), BF16),
        jax.ShapeDtypeStruct((t, hk), BF16),
        jax.ShapeDtypeStruct((t, hk), BF16),
        jax.ShapeDtypeStruct((t, hv), BF16),
        jax.ShapeDtypeStruct((t, hk), F32),
        jax.ShapeDtypeStruct((t, hk), F32),
        jax.ShapeDtypeStruct((t, hv), BF16),
    )
    return pl.pallas_call(
        _inproj_body,
        grid=(t // tm,),
        in_specs=[row(D_MODEL), tab, tab,
                  _full_spec((1, D_MODEL)), _full_spec((D_MODEL, D_IN_PACKED)),
                  _full_spec((1, MLA_Q_RANK)), _full_spec(w['wuq'].shape),
                  _full_spec((1, MLA_KV_RANK)), _full_spec(w['wukv'].shape),
                  _full_spec(w['wgate'].shape), _full_spec(w['bgate'].shape)],
        out_specs=(head_rows(QK_WIDTH), head_rows(QK_WIDTH), head_rows(V_WIDTH),
                   row(hk), row(hk), row(hv), row(hk), row(hk), row(hv)),
        out_shape=out_shape,
        compiler_params=_cparams("parallel"),
        name="inproj",
    )(x2d, cos, sin, w['norm_mix'], w['win'], w['q_a_norm'], w['wuq'], w['kv_a_norm'], w['wukv'],
      w['wgate'], w['bgate'])


def _attn_body(q_ref, k_ref, v_ref, km_ref, vm_ref, o_ref, s_ref, acc_ref, *, tk):
    q = q_ref[...]
    n_groups = k_ref.shape[0] // (ATTN_GROUP * tk)

    def scores(j):
        return _dot_nt(q, k_ref[pl.ds(pl.multiple_of(j * tk, tk), tk), :])

    def values(j):
        return v_ref[pl.ds(pl.multiple_of(j * tk, tk), tk), :]

    def absorb(m, s, v):
        m_new = jnp.maximum(m, jnp.max(s, axis=-1, keepdims=True))
        p = jnp.exp2(s - m_new)
        acc_ref[...] = jnp.exp2(m - m_new) * acc_ref[...] + _dot(p.astype(BF16), v)
        return m_new

    s_ref[0] = scores(0)
    sm = _dot_nt(q, km_ref[...])
    sm = jnp.where(lax.broadcasted_iota(jnp.int32, sm.shape, 1) < N_META, sm, -jnp.inf)
    m = jnp.max(sm, axis=-1, keepdims=True)
    acc_ref[...] = _dot(jnp.exp2(sm - m).astype(BF16), vm_ref[...])

    def group(g, m, last):
        for i in range(ATTN_GROUP):
            j = ATTN_GROUP * g + i
            s = s_ref[i % 2]
            if not (last and i == ATTN_GROUP - 1):
                s_ref[(i + 1) % 2] = scores(j + 1)
            m = absorb(m, s, values(j))
        return m

    m = lax.fori_loop(0, n_groups - 1, lambda g, m: group(g, m, False), m)
    group(n_groups - 1, m, True)
    acc = acc_ref[...]
    o_ref[...] = (acc[:, :MLA_V] / acc[:, MLA_V:]).astype(o_ref.dtype)


def _attention(q, k, v, km, vm, bsz, seq, tq, tk):
    nq = seq // tq
    return pl.pallas_call(
        functools.partial(_attn_body, tk=tk),
        grid=(bsz, MLA_HEADS, nq),
        in_specs=[pl.BlockSpec((None, tq, QK_WIDTH), lambda b, h, i: (h, b * nq + i, 0)),
                  pl.BlockSpec((None, seq, QK_WIDTH), lambda b, h, i: (h, b, 0)),
                  pl.BlockSpec((None, seq, V_WIDTH), lambda b, h, i: (h, b, 0)),
                  pl.BlockSpec((None, LANES, QK_WIDTH), lambda b, h, i: (h, 0, 0)),
                  pl.BlockSpec((None, LANES, V_WIDTH), lambda b, h, i: (h, 0, 0))],
        out_specs=pl.BlockSpec((tq, MLA_V), lambda b, h, i: (b * nq + i, h)),
        out_shape=jax.ShapeDtypeStruct((bsz * seq, MLA_HEADS * MLA_V), BF16),
        scratch_shapes=[pltpu.VMEM((2, tq, tk), F32), pltpu.VMEM((tq, V_WIDTH), F32)],
        compiler_params=_cparams("parallel", "parallel", "arbitrary"),
        name="mla_attention",
    )(q, k, v, km, vm)


def _split3(x):
    hi = x.astype(BF16)
    r1 = x - hi.astype(F32)
    mid = r1.astype(BF16)
    lo = (r1 - mid.astype(F32)).astype(BF16)
    return hi, mid, lo


def _gla_log_decay(g, tri):
    g_hi, g_mid, g_lo = _split3(g)
    return _dot(tri, g_hi) + _dot(tri, g_mid) + _dot(tri, g_lo)


def _head_rows(x):
    even = lax.broadcasted_iota(jnp.int32, x.shape, 1) < GLA_DK
    return jnp.concatenate([jnp.where(even, x, 0.0), jnp.where(even, 0.0, x)], axis=0).astype(BF16)


def _gla_operands(q, k, b, mid, last):
    b_last = b[last:last + 1, :]
    ke = (k * jnp.exp(b_last - b)).astype(BF16)
    if q is None:
        return jnp.exp(b_last), ke, None, None, None
    b_mid = b[mid:mid + 1, :]
    ks = (k * jnp.exp(b_mid - b)).astype(BF16)
    return jnp.exp(b_last), ke, ks, _head_rows(q * jnp.exp(b - b_mid)), _head_rows(q * jnp.exp(b))


def _gla_state_update(v_even, v_odd, ke):
    return jnp.where(lax.broadcasted_iota(jnp.int32, (GLA_DV, LANES), 1) < GLA_DK,
                     _dot_tn(v_even, ke), _dot_tn(v_odd, ke))


def _gla_intra(v_even, v_odd, ks, qs2, keep):
    scores = jnp.where(keep, _dot_nt(qs2, ks), 0.0).astype(BF16)
    return _dot(scores[:GLA_CHUNK], v_even), _dot(scores[GLA_CHUNK:], v_odd)


def _gla_body(qf_ref, kf_ref, vf_ref, gf_ref, qb_ref, kb_ref, vb_ref, gb_ref, mk_ref, mv_ref, mg_ref,
              of_ref, ob_ref, state_ref):
    n_chunks = qf_ref.shape[0] // GLA_CHUNK
    n_pairs = GLA_HEADS // 2
    r = lax.broadcasted_iota(jnp.int32, (GLA_CHUNK, GLA_CHUNK), 0)
    c = lax.broadcasted_iota(jnp.int32, (GLA_CHUNK, GLA_CHUNK), 1)
    tri_f = jnp.where(c <= r, 1.0, 0.0).astype(BF16)
    tri_b = jnp.where(c >= r, 1.0, 0.0).astype(BF16)
    r2 = lax.broadcasted_iota(jnp.int32, (2 * GLA_CHUNK, GLA_CHUNK), 0) & (GLA_CHUNK - 1)
    c2 = lax.broadcasted_iota(jnp.int32, (2 * GLA_CHUNK, GLA_CHUNK), 1)
    keep_f = c2 <= r2
    keep_b = c2 >= r2
    mid_f, last_f = GLA_CHUNK // 2 - 1, GLA_CHUNK - 1
    mid_b, last_b = GLA_CHUNK // 2, 0

    def pair_cols(p):
        return slice(p * LANES, (p + 1) * LANES)

    def head_cols(h):
        return slice(h * GLA_DV, (h + 1) * GLA_DV)

    @pl.when(pl.program_id(1) == 0)
    def _():
        for p in range(n_pairs):
            b = _gla_log_decay(mg_ref[:, pair_cols(p)], tri_f)
            _, ke, _, _, _ = _gla_operands(None, mk_ref[:, pair_cols(p)], b, mid_f, last_f)
            state_ref[p] = _gla_state_update(mv_ref[:, head_cols(2 * p)].astype(BF16),
                                             mv_ref[:, head_cols(2 * p + 1)].astype(BF16), ke)
            state_ref[n_pairs + p] = jnp.zeros((GLA_DV, LANES), F32)

    scans = []
    for p in range(n_pairs):
        scans.append((p, list(range(n_chunks)), qf_ref, kf_ref, vf_ref, gf_ref, of_ref, p,
                      tri_f, keep_f, mid_f, last_f))
        scans.append((p, list(reversed(range(n_chunks))), qb_ref, kb_ref, vb_ref, gb_ref, ob_ref, n_pairs + p,
                      tri_b, keep_b, mid_b, last_b))

    def rows(c):
        return slice(c * GLA_CHUNK, (c + 1) * GLA_CHUNK)

    def values(v_ref, p, c):
        return (v_ref[rows(c), head_cols(2 * p)].astype(BF16), v_ref[rows(c), head_cols(2 * p + 1)].astype(BF16))

    log_decay = [[_gla_log_decay(g_ref[rows(c), pair_cols(p)], tri) for c in order]
                 for (p, order, _, _, _, g_ref, _, _, tri, _, _, _) in scans]
    operands = [[_gla_operands(q_ref[rows(c), pair_cols(p)], k_ref[rows(c), pair_cols(p)], b, mid, last)
                 for c, b in zip(order, bs)]
                for (p, order, q_ref, k_ref, _, _, _, _, _, _, mid, last), bs in zip(scans, log_decay)]
    updates = [[_gla_state_update(*values(v_ref, p, c), ops[1]) for c, ops in zip(order, opss)]
               for (p, order, _, _, v_ref, _, _, _, _, _, _, _), opss in zip(scans, operands)]
    intra = [[_gla_intra(*values(v_ref, p, c), ops[2], ops[3], keep) for c, ops in zip(order, opss)]
             for (p, order, _, _, v_ref, _, _, _, _, keep, _, _), opss in zip(scans, operands)]
    states = []
    for (_, order, _, _, _, _, _, slot, _, _, _, _), opss, upds in zip(scans, operands, updates):
        st = state_ref[slot]
        entering = []
        for ops, upd in zip(opss, upds):
            entering.append(st.astype(BF16))
            st = st * ops[0] + upd
        state_ref[slot] = st
        states.append(entering)
    for (p, order, _, _, _, _, o_ref, _, _, _, _, _), opss, sts, locs in zip(scans, operands, states, intra):
        for c, ops, st, (o_even, o_odd) in zip(order, opss, sts, locs):
            inter = _dot_nt(ops[4], st)
            o_ref[rows(c), head_cols(2 * p)] = (o_even + inter[:GLA_CHUNK]).astype(o_ref.dtype)
            o_ref[rows(c), head_cols(2 * p + 1)] = (o_odd + inter[GLA_CHUNK:]).astype(o_ref.dtype)


def _gla(gq, gk, gv, gf, gb, mk, mv, mg, bsz, seq, tb):
    nb = seq // tb
    hk = GLA_HEADS * GLA_DK
    hv = GLA_HEADS * GLA_DV
    fwd = lambda width: pl.BlockSpec((tb, width), lambda b, j: (b * nb + j, 0))
    bwd = lambda width: pl.BlockSpec((tb, width), lambda b, j: (b * nb + nb - 1 - j, 0))
    t = bsz * seq
    return pl.pallas_call(
        _gla_body,
        grid=(bsz, nb),
        in_specs=[fwd(hk), fwd(hk), fwd(hv), fwd(hk), bwd(hk), bwd(hk), bwd(hv), bwd(hk),
                  _full_spec(mk.shape), _full_spec(mv.shape), _full_spec(mg.shape)],
        out_specs=(fwd(hv), bwd(hv)),
        out_shape=(jax.ShapeDtypeStruct((t, hv), BF16), jax.ShapeDtypeStruct((t, hv), BF16)),
        scratch_shapes=[pltpu.VMEM((2 * (GLA_HEADS // 2), GLA_DV, LANES), F32)],
        compiler_params=_cparams("parallel", "arbitrary"),
        name="gla_scan",
    )(gq, gk, gv, gf, gq, gk, gv, gb, mk, mv, mg)


def _mix_body(x_ref, a_ref, of_ref, ob_ref, og_ref, gnorm_ref, wout_ref, nffn_ref, wr_ref, br_ref,
              h1_ref, hn_ref, route_ref, ids_ref, count_ref, tri_ref):
    tm = x_ref.shape[0]

    @pl.when(pl.program_id(0) == 0)
    def _():
        r = lax.broadcasted_iota(jnp.int32, (tm, tm), 0)
        c = lax.broadcasted_iota(jnp.int32, (tm, tm), 1)
        tri_ref[...] = jnp.where(c < r, 1.0, 0.0).astype(BF16)
        count_ref[...] = jnp.zeros_like(count_ref)

    a_width = MLA_HEADS * MLA_V
    gated = []
    for h in range(GLA_HEADS):
        cols = slice(h * GLA_DV, (h + 1) * GLA_DV)
        o = of_ref[:, cols].astype(F32) + ob_ref[:, cols].astype(F32)
        og = og_ref[:, cols].astype(F32)
        silu = og / (1.0 + jnp.exp(-og))
        gated.append((_rms(o, gnorm_ref[...]) * silu).astype(BF16))
    h1 = (x_ref[...] + _dot(a_ref[...], wout_ref[0:a_width, :])
          + _dot(jnp.concatenate(gated, axis=-1), wout_ref[a_width:, :]))
    h1_ref[...] = h1
    hn = _rms(h1, nffn_ref[...])
    hn_ref[...] = _rows_to_tiles(_pack_bf16_pairs(hn))

    hn_hi = hn.astype(BF16)
    hn_lo = (hn - hn_hi.astype(F32)).astype(BF16)
    hi_terms = _dot(hn_hi, wr_ref[...])
    logits = hi_terms[:, :LANES] + hi_terms[:, LANES:] + _dot(hn_lo, wr_ref[:, :LANES]) + br_ref[...]
    lane = lax.broadcasted_iota(jnp.int32, logits.shape, 1).astype(F32)
    none = float(LANES)
    neg = -jnp.inf

    def lane_max(x):
        return jnp.max(x, axis=-1, keepdims=True)

    def lane_sum(x):
        return jnp.sum(x, axis=-1, keepdims=True)

    def first_lane(mask):
        return jnp.min(jnp.where(mask, lane, none), axis=-1, keepdims=True)

    is_group = lane < float(N_GROUPS)
    g_max = lane_max(jnp.where(is_group, logits, neg))
    g_exp = jnp.where(is_group, jnp.exp(logits - g_max), 0.0)
    g_prob = g_exp / lane_sum(g_exp)
    g_w = lane_max(g_prob)
    g_idx = first_lane(is_group & (g_prob == g_w))
    e_lo = float(ROUTER_EXPERT_LANE0) + float(EXPERTS_PER_GROUP) * g_idx
    sel = (lane >= e_lo) & (lane < e_lo + float(EXPERTS_PER_GROUP))
    e_max = lane_max(jnp.where(sel, logits, neg))
    e_exp = jnp.where(sel, jnp.exp(logits - e_max), 0.0)
    e_prob = e_exp / lane_sum(e_exp)
    p1 = lane_max(jnp.where(sel, e_prob, neg))
    i1 = first_lane(sel & (e_prob == p1))
    rest = sel & (lane != i1)
    p2 = lane_max(jnp.where(rest, e_prob, neg))
    i2 = first_lane(rest & (e_prob == p2))
    denom = p1 + p2
    chosen = jnp.where((lane == i1) | (lane == i2), 1.0, 0.0)
    rank = count_ref[...] + _dot(tri_ref[...], chosen.astype(BF16))
    count_ref[...] += jnp.sum(chosen, axis=0, keepdims=True)
    fields = (i1 - float(ROUTER_EXPERT_LANE0), i2 - float(ROUTER_EXPERT_LANE0),
              lane_sum(jnp.where(lane == i1, rank, 0.0)), lane_sum(jnp.where(lane == i2, rank, 0.0)),
              g_w * (p1 / denom), g_w * (p2 / denom))
    route = jnp.zeros_like(logits)
    for k, value in enumerate(fields):
        route = jnp.where(lane == float(k), value, route)
    route_ref[...] = route
    ids_ref[...] = jnp.transpose(route)[0:ROUTE_ID_ROWS, :].astype(jnp.int32)


ROUTE_E1, ROUTE_E2, ROUTE_RANK1, ROUTE_RANK2, ROUTE_W1, ROUTE_W2 = range(6)
ROUTE_ID_ROWS = 8


def _mix(x2d, a, o_f, o_b, og, w, tm):
    t = x2d.shape[0]
    hv = GLA_HEADS * GLA_DV
    row = lambda width: pl.BlockSpec((tm, width), lambda i: (i, 0))
    return pl.pallas_call(
        _mix_body,
        grid=(t // tm,),
        in_specs=[row(D_MODEL), row(MLA_HEADS * MLA_V), row(hv), row(hv), row(hv),
                  _full_spec((1, GLA_DV)), _full_spec(w['wout'].shape), _full_spec((1, D_MODEL)),
                  _full_spec(w['wr'].shape), _full_spec(w['br'].shape)],
        out_specs=(row(D_MODEL), pl.BlockSpec((tm, PACKED_CHUNKS, LANES), lambda i: (i, 0, 0)), row(LANES),
                   pl.BlockSpec((ROUTE_ID_ROWS, tm), lambda i: (0, i)), _full_spec((1, LANES))),
        out_shape=(jax.ShapeDtypeStruct((t, D_MODEL), F32), jax.ShapeDtypeStruct((t, PACKED_CHUNKS, LANES), jnp.uint32),
                   jax.ShapeDtypeStruct((t, LANES), F32), jax.ShapeDtypeStruct((ROUTE_ID_ROWS, t), jnp.int32),
                   jax.ShapeDtypeStruct((1, LANES), F32)),
        scratch_shapes=[pltpu.VMEM((tm, tm), BF16)],
        compiler_params=_cparams("arbitrary"),
        name="mix_router",
    )(x2d, a, o_f, o_b, og, w['gla_norm'], w['wout'], w['norm_ffn'], w['wr'], w['br'])


EXPERT_TILE = 256
ROW_TILE = 512


SC_CORES = 2
SC_SUBCORES = 16
SC_GATHER_ROWS = 32


def _sc_gather(table, idx):
    n = idx.shape[0]
    workers = SC_CORES * SC_SUBCORES
    per_worker = n // workers
    assert n % (workers * SC_GATHER_ROWS) == 0
    mesh = plsc.VectorSubcoreMesh(core_axis_name="c", subcore_axis_name="s")

    @functools.partial(
        pl.kernel, mesh=mesh,
        out_type=jax.ShapeDtypeStruct((n,) + table.shape[1:], table.dtype),
        scratch_types=[pltpu.VMEM((SC_GATHER_ROWS,), jnp.int32),
                       pltpu.VMEM((SC_GATHER_ROWS,) + table.shape[1:], table.dtype),
                       pltpu.SemaphoreType.DMA])
    def gather(table_ref, idx_ref, out_ref, idx_buf, rows_buf, sem):
        base = (lax.axis_index("s") * SC_CORES + lax.axis_index("c")) * per_worker

        @pl.loop(0, per_worker // SC_GATHER_ROWS)
        def _(j):
            rows = pl.ds(base + j * SC_GATHER_ROWS, SC_GATHER_ROWS)
            pltpu.sync_copy(idx_ref.at[rows], idx_buf)
            pltpu.async_copy(table_ref.at[idx_buf], rows_buf, sem).wait()
            pltpu.sync_copy(rows_buf, out_ref.at[rows])

    return gather(table, idx)


def _sc_scatter(rows, idx, n_out):
    copies, n = idx.shape
    idx = idx.reshape(copies * n)
    workers = SC_CORES * SC_SUBCORES
    per_worker = n // workers
    assert n % (workers * SC_GATHER_ROWS) == 0
    mesh = plsc.VectorSubcoreMesh(core_axis_name="c", subcore_axis_name="s")

    @functools.partial(
        pl.kernel, mesh=mesh,
        out_type=jax.ShapeDtypeStruct((n_out,) + rows.shape[1:], rows.dtype),
        scratch_types=[pltpu.VMEM((SC_GATHER_ROWS,), jnp.int32),
                       pltpu.VMEM((SC_GATHER_ROWS,) + rows.shape[1:], rows.dtype),
                       pltpu.SemaphoreType.DMA])
    def scatter(rows_ref, idx_ref, out_ref, idx_buf, rows_buf, sem):
        base = (lax.axis_index("s") * SC_CORES + lax.axis_index("c")) * per_worker

        @pl.loop(0, per_worker // SC_GATHER_ROWS)
        def _(j):
            first = base + j * SC_GATHER_ROWS
            pltpu.sync_copy(rows_ref.at[pl.ds(first, SC_GATHER_ROWS)], rows_buf)
            for k in range(copies):
                pltpu.sync_copy(idx_ref.at[pl.ds(k * n + first, SC_GATHER_ROWS)], idx_buf)
                pltpu.async_copy(rows_buf, out_ref.at[idx_buf], sem).wait()

    return scatter(rows, idx)


SLOT_TILE = 2048


def _slots_body(starts_ref, ids_ref, slots_ref):
    ids = ids_ref[...]
    experts = ids[ROUTE_E1:ROUTE_E1 + 2, :]
    start = jnp.zeros_like(experts)
    for e in range(N_EXPERTS):
        start = jnp.where(experts == e, starts_ref[e], start)
    slots_ref[...] = start + ids[ROUTE_RANK1:ROUTE_RANK1 + 2, :]


def _slots(starts, ids):
    t = ids.shape[1]
    return pl.pallas_call(
        _slots_body,
        grid_spec=pltpu.PrefetchScalarGridSpec(
            num_scalar_prefetch=1,
            grid=(t // SLOT_TILE,),
            in_specs=[pl.BlockSpec((ROUTE_ID_ROWS, SLOT_TILE), lambda i, *_: (0, i))],
            out_specs=pl.BlockSpec((2, SLOT_TILE), lambda i, *_: (0, i))),
        out_shape=jax.ShapeDtypeStruct((2, t), jnp.int32),
        compiler_params=_cparams("parallel"),
        name="moe_slots",
    )(starts, ids)


def _queue_body(slots_ref, token_ref, *, n_tokens):
    i = pl.program_id(0)

    @pl.when(i == 0)
    def _():
        for base in range(0, token_ref.shape[0], n_tokens):
            def clear(j, carry):
                token_ref[base + j] = j
                return carry
            lax.fori_loop(0, min(n_tokens, token_ref.shape[0] - base), clear, 0, unroll=8)

    def place(r, carry):
        for k in range(2):
            token_ref[slots_ref[k, r]] = i * SLOT_TILE + r
        return carry

    lax.fori_loop(0, SLOT_TILE, place, 0, unroll=8)


def _queue_tokens(slots, n_rows):
    t = slots.shape[1]
    return pl.pallas_call(
        functools.partial(_queue_body, n_tokens=t),
        grid=(t // SLOT_TILE,),
        in_specs=[pl.BlockSpec((2, SLOT_TILE), lambda i: (0, i), memory_space=pltpu.SMEM)],
        out_specs=pl.BlockSpec(memory_space=pltpu.SMEM),
        out_shape=jax.ShapeDtypeStruct((n_rows,), jnp.int32),
        compiler_params=_cparams("arbitrary"),
        name="moe_queue",
    )(slots)


TILES_PER_STEP = 4


def _expert_body(tile_expert_ref, n_tiles_ref, tile_valid_ref, xs_ref, *refs):
    wg_refs = refs[0:TILES_PER_STEP]
    wu_refs = refs[TILES_PER_STEP:2 * TILES_PER_STEP]
    wd_refs = refs[2 * TILES_PER_STEP:3 * TILES_PER_STEP]
    ys_ref = refs[3 * TILES_PER_STEP]
    first = pl.program_id(0) * TILES_PER_STEP
    used = first < n_tiles_ref[0]

    @pl.when(used)
    def _():
        rows = [slice(j * EXPERT_TILE, (j + 1) * EXPERT_TILE) for j in range(TILES_PER_STEP)]
        xs = [_unpack_bf16_pairs(_tiles_to_rows(xs_ref[r])) for r in rows]
        xs = [jnp.where(lax.broadcasted_iota(jnp.int32, x.shape, 0) < tile_valid_ref[first + j], x, 0.0)
              for j, x in enumerate(xs)]
        xs = [x.astype(BF16) for x in xs]
        gates = [_dot(x, wg[...]) for x, wg in zip(xs, wg_refs)]
        ups = [_dot(x, wu[...]) for x, wu in zip(xs, wu_refs)]
        acts = [(g / (1.0 + jnp.exp(-g))) * u for g, u in zip(gates, ups)]
        ys = [_dot(act.astype(BF16), wd[...]) for act, wd in zip(acts, wd_refs)]
        for r, y in zip(rows, ys):
            ys_ref[r] = _rows_to_tiles(_pack_bf16_pairs(y))

    @pl.when(jnp.logical_not(used))
    def _():
        ys_ref[...] = jnp.zeros_like(ys_ref)


def _experts(tile_expert, n_tiles, tile_valid, xs, w):
    n_steps = xs.shape[0] // (EXPERT_TILE * TILES_PER_STEP)

    def step(i, nt):
        return jnp.minimum(i, (nt[0] - 1) // TILES_PER_STEP)

    def expert(i, j, te, nt):
        return te[jnp.minimum(step(i, nt) * TILES_PER_STEP + j, nt[0] - 1)]

    tiles = lambda index: pl.BlockSpec((EXPERT_TILE * TILES_PER_STEP, PACKED_CHUNKS, LANES), index)
    up_spec = lambda j: pl.BlockSpec((None, D_MODEL, D_EXPERT), lambda i, te, nt, tv: (expert(i, j, te, nt), 0, 0))
    down_spec = lambda j: pl.BlockSpec((None, D_EXPERT, D_MODEL), lambda i, te, nt, tv: (expert(i, j, te, nt), 0, 0))
    slots = range(TILES_PER_STEP)
    return pl.pallas_call(
        _expert_body,
        grid_spec=pltpu.PrefetchScalarGridSpec(
            num_scalar_prefetch=3,
            grid=(n_steps,),
            in_specs=([tiles(lambda i, te, nt, tv: (step(i, nt), 0, 0))]
                      + [up_spec(j) for j in slots] + [up_spec(j) for j in slots] + [down_spec(j) for j in slots]),
            out_specs=tiles(lambda i, te, nt, tv: (i, 0, 0))),
        out_shape=jax.ShapeDtypeStruct(xs.shape, xs.dtype),
        compiler_params=_cparams("arbitrary"),
        name="moe_experts",
    )(tile_expert, n_tiles, tile_valid, xs, *([w['w_gate']] * TILES_PER_STEP), *([w['w_up']] * TILES_PER_STEP),
      *([w['w_down']] * TILES_PER_STEP))


def _combine_body(h1_ref, route_ref, nfin_ref, y1_ref, y2_ref, out_ref):
    route = route_ref[...]
    lane = lax.broadcasted_iota(jnp.int32, route.shape, 1)
    w1 = jnp.sum(jnp.where(lane == ROUTE_W1, route, 0.0), axis=-1, keepdims=True)
    w2 = jnp.sum(jnp.where(lane == ROUTE_W2, route, 0.0), axis=-1, keepdims=True)
    y1 = _unpack_bf16_pairs(_tiles_to_rows(y1_ref[...]))
    y2 = _unpack_bf16_pairs(_tiles_to_rows(y2_ref[...]))
    y = w1 * y1 + w2 * y2
    out_ref[...] = _rms(h1_ref[...] + y, nfin_ref[...])


def _combine(h1, route, y12, w):
    t = h1.shape[0]
    row = lambda width: pl.BlockSpec((ROW_TILE, width), lambda i: (i, 0))
    tiles = lambda k: pl.BlockSpec((None, ROW_TILE, PACKED_CHUNKS, LANES), lambda i: (k, i, 0, 0))
    return pl.pallas_call(
        _combine_body,
        grid=(t // ROW_TILE,),
        in_specs=[row(D_MODEL), row(LANES), _full_spec((1, D_MODEL)), tiles(0), tiles(1)],
        out_specs=row(D_MODEL),
        out_shape=jax.ShapeDtypeStruct((t, D_MODEL), F32),
        compiler_params=_cparams("parallel"),
        name="moe_combine",
    )(h1, route, w['norm_final'], y12, y12)


def _moe(hn, route, ids, counts, h1, w):
    t = hn.shape[0]
    n_rows = 2 * t + N_EXPERTS * EXPERT_TILE
    n_grid_tiles = n_rows // EXPERT_TILE
    count = counts[0, ROUTER_EXPERT_LANE0:ROUTER_EXPERT_LANE0 + N_EXPERTS].astype(jnp.int32)
    padded = jnp.maximum((count + EXPERT_TILE - 1) // EXPERT_TILE, 1) * EXPERT_TILE
    ends = jnp.sum(jnp.where(jnp.arange(N_EXPERTS)[:, None] <= jnp.arange(N_EXPERTS)[None, :], padded[:, None], 0),
                   axis=0)
    starts = ends - padded
    n_tiles = (ends[-1:] // EXPERT_TILE)
    tile_rows = jnp.arange(n_grid_tiles, dtype=jnp.int32) * EXPERT_TILE
    tile_expert = jnp.minimum(jnp.sum((ends[None, :] <= tile_rows[:, None]).astype(jnp.int32), axis=1), N_EXPERTS - 1)
    tile_valid = jnp.clip(jnp.sum(jnp.where(jnp.arange(N_EXPERTS)[None, :] == tile_expert[:, None],
                                            (starts + count)[None, :], 0), axis=1) - tile_rows, 0, EXPERT_TILE)
    slots = _slots(starts, ids)
    ys = _experts(tile_expert, n_tiles, tile_valid, _sc_scatter(hn, slots, n_rows), w)
    y12 = _sc_gather(ys, slots.reshape(2 * t)).reshape(2, t, PACKED_CHUNKS, LANES)
    return _combine(h1, route, y12, w)


def _rope_tables(positions):
    inv_freq = 1.0 / (ROPE_THETA ** (jnp.arange(0, MLA_ROPE, 2, dtype=F32) / MLA_ROPE))
    ang = positions.astype(F32)[:, None] * inv_freq[None, :]
    cos, sin = jnp.cos(ang), jnp.sin(ang)
    reps = LANES // MLA_ROPE
    return jnp.tile(jnp.concatenate([cos, cos], axis=-1), (1, reps)), jnp.tile(jnp.concatenate([-sin, sin], axis=-1), (1, reps))


def _pack_weights(norm_mix, w_in, q_a_norm, w_uq, kv_a_norm, w_ukv, w_gate_fwd, b_gate_fwd, w_gate_bwd, b_gate_bwd,
                  gla_norm, w_out, norm_ffn, w_router_group, b_router_group, w_router_expert, b_router_expert,
                  w_expert_gate, w_expert_up, w_expert_down, norm_final):
    l = 0
    hk = GLA_HEADS * GLA_DK
    hv = GLA_HEADS * GLA_DV
    c_q, c_kv, k_pe, gq, gk, gv, lr_f, lr_b, og = jnp.split(
        w_in[l], np.cumsum([MLA_Q_RANK, MLA_KV_RANK, MLA_ROPE, hk, hk, hv, GLA_GATE_RANK, GLA_GATE_RANK])[:].tolist(),
        axis=-1)
    lr_pad = jnp.zeros((D_MODEL, LANES - 2 * GLA_GATE_RANK), F32)
    win = jnp.concatenate([c_q, c_kv, k_pe, k_pe, gq, gk, gv, og, lr_f, lr_b, lr_pad], axis=-1).astype(BF16)
    wuq = w_uq[l].reshape(MLA_Q_RANK, MLA_HEADS, MLA_NOPE + MLA_ROPE)
    wuq = jnp.concatenate([wuq[:, :, :MLA_NOPE].reshape(MLA_Q_RANK, -1), wuq[:, :, MLA_NOPE:].reshape(MLA_Q_RANK, -1)],
                          axis=-1).astype(BF16)
    wgate = jnp.zeros((LANES, 2 * hk), F32)
    wgate = wgate.at[0:GLA_GATE_RANK, 0:hk].set(w_gate_fwd[l])
    wgate = wgate.at[GLA_GATE_RANK:2 * GLA_GATE_RANK, hk:].set(w_gate_bwd[l])
    wr = jnp.zeros((D_MODEL, LANES), F32)
    wr = wr.at[:, ROUTER_GROUP_LANE0:ROUTER_GROUP_LANE0 + N_GROUPS].set(w_router_group[l])
    wr = wr.at[:, ROUTER_EXPERT_LANE0:ROUTER_EXPERT_LANE0 + N_EXPERTS].set(w_router_expert[l])
    wr_hi = wr.astype(BF16)
    br = jnp.zeros((1, LANES), F32)
    br = br.at[0, ROUTER_GROUP_LANE0:ROUTER_GROUP_LANE0 + N_GROUPS].set(b_router_group[l])
    br = br.at[0, ROUTER_EXPERT_LANE0:ROUTER_EXPERT_LANE0 + N_EXPERTS].set(b_router_expert[l])
    return {
        'norm_mix': norm_mix[l][None], 'win': win, 'q_a_norm': q_a_norm[l][None], 'wuq': wuq,
        'kv_a_norm': kv_a_norm[l][None], 'wukv': w_ukv[l].astype(BF16),
        'wgate': wgate.astype(BF16), 'bgate': jnp.concatenate([b_gate_fwd[l], b_gate_bwd[l]])[None],
        'gla_norm': gla_norm[l][None], 'wout': w_out[l].astype(BF16), 'norm_ffn': norm_ffn[l][None],
        'wr': jnp.concatenate([wr_hi, (wr - wr_hi.astype(F32)).astype(BF16)], axis=-1), 'br': br,
        'w_gate': w_expert_gate[l].reshape(N_EXPERTS, D_MODEL, D_EXPERT).astype(BF16),
        'w_up': w_expert_up[l].reshape(N_EXPERTS, D_MODEL, D_EXPERT).astype(BF16),
        'w_down': w_expert_down[l].reshape(N_EXPERTS, D_EXPERT, D_MODEL).astype(BF16),
        'norm_final': norm_final[None],
    }


def _meta_streams(meta_tokens, w):
    cos, sin = _rope_tables(jnp.arange(N_META))
    _, k, v, _, gk, gv, gf, _, _ = _inproj(meta_tokens, cos, sin, w, N_META)
    pad_keys = ((0, 0), (0, LANES - N_META), (0, 0))
    front = ((GLA_CHUNK - N_META, 0), (0, 0))
    return (jnp.pad(k, pad_keys), jnp.pad(v, pad_keys), jnp.pad(gk, front), jnp.pad(gv, front), jnp.pad(gf, front))


def _token_mixers(x, meta, w, tm, tq, tk, tb):
    bsz, seq, _ = x.shape
    km, vm, mk, mv, mg = meta
    x2d = x.reshape(bsz * seq, D_MODEL)
    cos, sin = _rope_tables(N_META + jnp.arange(seq))
    q, k, v, gq, gk, gv, gf, gb, og = _inproj(x2d, cos, sin, w, tm)
    a = _attention(q, k, v, km, vm, bsz, seq, tq, tk)
    o_f, o_b = _gla(gq, gk, gv, gf, gb, mk, mv, mg, bsz, seq, tb)
    return _mix(x2d, a, o_f, o_b, og, w, tm)


def kernel(x_prompt, x_sample, meta_tokens, norm_mix, w_in, q_a_norm, w_uq, kv_a_norm, w_ukv, w_gate_fwd, b_gate_fwd, w_gate_bwd, b_gate_bwd, gla_norm, w_out, norm_ffn, w_router_group, b_router_group, w_router_expert, b_router_expert, w_expert_gate, w_expert_up, w_expert_down, norm_final):
    w = _pack_weights(norm_mix, w_in, q_a_norm, w_uq, kv_a_norm, w_ukv, w_gate_fwd, b_gate_fwd, w_gate_bwd,
                      b_gate_bwd, gla_norm, w_out, norm_ffn, w_router_group, b_router_group, w_router_expert,
                      b_router_expert, w_expert_gate, w_expert_up, w_expert_down, norm_final)
    meta = _meta_streams(meta_tokens, w)
    outs = []
    for x in (x_prompt, x_sample):
        h1, hn, route, ids, counts = _token_mixers(x, meta, w, tm=512, tq=1024, tk=512, tb=512)
        outs.append(_moe(hn, route, ids, counts, h1, w).reshape(x.shape))
    return tuple(outs)
```

```python
import functools

import numpy as np
import jax
import jax.numpy as jnp
from jax import lax
from jax.experimental import pallas as pl
from jax.experimental.pallas import tpu as pltpu
from jax.experimental.pallas import tpu_sc as plsc

F32 = jnp.float32
BF16 = jnp.bfloat16

D_MODEL = 1024
N_META = 16
MLA_HEADS = 4
MLA_Q_RANK = 384
MLA_KV_RANK = 256
MLA_NOPE = 128
MLA_ROPE = 64
MLA_V = 128
ROPE_THETA = 10000.0
GLA_HEADS = 4
GLA_DK = 64
GLA_DV = 128
GLA_GATE_RANK = 16
GLA_TAU = 16.0
GLA_CHUNK = 64
N_GROUPS = 4
EXPERTS_PER_GROUP = 8
N_EXPERTS = N_GROUPS * EXPERTS_PER_GROUP
D_EXPERT = 256
EPS = 1e-6

LANES = 128
V7X_VMEM_BYTES = 64 * 1024 * 1024
VMEM_LIMIT = V7X_VMEM_BYTES * 7 // 8

ATTN_SCALE = (MLA_NOPE + MLA_ROPE) ** -0.5 * float(np.log2(np.e))
QK_WIDTH = 2 * LANES
V_WIDTH = 2 * LANES
ATTN_GROUP = 8

C_CQ = 0
C_CKV = C_CQ + MLA_Q_RANK
C_KPE = C_CKV + MLA_KV_RANK
C_GQ = C_KPE + LANES
C_GK = C_GQ + GLA_HEADS * GLA_DK
C_GV = C_GK + GLA_HEADS * GLA_DK
C_OG = C_GV + GLA_HEADS * GLA_DV
C_LR = C_OG + GLA_HEADS * GLA_DV
D_IN_PACKED = C_LR + LANES

ROUTER_GROUP_LANE0 = 0
ROUTER_EXPERT_LANE0 = N_GROUPS


def _cparams(*semantics):
    return pltpu.CompilerParams(dimension_semantics=semantics, vmem_limit_bytes=VMEM_LIMIT)


def _rms(x, g):
    return x * lax.rsqrt(jnp.mean(x * x, axis=-1, keepdims=True) + EPS) * g


def _dot(a, b):
    return jnp.dot(a, b, preferred_element_type=F32)


def _dot_nt(a, b):
    return lax.dot_general(a, b, (((1,), (1,)), ((), ())), preferred_element_type=F32)


def _dot_tn(a, b):
    return lax.dot_general(a, b, (((0,), (0,)), ((), ())), preferred_element_type=F32)


def _full_spec(shape):
    return pl.BlockSpec(shape, lambda *_: (0,) * len(shape))


ROW_CHUNKS = D_MODEL // LANES


def _rows_to_tiles(x):
    chunks = jnp.stack([x[:, s * LANES:(s + 1) * LANES] for s in range(x.shape[1] // LANES)], axis=0)
    return pltpu.einshape("smd->msd", chunks)


def _tiles_to_rows(x):
    chunks = pltpu.einshape("msd->smd", x)
    return jnp.concatenate([chunks[s] for s in range(x.shape[1])], axis=-1)


PACKED_CHUNKS = ROW_CHUNKS // 2


def _pack_bf16_pairs(x):
    half = x.shape[1] // 2
    bits = lambda v: lax.bitcast_convert_type(v.astype(BF16).astype(F32), jnp.uint32)
    return (bits(x[:, half:]) & jnp.uint32(0xFFFF0000)) | (bits(x[:, :half]) >> 16)


def _unpack_bf16_pairs(w):
    lo = lax.bitcast_convert_type(w << 16, F32)
    hi = lax.bitcast_convert_type(w & jnp.uint32(0xFFFF0000), F32)
    return jnp.concatenate([lo, hi], axis=-1)


def _rope_pairs(x, cos, sin_signed, first_half):
    swapped = jnp.where(first_half, pltpu.roll(x, LANES - MLA_ROPE // 2, 1), pltpu.roll(x, MLA_ROPE // 2, 1))
    return x * cos + swapped * sin_signed


def _inproj_body(x_ref, cos_ref, sin_ref, nmix_ref, win_ref, qan_ref, wuq_ref, kvan_ref, wukv_ref,
                 wgate_ref, bgate_ref,
                 q_ref, k_ref, v_ref, gq_ref, gk_ref, gv_ref, gf_ref, gb_ref, og_ref):
    hn = _rms(x_ref[...], nmix_ref[...]).astype(BF16)

    def proj(lo, hi):
        return _dot(hn, win_ref[:, lo:hi])

    cos = cos_ref[...]
    sin = sin_ref[...]
    lane = lax.broadcasted_iota(jnp.int32, cos.shape, 1)
    first_half = (lane & (MLA_ROPE - 1)) < MLA_ROPE // 2
    low_lanes = lane < MLA_ROPE

    cq = _rms(proj(C_CQ, C_CKV), qan_ref[...]).astype(BF16)
    qn = _dot(cq, wuq_ref[:, 0:MLA_HEADS * MLA_NOPE]) * ATTN_SCALE
    qr = _dot(cq, wuq_ref[:, MLA_HEADS * MLA_NOPE:])
    for j in range(MLA_HEADS // 2):
        rj = (_rope_pairs(qr[:, j * LANES:(j + 1) * LANES], cos, sin, first_half) * ATTN_SCALE).astype(BF16)
        for h in (2 * j, 2 * j + 1):
            q_ref[h, :, 0:LANES] = qn[:, h * LANES:(h + 1) * LANES].astype(BF16)
            q_ref[h, :, LANES:QK_WIDTH] = rj

    ckv = _rms(proj(C_CKV, C_KPE), kvan_ref[...]).astype(BF16)
    kv = _dot(ckv, wukv_ref[...])
    kr = _rope_pairs(proj(C_KPE, C_GQ), cos, sin, first_half)
    kr_even = jnp.where(low_lanes, kr, 0.0).astype(BF16)
    kr_odd = jnp.where(low_lanes, 0.0, kr).astype(BF16)
    for h in range(MLA_HEADS):
        base = h * (MLA_NOPE + MLA_V)
        k_ref[h, :, 0:LANES] = kv[:, base:base + MLA_NOPE].astype(BF16)
        k_ref[h, :, LANES:QK_WIDTH] = kr_even if h % 2 == 0 else kr_odd
        v_ref[h, :, 0:MLA_V] = kv[:, base + MLA_NOPE:base + MLA_NOPE + MLA_V].astype(BF16)
        v_ref[h, :, MLA_V:V_WIDTH] = jnp.ones((kv.shape[0], V_WIDTH - MLA_V), BF16)

    gq_ref[...] = (proj(C_GQ, C_GK) * (GLA_DK ** -0.5)).astype(BF16)
    gk_ref[...] = proj(C_GK, C_GV).astype(BF16)
    gv_ref[...] = proj(C_GV, C_OG).astype(BF16)
    og_ref[...] = proj(C_OG, C_LR).astype(BF16)
    pre = _dot(proj(C_LR, D_IN_PACKED).astype(BF16), wgate_ref[...]) + bgate_ref[...]
    logsig = jnp.minimum(pre, 0.0) - jnp.log1p(jnp.exp(-jnp.abs(pre)))
    gates = logsig * (1.0 / GLA_TAU)
    gf_ref[...] = gates[:, 0:GLA_HEADS * GLA_DK]
    gb_ref[...] = gates[:, GLA_HEADS * GLA_DK:]


def _inproj(x2d, cos, sin, w, tm):
    t = x2d.shape[0]
    blocks_per_seq = cos.shape[0] // tm
    hk = GLA_HEADS * GLA_DK
    hv = GLA_HEADS * GLA_DV
    row = lambda width: pl.BlockSpec((tm, width), lambda i: (i, 0))
    head_rows = lambda width: pl.BlockSpec((MLA_HEADS, tm, width), lambda i: (0, i, 0))
    tab = pl.BlockSpec((tm, LANES), lambda i: (i % blocks_per_seq, 0))
    out_shape = (
        jax.ShapeDtypeStruct((MLA_HEADS, t, QK_WIDTH), BF16),
        jax.ShapeDtypeStruct((MLA_HEADS, t, QK_WIDTH), BF16),
        jax.ShapeDtypeStruct((MLA_HEADS, t, V_WIDTH), BF16),
        jax.ShapeDtypeStruct((t, hk), BF16),
        jax.ShapeDtypeStruct((t, hk), BF16),
        jax.ShapeDtypeStruct((t, hv), BF16),
        jax.ShapeDtypeStruct((t, hk), F32),
        jax.ShapeDtypeStruct((t, hk), F32),
        jax.ShapeDtypeStruct((t, hv), BF16),
    )
    return pl.pallas_call(
        _inproj_body,
        grid=(t // tm,),
        in_specs=[row(D_MODEL), tab, tab,
                  _full_spec((1, D_MODEL)), _full_spec((D_MODEL, D_IN_PACKED)),
                  _full_spec((1, MLA_Q_RANK)), _full_spec(w['wuq'].shape),
                  _full_spec((1, MLA_KV_RANK)), _full_spec(w['wukv'].shape),
                  _full_spec(w['wgate'].shape), _full_spec(w['bgate'].shape)],
        out_specs=(head_rows(QK_WIDTH), head_rows(QK_WIDTH), head_rows(V_WIDTH),
                   row(hk), row(hk), row(hv), row(hk), row(hk), row(hv)),
        out_shape=out_shape,
        compiler_params=_cparams("parallel"),
        name="inproj",
    )(x2d, cos, sin, w['norm_mix'], w['win'], w['q_a_norm'], w['wuq'], w['kv_a_norm'], w['wukv'],
      w['wgate'], w['bgate'])


def _attn_body(q_ref, k_ref, v_ref, km_ref, vm_ref, o_ref, s_ref, acc_ref, *, tk):
    q = q_ref[...]
    n_groups = k_ref.shape[0] // (ATTN_GROUP * tk)

    def scores(j):
        return _dot_nt(q, k_ref[pl.ds(pl.multiple_of(j * tk, tk), tk), :])

    def values(j):
        return v_ref[pl.ds(pl.multiple_of(j * tk, tk), tk), :]

    def absorb(m, s, v):
        m_new = jnp.maximum(m, jnp.max(s, axis=-1, keepdims=True))
        p = jnp.exp2(s - m_new)
        acc_ref[...] = jnp.exp2(m - m_new) * acc_ref[...] + _dot(p.astype(BF16), v)
        return m_new

    s_ref[0] = scores(0)
    sm = _dot_nt(q, km_ref[...])
    sm = jnp.where(lax.broadcasted_iota(jnp.int32, sm.shape, 1) < N_META, sm, -jnp.inf)
    m = jnp.max(sm, axis=-1, keepdims=True)
    acc_ref[...] = _dot(jnp.exp2(sm - m).astype(BF16), vm_ref[...])

    def group(g, m, last):
        for i in range(ATTN_GROUP):
            j = ATTN_GROUP * g + i
            s = s_ref[i % 2]
            if not (last and i == ATTN_GROUP - 1):
                s_ref[(i + 1) % 2] = scores(j + 1)
            m = absorb(m, s, values(j))
        return m

    m = lax.fori_loop(0, n_groups - 1, lambda g, m: group(g, m, False), m)
    group(n_groups - 1, m, True)
    acc = acc_ref[...]
    o_ref[...] = (acc[:, :MLA_V] / acc[:, MLA_V:]).astype(o_ref.dtype)


def _attention(q, k, v, km, vm, bsz, seq, tq, tk):
    nq = seq // tq
    return pl.pallas_call(
        functools.partial(_attn_body, tk=tk),
        grid=(bsz, MLA_HEADS, nq),
        in_specs=[pl.BlockSpec((None, tq, QK_WIDTH), lambda b, h, i: (h, b * nq + i, 0)),
                  pl.BlockSpec((None, seq, QK_WIDTH), lambda b, h, i: (h, b, 0)),
                  pl.BlockSpec((None, seq, V_WIDTH), lambda b, h, i: (h, b, 0)),
                  pl.BlockSpec((None, LANES, QK_WIDTH), lambda b, h, i: (h, 0, 0)),
                  pl.BlockSpec((None, LANES, V_WIDTH), lambda b, h, i: (h, 0, 0))],
        out_specs=pl.BlockSpec((tq, MLA_V), lambda b, h, i: (b * nq + i, h)),
        out_shape=jax.ShapeDtypeStruct((bsz * seq, MLA_HEADS * MLA_V), BF16),
        scratch_shapes=[pltpu.VMEM((2, tq, tk), F32), pltpu.VMEM((tq, V_WIDTH), F32)],
        compiler_params=_cparams("parallel", "parallel", "arbitrary"),
        name="mla_attention",
    )(q, k, v, km, vm)


def _split3(x):
    hi = x.astype(BF16)
    r1 = x - hi.astype(F32)
    mid = r1.astype(BF16)
    lo = (r1 - mid.astype(F32)).astype(BF16)
    return hi, mid, lo


def _gla_log_decay(g, tri):
    g_hi, g_mid, g_lo = _split3(g)
    return _dot(tri, g_hi) + _dot(tri, g_mid) + _dot(tri, g_lo)


def _head_rows(x):
    even = lax.broadcasted_iota(jnp.int32, x.shape, 1) < GLA_DK
    return jnp.concatenate([jnp.where(even, x, 0.0), jnp.where(even, 0.0, x)], axis=0).astype(BF16)


def _gla_operands(q, k, b, mid, last):
    b_last = b[last:last + 1, :]
    ke = (k * jnp.exp(b_last - b)).astype(BF16)
    if q is None:
        return jnp.exp(b_last), ke, None, None, None
    b_mid = b[mid:mid + 1, :]
    ks = (k * jnp.exp(b_mid - b)).astype(BF16)
    return jnp.exp(b_last), ke, ks, _head_rows(q * jnp.exp(b - b_mid)), _head_rows(q * jnp.exp(b))


def _gla_state_update(v_even, v_odd, ke):
    return jnp.where(lax.broadcasted_iota(jnp.int32, (GLA_DV, LANES), 1) < GLA_DK,
                     _dot_tn(v_even, ke), _dot_tn(v_odd, ke))


def _gla_intra(v_even, v_odd, ks, qs2, keep):
    scores = jnp.where(keep, _dot_nt(qs2, ks), 0.0).astype(BF16)
    return _dot(scores[:GLA_CHUNK], v_even), _dot(scores[GLA_CHUNK:], v_odd)


def _gla_body(qf_ref, kf_ref, vf_ref, gf_ref, qb_ref, kb_ref, vb_ref, gb_ref, mk_ref, mv_ref, mg_ref,
              of_ref, ob_ref, state_ref):
    n_chunks = qf_ref.shape[0] // GLA_CHUNK
    n_pairs = GLA_HEADS // 2
    r = lax.broadcasted_iota(jnp.int32, (GLA_CHUNK, GLA_CHUNK), 0)
    c = lax.broadcasted_iota(jnp.int32, (GLA_CHUNK, GLA_CHUNK), 1)
    tri_f = jnp.where(c <= r, 1.0, 0.0).astype(BF16)
    tri_b = jnp.where(c >= r, 1.0, 0.0).astype(BF16)
    r2 = lax.broadcasted_iota(jnp.int32, (2 * GLA_CHUNK, GLA_CHUNK), 0) & (GLA_CHUNK - 1)
    c2 = lax.broadcasted_iota(jnp.int32, (2 * GLA_CHUNK, GLA_CHUNK), 1)
    keep_f = c2 <= r2
    keep_b = c2 >= r2
    mid_f, last_f = GLA_CHUNK // 2 - 1, GLA_CHUNK - 1
    mid_b, last_b = GLA_CHUNK // 2, 0

    def pair_cols(p):
        return slice(p * LANES, (p + 1) * LANES)

    def head_cols(h):
        return slice(h * GLA_DV, (h + 1) * GLA_DV)

    @pl.when(pl.program_id(1) == 0)
    def _():
        for p in range(n_pairs):
            b = _gla_log_decay(mg_ref[:, pair_cols(p)], tri_f)
            _, ke, _, _, _ = _gla_operands(None, mk_ref[:, pair_cols(p)], b, mid_f, last_f)
            state_ref[p] = _gla_state_update(mv_ref[:, head_cols(2 * p)].astype(BF16),
                                             mv_ref[:, head_cols(2 * p + 1)].astype(BF16), ke)
            state_ref[n_pairs + p] = jnp.zeros((GLA_DV, LANES), F32)

    scans = []
    for p in range(n_pairs):
        scans.append((p, list(range(n_chunks)), qf_ref, kf_ref, vf_ref, gf_ref, of_ref, p,
                      tri_f, keep_f, mid_f, last_f))
        scans.append((p, list(reversed(range(n_chunks))), qb_ref, kb_ref, vb_ref, gb_ref, ob_ref, n_pairs + p,
                      tri_b, keep_b, mid_b, last_b))

    def rows(c):
        return slice(c * GLA_CHUNK, (c + 1) * GLA_CHUNK)

    def values(v_ref, p, c):
        return (v_ref[rows(c), head_cols(2 * p)].astype(BF16), v_ref[rows(c), head_cols(2 * p + 1)].astype(BF16))

    log_decay = [[_gla_log_decay(g_ref[rows(c), pair_cols(p)], tri) for c in order]
                 for (p, order, _, _, _, g_ref, _, _, tri, _, _, _) in scans]
    operands = [[_gla_operands(q_ref[rows(c), pair_cols(p)], k_ref[rows(c), pair_cols(p)], b, mid, last)
                 for c, b in zip(order, bs)]
                for (p, order, q_ref, k_ref, _, _, _, _, _, _, mid, last), bs in zip(scans, log_decay)]
    updates = [[_gla_state_update(*values(v_ref, p, c), ops[1]) for c, ops in zip(order, opss)]
               for (p, order, _, _, v_ref, _, _, _, _, _, _, _), opss in zip(scans, operands)]
    intra = [[_gla_intra(*values(v_ref, p, c), ops[2], ops[3], keep) for c, ops in zip(order, opss)]
             for (p, order, _, _, v_ref, _, _, _, _, keep, _, _), opss in zip(scans, operands)]
    states = []
    for (_, order, _, _, _, _, _, slot, _, _, _, _), opss, upds in zip(scans, operands, updates):
        st = state_ref[slot]
        entering = []
        for ops, upd in zip(opss, upds):
            entering.append(st.astype(BF16))
            st = st * ops[0] + upd
        state_ref[slot] = st
        states.append(entering)
    for (p, order, _, _, _, _, o_ref, _, _, _, _, _), opss, sts, locs in zip(scans, operands, states, intra):
        for c, ops, st, (o_even, o_odd) in zip(order, opss, sts, locs):
            inter = _dot_nt(ops[4], st)
            o_ref[rows(c), head_cols(2 * p)] = (o_even + inter[:GLA_CHUNK]).astype(o_ref.dtype)
            o_ref[rows(c), head_cols(2 * p + 1)] = (o_odd + inter[GLA_CHUNK:]).astype(o_ref.dtype)


def _gla(gq, gk, gv, gf, gb, mk, mv, mg, bsz, seq, tb):
    nb = seq // tb
    hk = GLA_HEADS * GLA_DK
    hv = GLA_HEADS * GLA_DV
    fwd = lambda width: pl.BlockSpec((tb, width), lambda b, j: (b * nb + j, 0))
    bwd = lambda width: pl.BlockSpec((tb, width), lambda b, j: (b * nb + nb - 1 - j, 0))
    t = bsz * seq
    return pl.pallas_call(
        _gla_body,
        grid=(bsz, nb),
        in_specs=[fwd(hk), fwd(hk), fwd(hv), fwd(hk), bwd(hk), bwd(hk), bwd(hv), bwd(hk),
                  _full_spec(mk.shape), _full_spec(mv.shape), _full_spec(mg.shape)],
        out_specs=(fwd(hv), bwd(hv)),
        out_shape=(jax.ShapeDtypeStruct((t, hv), BF16), jax.ShapeDtypeStruct((t, hv), BF16)),
        scratch_shapes=[pltpu.VMEM((2 * (GLA_HEADS // 2), GLA_DV, LANES), F32)],
        compiler_params=_cparams("parallel", "arbitrary"),
        name="gla_scan",
    )(gq, gk, gv, gf, gq, gk, gv, gb, mk, mv, mg)


def _mix_body(x_ref, a_ref, of_ref, ob_ref, og_ref, gnorm_ref, wout_ref, nffn_ref, wr_ref, br_ref,
              h1_ref, hn_ref, route_ref, ids_ref, count_ref, tri_ref):
    tm = x_ref.shape[0]

    @pl.when(pl.program_id(0) == 0)
    def _():
        r = lax.broadcasted_iota(jnp.int32, (tm, tm), 0)
        c = lax.broadcasted_iota(jnp.int32, (tm, tm), 1)
        tri_ref[...] = jnp.where(c < r, 1.0, 0.0).astype(BF16)
        count_ref[...] = jnp.zeros_like(count_ref)

    a_width = MLA_HEADS * MLA_V
    gated = []
    for h in range(GLA_HEADS):
        cols = slice(h * GLA_DV, (h + 1) * GLA_DV)
        o = of_ref[:, cols].astype(F32) + ob_ref[:, cols].astype(F32)
        og = og_ref[:, cols].astype(F32)
        silu = og / (1.0 + jnp.exp(-og))
        gated.append((_rms(o, gnorm_ref[...]) * silu).astype(BF16))
    h1 = (x_ref[...] + _dot(a_ref[...], wout_ref[0:a_width, :])
          + _dot(jnp.concatenate(gated, axis=-1), wout_ref[a_width:, :]))
    h1_ref[...] = h1
    hn = _rms(h1, nffn_ref[...])
    hn_ref[...] = _rows_to_tiles(_pack_bf16_pairs(hn))

    hn_hi = hn.astype(BF16)
    hn_lo = (hn - hn_hi.astype(F32)).astype(BF16)
    hi_terms = _dot(hn_hi, wr_ref[...])
    logits = hi_terms[:, :LANES] + hi_terms[:, LANES:] + _dot(hn_lo, wr_ref[:, :LANES]) + br_ref[...]
    lane = lax.broadcasted_iota(jnp.int32, logits.shape, 1).astype(F32)
    none = float(LANES)
    neg = -jnp.inf

    def lane_max(x):
        return jnp.max(x, axis=-1, keepdims=True)

    def lane_sum(x):
        return jnp.sum(x, axis=-1, keepdims=True)

    def first_lane(mask):
        return jnp.min(jnp.where(mask, lane, none), axis=-1, keepdims=True)

    is_group = lane < float(N_GROUPS)
    g_max = lane_max(jnp.where(is_group, logits, neg))
    g_exp = jnp.where(is_group, jnp.exp(logits - g_max), 0.0)
    g_prob = g_exp / lane_sum(g_exp)
    g_w = lane_max(g_prob)
    g_idx = first_lane(is_group & (g_prob == g_w))
    e_lo = float(ROUTER_EXPERT_LANE0) + float(EXPERTS_PER_GROUP) * g_idx
    sel = (lane >= e_lo) & (lane < e_lo + float(EXPERTS_PER_GROUP))
    e_max = lane_max(jnp.where(sel, logits, neg))
    e_exp = jnp.where(sel, jnp.exp(logits - e_max), 0.0)
    e_prob = e_exp / lane_sum(e_exp)
    p1 = lane_max(jnp.where(sel, e_prob, neg))
    i1 = first_lane(sel & (e_prob == p1))
    rest = sel & (lane != i1)
    p2 = lane_max(jnp.where(rest, e_prob, neg))
    i2 = first_lane(rest & (e_prob == p2))
    denom = p1 + p2
    chosen = jnp.where((lane == i1) | (lane == i2), 1.0, 0.0)
    rank = count_ref[...] + _dot(tri_ref[...], chosen.astype(BF16))
    count_ref[...] += jnp.sum(chosen, axis=0, keepdims=True)
    fields = (i1 - float(ROUTER_EXPERT_LANE0), i2 - float(ROUTER_EXPERT_LANE0),
              lane_sum(jnp.where(lane == i1, rank, 0.0)), lane_sum(jnp.where(lane == i2, rank, 0.0)),
              g_w * (p1 / denom), g_w * (p2 / denom))
    route = jnp.zeros_like(logits)
    for k, value in enumerate(fields):
        route = jnp.where(lane == float(k), value, route)
    route_ref[...] = route
    ids_ref[...] = jnp.transpose(route)[0:ROUTE_ID_ROWS, :].astype(jnp.int32)


ROUTE_E1, ROUTE_E2, ROUTE_RANK1, ROUTE_RANK2, ROUTE_W1, ROUTE_W2 = range(6)
ROUTE_ID_ROWS = 8


def _mix(x2d, a, o_f, o_b, og, w, tm):
    t = x2d.shape[0]
    hv = GLA_HEADS * GLA_DV
    row = lambda width: pl.BlockSpec((tm, width), lambda i: (i, 0))
    return pl.pallas_call(
        _mix_body,
        grid=(t // tm,),
        in_specs=[row(D_MODEL), row(MLA_HEADS * MLA_V), row(hv), row(hv), row(hv),
                  _full_spec((1, GLA_DV)), _full_spec(w['wout'].shape), _full_spec((1, D_MODEL)),
                  _full_spec(w['wr'].shape), _full_spec(w['br'].shape)],
        out_specs=(row(D_MODEL), pl.BlockSpec((tm, PACKED_CHUNKS, LANES), lambda i: (i, 0, 0)), row(LANES),
                   pl.BlockSpec((ROUTE_ID_ROWS, tm), lambda i: (0, i)), _full_spec((1, LANES))),
        out_shape=(jax.ShapeDtypeStruct((t, D_MODEL), F32), jax.ShapeDtypeStruct((t, PACKED_CHUNKS, LANES), jnp.uint32),
                   jax.ShapeDtypeStruct((t, LANES), F32), jax.ShapeDtypeStruct((ROUTE_ID_ROWS, t), jnp.int32),
                   jax.ShapeDtypeStruct((1, LANES), F32)),
        scratch_shapes=[pltpu.VMEM((tm, tm), BF16)],
        compiler_params=_cparams("arbitrary"),
        name="mix_router",
    )(x2d, a, o_f, o_b, og, w['gla_norm'], w['wout'], w['norm_ffn'], w['wr'], w['br'])


EXPERT_TILE = 256
ROW_TILE = 1024


SC_CORES = 2
SC_SUBCORES = 16
SC_GATHER_ROWS = 32


def _sc_gather(table, idx):
    n = idx.shape[0]
    workers = SC_CORES * SC_SUBCORES
    per_worker = n // workers
    assert n % (workers * SC_GATHER_ROWS) == 0
    mesh = plsc.VectorSubcoreMesh(core_axis_name="c", subcore_axis_name="s")

    @functools.partial(
        pl.kernel, mesh=mesh,
        out_type=jax.ShapeDtypeStruct((n,) + table.shape[1:], table.dtype),
        scratch_types=[pltpu.VMEM((SC_GATHER_ROWS,), jnp.int32),
                       pltpu.VMEM((SC_GATHER_ROWS,) + table.shape[1:], table.dtype),
                       pltpu.SemaphoreType.DMA])
    def gather(table_ref, idx_ref, out_ref, idx_buf, rows_buf, sem):
        base = (lax.axis_index("s") * SC_CORES + lax.axis_index("c")) * per_worker

        @pl.loop(0, per_worker // SC_GATHER_ROWS)
        def _(j):
            rows = pl.ds(base + j * SC_GATHER_ROWS, SC_GATHER_ROWS)
            pltpu.sync_copy(idx_ref.at[rows], idx_buf)
            pltpu.async_copy(table_ref.at[idx_buf], rows_buf, sem).wait()
            pltpu.sync_copy(rows_buf, out_ref.at[rows])

    return gather(table, idx)


def _sc_scatter(rows, idx, n_out):
    copies, n = idx.shape
    idx = idx.reshape(copies * n)
    workers = SC_CORES * SC_SUBCORES
    per_worker = n // workers
    assert n % (workers * SC_GATHER_ROWS) == 0
    mesh = plsc.VectorSubcoreMesh(core_axis_name="c", subcore_axis_name="s")

    @functools.partial(
        pl.kernel, mesh=mesh,
        out_type=jax.ShapeDtypeStruct((n_out,) + rows.shape[1:], rows.dtype),
        scratch_types=[pltpu.VMEM((SC_GATHER_ROWS,), jnp.int32),
                       pltpu.VMEM((SC_GATHER_ROWS,) + rows.shape[1:], rows.dtype),
                       pltpu.SemaphoreType.DMA])
    def scatter(rows_ref, idx_ref, out_ref, idx_buf, rows_buf, sem):
        base = (lax.axis_index("s") * SC_CORES + lax.axis_index("c")) * per_worker

        @pl.loop(0, per_worker // SC_GATHER_ROWS)
        def _(j):
            first = base + j * SC_GATHER_ROWS
            pltpu.sync_copy(rows_ref.at[pl.ds(first, SC_GATHER_ROWS)], rows_buf)
            for k in range(copies):
                pltpu.sync_copy(idx_ref.at[pl.ds(k * n + first, SC_GATHER_ROWS)], idx_buf)
                pltpu.async_copy(rows_buf, out_ref.at[idx_buf], sem).wait()

    return scatter(rows, idx)


SLOT_TILE = 2048


def _slots_body(starts_ref, ids_ref, slots_ref):
    ids = ids_ref[...]
    experts = ids[ROUTE_E1:ROUTE_E1 + 2, :]
    start = jnp.zeros_like(experts)
    for e in range(N_EXPERTS):
        start = jnp.where(experts == e, starts_ref[e], start)
    slots_ref[...] = start + ids[ROUTE_RANK1:ROUTE_RANK1 + 2, :]


def _slots(starts, ids):
    t = ids.shape[1]
    return pl.pallas_call(
        _slots_body,
        grid_spec=pltpu.PrefetchScalarGridSpec(
            num_scalar_prefetch=1,
            grid=(t // SLOT_TILE,),
            in_specs=[pl.BlockSpec((ROUTE_ID_ROWS, SLOT_TILE), lambda i, *_: (0, i))],
            out_specs=pl.BlockSpec((2, SLOT_TILE), lambda i, *_: (0, i))),
        out_shape=jax.ShapeDtypeStruct((2, t), jnp.int32),
        compiler_params=_cparams("parallel"),
        name="moe_slots",
    )(starts, ids)


def _queue_body(slots_ref, token_ref, *, n_tokens):
    i = pl.program_id(0)

    @pl.when(i == 0)
    def _():
        for base in range(0, token_ref.shape[0], n_tokens):
            def clear(j, carry):
                token_ref[base + j] = j
                return carry
            lax.fori_loop(0, min(n_tokens, token_ref.shape[0] - base), clear, 0, unroll=8)

    def place(r, carry):
        for k in range(2):
            token_ref[slots_ref[k, r]] = i * SLOT_TILE + r
        return carry

    lax.fori_loop(0, SLOT_TILE, place, 0, unroll=8)


def _queue_tokens(slots, n_rows):
    t = slots.shape[1]
    return pl.pallas_call(
        functools.partial(_queue_body, n_tokens=t),
        grid=(t // SLOT_TILE,),
        in_specs=[pl.BlockSpec((2, SLOT_TILE), lambda i: (0, i), memory_space=pltpu.SMEM)],
        out_specs=pl.BlockSpec(memory_space=pltpu.SMEM),
        out_shape=jax.ShapeDtypeStruct((n_rows,), jnp.int32),
        compiler_params=_cparams("arbitrary"),
        name="moe_queue",
    )(slots)


TILES_PER_STEP = 4


def _expert_body(tile_expert_ref, n_tiles_ref, tile_valid_ref, xs_ref, *refs):
    wg_refs = refs[0:TILES_PER_STEP]
    wu_refs = refs[TILES_PER_STEP:2 * TILES_PER_STEP]
    wd_refs = refs[2 * TILES_PER_STEP:3 * TILES_PER_STEP]
    ys_ref = refs[3 * TILES_PER_STEP]
    first = pl.program_id(0) * TILES_PER_STEP
    used = first < n_tiles_ref[0]

    @pl.when(used)
    def _():
        rows = [slice(j * EXPERT_TILE, (j + 1) * EXPERT_TILE) for j in range(TILES_PER_STEP)]
        xs = [_unpack_bf16_pairs(_tiles_to_rows(xs_ref[r])) for r in rows]
        xs = [jnp.where(lax.broadcasted_iota(jnp.int32, x.shape, 0) < tile_valid_ref[first + j], x, 0.0)
              for j, x in enumerate(xs)]
        xs = [x.astype(BF16) for x in xs]
        gates = [_dot(x, wg[...]) for x, wg in zip(xs, wg_refs)]
        ups = [_dot(x, wu[...]) for x, wu in zip(xs, wu_refs)]
        acts = [(g / (1.0 + jnp.exp(-g))) * u for g, u in zip(gates, ups)]
        ys = [_dot(act.astype(BF16), wd[...]) for act, wd in zip(acts, wd_refs)]
        for r, y in zip(rows, ys):
            ys_ref[r] = _rows_to_tiles(_pack_bf16_pairs(y))

    @pl.when(jnp.logical_not(used))
    def _():
        ys_ref[...] = jnp.zeros_like(ys_ref)


def _experts(tile_expert, n_tiles, tile_valid, xs, w):
    n_steps = xs.shape[0] // (EXPERT_TILE * TILES_PER_STEP)

    def step(i, nt):
        return jnp.minimum(i, (nt[0] - 1) // TILES_PER_STEP)

    def expert(i, j, te, nt):
        return te[jnp.minimum(step(i, nt) * TILES_PER_STEP + j, nt[0] - 1)]

    tiles = lambda index: pl.BlockSpec((EXPERT_TILE * TILES_PER_STEP, PACKED_CHUNKS, LANES), index)
    up_spec = lambda j: pl.BlockSpec((None, D_MODEL, D_EXPERT), lambda i, te, nt, tv: (expert(i, j, te, nt), 0, 0))
    down_spec = lambda j: pl.BlockSpec((None, D_EXPERT, D_MODEL), lambda i, te, nt, tv: (expert(i, j, te, nt), 0, 0))
    slots = range(TILES_PER_STEP)
    return pl.pallas_call(
        _expert_body,
        grid_spec=pltpu.PrefetchScalarGridSpec(
            num_scalar_prefetch=3,
            grid=(n_steps,),
            in_specs=([tiles(lambda i, te, nt, tv: (step(i, nt), 0, 0))]
                      + [up_spec(j) for j in slots] + [up_spec(j) for j in slots] + [down_spec(j) for j in slots]),
            out_specs=tiles(lambda i, te, nt, tv: (i, 0, 0))),
        out_shape=jax.ShapeDtypeStruct(xs.shape, xs.dtype),
        compiler_params=_cparams("arbitrary"),
        name="moe_experts",
    )(tile_expert, n_tiles, tile_valid, xs, *([w['w_gate']] * TILES_PER_STEP), *([w['w_up']] * TILES_PER_STEP),
      *([w['w_down']] * TILES_PER_STEP))


def _combine_body(h1_ref, route_ref, nfin_ref, y1_ref, y2_ref, out_ref):
    route = route_ref[...]
    lane = lax.broadcasted_iota(jnp.int32, route.shape, 1)
    w1 = jnp.sum(jnp.where(lane == ROUTE_W1, route, 0.0), axis=-1, keepdims=True)
    w2 = jnp.sum(jnp.where(lane == ROUTE_W2, route, 0.0), axis=-1, keepdims=True)
    y1 = _unpack_bf16_pairs(_tiles_to_rows(y1_ref[...]))
    y2 = _unpack_bf16_pairs(_tiles_to_rows(y2_ref[...]))
    y = w1 * y1 + w2 * y2
    out_ref[...] = _rms(h1_ref[...] + y, nfin_ref[...])


def _combine(h1, route, y12, w):
    t = h1.shape[0]
    row = lambda width: pl.BlockSpec((ROW_TILE, width), lambda i: (i, 0))
    tiles = lambda k: pl.BlockSpec((None, ROW_TILE, PACKED_CHUNKS, LANES), lambda i: (k, i, 0, 0))
    return pl.pallas_call(
        _combine_body,
        grid=(t // ROW_TILE,),
        in_specs=[row(D_MODEL), row(LANES), _full_spec((1, D_MODEL)), tiles(0), tiles(1)],
        out_specs=row(D_MODEL),
        out_shape=jax.ShapeDtypeStruct((t, D_MODEL), F32),
        compiler_params=_cparams("parallel"),
        name="moe_combine",
    )(h1, route, w['norm_final'], y12, y12)


def _moe(hn, route, ids, counts, h1, w):
    t = hn.shape[0]
    n_rows = 2 * t + N_EXPERTS * EXPERT_TILE
    n_grid_tiles = n_rows // EXPERT_TILE
    count = counts[0, ROUTER_EXPERT_LANE0:ROUTER_EXPERT_LANE0 + N_EXPERTS].astype(jnp.int32)
    padded = jnp.maximum((count + EXPERT_TILE - 1) // EXPERT_TILE, 1) * EXPERT_TILE
    ends = jnp.sum(jnp.where(jnp.arange(N_EXPERTS)[:, None] <= jnp.arange(N_EXPERTS)[None, :], padded[:, None], 0),
                   axis=0)
    starts = ends - padded
    n_tiles = (ends[-1:] // EXPERT_TILE)
    tile_rows = jnp.arange(n_grid_tiles, dtype=jnp.int32) * EXPERT_TILE
    tile_expert = jnp.minimum(jnp.sum((ends[None, :] <= tile_rows[:, None]).astype(jnp.int32), axis=1), N_EXPERTS - 1)
    tile_valid = jnp.clip(jnp.sum(jnp.where(jnp.arange(N_EXPERTS)[None, :] == tile_expert[:, None],
                                            (starts + count)[None, :], 0), axis=1) - tile_rows, 0, EXPERT_TILE)
    slots = _slots(starts, ids)
    ys = _experts(tile_expert, n_tiles, tile_valid, _sc_scatter(hn, slots, n_rows), w)
    y12 = _sc_gather(ys, slots.reshape(2 * t)).reshape(2, t, PACKED_CHUNKS, LANES)
    return _combine(h1, route, y12, w)


def _rope_tables(positions):
    inv_freq = 1.0 / (ROPE_THETA ** (jnp.arange(0, MLA_ROPE, 2, dtype=F32) / MLA_ROPE))
    ang = positions.astype(F32)[:, None] * inv_freq[None, :]
    cos, sin = jnp.cos(ang), jnp.sin(ang)
    reps = LANES // MLA_ROPE
    return jnp.tile(jnp.concatenate([cos, cos], axis=-1), (1, reps)), jnp.tile(jnp.concatenate([-sin, sin], axis=-1), (1, reps))


def _pack_weights(norm_mix, w_in, q_a_norm, w_uq, kv_a_norm, w_ukv, w_gate_fwd, b_gate_fwd, w_gate_bwd, b_gate_bwd,
                  gla_norm, w_out, norm_ffn, w_router_group, b_router_group, w_router_expert, b_router_expert,
                  w_expert_gate, w_expert_up, w_expert_down, norm_final):
    l = 0
    hk = GLA_HEADS * GLA_DK
    hv = GLA_HEADS * GLA_DV
    c_q, c_kv, k_pe, gq, gk, gv, lr_f, lr_b, og = jnp.split(
        w_in[l], np.cumsum([MLA_Q_RANK, MLA_KV_RANK, MLA_ROPE, hk, hk, hv, GLA_GATE_RANK, GLA_GATE_RANK])[:].tolist(),
        axis=-1)
    lr_pad = jnp.zeros((D_MODEL, LANES - 2 * GLA_GATE_RANK), F32)
    win = jnp.concatenate([c_q, c_kv, k_pe, k_pe, gq, gk, gv, og, lr_f, lr_b, lr_pad], axis=-1).astype(BF16)
    wuq = w_uq[l].reshape(MLA_Q_RANK, MLA_HEADS, MLA_NOPE + MLA_ROPE)
    wuq = jnp.concatenate([wuq[:, :, :MLA_NOPE].reshape(MLA_Q_RANK, -1), wuq[:, :, MLA_NOPE:].reshape(MLA_Q_RANK, -1)],
                          axis=-1).astype(BF16)
    wgate = jnp.zeros((LANES, 2 * hk), F32)
    wgate = wgate.at[0:GLA_GATE_RANK, 0:hk].set(w_gate_fwd[l])
    wgate = wgate.at[GLA_GATE_RANK:2 * GLA_GATE_RANK, hk:].set(w_gate_bwd[l])
    wr = jnp.zeros((D_MODEL, LANES), F32)
    wr = wr.at[:, ROUTER_GROUP_LANE0:ROUTER_GROUP_LANE0 + N_GROUPS].set(w_router_group[l])
    wr = wr.at[:, ROUTER_EXPERT_LANE0:ROUTER_EXPERT_LANE0 + N_EXPERTS].set(w_router_expert[l])
    wr_hi = wr.astype(BF16)
    br = jnp.zeros((1, LANES), F32)
    br = br.at[0, ROUTER_GROUP_LANE0:ROUTER_GROUP_LANE0 + N_GROUPS].set(b_router_group[l])
    br = br.at[0, ROUTER_EXPERT_LANE0:ROUTER_EXPERT_LANE0 + N_EXPERTS].set(b_router_expert[l])
    return {
        'norm_mix': norm_mix[l][None], 'win': win, 'q_a_norm': q_a_norm[l][None], 'wuq': wuq,
        'kv_a_norm': kv_a_norm[l][None], 'wukv': w_ukv[l].astype(BF16),
        'wgate': wgate.astype(BF16), 'bgate': jnp.concatenate([b_gate_fwd[l], b_gate_bwd[l]])[None],
        'gla_norm': gla_norm[l][None], 'wout': w_out[l].astype(BF16), 'norm_ffn': norm_ffn[l][None],
        'wr': jnp.concatenate([wr_hi, (wr - wr_hi.astype(F32)).astype(BF16)], axis=-1), 'br': br,
        'w_gate': w_expert_gate[l].reshape(N_EXPERTS, D_MODEL, D_EXPERT).astype(BF16),
        'w_up': w_expert_up[l].reshape(N_EXPERTS, D_MODEL, D_EXPERT).astype(BF16),
        'w_down': w_expert_down[l].reshape(N_EXPERTS, D_EXPERT, D_MODEL).astype(BF16),
        'norm_final': norm_final[None],
    }


def _meta_streams(meta_tokens, w):
    cos, sin = _rope_tables(jnp.arange(N_META))
    _, k, v, _, gk, gv, gf, _, _ = _inproj(meta_tokens, cos, sin, w, N_META)
    pad_keys = ((0, 0), (0, LANES - N_META), (0, 0))
    front = ((GLA_CHUNK - N_META, 0), (0, 0))
    return (jnp.pad(k, pad_keys), jnp.pad(v, pad_keys), jnp.pad(gk, front), jnp.pad(gv, front), jnp.pad(gf, front))


def _token_mixers(x, meta, w, tm, tq, tk, tb, tmix):
    bsz, seq, _ = x.shape
    km, vm, mk, mv, mg = meta
    x2d = x.reshape(bsz * seq, D_MODEL)
    cos, sin = _rope_tables(N_META + jnp.arange(seq))
    q, k, v, gq, gk, gv, gf, gb, og = _inproj(x2d, cos, sin, w, tm)
    a = _attention(q, k, v, km, vm, bsz, seq, tq, tk)
    o_f, o_b = _gla(gq, gk, gv, gf, gb, mk, mv, mg, bsz, seq, tb)
    return _mix(x2d, a, o_f, o_b, og, w, tmix)


def kernel(x_prompt, x_sample, meta_tokens, norm_mix, w_in, q_a_norm, w_uq, kv_a_norm, w_ukv, w_gate_fwd, b_gate_fwd, w_gate_bwd, b_gate_bwd, gla_norm, w_out, norm_ffn, w_router_group, b_router_group, w_router_expert, b_router_expert, w_expert_gate, w_expert_up, w_expert_down, norm_final):
    w = _pack_weights(norm_mix, w_in, q_a_norm, w_uq, kv_a_norm, w_ukv, w_gate_fwd, b_gate_fwd, w_gate_bwd,
                      b_gate_bwd, gla_norm, w_out, norm_ffn, w_router_group, b_router_group, w_router_expert,
                      b_router_expert, w_expert_gate, w_expert_up, w_expert_down, norm_final)
    meta = _meta_streams(meta_tokens, w)
    outs = []
    for x in (x_prompt, x_sample):
        h1, hn, route, ids, counts = _token_mixers(x, meta, w, tm=512, tq=1024, tk=512, tb=512, tmix=1024)
        outs.append(_moe(hn, route, ids, counts, h1, w).reshape(x.shape))
    return tuple(outs)
```

```python
import functools

import numpy as np
import jax
import jax.numpy as jnp
from jax import lax
from jax.experimental import pallas as pl
from jax.experimental.pallas import tpu as pltpu
from jax.experimental.pallas import tpu_sc as plsc

F32 = jnp.float32
BF16 = jnp.bfloat16

D_MODEL = 1024
N_META = 16
MLA_HEADS = 4
MLA_Q_RANK = 384
MLA_KV_RANK = 256
MLA_NOPE = 128
MLA_ROPE = 64
MLA_V = 128
ROPE_THETA = 10000.0
GLA_HEADS = 4
GLA_DK = 64
GLA_DV = 128
GLA_GATE_RANK = 16
GLA_TAU = 16.0
GLA_CHUNK = 64
N_GROUPS = 4
EXPERTS_PER_GROUP = 8
N_EXPERTS = N_GROUPS * EXPERTS_PER_GROUP
D_EXPERT = 256
EPS = 1e-6

LANES = 128
V7X_VMEM_BYTES = 64 * 1024 * 1024
VMEM_LIMIT = V7X_VMEM_BYTES * 7 // 8

ATTN_SCALE = (MLA_NOPE + MLA_ROPE) ** -0.5 * float(np.log2(np.e))
QK_WIDTH = 2 * LANES
V_WIDTH = 2 * LANES
ATTN_GROUP = 8

C_CQ = 0
C_CKV = C_CQ + MLA_Q_RANK
C_KPE = C_CKV + MLA_KV_RANK
C_GQ = C_KPE + LANES
C_GK = C_GQ + GLA_HEADS * GLA_DK
C_GV = C_GK + GLA_HEADS * GLA_DK
C_OG = C_GV + GLA_HEADS * GLA_DV
C_LR = C_OG + GLA_HEADS * GLA_DV
D_IN_PACKED = C_LR + LANES

ROUTER_GROUP_LANE0 = 0
ROUTER_EXPERT_LANE0 = N_GROUPS


def _cparams(*semantics):
    return pltpu.CompilerParams(dimension_semantics=semantics, vmem_limit_bytes=VMEM_LIMIT)


def _rms(x, g):
    return x * lax.rsqrt(jnp.mean(x * x, axis=-1, keepdims=True) + EPS) * g


def _dot(a, b):
    return jnp.dot(a, b, preferred_element_type=F32)


def _dot_nt(a, b):
    return lax.dot_general(a, b, (((1,), (1,)), ((), ())), preferred_element_type=F32)


def _dot_tn(a, b):
    return lax.dot_general(a, b, (((0,), (0,)), ((), ())), preferred_element_type=F32)


def _full_spec(shape):
    return pl.BlockSpec(shape, lambda *_: (0,) * len(shape))


ROW_CHUNKS = D_MODEL // LANES


def _rows_to_tiles(x):
    chunks = jnp.stack([x[:, s * LANES:(s + 1) * LANES] for s in range(x.shape[1] // LANES)], axis=0)
    return pltpu.einshape("smd->msd", chunks)


def _tiles_to_rows(x):
    chunks = pltpu.einshape("msd->smd", x)
    return jnp.concatenate([chunks[s] for s in range(x.shape[1])], axis=-1)


PACKED_CHUNKS = ROW_CHUNKS // 2


def _pack_bf16_pairs(x):
    half = x.shape[1] // 2
    bits = lambda v: lax.bitcast_convert_type(v.astype(BF16).astype(F32), jnp.uint32)
    return (bits(x[:, half:]) & jnp.uint32(0xFFFF0000)) | (bits(x[:, :half]) >> 16)


def _unpack_bf16_pairs(w):
    lo = lax.bitcast_convert_type(w << 16, F32)
    hi = lax.bitcast_convert_type(w & jnp.uint32(0xFFFF0000), F32)
    return jnp.concatenate([lo, hi], axis=-1)


def _rope_pairs(x, cos, sin_signed, first_half):
    swapped = jnp.where(first_half, pltpu.roll(x, LANES - MLA_ROPE // 2, 1), pltpu.roll(x, MLA_ROPE // 2, 1))
    return x * cos + swapped * sin_signed


def _inproj_body(x_ref, cos_ref, sin_ref, nmix_ref, win_ref, qan_ref, wuq_ref, kvan_ref, wukv_ref,
                 wgate_ref, bgate_ref,
                 q_ref, k_ref, v_ref, gq_ref, gk_ref, gv_ref, gf_ref, gb_ref, og_ref):
    hn = _rms(x_ref[...], nmix_ref[...]).astype(BF16)

    def proj(lo, hi):
        return _dot(hn, win_ref[:, lo:hi])

    cos = cos_ref[...]
    sin = sin_ref[...]
    lane = lax.broadcasted_iota(jnp.int32, cos.shape, 1)
    first_half = (lane & (MLA_ROPE - 1)) < MLA_ROPE // 2
    low_lanes = lane < MLA_ROPE

    cq = _rms(proj(C_CQ, C_CKV), qan_ref[...]).astype(BF16)
    qn = _dot(cq, wuq_ref[:, 0:MLA_HEADS * MLA_NOPE]) * ATTN_SCALE
    qr = _dot(cq, wuq_ref[:, MLA_HEADS * MLA_NOPE:])
    for j in range(MLA_HEADS // 2):
        rj = (_rope_pairs(qr[:, j * LANES:(j + 1) * LANES], cos, sin, first_half) * ATTN_SCALE).astype(BF16)
        for h in (2 * j, 2 * j + 1):
            q_ref[h, :, 0:LANES] = qn[:, h * LANES:(h + 1) * LANES].astype(BF16)
            q_ref[h, :, LANES:QK_WIDTH] = rj

    ckv = _rms(proj(C_CKV, C_KPE), kvan_ref[...]).astype(BF16)
    kv = _dot(ckv, wukv_ref[...])
    kr = _rope_pairs(proj(C_KPE, C_GQ), cos, sin, first_half)
    kr_even = jnp.where(low_lanes, kr, 0.0).astype(BF16)
    kr_odd = jnp.where(low_lanes, 0.0, kr).astype(BF16)
    for h in range(MLA_HEADS):
        base = h * (MLA_NOPE + MLA_V)
        k_ref[h, :, 0:LANES] = kv[:, base:base + MLA_NOPE].astype(BF16)
        k_ref[h, :, LANES:QK_WIDTH] = kr_even if h % 2 == 0 else kr_odd
        v_ref[h, :, 0:MLA_V] = kv[:, base + MLA_NOPE:base + MLA_NOPE + MLA_V].astype(BF16)
        v_ref[h, :, MLA_V:V_WIDTH] = jnp.ones((kv.shape[0], V_WIDTH - MLA_V), BF16)

    gq_ref[...] = (proj(C_GQ, C_GK) * (GLA_DK ** -0.5)).astype(BF16)
    gk_ref[...] = proj(C_GK, C_GV).astype(BF16)
    gv_ref[...] = proj(C_GV, C_OG).astype(BF16)
    og_ref[...] = proj(C_OG, C_LR).astype(BF16)
    pre = _dot(proj(C_LR, D_IN_PACKED).astype(BF16), wgate_ref[...]) + bgate_ref[...]
    logsig = jnp.minimum(pre, 0.0) - jnp.log1p(jnp.exp(-jnp.abs(pre)))
    gates = logsig * (1.0 / GLA_TAU)
    gf_ref[...] = gates[:, 0:GLA_HEADS * GLA_DK]
    gb_ref[...] = gates[:, GLA_HEADS * GLA_DK:]


def _inproj(x2d, cos, sin, w, tm):
    t = x2d.shape[0]
    blocks_per_seq = cos.shape[0] // tm
    hk = GLA_HEADS * GLA_DK
    hv = GLA_HEADS * GLA_DV
    row = lambda width: pl.BlockSpec((tm, width), lambda i: (i, 0))
    head_rows = lambda width: pl.BlockSpec((MLA_HEADS, tm, width), lambda i: (0, i, 0))
    tab = pl.BlockSpec((tm, LANES), lambda i: (i % blocks_per_seq, 0))
    out_shape = (
        jax.ShapeDtypeStruct((MLA_HEADS, t, QK_WIDTH), BF16),
        jax.ShapeDtypeStruct((MLA_HEADS, t, QK_WIDTH), BF16),
        jax.ShapeDtypeStruct((MLA_HEADS, t, V_WIDTH), BF16),
        jax.ShapeDtypeStruct((t, hk), BF16),
        jax.ShapeDtypeStruct((t, hk), BF16),
        jax.ShapeDtypeStruct((t, hv), BF16),
        jax.ShapeDtypeStruct((t, hk), F32),
        jax.ShapeDtypeStruct((t, hk), F32),
        jax.ShapeDtypeStruct((t, hv), BF16),
    )
    return pl.pallas_call(
        _inproj_body,
        grid=(t // tm,),
        in_specs=[row(D_MODEL), tab, tab,
                  _full_spec((1, D_MODEL)), _full_spec((D_MODEL, D_IN_PACKED)),
                  _full_spec((1, MLA_Q_RANK)), _full_spec(w['wuq'].shape),
                  _full_spec((1, MLA_KV_RANK)), _full_spec(w['wukv'].shape),
                  _full_spec(w['wgate'].shape), _full_spec(w['bgate'].shape)],
        out_specs=(head_rows(QK_WIDTH), head_rows(QK_WIDTH), head_rows(V_WIDTH),
                   row(hk), row(hk), row(hv), row(hk), row(hk), row(hv)),
        out_shape=out_shape,
        compiler_params=_cparams("parallel"),
        name="inproj",
    )(x2d, cos, sin, w['norm_mix'], w['win'], w['q_a_norm'], w['wuq'], w['kv_a_norm'], w['wukv'],
      w['wgate'], w['bgate'])


def _attn_body(q_ref, k_ref, v_ref, km_ref, vm_ref, o_ref, s_ref, acc_ref, *, tk):
    q = q_ref[...]
    n_blocks = k_ref.shape[0] // tk
    group_size = min(ATTN_GROUP, n_blocks)
    assert n_blocks % group_size == 0 and group_size % 2 == 0
    n_groups = n_blocks // group_size

    def scores(j):
        return _dot_nt(q, k_ref[pl.ds(pl.multiple_of(j * tk, tk), tk), :])

    def values(j):
        return v_ref[pl.ds(pl.multiple_of(j * tk, tk), tk), :]

    def absorb(m, s, v):
        m_new = jnp.maximum(m, jnp.max(s, axis=-1, keepdims=True))
        p = jnp.exp2(s - m_new)
        acc_ref[...] = jnp.exp2(m - m_new) * acc_ref[...] + _dot(p.astype(BF16), v)
        return m_new

    s_ref[0] = scores(0)
    sm = _dot_nt(q, km_ref[...])
    sm = jnp.where(lax.broadcasted_iota(jnp.int32, sm.shape, 1) < N_META, sm, -jnp.inf)
    m = jnp.max(sm, axis=-1, keepdims=True)
    acc_ref[...] = _dot(jnp.exp2(sm - m).astype(BF16), vm_ref[...])

    def group(g, m, last):
        for i in range(group_size):
            j = group_size * g + i
            s = s_ref[i % 2]
            if not (last and i == group_size - 1):
                s_ref[(i + 1) % 2] = scores(j + 1)
            m = absorb(m, s, values(j))
        return m

    m = lax.fori_loop(0, n_groups - 1, lambda g, m: group(g, m, False), m)
    group(n_groups - 1, m, True)
    acc = acc_ref[...]
    o_ref[...] = (acc[:, :MLA_V] / acc[:, MLA_V:]).astype(o_ref.dtype)


def _attention(q, k, v, km, vm, bsz, seq, tq, tk):
    nq = seq // tq
    return pl.pallas_call(
        functools.partial(_attn_body, tk=tk),
        grid=(bsz, MLA_HEADS, nq),
        in_specs=[pl.BlockSpec((None, tq, QK_WIDTH), lambda b, h, i: (h, b * nq + i, 0)),
                  pl.BlockSpec((None, seq, QK_WIDTH), lambda b, h, i: (h, b, 0)),
                  pl.BlockSpec((None, seq, V_WIDTH), lambda b, h, i: (h, b, 0)),
                  pl.BlockSpec((None, LANES, QK_WIDTH), lambda b, h, i: (h, 0, 0)),
                  pl.BlockSpec((None, LANES, V_WIDTH), lambda b, h, i: (h, 0, 0))],
        out_specs=pl.BlockSpec((tq, MLA_V), lambda b, h, i: (b * nq + i, h)),
        out_shape=jax.ShapeDtypeStruct((bsz * seq, MLA_HEADS * MLA_V), BF16),
        scratch_shapes=[pltpu.VMEM((2, tq, tk), F32), pltpu.VMEM((tq, V_WIDTH), F32)],
        compiler_params=_cparams("parallel", "parallel", "arbitrary"),
        name="mla_attention",
    )(q, k, v, km, vm)


def _split3(x):
    hi = x.astype(BF16)
    r1 = x - hi.astype(F32)
    mid = r1.astype(BF16)
    lo = (r1 - mid.astype(F32)).astype(BF16)
    return hi, mid, lo


def _gla_log_decay(g, tri):
    g_hi, g_mid, g_lo = _split3(g)
    return _dot(tri, g_hi) + _dot(tri, g_mid) + _dot(tri, g_lo)


def _head_rows(x):
    even = lax.broadcasted_iota(jnp.int32, x.shape, 1) < GLA_DK
    return jnp.concatenate([jnp.where(even, x, 0.0), jnp.where(even, 0.0, x)], axis=0).astype(BF16)


def _gla_operands(q, k, b, mid, last):
    b_last = b[last:last + 1, :]
    ke = (k * jnp.exp(b_last - b)).astype(BF16)
    if q is None:
        return jnp.exp(b_last), ke, None, None, None
    b_mid = b[mid:mid + 1, :]
    ks = (k * jnp.exp(b_mid - b)).astype(BF16)
    return jnp.exp(b_last), ke, ks, _head_rows(q * jnp.exp(b - b_mid)), _head_rows(q * jnp.exp(b))


def _gla_state_update(v_even, v_odd, ke):
    return jnp.where(lax.broadcasted_iota(jnp.int32, (GLA_DV, LANES), 1) < GLA_DK,
                     _dot_tn(v_even, ke), _dot_tn(v_odd, ke))


def _gla_intra(v_even, v_odd, ks, qs2, keep):
    scores = jnp.where(keep, _dot_nt(qs2, ks), 0.0).astype(BF16)
    return _dot(scores[:GLA_CHUNK], v_even), _dot(scores[GLA_CHUNK:], v_odd)


def _gla_body(qf_ref, kf_ref, vf_ref, gf_ref, qb_ref, kb_ref, vb_ref, gb_ref, mk_ref, mv_ref, mg_ref,
              of_ref, ob_ref, state_ref):
    n_chunks = qf_ref.shape[0] // GLA_CHUNK
    n_pairs = GLA_HEADS // 2
    r = lax.broadcasted_iota(jnp.int32, (GLA_CHUNK, GLA_CHUNK), 0)
    c = lax.broadcasted_iota(jnp.int32, (GLA_CHUNK, GLA_CHUNK), 1)
    tri_f = jnp.where(c <= r, 1.0, 0.0).astype(BF16)
    tri_b = jnp.where(c >= r, 1.0, 0.0).astype(BF16)
    r2 = lax.broadcasted_iota(jnp.int32, (2 * GLA_CHUNK, GLA_CHUNK), 0) & (GLA_CHUNK - 1)
    c2 = lax.broadcasted_iota(jnp.int32, (2 * GLA_CHUNK, GLA_CHUNK), 1)
    keep_f = c2 <= r2
    keep_b = c2 >= r2
    mid_f, last_f = GLA_CHUNK // 2 - 1, GLA_CHUNK - 1
    mid_b, last_b = GLA_CHUNK // 2, 0

    def pair_cols(p):
        return slice(p * LANES, (p + 1) * LANES)

    def head_cols(h):
        return slice(h * GLA_DV, (h + 1) * GLA_DV)

    @pl.when(pl.program_id(1) == 0)
    def _():
        for p in range(n_pairs):
            b = _gla_log_decay(mg_ref[:, pair_cols(p)], tri_f)
            _, ke, _, _, _ = _gla_operands(None, mk_ref[:, pair_cols(p)], b, mid_f, last_f)
            state_ref[p] = _gla_state_update(mv_ref[:, head_cols(2 * p)].astype(BF16),
                                             mv_ref[:, head_cols(2 * p + 1)].astype(BF16), ke)
            state_ref[n_pairs + p] = jnp.zeros((GLA_DV, LANES), F32)

    scans = []
    for p in range(n_pairs):
        scans.append((p, list(range(n_chunks)), qf_ref, kf_ref, vf_ref, gf_ref, of_ref, p,
                      tri_f, keep_f, mid_f, last_f))
        scans.append((p, list(reversed(range(n_chunks))), qb_ref, kb_ref, vb_ref, gb_ref, ob_ref, n_pairs + p,
                      tri_b, keep_b, mid_b, last_b))

    def rows(c):
        return slice(c * GLA_CHUNK, (c + 1) * GLA_CHUNK)

    def values(v_ref, p, c):
        return (v_ref[rows(c), head_cols(2 * p)].astype(BF16), v_ref[rows(c), head_cols(2 * p + 1)].astype(BF16))

    log_decay = [[_gla_log_decay(g_ref[rows(c), pair_cols(p)], tri) for c in order]
                 for (p, order, _, _, _, g_ref, _, _, tri, _, _, _) in scans]
    operands = [[_gla_operands(q_ref[rows(c), pair_cols(p)], k_ref[rows(c), pair_cols(p)], b, mid, last)
                 for c, b in zip(order, bs)]
                for (p, order, q_ref, k_ref, _, _, _, _, _, _, mid, last), bs in zip(scans, log_decay)]
    updates = [[_gla_state_update(*values(v_ref, p, c), ops[1]) for c, ops in zip(order, opss)]
               for (p, order, _, _, v_ref, _, _, _, _, _, _, _), opss in zip(scans, operands)]
    intra = [[_gla_intra(*values(v_ref, p, c), ops[2], ops[3], keep) for c, ops in zip(order, opss)]
             for (p, order, _, _, v_ref, _, _, _, _, keep, _, _), opss in zip(scans, operands)]
    states = []
    for (_, order, _, _, _, _, _, slot, _, _, _, _), opss, upds in zip(scans, operands, updates):
        st = state_ref[slot]
        entering = []
        for ops, upd in zip(opss, upds):
            entering.append(st.astype(BF16))
            st = st * ops[0] + upd
        state_ref[slot] = st
        states.append(entering)
    for (p, order, _, _, _, _, o_ref, _, _, _, _, _), opss, sts, locs in zip(scans, operands, states, intra):
        for c, ops, st, (o_even, o_odd) in zip(order, opss, sts, locs):
            inter = _dot_nt(ops[4], st)
            o_ref[rows(c), head_cols(2 * p)] = (o_even + inter[:GLA_CHUNK]).astype(o_ref.dtype)
            o_ref[rows(c), head_cols(2 * p + 1)] = (o_odd + inter[GLA_CHUNK:]).astype(o_ref.dtype)


def _gla(gq, gk, gv, gf, gb, mk, mv, mg, bsz, seq, tb):
    nb = seq // tb
    hk = GLA_HEADS * GLA_DK
    hv = GLA_HEADS * GLA_DV
    fwd = lambda width: pl.BlockSpec((tb, width), lambda b, j: (b * nb + j, 0))
    bwd = lambda width: pl.BlockSpec((tb, width), lambda b, j: (b * nb + nb - 1 - j, 0))
    t = bsz * seq
    return pl.pallas_call(
        _gla_body,
        grid=(bsz, nb),
        in_specs=[fwd(hk), fwd(hk), fwd(hv), fwd(hk), bwd(hk), bwd(hk), bwd(hv), bwd(hk),
                  _full_spec(mk.shape), _full_spec(mv.shape), _full_spec(mg.shape)],
        out_specs=(fwd(hv), bwd(hv)),
        out_shape=(jax.ShapeDtypeStruct((t, hv), BF16), jax.ShapeDtypeStruct((t, hv), BF16)),
        scratch_shapes=[pltpu.VMEM((2 * (GLA_HEADS // 2), GLA_DV, LANES), F32)],
        compiler_params=_cparams("parallel", "arbitrary"),
        name="gla_scan",
    )(gq, gk, gv, gf, gq, gk, gv, gb, mk, mv, mg)


def _mix_body(x_ref, a_ref, of_ref, ob_ref, og_ref, gnorm_ref, wout_ref, nffn_ref, wr_ref, br_ref,
              h1_ref, hn_ref, route_ref, ids_ref, count_ref, tri_ref):
    tm = x_ref.shape[0]

    @pl.when(pl.program_id(0) == 0)
    def _():
        r = lax.broadcasted_iota(jnp.int32, (tm, tm), 0)
        c = lax.broadcasted_iota(jnp.int32, (tm, tm), 1)
        tri_ref[...] = jnp.where(c < r, 1.0, 0.0).astype(BF16)
        count_ref[...] = jnp.zeros_like(count_ref)

    a_width = MLA_HEADS * MLA_V
    gated = []
    for h in range(GLA_HEADS):
        cols = slice(h * GLA_DV, (h + 1) * GLA_DV)
        o = of_ref[:, cols].astype(F32) + ob_ref[:, cols].astype(F32)
        og = og_ref[:, cols].astype(F32)
        silu = og / (1.0 + jnp.exp(-og))
        gated.append((_rms(o, gnorm_ref[...]) * silu).astype(BF16))
    h1 = (x_ref[...] + _dot(a_ref[...], wout_ref[0:a_width, :])
          + _dot(jnp.concatenate(gated, axis=-1), wout_ref[a_width:, :]))
    h1_ref[...] = h1
    hn = _rms(h1, nffn_ref[...])
    hn_ref[...] = _rows_to_tiles(_pack_bf16_pairs(hn))

    hn_hi = hn.astype(BF16)
    hn_lo = (hn - hn_hi.astype(F32)).astype(BF16)
    hi_terms = _dot(hn_hi, wr_ref[...])
    logits = hi_terms[:, :LANES] + hi_terms[:, LANES:] + _dot(hn_lo, wr_ref[:, :LANES]) + br_ref[...]
    lane = lax.broadcasted_iota(jnp.int32, logits.shape, 1).astype(F32)
    none = float(LANES)
    neg = -jnp.inf

    def lane_max(x):
        return jnp.max(x, axis=-1, keepdims=True)

    def lane_sum(x):
        return jnp.sum(x, axis=-1, keepdims=True)

    def first_lane(mask):
        return jnp.min(jnp.where(mask, lane, none), axis=-1, keepdims=True)

    is_group = lane < float(N_GROUPS)
    g_max = lane_max(jnp.where(is_group, logits, neg))
    g_exp = jnp.where(is_group, jnp.exp(logits - g_max), 0.0)
    g_prob = g_exp / lane_sum(g_exp)
    g_w = lane_max(g_prob)
    g_idx = first_lane(is_group & (g_prob == g_w))
    e_lo = float(ROUTER_EXPERT_LANE0) + float(EXPERTS_PER_GROUP) * g_idx
    sel = (lane >= e_lo) & (lane < e_lo + float(EXPERTS_PER_GROUP))
    e_max = lane_max(jnp.where(sel, logits, neg))
    e_exp = jnp.where(sel, jnp.exp(logits - e_max), 0.0)
    e_prob = e_exp / lane_sum(e_exp)
    p1 = lane_max(jnp.where(sel, e_prob, neg))
    i1 = first_lane(sel & (e_prob == p1))
    rest = sel & (lane != i1)
    p2 = lane_max(jnp.where(rest, e_prob, neg))
    i2 = first_lane(rest & (e_prob == p2))
    denom = p1 + p2
    chosen = jnp.where((lane == i1) | (lane == i2), 1.0, 0.0)
    rank = count_ref[...] + _dot(tri_ref[...], chosen.astype(BF16))
    count_ref[...] += jnp.sum(chosen, axis=0, keepdims=True)
    fields = (i1 - float(ROUTER_EXPERT_LANE0), i2 - float(ROUTER_EXPERT_LANE0),
              lane_sum(jnp.where(lane == i1, rank, 0.0)), lane_sum(jnp.where(lane == i2, rank, 0.0)),
              g_w * (p1 / denom), g_w * (p2 / denom))
    route = jnp.zeros_like(logits)
    for k, value in enumerate(fields):
        route = jnp.where(lane == float(k), value, route)
    route_ref[...] = route
    ids_ref[...] = jnp.transpose(route)[0:ROUTE_ID_ROWS, :].astype(jnp.int32)


ROUTE_E1, ROUTE_E2, ROUTE_RANK1, ROUTE_RANK2, ROUTE_W1, ROUTE_W2 = range(6)
ROUTE_ID_ROWS = 8


def _mix(x2d, a, o_f, o_b, og, w, tm):
    t = x2d.shape[0]
    hv = GLA_HEADS * GLA_DV
    row = lambda width: pl.BlockSpec((tm, width), lambda i: (i, 0))
    return pl.pallas_call(
        _mix_body,
        grid=(t // tm,),
        in_specs=[row(D_MODEL), row(MLA_HEADS * MLA_V), row(hv), row(hv), row(hv),
                  _full_spec((1, GLA_DV)), _full_spec(w['wout'].shape), _full_spec((1, D_MODEL)),
                  _full_spec(w['wr'].shape), _full_spec(w['br'].shape)],
        out_specs=(row(D_MODEL), pl.BlockSpec((tm, PACKED_CHUNKS, LANES), lambda i: (i, 0, 0)), row(LANES),
                   pl.BlockSpec((ROUTE_ID_ROWS, tm), lambda i: (0, i)), _full_spec((1, LANES))),
        out_shape=(jax.ShapeDtypeStruct((t, D_MODEL), F32), jax.ShapeDtypeStruct((t, PACKED_CHUNKS, LANES), jnp.uint32),
                   jax.ShapeDtypeStruct((t, LANES), F32), jax.ShapeDtypeStruct((ROUTE_ID_ROWS, t), jnp.int32),
                   jax.ShapeDtypeStruct((1, LANES), F32)),
        scratch_shapes=[pltpu.VMEM((tm, tm), BF16)],
        compiler_params=_cparams("arbitrary"),
        name="mix_router",
    )(x2d, a, o_f, o_b, og, w['gla_norm'], w['wout'], w['norm_ffn'], w['wr'], w['br'])


EXPERT_TILE = 256
ROW_TILE = 1024


SC_CORES = 2
SC_SUBCORES = 16
SC_GATHER_ROWS = 32


def _sc_gather(table, idx):
    n = idx.shape[0]
    workers = SC_CORES * SC_SUBCORES
    per_worker = n // workers
    assert n % (workers * SC_GATHER_ROWS) == 0
    mesh = plsc.VectorSubcoreMesh(core_axis_name="c", subcore_axis_name="s")

    @functools.partial(
        pl.kernel, mesh=mesh,
        out_type=jax.ShapeDtypeStruct((n,) + table.shape[1:], table.dtype),
        scratch_types=[pltpu.VMEM((SC_GATHER_ROWS,), jnp.int32),
                       pltpu.VMEM((SC_GATHER_ROWS,) + table.shape[1:], table.dtype),
                       pltpu.SemaphoreType.DMA])
    def gather(table_ref, idx_ref, out_ref, idx_buf, rows_buf, sem):
        base = (lax.axis_index("s") * SC_CORES + lax.axis_index("c")) * per_worker

        @pl.loop(0, per_worker // SC_GATHER_ROWS)
        def _(j):
            rows = pl.ds(base + j * SC_GATHER_ROWS, SC_GATHER_ROWS)
            pltpu.sync_copy(idx_ref.at[rows], idx_buf)
            pltpu.async_copy(table_ref.at[idx_buf], rows_buf, sem).wait()
            pltpu.sync_copy(rows_buf, out_ref.at[rows])

    return gather(table, idx)


def _sc_scatter(rows, idx, n_out):
    copies, n = idx.shape
    idx = idx.reshape(copies * n)
    workers = SC_CORES * SC_SUBCORES
    per_worker = n // workers
    assert n % (workers * SC_GATHER_ROWS) == 0
    mesh = plsc.VectorSubcoreMesh(core_axis_name="c", subcore_axis_name="s")

    @functools.partial(
        pl.kernel, mesh=mesh,
        out_type=jax.ShapeDtypeStruct((n_out,) + rows.shape[1:], rows.dtype),
        scratch_types=[pltpu.VMEM((SC_GATHER_ROWS,), jnp.int32),
                       pltpu.VMEM((SC_GATHER_ROWS,) + rows.shape[1:], rows.dtype),
                       pltpu.SemaphoreType.DMA])
    def scatter(rows_ref, idx_ref, out_ref, idx_buf, rows_buf, sem):
        base = (lax.axis_index("s") * SC_CORES + lax.axis_index("c")) * per_worker

        @pl.loop(0, per_worker // SC_GATHER_ROWS)
        def _(j):
            first = base + j * SC_GATHER_ROWS
            pltpu.sync_copy(rows_ref.at[pl.ds(first, SC_GATHER_ROWS)], rows_buf)
            for k in range(copies):
                pltpu.sync_copy(idx_ref.at[pl.ds(k * n + first, SC_GATHER_ROWS)], idx_buf)
                pltpu.async_copy(rows_buf, out_ref.at[idx_buf], sem).wait()

    return scatter(rows, idx)


SLOT_TILE = 2048


def _slots_body(starts_ref, ids_ref, slots_ref):
    ids = ids_ref[...]
    experts = ids[ROUTE_E1:ROUTE_E1 + 2, :]
    start = jnp.zeros_like(experts)
    for e in range(N_EXPERTS):
        start = jnp.where(experts == e, starts_ref[e], start)
    slots_ref[...] = start + ids[ROUTE_RANK1:ROUTE_RANK1 + 2, :]


def _slots(starts, ids):
    t = ids.shape[1]
    return pl.pallas_call(
        _slots_body,
        grid_spec=pltpu.PrefetchScalarGridSpec(
            num_scalar_prefetch=1,
            grid=(t // SLOT_TILE,),
            in_specs=[pl.BlockSpec((ROUTE_ID_ROWS, SLOT_TILE), lambda i, *_: (0, i))],
            out_specs=pl.BlockSpec((2, SLOT_TILE), lambda i, *_: (0, i))),
        out_shape=jax.ShapeDtypeStruct((2, t), jnp.int32),
        compiler_params=_cparams("parallel"),
        name="moe_slots",
    )(starts, ids)


def _queue_body(slots_ref, token_ref, *, n_tokens):
    i = pl.program_id(0)

    @pl.when(i == 0)
    def _():
        for base in range(0, token_ref.shape[0], n_tokens):
            def clear(j, carry):
                token_ref[base + j] = j
                return carry
            lax.fori_loop(0, min(n_tokens, token_ref.shape[0] - base), clear, 0, unroll=8)

    def place(r, carry):
        for k in range(2):
            token_ref[slots_ref[k, r]] = i * SLOT_TILE + r
        return carry

    lax.fori_loop(0, SLOT_TILE, place, 0, unroll=8)


def _queue_tokens(slots, n_rows):
    t = slots.shape[1]
    return pl.pallas_call(
        functools.partial(_queue_body, n_tokens=t),
        grid=(t // SLOT_TILE,),
        in_specs=[pl.BlockSpec((2, SLOT_TILE), lambda i: (0, i), memory_space=pltpu.SMEM)],
        out_specs=pl.BlockSpec(memory_space=pltpu.SMEM),
        out_shape=jax.ShapeDtypeStruct((n_rows,), jnp.int32),
        compiler_params=_cparams("arbitrary"),
        name="moe_queue",
    )(slots)


TILES_PER_STEP = 4


def _expert_body(tile_expert_ref, n_tiles_ref, tile_valid_ref, xs_ref, *refs):
    wg_refs = refs[0:TILES_PER_STEP]
    wu_refs = refs[TILES_PER_STEP:2 * TILES_PER_STEP]
    wd_refs = refs[2 * TILES_PER_STEP:3 * TILES_PER_STEP]
    ys_ref = refs[3 * TILES_PER_STEP]
    first = pl.program_id(0) * TILES_PER_STEP
    used = first < n_tiles_ref[0]

    @pl.when(used)
    def _():
        rows = [slice(j * EXPERT_TILE, (j + 1) * EXPERT_TILE) for j in range(TILES_PER_STEP)]
        xs = [_unpack_bf16_pairs(_tiles_to_rows(xs_ref[r])) for r in rows]
        xs = [jnp.where(lax.broadcasted_iota(jnp.int32, x.shape, 0) < tile_valid_ref[first + j], x, 0.0)
              for j, x in enumerate(xs)]
        xs = [x.astype(BF16) for x in xs]
        gates = [_dot(x, wg[...]) for x, wg in zip(xs, wg_refs)]
        ups = [_dot(x, wu[...]) for x, wu in zip(xs, wu_refs)]
        acts = [(g / (1.0 + jnp.exp(-g))) * u for g, u in zip(gates, ups)]
        ys = [_dot(act.astype(BF16), wd[...]) for act, wd in zip(acts, wd_refs)]
        for r, y in zip(rows, ys):
            ys_ref[r] = _rows_to_tiles(_pack_bf16_pairs(y))

    @pl.when(jnp.logical_not(used))
    def _():
        ys_ref[...] = jnp.zeros_like(ys_ref)


def _experts(tile_expert, n_tiles, tile_valid, xs, w):
    n_steps = xs.shape[0] // (EXPERT_TILE * TILES_PER_STEP)

    def step(i, nt):
        return jnp.minimum(i, (nt[0] - 1) // TILES_PER_STEP)

    def expert(i, j, te, nt):
        return te[jnp.minimum(step(i, nt) * TILES_PER_STEP + j, nt[0] - 1)]

    tiles = lambda index: pl.BlockSpec((EXPERT_TILE * TILES_PER_STEP, PACKED_CHUNKS, LANES), index)
    up_spec = lambda j: pl.BlockSpec((None, D_MODEL, D_EXPERT), lambda i, te, nt, tv: (expert(i, j, te, nt), 0, 0))
    down_spec = lambda j: pl.BlockSpec((None, D_EXPERT, D_MODEL), lambda i, te, nt, tv: (expert(i, j, te, nt), 0, 0))
    slots = range(TILES_PER_STEP)
    return pl.pallas_call(
        _expert_body,
        grid_spec=pltpu.PrefetchScalarGridSpec(
            num_scalar_prefetch=3,
            grid=(n_steps,),
            in_specs=([tiles(lambda i, te, nt, tv: (step(i, nt), 0, 0))]
                      + [up_spec(j) for j in slots] + [up_spec(j) for j in slots] + [down_spec(j) for j in slots]),
            out_specs=tiles(lambda i, te, nt, tv: (i, 0, 0))),
        out_shape=jax.ShapeDtypeStruct(xs.shape, xs.dtype),
        compiler_params=_cparams("arbitrary"),
        name="moe_experts",
    )(tile_expert, n_tiles, tile_valid, xs, *([w['w_gate']] * TILES_PER_STEP), *([w['w_up']] * TILES_PER_STEP),
      *([w['w_down']] * TILES_PER_STEP))


def _combine_body(h1_ref, route_ref, nfin_ref, y1_ref, y2_ref, out_ref):
    route = route_ref[...]
    lane = lax.broadcasted_iota(jnp.int32, route.shape, 1)
    w1 = jnp.sum(jnp.where(lane == ROUTE_W1, route, 0.0), axis=-1, keepdims=True)
    w2 = jnp.sum(jnp.where(lane == ROUTE_W2, route, 0.0), axis=-1, keepdims=True)
    y1 = _unpack_bf16_pairs(_tiles_to_rows(y1_ref[...]))
    y2 = _unpack_bf16_pairs(_tiles_to_rows(y2_ref[...]))
    y = w1 * y1 + w2 * y2
    out_ref[...] = _rms(h1_ref[...] + y, nfin_ref[...])


def _combine(h1, route, y12, w):
    t = h1.shape[0]
    row = lambda width: pl.BlockSpec((ROW_TILE, width), lambda i: (i, 0))
    tiles = lambda k: pl.BlockSpec((None, ROW_TILE, PACKED_CHUNKS, LANES), lambda i: (k, i, 0, 0))
    return pl.pallas_call(
        _combine_body,
        grid=(t // ROW_TILE,),
        in_specs=[row(D_MODEL), row(LANES), _full_spec((1, D_MODEL)), tiles(0), tiles(1)],
        out_specs=row(D_MODEL),
        out_shape=jax.ShapeDtypeStruct((t, D_MODEL), F32),
        compiler_params=_cparams("parallel"),
        name="moe_combine",
    )(h1, route, w['norm_final'], y12, y12)


def _moe(hn, route, ids, counts, h1, w):
    t = hn.shape[0]
    n_rows = 2 * t + N_EXPERTS * EXPERT_TILE
    n_grid_tiles = n_rows // EXPERT_TILE
    count = counts[0, ROUTER_EXPERT_LANE0:ROUTER_EXPERT_LANE0 + N_EXPERTS].astype(jnp.int32)
    padded = jnp.maximum((count + EXPERT_TILE - 1) // EXPERT_TILE, 1) * EXPERT_TILE
    ends = jnp.sum(jnp.where(jnp.arange(N_EXPERTS)[:, None] <= jnp.arange(N_EXPERTS)[None, :], padded[:, None], 0),
                   axis=0)
    starts = ends - padded
    n_tiles = (ends[-1:] // EXPERT_TILE)
    tile_rows = jnp.arange(n_grid_tiles, dtype=jnp.int32) * EXPERT_TILE
    tile_expert = jnp.minimum(jnp.sum((ends[None, :] <= tile_rows[:, None]).astype(jnp.int32), axis=1), N_EXPERTS - 1)
    tile_valid = jnp.clip(jnp.sum(jnp.where(jnp.arange(N_EXPERTS)[None, :] == tile_expert[:, None],
                                            (starts + count)[None, :], 0), axis=1) - tile_rows, 0, EXPERT_TILE)
    slots = _slots(starts, ids)
    ys = _experts(tile_expert, n_tiles, tile_valid, _sc_scatter(hn, slots, n_rows), w)
    y12 = _sc_gather(ys, slots.reshape(2 * t)).reshape(2, t, PACKED_CHUNKS, LANES)
    return _combine(h1, route, y12, w)


def _rope_tables(positions):
    inv_freq = 1.0 / (ROPE_THETA ** (jnp.arange(0, MLA_ROPE, 2, dtype=F32) / MLA_ROPE))
    ang = positions.astype(F32)[:, None] * inv_freq[None, :]
    cos, sin = jnp.cos(ang), jnp.sin(ang)
    reps = LANES // MLA_ROPE
    return jnp.tile(jnp.concatenate([cos, cos], axis=-1), (1, reps)), jnp.tile(jnp.concatenate([-sin, sin], axis=-1), (1, reps))


def _pack_weights(norm_mix, w_in, q_a_norm, w_uq, kv_a_norm, w_ukv, w_gate_fwd, b_gate_fwd, w_gate_bwd, b_gate_bwd,
                  gla_norm, w_out, norm_ffn, w_router_group, b_router_group, w_router_expert, b_router_expert,
                  w_expert_gate, w_expert_up, w_expert_down, norm_final):
    l = 0
    hk = GLA_HEADS * GLA_DK
    hv = GLA_HEADS * GLA_DV
    c_q, c_kv, k_pe, gq, gk, gv, lr_f, lr_b, og = jnp.split(
        w_in[l], np.cumsum([MLA_Q_RANK, MLA_KV_RANK, MLA_ROPE, hk, hk, hv, GLA_GATE_RANK, GLA_GATE_RANK])[:].tolist(),
        axis=-1)
    lr_pad = jnp.zeros((D_MODEL, LANES - 2 * GLA_GATE_RANK), F32)
    win = jnp.concatenate([c_q, c_kv, k_pe, k_pe, gq, gk, gv, og, lr_f, lr_b, lr_pad], axis=-1).astype(BF16)
    wuq = w_uq[l].reshape(MLA_Q_RANK, MLA_HEADS, MLA_NOPE + MLA_ROPE)
    wuq = jnp.concatenate([wuq[:, :, :MLA_NOPE].reshape(MLA_Q_RANK, -1), wuq[:, :, MLA_NOPE:].reshape(MLA_Q_RANK, -1)],
                          axis=-1).astype(BF16)
    wgate = jnp.zeros((LANES, 2 * hk), F32)
    wgate = wgate.at[0:GLA_GATE_RANK, 0:hk].set(w_gate_fwd[l])
    wgate = wgate.at[GLA_GATE_RANK:2 * GLA_GATE_RANK, hk:].set(w_gate_bwd[l])
    wr = jnp.zeros((D_MODEL, LANES), F32)
    wr = wr.at[:, ROUTER_GROUP_LANE0:ROUTER_GROUP_LANE0 + N_GROUPS].set(w_router_group[l])
    wr = wr.at[:, ROUTER_EXPERT_LANE0:ROUTER_EXPERT_LANE0 + N_EXPERTS].set(w_router_expert[l])
    wr_hi = wr.astype(BF16)
    br = jnp.zeros((1, LANES), F32)
    br = br.at[0, ROUTER_GROUP_LANE0:ROUTER_GROUP_LANE0 + N_GROUPS].set(b_router_group[l])
    br = br.at[0, ROUTER_EXPERT_LANE0:ROUTER_EXPERT_LANE0 + N_EXPERTS].set(b_router_expert[l])
    return {
        'norm_mix': norm_mix[l][None], 'win': win, 'q_a_norm': q_a_norm[l][None], 'wuq': wuq,
        'kv_a_norm': kv_a_norm[l][None], 'wukv': w_ukv[l].astype(BF16),
        'wgate': wgate.astype(BF16), 'bgate': jnp.concatenate([b_gate_fwd[l], b_gate_bwd[l]])[None],
        'gla_norm': gla_norm[l][None], 'wout': w_out[l].astype(BF16), 'norm_ffn': norm_ffn[l][None],
        'wr': jnp.concatenate([wr_hi, (wr - wr_hi.astype(F32)).astype(BF16)], axis=-1), 'br': br,
        'w_gate': w_expert_gate[l].reshape(N_EXPERTS, D_MODEL, D_EXPERT).astype(BF16),
        'w_up': w_expert_up[l].reshape(N_EXPERTS, D_MODEL, D_EXPERT).astype(BF16),
        'w_down': w_expert_down[l].reshape(N_EXPERTS, D_EXPERT, D_MODEL).astype(BF16),
        'norm_final': norm_final[None],
    }


def _meta_streams(meta_tokens, w):
    cos, sin = _rope_tables(jnp.arange(N_META))
    _, k, v, _, gk, gv, gf, _, _ = _inproj(meta_tokens, cos, sin, w, N_META)
    pad_keys = ((0, 0), (0, LANES - N_META), (0, 0))
    front = ((GLA_CHUNK - N_META, 0), (0, 0))
    return (jnp.pad(k, pad_keys), jnp.pad(v, pad_keys), jnp.pad(gk, front), jnp.pad(gv, front), jnp.pad(gf, front))


def _token_mixers(x, meta, w, tm, tq, tk, tb, tmix):
    bsz, seq, _ = x.shape
    km, vm, mk, mv, mg = meta
    x2d = x.reshape(bsz * seq, D_MODEL)
    cos, sin = _rope_tables(N_META + jnp.arange(seq))
    q, k, v, gq, gk, gv, gf, gb, og = _inproj(x2d, cos, sin, w, tm)
    a = _attention(q, k, v, km, vm, bsz, seq, tq, tk)
    o_f, o_b = _gla(gq, gk, gv, gf, gb, mk, mv, mg, bsz, seq, tb)
    return _mix(x2d, a, o_f, o_b, og, w, tmix)


def kernel(x_prompt, x_sample, meta_tokens, norm_mix, w_in, q_a_norm, w_uq, kv_a_norm, w_ukv, w_gate_fwd, b_gate_fwd, w_gate_bwd, b_gate_bwd, gla_norm, w_out, norm_ffn, w_router_group, b_router_group, w_router_expert, b_router_expert, w_expert_gate, w_expert_up, w_expert_down, norm_final):
    w = _pack_weights(norm_mix, w_in, q_a_norm, w_uq, kv_a_norm, w_ukv, w_gate_fwd, b_gate_fwd, w_gate_bwd,
                      b_gate_bwd, gla_norm, w_out, norm_ffn, w_router_group, b_router_group, w_router_expert,
                      b_router_expert, w_expert_gate, w_expert_up, w_expert_down, norm_final)
    meta = _meta_streams(meta_tokens, w)
    outs = []
    for x in (x_prompt, x_sample):
        h1, hn, route, ids, counts = _token_mixers(x, meta, w, tm=512, tq=1024, tk=1024, tb=512, tmix=1024)
        outs.append(_moe(hn, route, ids, counts, h1, w).reshape(x.shape))
    return tuple(outs)
```

```python
import functools

import numpy as np
import jax
import jax.numpy as jnp
from jax import lax
from jax.experimental import pallas as pl
from jax.experimental.pallas import tpu as pltpu
from jax.experimental.pallas import tpu_sc as plsc

F32 = jnp.float32
BF16 = jnp.bfloat16

D_MODEL = 1024
N_META = 16
MLA_HEADS = 4
MLA_Q_RANK = 384
MLA_KV_RANK = 256
MLA_NOPE = 128
MLA_ROPE = 64
MLA_V = 128
ROPE_THETA = 10000.0
GLA_HEADS = 4
GLA_DK = 64
GLA_DV = 128
GLA_GATE_RANK = 16
GLA_TAU = 16.0
GLA_CHUNK = 64
N_GROUPS = 4
EXPERTS_PER_GROUP = 8
N_EXPERTS = N_GROUPS * EXPERTS_PER_GROUP
D_EXPERT = 256
EPS = 1e-6

LANES = 128
V7X_VMEM_BYTES = 64 * 1024 * 1024
VMEM_LIMIT = V7X_VMEM_BYTES * 7 // 8

ATTN_SCALE = (MLA_NOPE + MLA_ROPE) ** -0.5 * float(np.log2(np.e))
QK_WIDTH = 2 * LANES
V_WIDTH = 2 * LANES
ATTN_GROUP = 8

C_CQ = 0
C_CKV = C_CQ + MLA_Q_RANK
C_KPE = C_CKV + MLA_KV_RANK
C_GQ = C_KPE + LANES
C_GK = C_GQ + GLA_HEADS * GLA_DK
C_GV = C_GK + GLA_HEADS * GLA_DK
C_OG = C_GV + GLA_HEADS * GLA_DV
C_LR = C_OG + GLA_HEADS * GLA_DV
D_IN_PACKED = C_LR + LANES

ROUTER_GROUP_LANE0 = 0
ROUTER_EXPERT_LANE0 = N_GROUPS


def _cparams(*semantics):
    return pltpu.CompilerParams(dimension_semantics=semantics, vmem_limit_bytes=VMEM_LIMIT)


def _rms(x, g):
    return x * lax.rsqrt(jnp.mean(x * x, axis=-1, keepdims=True) + EPS) * g


def _dot(a, b):
    return jnp.dot(a, b, preferred_element_type=F32)


def _dot_nt(a, b):
    return lax.dot_general(a, b, (((1,), (1,)), ((), ())), preferred_element_type=F32)


def _dot_tn(a, b):
    return lax.dot_general(a, b, (((0,), (0,)), ((), ())), preferred_element_type=F32)


def _full_spec(shape):
    return pl.BlockSpec(shape, lambda *_: (0,) * len(shape))


ROW_CHUNKS = D_MODEL // LANES


def _rows_to_tiles(x):
    chunks = jnp.stack([x[:, s * LANES:(s + 1) * LANES] for s in range(x.shape[1] // LANES)], axis=0)
    return pltpu.einshape("smd->msd", chunks)


def _tiles_to_rows(x):
    chunks = pltpu.einshape("msd->smd", x)
    return jnp.concatenate([chunks[s] for s in range(x.shape[1])], axis=-1)


PACKED_CHUNKS = ROW_CHUNKS // 2


def _pack_bf16_pairs(x):
    half = x.shape[1] // 2
    bits = lambda v: lax.bitcast_convert_type(v.astype(BF16).astype(F32), jnp.uint32)
    return (bits(x[:, half:]) & jnp.uint32(0xFFFF0000)) | (bits(x[:, :half]) >> 16)


def _unpack_bf16_pairs(w):
    lo = lax.bitcast_convert_type(w << 16, F32)
    hi = lax.bitcast_convert_type(w & jnp.uint32(0xFFFF0000), F32)
    return jnp.concatenate([lo, hi], axis=-1)


def _rope_pairs(x, cos, sin_signed, first_half):
    swapped = jnp.where(first_half, pltpu.roll(x, LANES - MLA_ROPE // 2, 1), pltpu.roll(x, MLA_ROPE // 2, 1))
    return x * cos + swapped * sin_signed


def _inproj_body(x_ref, cos_ref, sin_ref, nmix_ref, win_ref, qan_ref, wuq_ref, kvan_ref, wukv_ref,
                 wgate_ref, bgate_ref,
                 q_ref, k_ref, v_ref, gq_ref, gk_ref, gv_ref, gf_ref, gb_ref, og_ref):
    hn = _rms(x_ref[...], nmix_ref[...]).astype(BF16)

    def proj(lo, hi):
        return _dot(hn, win_ref[:, lo:hi])

    cos = cos_ref[...]
    sin = sin_ref[...]
    lane = lax.broadcasted_iota(jnp.int32, cos.shape, 1)
    first_half = (lane & (MLA_ROPE - 1)) < MLA_ROPE // 2
    low_lanes = lane < MLA_ROPE

    cq = _rms(proj(C_CQ, C_CKV), qan_ref[...]).astype(BF16)
    qn = _dot(cq, wuq_ref[:, 0:MLA_HEADS * MLA_NOPE]) * ATTN_SCALE
    qr = _dot(cq, wuq_ref[:, MLA_HEADS * MLA_NOPE:])
    for j in range(MLA_HEADS // 2):
        rj = (_rope_pairs(qr[:, j * LANES:(j + 1) * LANES], cos, sin, first_half) * ATTN_SCALE).astype(BF16)
        for h in (2 * j, 2 * j + 1):
            q_ref[h, :, 0:LANES] = qn[:, h * LANES:(h + 1) * LANES].astype(BF16)
            q_ref[h, :, LANES:QK_WIDTH] = rj

    ckv = _rms(proj(C_CKV, C_KPE), kvan_ref[...]).astype(BF16)
    kv = _dot(ckv, wukv_ref[...])
    kr = _rope_pairs(proj(C_KPE, C_GQ), cos, sin, first_half)
    kr_even = jnp.where(low_lanes, kr, 0.0).astype(BF16)
    kr_odd = jnp.where(low_lanes, 0.0, kr).astype(BF16)
    for h in range(MLA_HEADS):
        base = h * (MLA_NOPE + MLA_V)
        k_ref[h, :, 0:LANES] = kv[:, base:base + MLA_NOPE].astype(BF16)
        k_ref[h, :, LANES:QK_WIDTH] = kr_even if h % 2 == 0 else kr_odd
        v_ref[h, :, 0:MLA_V] = kv[:, base + MLA_NOPE:base + MLA_NOPE + MLA_V].astype(BF16)
        v_ref[h, :, MLA_V:V_WIDTH] = jnp.ones((kv.shape[0], V_WIDTH - MLA_V), BF16)

    gq_ref[...] = (proj(C_GQ, C_GK) * (GLA_DK ** -0.5)).astype(BF16)
    gk_ref[...] = proj(C_GK, C_GV).astype(BF16)
    gv_ref[...] = proj(C_GV, C_OG).astype(BF16)
    og_ref[...] = proj(C_OG, C_LR).astype(BF16)
    pre = _dot(proj(C_LR, D_IN_PACKED).astype(BF16), wgate_ref[...]) + bgate_ref[...]
    logsig = jnp.minimum(pre, 0.0) - jnp.log1p(jnp.exp(-jnp.abs(pre)))
    gates = logsig * (1.0 / GLA_TAU)
    gf_ref[...] = gates[:, 0:GLA_HEADS * GLA_DK]
    gb_ref[...] = gates[:, GLA_HEADS * GLA_DK:]


def _inproj(x2d, cos, sin, w, tm):
    t = x2d.shape[0]
    blocks_per_seq = cos.shape[0] // tm
    hk = GLA_HEADS * GLA_DK
    hv = GLA_HEADS * GLA_DV
    row = lambda width: pl.BlockSpec((tm, width), lambda i: (i, 0))
    head_rows = lambda width: pl.BlockSpec((MLA_HEADS, tm, width), lambda i: (0, i, 0))
    tab = pl.BlockSpec((tm, LANES), lambda i: (i % blocks_per_seq, 0))
    out_shape = (
        jax.ShapeDtypeStruct((MLA_HEADS, t, QK_WIDTH), BF16),
        jax.ShapeDtypeStruct((MLA_HEADS, t, QK_WIDTH), BF16),
        jax.ShapeDtypeStruct((MLA_HEADS, t, V_WIDTH), BF16),
        jax.ShapeDtypeStruct((t, hk), BF16),
        jax.ShapeDtypeStruct((t, hk), BF16),
        jax.ShapeDtypeStruct((t, hv), BF16),
        jax.ShapeDtypeStruct((t, hk), F32),
        jax.ShapeDtypeStruct((t, hk), F32),
        jax.ShapeDtypeStruct((t, hv), BF16),
    )
    return pl.pallas_call(
        _inproj_body,
        grid=(t // tm,),
        in_specs=[row(D_MODEL), tab, tab,
                  _full_spec((1, D_MODEL)), _full_spec((D_MODEL, D_IN_PACKED)),
                  _full_spec((1, MLA_Q_RANK)), _full_spec(w['wuq'].shape),
                  _full_spec((1, MLA_KV_RANK)), _full_spec(w['wukv'].shape),
                  _full_spec(w['wgate'].shape), _full_spec(w['bgate'].shape)],
        out_specs=(head_rows(QK_WIDTH), head_rows(QK_WIDTH), head_rows(V_WIDTH),
                   row(hk), row(hk), row(hv), row(hk), row(hk), row(hv)),
        out_shape=out_shape,
        compiler_params=_cparams("parallel"),
        name="inproj",
    )(x2d, cos, sin, w['norm_mix'], w['win'], w['q_a_norm'], w['wuq'], w['kv_a_norm'], w['wukv'],
      w['wgate'], w['bgate'])


def _attn_body(q_ref, k_ref, v_ref, km_ref, vm_ref, o_ref, s_ref, acc_ref, *, tk):
    q = q_ref[...]
    n_blocks = k_ref.shape[0] // tk
    group_size = min(ATTN_GROUP, n_blocks)
    assert n_blocks % group_size == 0 and group_size % 2 == 0
    n_groups = n_blocks // group_size

    def scores(j):
        return _dot_nt(q, k_ref[pl.ds(pl.multiple_of(j * tk, tk), tk), :])

    def values(j):
        return v_ref[pl.ds(pl.multiple_of(j * tk, tk), tk), :]

    def absorb(m, s, v):
        m_new = jnp.maximum(m, jnp.max(s, axis=-1, keepdims=True))
        p = jnp.exp2(s - m_new)
        acc_ref[...] = jnp.exp2(m - m_new) * acc_ref[...] + _dot(p.astype(BF16), v)
        return m_new

    s_ref[0] = scores(0)
    sm = _dot_nt(q, km_ref[...])
    sm = jnp.where(lax.broadcasted_iota(jnp.int32, sm.shape, 1) < N_META, sm, -jnp.inf)
    m = jnp.max(sm, axis=-1, keepdims=True)
    acc_ref[...] = _dot(jnp.exp2(sm - m).astype(BF16), vm_ref[...])

    def group(g, m, last):
        for i in range(group_size):
            j = group_size * g + i
            s = s_ref[i % 2]
            if not (last and i == group_size - 1):
                s_ref[(i + 1) % 2] = scores(j + 1)
            m = absorb(m, s, values(j))
        return m

    m = lax.fori_loop(0, n_groups - 1, lambda g, m: group(g, m, False), m)
    group(n_groups - 1, m, True)
    acc = acc_ref[...]
    o_ref[...] = (acc[:, :MLA_V] / acc[:, MLA_V:]).astype(o_ref.dtype)


def _attention(q, k, v, km, vm, bsz, seq, tq, tk):
    nq = seq // tq
    return pl.pallas_call(
        functools.partial(_attn_body, tk=tk),
        grid=(bsz, MLA_HEADS, nq),
        in_specs=[pl.BlockSpec((None, tq, QK_WIDTH), lambda b, h, i: (h, b * nq + i, 0)),
                  pl.BlockSpec((None, seq, QK_WIDTH), lambda b, h, i: (h, b, 0)),
                  pl.BlockSpec((None, seq, V_WIDTH), lambda b, h, i: (h, b, 0)),
                  pl.BlockSpec((None, LANES, QK_WIDTH), lambda b, h, i: (h, 0, 0)),
                  pl.BlockSpec((None, LANES, V_WIDTH), lambda b, h, i: (h, 0, 0))],
        out_specs=pl.BlockSpec((tq, MLA_V), lambda b, h, i: (b * nq + i, h)),
        out_shape=jax.ShapeDtypeStruct((bsz * seq, MLA_HEADS * MLA_V), BF16),
        scratch_shapes=[pltpu.VMEM((2, tq, tk), F32), pltpu.VMEM((tq, V_WIDTH), F32)],
        compiler_params=_cparams("parallel", "parallel", "arbitrary"),
        name="mla_attention",
    )(q, k, v, km, vm)


def _split3(x):
    hi = x.astype(BF16)
    r1 = x - hi.astype(F32)
    mid = r1.astype(BF16)
    lo = (r1 - mid.astype(F32)).astype(BF16)
    return hi, mid, lo


def _gla_log_decay(g, tri):
    g_hi, g_mid, g_lo = _split3(g)
    return _dot(tri, g_hi) + _dot(tri, g_mid) + _dot(tri, g_lo)


def _head_rows(x):
    even = lax.broadcasted_iota(jnp.int32, x.shape, 1) < GLA_DK
    return jnp.concatenate([jnp.where(even, x, 0.0), jnp.where(even, 0.0, x)], axis=0).astype(BF16)


def _gla_operands(q, k, b, mid, last):
    b_last = b[last:last + 1, :]
    ke = (k * jnp.exp(b_last - b)).astype(BF16)
    if q is None:
        return jnp.exp(b_last), ke, None, None, None
    b_mid = b[mid:mid + 1, :]
    ks = (k * jnp.exp(b_mid - b)).astype(BF16)
    return jnp.exp(b_last), ke, ks, _head_rows(q * jnp.exp(b - b_mid)), _head_rows(q * jnp.exp(b))


def _gla_state_update(v_even, v_odd, ke):
    return jnp.where(lax.broadcasted_iota(jnp.int32, (GLA_DV, LANES), 1) < GLA_DK,
                     _dot_tn(v_even, ke), _dot_tn(v_odd, ke))


def _gla_intra(v_even, v_odd, ks, qs2, keep):
    scores = jnp.where(keep, _dot_nt(qs2, ks), 0.0).astype(BF16)
    return _dot(scores[:GLA_CHUNK], v_even), _dot(scores[GLA_CHUNK:], v_odd)


def _gla_body(qf_ref, kf_ref, vf_ref, gf_ref, qb_ref, kb_ref, vb_ref, gb_ref, mk_ref, mv_ref, mg_ref,
              of_ref, ob_ref, state_ref):
    n_chunks = qf_ref.shape[0] // GLA_CHUNK
    n_pairs = GLA_HEADS // 2
    r = lax.broadcasted_iota(jnp.int32, (GLA_CHUNK, GLA_CHUNK), 0)
    c = lax.broadcasted_iota(jnp.int32, (GLA_CHUNK, GLA_CHUNK), 1)
    tri_f = jnp.where(c <= r, 1.0, 0.0).astype(BF16)
    tri_b = jnp.where(c >= r, 1.0, 0.0).astype(BF16)
    r2 = lax.broadcasted_iota(jnp.int32, (2 * GLA_CHUNK, GLA_CHUNK), 0) & (GLA_CHUNK - 1)
    c2 = lax.broadcasted_iota(jnp.int32, (2 * GLA_CHUNK, GLA_CHUNK), 1)
    keep_f = c2 <= r2
    keep_b = c2 >= r2
    mid_f, last_f = GLA_CHUNK // 2 - 1, GLA_CHUNK - 1
    mid_b, last_b = GLA_CHUNK // 2, 0

    def pair_cols(p):
        return slice(p * LANES, (p + 1) * LANES)

    def head_cols(h):
        return slice(h * GLA_DV, (h + 1) * GLA_DV)

    @pl.when(pl.program_id(1) == 0)
    def _():
        for p in range(n_pairs):
            b = _gla_log_decay(mg_ref[:, pair_cols(p)], tri_f)
            _, ke, _, _, _ = _gla_operands(None, mk_ref[:, pair_cols(p)], b, mid_f, last_f)
            state_ref[p] = _gla_state_update(mv_ref[:, head_cols(2 * p)].astype(BF16),
                                             mv_ref[:, head_cols(2 * p + 1)].astype(BF16), ke)
            state_ref[n_pairs + p] = jnp.zeros((GLA_DV, LANES), F32)

    scans = []
    for p in range(n_pairs):
        scans.append((p, list(range(n_chunks)), qf_ref, kf_ref, vf_ref, gf_ref, of_ref, p,
                      tri_f, keep_f, mid_f, last_f))
        scans.append((p, list(reversed(range(n_chunks))), qb_ref, kb_ref, vb_ref, gb_ref, ob_ref, n_pairs + p,
                      tri_b, keep_b, mid_b, last_b))

    def rows(c):
        return slice(c * GLA_CHUNK, (c + 1) * GLA_CHUNK)

    def values(v_ref, p, c):
        return (v_ref[rows(c), head_cols(2 * p)].astype(BF16), v_ref[rows(c), head_cols(2 * p + 1)].astype(BF16))

    log_decay = [[_gla_log_decay(g_ref[rows(c), pair_cols(p)], tri) for c in order]
                 for (p, order, _, _, _, g_ref, _, _, tri, _, _, _) in scans]
    operands = [[_gla_operands(q_ref[rows(c), pair_cols(p)], k_ref[rows(c), pair_cols(p)], b, mid, last)
                 for c, b in zip(order, bs)]
                for (p, order, q_ref, k_ref, _, _, _, _, _, _, mid, last), bs in zip(scans, log_decay)]
    updates = [[_gla_state_update(*values(v_ref, p, c), ops[1]) for c, ops in zip(order, opss)]
               for (p, order, _, _, v_ref, _, _, _, _, _, _, _), opss in zip(scans, operands)]
    intra = [[_gla_intra(*values(v_ref, p, c), ops[2], ops[3], keep) for c, ops in zip(order, opss)]
             for (p, order, _, _, v_ref, _, _, _, _, keep, _, _), opss in zip(scans, operands)]
    states = []
    for (_, order, _, _, _, _, _, slot, _, _, _, _), opss, upds in zip(scans, operands, updates):
        st = state_ref[slot]
        entering = []
        for ops, upd in zip(opss, upds):
            entering.append(st.astype(BF16))
            st = st * ops[0] + upd
        state_ref[slot] = st
        states.append(entering)
    for (p, order, _, _, _, _, o_ref, _, _, _, _, _), opss, sts, locs in zip(scans, operands, states, intra):
        for c, ops, st, (o_even, o_odd) in zip(order, opss, sts, locs):
            inter = _dot_nt(ops[4], st)
            o_ref[rows(c), head_cols(2 * p)] = (o_even + inter[:GLA_CHUNK]).astype(o_ref.dtype)
            o_ref[rows(c), head_cols(2 * p + 1)] = (o_odd + inter[GLA_CHUNK:]).astype(o_ref.dtype)


def _gla(gq, gk, gv, gf, gb, mk, mv, mg, bsz, seq, tb):
    nb = seq // tb
    hk = GLA_HEADS * GLA_DK
    hv = GLA_HEADS * GLA_DV
    fwd = lambda width: pl.BlockSpec((tb, width), lambda b, j: (b * nb + j, 0))
    bwd = lambda width: pl.BlockSpec((tb, width), lambda b, j: (b * nb + nb - 1 - j, 0))
    t = bsz * seq
    return pl.pallas_call(
        _gla_body,
        grid=(bsz, nb),
        in_specs=[fwd(hk), fwd(hk), fwd(hv), fwd(hk), bwd(hk), bwd(hk), bwd(hv), bwd(hk),
                  _full_spec(mk.shape), _full_spec(mv.shape), _full_spec(mg.shape)],
        out_specs=(fwd(hv), bwd(hv)),
        out_shape=(jax.ShapeDtypeStruct((t, hv), BF16), jax.ShapeDtypeStruct((t, hv), BF16)),
        scratch_shapes=[pltpu.VMEM((2 * (GLA_HEADS // 2), GLA_DV, LANES), F32)],
        compiler_params=_cparams("parallel", "arbitrary"),
        name="gla_scan",
    )(gq, gk, gv, gf, gq, gk, gv, gb, mk, mv, mg)


def _mix_body(x_ref, a_ref, of_ref, ob_ref, og_ref, gnorm_ref, wout_ref, nffn_ref, wr_ref, br_ref,
              h1_ref, hn_ref, route_ref, ids_ref, count_ref, tri_ref):
    tm = x_ref.shape[0]

    @pl.when(pl.program_id(0) == 0)
    def _():
        r = lax.broadcasted_iota(jnp.int32, (tm, tm), 0)
        c = lax.broadcasted_iota(jnp.int32, (tm, tm), 1)
        tri_ref[...] = jnp.where(c < r, 1.0, 0.0).astype(BF16)
        count_ref[...] = jnp.zeros_like(count_ref)

    a_width = MLA_HEADS * MLA_V
    gated = []
    for h in range(GLA_HEADS):
        cols = slice(h * GLA_DV, (h + 1) * GLA_DV)
        o = of_ref[:, cols].astype(F32) + ob_ref[:, cols].astype(F32)
        og = og_ref[:, cols].astype(F32)
        silu = og / (1.0 + jnp.exp(-og))
        gated.append((_rms(o, gnorm_ref[...]) * silu).astype(BF16))
    h1 = (x_ref[...] + _dot(a_ref[...], wout_ref[0:a_width, :])
          + _dot(jnp.concatenate(gated, axis=-1), wout_ref[a_width:, :]))
    h1_ref[...] = h1
    hn = _rms(h1, nffn_ref[...])
    hn_ref[...] = _rows_to_tiles(_pack_bf16_pairs(hn))

    hn_hi = hn.astype(BF16)
    hn_lo = (hn - hn_hi.astype(F32)).astype(BF16)
    hi_terms = _dot(hn_hi, wr_ref[...])
    logits = hi_terms[:, :LANES] + hi_terms[:, LANES:] + _dot(hn_lo, wr_ref[:, :LANES]) + br_ref[...]
    lane = lax.broadcasted_iota(jnp.int32, logits.shape, 1).astype(F32)
    none = float(LANES)
    neg = -jnp.inf

    def lane_max(x):
        return jnp.max(x, axis=-1, keepdims=True)

    def lane_sum(x):
        return jnp.sum(x, axis=-1, keepdims=True)

    def first_lane(mask):
        return jnp.min(jnp.where(mask, lane, none), axis=-1, keepdims=True)

    is_group = lane < float(N_GROUPS)
    g_max = lane_max(jnp.where(is_group, logits, neg))
    g_exp = jnp.where(is_group, jnp.exp(logits - g_max), 0.0)
    g_prob = g_exp / lane_sum(g_exp)
    g_w = lane_max(g_prob)
    g_idx = first_lane(is_group & (g_prob == g_w))
    e_lo = float(ROUTER_EXPERT_LANE0) + float(EXPERTS_PER_GROUP) * g_idx
    sel = (lane >= e_lo) & (lane < e_lo + float(EXPERTS_PER_GROUP))
    e_max = lane_max(jnp.where(sel, logits, neg))
    e_exp = jnp.where(sel, jnp.exp(logits - e_max), 0.0)
    e_prob = e_exp / lane_sum(e_exp)
    p1 = lane_max(jnp.where(sel, e_prob, neg))
    i1 = first_lane(sel & (e_prob == p1))
    rest = sel & (lane != i1)
    p2 = lane_max(jnp.where(rest, e_prob, neg))
    i2 = first_lane(rest & (e_prob == p2))
    denom = p1 + p2
    chosen = jnp.where((lane == i1) | (lane == i2), 1.0, 0.0)
    rank = count_ref[...] + _dot(tri_ref[...], chosen.astype(BF16))
    count_ref[...] += jnp.sum(chosen, axis=0, keepdims=True)
    fields = (i1 - float(ROUTER_EXPERT_LANE0), i2 - float(ROUTER_EXPERT_LANE0),
              lane_sum(jnp.where(lane == i1, rank, 0.0)), lane_sum(jnp.where(lane == i2, rank, 0.0)),
              g_w * (p1 / denom), g_w * (p2 / denom))
    route = jnp.zeros_like(logits)
    for k, value in enumerate(fields):
        route = jnp.where(lane == float(k), value, route)
    route_ref[...] = route
    ids_ref[...] = jnp.transpose(route)[0:ROUTE_ID_ROWS, :].astype(jnp.int32)


ROUTE_E1, ROUTE_E2, ROUTE_RANK1, ROUTE_RANK2, ROUTE_W1, ROUTE_W2 = range(6)
ROUTE_ID_ROWS = 8


def _mix(x2d, a, o_f, o_b, og, w, tm):
    t = x2d.shape[0]
    hv = GLA_HEADS * GLA_DV
    row = lambda width: pl.BlockSpec((tm, width), lambda i: (i, 0))
    return pl.pallas_call(
        _mix_body,
        grid=(t // tm,),
        in_specs=[row(D_MODEL), row(MLA_HEADS * MLA_V), row(hv), row(hv), row(hv),
                  _full_spec((1, GLA_DV)), _full_spec(w['wout'].shape), _full_spec((1, D_MODEL)),
                  _full_spec(w['wr'].shape), _full_spec(w['br'].shape)],
        out_specs=(row(D_MODEL), pl.BlockSpec((tm, PACKED_CHUNKS, LANES), lambda i: (i, 0, 0)), row(LANES),
                   pl.BlockSpec((ROUTE_ID_ROWS, tm), lambda i: (0, i)), _full_spec((1, LANES))),
        out_shape=(jax.ShapeDtypeStruct((t, D_MODEL), F32), jax.ShapeDtypeStruct((t, PACKED_CHUNKS, LANES), jnp.uint32),
                   jax.ShapeDtypeStruct((t, LANES), F32), jax.ShapeDtypeStruct((ROUTE_ID_ROWS, t), jnp.int32),
                   jax.ShapeDtypeStruct((1, LANES), F32)),
        scratch_shapes=[pltpu.VMEM((tm, tm), BF16)],
        compiler_params=_cparams("arbitrary"),
        name="mix_router",
    )(x2d, a, o_f, o_b, og, w['gla_norm'], w['wout'], w['norm_ffn'], w['wr'], w['br'])


EXPERT_TILE = 256
ROW_TILE = 1024


SC_CORES = 2
SC_SUBCORES = 16
SC_GATHER_ROWS = 32


def _sc_gather(table, idx):
    n = idx.shape[0]
    workers = SC_CORES * SC_SUBCORES
    per_worker = n // workers
    assert n % (workers * SC_GATHER_ROWS) == 0
    mesh = plsc.VectorSubcoreMesh(core_axis_name="c", subcore_axis_name="s")

    @functools.partial(
        pl.kernel, mesh=mesh,
        out_type=jax.ShapeDtypeStruct((n,) + table.shape[1:], table.dtype),
        scratch_types=[pltpu.VMEM((SC_GATHER_ROWS,), jnp.int32),
                       pltpu.VMEM((SC_GATHER_ROWS,) + table.shape[1:], table.dtype),
                       pltpu.SemaphoreType.DMA])
    def gather(table_ref, idx_ref, out_ref, idx_buf, rows_buf, sem):
        base = (lax.axis_index("s") * SC_CORES + lax.axis_index("c")) * per_worker

        @pl.loop(0, per_worker // SC_GATHER_ROWS)
        def _(j):
            rows = pl.ds(base + j * SC_GATHER_ROWS, SC_GATHER_ROWS)
            pltpu.sync_copy(idx_ref.at[rows], idx_buf)
            pltpu.async_copy(table_ref.at[idx_buf], rows_buf, sem).wait()
            pltpu.sync_copy(rows_buf, out_ref.at[rows])

    return gather(table, idx)


def _sc_scatter(rows, idx, n_out):
    copies, n = idx.shape
    idx = idx.reshape(copies * n)
    workers = SC_CORES * SC_SUBCORES
    per_worker = n // workers
    assert n % (workers * SC_GATHER_ROWS) == 0
    mesh = plsc.VectorSubcoreMesh(core_axis_name="c", subcore_axis_name="s")

    @functools.partial(
        pl.kernel, mesh=mesh,
        out_type=jax.ShapeDtypeStruct((n_out,) + rows.shape[1:], rows.dtype),
        scratch_types=[pltpu.VMEM((SC_GATHER_ROWS,), jnp.int32),
                       pltpu.VMEM((SC_GATHER_ROWS,) + rows.shape[1:], rows.dtype),
                       pltpu.SemaphoreType.DMA])
    def scatter(rows_ref, idx_ref, out_ref, idx_buf, rows_buf, sem):
        base = (lax.axis_index("s") * SC_CORES + lax.axis_index("c")) * per_worker

        @pl.loop(0, per_worker // SC_GATHER_ROWS)
        def _(j):
            first = base + j * SC_GATHER_ROWS
            pltpu.sync_copy(rows_ref.at[pl.ds(first, SC_GATHER_ROWS)], rows_buf)
            for k in range(copies):
                pltpu.sync_copy(idx_ref.at[pl.ds(k * n + first, SC_GATHER_ROWS)], idx_buf)
                pltpu.async_copy(rows_buf, out_ref.at[idx_buf], sem).wait()

    return scatter(rows, idx)


SLOT_TILE = 2048


def _slots_body(starts_ref, ids_ref, slots_ref):
    ids = ids_ref[...]
    experts = ids[ROUTE_E1:ROUTE_E1 + 2, :]
    start = jnp.zeros_like(experts)
    for e in range(N_EXPERTS):
        start = jnp.where(experts == e, starts_ref[e], start)
    slots_ref[...] = start + ids[ROUTE_RANK1:ROUTE_RANK1 + 2, :]


def _slots(starts, ids):
    t = ids.shape[1]
    return pl.pallas_call(
        _slots_body,
        grid_spec=pltpu.PrefetchScalarGridSpec(
            num_scalar_prefetch=1,
            grid=(t // SLOT_TILE,),
            in_specs=[pl.BlockSpec((ROUTE_ID_ROWS, SLOT_TILE), lambda i, *_: (0, i))],
            out_specs=pl.BlockSpec((2, SLOT_TILE), lambda i, *_: (0, i))),
        out_shape=jax.ShapeDtypeStruct((2, t), jnp.int32),
        compiler_params=_cparams("parallel"),
        name="moe_slots",
    )(starts, ids)


def _queue_body(slots_ref, token_ref, *, n_tokens):
    i = pl.program_id(0)

    @pl.when(i == 0)
    def _():
        for base in range(0, token_ref.shape[0], n_tokens):
            def clear(j, carry):
                token_ref[base + j] = j
                return carry
            lax.fori_loop(0, min(n_tokens, token_ref.shape[0] - base), clear, 0, unroll=8)

    def place(r, carry):
        for k in range(2):
            token_ref[slots_ref[k, r]] = i * SLOT_TILE + r
        return carry

    lax.fori_loop(0, SLOT_TILE, place, 0, unroll=8)


def _queue_tokens(slots, n_rows):
    t = slots.shape[1]
    return pl.pallas_call(
        functools.partial(_queue_body, n_tokens=t),
        grid=(t // SLOT_TILE,),
        in_specs=[pl.BlockSpec((2, SLOT_TILE), lambda i: (0, i), memory_space=pltpu.SMEM)],
        out_specs=pl.BlockSpec(memory_space=pltpu.SMEM),
        out_shape=jax.ShapeDtypeStruct((n_rows,), jnp.int32),
        compiler_params=_cparams("arbitrary"),
        name="moe_queue",
    )(slots)


TILES_PER_STEP = 4


def _expert_body(tile_expert_ref, n_tiles_ref, tile_valid_ref, xs_ref, *refs):
    wg_refs = refs[0:TILES_PER_STEP]
    wu_refs = refs[TILES_PER_STEP:2 * TILES_PER_STEP]
    wd_refs = refs[2 * TILES_PER_STEP:3 * TILES_PER_STEP]
    ys_ref = refs[3 * TILES_PER_STEP]
    first = pl.program_id(0) * TILES_PER_STEP
    used = first < n_tiles_ref[0]

    @pl.when(used)
    def _():
        rows = [slice(j * EXPERT_TILE, (j + 1) * EXPERT_TILE) for j in range(TILES_PER_STEP)]
        xs = [_unpack_bf16_pairs(_tiles_to_rows(xs_ref[r])) for r in rows]
        xs = [jnp.where(lax.broadcasted_iota(jnp.int32, x.shape, 0) < tile_valid_ref[first + j], x, 0.0)
              for j, x in enumerate(xs)]
        xs = [x.astype(BF16) for x in xs]
        gates = [_dot(x, wg[...]) for x, wg in zip(xs, wg_refs)]
        ups = [_dot(x, wu[...]) for x, wu in zip(xs, wu_refs)]
        acts = [(g / (1.0 + jnp.exp(-g))) * u for g, u in zip(gates, ups)]
        ys = [_dot(act.astype(BF16), wd[...]) for act, wd in zip(acts, wd_refs)]
        for r, y in zip(rows, ys):
            ys_ref[r] = _rows_to_tiles(_pack_bf16_pairs(y))

    @pl.when(jnp.logical_not(used))
    def _():
        ys_ref[...] = jnp.zeros_like(ys_ref)


def _experts(tile_expert, n_tiles, tile_valid, xs, w):
    n_steps = xs.shape[0] // (EXPERT_TILE * TILES_PER_STEP)

    def step(i, nt):
        return jnp.minimum(i, (nt[0] - 1) // TILES_PER_STEP)

    def expert(i, j, te, nt):
        return te[jnp.minimum(step(i, nt) * TILES_PER_STEP + j, nt[0] - 1)]

    tiles = lambda index: pl.BlockSpec((EXPERT_TILE * TILES_PER_STEP, PACKED_CHUNKS, LANES), index)
    up_spec = lambda j: pl.BlockSpec((None, D_MODEL, D_EXPERT), lambda i, te, nt, tv: (expert(i, j, te, nt), 0, 0))
    down_spec = lambda j: pl.BlockSpec((None, D_EXPERT, D_MODEL), lambda i, te, nt, tv: (expert(i, j, te, nt), 0, 0))
    slots = range(TILES_PER_STEP)
    return pl.pallas_call(
        _expert_body,
        grid_spec=pltpu.PrefetchScalarGridSpec(
            num_scalar_prefetch=3,
            grid=(n_steps,),
            in_specs=([tiles(lambda i, te, nt, tv: (step(i, nt), 0, 0))]
                      + [up_spec(j) for j in slots] + [up_spec(j) for j in slots] + [down_spec(j) for j in slots]),
            out_specs=tiles(lambda i, te, nt, tv: (i, 0, 0))),
        out_shape=jax.ShapeDtypeStruct(xs.shape, xs.dtype),
        compiler_params=_cparams("arbitrary"),
        name="moe_experts",
    )(tile_expert, n_tiles, tile_valid, xs, *([w['w_gate']] * TILES_PER_STEP), *([w['w_up']] * TILES_PER_STEP),
      *([w['w_down']] * TILES_PER_STEP))


def _combine_body(h1_ref, route_ref, nfin_ref, y1_ref, y2_ref, out_ref):
    route = route_ref[...]
    lane = lax.broadcasted_iota(jnp.int32, route.shape, 1)
    w1 = jnp.sum(jnp.where(lane == ROUTE_W1, route, 0.0), axis=-1, keepdims=True)
    w2 = jnp.sum(jnp.where(lane == ROUTE_W2, route, 0.0), axis=-1, keepdims=True)
    y1 = _unpack_bf16_pairs(_tiles_to_rows(y1_ref[...]))
    y2 = _unpack_bf16_pairs(_tiles_to_rows(y2_ref[...]))
    y = w1 * y1 + w2 * y2
    out_ref[...] = _rms(h1_ref[...] + y, nfin_ref[...])


def _combine(h1, route, y12, w):
    t = h1.shape[0]
    row = lambda width: pl.BlockSpec((ROW_TILE, width), lambda i: (i, 0))
    tiles = lambda k: pl.BlockSpec((None, ROW_TILE, PACKED_CHUNKS, LANES), lambda i: (k, i, 0, 0))
    return pl.pallas_call(
        _combine_body,
        grid=(t // ROW_TILE,),
        in_specs=[row(D_MODEL), row(LANES), _full_spec((1, D_MODEL)), tiles(0), tiles(1)],
        out_specs=row(D_MODEL),
        out_shape=jax.ShapeDtypeStruct((t, D_MODEL), F32),
        compiler_params=_cparams("parallel"),
        name="moe_combine",
    )(h1, route, w['norm_final'], y12, y12)


def _moe(hn, route, ids, counts, h1, w):
    t = hn.shape[0]
    n_rows = 2 * t + N_EXPERTS * EXPERT_TILE
    n_grid_tiles = n_rows // EXPERT_TILE
    count = counts[0, ROUTER_EXPERT_LANE0:ROUTER_EXPERT_LANE0 + N_EXPERTS].astype(jnp.int32)
    padded = jnp.maximum((count + EXPERT_TILE - 1) // EXPERT_TILE, 1) * EXPERT_TILE
    ends = jnp.sum(jnp.where(jnp.arange(N_EXPERTS)[:, None] <= jnp.arange(N_EXPERTS)[None, :], padded[:, None], 0),
                   axis=0)
    starts = ends - padded
    n_tiles = (ends[-1:] // EXPERT_TILE)
    tile_rows = jnp.arange(n_grid_tiles, dtype=jnp.int32) * EXPERT_TILE
    tile_expert = jnp.minimum(jnp.sum((ends[None, :] <= tile_rows[:, None]).astype(jnp.int32), axis=1), N_EXPERTS - 1)
    tile_valid = jnp.clip(jnp.sum(jnp.where(jnp.arange(N_EXPERTS)[None, :] == tile_expert[:, None],
                                            (starts + count)[None, :], 0), axis=1) - tile_rows, 0, EXPERT_TILE)
    slots = _slots(starts, ids)
    ys = _experts(tile_expert, n_tiles, tile_valid, _sc_scatter(hn, slots, n_rows), w)
    y12 = _sc_gather(ys, slots.reshape(2 * t)).reshape(2, t, PACKED_CHUNKS, LANES)
    return _combine(h1, route, y12, w)


def _rope_tables(positions):
    inv_freq = 1.0 / (ROPE_THETA ** (jnp.arange(0, MLA_ROPE, 2, dtype=F32) / MLA_ROPE))
    ang = positions.astype(F32)[:, None] * inv_freq[None, :]
    cos, sin = jnp.cos(ang), jnp.sin(ang)
    reps = LANES // MLA_ROPE
    return jnp.tile(jnp.concatenate([cos, cos], axis=-1), (1, reps)), jnp.tile(jnp.concatenate([-sin, sin], axis=-1), (1, reps))


def _pack_weights(norm_mix, w_in, q_a_norm, w_uq, kv_a_norm, w_ukv, w_gate_fwd, b_gate_fwd, w_gate_bwd, b_gate_bwd,
                  gla_norm, w_out, norm_ffn, w_router_group, b_router_group, w_router_expert, b_router_expert,
                  w_expert_gate, w_expert_up, w_expert_down, norm_final):
    l = 0
    hk = GLA_HEADS * GLA_DK
    hv = GLA_HEADS * GLA_DV
    c_q, c_kv, k_pe, gq, gk, gv, lr_f, lr_b, og = jnp.split(
        w_in[l], np.cumsum([MLA_Q_RANK, MLA_KV_RANK, MLA_ROPE, hk, hk, hv, GLA_GATE_RANK, GLA_GATE_RANK])[:].tolist(),
        axis=-1)
    lr_pad = jnp.zeros((D_MODEL, LANES - 2 * GLA_GATE_RANK), F32)
    win = jnp.concatenate([c_q, c_kv, k_pe, k_pe, gq, gk, gv, og, lr_f, lr_b, lr_pad], axis=-1).astype(BF16)
    wuq = w_uq[l].reshape(MLA_Q_RANK, MLA_HEADS, MLA_NOPE + MLA_ROPE)
    wuq = jnp.concatenate([wuq[:, :, :MLA_NOPE].reshape(MLA_Q_RANK, -1), wuq[:, :, MLA_NOPE:].reshape(MLA_Q_RANK, -1)],
                          axis=-1).astype(BF16)
    wgate = jnp.zeros((LANES, 2 * hk), F32)
    wgate = wgate.at[0:GLA_GATE_RANK, 0:hk].set(w_gate_fwd[l])
    wgate = wgate.at[GLA_GATE_RANK:2 * GLA_GATE_RANK, hk:].set(w_gate_bwd[l])
    wr = jnp.zeros((D_MODEL, LANES), F32)
    wr = wr.at[:, ROUTER_GROUP_LANE0:ROUTER_GROUP_LANE0 + N_GROUPS].set(w_router_group[l])
    wr = wr.at[:, ROUTER_EXPERT_LANE0:ROUTER_EXPERT_LANE0 + N_EXPERTS].set(w_router_expert[l])
    wr_hi = wr.astype(BF16)
    br = jnp.zeros((1, LANES), F32)
    br = br.at[0, ROUTER_GROUP_LANE0:ROUTER_GROUP_LANE0 + N_GROUPS].set(b_router_group[l])
    br = br.at[0, ROUTER_EXPERT_LANE0:ROUTER_EXPERT_LANE0 + N_EXPERTS].set(b_router_expert[l])
    return {
        'norm_mix': norm_mix[l][None], 'win': win, 'q_a_norm': q_a_norm[l][None], 'wuq': wuq,
        'kv_a_norm': kv_a_norm[l][None], 'wukv': w_ukv[l].astype(BF16),
        'wgate': wgate.astype(BF16), 'bgate': jnp.concatenate([b_gate_fwd[l], b_gate_bwd[l]])[None],
        'gla_norm': gla_norm[l][None], 'wout': w_out[l].astype(BF16), 'norm_ffn': norm_ffn[l][None],
        'wr': jnp.concatenate([wr_hi, (wr - wr_hi.astype(F32)).astype(BF16)], axis=-1), 'br': br,
        'w_gate': w_expert_gate[l].reshape(N_EXPERTS, D_MODEL, D_EXPERT).astype(BF16),
        'w_up': w_expert_up[l].reshape(N_EXPERTS, D_MODEL, D_EXPERT).astype(BF16),
        'w_down': w_expert_down[l].reshape(N_EXPERTS, D_EXPERT, D_MODEL).astype(BF16),
        'norm_final': norm_final[None],
    }


def _meta_streams(meta_tokens, w):
    cos, sin = _rope_tables(jnp.arange(N_META))
    _, k, v, _, gk, gv, gf, _, _ = _inproj(meta_tokens, cos, sin, w, N_META)
    pad_keys = ((0, 0), (0, LANES - N_META), (0, 0))
    front = ((GLA_CHUNK - N_META, 0), (0, 0))
    return (jnp.pad(k, pad_keys), jnp.pad(v, pad_keys), jnp.pad(gk, front), jnp.pad(gv, front), jnp.pad(gf, front))


def _token_mixers(x, meta, w, tm, tq, tk, tb, tmix):
    bsz, seq, _ = x.shape
    km, vm, mk, mv, mg = meta
    x2d = x.reshape(bsz * seq, D_MODEL)
    cos, sin = _rope_tables(N_META + jnp.arange(seq))
    q, k, v, gq, gk, gv, gf, gb, og = _inproj(x2d, cos, sin, w, tm)
    a = _attention(q, k, v, km, vm, bsz, seq, tq, tk)
    o_f, o_b = _gla(gq, gk, gv, gf, gb, mk, mv, mg, bsz, seq, tb)
    return _mix(x2d, a, o_f, o_b, og, w, tmix)


def kernel(x_prompt, x_sample, meta_tokens, norm_mix, w_in, q_a_norm, w_uq, kv_a_norm, w_ukv, w_gate_fwd, b_gate_fwd, w_gate_bwd, b_gate_bwd, gla_norm, w_out, norm_ffn, w_router_group, b_router_group, w_router_expert, b_router_expert, w_expert_gate, w_expert_up, w_expert_down, norm_final):
    w = _pack_weights(norm_mix, w_in, q_a_norm, w_uq, kv_a_norm, w_ukv, w_gate_fwd, b_gate_fwd, w_gate_bwd,
                      b_gate_bwd, gla_norm, w_out, norm_ffn, w_router_group, b_router_group, w_router_expert,
                      b_router_expert, w_expert_gate, w_expert_up, w_expert_down, norm_final)
    meta = _meta_streams(meta_tokens, w)
    outs = []
    for x in (x_prompt, x_sample):
        h1, hn, route, ids, counts = _token_mixers(x, meta, w, tm=512, tq=1024, tk=2048, tb=512, tmix=1024)
        outs.append(_moe(hn, route, ids, counts, h1, w).reshape(x.shape))
    return tuple(outs)
```

```python
import functools

import numpy as np
import jax
import jax.numpy as jnp
from jax import lax
from jax.experimental import pallas as pl
from jax.experimental.pallas import tpu as pltpu
from jax.experimental.pallas import tpu_sc as plsc

F32 = jnp.float32
BF16 = jnp.bfloat16

D_MODEL = 1024
N_META = 16
MLA_HEADS = 4
MLA_Q_RANK = 384
MLA_KV_RANK = 256
MLA_NOPE = 128
MLA_ROPE = 64
MLA_V = 128
ROPE_THETA = 10000.0
GLA_HEADS = 4
GLA_DK = 64
GLA_DV = 128
GLA_GATE_RANK = 16
GLA_TAU = 16.0
GLA_CHUNK = 64
N_GROUPS = 4
EXPERTS_PER_GROUP = 8
N_EXPERTS = N_GROUPS * EXPERTS_PER_GROUP
D_EXPERT = 256
EPS = 1e-6

LANES = 128
V7X_VMEM_BYTES = 64 * 1024 * 1024
VMEM_LIMIT = V7X_VMEM_BYTES * 7 // 8

ATTN_SCALE = (MLA_NOPE + MLA_ROPE) ** -0.5 * float(np.log2(np.e))
QK_WIDTH = 2 * LANES
V_WIDTH = 2 * LANES
ATTN_GROUP = 8

C_CQ = 0
C_CKV = C_CQ + MLA_Q_RANK
C_KPE = C_CKV + MLA_KV_RANK
C_GQ = C_KPE + LANES
C_GK = C_GQ + GLA_HEADS * GLA_DK
C_GV = C_GK + GLA_HEADS * GLA_DK
C_OG = C_GV + GLA_HEADS * GLA_DV
C_LR = C_OG + GLA_HEADS * GLA_DV
D_IN_PACKED = C_LR + LANES

ROUTER_GROUP_LANE0 = 0
ROUTER_EXPERT_LANE0 = N_GROUPS


def _cparams(*semantics):
    return pltpu.CompilerParams(dimension_semantics=semantics, vmem_limit_bytes=VMEM_LIMIT)


def _rms(x, g):
    return x * lax.rsqrt(jnp.mean(x * x, axis=-1, keepdims=True) + EPS) * g


def _dot(a, b):
    return jnp.dot(a, b, preferred_element_type=F32)


def _dot_nt(a, b):
    return lax.dot_general(a, b, (((1,), (1,)), ((), ())), preferred_element_type=F32)


def _dot_tn(a, b):
    return lax.dot_general(a, b, (((0,), (0,)), ((), ())), preferred_element_type=F32)


def _full_spec(shape):
    return pl.BlockSpec(shape, lambda *_: (0,) * len(shape))


ROW_CHUNKS = D_MODEL // LANES


def _rows_to_tiles(x):
    chunks = jnp.stack([x[:, s * LANES:(s + 1) * LANES] for s in range(x.shape[1] // LANES)], axis=0)
    return pltpu.einshape("smd->msd", chunks)


def _tiles_to_rows(x):
    chunks = pltpu.einshape("msd->smd", x)
    return jnp.concatenate([chunks[s] for s in range(x.shape[1])], axis=-1)


PACKED_CHUNKS = ROW_CHUNKS // 2


def _pack_bf16_pairs(x):
    half = x.shape[1] // 2
    bits = lambda v: lax.bitcast_convert_type(v.astype(BF16).astype(F32), jnp.uint32)
    return (bits(x[:, half:]) & jnp.uint32(0xFFFF0000)) | (bits(x[:, :half]) >> 16)


def _unpack_bf16_pairs(w):
    lo = lax.bitcast_convert_type(w << 16, F32)
    hi = lax.bitcast_convert_type(w & jnp.uint32(0xFFFF0000), F32)
    return jnp.concatenate([lo, hi], axis=-1)


def _rope_pairs(x, cos, sin_signed, first_half):
    swapped = jnp.where(first_half, pltpu.roll(x, LANES - MLA_ROPE // 2, 1), pltpu.roll(x, MLA_ROPE // 2, 1))
    return x * cos + swapped * sin_signed


def _inproj_body(x_ref, cos_ref, sin_ref, nmix_ref, win_ref, qan_ref, wuq_ref, kvan_ref, wukv_ref,
                 wgate_ref, bgate_ref,
                 q_ref, k_ref, v_ref, gq_ref, gk_ref, gv_ref, gf_ref, gb_ref, og_ref):
    hn = _rms(x_ref[...], nmix_ref[...]).astype(BF16)

    def proj(lo, hi):
        return _dot(hn, win_ref[:, lo:hi])

    cos = cos_ref[...]
    sin = sin_ref[...]
    lane = lax.broadcasted_iota(jnp.int32, cos.shape, 1)
    first_half = (lane & (MLA_ROPE - 1)) < MLA_ROPE // 2
    low_lanes = lane < MLA_ROPE

    cq = _rms(proj(C_CQ, C_CKV), qan_ref[...]).astype(BF16)
    qn = _dot(cq, wuq_ref[:, 0:MLA_HEADS * MLA_NOPE]) * ATTN_SCALE
    qr = _dot(cq, wuq_ref[:, MLA_HEADS * MLA_NOPE:])
    for j in range(MLA_HEADS // 2):
        rj = (_rope_pairs(qr[:, j * LANES:(j + 1) * LANES], cos, sin, first_half) * ATTN_SCALE).astype(BF16)
        for h in (2 * j, 2 * j + 1):
            q_ref[h, :, 0:LANES] = qn[:, h * LANES:(h + 1) * LANES].astype(BF16)
            q_ref[h, :, LANES:QK_WIDTH] = rj

    ckv = _rms(proj(C_CKV, C_KPE), kvan_ref[...]).astype(BF16)
    kv = _dot(ckv, wukv_ref[...])
    kr = _rope_pairs(proj(C_KPE, C_GQ), cos, sin, first_half)
    kr_even = jnp.where(low_lanes, kr, 0.0).astype(BF16)
    kr_odd = jnp.where(low_lanes, 0.0, kr).astype(BF16)
    for h in range(MLA_HEADS):
        base = h * (MLA_NOPE + MLA_V)
        k_ref[h, :, 0:LANES] = kv[:, base:base + MLA_NOPE].astype(BF16)
        k_ref[h, :, LANES:QK_WIDTH] = kr_even if h % 2 == 0 else kr_odd
        v_ref[h, :, 0:MLA_V] = kv[:, base + MLA_NOPE:base + MLA_NOPE + MLA_V].astype(BF16)
        v_ref[h, :, MLA_V:V_WIDTH] = jnp.ones((kv.shape[0], V_WIDTH - MLA_V), BF16)

    gq_ref[...] = (proj(C_GQ, C_GK) * (GLA_DK ** -0.5)).astype(BF16)
    gk_ref[...] = proj(C_GK, C_GV).astype(BF16)
    gv_ref[...] = proj(C_GV, C_OG).astype(BF16)
    og_ref[...] = proj(C_OG, C_LR).astype(BF16)
    pre = _dot(proj(C_LR, D_IN_PACKED).astype(BF16), wgate_ref[...]) + bgate_ref[...]
    logsig = jnp.minimum(pre, 0.0) - jnp.log1p(jnp.exp(-jnp.abs(pre)))
    gates = logsig * (1.0 / GLA_TAU)
    gf_ref[...] = gates[:, 0:GLA_HEADS * GLA_DK]
    gb_ref[...] = gates[:, GLA_HEADS * GLA_DK:]


def _inproj(x2d, cos, sin, w, tm):
    t = x2d.shape[0]
    blocks_per_seq = cos.shape[0] // tm
    hk = GLA_HEADS * GLA_DK
    hv = GLA_HEADS * GLA_DV
    row = lambda width: pl.BlockSpec((tm, width), lambda i: (i, 0))
    head_rows = lambda width: pl.BlockSpec((MLA_HEADS, tm, width), lambda i: (0, i, 0))
    tab = pl.BlockSpec((tm, LANES), lambda i: (i % blocks_per_seq, 0))
    out_shape = (
        jax.ShapeDtypeStruct((MLA_HEADS, t, QK_WIDTH), BF16),
        jax.ShapeDtypeStruct((MLA_HEADS, t, QK_WIDTH), BF16),
        jax.ShapeDtypeStruct((MLA_HEADS, t, V_WIDTH), BF16),
        jax.ShapeDtypeStruct((t, hk), BF16),
        jax.ShapeDtypeStruct((t, hk), BF16),
        jax.ShapeDtypeStruct((t, hv), BF16),
        jax.ShapeDtypeStruct((t, hk), F32),
        jax.ShapeDtypeStruct((t, hk), F32),
        jax.ShapeDtypeStruct((t, hv), BF16),
    )
    return pl.pallas_call(
        _inproj_body,
        grid=(t // tm,),
        in_specs=[row(D_MODEL), tab, tab,
                  _full_spec((1, D_MODEL)), _full_spec((D_MODEL, D_IN_PACKED)),
                  _full_spec((1, MLA_Q_RANK)), _full_spec(w['wuq'].shape),
                  _full_spec((1, MLA_KV_RANK)), _full_spec(w['wukv'].shape),
                  _full_spec(w['wgate'].shape), _full_spec(w['bgate'].shape)],
        out_specs=(head_rows(QK_WIDTH), head_rows(QK_WIDTH), head_rows(V_WIDTH),
                   row(hk), row(hk), row(hv), row(hk), row(hk), row(hv)),
        out_shape=out_shape,
        compiler_params=_cparams("parallel"),
        name="inproj",
    )(x2d, cos, sin, w['norm_mix'], w['win'], w['q_a_norm'], w['wuq'], w['kv_a_norm'], w['wukv'],
      w['wgate'], w['bgate'])


def _attn_body(q_ref, k_ref, v_ref, km_ref, vm_ref, o_ref, s_ref, acc_ref, *, tk):
    q = q_ref[...]
    n_blocks = k_ref.shape[0] // tk
    group_size = min(ATTN_GROUP, n_blocks)
    assert n_blocks % group_size == 0 and group_size % 2 == 0
    n_groups = n_blocks // group_size

    def scores(j):
        return _dot_nt(q, k_ref[pl.ds(pl.multiple_of(j * tk, tk), tk), :])

    def values(j):
        return v_ref[pl.ds(pl.multiple_of(j * tk, tk), tk), :]

    def absorb(m, s, v):
        m_new = jnp.maximum(m, jnp.max(s, axis=-1, keepdims=True))
        p = jnp.exp2(s - m_new)
        acc_ref[...] = jnp.exp2(m - m_new) * acc_ref[...] + _dot(p.astype(BF16), v)
        return m_new

    s_ref[0] = scores(0)
    sm = _dot_nt(q, km_ref[...])
    sm = jnp.where(lax.broadcasted_iota(jnp.int32, sm.shape, 1) < N_META, sm, -jnp.inf)
    m = jnp.max(sm, axis=-1, keepdims=True)
    acc_ref[...] = _dot(jnp.exp2(sm - m).astype(BF16), vm_ref[...])

    def group(g, m, last):
        for i in range(group_size):
            j = group_size * g + i
            s = s_ref[i % 2]
            if not (last and i == group_size - 1):
                s_ref[(i + 1) % 2] = scores(j + 1)
            m = absorb(m, s, values(j))
        return m

    m = lax.fori_loop(0, n_groups - 1, lambda g, m: group(g, m, False), m)
    group(n_groups - 1, m, True)
    acc = acc_ref[...]
    o_ref[...] = (acc[:, :MLA_V] / acc[:, MLA_V:]).astype(o_ref.dtype)


def _attention(q, k, v, km, vm, bsz, seq, tq, tk):
    nq = seq // tq
    return pl.pallas_call(
        functools.partial(_attn_body, tk=tk),
        grid=(bsz, MLA_HEADS, nq),
        in_specs=[pl.BlockSpec((None, tq, QK_WIDTH), lambda b, h, i: (h, b * nq + i, 0)),
                  pl.BlockSpec((None, seq, QK_WIDTH), lambda b, h, i: (h, b, 0)),
                  pl.BlockSpec((None, seq, V_WIDTH), lambda b, h, i: (h, b, 0)),
                  pl.BlockSpec((None, LANES, QK_WIDTH), lambda b, h, i: (h, 0, 0)),
                  pl.BlockSpec((None, LANES, V_WIDTH), lambda b, h, i: (h, 0, 0))],
        out_specs=pl.BlockSpec((tq, MLA_V), lambda b, h, i: (b * nq + i, h)),
        out_shape=jax.ShapeDtypeStruct((bsz * seq, MLA_HEADS * MLA_V), BF16),
        scratch_shapes=[pltpu.VMEM((2, tq, tk), F32), pltpu.VMEM((tq, V_WIDTH), F32)],
        compiler_params=_cparams("parallel", "parallel", "arbitrary"),
        name="mla_attention",
    )(q, k, v, km, vm)


def _split3(x):
    hi = x.astype(BF16)
    r1 = x - hi.astype(F32)
    mid = r1.astype(BF16)
    lo = (r1 - mid.astype(F32)).astype(BF16)
    return hi, mid, lo


def _gla_log_decay(g, tri):
    g_hi, g_mid, g_lo = _split3(g)
    return _dot(tri, g_hi) + _dot(tri, g_mid) + _dot(tri, g_lo)


def _head_rows(x):
    even = lax.broadcasted_iota(jnp.int32, x.shape, 1) < GLA_DK
    return jnp.concatenate([jnp.where(even, x, 0.0), jnp.where(even, 0.0, x)], axis=0).astype(BF16)


def _gla_operands(q, k, b, mid, last):
    b_last = b[last:last + 1, :]
    ke = (k * jnp.exp(b_last - b)).astype(BF16)
    if q is None:
        return jnp.exp(b_last), ke, None, None, None
    b_mid = b[mid:mid + 1, :]
    ks = (k * jnp.exp(b_mid - b)).astype(BF16)
    return jnp.exp(b_last), ke, ks, _head_rows(q * jnp.exp(b - b_mid)), _head_rows(q * jnp.exp(b))


def _gla_state_update(v_even, v_odd, ke):
    return jnp.where(lax.broadcasted_iota(jnp.int32, (GLA_DV, LANES), 1) < GLA_DK,
                     _dot_tn(v_even, ke), _dot_tn(v_odd, ke))


def _gla_intra(v_even, v_odd, ks, qs2, keep):
    scores = jnp.where(keep, _dot_nt(qs2, ks), 0.0).astype(BF16)
    return _dot(scores[:GLA_CHUNK], v_even), _dot(scores[GLA_CHUNK:], v_odd)


def _gla_body(qf_ref, kf_ref, vf_ref, gf_ref, qb_ref, kb_ref, vb_ref, gb_ref, mk_ref, mv_ref, mg_ref,
              of_ref, ob_ref, state_ref):
    n_chunks = qf_ref.shape[0] // GLA_CHUNK
    n_pairs = GLA_HEADS // 2
    r = lax.broadcasted_iota(jnp.int32, (GLA_CHUNK, GLA_CHUNK), 0)
    c = lax.broadcasted_iota(jnp.int32, (GLA_CHUNK, GLA_CHUNK), 1)
    tri_f = jnp.where(c <= r, 1.0, 0.0).astype(BF16)
    tri_b = jnp.where(c >= r, 1.0, 0.0).astype(BF16)
    r2 = lax.broadcasted_iota(jnp.int32, (2 * GLA_CHUNK, GLA_CHUNK), 0) & (GLA_CHUNK - 1)
    c2 = lax.broadcasted_iota(jnp.int32, (2 * GLA_CHUNK, GLA_CHUNK), 1)
    keep_f = c2 <= r2
    keep_b = c2 >= r2
    mid_f, last_f = GLA_CHUNK // 2 - 1, GLA_CHUNK - 1
    mid_b, last_b = GLA_CHUNK // 2, 0

    def pair_cols(p):
        return slice(p * LANES, (p + 1) * LANES)

    def head_cols(h):
        return slice(h * GLA_DV, (h + 1) * GLA_DV)

    @pl.when(pl.program_id(1) == 0)
    def _():
        for p in range(n_pairs):
            b = _gla_log_decay(mg_ref[:, pair_cols(p)], tri_f)
            _, ke, _, _, _ = _gla_operands(None, mk_ref[:, pair_cols(p)], b, mid_f, last_f)
            state_ref[p] = _gla_state_update(mv_ref[:, head_cols(2 * p)].astype(BF16),
                                             mv_ref[:, head_cols(2 * p + 1)].astype(BF16), ke)
            state_ref[n_pairs + p] = jnp.zeros((GLA_DV, LANES), F32)

    scans = []
    for p in range(n_pairs):
        scans.append((p, list(range(n_chunks)), qf_ref, kf_ref, vf_ref, gf_ref, of_ref, p,
                      tri_f, keep_f, mid_f, last_f))
        scans.append((p, list(reversed(range(n_chunks))), qb_ref, kb_ref, vb_ref, gb_ref, ob_ref, n_pairs + p,
                      tri_b, keep_b, mid_b, last_b))

    def rows(c):
        return slice(c * GLA_CHUNK, (c + 1) * GLA_CHUNK)

    def values(v_ref, p, c):
        return (v_ref[rows(c), head_cols(2 * p)].astype(BF16), v_ref[rows(c), head_cols(2 * p + 1)].astype(BF16))

    log_decay = [[_gla_log_decay(g_ref[rows(c), pair_cols(p)], tri) for c in order]
                 for (p, order, _, _, _, g_ref, _, _, tri, _, _, _) in scans]
    operands = [[_gla_operands(q_ref[rows(c), pair_cols(p)], k_ref[rows(c), pair_cols(p)], b, mid, last)
                 for c, b in zip(order, bs)]
                for (p, order, q_ref, k_ref, _, _, _, _, _, _, mid, last), bs in zip(scans, log_decay)]
    updates = [[_gla_state_update(*values(v_ref, p, c), ops[1]) for c, ops in zip(order, opss)]
               for (p, order, _, _, v_ref, _, _, _, _, _, _, _), opss in zip(scans, operands)]
    intra = [[_gla_intra(*values(v_ref, p, c), ops[2], ops[3], keep) for c, ops in zip(order, opss)]
             for (p, order, _, _, v_ref, _, _, _, _, keep, _, _), opss in zip(scans, operands)]
    states = []
    for (_, order, _, _, _, _, _, slot, _, _, _, _), opss, upds in zip(scans, operands, updates):
        st = state_ref[slot]
        entering = []
        for ops, upd in zip(opss, upds):
            entering.append(st.astype(BF16))
            st = st * ops[0] + upd
        state_ref[slot] = st
        states.append(entering)
    for (p, order, _, _, _, _, o_ref, _, _, _, _, _), opss, sts, locs in zip(scans, operands, states, intra):
        for c, ops, st, (o_even, o_odd) in zip(order, opss, sts, locs):
            inter = _dot_nt(ops[4], st)
            o_ref[rows(c), head_cols(2 * p)] = (o_even + inter[:GLA_CHUNK]).astype(o_ref.dtype)
            o_ref[rows(c), head_cols(2 * p + 1)] = (o_odd + inter[GLA_CHUNK:]).astype(o_ref.dtype)


def _gla(gq, gk, gv, gf, gb, mk, mv, mg, bsz, seq, tb):
    nb = seq // tb
    hk = GLA_HEADS * GLA_DK
    hv = GLA_HEADS * GLA_DV
    fwd = lambda width: pl.BlockSpec((tb, width), lambda b, j: (b * nb + j, 0))
    bwd = lambda width: pl.BlockSpec((tb, width), lambda b, j: (b * nb + nb - 1 - j, 0))
    t = bsz * seq
    return pl.pallas_call(
        _gla_body,
        grid=(bsz, nb),
        in_specs=[fwd(hk), fwd(hk), fwd(hv), fwd(hk), bwd(hk), bwd(hk), bwd(hv), bwd(hk),
                  _full_spec(mk.shape), _full_spec(mv.shape), _full_spec(mg.shape)],
        out_specs=(fwd(hv), bwd(hv)),
        out_shape=(jax.ShapeDtypeStruct((t, hv), BF16), jax.ShapeDtypeStruct((t, hv), BF16)),
        scratch_shapes=[pltpu.VMEM((2 * (GLA_HEADS // 2), GLA_DV, LANES), F32)],
        compiler_params=_cparams("parallel", "arbitrary"),
        name="gla_scan",
    )(gq, gk, gv, gf, gq, gk, gv, gb, mk, mv, mg)


def _mix_body(x_ref, a_ref, of_ref, ob_ref, og_ref, gnorm_ref, wout_ref, nffn_ref, wr_ref, br_ref,
              h1_ref, hn_ref, route_ref, ids_ref, count_ref, tri_ref):
    tm = x_ref.shape[0]

    @pl.when(pl.program_id(0) == 0)
    def _():
        r = lax.broadcasted_iota(jnp.int32, (tm, tm), 0)
        c = lax.broadcasted_iota(jnp.int32, (tm, tm), 1)
        tri_ref[...] = jnp.where(c < r, 1.0, 0.0).astype(BF16)
        count_ref[...] = jnp.zeros_like(count_ref)

    a_width = MLA_HEADS * MLA_V
    gated = []
    for h in range(GLA_HEADS):
        cols = slice(h * GLA_DV, (h + 1) * GLA_DV)
        o = of_ref[:, cols].astype(F32) + ob_ref[:, cols].astype(F32)
        og = og_ref[:, cols].astype(F32)
        silu = og / (1.0 + jnp.exp(-og))
        gated.append((_rms(o, gnorm_ref[...]) * silu).astype(BF16))
    h1 = (x_ref[...] + _dot(a_ref[...], wout_ref[0:a_width, :])
          + _dot(jnp.concatenate(gated, axis=-1), wout_ref[a_width:, :]))
    h1_ref[...] = h1
    hn = _rms(h1, nffn_ref[...])
    hn_ref[...] = _rows_to_tiles(_pack_bf16_pairs(hn))

    hn_hi = hn.astype(BF16)
    hn_lo = (hn - hn_hi.astype(F32)).astype(BF16)
    hi_terms = _dot(hn_hi, wr_ref[...])
    logits = hi_terms[:, :LANES] + hi_terms[:, LANES:] + _dot(hn_lo, wr_ref[:, :LANES]) + br_ref[...]
    lane = lax.broadcasted_iota(jnp.int32, logits.shape, 1).astype(F32)
    none = float(LANES)
    neg = -jnp.inf

    def lane_max(x):
        return jnp.max(x, axis=-1, keepdims=True)

    def lane_sum(x):
        return jnp.sum(x, axis=-1, keepdims=True)

    def first_lane(mask):
        return jnp.min(jnp.where(mask, lane, none), axis=-1, keepdims=True)

    is_group = lane < float(N_GROUPS)
    g_max = lane_max(jnp.where(is_group, logits, neg))
    g_exp = jnp.where(is_group, jnp.exp(logits - g_max), 0.0)
    g_prob = g_exp / lane_sum(g_exp)
    g_w = lane_max(g_prob)
    g_idx = first_lane(is_group & (g_prob == g_w))
    e_lo = float(ROUTER_EXPERT_LANE0) + float(EXPERTS_PER_GROUP) * g_idx
    sel = (lane >= e_lo) & (lane < e_lo + float(EXPERTS_PER_GROUP))
    e_max = lane_max(jnp.where(sel, logits, neg))
    e_exp = jnp.where(sel, jnp.exp(logits - e_max), 0.0)
    e_prob = e_exp / lane_sum(e_exp)
    p1 = lane_max(jnp.where(sel, e_prob, neg))
    i1 = first_lane(sel & (e_prob == p1))
    rest = sel & (lane != i1)
    p2 = lane_max(jnp.where(rest, e_prob, neg))
    i2 = first_lane(rest & (e_prob == p2))
    denom = p1 + p2
    chosen = jnp.where((lane == i1) | (lane == i2), 1.0, 0.0)
    rank = count_ref[...] + _dot(tri_ref[...], chosen.astype(BF16))
    count_ref[...] += jnp.sum(chosen, axis=0, keepdims=True)
    fields = (i1 - float(ROUTER_EXPERT_LANE0), i2 - float(ROUTER_EXPERT_LANE0),
              lane_sum(jnp.where(lane == i1, rank, 0.0)), lane_sum(jnp.where(lane == i2, rank, 0.0)),
              g_w * (p1 / denom), g_w * (p2 / denom))
    route = jnp.zeros_like(logits)
    for k, value in enumerate(fields):
        route = jnp.where(lane == float(k), value, route)
    route_ref[...] = route
    ids_ref[...] = jnp.transpose(route)[0:ROUTE_ID_ROWS, :].astype(jnp.int32)


ROUTE_E1, ROUTE_E2, ROUTE_RANK1, ROUTE_RANK2, ROUTE_W1, ROUTE_W2 = range(6)
ROUTE_ID_ROWS = 8


def _mix(x2d, a, o_f, o_b, og, w, tm):
    t = x2d.shape[0]
    hv = GLA_HEADS * GLA_DV
    row = lambda width: pl.BlockSpec((tm, width), lambda i: (i, 0))
    return pl.pallas_call(
        _mix_body,
        grid=(t // tm,),
        in_specs=[row(D_MODEL), row(MLA_HEADS * MLA_V), row(hv), row(hv), row(hv),
                  _full_spec((1, GLA_DV)), _full_spec(w['wout'].shape), _full_spec((1, D_MODEL)),
                  _full_spec(w['wr'].shape), _full_spec(w['br'].shape)],
        out_specs=(row(D_MODEL), pl.BlockSpec((tm, PACKED_CHUNKS, LANES), lambda i: (i, 0, 0)), row(LANES),
                   pl.BlockSpec((ROUTE_ID_ROWS, tm), lambda i: (0, i)), _full_spec((1, LANES))),
        out_shape=(jax.ShapeDtypeStruct((t, D_MODEL), F32), jax.ShapeDtypeStruct((t, PACKED_CHUNKS, LANES), jnp.uint32),
                   jax.ShapeDtypeStruct((t, LANES), F32), jax.ShapeDtypeStruct((ROUTE_ID_ROWS, t), jnp.int32),
                   jax.ShapeDtypeStruct((1, LANES), F32)),
        scratch_shapes=[pltpu.VMEM((tm, tm), BF16)],
        compiler_params=_cparams("arbitrary"),
        name="mix_router",
    )(x2d, a, o_f, o_b, og, w['gla_norm'], w['wout'], w['norm_ffn'], w['wr'], w['br'])


EXPERT_TILE = 256
ROW_TILE = 1024


SC_CORES = 2
SC_SUBCORES = 16
SC_GATHER_ROWS = 32


def _sc_gather(table, idx):
    n = idx.shape[0]
    workers = SC_CORES * SC_SUBCORES
    per_worker = n // workers
    assert n % (workers * SC_GATHER_ROWS) == 0
    mesh = plsc.VectorSubcoreMesh(core_axis_name="c", subcore_axis_name="s")

    @functools.partial(
        pl.kernel, mesh=mesh,
        out_type=jax.ShapeDtypeStruct((n,) + table.shape[1:], table.dtype),
        scratch_types=[pltpu.VMEM((SC_GATHER_ROWS,), jnp.int32),
                       pltpu.VMEM((SC_GATHER_ROWS,) + table.shape[1:], table.dtype),
                       pltpu.SemaphoreType.DMA])
    def gather(table_ref, idx_ref, out_ref, idx_buf, rows_buf, sem):
        base = (lax.axis_index("s") * SC_CORES + lax.axis_index("c")) * per_worker

        @pl.loop(0, per_worker // SC_GATHER_ROWS)
        def _(j):
            rows = pl.ds(base + j * SC_GATHER_ROWS, SC_GATHER_ROWS)
            pltpu.sync_copy(idx_ref.at[rows], idx_buf)
            pltpu.async_copy(table_ref.at[idx_buf], rows_buf, sem).wait()
            pltpu.sync_copy(rows_buf, out_ref.at[rows])

    return gather(table, idx)


def _sc_scatter(rows, idx, n_out):
    copies, n = idx.shape
    idx = idx.reshape(copies * n)
    workers = SC_CORES * SC_SUBCORES
    per_worker = n // workers
    assert n % (workers * SC_GATHER_ROWS) == 0
    mesh = plsc.VectorSubcoreMesh(core_axis_name="c", subcore_axis_name="s")

    @functools.partial(
        pl.kernel, mesh=mesh,
        out_type=jax.ShapeDtypeStruct((n_out,) + rows.shape[1:], rows.dtype),
        scratch_types=[pltpu.VMEM((SC_GATHER_ROWS,), jnp.int32),
                       pltpu.VMEM((SC_GATHER_ROWS,) + rows.shape[1:], rows.dtype),
                       pltpu.SemaphoreType.DMA])
    def scatter(rows_ref, idx_ref, out_ref, idx_buf, rows_buf, sem):
        base = (lax.axis_index("s") * SC_CORES + lax.axis_index("c")) * per_worker

        @pl.loop(0, per_worker // SC_GATHER_ROWS)
        def _(j):
            first = base + j * SC_GATHER_ROWS
            pltpu.sync_copy(rows_ref.at[pl.ds(first, SC_GATHER_ROWS)], rows_buf)
            for k in range(copies):
                pltpu.sync_copy(idx_ref.at[pl.ds(k * n + first, SC_GATHER_ROWS)], idx_buf)
                pltpu.async_copy(rows_buf, out_ref.at[idx_buf], sem).wait()

    return scatter(rows, idx)


SLOT_TILE = 2048


def _slots_body(starts_ref, ids_ref, slots_ref):
    ids = ids_ref[...]
    experts = ids[ROUTE_E1:ROUTE_E1 + 2, :]
    start = jnp.zeros_like(experts)
    for e in range(N_EXPERTS):
        start = jnp.where(experts == e, starts_ref[e], start)
    slots_ref[...] = start + ids[ROUTE_RANK1:ROUTE_RANK1 + 2, :]


def _slots(starts, ids):
    t = ids.shape[1]
    return pl.pallas_call(
        _slots_body,
        grid_spec=pltpu.PrefetchScalarGridSpec(
            num_scalar_prefetch=1,
            grid=(t // SLOT_TILE,),
            in_specs=[pl.BlockSpec((ROUTE_ID_ROWS, SLOT_TILE), lambda i, *_: (0, i))],
            out_specs=pl.BlockSpec((2, SLOT_TILE), lambda i, *_: (0, i))),
        out_shape=jax.ShapeDtypeStruct((2, t), jnp.int32),
        compiler_params=_cparams("parallel"),
        name="moe_slots",
    )(starts, ids)


def _queue_body(slots_ref, token_ref, *, n_tokens):
    i = pl.program_id(0)

    @pl.when(i == 0)
    def _():
        for base in range(0, token_ref.shape[0], n_tokens):
            def clear(j, carry):
                token_ref[base + j] = j
                return carry
            lax.fori_loop(0, min(n_tokens, token_ref.shape[0] - base), clear, 0, unroll=8)

    def place(r, carry):
        for k in range(2):
            token_ref[slots_ref[k, r]] = i * SLOT_TILE + r
        return carry

    lax.fori_loop(0, SLOT_TILE, place, 0, unroll=8)


def _queue_tokens(slots, n_rows):
    t = slots.shape[1]
    return pl.pallas_call(
        functools.partial(_queue_body, n_tokens=t),
        grid=(t // SLOT_TILE,),
        in_specs=[pl.BlockSpec((2, SLOT_TILE), lambda i: (0, i), memory_space=pltpu.SMEM)],
        out_specs=pl.BlockSpec(memory_space=pltpu.SMEM),
        out_shape=jax.ShapeDtypeStruct((n_rows,), jnp.int32),
        compiler_params=_cparams("arbitrary"),
        name="moe_queue",
    )(slots)


TILES_PER_STEP = 4


def _expert_body(tile_expert_ref, n_tiles_ref, tile_valid_ref, xs_ref, *refs):
    wg_refs = refs[0:TILES_PER_STEP]
    wu_refs = refs[TILES_PER_STEP:2 * TILES_PER_STEP]
    wd_refs = refs[2 * TILES_PER_STEP:3 * TILES_PER_STEP]
    ys_ref = refs[3 * TILES_PER_STEP]
    first = pl.program_id(0) * TILES_PER_STEP
    used = first < n_tiles_ref[0]

    @pl.when(used)
    def _():
        rows = [slice(j * EXPERT_TILE, (j + 1) * EXPERT_TILE) for j in range(TILES_PER_STEP)]
        xs = [_unpack_bf16_pairs(_tiles_to_rows(xs_ref[r])) for r in rows]
        xs = [jnp.where(lax.broadcasted_iota(jnp.int32, x.shape, 0) < tile_valid_ref[first + j], x, 0.0)
              for j, x in enumerate(xs)]
        xs = [x.astype(BF16) for x in xs]
        gates = [_dot(x, wg[...]) for x, wg in zip(xs, wg_refs)]
        ups = [_dot(x, wu[...]) for x, wu in zip(xs, wu_refs)]
        acts = [(g / (1.0 + jnp.exp(-g))) * u for g, u in zip(gates, ups)]
        ys = [_dot(act.astype(BF16), wd[...]) for act, wd in zip(acts, wd_refs)]
        for r, y in zip(rows, ys):
            ys_ref[r] = _rows_to_tiles(_pack_bf16_pairs(y))

    @pl.when(jnp.logical_not(used))
    def _():
        ys_ref[...] = jnp.zeros_like(ys_ref)


def _experts(tile_expert, n_tiles, tile_valid, xs, w):
    n_steps = xs.shape[0] // (EXPERT_TILE * TILES_PER_STEP)

    def step(i, nt):
        return jnp.minimum(i, (nt[0] - 1) // TILES_PER_STEP)

    def expert(i, j, te, nt):
        return te[jnp.minimum(step(i, nt) * TILES_PER_STEP + j, nt[0] - 1)]

    tiles = lambda index: pl.BlockSpec((EXPERT_TILE * TILES_PER_STEP, PACKED_CHUNKS, LANES), index)
    up_spec = lambda j: pl.BlockSpec((None, D_MODEL, D_EXPERT), lambda i, te, nt, tv: (expert(i, j, te, nt), 0, 0))
    down_spec = lambda j: pl.BlockSpec((None, D_EXPERT, D_MODEL), lambda i, te, nt, tv: (expert(i, j, te, nt), 0, 0))
    slots = range(TILES_PER_STEP)
    return pl.pallas_call(
        _expert_body,
        grid_spec=pltpu.PrefetchScalarGridSpec(
            num_scalar_prefetch=3,
            grid=(n_steps,),
            in_specs=([tiles(lambda i, te, nt, tv: (step(i, nt), 0, 0))]
                      + [up_spec(j) for j in slots] + [up_spec(j) for j in slots] + [down_spec(j) for j in slots]),
            out_specs=tiles(lambda i, te, nt, tv: (i, 0, 0))),
        out_shape=jax.ShapeDtypeStruct(xs.shape, xs.dtype),
        compiler_params=_cparams("arbitrary"),
        name="moe_experts",
    )(tile_expert, n_tiles, tile_valid, xs, *([w['w_gate']] * TILES_PER_STEP), *([w['w_up']] * TILES_PER_STEP),
      *([w['w_down']] * TILES_PER_STEP))


def _combine_body(h1_ref, route_ref, nfin_ref, y1_ref, y2_ref, out_ref):
    route = route_ref[...]
    lane = lax.broadcasted_iota(jnp.int32, route.shape, 1)
    w1 = jnp.sum(jnp.where(lane == ROUTE_W1, route, 0.0), axis=-1, keepdims=True)
    w2 = jnp.sum(jnp.where(lane == ROUTE_W2, route, 0.0), axis=-1, keepdims=True)
    y1 = _unpack_bf16_pairs(_tiles_to_rows(y1_ref[...]))
    y2 = _unpack_bf16_pairs(_tiles_to_rows(y2_ref[...]))
    y = w1 * y1 + w2 * y2
    out_ref[...] = _rms(h1_ref[...] + y, nfin_ref[...])


def _combine(h1, route, y12, w):
    t = h1.shape[0]
    row = lambda width: pl.BlockSpec((ROW_TILE, width), lambda i: (i, 0))
    tiles = lambda k: pl.BlockSpec((None, ROW_TILE, PACKED_CHUNKS, LANES), lambda i: (k, i, 0, 0))
    return pl.pallas_call(
        _combine_body,
        grid=(t // ROW_TILE,),
        in_specs=[row(D_MODEL), row(LANES), _full_spec((1, D_MODEL)), tiles(0), tiles(1)],
        out_specs=row(D_MODEL),
        out_shape=jax.ShapeDtypeStruct((t, D_MODEL), F32),
        compiler_params=_cparams("parallel"),
        name="moe_combine",
    )(h1, route, w['norm_final'], y12, y12)


def _moe(hn, route, ids, counts, h1, w):
    t = hn.shape[0]
    n_rows = 2 * t + N_EXPERTS * EXPERT_TILE
    n_grid_tiles = n_rows // EXPERT_TILE
    count = counts[0, ROUTER_EXPERT_LANE0:ROUTER_EXPERT_LANE0 + N_EXPERTS].astype(jnp.int32)
    padded = jnp.maximum((count + EXPERT_TILE - 1) // EXPERT_TILE, 1) * EXPERT_TILE
    ends = jnp.sum(jnp.where(jnp.arange(N_EXPERTS)[:, None] <= jnp.arange(N_EXPERTS)[None, :], padded[:, None], 0),
                   axis=0)
    starts = ends - padded
    n_tiles = (ends[-1:] // EXPERT_TILE)
    tile_rows = jnp.arange(n_grid_tiles, dtype=jnp.int32) * EXPERT_TILE
    tile_expert = jnp.minimum(jnp.sum((ends[None, :] <= tile_rows[:, None]).astype(jnp.int32), axis=1), N_EXPERTS - 1)
    tile_valid = jnp.clip(jnp.sum(jnp.where(jnp.arange(N_EXPERTS)[None, :] == tile_expert[:, None],
                                            (starts + count)[None, :], 0), axis=1) - tile_rows, 0, EXPERT_TILE)
    slots = _slots(starts, ids)
    ys = _experts(tile_expert, n_tiles, tile_valid, _sc_scatter(hn, slots, n_rows), w)
    y12 = _sc_gather(ys, slots.reshape(2 * t)).reshape(2, t, PACKED_CHUNKS, LANES)
    return _combine(h1, route, y12, w)


def _rope_tables(positions):
    inv_freq = 1.0 / (ROPE_THETA ** (jnp.arange(0, MLA_ROPE, 2, dtype=F32) / MLA_ROPE))
    ang = positions.astype(F32)[:, None] * inv_freq[None, :]
    cos, sin = jnp.cos(ang), jnp.sin(ang)
    reps = LANES // MLA_ROPE
    return jnp.tile(jnp.concatenate([cos, cos], axis=-1), (1, reps)), jnp.tile(jnp.concatenate([-sin, sin], axis=-1), (1, reps))


def _pack_weights(norm_mix, w_in, q_a_norm, w_uq, kv_a_norm, w_ukv, w_gate_fwd, b_gate_fwd, w_gate_bwd, b_gate_bwd,
                  gla_norm, w_out, norm_ffn, w_router_group, b_router_group, w_router_expert, b_router_expert,
                  w_expert_gate, w_expert_up, w_expert_down, norm_final):
    l = 0
    hk = GLA_HEADS * GLA_DK
    hv = GLA_HEADS * GLA_DV
    c_q, c_kv, k_pe, gq, gk, gv, lr_f, lr_b, og = jnp.split(
        w_in[l], np.cumsum([MLA_Q_RANK, MLA_KV_RANK, MLA_ROPE, hk, hk, hv, GLA_GATE_RANK, GLA_GATE_RANK])[:].tolist(),
        axis=-1)
    lr_pad = jnp.zeros((D_MODEL, LANES - 2 * GLA_GATE_RANK), F32)
    win = jnp.concatenate([c_q, c_kv, k_pe, k_pe, gq, gk, gv, og, lr_f, lr_b, lr_pad], axis=-1).astype(BF16)
    wuq = w_uq[l].reshape(MLA_Q_RANK, MLA_HEADS, MLA_NOPE + MLA_ROPE)
    wuq = jnp.concatenate([wuq[:, :, :MLA_NOPE].reshape(MLA_Q_RANK, -1), wuq[:, :, MLA_NOPE:].reshape(MLA_Q_RANK, -1)],
                          axis=-1).astype(BF16)
    wgate = jnp.zeros((LANES, 2 * hk), F32)
    wgate = wgate.at[0:GLA_GATE_RANK, 0:hk].set(w_gate_fwd[l])
    wgate = wgate.at[GLA_GATE_RANK:2 * GLA_GATE_RANK, hk:].set(w_gate_bwd[l])
    wr = jnp.zeros((D_MODEL, LANES), F32)
    wr = wr.at[:, ROUTER_GROUP_LANE0:ROUTER_GROUP_LANE0 + N_GROUPS].set(w_router_group[l])
    wr = wr.at[:, ROUTER_EXPERT_LANE0:ROUTER_EXPERT_LANE0 + N_EXPERTS].set(w_router_expert[l])
    wr_hi = wr.astype(BF16)
    br = jnp.zeros((1, LANES), F32)
    br = br.at[0, ROUTER_GROUP_LANE0:ROUTER_GROUP_LANE0 + N_GROUPS].set(b_router_group[l])
    br = br.at[0, ROUTER_EXPERT_LANE0:ROUTER_EXPERT_LANE0 + N_EXPERTS].set(b_router_expert[l])
    return {
        'norm_mix': norm_mix[l][None], 'win': win, 'q_a_norm': q_a_norm[l][None], 'wuq': wuq,
        'kv_a_norm': kv_a_norm[l][None], 'wukv': w_ukv[l].astype(BF16),
        'wgate': wgate.astype(BF16), 'bgate': jnp.concatenate([b_gate_fwd[l], b_gate_bwd[l]])[None],
        'gla_norm': gla_norm[l][None], 'wout': w_out[l].astype(BF16), 'norm_ffn': norm_ffn[l][None],
        'wr': jnp.concatenate([wr_hi, (wr - wr_hi.astype(F32)).astype(BF16)], axis=-1), 'br': br,
        'w_gate': w_expert_gate[l].reshape(N_EXPERTS, D_MODEL, D_EXPERT).astype(BF16),
        'w_up': w_expert_up[l].reshape(N_EXPERTS, D_MODEL, D_EXPERT).astype(BF16),
        'w_down': w_expert_down[l].reshape(N_EXPERTS, D_EXPERT, D_MODEL).astype(BF16),
        'norm_final': norm_final[None],
    }


def _meta_streams(meta_tokens, w):
    cos, sin = _rope_tables(jnp.arange(N_META))
    _, k, v, _, gk, gv, gf, _, _ = _inproj(meta_tokens, cos, sin, w, N_META)
    pad_keys = ((0, 0), (0, LANES - N_META), (0, 0))
    front = ((GLA_CHUNK - N_META, 0), (0, 0))
    return (jnp.pad(k, pad_keys), jnp.pad(v, pad_keys), jnp.pad(gk, front), jnp.pad(gv, front), jnp.pad(gf, front))


def _token_mixers(x, meta, w, tm, tq, tk, tb, tmix):
    bsz, seq, _ = x.shape
    km, vm, mk, mv, mg = meta
    x2d = x.reshape(bsz * seq, D_MODEL)
    cos, sin = _rope_tables(N_META + jnp.arange(seq))
    q, k, v, gq, gk, gv, gf, gb, og = _inproj(x2d, cos, sin, w, tm)
    a = _attention(q, k, v, km, vm, bsz, seq, tq, tk)
    o_f, o_b = _gla(gq, gk, gv, gf, gb, mk, mv, mg, bsz, seq, tb)
    return _mix(x2d, a, o_f, o_b, og, w, tmix)


def kernel(x_prompt, x_sample, meta_tokens, norm_mix, w_in, q_a_norm, w_uq, kv_a_norm, w_ukv, w_gate_fwd, b_gate_fwd, w_gate_bwd, b_gate_bwd, gla_norm, w_out, norm_ffn, w_router_group, b_router_group, w_router_expert, b_router_expert, w_expert_gate, w_expert_up, w_expert_down, norm_final):
    w = _pack_weights(norm_mix, w_in, q_a_norm, w_uq, kv_a_norm, w_ukv, w_gate_fwd, b_gate_fwd, w_gate_bwd,
                      b_gate_bwd, gla_norm, w_out, norm_ffn, w_router_group, b_router_group, w_router_expert,
                      b_router_expert, w_expert_gate, w_expert_up, w_expert_down, norm_final)
    meta = _meta_streams(meta_tokens, w)
    outs = []
    for x in (x_prompt, x_sample):
        h1, hn, route, ids, counts = _token_mixers(x, meta, w, tm=1024, tq=1024, tk=2048, tb=512, tmix=1024)
        outs.append(_moe(hn, route, ids, counts, h1, w).reshape(x.shape))
    return tuple(outs)
```

```python
import functools

import numpy as np
import jax
import jax.numpy as jnp
from jax import lax
from jax.experimental import pallas as pl
from jax.experimental.pallas import tpu as pltpu
from jax.experimental.pallas import tpu_sc as plsc

F32 = jnp.float32
BF16 = jnp.bfloat16

D_MODEL = 1024
N_META = 16
MLA_HEADS = 4
MLA_Q_RANK = 384
MLA_KV_RANK = 256
MLA_NOPE = 128
MLA_ROPE = 64
MLA_V = 128
ROPE_THETA = 10000.0
GLA_HEADS = 4
GLA_DK = 64
GLA_DV = 128
GLA_GATE_RANK = 16
GLA_TAU = 16.0
GLA_CHUNK = 64
N_GROUPS = 4
EXPERTS_PER_GROUP = 8
N_EXPERTS = N_GROUPS * EXPERTS_PER_GROUP
D_EXPERT = 256
EPS = 1e-6

LANES = 128
V7X_VMEM_BYTES = 64 * 1024 * 1024
VMEM_LIMIT = V7X_VMEM_BYTES * 7 // 8

ATTN_SCALE = (MLA_NOPE + MLA_ROPE) ** -0.5 * float(np.log2(np.e))
QK_WIDTH = 2 * LANES
V_WIDTH = 2 * LANES
ATTN_GROUP = 8

C_CQ = 0
C_CKV = C_CQ + MLA_Q_RANK
C_KPE = C_CKV + MLA_KV_RANK
C_GQ = C_KPE + LANES
C_GK = C_GQ + GLA_HEADS * GLA_DK
C_GV = C_GK + GLA_HEADS * GLA_DK
C_OG = C_GV + GLA_HEADS * GLA_DV
C_LR = C_OG + GLA_HEADS * GLA_DV
D_IN_PACKED = C_LR + LANES

ROUTER_GROUP_LANE0 = 0
ROUTER_EXPERT_LANE0 = N_GROUPS


def _cparams(*semantics):
    return pltpu.CompilerParams(dimension_semantics=semantics, vmem_limit_bytes=VMEM_LIMIT)


def _rms(x, g):
    return x * lax.rsqrt(jnp.mean(x * x, axis=-1, keepdims=True) + EPS) * g


def _dot(a, b):
    return jnp.dot(a, b, preferred_element_type=F32)


def _dot_nt(a, b):
    return lax.dot_general(a, b, (((1,), (1,)), ((), ())), preferred_element_type=F32)


def _dot_tn(a, b):
    return lax.dot_general(a, b, (((0,), (0,)), ((), ())), preferred_element_type=F32)


def _full_spec(shape):
    return pl.BlockSpec(shape, lambda *_: (0,) * len(shape))


ROW_CHUNKS = D_MODEL // LANES


def _rows_to_tiles(x):
    chunks = jnp.stack([x[:, s * LANES:(s + 1) * LANES] for s in range(x.shape[1] // LANES)], axis=0)
    return pltpu.einshape("smd->msd", chunks)


def _tiles_to_rows(x):
    chunks = pltpu.einshape("msd->smd", x)
    return jnp.concatenate([chunks[s] for s in range(x.shape[1])], axis=-1)


PACKED_CHUNKS = ROW_CHUNKS // 2


def _pack_bf16_pairs(x):
    half = x.shape[1] // 2
    bits = lambda v: lax.bitcast_convert_type(v.astype(BF16).astype(F32), jnp.uint32)
    return (bits(x[:, half:]) & jnp.uint32(0xFFFF0000)) | (bits(x[:, :half]) >> 16)


def _unpack_bf16_pairs(w):
    lo = lax.bitcast_convert_type(w << 16, F32)
    hi = lax.bitcast_convert_type(w & jnp.uint32(0xFFFF0000), F32)
    return jnp.concatenate([lo, hi], axis=-1)


def _rope_pairs(x, cos, sin_signed, first_half):
    swapped = jnp.where(first_half, pltpu.roll(x, LANES - MLA_ROPE // 2, 1), pltpu.roll(x, MLA_ROPE // 2, 1))
    return x * cos + swapped * sin_signed


def _inproj_body(x_ref, cos_ref, sin_ref, nmix_ref, win_ref, qan_ref, wuq_ref, kvan_ref, wukv_ref,
                 wgate_ref, bgate_ref,
                 q_ref, k_ref, v_ref, gq_ref, gk_ref, gv_ref, gf_ref, gb_ref, og_ref):
    hn = _rms(x_ref[...], nmix_ref[...]).astype(BF16)

    def proj(lo, hi):
        return _dot(hn, win_ref[:, lo:hi])

    cos = cos_ref[...]
    sin = sin_ref[...]
    lane = lax.broadcasted_iota(jnp.int32, cos.shape, 1)
    first_half = (lane & (MLA_ROPE - 1)) < MLA_ROPE // 2
    low_lanes = lane < MLA_ROPE

    cq = _rms(proj(C_CQ, C_CKV), qan_ref[...]).astype(BF16)
    qn = _dot(cq, wuq_ref[:, 0:MLA_HEADS * MLA_NOPE]) * ATTN_SCALE
    qr = _dot(cq, wuq_ref[:, MLA_HEADS * MLA_NOPE:])
    for j in range(MLA_HEADS // 2):
        rj = (_rope_pairs(qr[:, j * LANES:(j + 1) * LANES], cos, sin, first_half) * ATTN_SCALE).astype(BF16)
        for h in (2 * j, 2 * j + 1):
            q_ref[h, :, 0:LANES] = qn[:, h * LANES:(h + 1) * LANES].astype(BF16)
            q_ref[h, :, LANES:QK_WIDTH] = rj

    ckv = _rms(proj(C_CKV, C_KPE), kvan_ref[...]).astype(BF16)
    kv = _dot(ckv, wukv_ref[...])
    kr = _rope_pairs(proj(C_KPE, C_GQ), cos, sin, first_half)
    kr_even = jnp.where(low_lanes, kr, 0.0).astype(BF16)
    kr_odd = jnp.where(low_lanes, 0.0, kr).astype(BF16)
    for h in range(MLA_HEADS):
        base = h * (MLA_NOPE + MLA_V)
        k_ref[h, :, 0:LANES] = kv[:, base:base + MLA_NOPE].astype(BF16)
        k_ref[h, :, LANES:QK_WIDTH] = kr_even if h % 2 == 0 else kr_odd
        v_ref[h, :, 0:MLA_V] = kv[:, base + MLA_NOPE:base + MLA_NOPE + MLA_V].astype(BF16)
        v_ref[h, :, MLA_V:V_WIDTH] = jnp.ones((kv.shape[0], V_WIDTH - MLA_V), BF16)

    gq_ref[...] = (proj(C_GQ, C_GK) * (GLA_DK ** -0.5)).astype(BF16)
    gk_ref[...] = proj(C_GK, C_GV).astype(BF16)
    gv_ref[...] = proj(C_GV, C_OG).astype(BF16)
    og_ref[...] = proj(C_OG, C_LR).astype(BF16)
    pre = _dot(proj(C_LR, D_IN_PACKED).astype(BF16), wgate_ref[...]) + bgate_ref[...]
    logsig = jnp.minimum(pre, 0.0) - jnp.log1p(jnp.exp(-jnp.abs(pre)))
    gates = logsig * (1.0 / GLA_TAU)
    gf_ref[...] = gates[:, 0:GLA_HEADS * GLA_DK]
    gb_ref[...] = gates[:, GLA_HEADS * GLA_DK:]


def _inproj(x2d, cos, sin, w, tm):
    t = x2d.shape[0]
    blocks_per_seq = cos.shape[0] // tm
    hk = GLA_HEADS * GLA_DK
    hv = GLA_HEADS * GLA_DV
    row = lambda width: pl.BlockSpec((tm, width), lambda i: (i, 0))
    head_rows = lambda width: pl.BlockSpec((MLA_HEADS, tm, width), lambda i: (0, i, 0))
    tab = pl.BlockSpec((tm, LANES), lambda i: (i % blocks_per_seq, 0))
    out_shape = (
        jax.ShapeDtypeStruct((MLA_HEADS, t, QK_WIDTH), BF16),
        jax.ShapeDtypeStruct((MLA_HEADS, t, QK_WIDTH), BF16),
        jax.ShapeDtypeStruct((MLA_HEADS, t, V_WIDTH), BF16),
        jax.ShapeDtypeStruct((t, hk), BF16),
        jax.ShapeDtypeStruct((t, hk), BF16),
        jax.ShapeDtypeStruct((t, hv), BF16),
        jax.ShapeDtypeStruct((t, hk), F32),
        jax.ShapeDtypeStruct((t, hk), F32),
        jax.ShapeDtypeStruct((t, hv), BF16),
    )
    return pl.pallas_call(
        _inproj_body,
        grid=(t // tm,),
        in_specs=[row(D_MODEL), tab, tab,
                  _full_spec((1, D_MODEL)), _full_spec((D_MODEL, D_IN_PACKED)),
                  _full_spec((1, MLA_Q_RANK)), _full_spec(w['wuq'].shape),
                  _full_spec((1, MLA_KV_RANK)), _full_spec(w['wukv'].shape),
                  _full_spec(w['wgate'].shape), _full_spec(w['bgate'].shape)],
        out_specs=(head_rows(QK_WIDTH), head_rows(QK_WIDTH), head_rows(V_WIDTH),
                   row(hk), row(hk), row(hv), row(hk), row(hk), row(hv)),
        out_shape=out_shape,
        compiler_params=_cparams("parallel"),
        name="inproj",
    )(x2d, cos, sin, w['norm_mix'], w['win'], w['q_a_norm'], w['wuq'], w['kv_a_norm'], w['wukv'],
      w['wgate'], w['bgate'])


def _attn_body(q_ref, k_ref, v_ref, km_ref, vm_ref, o_ref, s_ref, acc_ref, *, tk):
    q = q_ref[...]
    n_blocks = k_ref.shape[0] // tk
    group_size = min(ATTN_GROUP, n_blocks)
    assert n_blocks % group_size == 0 and group_size % 2 == 0
    n_groups = n_blocks // group_size

    def scores(j):
        return _dot_nt(q, k_ref[pl.ds(pl.multiple_of(j * tk, tk), tk), :])

    def values(j):
        return v_ref[pl.ds(pl.multiple_of(j * tk, tk), tk), :]

    def absorb(m, s, v):
        m_new = jnp.maximum(m, jnp.max(s, axis=-1, keepdims=True))
        p = jnp.exp2(s - m_new)
        acc_ref[...] = jnp.exp2(m - m_new) * acc_ref[...] + _dot(p.astype(BF16), v)
        return m_new

    s_ref[0] = scores(0)
    sm = _dot_nt(q, km_ref[...])
    sm = jnp.where(lax.broadcasted_iota(jnp.int32, sm.shape, 1) < N_META, sm, -jnp.inf)
    m = jnp.max(sm, axis=-1, keepdims=True)
    acc_ref[...] = _dot(jnp.exp2(sm - m).astype(BF16), vm_ref[...])

    def group(g, m, last):
        for i in range(group_size):
            j = group_size * g + i
            s = s_ref[i % 2]
            if not (last and i == group_size - 1):
                s_ref[(i + 1) % 2] = scores(j + 1)
            m = absorb(m, s, values(j))
        return m

    m = lax.fori_loop(0, n_groups - 1, lambda g, m: group(g, m, False), m)
    group(n_groups - 1, m, True)
    acc = acc_ref[...]
    o_ref[...] = (acc[:, :MLA_V] / acc[:, MLA_V:]).astype(o_ref.dtype)


def _attention(q, k, v, km, vm, bsz, seq, tq, tk):
    nq = seq // tq
    return pl.pallas_call(
        functools.partial(_attn_body, tk=tk),
        grid=(bsz, MLA_HEADS, nq),
        in_specs=[pl.BlockSpec((None, tq, QK_WIDTH), lambda b, h, i: (h, b * nq + i, 0)),
                  pl.BlockSpec((None, seq, QK_WIDTH), lambda b, h, i: (h, b, 0)),
                  pl.BlockSpec((None, seq, V_WIDTH), lambda b, h, i: (h, b, 0)),
                  pl.BlockSpec((None, LANES, QK_WIDTH), lambda b, h, i: (h, 0, 0)),
                  pl.BlockSpec((None, LANES, V_WIDTH), lambda b, h, i: (h, 0, 0))],
        out_specs=pl.BlockSpec((tq, MLA_V), lambda b, h, i: (b * nq + i, h)),
        out_shape=jax.ShapeDtypeStruct((bsz * seq, MLA_HEADS * MLA_V), BF16),
        scratch_shapes=[pltpu.VMEM((2, tq, tk), F32), pltpu.VMEM((tq, V_WIDTH), F32)],
        compiler_params=_cparams("parallel", "parallel", "arbitrary"),
        name="mla_attention",
    )(q, k, v, km, vm)


def _split3(x):
    hi = x.astype(BF16)
    r1 = x - hi.astype(F32)
    mid = r1.astype(BF16)
    lo = (r1 - mid.astype(F32)).astype(BF16)
    return hi, mid, lo


def _gla_log_decay(g, tri):
    g_hi, g_mid, g_lo = _split3(g)
    return _dot(tri, g_hi) + _dot(tri, g_mid) + _dot(tri, g_lo)


def _head_rows(x):
    even = lax.broadcasted_iota(jnp.int32, x.shape, 1) < GLA_DK
    return jnp.concatenate([jnp.where(even, x, 0.0), jnp.where(even, 0.0, x)], axis=0).astype(BF16)


def _gla_operands(q, k, b, mid, last):
    b_last = b[last:last + 1, :]
    ke = (k * jnp.exp(b_last - b)).astype(BF16)
    if q is None:
        return jnp.exp(b_last), ke, None, None, None
    b_mid = b[mid:mid + 1, :]
    ks = (k * jnp.exp(b_mid - b)).astype(BF16)
    return jnp.exp(b_last), ke, ks, _head_rows(q * jnp.exp(b - b_mid)), _head_rows(q * jnp.exp(b))


def _gla_state_update(v_even, v_odd, ke):
    return jnp.where(lax.broadcasted_iota(jnp.int32, (GLA_DV, LANES), 1) < GLA_DK,
                     _dot_tn(v_even, ke), _dot_tn(v_odd, ke))


def _gla_intra(v_even, v_odd, ks, qs2, keep):
    scores = jnp.where(keep, _dot_nt(qs2, ks), 0.0).astype(BF16)
    return _dot(scores[:GLA_CHUNK], v_even), _dot(scores[GLA_CHUNK:], v_odd)


def _gla_body(qf_ref, kf_ref, vf_ref, gf_ref, qb_ref, kb_ref, vb_ref, gb_ref, mk_ref, mv_ref, mg_ref,
              of_ref, ob_ref, state_ref):
    n_chunks = qf_ref.shape[0] // GLA_CHUNK
    n_pairs = GLA_HEADS // 2
    r = lax.broadcasted_iota(jnp.int32, (GLA_CHUNK, GLA_CHUNK), 0)
    c = lax.broadcasted_iota(jnp.int32, (GLA_CHUNK, GLA_CHUNK), 1)
    tri_f = jnp.where(c <= r, 1.0, 0.0).astype(BF16)
    tri_b = jnp.where(c >= r, 1.0, 0.0).astype(BF16)
    r2 = lax.broadcasted_iota(jnp.int32, (2 * GLA_CHUNK, GLA_CHUNK), 0) & (GLA_CHUNK - 1)
    c2 = lax.broadcasted_iota(jnp.int32, (2 * GLA_CHUNK, GLA_CHUNK), 1)
    keep_f = c2 <= r2
    keep_b = c2 >= r2
    mid_f, last_f = GLA_CHUNK // 2 - 1, GLA_CHUNK - 1
    mid_b, last_b = GLA_CHUNK // 2, 0

    def pair_cols(p):
        return slice(p * LANES, (p + 1) * LANES)

    def head_cols(h):
        return slice(h * GLA_DV, (h + 1) * GLA_DV)

    @pl.when(pl.program_id(1) == 0)
    def _():
        for p in range(n_pairs):
            b = _gla_log_decay(mg_ref[:, pair_cols(p)], tri_f)
            _, ke, _, _, _ = _gla_operands(None, mk_ref[:, pair_cols(p)], b, mid_f, last_f)
            state_ref[p] = _gla_state_update(mv_ref[:, head_cols(2 * p)].astype(BF16),
                                             mv_ref[:, head_cols(2 * p + 1)].astype(BF16), ke)
            state_ref[n_pairs + p] = jnp.zeros((GLA_DV, LANES), F32)

    scans = []
    for p in range(n_pairs):
        scans.append((p, list(range(n_chunks)), qf_ref, kf_ref, vf_ref, gf_ref, of_ref, p,
                      tri_f, keep_f, mid_f, last_f))
        scans.append((p, list(reversed(range(n_chunks))), qb_ref, kb_ref, vb_ref, gb_ref, ob_ref, n_pairs + p,
                      tri_b, keep_b, mid_b, last_b))

    def rows(c):
        return slice(c * GLA_CHUNK, (c + 1) * GLA_CHUNK)

    def values(v_ref, p, c):
        return (v_ref[rows(c), head_cols(2 * p)].astype(BF16), v_ref[rows(c), head_cols(2 * p + 1)].astype(BF16))

    log_decay = [[_gla_log_decay(g_ref[rows(c), pair_cols(p)], tri) for c in order]
                 for (p, order, _, _, _, g_ref, _, _, tri, _, _, _) in scans]
    operands = [[_gla_operands(q_ref[rows(c), pair_cols(p)], k_ref[rows(c), pair_cols(p)], b, mid, last)
                 for c, b in zip(order, bs)]
                for (p, order, q_ref, k_ref, _, _, _, _, _, _, mid, last), bs in zip(scans, log_decay)]
    updates = [[_gla_state_update(*values(v_ref, p, c), ops[1]) for c, ops in zip(order, opss)]
               for (p, order, _, _, v_ref, _, _, _, _, _, _, _), opss in zip(scans, operands)]
    intra = [[_gla_intra(*values(v_ref, p, c), ops[2], ops[3], keep) for c, ops in zip(order, opss)]
             for (p, order, _, _, v_ref, _, _, _, _, keep, _, _), opss in zip(scans, operands)]
    states = []
    for (_, order, _, _, _, _, _, slot, _, _, _, _), opss, upds in zip(scans, operands, updates):
        st = state_ref[slot]
        entering = []
        for ops, upd in zip(opss, upds):
            entering.append(st.astype(BF16))
            st = st * ops[0] + upd
        state_ref[slot] = st
        states.append(entering)
    for (p, order, _, _, _, _, o_ref, _, _, _, _, _), opss, sts, locs in zip(scans, operands, states, intra):
        for c, ops, st, (o_even, o_odd) in zip(order, opss, sts, locs):
            inter = _dot_nt(ops[4], st)
            o_ref[rows(c), head_cols(2 * p)] = (o_even + inter[:GLA_CHUNK]).astype(o_ref.dtype)
            o_ref[rows(c), head_cols(2 * p + 1)] = (o_odd + inter[GLA_CHUNK:]).astype(o_ref.dtype)


def _gla(gq, gk, gv, gf, gb, mk, mv, mg, bsz, seq, tb):
    nb = seq // tb
    hk = GLA_HEADS * GLA_DK
    hv = GLA_HEADS * GLA_DV
    fwd = lambda width: pl.BlockSpec((tb, width), lambda b, j: (b * nb + j, 0))
    bwd = lambda width: pl.BlockSpec((tb, width), lambda b, j: (b * nb + nb - 1 - j, 0))
    t = bsz * seq
    return pl.pallas_call(
        _gla_body,
        grid=(bsz, nb),
        in_specs=[fwd(hk), fwd(hk), fwd(hv), fwd(hk), bwd(hk), bwd(hk), bwd(hv), bwd(hk),
                  _full_spec(mk.shape), _full_spec(mv.shape), _full_spec(mg.shape)],
        out_specs=(fwd(hv), bwd(hv)),
        out_shape=(jax.ShapeDtypeStruct((t, hv), BF16), jax.ShapeDtypeStruct((t, hv), BF16)),
        scratch_shapes=[pltpu.VMEM((2 * (GLA_HEADS // 2), GLA_DV, LANES), F32)],
        compiler_params=_cparams("parallel", "arbitrary"),
        name="gla_scan",
    )(gq, gk, gv, gf, gq, gk, gv, gb, mk, mv, mg)


def _mix_body(x_ref, a_ref, of_ref, ob_ref, og_ref, gnorm_ref, wout_ref, nffn_ref, wr_ref, br_ref,
              h1_ref, hn_ref, route_ref, ids_ref, count_ref, tri_ref):
    tm = x_ref.shape[0]

    @pl.when(pl.program_id(0) == 0)
    def _():
        r = lax.broadcasted_iota(jnp.int32, (tm, tm), 0)
        c = lax.broadcasted_iota(jnp.int32, (tm, tm), 1)
        tri_ref[...] = jnp.where(c < r, 1.0, 0.0).astype(BF16)
        count_ref[...] = jnp.zeros_like(count_ref)

    a_width = MLA_HEADS * MLA_V
    gated = []
    for h in range(GLA_HEADS):
        cols = slice(h * GLA_DV, (h + 1) * GLA_DV)
        o = of_ref[:, cols].astype(F32) + ob_ref[:, cols].astype(F32)
        og = og_ref[:, cols].astype(F32)
        silu = og / (1.0 + jnp.exp(-og))
        gated.append((_rms(o, gnorm_ref[...]) * silu).astype(BF16))
    h1 = (x_ref[...] + _dot(a_ref[...], wout_ref[0:a_width, :])
          + _dot(jnp.concatenate(gated, axis=-1), wout_ref[a_width:, :]))
    h1_ref[...] = h1
    hn = _rms(h1, nffn_ref[...])
    hn_ref[...] = _rows_to_tiles(_pack_bf16_pairs(hn))

    hn_hi = hn.astype(BF16)
    hn_lo = (hn - hn_hi.astype(F32)).astype(BF16)
    hi_terms = _dot(hn_hi, wr_ref[...])
    logits = hi_terms[:, :LANES] + hi_terms[:, LANES:] + _dot(hn_lo, wr_ref[:, :LANES]) + br_ref[...]
    lane = lax.broadcasted_iota(jnp.int32, logits.shape, 1).astype(F32)
    none = float(LANES)
    neg = -jnp.inf

    def lane_max(x):
        return jnp.max(x, axis=-1, keepdims=True)

    def lane_sum(x):
        return jnp.sum(x, axis=-1, keepdims=True)

    def first_lane(mask):
        return jnp.min(jnp.where(mask, lane, none), axis=-1, keepdims=True)

    is_group = lane < float(N_GROUPS)
    g_max = lane_max(jnp.where(is_group, logits, neg))
    g_exp = jnp.where(is_group, jnp.exp(logits - g_max), 0.0)
    g_prob = g_exp / lane_sum(g_exp)
    g_w = lane_max(g_prob)
    g_idx = first_lane(is_group & (g_prob == g_w))
    e_lo = float(ROUTER_EXPERT_LANE0) + float(EXPERTS_PER_GROUP) * g_idx
    sel = (lane >= e_lo) & (lane < e_lo + float(EXPERTS_PER_GROUP))
    e_max = lane_max(jnp.where(sel, logits, neg))
    e_exp = jnp.where(sel, jnp.exp(logits - e_max), 0.0)
    e_prob = e_exp / lane_sum(e_exp)
    p1 = lane_max(jnp.where(sel, e_prob, neg))
    i1 = first_lane(sel & (e_prob == p1))
    rest = sel & (lane != i1)
    p2 = lane_max(jnp.where(rest, e_prob, neg))
    i2 = first_lane(rest & (e_prob == p2))
    denom = p1 + p2
    chosen = jnp.where((lane == i1) | (lane == i2), 1.0, 0.0)
    rank = count_ref[...] + _dot(tri_ref[...], chosen.astype(BF16))
    count_ref[...] += jnp.sum(chosen, axis=0, keepdims=True)
    fields = (i1 - float(ROUTER_EXPERT_LANE0), i2 - float(ROUTER_EXPERT_LANE0),
              lane_sum(jnp.where(lane == i1, rank, 0.0)), lane_sum(jnp.where(lane == i2, rank, 0.0)),
              g_w * (p1 / denom), g_w * (p2 / denom))
    route = jnp.zeros_like(logits)
    for k, value in enumerate(fields):
        route = jnp.where(lane == float(k), value, route)
    route_ref[...] = route
    ids_ref[...] = jnp.transpose(route)[0:ROUTE_ID_ROWS, :].astype(jnp.int32)


ROUTE_E1, ROUTE_E2, ROUTE_RANK1, ROUTE_RANK2, ROUTE_W1, ROUTE_W2 = range(6)
ROUTE_ID_ROWS = 8


def _mix(x2d, a, o_f, o_b, og, w, tm):
    t = x2d.shape[0]
    hv = GLA_HEADS * GLA_DV
    row = lambda width: pl.BlockSpec((tm, width), lambda i: (i, 0))
    return pl.pallas_call(
        _mix_body,
        grid=(t // tm,),
        in_specs=[row(D_MODEL), row(MLA_HEADS * MLA_V), row(hv), row(hv), row(hv),
                  _full_spec((1, GLA_DV)), _full_spec(w['wout'].shape), _full_spec((1, D_MODEL)),
                  _full_spec(w['wr'].shape), _full_spec(w['br'].shape)],
        out_specs=(row(D_MODEL), pl.BlockSpec((tm, PACKED_CHUNKS, LANES), lambda i: (i, 0, 0)), row(LANES),
                   pl.BlockSpec((ROUTE_ID_ROWS, tm), lambda i: (0, i)), _full_spec((1, LANES))),
        out_shape=(jax.ShapeDtypeStruct((t, D_MODEL), F32), jax.ShapeDtypeStruct((t, PACKED_CHUNKS, LANES), jnp.uint32),
                   jax.ShapeDtypeStruct((t, LANES), F32), jax.ShapeDtypeStruct((ROUTE_ID_ROWS, t), jnp.int32),
                   jax.ShapeDtypeStruct((1, LANES), F32)),
        scratch_shapes=[pltpu.VMEM((tm, tm), BF16)],
        compiler_params=_cparams("arbitrary"),
        name="mix_router",
    )(x2d, a, o_f, o_b, og, w['gla_norm'], w['wout'], w['norm_ffn'], w['wr'], w['br'])


EXPERT_TILE = 256
ROW_TILE = 1024


SC_CORES = 2
SC_SUBCORES = 16
SC_GATHER_ROWS = 32


def _sc_gather(table, idx):
    n = idx.shape[0]
    workers = SC_CORES * SC_SUBCORES
    per_worker = n // workers
    assert n % (workers * SC_GATHER_ROWS) == 0
    mesh = plsc.VectorSubcoreMesh(core_axis_name="c", subcore_axis_name="s")

    @functools.partial(
        pl.kernel, mesh=mesh,
        out_type=jax.ShapeDtypeStruct((n,) + table.shape[1:], table.dtype),
        scratch_types=[pltpu.VMEM((SC_GATHER_ROWS,), jnp.int32),
                       pltpu.VMEM((SC_GATHER_ROWS,) + table.shape[1:], table.dtype),
                       pltpu.SemaphoreType.DMA])
    def gather(table_ref, idx_ref, out_ref, idx_buf, rows_buf, sem):
        base = (lax.axis_index("s") * SC_CORES + lax.axis_index("c")) * per_worker

        @pl.loop(0, per_worker // SC_GATHER_ROWS)
        def _(j):
            rows = pl.ds(base + j * SC_GATHER_ROWS, SC_GATHER_ROWS)
            pltpu.sync_copy(idx_ref.at[rows], idx_buf)
            pltpu.async_copy(table_ref.at[idx_buf], rows_buf, sem).wait()
            pltpu.sync_copy(rows_buf, out_ref.at[rows])

    return gather(table, idx)


def _sc_scatter(rows, idx, n_out):
    copies, n = idx.shape
    idx = idx.reshape(copies * n)
    workers = SC_CORES * SC_SUBCORES
    per_worker = n // workers
    assert n % (workers * SC_GATHER_ROWS) == 0
    mesh = plsc.VectorSubcoreMesh(core_axis_name="c", subcore_axis_name="s")

    @functools.partial(
        pl.kernel, mesh=mesh,
        out_type=jax.ShapeDtypeStruct((n_out,) + rows.shape[1:], rows.dtype),
        scratch_types=[pltpu.VMEM((SC_GATHER_ROWS,), jnp.int32),
                       pltpu.VMEM((SC_GATHER_ROWS,) + rows.shape[1:], rows.dtype),
                       pltpu.SemaphoreType.DMA])
    def scatter(rows_ref, idx_ref, out_ref, idx_buf, rows_buf, sem):
        base = (lax.axis_index("s") * SC_CORES + lax.axis_index("c")) * per_worker

        @pl.loop(0, per_worker // SC_GATHER_ROWS)
        def _(j):
            first = base + j * SC_GATHER_ROWS
            pltpu.sync_copy(rows_ref.at[pl.ds(first, SC_GATHER_ROWS)], rows_buf)
            for k in range(copies):
                pltpu.sync_copy(idx_ref.at[pl.ds(k * n + first, SC_GATHER_ROWS)], idx_buf)
                pltpu.async_copy(rows_buf, out_ref.at[idx_buf], sem).wait()

    return scatter(rows, idx)


SLOT_TILE = 2048


def _slots_body(starts_ref, ids_ref, slots_ref):
    ids = ids_ref[...]
    experts = ids[ROUTE_E1:ROUTE_E1 + 2, :]
    start = jnp.zeros_like(experts)
    for e in range(N_EXPERTS):
        start = jnp.where(experts == e, starts_ref[e], start)
    slots_ref[...] = start + ids[ROUTE_RANK1:ROUTE_RANK1 + 2, :]


def _slots(starts, ids):
    t = ids.shape[1]
    return pl.pallas_call(
        _slots_body,
        grid_spec=pltpu.PrefetchScalarGridSpec(
            num_scalar_prefetch=1,
            grid=(t // SLOT_TILE,),
            in_specs=[pl.BlockSpec((ROUTE_ID_ROWS, SLOT_TILE), lambda i, *_: (0, i))],
            out_specs=pl.BlockSpec((2, SLOT_TILE), lambda i, *_: (0, i))),
        out_shape=jax.ShapeDtypeStruct((2, t), jnp.int32),
        compiler_params=_cparams("parallel"),
        name="moe_slots",
    )(starts, ids)


def _queue_body(slots_ref, token_ref, *, n_tokens):
    i = pl.program_id(0)

    @pl.when(i == 0)
    def _():
        for base in range(0, token_ref.shape[0], n_tokens):
            def clear(j, carry):
                token_ref[base + j] = j
                return carry
            lax.fori_loop(0, min(n_tokens, token_ref.shape[0] - base), clear, 0, unroll=8)

    def place(r, carry):
        for k in range(2):
            token_ref[slots_ref[k, r]] = i * SLOT_TILE + r
        return carry

    lax.fori_loop(0, SLOT_TILE, place, 0, unroll=8)


def _queue_tokens(slots, n_rows):
    t = slots.shape[1]
    return pl.pallas_call(
        functools.partial(_queue_body, n_tokens=t),
        grid=(t // SLOT_TILE,),
        in_specs=[pl.BlockSpec((2, SLOT_TILE), lambda i: (0, i), memory_space=pltpu.SMEM)],
        out_specs=pl.BlockSpec(memory_space=pltpu.SMEM),
        out_shape=jax.ShapeDtypeStruct((n_rows,), jnp.int32),
        compiler_params=_cparams("arbitrary"),
        name="moe_queue",
    )(slots)


TILES_PER_STEP = 4


def _expert_body(tile_expert_ref, n_tiles_ref, tile_valid_ref, xs_ref, *refs):
    wg_refs = refs[0:TILES_PER_STEP]
    wu_refs = refs[TILES_PER_STEP:2 * TILES_PER_STEP]
    wd_refs = refs[2 * TILES_PER_STEP:3 * TILES_PER_STEP]
    ys_ref = refs[3 * TILES_PER_STEP]
    first = pl.program_id(0) * TILES_PER_STEP
    used = first < n_tiles_ref[0]

    @pl.when(used)
    def _():
        rows = [slice(j * EXPERT_TILE, (j + 1) * EXPERT_TILE) for j in range(TILES_PER_STEP)]
        xs = [_unpack_bf16_pairs(_tiles_to_rows(xs_ref[r])) for r in rows]
        xs = [jnp.where(lax.broadcasted_iota(jnp.int32, x.shape, 0) < tile_valid_ref[first + j], x, 0.0)
              for j, x in enumerate(xs)]
        xs = [x.astype(BF16) for x in xs]
        gates = [_dot(x, wg[...]) for x, wg in zip(xs, wg_refs)]
        ups = [_dot(x, wu[...]) for x, wu in zip(xs, wu_refs)]
        acts = [(g / (1.0 + jnp.exp(-g))) * u for g, u in zip(gates, ups)]
        ys = [_dot(act.astype(BF16), wd[...]) for act, wd in zip(acts, wd_refs)]
        for r, y in zip(rows, ys):
            ys_ref[r] = _rows_to_tiles(_pack_bf16_pairs(y))

    @pl.when(jnp.logical_not(used))
    def _():
        ys_ref[...] = jnp.zeros_like(ys_ref)


def _experts(tile_expert, n_tiles, tile_valid, xs, w):
    n_steps = xs.shape[0] // (EXPERT_TILE * TILES_PER_STEP)

    def step(i, nt):
        return jnp.minimum(i, (nt[0] - 1) // TILES_PER_STEP)

    def expert(i, j, te, nt):
        return te[jnp.minimum(step(i, nt) * TILES_PER_STEP + j, nt[0] - 1)]

    tiles = lambda index: pl.BlockSpec((EXPERT_TILE * TILES_PER_STEP, PACKED_CHUNKS, LANES), index)
    up_spec = lambda j: pl.BlockSpec((None, D_MODEL, D_EXPERT), lambda i, te, nt, tv: (expert(i, j, te, nt), 0, 0))
    down_spec = lambda j: pl.BlockSpec((None, D_EXPERT, D_MODEL), lambda i, te, nt, tv: (expert(i, j, te, nt), 0, 0))
    slots = range(TILES_PER_STEP)
    return pl.pallas_call(
        _expert_body,
        grid_spec=pltpu.PrefetchScalarGridSpec(
            num_scalar_prefetch=3,
            grid=(n_steps,),
            in_specs=([tiles(lambda i, te, nt, tv: (step(i, nt), 0, 0))]
                      + [up_spec(j) for j in slots] + [up_spec(j) for j in slots] + [down_spec(j) for j in slots]),
            out_specs=tiles(lambda i, te, nt, tv: (i, 0, 0))),
        out_shape=jax.ShapeDtypeStruct(xs.shape, xs.dtype),
        compiler_params=_cparams("arbitrary"),
        name="moe_experts",
    )(tile_expert, n_tiles, tile_valid, xs, *([w['w_gate']] * TILES_PER_STEP), *([w['w_up']] * TILES_PER_STEP),
      *([w['w_down']] * TILES_PER_STEP))


def _combine_body(h1_ref, route_ref, nfin_ref, y1_ref, y2_ref, out_ref):
    route = route_ref[...]
    lane = lax.broadcasted_iota(jnp.int32, route.shape, 1)
    w1 = jnp.sum(jnp.where(lane == ROUTE_W1, route, 0.0), axis=-1, keepdims=True)
    w2 = jnp.sum(jnp.where(lane == ROUTE_W2, route, 0.0), axis=-1, keepdims=True)
    y1 = _unpack_bf16_pairs(_tiles_to_rows(y1_ref[...]))
    y2 = _unpack_bf16_pairs(_tiles_to_rows(y2_ref[...]))
    y = w1 * y1 + w2 * y2
    out_ref[...] = _rms(h1_ref[...] + y, nfin_ref[...])


def _combine(h1, route, y12, w):
    t = h1.shape[0]
    row = lambda width: pl.BlockSpec((ROW_TILE, width), lambda i: (i, 0))
    tiles = lambda k: pl.BlockSpec((None, ROW_TILE, PACKED_CHUNKS, LANES), lambda i: (k, i, 0, 0))
    return pl.pallas_call(
        _combine_body,
        grid=(t // ROW_TILE,),
        in_specs=[row(D_MODEL), row(LANES), _full_spec((1, D_MODEL)), tiles(0), tiles(1)],
        out_specs=row(D_MODEL),
        out_shape=jax.ShapeDtypeStruct((t, D_MODEL), F32),
        compiler_params=_cparams("parallel"),
        name="moe_combine",
    )(h1, route, w['norm_final'], y12, y12)


def _moe(hn, route, ids, counts, h1, w):
    t = hn.shape[0]
    n_rows = 2 * t + N_EXPERTS * EXPERT_TILE
    n_grid_tiles = n_rows // EXPERT_TILE
    count = counts[0, ROUTER_EXPERT_LANE0:ROUTER_EXPERT_LANE0 + N_EXPERTS].astype(jnp.int32)
    padded = jnp.maximum((count + EXPERT_TILE - 1) // EXPERT_TILE, 1) * EXPERT_TILE
    ends = jnp.sum(jnp.where(jnp.arange(N_EXPERTS)[:, None] <= jnp.arange(N_EXPERTS)[None, :], padded[:, None], 0),
                   axis=0)
    starts = ends - padded
    n_tiles = (ends[-1:] // EXPERT_TILE)
    tile_rows = jnp.arange(n_grid_tiles, dtype=jnp.int32) * EXPERT_TILE
    tile_expert = jnp.minimum(jnp.sum((ends[None, :] <= tile_rows[:, None]).astype(jnp.int32), axis=1), N_EXPERTS - 1)
    tile_valid = jnp.clip(jnp.sum(jnp.where(jnp.arange(N_EXPERTS)[None, :] == tile_expert[:, None],
                                            (starts + count)[None, :], 0), axis=1) - tile_rows, 0, EXPERT_TILE)
    slots = _slots(starts, ids)
    ys = _experts(tile_expert, n_tiles, tile_valid, _sc_scatter(hn, slots, n_rows), w)
    y12 = _sc_gather(ys, slots.reshape(2 * t)).reshape(2, t, PACKED_CHUNKS, LANES)
    return _combine(h1, route, y12, w)


def _rope_tables(positions):
    inv_freq = 1.0 / (ROPE_THETA ** (jnp.arange(0, MLA_ROPE, 2, dtype=F32) / MLA_ROPE))
    ang = positions.astype(F32)[:, None] * inv_freq[None, :]
    cos, sin = jnp.cos(ang), jnp.sin(ang)
    reps = LANES // MLA_ROPE
    return jnp.tile(jnp.concatenate([cos, cos], axis=-1), (1, reps)), jnp.tile(jnp.concatenate([-sin, sin], axis=-1), (1, reps))


def _pack_weights(norm_mix, w_in, q_a_norm, w_uq, kv_a_norm, w_ukv, w_gate_fwd, b_gate_fwd, w_gate_bwd, b_gate_bwd,
                  gla_norm, w_out, norm_ffn, w_router_group, b_router_group, w_router_expert, b_router_expert,
                  w_expert_gate, w_expert_up, w_expert_down, norm_final):
    l = 0
    hk = GLA_HEADS * GLA_DK
    hv = GLA_HEADS * GLA_DV
    c_q, c_kv, k_pe, gq, gk, gv, lr_f, lr_b, og = jnp.split(
        w_in[l], np.cumsum([MLA_Q_RANK, MLA_KV_RANK, MLA_ROPE, hk, hk, hv, GLA_GATE_RANK, GLA_GATE_RANK])[:].tolist(),
        axis=-1)
    lr_pad = jnp.zeros((D_MODEL, LANES - 2 * GLA_GATE_RANK), F32)
    win = jnp.concatenate([c_q, c_kv, k_pe, k_pe, gq, gk, gv, og, lr_f, lr_b, lr_pad], axis=-1).astype(BF16)
    wuq = w_uq[l].reshape(MLA_Q_RANK, MLA_HEADS, MLA_NOPE + MLA_ROPE)
    wuq = jnp.concatenate([wuq[:, :, :MLA_NOPE].reshape(MLA_Q_RANK, -1), wuq[:, :, MLA_NOPE:].reshape(MLA_Q_RANK, -1)],
                          axis=-1).astype(BF16)
    wgate = jnp.zeros((LANES, 2 * hk), F32)
    wgate = wgate.at[0:GLA_GATE_RANK, 0:hk].set(w_gate_fwd[l])
    wgate = wgate.at[GLA_GATE_RANK:2 * GLA_GATE_RANK, hk:].set(w_gate_bwd[l])
    wr = jnp.zeros((D_MODEL, LANES), F32)
    wr = wr.at[:, ROUTER_GROUP_LANE0:ROUTER_GROUP_LANE0 + N_GROUPS].set(w_router_group[l])
    wr = wr.at[:, ROUTER_EXPERT_LANE0:ROUTER_EXPERT_LANE0 + N_EXPERTS].set(w_router_expert[l])
    wr_hi = wr.astype(BF16)
    br = jnp.zeros((1, LANES), F32)
    br = br.at[0, ROUTER_GROUP_LANE0:ROUTER_GROUP_LANE0 + N_GROUPS].set(b_router_group[l])
    br = br.at[0, ROUTER_EXPERT_LANE0:ROUTER_EXPERT_LANE0 + N_EXPERTS].set(b_router_expert[l])
    return {
        'norm_mix': norm_mix[l][None], 'win': win, 'q_a_norm': q_a_norm[l][None], 'wuq': wuq,
        'kv_a_norm': kv_a_norm[l][None], 'wukv': w_ukv[l].astype(BF16),
        'wgate': wgate.astype(BF16), 'bgate': jnp.concatenate([b_gate_fwd[l], b_gate_bwd[l]])[None],
        'gla_norm': gla_norm[l][None], 'wout': w_out[l].astype(BF16), 'norm_ffn': norm_ffn[l][None],
        'wr': jnp.concatenate([wr_hi, (wr - wr_hi.astype(F32)).astype(BF16)], axis=-1), 'br': br,
        'w_gate': w_expert_gate[l].reshape(N_EXPERTS, D_MODEL, D_EXPERT).astype(BF16),
        'w_up': w_expert_up[l].reshape(N_EXPERTS, D_MODEL, D_EXPERT).astype(BF16),
        'w_down': w_expert_down[l].reshape(N_EXPERTS, D_EXPERT, D_MODEL).astype(BF16),
        'norm_final': norm_final[None],
    }


def _meta_streams(meta_tokens, w):
    cos, sin = _rope_tables(jnp.arange(N_META))
    _, k, v, _, gk, gv, gf, _, _ = _inproj(meta_tokens, cos, sin, w, N_META)
    pad_keys = ((0, 0), (0, LANES - N_META), (0, 0))
    front = ((GLA_CHUNK - N_META, 0), (0, 0))
    return (jnp.pad(k, pad_keys), jnp.pad(v, pad_keys), jnp.pad(gk, front), jnp.pad(gv, front), jnp.pad(gf, front))


def _token_mixers(x, meta, w, tm, tq, tk, tb, tmix):
    bsz, seq, _ = x.shape
    km, vm, mk, mv, mg = meta
    x2d = x.reshape(bsz * seq, D_MODEL)
    cos, sin = _rope_tables(N_META + jnp.arange(seq))
    q, k, v, gq, gk, gv, gf, gb, og = _inproj(x2d, cos, sin, w, tm)
    a = _attention(q, k, v, km, vm, bsz, seq, tq, tk)
    o_f, o_b = _gla(gq, gk, gv, gf, gb, mk, mv, mg, bsz, seq, tb)
    return _mix(x2d, a, o_f, o_b, og, w, tmix)


def kernel(x_prompt, x_sample, meta_tokens, norm_mix, w_in, q_a_norm, w_uq, kv_a_norm, w_ukv, w_gate_fwd, b_gate_fwd, w_gate_bwd, b_gate_bwd, gla_norm, w_out, norm_ffn, w_router_group, b_router_group, w_router_expert, b_router_expert, w_expert_gate, w_expert_up, w_expert_down, norm_final):
    w = _pack_weights(norm_mix, w_in, q_a_norm, w_uq, kv_a_norm, w_ukv, w_gate_fwd, b_gate_fwd, w_gate_bwd,
                      b_gate_bwd, gla_norm, w_out, norm_ffn, w_router_group, b_router_group, w_router_expert,
                      b_router_expert, w_expert_gate, w_expert_up, w_expert_down, norm_final)
    meta = _meta_streams(meta_tokens, w)
    outs = []
    for x in (x_prompt, x_sample):
        h1, hn, route, ids, counts = _token_mixers(x, meta, w, tm=1024, tq=1024, tk=2048, tb=1024, tmix=1024)
        outs.append(_moe(hn, route, ids, counts, h1, w).reshape(x.shape))
    return tuple(outs)
```

```python
import functools

import numpy as np
import jax
import jax.numpy as jnp
from jax import lax
from jax.experimental import pallas as pl
from jax.experimental.pallas import tpu as pltpu
from jax.experimental.pallas import tpu_sc as plsc

F32 = jnp.float32
BF16 = jnp.bfloat16

D_MODEL = 1024
N_META = 16
MLA_HEADS = 4
MLA_Q_RANK = 384
MLA_KV_RANK = 256
MLA_NOPE = 128
MLA_ROPE = 64
MLA_V = 128
ROPE_THETA = 10000.0
GLA_HEADS = 4
GLA_DK = 64
GLA_DV = 128
GLA_GATE_RANK = 16
GLA_TAU = 16.0
GLA_CHUNK = 64
N_GROUPS = 4
EXPERTS_PER_GROUP = 8
N_EXPERTS = N_GROUPS * EXPERTS_PER_GROUP
D_EXPERT = 256
EPS = 1e-6

LANES = 128
V7X_VMEM_BYTES = 64 * 1024 * 1024
VMEM_LIMIT = V7X_VMEM_BYTES * 7 // 8

ATTN_SCALE = (MLA_NOPE + MLA_ROPE) ** -0.5 * float(np.log2(np.e))
QK_WIDTH = 2 * LANES
V_WIDTH = 2 * LANES
ATTN_GROUP = 8

C_CQ = 0
C_CKV = C_CQ + MLA_Q_RANK
C_KPE = C_CKV + MLA_KV_RANK
C_GQ = C_KPE + LANES
C_GK = C_GQ + GLA_HEADS * GLA_DK
C_GV = C_GK + GLA_HEADS * GLA_DK
C_OG = C_GV + GLA_HEADS * GLA_DV
C_LR = C_OG + GLA_HEADS * GLA_DV
D_IN_PACKED = C_LR + LANES

ROUTER_GROUP_LANE0 = 0
ROUTER_EXPERT_LANE0 = N_GROUPS


def _cparams(*semantics):
    return pltpu.CompilerParams(dimension_semantics=semantics, vmem_limit_bytes=VMEM_LIMIT)


def _rms(x, g):
    return x * lax.rsqrt(jnp.mean(x * x, axis=-1, keepdims=True) + EPS) * g


def _dot(a, b):
    return jnp.dot(a, b, preferred_element_type=F32)


def _dot_nt(a, b):
    return lax.dot_general(a, b, (((1,), (1,)), ((), ())), preferred_element_type=F32)


def _dot_tn(a, b):
    return lax.dot_general(a, b, (((0,), (0,)), ((), ())), preferred_element_type=F32)


def _full_spec(shape):
    return pl.BlockSpec(shape, lambda *_: (0,) * len(shape))


ROW_CHUNKS = D_MODEL // LANES


def _rows_to_tiles(x):
    chunks = jnp.stack([x[:, s * LANES:(s + 1) * LANES] for s in range(x.shape[1] // LANES)], axis=0)
    return pltpu.einshape("smd->msd", chunks)


def _tiles_to_rows(x):
    chunks = pltpu.einshape("msd->smd", x)
    return jnp.concatenate([chunks[s] for s in range(x.shape[1])], axis=-1)


PACKED_CHUNKS = ROW_CHUNKS // 2


def _pack_bf16_pairs(x):
    half = x.shape[1] // 2
    bits = lambda v: lax.bitcast_convert_type(v.astype(BF16).astype(F32), jnp.uint32)
    return (bits(x[:, half:]) & jnp.uint32(0xFFFF0000)) | (bits(x[:, :half]) >> 16)


def _unpack_bf16_pairs(w):
    lo = lax.bitcast_convert_type(w << 16, F32)
    hi = lax.bitcast_convert_type(w & jnp.uint32(0xFFFF0000), F32)
    return jnp.concatenate([lo, hi], axis=-1)


def _rope_pairs(x, cos, sin_signed, first_half):
    swapped = jnp.where(first_half, pltpu.roll(x, LANES - MLA_ROPE // 2, 1), pltpu.roll(x, MLA_ROPE // 2, 1))
    return x * cos + swapped * sin_signed


def _inproj_body(x_ref, cos_ref, sin_ref, nmix_ref, win_ref, qan_ref, wuq_ref, kvan_ref, wukv_ref,
                 wgate_ref, bgate_ref,
                 q_ref, k_ref, v_ref, gq_ref, gk_ref, gv_ref, gf_ref, gb_ref, og_ref):
    hn = _rms(x_ref[...], nmix_ref[...]).astype(BF16)

    def proj(lo, hi):
        return _dot(hn, win_ref[:, lo:hi])

    cos = cos_ref[...]
    sin = sin_ref[...]
    lane = lax.broadcasted_iota(jnp.int32, cos.shape, 1)
    first_half = (lane & (MLA_ROPE - 1)) < MLA_ROPE // 2
    low_lanes = lane < MLA_ROPE

    cq = _rms(proj(C_CQ, C_CKV), qan_ref[...]).astype(BF16)
    qn = _dot(cq, wuq_ref[:, 0:MLA_HEADS * MLA_NOPE]) * ATTN_SCALE
    qr = _dot(cq, wuq_ref[:, MLA_HEADS * MLA_NOPE:])
    for j in range(MLA_HEADS // 2):
        rj = (_rope_pairs(qr[:, j * LANES:(j + 1) * LANES], cos, sin, first_half) * ATTN_SCALE).astype(BF16)
        for h in (2 * j, 2 * j + 1):
            q_ref[h, :, 0:LANES] = qn[:, h * LANES:(h + 1) * LANES].astype(BF16)
            q_ref[h, :, LANES:QK_WIDTH] = rj

    ckv = _rms(proj(C_CKV, C_KPE), kvan_ref[...]).astype(BF16)
    kv = _dot(ckv, wukv_ref[...])
    kr = _rope_pairs(proj(C_KPE, C_GQ), cos, sin, first_half)
    kr_even = jnp.where(low_lanes, kr, 0.0).astype(BF16)
    kr_odd = jnp.where(low_lanes, 0.0, kr).astype(BF16)
    for h in range(MLA_HEADS):
        base = h * (MLA_NOPE + MLA_V)
        k_ref[h, :, 0:LANES] = kv[:, base:base + MLA_NOPE].astype(BF16)
        k_ref[h, :, LANES:QK_WIDTH] = kr_even if h % 2 == 0 else kr_odd
        v_ref[h, :, 0:MLA_V] = kv[:, base + MLA_NOPE:base + MLA_NOPE + MLA_V].astype(BF16)
        v_ref[h, :, MLA_V:V_WIDTH] = jnp.ones((kv.shape[0], V_WIDTH - MLA_V), BF16)

    gq_ref[...] = (proj(C_GQ, C_GK) * (GLA_DK ** -0.5)).astype(BF16)
    gk_ref[...] = proj(C_GK, C_GV).astype(BF16)
    gv_ref[...] = proj(C_GV, C_OG).astype(BF16)
    og_ref[...] = proj(C_OG, C_LR).astype(BF16)
    pre = _dot(proj(C_LR, D_IN_PACKED).astype(BF16), wgate_ref[...]) + bgate_ref[...]
    logsig = jnp.minimum(pre, 0.0) - jnp.log1p(jnp.exp(-jnp.abs(pre)))
    gates = logsig * (1.0 / GLA_TAU)
    gf_ref[...] = gates[:, 0:GLA_HEADS * GLA_DK]
    gb_ref[...] = gates[:, GLA_HEADS * GLA_DK:]


def _inproj(x2d, cos, sin, w, tm):
    t = x2d.shape[0]
    blocks_per_seq = cos.shape[0] // tm
    hk = GLA_HEADS * GLA_DK
    hv = GLA_HEADS * GLA_DV
    row = lambda width: pl.BlockSpec((tm, width), lambda i: (i, 0))
    head_rows = lambda width: pl.BlockSpec((MLA_HEADS, tm, width), lambda i: (0, i, 0))
    tab = pl.BlockSpec((tm, LANES), lambda i: (i % blocks_per_seq, 0))
    out_shape = (
        jax.ShapeDtypeStruct((MLA_HEADS, t, QK_WIDTH), BF16),
        jax.ShapeDtypeStruct((MLA_HEADS, t, QK_WIDTH), BF16),
        jax.ShapeDtypeStruct((MLA_HEADS, t, V_WIDTH), BF16),
        jax.ShapeDtypeStruct((t, hk), BF16),
        jax.ShapeDtypeStruct((t, hk), BF16),
        jax.ShapeDtypeStruct((t, hv), BF16),
        jax.ShapeDtypeStruct((t, hk), F32),
        jax.ShapeDtypeStruct((t, hk), F32),
        jax.ShapeDtypeStruct((t, hv), BF16),
    )
    return pl.pallas_call(
        _inproj_body,
        grid=(t // tm,),
        in_specs=[row(D_MODEL), tab, tab,
                  _full_spec((1, D_MODEL)), _full_spec((D_MODEL, D_IN_PACKED)),
                  _full_spec((1, MLA_Q_RANK)), _full_spec(w['wuq'].shape),
                  _full_spec((1, MLA_KV_RANK)), _full_spec(w['wukv'].shape),
                  _full_spec(w['wgate'].shape), _full_spec(w['bgate'].shape)],
        out_specs=(head_rows(QK_WIDTH), head_rows(QK_WIDTH), head_rows(V_WIDTH),
                   row(hk), row(hk), row(hv), row(hk), row(hk), row(hv)),
        out_shape=out_shape,
        compiler_params=_cparams("parallel"),
        name="inproj",
    )(x2d, cos, sin, w['norm_mix'], w['win'], w['q_a_norm'], w['wuq'], w['kv_a_norm'], w['wukv'],
      w['wgate'], w['bgate'])


def _attn_body(q_ref, k_ref, v_ref, km_ref, vm_ref, o_ref, s_ref, acc_ref, *, tk):
    q = q_ref[...]
    n_blocks = k_ref.shape[0] // tk
    group_size = min(ATTN_GROUP, n_blocks)
    assert n_blocks % group_size == 0 and group_size % 2 == 0
    n_groups = n_blocks // group_size

    def scores(j):
        return _dot_nt(q, k_ref[pl.ds(pl.multiple_of(j * tk, tk), tk), :])

    def values(j):
        return v_ref[pl.ds(pl.multiple_of(j * tk, tk), tk), :]

    def absorb(m, s, v):
        m_new = jnp.maximum(m, jnp.max(s, axis=-1, keepdims=True))
        p = jnp.exp2(s - m_new)
        acc_ref[...] = jnp.exp2(m - m_new) * acc_ref[...] + _dot(p.astype(BF16), v)
        return m_new

    s_ref[0] = scores(0)
    sm = _dot_nt(q, km_ref[...])
    sm = jnp.where(lax.broadcasted_iota(jnp.int32, sm.shape, 1) < N_META, sm, -jnp.inf)
    m = jnp.max(sm, axis=-1, keepdims=True)
    acc_ref[...] = _dot(jnp.exp2(sm - m).astype(BF16), vm_ref[...])

    def group(g, m, last):
        for i in range(group_size):
            j = group_size * g + i
            s = s_ref[i % 2]
            if not (last and i == group_size - 1):
                s_ref[(i + 1) % 2] = scores(j + 1)
            m = absorb(m, s, values(j))
        return m

    m = lax.fori_loop(0, n_groups - 1, lambda g, m: group(g, m, False), m)
    group(n_groups - 1, m, True)
    acc = acc_ref[...]
    o_ref[...] = (acc[:, :MLA_V] / acc[:, MLA_V:]).astype(o_ref.dtype)


def _attention(q, k, v, km, vm, bsz, seq, tq, tk):
    nq = seq // tq
    return pl.pallas_call(
        functools.partial(_attn_body, tk=tk),
        grid=(bsz, MLA_HEADS, nq),
        in_specs=[pl.BlockSpec((None, tq, QK_WIDTH), lambda b, h, i: (h, b * nq + i, 0)),
                  pl.BlockSpec((None, seq, QK_WIDTH), lambda b, h, i: (h, b, 0)),
                  pl.BlockSpec((None, seq, V_WIDTH), lambda b, h, i: (h, b, 0)),
                  pl.BlockSpec((None, LANES, QK_WIDTH), lambda b, h, i: (h, 0, 0)),
                  pl.BlockSpec((None, LANES, V_WIDTH), lambda b, h, i: (h, 0, 0))],
        out_specs=pl.BlockSpec((tq, MLA_V), lambda b, h, i: (b * nq + i, h)),
        out_shape=jax.ShapeDtypeStruct((bsz * seq, MLA_HEADS * MLA_V), BF16),
        scratch_shapes=[pltpu.VMEM((2, tq, tk), F32), pltpu.VMEM((tq, V_WIDTH), F32)],
        compiler_params=_cparams("parallel", "parallel", "arbitrary"),
        name="mla_attention",
    )(q, k, v, km, vm)


def _split3(x):
    hi = x.astype(BF16)
    r1 = x - hi.astype(F32)
    mid = r1.astype(BF16)
    lo = (r1 - mid.astype(F32)).astype(BF16)
    return hi, mid, lo


def _gla_log_decay(g, tri):
    g_hi, g_mid, g_lo = _split3(g)
    return _dot(tri, g_hi) + _dot(tri, g_mid) + _dot(tri, g_lo)


def _head_rows(x):
    even = lax.broadcasted_iota(jnp.int32, x.shape, 1) < GLA_DK
    return jnp.concatenate([jnp.where(even, x, 0.0), jnp.where(even, 0.0, x)], axis=0).astype(BF16)


def _gla_operands(q, k, b, mid, last):
    b_last = b[last:last + 1, :]
    ke = (k * jnp.exp(b_last - b)).astype(BF16)
    if q is None:
        return jnp.exp(b_last), ke, None, None, None
    b_mid = b[mid:mid + 1, :]
    ks = (k * jnp.exp(b_mid - b)).astype(BF16)
    return jnp.exp(b_last), ke, ks, _head_rows(q * jnp.exp(b - b_mid)), _head_rows(q * jnp.exp(b))


def _gla_state_update(v_even, v_odd, ke):
    return jnp.where(lax.broadcasted_iota(jnp.int32, (GLA_DV, LANES), 1) < GLA_DK,
                     _dot_tn(v_even, ke), _dot_tn(v_odd, ke))


def _gla_intra(v_even, v_odd, ks, qs2, keep):
    scores = jnp.where(keep, _dot_nt(qs2, ks), 0.0).astype(BF16)
    return _dot(scores[:GLA_CHUNK], v_even), _dot(scores[GLA_CHUNK:], v_odd)


def _gla_body(qf_ref, kf_ref, vf_ref, gf_ref, qb_ref, kb_ref, vb_ref, gb_ref, mk_ref, mv_ref, mg_ref,
              of_ref, ob_ref, state_ref):
    n_chunks = qf_ref.shape[0] // GLA_CHUNK
    n_pairs = GLA_HEADS // 2
    r = lax.broadcasted_iota(jnp.int32, (GLA_CHUNK, GLA_CHUNK), 0)
    c = lax.broadcasted_iota(jnp.int32, (GLA_CHUNK, GLA_CHUNK), 1)
    tri_f = jnp.where(c <= r, 1.0, 0.0).astype(BF16)
    tri_b = jnp.where(c >= r, 1.0, 0.0).astype(BF16)
    r2 = lax.broadcasted_iota(jnp.int32, (2 * GLA_CHUNK, GLA_CHUNK), 0) & (GLA_CHUNK - 1)
    c2 = lax.broadcasted_iota(jnp.int32, (2 * GLA_CHUNK, GLA_CHUNK), 1)
    keep_f = c2 <= r2
    keep_b = c2 >= r2
    mid_f, last_f = GLA_CHUNK // 2 - 1, GLA_CHUNK - 1
    mid_b, last_b = GLA_CHUNK // 2, 0

    def pair_cols(p):
        return slice(p * LANES, (p + 1) * LANES)

    def head_cols(h):
        return slice(h * GLA_DV, (h + 1) * GLA_DV)

    @pl.when(pl.program_id(1) == 0)
    def _():
        for p in range(n_pairs):
            b = _gla_log_decay(mg_ref[:, pair_cols(p)], tri_f)
            _, ke, _, _, _ = _gla_operands(None, mk_ref[:, pair_cols(p)], b, mid_f, last_f)
            state_ref[p] = _gla_state_update(mv_ref[:, head_cols(2 * p)].astype(BF16),
                                             mv_ref[:, head_cols(2 * p + 1)].astype(BF16), ke)
            state_ref[n_pairs + p] = jnp.zeros((GLA_DV, LANES), F32)

    scans = []
    for p in range(n_pairs):
        scans.append((p, list(range(n_chunks)), qf_ref, kf_ref, vf_ref, gf_ref, of_ref, p,
                      tri_f, keep_f, mid_f, last_f))
        scans.append((p, list(reversed(range(n_chunks))), qb_ref, kb_ref, vb_ref, gb_ref, ob_ref, n_pairs + p,
                      tri_b, keep_b, mid_b, last_b))

    def rows(c):
        return slice(c * GLA_CHUNK, (c + 1) * GLA_CHUNK)

    def values(v_ref, p, c):
        return (v_ref[rows(c), head_cols(2 * p)].astype(BF16), v_ref[rows(c), head_cols(2 * p + 1)].astype(BF16))

    log_decay = [[_gla_log_decay(g_ref[rows(c), pair_cols(p)], tri) for c in order]
                 for (p, order, _, _, _, g_ref, _, _, tri, _, _, _) in scans]
    operands = [[_gla_operands(q_ref[rows(c), pair_cols(p)], k_ref[rows(c), pair_cols(p)], b, mid, last)
                 for c, b in zip(order, bs)]
                for (p, order, q_ref, k_ref, _, _, _, _, _, _, mid, last), bs in zip(scans, log_decay)]
    updates = [[_gla_state_update(*values(v_ref, p, c), ops[1]) for c, ops in zip(order, opss)]
               for (p, order, _, _, v_ref, _, _, _, _, _, _, _), opss in zip(scans, operands)]
    intra = [[_gla_intra(*values(v_ref, p, c), ops[2], ops[3], keep) for c, ops in zip(order, opss)]
             for (p, order, _, _, v_ref, _, _, _, _, keep, _, _), opss in zip(scans, operands)]
    states = []
    for (_, order, _, _, _, _, _, slot, _, _, _, _), opss, upds in zip(scans, operands, updates):
        st = state_ref[slot]
        entering = []
        for ops, upd in zip(opss, upds):
            entering.append(st.astype(BF16))
            st = st * ops[0] + upd
        state_ref[slot] = st
        states.append(entering)
    for (p, order, _, _, _, _, o_ref, _, _, _, _, _), opss, sts, locs in zip(scans, operands, states, intra):
        for c, ops, st, (o_even, o_odd) in zip(order, opss, sts, locs):
            inter = _dot_nt(ops[4], st)
            o_ref[rows(c), head_cols(2 * p)] = (o_even + inter[:GLA_CHUNK]).astype(o_ref.dtype)
            o_ref[rows(c), head_cols(2 * p + 1)] = (o_odd + inter[GLA_CHUNK:]).astype(o_ref.dtype)


def _gla(gq, gk, gv, gf, gb, mk, mv, mg, bsz, seq, tb):
    nb = seq // tb
    hk = GLA_HEADS * GLA_DK
    hv = GLA_HEADS * GLA_DV
    fwd = lambda width: pl.BlockSpec((tb, width), lambda b, j: (b * nb + j, 0))
    bwd = lambda width: pl.BlockSpec((tb, width), lambda b, j: (b * nb + nb - 1 - j, 0))
    t = bsz * seq
    return pl.pallas_call(
        _gla_body,
        grid=(bsz, nb),
        in_specs=[fwd(hk), fwd(hk), fwd(hv), fwd(hk), bwd(hk), bwd(hk), bwd(hv), bwd(hk),
                  _full_spec(mk.shape), _full_spec(mv.shape), _full_spec(mg.shape)],
        out_specs=(fwd(hv), bwd(hv)),
        out_shape=(jax.ShapeDtypeStruct((t, hv), BF16), jax.ShapeDtypeStruct((t, hv), BF16)),
        scratch_shapes=[pltpu.VMEM((2 * (GLA_HEADS // 2), GLA_DV, LANES), F32)],
        compiler_params=_cparams("parallel", "arbitrary"),
        name="gla_scan",
    )(gq, gk, gv, gf, gq, gk, gv, gb, mk, mv, mg)


def _mix_body(x_ref, a_ref, of_ref, ob_ref, og_ref, gnorm_ref, wout_ref, nffn_ref, wr_ref, br_ref,
              h1_ref, hn_ref, route_ref, ids_ref, count_ref, tri_ref):
    tm = x_ref.shape[0]

    @pl.when(pl.program_id(0) == 0)
    def _():
        r = lax.broadcasted_iota(jnp.int32, (tm, tm), 0)
        c = lax.broadcasted_iota(jnp.int32, (tm, tm), 1)
        tri_ref[...] = jnp.where(c < r, 1.0, 0.0).astype(BF16)
        count_ref[...] = jnp.zeros_like(count_ref)

    a_width = MLA_HEADS * MLA_V
    gated = []
    for h in range(GLA_HEADS):
        cols = slice(h * GLA_DV, (h + 1) * GLA_DV)
        o = of_ref[:, cols].astype(F32) + ob_ref[:, cols].astype(F32)
        og = og_ref[:, cols].astype(F32)
        silu = og / (1.0 + jnp.exp(-og))
        gated.append((_rms(o, gnorm_ref[...]) * silu).astype(BF16))
    h1 = (x_ref[...] + _dot(a_ref[...], wout_ref[0:a_width, :])
          + _dot(jnp.concatenate(gated, axis=-1), wout_ref[a_width:, :]))
    h1_ref[...] = h1
    hn = _rms(h1, nffn_ref[...])
    hn_ref[...] = _rows_to_tiles(_pack_bf16_pairs(hn))

    hn_hi = hn.astype(BF16)
    hn_lo = (hn - hn_hi.astype(F32)).astype(BF16)
    hi_terms = _dot(hn_hi, wr_ref[...])
    logits = hi_terms[:, :LANES] + hi_terms[:, LANES:] + _dot(hn_lo, wr_ref[:, :LANES]) + br_ref[...]
    lane = lax.broadcasted_iota(jnp.int32, logits.shape, 1).astype(F32)
    none = float(LANES)
    neg = -jnp.inf

    def lane_max(x):
        return jnp.max(x, axis=-1, keepdims=True)

    def lane_sum(x):
        return jnp.sum(x, axis=-1, keepdims=True)

    def first_lane(mask):
        return jnp.min(jnp.where(mask, lane, none), axis=-1, keepdims=True)

    is_group = lane < float(N_GROUPS)
    g_max = lane_max(jnp.where(is_group, logits, neg))
    g_exp = jnp.where(is_group, jnp.exp(logits - g_max), 0.0)
    g_prob = g_exp / lane_sum(g_exp)
    g_w = lane_max(g_prob)
    g_idx = first_lane(is_group & (g_prob == g_w))
    e_lo = float(ROUTER_EXPERT_LANE0) + float(EXPERTS_PER_GROUP) * g_idx
    sel = (lane >= e_lo) & (lane < e_lo + float(EXPERTS_PER_GROUP))
    e_max = lane_max(jnp.where(sel, logits, neg))
    e_exp = jnp.where(sel, jnp.exp(logits - e_max), 0.0)
    e_prob = e_exp / lane_sum(e_exp)
    p1 = lane_max(jnp.where(sel, e_prob, neg))
    i1 = first_lane(sel & (e_prob == p1))
    rest = sel & (lane != i1)
    p2 = lane_max(jnp.where(rest, e_prob, neg))
    i2 = first_lane(rest & (e_prob == p2))
    denom = p1 + p2
    chosen = jnp.where((lane == i1) | (lane == i2), 1.0, 0.0)
    rank = count_ref[...] + _dot(tri_ref[...], chosen.astype(BF16))
    count_ref[...] += jnp.sum(chosen, axis=0, keepdims=True)
    fields = (i1 - float(ROUTER_EXPERT_LANE0), i2 - float(ROUTER_EXPERT_LANE0),
              lane_sum(jnp.where(lane == i1, rank, 0.0)), lane_sum(jnp.where(lane == i2, rank, 0.0)),
              g_w * (p1 / denom), g_w * (p2 / denom))
    route = jnp.zeros_like(logits)
    for k, value in enumerate(fields):
        route = jnp.where(lane == float(k), value, route)
    route_ref[...] = route
    ids_ref[...] = jnp.transpose(route)[0:ROUTE_ID_ROWS, :].astype(jnp.int32)


ROUTE_E1, ROUTE_E2, ROUTE_RANK1, ROUTE_RANK2, ROUTE_W1, ROUTE_W2 = range(6)
ROUTE_ID_ROWS = 8


def _mix(x2d, a, o_f, o_b, og, w, tm):
    t = x2d.shape[0]
    hv = GLA_HEADS * GLA_DV
    row = lambda width: pl.BlockSpec((tm, width), lambda i: (i, 0))
    return pl.pallas_call(
        _mix_body,
        grid=(t // tm,),
        in_specs=[row(D_MODEL), row(MLA_HEADS * MLA_V), row(hv), row(hv), row(hv),
                  _full_spec((1, GLA_DV)), _full_spec(w['wout'].shape), _full_spec((1, D_MODEL)),
                  _full_spec(w['wr'].shape), _full_spec(w['br'].shape)],
        out_specs=(row(D_MODEL), pl.BlockSpec((tm, PACKED_CHUNKS, LANES), lambda i: (i, 0, 0)), row(LANES),
                   pl.BlockSpec((ROUTE_ID_ROWS, tm), lambda i: (0, i)), _full_spec((1, LANES))),
        out_shape=(jax.ShapeDtypeStruct((t, D_MODEL), F32), jax.ShapeDtypeStruct((t, PACKED_CHUNKS, LANES), jnp.uint32),
                   jax.ShapeDtypeStruct((t, LANES), F32), jax.ShapeDtypeStruct((ROUTE_ID_ROWS, t), jnp.int32),
                   jax.ShapeDtypeStruct((1, LANES), F32)),
        scratch_shapes=[pltpu.VMEM((tm, tm), BF16)],
        compiler_params=_cparams("arbitrary"),
        name="mix_router",
    )(x2d, a, o_f, o_b, og, w['gla_norm'], w['wout'], w['norm_ffn'], w['wr'], w['br'])


EXPERT_TILE = 256
ROW_TILE = 1024


SC_CORES = 2
SC_SUBCORES = 16
SC_GATHER_ROWS = 128


def _sc_gather(table, idx):
    n = idx.shape[0]
    workers = SC_CORES * SC_SUBCORES
    per_worker = n // workers
    assert n % (workers * SC_GATHER_ROWS) == 0
    mesh = plsc.VectorSubcoreMesh(core_axis_name="c", subcore_axis_name="s")

    @functools.partial(
        pl.kernel, mesh=mesh,
        out_type=jax.ShapeDtypeStruct((n,) + table.shape[1:], table.dtype),
        scratch_types=[pltpu.VMEM((SC_GATHER_ROWS,), jnp.int32),
                       pltpu.VMEM((SC_GATHER_ROWS,) + table.shape[1:], table.dtype),
                       pltpu.SemaphoreType.DMA])
    def gather(table_ref, idx_ref, out_ref, idx_buf, rows_buf, sem):
        base = (lax.axis_index("s") * SC_CORES + lax.axis_index("c")) * per_worker

        @pl.loop(0, per_worker // SC_GATHER_ROWS)
        def _(j):
            rows = pl.ds(base + j * SC_GATHER_ROWS, SC_GATHER_ROWS)
            pltpu.sync_copy(idx_ref.at[rows], idx_buf)
            pltpu.async_copy(table_ref.at[idx_buf], rows_buf, sem).wait()
            pltpu.sync_copy(rows_buf, out_ref.at[rows])

    return gather(table, idx)


def _sc_scatter(rows, idx, n_out):
    copies, n = idx.shape
    idx = idx.reshape(copies * n)
    workers = SC_CORES * SC_SUBCORES
    per_worker = n // workers
    assert n % (workers * SC_GATHER_ROWS) == 0
    mesh = plsc.VectorSubcoreMesh(core_axis_name="c", subcore_axis_name="s")

    @functools.partial(
        pl.kernel, mesh=mesh,
        out_type=jax.ShapeDtypeStruct((n_out,) + rows.shape[1:], rows.dtype),
        scratch_types=[pltpu.VMEM((SC_GATHER_ROWS,), jnp.int32),
                       pltpu.VMEM((SC_GATHER_ROWS,) + rows.shape[1:], rows.dtype),
                       pltpu.SemaphoreType.DMA])
    def scatter(rows_ref, idx_ref, out_ref, idx_buf, rows_buf, sem):
        base = (lax.axis_index("s") * SC_CORES + lax.axis_index("c")) * per_worker

        @pl.loop(0, per_worker // SC_GATHER_ROWS)
        def _(j):
            first = base + j * SC_GATHER_ROWS
            pltpu.sync_copy(rows_ref.at[pl.ds(first, SC_GATHER_ROWS)], rows_buf)
            for k in range(copies):
                pltpu.sync_copy(idx_ref.at[pl.ds(k * n + first, SC_GATHER_ROWS)], idx_buf)
                pltpu.async_copy(rows_buf, out_ref.at[idx_buf], sem).wait()

    return scatter(rows, idx)


SLOT_TILE = 2048


def _slots_body(starts_ref, ids_ref, slots_ref):
    ids = ids_ref[...]
    experts = ids[ROUTE_E1:ROUTE_E1 + 2, :]
    start = jnp.zeros_like(experts)
    for e in range(N_EXPERTS):
        start = jnp.where(experts == e, starts_ref[e], start)
    slots_ref[...] = start + ids[ROUTE_RANK1:ROUTE_RANK1 + 2, :]


def _slots(starts, ids):
    t = ids.shape[1]
    return pl.pallas_call(
        _slots_body,
        grid_spec=pltpu.PrefetchScalarGridSpec(
            num_scalar_prefetch=1,
            grid=(t // SLOT_TILE,),
            in_specs=[pl.BlockSpec((ROUTE_ID_ROWS, SLOT_TILE), lambda i, *_: (0, i))],
            out_specs=pl.BlockSpec((2, SLOT_TILE), lambda i, *_: (0, i))),
        out_shape=jax.ShapeDtypeStruct((2, t), jnp.int32),
        compiler_params=_cparams("parallel"),
        name="moe_slots",
    )(starts, ids)


def _queue_body(slots_ref, token_ref, *, n_tokens):
    i = pl.program_id(0)

    @pl.when(i == 0)
    def _():
        for base in range(0, token_ref.shape[0], n_tokens):
            def clear(j, carry):
                token_ref[base + j] = j
                return carry
            lax.fori_loop(0, min(n_tokens, token_ref.shape[0] - base), clear, 0, unroll=8)

    def place(r, carry):
        for k in range(2):
            token_ref[slots_ref[k, r]] = i * SLOT_TILE + r
        return carry

    lax.fori_loop(0, SLOT_TILE, place, 0, unroll=8)


def _queue_tokens(slots, n_rows):
    t = slots.shape[1]
    return pl.pallas_call(
        functools.partial(_queue_body, n_tokens=t),
        grid=(t // SLOT_TILE,),
        in_specs=[pl.BlockSpec((2, SLOT_TILE), lambda i: (0, i), memory_space=pltpu.SMEM)],
        out_specs=pl.BlockSpec(memory_space=pltpu.SMEM),
        out_shape=jax.ShapeDtypeStruct((n_rows,), jnp.int32),
        compiler_params=_cparams("arbitrary"),
        name="moe_queue",
    )(slots)


TILES_PER_STEP = 4


def _expert_body(tile_expert_ref, n_tiles_ref, tile_valid_ref, xs_ref, *refs):
    wg_refs = refs[0:TILES_PER_STEP]
    wu_refs = refs[TILES_PER_STEP:2 * TILES_PER_STEP]
    wd_refs = refs[2 * TILES_PER_STEP:3 * TILES_PER_STEP]
    ys_ref = refs[3 * TILES_PER_STEP]
    first = pl.program_id(0) * TILES_PER_STEP
    used = first < n_tiles_ref[0]

    @pl.when(used)
    def _():
        rows = [slice(j * EXPERT_TILE, (j + 1) * EXPERT_TILE) for j in range(TILES_PER_STEP)]
        xs = [_unpack_bf16_pairs(_tiles_to_rows(xs_ref[r])) for r in rows]
        xs = [jnp.where(lax.broadcasted_iota(jnp.int32, x.shape, 0) < tile_valid_ref[first + j], x, 0.0)
              for j, x in enumerate(xs)]
        xs = [x.astype(BF16) for x in xs]
        gates = [_dot(x, wg[...]) for x, wg in zip(xs, wg_refs)]
        ups = [_dot(x, wu[...]) for x, wu in zip(xs, wu_refs)]
        acts = [(g / (1.0 + jnp.exp(-g))) * u for g, u in zip(gates, ups)]
        ys = [_dot(act.astype(BF16), wd[...]) for act, wd in zip(acts, wd_refs)]
        for r, y in zip(rows, ys):
            ys_ref[r] = _rows_to_tiles(_pack_bf16_pairs(y))

    @pl.when(jnp.logical_not(used))
    def _():
        ys_ref[...] = jnp.zeros_like(ys_ref)


def _experts(tile_expert, n_tiles, tile_valid, xs, w):
    n_steps = xs.shape[0] // (EXPERT_TILE * TILES_PER_STEP)

    def step(i, nt):
        return jnp.minimum(i, (nt[0] - 1) // TILES_PER_STEP)

    def expert(i, j, te, nt):
        return te[jnp.minimum(step(i, nt) * TILES_PER_STEP + j, nt[0] - 1)]

    tiles = lambda index: pl.BlockSpec((EXPERT_TILE * TILES_PER_STEP, PACKED_CHUNKS, LANES), index)
    up_spec = lambda j: pl.BlockSpec((None, D_MODEL, D_EXPERT), lambda i, te, nt, tv: (expert(i, j, te, nt), 0, 0))
    down_spec = lambda j: pl.BlockSpec((None, D_EXPERT, D_MODEL), lambda i, te, nt, tv: (expert(i, j, te, nt), 0, 0))
    slots = range(TILES_PER_STEP)
    return pl.pallas_call(
        _expert_body,
        grid_spec=pltpu.PrefetchScalarGridSpec(
            num_scalar_prefetch=3,
            grid=(n_steps,),
            in_specs=([tiles(lambda i, te, nt, tv: (step(i, nt), 0, 0))]
                      + [up_spec(j) for j in slots] + [up_spec(j) for j in slots] + [down_spec(j) for j in slots]),
            out_specs=tiles(lambda i, te, nt, tv: (i, 0, 0))),
        out_shape=jax.ShapeDtypeStruct(xs.shape, xs.dtype),
        compiler_params=_cparams("arbitrary"),
        name="moe_experts",
    )(tile_expert, n_tiles, tile_valid, xs, *([w['w_gate']] * TILES_PER_STEP), *([w['w_up']] * TILES_PER_STEP),
      *([w['w_down']] * TILES_PER_STEP))


def _combine_body(h1_ref, route_ref, nfin_ref, y1_ref, y2_ref, out_ref):
    route = route_ref[...]
    lane = lax.broadcasted_iota(jnp.int32, route.shape, 1)
    w1 = jnp.sum(jnp.where(lane == ROUTE_W1, route, 0.0), axis=-1, keepdims=True)
    w2 = jnp.sum(jnp.where(lane == ROUTE_W2, route, 0.0), axis=-1, keepdims=True)
    y1 = _unpack_bf16_pairs(_tiles_to_rows(y1_ref[...]))
    y2 = _unpack_bf16_pairs(_tiles_to_rows(y2_ref[...]))
    y = w1 * y1 + w2 * y2
    out_ref[...] = _rms(h1_ref[...] + y, nfin_ref[...])


def _combine(h1, route, y12, w):
    t = h1.shape[0]
    row = lambda width: pl.BlockSpec((ROW_TILE, width), lambda i: (i, 0))
    tiles = lambda k: pl.BlockSpec((None, ROW_TILE, PACKED_CHUNKS, LANES), lambda i: (k, i, 0, 0))
    return pl.pallas_call(
        _combine_body,
        grid=(t // ROW_TILE,),
        in_specs=[row(D_MODEL), row(LANES), _full_spec((1, D_MODEL)), tiles(0), tiles(1)],
        out_specs=row(D_MODEL),
        out_shape=jax.ShapeDtypeStruct((t, D_MODEL), F32),
        compiler_params=_cparams("parallel"),
        name="moe_combine",
    )(h1, route, w['norm_final'], y12, y12)


def _moe(hn, route, ids, counts, h1, w):
    t = hn.shape[0]
    n_rows = 2 * t + N_EXPERTS * EXPERT_TILE
    n_grid_tiles = n_rows // EXPERT_TILE
    count = counts[0, ROUTER_EXPERT_LANE0:ROUTER_EXPERT_LANE0 + N_EXPERTS].astype(jnp.int32)
    padded = jnp.maximum((count + EXPERT_TILE - 1) // EXPERT_TILE, 1) * EXPERT_TILE
    ends = jnp.sum(jnp.where(jnp.arange(N_EXPERTS)[:, None] <= jnp.arange(N_EXPERTS)[None, :], padded[:, None], 0),
                   axis=0)
    starts = ends - padded
    n_tiles = (ends[-1:] // EXPERT_TILE)
    tile_rows = jnp.arange(n_grid_tiles, dtype=jnp.int32) * EXPERT_TILE
    tile_expert = jnp.minimum(jnp.sum((ends[None, :] <= tile_rows[:, None]).astype(jnp.int32), axis=1), N_EXPERTS - 1)
    tile_valid = jnp.clip(jnp.sum(jnp.where(jnp.arange(N_EXPERTS)[None, :] == tile_expert[:, None],
                                            (starts + count)[None, :], 0), axis=1) - tile_rows, 0, EXPERT_TILE)
    slots = _slots(starts, ids)
    ys = _experts(tile_expert, n_tiles, tile_valid, _sc_scatter(hn, slots, n_rows), w)
    y12 = _sc_gather(ys, slots.reshape(2 * t)).reshape(2, t, PACKED_CHUNKS, LANES)
    return _combine(h1, route, y12, w)


def _rope_tables(positions):
    inv_freq = 1.0 / (ROPE_THETA ** (jnp.arange(0, MLA_ROPE, 2, dtype=F32) / MLA_ROPE))
    ang = positions.astype(F32)[:, None] * inv_freq[None, :]
    cos, sin = jnp.cos(ang), jnp.sin(ang)
    reps = LANES // MLA_ROPE
    return jnp.tile(jnp.concatenate([cos, cos], axis=-1), (1, reps)), jnp.tile(jnp.concatenate([-sin, sin], axis=-1), (1, reps))


def _pack_weights(norm_mix, w_in, q_a_norm, w_uq, kv_a_norm, w_ukv, w_gate_fwd, b_gate_fwd, w_gate_bwd, b_gate_bwd,
                  gla_norm, w_out, norm_ffn, w_router_group, b_router_group, w_router_expert, b_router_expert,
                  w_expert_gate, w_expert_up, w_expert_down, norm_final):
    l = 0
    hk = GLA_HEADS * GLA_DK
    hv = GLA_HEADS * GLA_DV
    c_q, c_kv, k_pe, gq, gk, gv, lr_f, lr_b, og = jnp.split(
        w_in[l], np.cumsum([MLA_Q_RANK, MLA_KV_RANK, MLA_ROPE, hk, hk, hv, GLA_GATE_RANK, GLA_GATE_RANK])[:].tolist(),
        axis=-1)
    lr_pad = jnp.zeros((D_MODEL, LANES - 2 * GLA_GATE_RANK), F32)
    win = jnp.concatenate([c_q, c_kv, k_pe, k_pe, gq, gk, gv, og, lr_f, lr_b, lr_pad], axis=-1).astype(BF16)
    wuq = w_uq[l].reshape(MLA_Q_RANK, MLA_HEADS, MLA_NOPE + MLA_ROPE)
    wuq = jnp.concatenate([wuq[:, :, :MLA_NOPE].reshape(MLA_Q_RANK, -1), wuq[:, :, MLA_NOPE:].reshape(MLA_Q_RANK, -1)],
                          axis=-1).astype(BF16)
    wgate = jnp.zeros((LANES, 2 * hk), F32)
    wgate = wgate.at[0:GLA_GATE_RANK, 0:hk].set(w_gate_fwd[l])
    wgate = wgate.at[GLA_GATE_RANK:2 * GLA_GATE_RANK, hk:].set(w_gate_bwd[l])
    wr = jnp.zeros((D_MODEL, LANES), F32)
    wr = wr.at[:, ROUTER_GROUP_LANE0:ROUTER_GROUP_LANE0 + N_GROUPS].set(w_router_group[l])
    wr = wr.at[:, ROUTER_EXPERT_LANE0:ROUTER_EXPERT_LANE0 + N_EXPERTS].set(w_router_expert[l])
    wr_hi = wr.astype(BF16)
    br = jnp.zeros((1, LANES), F32)
    br = br.at[0, ROUTER_GROUP_LANE0:ROUTER_GROUP_LANE0 + N_GROUPS].set(b_router_group[l])
    br = br.at[0, ROUTER_EXPERT_LANE0:ROUTER_EXPERT_LANE0 + N_EXPERTS].set(b_router_expert[l])
    return {
        'norm_mix': norm_mix[l][None], 'win': win, 'q_a_norm': q_a_norm[l][None], 'wuq': wuq,
        'kv_a_norm': kv_a_norm[l][None], 'wukv': w_ukv[l].astype(BF16),
        'wgate': wgate.astype(BF16), 'bgate': jnp.concatenate([b_gate_fwd[l], b_gate_bwd[l]])[None],
        'gla_norm': gla_norm[l][None], 'wout': w_out[l].astype(BF16), 'norm_ffn': norm_ffn[l][None],
        'wr': jnp.concatenate([wr_hi, (wr - wr_hi.astype(F32)).astype(BF16)], axis=-1), 'br': br,
        'w_gate': w_expert_gate[l].reshape(N_EXPERTS, D_MODEL, D_EXPERT).astype(BF16),
        'w_up': w_expert_up[l].reshape(N_EXPERTS, D_MODEL, D_EXPERT).astype(BF16),
        'w_down': w_expert_down[l].reshape(N_EXPERTS, D_EXPERT, D_MODEL).astype(BF16),
        'norm_final': norm_final[None],
    }


def _meta_streams(meta_tokens, w):
    cos, sin = _rope_tables(jnp.arange(N_META))
    _, k, v, _, gk, gv, gf, _, _ = _inproj(meta_tokens, cos, sin, w, N_META)
    pad_keys = ((0, 0), (0, LANES - N_META), (0, 0))
    front = ((GLA_CHUNK - N_META, 0), (0, 0))
    return (jnp.pad(k, pad_keys), jnp.pad(v, pad_keys), jnp.pad(gk, front), jnp.pad(gv, front), jnp.pad(gf, front))


def _token_mixers(x, meta, w, tm, tq, tk, tb, tmix):
    bsz, seq, _ = x.shape
    km, vm, mk, mv, mg = meta
    x2d = x.reshape(bsz * seq, D_MODEL)
    cos, sin = _rope_tables(N_META + jnp.arange(seq))
    q, k, v, gq, gk, gv, gf, gb, og = _inproj(x2d, cos, sin, w, tm)
    a = _attention(q, k, v, km, vm, bsz, seq, tq, tk)
    o_f, o_b = _gla(gq, gk, gv, gf, gb, mk, mv, mg, bsz, seq, tb)
    return _mix(x2d, a, o_f, o_b, og, w, tmix)


def kernel(x_prompt, x_sample, meta_tokens, norm_mix, w_in, q_a_norm, w_uq, kv_a_norm, w_ukv, w_gate_fwd, b_gate_fwd, w_gate_bwd, b_gate_bwd, gla_norm, w_out, norm_ffn, w_router_group, b_router_group, w_router_expert, b_router_expert, w_expert_gate, w_expert_up, w_expert_down, norm_final):
    w = _pack_weights(norm_mix, w_in, q_a_norm, w_uq, kv_a_norm, w_ukv, w_gate_fwd, b_gate_fwd, w_gate_bwd,
                      b_gate_bwd, gla_norm, w_out, norm_ffn, w_router_group, b_router_group, w_router_expert,
                      b_router_expert, w_expert_gate, w_expert_up, w_expert_down, norm_final)
    meta = _meta_streams(meta_tokens, w)
    outs = []
    for x in (x_prompt, x_sample):
        h1, hn, route, ids, counts = _token_mixers(x, meta, w, tm=1024, tq=1024, tk=2048, tb=1024, tmix=1024)
        outs.append(_moe(hn, route, ids, counts, h1, w).reshape(x.shape))
    return tuple(outs)
```
